```python
import jax, jax.numpy as jnp
from jax import lax
import numpy as np

D_MODEL = 1024
BATCH = 8
SEQ = 2048
DEPTH = 1

D_FF = 4 * D_MODEL
GMLP_WIDTH = D_MODEL
GMLP_HEADS = 8
GMLP_HEAD_DIM = GMLP_WIDTH // GMLP_HEADS
CHUNK = 128
CONV_WIDTH = D_MODEL
CONV_KERNEL = 31
N_MOD = 9
EPS = 1e-6
SPLIT_SIZES = (GMLP_WIDTH, GMLP_WIDTH, CONV_WIDTH, CONV_WIDTH, D_MODEL, D_MODEL)
D_IN = sum(SPLIT_SIZES)

kernel_name = "hybrid_gmlp_conformer_conv_gated_macaron"


def _split_points(sizes):
    pts, acc = [], 0
    for s in sizes[:-1]:
        acc += s
        pts.append(acc)
    return pts


def rms_norm(x, g):
    xf = x.astype(jnp.float32)
    y = xf * lax.rsqrt(jnp.mean(xf * xf, axis=-1, keepdims=True) + EPS)
    return (y * g.astype(jnp.float32)).astype(x.dtype)


def layer_norm(x, g, b):
    xf = x.astype(jnp.float32)
    mu = jnp.mean(xf, axis=-1, keepdims=True)
    var = jnp.mean(jnp.square(xf - mu), axis=-1, keepdims=True)
    y = (xf - mu) * lax.rsqrt(var + EPS)
    return (y * g.astype(jnp.float32) + b.astype(jnp.float32)).astype(x.dtype)


def modulate(h, shift, scale):
    return h * (1.0 + scale[:, None, :]) + shift[:, None, :]


def swiglu(h, w_gate, w_up, w_down):
    return (jax.nn.silu(h @ w_gate) * (h @ w_up)) @ w_down


def token_mixer(h, w_in, b_in, sgu_ln_g, sgu_ln_b, sgu_w_s, sgu_b_s,
                conv_w, conv_b, conv_ln_g, conv_ln_b, w_branch_a, w_branch_b, w_out):
    bsz, seq, _ = h.shape
    proj = h @ w_in + b_in
    u, v, cv, cg, ga, gb = jnp.split(proj, _split_points(SPLIT_SIZES), axis=-1)

    u = jax.nn.gelu(u)
    v = layer_norm(jax.nn.gelu(v), sgu_ln_g, sgu_ln_b)
    n_chunks = seq // CHUNK
    v = v.reshape(bsz, n_chunks, CHUNK, GMLP_HEADS, GMLP_HEAD_DIM)
    causal = jnp.tril(jnp.ones((CHUNK, CHUNK), dtype=bool))
    w_s = jnp.where(causal[None], sgu_w_s, jnp.zeros_like(sgu_w_s))
    v_mix = jnp.einsum('hts,bcshd->bcthd', w_s, v) + sgu_b_s.T[None, None, :, :, None]
    y_a = (u * v_mix.reshape(bsz, seq, GMLP_WIDTH)) @ w_branch_a

    z = cv * jax.nn.sigmoid(cg)
    z = lax.conv_general_dilated(
        z, conv_w[:, None, :], window_strides=(1,),
        padding=[(CONV_KERNEL - 1, 0)],
        dimension_numbers=('NWC', 'WIO', 'NWC'),
        feature_group_count=CONV_WIDTH) + conv_b
    z = jax.nn.silu(layer_norm(z, conv_ln_g, conv_ln_b))
    y_b = z @ w_branch_b

    merged = jax.nn.sigmoid(ga) * y_a + jax.nn.sigmoid(gb) * y_b
    return merged @ w_out


def _fwd_setup_inputs(seed: int = 0) -> dict:
    key = jax.random.key(seed)
    ks = jax.random.split(key, 32)
    f32 = jnp.float32

    def nrm(k, shape, fan_in, mult=1.0):
        return jax.random.normal(k, shape, f32) * (mult * fan_in ** -0.5)

    def gain(k, shape):
        return 1.0 + 0.05 * jax.random.normal(k, shape, f32)

    def bias(k, shape):
        return 0.02 * jax.random.normal(k, shape, f32)

    L = DEPTH
    return {
        "x": jax.random.normal(ks[0], (BATCH, SEQ, D_MODEL), f32),
        "c": jax.random.normal(ks[1], (BATCH, D_MODEL), f32),
        "ada_w": nrm(ks[2], (L, D_MODEL, N_MOD * D_MODEL), D_MODEL, 0.5),
        "ada_b": bias(ks[3], (L, N_MOD * D_MODEL)),
        "norm_ffn1": gain(ks[4], (L, D_MODEL)),
        "ffn1_w_gate": nrm(ks[5], (L, D_MODEL, D_FF), D_MODEL),
        "ffn1_w_up": nrm(ks[6], (L, D_MODEL, D_FF), D_MODEL),
        "ffn1_w_down": nrm(ks[7], (L, D_FF, D_MODEL), D_FF),
        "norm_mix": gain(ks[8], (L, D_MODEL)),
        "mix_w_in": nrm(ks[9], (L, D_MODEL, D_IN), D_MODEL),
        "mix_b_in": bias(ks[10], (L, D_IN)),
        "sgu_ln_g": gain(ks[11], (L, GMLP_WIDTH)),
        "sgu_ln_b": bias(ks[12], (L, GMLP_WIDTH)),
        "sgu_w_s": nrm(ks[13], (L, GMLP_HEADS, CHUNK, CHUNK), CHUNK),
        "sgu_b_s": gain(ks[14], (L, GMLP_HEADS, CHUNK)),
        "conv_w": nrm(ks[15], (L, CONV_KERNEL, CONV_WIDTH), CONV_KERNEL),
        "conv_b": bias(ks[16], (L, CONV_WIDTH)),
        "conv_ln_g": gain(ks[17], (L, CONV_WIDTH)),
        "conv_ln_b": bias(ks[18], (L, CONV_WIDTH)),
        "w_branch_a": nrm(ks[19], (L, GMLP_WIDTH, D_MODEL), GMLP_WIDTH),
        "w_branch_b": nrm(ks[20], (L, CONV_WIDTH, D_MODEL), CONV_WIDTH),
        "w_out": nrm(ks[21], (L, D_MODEL, D_MODEL), D_MODEL),
        "norm_ffn2": gain(ks[22], (L, D_MODEL)),
        "ffn2_w_gate": nrm(ks[23], (L, D_MODEL, D_FF), D_MODEL),
        "ffn2_w_up": nrm(ks[24], (L, D_MODEL, D_FF), D_MODEL),
        "ffn2_w_down": nrm(ks[25], (L, D_FF, D_MODEL), D_FF),
        "norm_final": gain(ks[26], (D_MODEL,)),
    }


def _fwd_reference(x, c, ada_w, ada_b, norm_ffn1, ffn1_w_gate, ffn1_w_up, ffn1_w_down,
              norm_mix, mix_w_in, mix_b_in, sgu_ln_g, sgu_ln_b, sgu_w_s, sgu_b_s,
              conv_w, conv_b, conv_ln_g, conv_ln_b, w_branch_a, w_branch_b, w_out,
              norm_ffn2, ffn2_w_gate, ffn2_w_up, ffn2_w_down, norm_final):
    c_act = jax.nn.silu(c)
    for l in range(DEPTH):
        mod = c_act @ ada_w[l] + ada_b[l]
        sh1, sc1, g1, sh2, sc2, g2, sh3, sc3, g3 = jnp.split(mod, N_MOD, axis=-1)

        h = modulate(rms_norm(x, norm_ffn1[l]), sh1, sc1)
        x = x + 0.5 * g1[:, None, :] * swiglu(h, ffn1_w_gate[l], ffn1_w_up[l], ffn1_w_down[l])

        h = modulate(rms_norm(x, norm_mix[l]), sh2, sc2)
        y = token_mixer(h, mix_w_in[l], mix_b_in[l], sgu_ln_g[l], sgu_ln_b[l],
                        sgu_w_s[l], sgu_b_s[l], conv_w[l], conv_b[l], conv_ln_g[l],
                        conv_ln_b[l], w_branch_a[l], w_branch_b[l], w_out[l])
        x = x + g2[:, None, :] * y

        h = modulate(rms_norm(x, norm_ffn2[l]), sh3, sc3)
        x = x + 0.5 * g3[:, None, :] * swiglu(h, ffn2_w_gate[l], ffn2_w_up[l], ffn2_w_down[l])

    return rms_norm(x, norm_final)


import jax as _jax
import jax.numpy as _jnp

TWIN_FORMAT = 'train_step'
FWD_PARAMS = ['x', 'c', 'ada_w', 'ada_b', 'norm_ffn1', 'ffn1_w_gate', 'ffn1_w_up', 'ffn1_w_down', 'norm_mix', 'mix_w_in', 'mix_b_in', 'sgu_ln_g', 'sgu_ln_b', 'sgu_w_s', 'sgu_b_s', 'conv_w', 'conv_b', 'conv_ln_g', 'conv_ln_b', 'w_branch_a', 'w_branch_b', 'w_out', 'norm_ffn2', 'ffn2_w_gate', 'ffn2_w_up', 'ffn2_w_down', 'norm_final']
TWIN_WEIGHTS = ['ada_w', 'ada_b', 'norm_ffn1', 'ffn1_w_gate', 'ffn1_w_up', 'ffn1_w_down', 'norm_mix', 'mix_w_in', 'mix_b_in', 'sgu_ln_g', 'sgu_ln_b', 'sgu_w_s', 'sgu_b_s', 'conv_w', 'conv_b', 'conv_ln_g', 'conv_ln_b', 'w_branch_a', 'w_branch_b', 'w_out', 'norm_ffn2', 'ffn2_w_gate', 'ffn2_w_up', 'ffn2_w_down', 'norm_final']
TWIN_DIFF_INPUT = 'x'
TWIN_INPUTS = ['x', 'c', 'ada_w', 'ada_b', 'norm_ffn1', 'ffn1_w_gate', 'ffn1_w_up', 'ffn1_w_down', 'norm_mix', 'mix_w_in', 'mix_b_in', 'sgu_ln_g', 'sgu_ln_b', 'sgu_w_s', 'sgu_b_s', 'conv_w', 'conv_b', 'conv_ln_g', 'conv_ln_b', 'w_branch_a', 'w_branch_b', 'w_out', 'norm_ffn2', 'ffn2_w_gate', 'ffn2_w_up', 'ffn2_w_down', 'norm_final', 'loss_target', 'm_ada_w', 'm_ada_b', 'm_norm_ffn1', 'm_ffn1_w_gate', 'm_ffn1_w_up', 'm_ffn1_w_down', 'm_norm_mix', 'm_mix_w_in', 'm_mix_b_in', 'm_sgu_ln_g', 'm_sgu_ln_b', 'm_sgu_w_s', 'm_sgu_b_s', 'm_conv_w', 'm_conv_b', 'm_conv_ln_g', 'm_conv_ln_b', 'm_w_branch_a', 'm_w_branch_b', 'm_w_out', 'm_norm_ffn2', 'm_ffn2_w_gate', 'm_ffn2_w_up', 'm_ffn2_w_down', 'm_norm_final', 'v_ada_w', 'v_ada_b', 'v_norm_ffn1', 'v_ffn1_w_gate', 'v_ffn1_w_up', 'v_ffn1_w_down', 'v_norm_mix', 'v_mix_w_in', 'v_mix_b_in', 'v_sgu_ln_g', 'v_sgu_ln_b', 'v_sgu_w_s', 'v_sgu_b_s', 'v_conv_w', 'v_conv_b', 'v_conv_ln_g', 'v_conv_ln_b', 'v_w_branch_a', 'v_w_branch_b', 'v_w_out', 'v_norm_ffn2', 'v_ffn2_w_gate', 'v_ffn2_w_up', 'v_ffn2_w_down', 'v_norm_final']
TWIN_OUTPUTS = ['loss', 'grad_x', 'grad_ada_w', 'grad_ada_b', 'grad_norm_ffn1', 'grad_ffn1_w_gate', 'grad_ffn1_w_up', 'grad_ffn1_w_down', 'grad_norm_mix', 'grad_mix_w_in', 'grad_mix_b_in', 'grad_sgu_ln_g', 'grad_sgu_ln_b', 'grad_sgu_w_s', 'grad_sgu_b_s', 'grad_conv_w', 'grad_conv_b', 'grad_conv_ln_g', 'grad_conv_ln_b', 'grad_w_branch_a', 'grad_w_branch_b', 'grad_w_out', 'grad_norm_ffn2', 'grad_ffn2_w_gate', 'grad_ffn2_w_up', 'grad_ffn2_w_down', 'grad_norm_final', 'delta_ada_w', 'delta_ada_b', 'delta_norm_ffn1', 'delta_ffn1_w_gate', 'delta_ffn1_w_up', 'delta_ffn1_w_down', 'delta_norm_mix', 'delta_mix_w_in', 'delta_mix_b_in', 'delta_sgu_ln_g', 'delta_sgu_ln_b', 'delta_sgu_w_s', 'delta_sgu_b_s', 'delta_conv_w', 'delta_conv_b', 'delta_conv_ln_g', 'delta_conv_ln_b', 'delta_w_branch_a', 'delta_w_branch_b', 'delta_w_out', 'delta_norm_ffn2', 'delta_ffn2_w_gate', 'delta_ffn2_w_up', 'delta_ffn2_w_down', 'delta_norm_final', 'new_m_ada_w', 'new_m_ada_b', 'new_m_norm_ffn1', 'new_m_ffn1_w_gate', 'new_m_ffn1_w_up', 'new_m_ffn1_w_down', 'new_m_norm_mix', 'new_m_mix_w_in', 'new_m_mix_b_in', 'new_m_sgu_ln_g', 'new_m_sgu_ln_b', 'new_m_sgu_w_s', 'new_m_sgu_b_s', 'new_m_conv_w', 'new_m_conv_b', 'new_m_conv_ln_g', 'new_m_conv_ln_b', 'new_m_w_branch_a', 'new_m_w_branch_b', 'new_m_w_out', 'new_m_norm_ffn2', 'new_m_ffn2_w_gate', 'new_m_ffn2_w_up', 'new_m_ffn2_w_down', 'new_m_norm_final', 'new_v_ada_w', 'new_v_ada_b', 'new_v_norm_ffn1', 'new_v_ffn1_w_gate', 'new_v_ffn1_w_up', 'new_v_ffn1_w_down', 'new_v_norm_mix', 'new_v_mix_w_in', 'new_v_mix_b_in', 'new_v_sgu_ln_g', 'new_v_sgu_ln_b', 'new_v_sgu_w_s', 'new_v_sgu_b_s', 'new_v_conv_w', 'new_v_conv_b', 'new_v_conv_ln_g', 'new_v_conv_ln_b', 'new_v_w_branch_a', 'new_v_w_branch_b', 'new_v_w_out', 'new_v_norm_ffn2', 'new_v_ffn2_w_gate', 'new_v_ffn2_w_up', 'new_v_ffn2_w_down', 'new_v_norm_final']
TWIN_LEAF_KINDS = {'loss': 'loss', 'grad_x': 'grad_x', 'grad_ada_w': 'grad_w', 'grad_ada_b': 'grad_w', 'grad_norm_ffn1': 'grad_w', 'grad_ffn1_w_gate': 'grad_w', 'grad_ffn1_w_up': 'grad_w', 'grad_ffn1_w_down': 'grad_w', 'grad_norm_mix': 'grad_w', 'grad_mix_w_in': 'grad_w', 'grad_mix_b_in': 'grad_w', 'grad_sgu_ln_g': 'grad_w', 'grad_sgu_ln_b': 'grad_w', 'grad_sgu_w_s': 'grad_w', 'grad_sgu_b_s': 'grad_w', 'grad_conv_w': 'grad_w', 'grad_conv_b': 'grad_w', 'grad_conv_ln_g': 'grad_w', 'grad_conv_ln_b': 'grad_w', 'grad_w_branch_a': 'grad_w', 'grad_w_branch_b': 'grad_w', 'grad_w_out': 'grad_w', 'grad_norm_ffn2': 'grad_w', 'grad_ffn2_w_gate': 'grad_w', 'grad_ffn2_w_up': 'grad_w', 'grad_ffn2_w_down': 'grad_w', 'grad_norm_final': 'grad_w', 'delta_ada_w': 'delta_w', 'delta_ada_b': 'delta_w', 'delta_norm_ffn1': 'delta_w', 'delta_ffn1_w_gate': 'delta_w', 'delta_ffn1_w_up': 'delta_w', 'delta_ffn1_w_down': 'delta_w', 'delta_norm_mix': 'delta_w', 'delta_mix_w_in': 'delta_w', 'delta_mix_b_in': 'delta_w', 'delta_sgu_ln_g': 'delta_w', 'delta_sgu_ln_b': 'delta_w', 'delta_sgu_w_s': 'delta_w', 'delta_sgu_b_s': 'delta_w', 'delta_conv_w': 'delta_w', 'delta_conv_b': 'delta_w', 'delta_conv_ln_g': 'delta_w', 'delta_conv_ln_b': 'delta_w', 'delta_w_branch_a': 'delta_w', 'delta_w_branch_b': 'delta_w', 'delta_w_out': 'delta_w', 'delta_norm_ffn2': 'delta_w', 'delta_ffn2_w_gate': 'delta_w', 'delta_ffn2_w_up': 'delta_w', 'delta_ffn2_w_down': 'delta_w', 'delta_norm_final': 'delta_w', 'new_m_ada_w': 'new_m', 'new_m_ada_b': 'new_m', 'new_m_norm_ffn1': 'new_m', 'new_m_ffn1_w_gate': 'new_m', 'new_m_ffn1_w_up': 'new_m', 'new_m_ffn1_w_down': 'new_m', 'new_m_norm_mix': 'new_m', 'new_m_mix_w_in': 'new_m', 'new_m_mix_b_in': 'new_m', 'new_m_sgu_ln_g': 'new_m', 'new_m_sgu_ln_b': 'new_m', 'new_m_sgu_w_s': 'new_m', 'new_m_sgu_b_s': 'new_m', 'new_m_conv_w': 'new_m', 'new_m_conv_b': 'new_m', 'new_m_conv_ln_g': 'new_m', 'new_m_conv_ln_b': 'new_m', 'new_m_w_branch_a': 'new_m', 'new_m_w_branch_b': 'new_m', 'new_m_w_out': 'new_m', 'new_m_norm_ffn2': 'new_m', 'new_m_ffn2_w_gate': 'new_m', 'new_m_ffn2_w_up': 'new_m', 'new_m_ffn2_w_down': 'new_m', 'new_m_norm_final': 'new_m', 'new_v_ada_w': 'new_v', 'new_v_ada_b': 'new_v', 'new_v_norm_ffn1': 'new_v', 'new_v_ffn1_w_gate': 'new_v', 'new_v_ffn1_w_up': 'new_v', 'new_v_ffn1_w_down': 'new_v', 'new_v_norm_mix': 'new_v', 'new_v_mix_w_in': 'new_v', 'new_v_mix_b_in': 'new_v', 'new_v_sgu_ln_g': 'new_v', 'new_v_sgu_ln_b': 'new_v', 'new_v_sgu_w_s': 'new_v', 'new_v_sgu_b_s': 'new_v', 'new_v_conv_w': 'new_v', 'new_v_conv_b': 'new_v', 'new_v_conv_ln_g': 'new_v', 'new_v_conv_ln_b': 'new_v', 'new_v_w_branch_a': 'new_v', 'new_v_w_branch_b': 'new_v', 'new_v_w_out': 'new_v', 'new_v_norm_ffn2': 'new_v', 'new_v_ffn2_w_gate': 'new_v', 'new_v_ffn2_w_up': 'new_v', 'new_v_ffn2_w_down': 'new_v', 'new_v_norm_final': 'new_v'}


def _forward(args):
    return _fwd_reference(*[args[k] for k in FWD_PARAMS])


def _output_shape():
    out = _jax.eval_shape(lambda: _forward(_fwd_setup_inputs(0)))
    return out.shape, out.dtype

N_MICROBATCH = 1
ADAM_LR = 0.001
ADAM_B1 = 0.9
ADAM_B2 = 0.999
ADAM_EPS = 1e-08
ADAM_WD = 0.01
ADAM_STEP = 10
PER_EXAMPLE_BATCH_AXIS = {'x': 0, 'c': 0, 'loss_target': 0}
SHARED_INPUTS = []
_WEIGHT_DTYPES = {'ada_w': _jnp.float32, 'ada_b': _jnp.float32, 'norm_ffn1': _jnp.float32, 'ffn1_w_gate': _jnp.float32, 'ffn1_w_up': _jnp.float32, 'ffn1_w_down': _jnp.float32, 'norm_mix': _jnp.float32, 'mix_w_in': _jnp.float32, 'mix_b_in': _jnp.float32, 'sgu_ln_g': _jnp.float32, 'sgu_ln_b': _jnp.float32, 'sgu_w_s': _jnp.float32, 'sgu_b_s': _jnp.float32, 'conv_w': _jnp.float32, 'conv_b': _jnp.float32, 'conv_ln_g': _jnp.float32, 'conv_ln_b': _jnp.float32, 'w_branch_a': _jnp.float32, 'w_branch_b': _jnp.float32, 'w_out': _jnp.float32, 'norm_ffn2': _jnp.float32, 'ffn2_w_gate': _jnp.float32, 'ffn2_w_up': _jnp.float32, 'ffn2_w_down': _jnp.float32, 'norm_final': _jnp.float32}
MOMENT_SCALE = {'ada_w': 2.692214e-02, 'ada_b': 4.606789e-02, 'norm_ffn1': 2.053471e-02, 'ffn1_w_gate': 7.162620e-03, 'ffn1_w_up': 6.954721e-03, 'ffn1_w_down': 1.389184e-02, 'norm_mix': 2.829526e-02, 'mix_w_in': 1.169199e-02, 'mix_b_in': 1.112184e-02, 'sgu_ln_g': 1.069093e-02, 'sgu_ln_b': 1.043972e-02, 'sgu_w_s': 1.082516e-02, 'sgu_b_s': 1.584839e-02, 'conv_w': 1.324449e-02, 'conv_b': 2.510187e-02, 'conv_ln_g': 1.771970e-02, 'conv_ln_b': 1.549924e-02, 'w_branch_a': 1.946254e-02, 'w_branch_b': 1.305737e-02, 'w_out': 2.368781e-02, 'norm_ffn2': 1.888312e-02, 'ffn2_w_gate': 7.075438e-03, 'ffn2_w_up': 6.888516e-03, 'ffn2_w_down': 1.374471e-02, 'norm_final': 1.601761e+01}


def _to_microbatches(a, axis):
    t = _jnp.moveaxis(a, axis, 0)
    t = t.reshape((N_MICROBATCH, t.shape[0] // N_MICROBATCH) + t.shape[1:])
    return _jnp.moveaxis(t, 1, axis + 1)


def setup_inputs(seed: int = 0) -> dict:
    inp = _fwd_setup_inputs(seed)
    key = _jax.random.fold_in(_jax.random.key(seed), 7919)
    shape, _ = _output_shape()
    out = dict(inp)
    out["loss_target"] = _jax.random.normal(_jax.random.fold_in(key, 0), shape, _jnp.float32)
    for i, name in enumerate(TWIN_WEIGHTS):
        w = inp[name].astype(_jnp.float32)
        if MOMENT_SCALE is None:
            s = _jnp.sqrt(_jnp.mean(_jnp.square(w)) + 1e-30)
        else:
            s = MOMENT_SCALE[name]
        km, kv = _jax.random.split(_jax.random.fold_in(key, i + 1))
        out[name] = w
        out["m_" + name] = s * _jax.random.normal(km, w.shape, _jnp.float32)
        out["v_" + name] = (s * s) * _jax.random.uniform(kv, w.shape, _jnp.float32, 0.5, 1.5)
    if N_MICROBATCH > 1:
        for name, axis in PER_EXAMPLE_BATCH_AXIS.items():
            out[name] = _to_microbatches(out[name], axis)
    return {'x': out['x'], 'c': out['c'], 'ada_w': out['ada_w'], 'ada_b': out['ada_b'], 'norm_ffn1': out['norm_ffn1'], 'ffn1_w_gate': out['ffn1_w_gate'], 'ffn1_w_up': out['ffn1_w_up'], 'ffn1_w_down': out['ffn1_w_down'], 'norm_mix': out['norm_mix'], 'mix_w_in': out['mix_w_in'], 'mix_b_in': out['mix_b_in'], 'sgu_ln_g': out['sgu_ln_g'], 'sgu_ln_b': out['sgu_ln_b'], 'sgu_w_s': out['sgu_w_s'], 'sgu_b_s': out['sgu_b_s'], 'conv_w': out['conv_w'], 'conv_b': out['conv_b'], 'conv_ln_g': out['conv_ln_g'], 'conv_ln_b': out['conv_ln_b'], 'w_branch_a': out['w_branch_a'], 'w_branch_b': out['w_branch_b'], 'w_out': out['w_out'], 'norm_ffn2': out['norm_ffn2'], 'ffn2_w_gate': out['ffn2_w_gate'], 'ffn2_w_up': out['ffn2_w_up'], 'ffn2_w_down': out['ffn2_w_down'], 'norm_final': out['norm_final'], 'loss_target': out['loss_target'], 'm_ada_w': out['m_ada_w'], 'm_ada_b': out['m_ada_b'], 'm_norm_ffn1': out['m_norm_ffn1'], 'm_ffn1_w_gate': out['m_ffn1_w_gate'], 'm_ffn1_w_up': out['m_ffn1_w_up'], 'm_ffn1_w_down': out['m_ffn1_w_down'], 'm_norm_mix': out['m_norm_mix'], 'm_mix_w_in': out['m_mix_w_in'], 'm_mix_b_in': out['m_mix_b_in'], 'm_sgu_ln_g': out['m_sgu_ln_g'], 'm_sgu_ln_b': out['m_sgu_ln_b'], 'm_sgu_w_s': out['m_sgu_w_s'], 'm_sgu_b_s': out['m_sgu_b_s'], 'm_conv_w': out['m_conv_w'], 'm_conv_b': out['m_conv_b'], 'm_conv_ln_g': out['m_conv_ln_g'], 'm_conv_ln_b': out['m_conv_ln_b'], 'm_w_branch_a': out['m_w_branch_a'], 'm_w_branch_b': out['m_w_branch_b'], 'm_w_out': out['m_w_out'], 'm_norm_ffn2': out['m_norm_ffn2'], 'm_ffn2_w_gate': out['m_ffn2_w_gate'], 'm_ffn2_w_up': out['m_ffn2_w_up'], 'm_ffn2_w_down': out['m_ffn2_w_down'], 'm_norm_final': out['m_norm_final'], 'v_ada_w': out['v_ada_w'], 'v_ada_b': out['v_ada_b'], 'v_norm_ffn1': out['v_norm_ffn1'], 'v_ffn1_w_gate': out['v_ffn1_w_gate'], 'v_ffn1_w_up': out['v_ffn1_w_up'], 'v_ffn1_w_down': out['v_ffn1_w_down'], 'v_norm_mix': out['v_norm_mix'], 'v_mix_w_in': out['v_mix_w_in'], 'v_mix_b_in': out['v_mix_b_in'], 'v_sgu_ln_g': out['v_sgu_ln_g'], 'v_sgu_ln_b': out['v_sgu_ln_b'], 'v_sgu_w_s': out['v_sgu_w_s'], 'v_sgu_b_s': out['v_sgu_b_s'], 'v_conv_w': out['v_conv_w'], 'v_conv_b': out['v_conv_b'], 'v_conv_ln_g': out['v_conv_ln_g'], 'v_conv_ln_b': out['v_conv_ln_b'], 'v_w_branch_a': out['v_w_branch_a'], 'v_w_branch_b': out['v_w_branch_b'], 'v_w_out': out['v_w_out'], 'v_norm_ffn2': out['v_norm_ffn2'], 'v_ffn2_w_gate': out['v_ffn2_w_gate'], 'v_ffn2_w_up': out['v_ffn2_w_up'], 'v_ffn2_w_down': out['v_ffn2_w_down'], 'v_norm_final': out['v_norm_final']}


def _loss(weights, diff, rest, loss_target):
    with _jax.named_scope("forward"):
        args = {**rest, TWIN_DIFF_INPUT: diff, **{k: w.astype(_WEIGHT_DTYPES[k]) for k, w in weights.items()}}
        y = _forward(args)
    with _jax.named_scope("loss_head"):
        err = _jnp.square(y.astype(_jnp.float32) - loss_target)
        return 0.5 * _jnp.sum(_jnp.mean(err, axis=-1)) if err.ndim else 0.5 * err


def _adamw(w, g, m, v):
    m = ADAM_B1 * m + (1.0 - ADAM_B1) * g
    v = ADAM_B2 * v + (1.0 - ADAM_B2) * _jnp.square(g)
    m_hat = m / (1.0 - ADAM_B1 ** ADAM_STEP)
    v_hat = v / (1.0 - ADAM_B2 ** ADAM_STEP)
    delta = -ADAM_LR * (m_hat / (_jnp.sqrt(v_hat) + ADAM_EPS) + ADAM_WD * w)
    return delta, m, v


def reference(x, c, ada_w, ada_b, norm_ffn1, ffn1_w_gate, ffn1_w_up, ffn1_w_down, norm_mix, mix_w_in, mix_b_in, sgu_ln_g, sgu_ln_b, sgu_w_s, sgu_b_s, conv_w, conv_b, conv_ln_g, conv_ln_b, w_branch_a, w_branch_b, w_out, norm_ffn2, ffn2_w_gate, ffn2_w_up, ffn2_w_down, norm_final, loss_target, m_ada_w, m_ada_b, m_norm_ffn1, m_ffn1_w_gate, m_ffn1_w_up, m_ffn1_w_down, m_norm_mix, m_mix_w_in, m_mix_b_in, m_sgu_ln_g, m_sgu_ln_b, m_sgu_w_s, m_sgu_b_s, m_conv_w, m_conv_b, m_conv_ln_g, m_conv_ln_b, m_w_branch_a, m_w_branch_b, m_w_out, m_norm_ffn2, m_ffn2_w_gate, m_ffn2_w_up, m_ffn2_w_down, m_norm_final, v_ada_w, v_ada_b, v_norm_ffn1, v_ffn1_w_gate, v_ffn1_w_up, v_ffn1_w_down, v_norm_mix, v_mix_w_in, v_mix_b_in, v_sgu_ln_g, v_sgu_ln_b, v_sgu_w_s, v_sgu_b_s, v_conv_w, v_conv_b, v_conv_ln_g, v_conv_ln_b, v_w_branch_a, v_w_branch_b, v_w_out, v_norm_ffn2, v_ffn2_w_gate, v_ffn2_w_up, v_ffn2_w_down, v_norm_final):
    given = dict(x=x, c=c, ada_w=ada_w, ada_b=ada_b, norm_ffn1=norm_ffn1, ffn1_w_gate=ffn1_w_gate, ffn1_w_up=ffn1_w_up, ffn1_w_down=ffn1_w_down, norm_mix=norm_mix, mix_w_in=mix_w_in, mix_b_in=mix_b_in, sgu_ln_g=sgu_ln_g, sgu_ln_b=sgu_ln_b, sgu_w_s=sgu_w_s, sgu_b_s=sgu_b_s, conv_w=conv_w, conv_b=conv_b, conv_ln_g=conv_ln_g, conv_ln_b=conv_ln_b, w_branch_a=w_branch_a, w_branch_b=w_branch_b, w_out=w_out, norm_ffn2=norm_ffn2, ffn2_w_gate=ffn2_w_gate, ffn2_w_up=ffn2_w_up, ffn2_w_down=ffn2_w_down, norm_final=norm_final, loss_target=loss_target, m_ada_w=m_ada_w, m_ada_b=m_ada_b, m_norm_ffn1=m_norm_ffn1, m_ffn1_w_gate=m_ffn1_w_gate, m_ffn1_w_up=m_ffn1_w_up, m_ffn1_w_down=m_ffn1_w_down, m_norm_mix=m_norm_mix, m_mix_w_in=m_mix_w_in, m_mix_b_in=m_mix_b_in, m_sgu_ln_g=m_sgu_ln_g, m_sgu_ln_b=m_sgu_ln_b, m_sgu_w_s=m_sgu_w_s, m_sgu_b_s=m_sgu_b_s, m_conv_w=m_conv_w, m_conv_b=m_conv_b, m_conv_ln_g=m_conv_ln_g, m_conv_ln_b=m_conv_ln_b, m_w_branch_a=m_w_branch_a, m_w_branch_b=m_w_branch_b, m_w_out=m_w_out, m_norm_ffn2=m_norm_ffn2, m_ffn2_w_gate=m_ffn2_w_gate, m_ffn2_w_up=m_ffn2_w_up, m_ffn2_w_down=m_ffn2_w_down, m_norm_final=m_norm_final, v_ada_w=v_ada_w, v_ada_b=v_ada_b, v_norm_ffn1=v_norm_ffn1, v_ffn1_w_gate=v_ffn1_w_gate, v_ffn1_w_up=v_ffn1_w_up, v_ffn1_w_down=v_ffn1_w_down, v_norm_mix=v_norm_mix, v_mix_w_in=v_mix_w_in, v_mix_b_in=v_mix_b_in, v_sgu_ln_g=v_sgu_ln_g, v_sgu_ln_b=v_sgu_ln_b, v_sgu_w_s=v_sgu_w_s, v_sgu_b_s=v_sgu_b_s, v_conv_w=v_conv_w, v_conv_b=v_conv_b, v_conv_ln_g=v_conv_ln_g, v_conv_ln_b=v_conv_ln_b, v_w_branch_a=v_w_branch_a, v_w_branch_b=v_w_branch_b, v_w_out=v_w_out, v_norm_ffn2=v_norm_ffn2, v_ffn2_w_gate=v_ffn2_w_gate, v_ffn2_w_up=v_ffn2_w_up, v_ffn2_w_down=v_ffn2_w_down, v_norm_final=v_norm_final)
    weights = {n: given[n] for n in TWIN_WEIGHTS}
    shared = {n: given[n] for n in SHARED_INPUTS}
    per_example = {n: given[n] for n in ['x', 'c']}
    grad_fn = _jax.value_and_grad(_loss, argnums=(0, 1))

    def one_microbatch(ex, loss_target):
        ex = dict(ex)
        diff = ex.pop(TWIN_DIFF_INPUT)
        return grad_fn(weights, diff, {**shared, **ex}, loss_target)

    if N_MICROBATCH == 1:
        loss, (grad_w, grad_x) = one_microbatch(per_example, given["loss_target"])
    else:
        def body(carry, xs):
            loss_sum, grad_sum = carry
            l_k, (gw_k, gx_k) = one_microbatch(xs[0], xs[1])
            with _jax.named_scope("update"):
                return (loss_sum + l_k, _jax.tree.map(_jnp.add, grad_sum, gw_k)), gx_k

        init = (_jnp.zeros((), _jnp.float32), _jax.tree.map(_jnp.zeros_like, weights))
        (loss, grad_w), grad_x = _jax.lax.scan(body, init, (per_example, given["loss_target"]))
    with _jax.named_scope("update"):
        delta_w, new_m, new_v = {}, {}, {}
        for n in TWIN_WEIGHTS:
            delta_w[n], new_m[n], new_v[n] = _adamw(weights[n], grad_w[n], given["m_" + n], given["v_" + n])
    return (loss, grad_x, *[grad_w[n] for n in TWIN_WEIGHTS], *[delta_w[n] for n in TWIN_WEIGHTS],
            *[new_m[n] for n in TWIN_WEIGHTS], *[new_v[n] for n in TWIN_WEIGHTS])
```

```python
import functools

import jax
import jax.numpy as jnp
from jax import lax
from jax.experimental import pallas as pl
from jax.experimental.pallas import tpu as pltpu

D = 1024
F = 4 * D
D_IN = 6 * D
HEADS = 8
CHUNK = 128
KW = 31
HALO = 32
N_MOD = 9
NDEV = 8
EPS = 1e-6
LR, B1, B2, ADAM_EPS, WD, STEP = 0.001, 0.9, 0.999, 1e-08, 0.01, 10
BC1 = 1.0 - B1 ** STEP
BC2 = 1.0 - B2 ** STEP
VMEM_LIMIT = 56 * 1024 * 1024
MESH = pl.DeviceIdType.MESH
HBM = pl.BlockSpec(memory_space=pltpu.HBM)
VMEM = pl.BlockSpec(memory_space=pltpu.VMEM)
BF = jnp.bfloat16
F32 = jnp.float32

NN = (((1,), (0,)), ((), ()))
NT = (((1,), (1,)), ((), ()))
TN = (((0,), (0,)), ((), ()))

R_ADA_B, R_NORMS, R_BIN, R_SGU, R_CONV, R_WS, R_CW, R_LOSS, R_TOTAL = 0, 9, 13, 19, 22, 25, 153, 184, 192


def _params(sem):
    return pltpu.CompilerParams(dimension_semantics=sem, vmem_limit_bytes=VMEM_LIMIT)


def _position():
    return lax.axis_index("x"), lax.axis_index("y"), lax.axis_index("c")


def _flip(pos, k):
    x, y, c = pos
    return (x ^ (k >> 2 & 1), y ^ (k >> 1 & 1), c ^ (k & 1))


def _index(pos):
    return 4 * pos[0] + 2 * pos[1] + pos[2]


def _allgather_vmem(name, shard):
    m_per, n = shard.shape

    def body(x_ref, out_ref, send_sems, recv_sems, local_sem):
        x, y, c = _position()
        me, sibling = (x, y, c), (x, y, 1 - c)
        chips = [(1 - x, y), (x, 1 - y), (1 - x, 1 - y)]

        def rows(pos):
            return out_ref.at[pl.ds(_index(pos) * m_per, m_per), :]

        def copy(k, block, to, src=None):
            return pltpu.make_async_remote_copy(
                src_ref=rows(block) if src is None else src, dst_ref=rows(block),
                send_sem=send_sems.at[k], recv_sem=recv_sems.at[k], device_id=to, device_id_type=MESH)

        mine = pltpu.make_async_copy(x_ref, rows(me), local_sem)
        mine.start()
        first = [copy(0, me, sibling, src=x_ref)]
        first += [copy(1 + j, me, (*chip, c), src=x_ref) for j, chip in enumerate(chips)]
        for cp in first:
            cp.start()
        passed = [copy(4 + j, (*chip, c), sibling) for j, chip in enumerate(chips)]
        for j, chip in enumerate(chips):
            copy(1 + j, (*chip, c), me).wait_recv()
            passed[j].start()
        copy(0, sibling, me).wait_recv()
        for j, chip in enumerate(chips):
            copy(4 + j, (*chip, 1 - c), me).wait_recv()
        for cp in first + passed:
            cp.wait_send()
        mine.wait()

    return pl.pallas_call(
        body, name=name,
        out_shape=jax.ShapeDtypeStruct((NDEV * m_per, n), shard.dtype),
        in_specs=[VMEM], out_specs=VMEM,
        scratch_shapes=[pltpu.SemaphoreType.DMA((7,)), pltpu.SemaphoreType.DMA((7,)), pltpu.SemaphoreType.DMA],
    )(shard)


def _allgather_hbm(name, shards):
    n = len(shards)

    def body(*refs):
        ins, outs = refs[:n], refs[n:2 * n]
        send_sems, recv_sems, local_sems = refs[2 * n:]
        x, y, c = _position()
        me, sibling = (x, y, c), (x, y, 1 - c)
        chips = [(1 - x, y), (x, 1 - y), (1 - x, 1 - y)]

        def copy(a, k, block, to, own=False):
            slot = outs[a].at[_index(block)]
            return pltpu.make_async_remote_copy(
                src_ref=ins[a] if own else slot, dst_ref=slot,
                send_sem=send_sems.at[k, a], recv_sem=recv_sems.at[k, a], device_id=to, device_id_type=MESH)

        mine = [pltpu.make_async_copy(ins[a], outs[a].at[_index(me)], local_sems.at[a]) for a in range(n)]
        for cp in mine:
            cp.start()
        first = []
        for a in range(n):
            first.append(copy(a, 0, me, sibling, own=True))
            first += [copy(a, 1 + j, me, (*chip, c), own=True) for j, chip in enumerate(chips)]
        for cp in first:
            cp.start()
        passed = []
        for j, chip in enumerate(chips):
            for a in range(n):
                copy(a, 1 + j, (*chip, c), me).wait_recv()
                fwd = copy(a, 4 + j, (*chip, c), sibling)
                fwd.start()
                passed.append(fwd)
        for a in range(n):
            copy(a, 0, sibling, me).wait_recv()
            for j, chip in enumerate(chips):
                copy(a, 4 + j, (*chip, 1 - c), me).wait_recv()
        for cp in first + passed:
            cp.wait_send()
        for cp in mine:
            cp.wait()

    return pl.pallas_call(
        body, name=name,
        out_shape=[jax.ShapeDtypeStruct((NDEV, *s.shape), s.dtype) for s in shards],
        in_specs=[HBM] * n, out_specs=[HBM] * n,
        scratch_shapes=[pltpu.SemaphoreType.DMA((7, n)), pltpu.SemaphoreType.DMA((7, n)), pltpu.SemaphoreType.DMA((n,))],
    )(*shards)


def _alltoall_hbm(name, parts):
    n = len(parts)

    def body(*refs):
        ins, outs = refs[:n], refs[n:2 * n]
        send_sems, recv_sems, local_sems = refs[2 * n:]
        me = _position()
        mine = [pltpu.make_async_copy(ins[a].at[_index(me)], outs[a].at[_index(me)], local_sems.at[a]) for a in range(n)]
        for cp in mine:
            cp.start()

        def copy(a, k):
            peer = _flip(me, k)
            return pltpu.make_async_remote_copy(
                src_ref=ins[a].at[_index(peer)], dst_ref=outs[a].at[_index(me)],
                send_sem=send_sems.at[k - 1, a], recv_sem=recv_sems.at[k - 1, a], device_id=peer, device_id_type=MESH)

        copies = [copy(a, k) for k in range(1, NDEV) for a in range(n)]
        for cp in copies:
            cp.start()

        def arrival(a, k):
            peer = _flip(me, k)
            return pltpu.make_async_remote_copy(
                src_ref=ins[a].at[_index(peer)], dst_ref=outs[a].at[_index(peer)],
                send_sem=send_sems.at[k - 1, a], recv_sem=recv_sems.at[k - 1, a], device_id=peer, device_id_type=MESH)

        for k in range(1, NDEV):
            for a in range(n):
                arrival(a, k).wait_recv()
        for cp in copies:
            cp.wait_send()
        for cp in mine:
            cp.wait()

    return pl.pallas_call(
        body, name=name,
        out_shape=[jax.ShapeDtypeStruct(p.shape, p.dtype) for p in parts],
        in_specs=[HBM] * n, out_specs=[HBM] * n,
        scratch_shapes=[pltpu.SemaphoreType.DMA((7, n)), pltpu.SemaphoreType.DMA((7, n)), pltpu.SemaphoreType.DMA((n,))],
    )(*parts)


def _mm(name, pairs, dims, grid, nk, out_shapes, out_specs, extras=(), extra_specs=(), epilogue=None, acc_shape=None):
    n_pairs = len(pairs)
    n_in = 2 * n_pairs + len(extras)
    n_out = len(out_shapes)

    def body(*refs):
        ins, outs, scratch = refs[:n_in], refs[n_in:n_in + n_out], refs[n_in + n_out:]

        def partial_sum():
            total = None
            for p in range(n_pairs):
                d = lax.dot_general(ins[2 * p][...], ins[2 * p + 1][...], dims, preferred_element_type=F32)
                total = d if total is None else total + d
            return total

        def finish(r):
            ex = [e[...] for e in ins[2 * n_pairs:]]
            res = epilogue(r, *ex) if epilogue is not None else (r,)
            for o, v in zip(outs, res):
                o[...] = v.astype(o.dtype)

        if nk == 1:
            finish(partial_sum())
        else:
            acc = scratch[0]
            k = pl.program_id(2)

            @pl.when(k == 0)
            def _():
                acc[...] = partial_sum()

            @pl.when(k > 0)
            def _():
                acc[...] += partial_sum()

            @pl.when(k == nk - 1)
            def _():
                finish(acc[...])

    operands, specs = [], []
    for a, a_spec, b, b_spec in pairs:
        operands += [a, b]
        specs += [a_spec, b_spec]
    return pl.pallas_call(
        body, name=name, grid=grid,
        out_shape=out_shapes, in_specs=specs + list(extra_specs), out_specs=out_specs,
        scratch_shapes=[pltpu.VMEM(acc_shape, F32)] if nk > 1 else [],
        compiler_params=_params(("parallel", "parallel", "arbitrary")),
    )(*operands, *extras)


def _silu(x):
    return x * jax.nn.sigmoid(x)


def _ffn_up(name, h, wg, wu):
    t = h.shape[0]
    tm = min(t, 1024)
    nb = F // NDEV

    def body(h_ref, wg_ref, wu_ref, g_ref, u_ref, a_ref):
        hv = h_ref[...]
        g = jnp.dot(hv, wg_ref[...], preferred_element_type=F32)
        u = jnp.dot(hv, wu_ref[...], preferred_element_type=F32)
        g_ref[...] = g.astype(BF)
        u_ref[...] = u.astype(BF)
        a_ref[...] = (_silu(g) * u).astype(BF)

    w_spec = pl.BlockSpec((None, D, nb), lambda i, j: (j, 0, 0))
    o_spec = pl.BlockSpec((tm, nb), lambda i, j: (i, j))
    return pl.pallas_call(
        body, name=name, grid=(t // tm, NDEV),
        out_shape=[jax.ShapeDtypeStruct((t, F), BF)] * 3,
        in_specs=[pl.BlockSpec((tm, D), lambda i, j: (i, 0)), w_spec, w_spec], out_specs=[o_spec] * 3,
        compiler_params=_params(("parallel", "arbitrary")),
    )(h, wg, wu)


def _mm_nn(name, a, b, tm, tn, tk, out_dtype=F32, extras=(), extra_specs=(), epilogue=None, out_dtypes=None):
    m, kk = a.shape
    n = b.shape[1]
    nk = kk // tk
    out_dtypes = out_dtypes or [out_dtype]
    return _mm(
        name, [(a, pl.BlockSpec((tm, tk), lambda i, j, k: (i, k)), b, pl.BlockSpec((tk, tn), lambda i, j, k: (k, j)))], NN,
        (m // tm, n // tn, nk), nk,
        [jax.ShapeDtypeStruct((m, n), dt) for dt in out_dtypes],
        [pl.BlockSpec((tm, tn), lambda i, j, k: (i, j))] * len(out_dtypes),
        extras, extra_specs, epilogue, (tm, tn))


def _mm_nn_blocked(name, a, b3, tm):
    m = a.shape[0]
    nb = b3.shape[2]
    return _mm(
        name, [(a, pl.BlockSpec((tm, D), lambda i, j, k: (i, 0)), b3, pl.BlockSpec((None, D, nb), lambda i, j, k: (j, 0, 0)))], NN,
        (m // tm, NDEV, 1), 1,
        [jax.ShapeDtypeStruct((m, NDEV * nb), F32)], [pl.BlockSpec((tm, nb), lambda i, j, k: (i, j))])[0]


def _mm_nt(name, a, b, tm, tn, out_dtypes=(F32,), extras=(), extra_specs=(), epilogue=None):
    m, kk = a.shape
    n = b.shape[0]
    return _mm(
        name, [(a, pl.BlockSpec((tm, kk), lambda i, j, k: (i, 0)), b, pl.BlockSpec((tn, kk), lambda i, j, k: (j, 0)))], NT,
        (m // tm, n // tn, 1), 1,
        [jax.ShapeDtypeStruct((m, n), dt) for dt in out_dtypes],
        [pl.BlockSpec((tm, tn), lambda i, j, k: (i, j))] * len(out_dtypes),
        extras, extra_specs, epilogue)


def _mm_nt_blocked(name, a_list, b3_list, tm):
    m = a_list[0].shape[0]
    nb = b3_list[0].shape[2]
    pairs = [(a, pl.BlockSpec((tm, nb), lambda i, j, k: (i, k)), b3, pl.BlockSpec((None, D, nb), lambda i, j, k: (k, 0, 0)))
             for a, b3 in zip(a_list, b3_list)]
    return _mm(name, pairs, NT, (m // tm, 1, NDEV), NDEV,
               [jax.ShapeDtypeStruct((m, D), F32)], [pl.BlockSpec((tm, D), lambda i, j, k: (i, 0))], acc_shape=(tm, D))[0]


def _mm_tn(name, a, b, tm, tn):
    t, m = a.shape
    n = b.shape[1]
    return _mm(
        name, [(a, pl.BlockSpec((t, tm), lambda i, j, k: (0, i)), b, pl.BlockSpec((t, tn), lambda i, j, k: (0, j)))], TN,
        (m // tm, n // tn, 1), 1,
        [jax.ShapeDtypeStruct((m, n), BF)], [pl.BlockSpec((tm, tn), lambda i, j, k: (i, j))])[0]


def _mm_tn_blocked(name, a, b):
    t = a.shape[0]
    nb = b.shape[1] // NDEV
    return _mm(
        name, [(a, pl.BlockSpec((t, D), lambda i, j, k: (0, 0)), b, pl.BlockSpec((t, nb), lambda i, j, k: (0, j)))], TN,
        (1, NDEV, 1), 1,
        [jax.ShapeDtypeStruct((NDEV, D, nb), BF)], [pl.BlockSpec((None, D, nb), lambda i, j, k: (j, 0, 0))])[0]


def _rowcall(name, fn, ins, in_specs, n_row_out, out_shapes, out_specs, grid, scratch_shapes=()):
    n_in = len(ins)
    n_out = len(out_shapes)

    def accumulate(o, v, i):
        @pl.when(i == 0)
        def _():
            o[...] = v.astype(o.dtype)

        @pl.when(i > 0)
        def _():
            o[...] += v.astype(o.dtype)

    def body(*refs):
        in_refs, out_refs, scr = refs[:n_in], refs[n_in:n_in + n_out], refs[n_in + n_out:]
        i = pl.program_id(0)
        vals = fn(i, in_refs, scr)
        for idx, (o, v) in enumerate(zip(out_refs, vals)):
            if idx < n_row_out:
                o[...] = v.astype(o.dtype)
            else:
                accumulate(o, v, i)

    return pl.pallas_call(
        body, name=name, grid=(grid,),
        out_shape=out_shapes, in_specs=in_specs, out_specs=out_specs, scratch_shapes=list(scratch_shapes),
        compiler_params=_params(("arbitrary",)),
    )(*ins)


def _rows(tr, w=D, cb=0):
    return pl.BlockSpec((tr, w), lambda i: (i, cb))


def _whole(shape):
    nd = len(shape)
    return pl.BlockSpec(shape, lambda i: (0,) * nd)


def _vec(n=1):
    return jax.ShapeDtypeStruct((n, D), F32)


def _rms_mod(x, gain, sc, sh):
    y = x * lax.rsqrt(jnp.mean(x * x, axis=-1, keepdims=True) + EPS)
    return (y * gain) * (1.0 + sc) + sh


def _layer_norm(x, g, b):
    mu = jnp.mean(x, axis=-1, keepdims=True)
    var = jnp.mean(jnp.square(x - mu), axis=-1, keepdims=True)
    return (x - mu) * lax.rsqrt(var + EPS) * g + b


def _norm_mod(name, x, gain, sc, sh):
    t = x.shape[0]
    tr = min(t, 256)

    def fn(i, r, _):
        return [_rms_mod(r[0][...], r[1][...], r[2][...], r[3][...])]

    return _rowcall(name, fn, [x, gain, sc, sh], [_rows(tr)] + [_whole((1, D))] * 3, 1,
                    [jax.ShapeDtypeStruct((t, D), BF)], [_rows(tr)], t // tr)[0]


def _norm_mod_bwd(name, x, gain, sc, sh, dh, dres):
    t = x.shape[0]
    tr = min(t, 256)

    def fn(i, r, _):
        _, vjp = jax.vjp(_rms_mod, r[0][...], r[1][...], r[2][...], r[3][...])
        dx, dgain, dsc, dsh = vjp(r[4][...])
        return [dx + r[5][...], dgain, dsc, dsh]

    return _rowcall(name, fn, [x, gain, sc, sh, dh, dres], [_rows(tr)] + [_whole((1, D))] * 3 + [_rows(tr)] * 2, 1,
                    [jax.ShapeDtypeStruct((t, D), F32), _vec(), _vec(), _vec()],
                    [_rows(tr)] + [_whole((1, D))] * 3, t // tr)


def _gate_bwd(name, dx, f, g, scale):
    t = dx.shape[0]
    tr = min(t, 256)

    def fn(i, r, _):
        d = r[0][...]
        return [scale * r[2][...] * d, jnp.sum(scale * d * r[1][...].astype(F32), axis=0, keepdims=True)]

    return _rowcall(name, fn, [dx, f, g], [_rows(tr), _rows(tr), _whole((1, D))], 1,
                    [jax.ShapeDtypeStruct((t, D), BF), _vec()], [_rows(tr), _whole((1, D))], t // tr)


def _sgu_pre(up, vp, bu, bv, ln_g, ln_b):
    return jax.nn.gelu(up + bu), _layer_norm(jax.nn.gelu(vp + bv), ln_g, ln_b)


def _causal(w_ref, h):
    rows = lax.broadcasted_iota(jnp.int32, (CHUNK, CHUNK), 0)
    cols = lax.broadcasted_iota(jnp.int32, (CHUNK, CHUNK), 1)
    return jnp.where(cols <= rows, w_ref[h], 0.0)


def _sgu(name, proj, b_in, ln_g, ln_b, w_s, bias_full):
    t = proj.shape[0]

    def fn(i, r, _):
        u, v = _sgu_pre(r[0][...], r[1][...], r[2][...], r[3][...], r[4][...], r[5][...])
        vb = v.astype(BF)
        mixed = [jnp.dot(_causal(r[6], h).astype(BF), vb[:, h * CHUNK:(h + 1) * CHUNK], preferred_element_type=F32)
                 for h in range(HEADS)]
        return [u * (jnp.concatenate(mixed, axis=1) + r[7][...])]

    return _rowcall(
        name, fn, [proj, proj, b_in, b_in, ln_g, ln_b, w_s, bias_full],
        [_rows(CHUNK, D, 0), _rows(CHUNK, D, 1), pl.BlockSpec((1, D), lambda i: (0, 0)), pl.BlockSpec((1, D), lambda i: (0, 1)),
         _whole((1, D)), _whole((1, D)), _whole((HEADS, CHUNK, CHUNK)), _whole((CHUNK, D))],
        1, [jax.ShapeDtypeStruct((t, D), BF)], [_rows(CHUNK)], t // CHUNK)[0]


def _sgu_bwd(name, proj, b_in, ln_g, ln_b, w_s, bias_full, dout):
    t = proj.shape[0]

    def fn(i, r, _):
        (u, v), vjp = jax.vjp(_sgu_pre, r[0][...], r[1][...], r[2][...], r[3][...], r[4][...], r[5][...])
        vb = v.astype(BF)
        d = r[8][...]
        masks = [_causal(r[6], h).astype(BF) for h in range(HEADS)]
        cols = [slice(h * CHUNK, (h + 1) * CHUNK) for h in range(HEADS)]
        mixed = jnp.concatenate([jnp.dot(masks[h], vb[:, cols[h]], preferred_element_type=F32) for h in range(HEADS)], axis=1)
        du = d * (mixed + r[7][...])
        dmix = d * u
        dmb = dmix.astype(BF)
        dv = jnp.concatenate([lax.dot_general(masks[h], dmb[:, cols[h]], TN, preferred_element_type=F32) for h in range(HEADS)], axis=1)
        rows = lax.broadcasted_iota(jnp.int32, (CHUNK, CHUNK), 0)
        lanes = lax.broadcasted_iota(jnp.int32, (CHUNK, CHUNK), 1)
        dws = jnp.stack([jnp.where(lanes <= rows, lax.dot_general(dmb[:, cols[h]], vb[:, cols[h]], NT, preferred_element_type=F32), 0.0)
                         for h in range(HEADS)])
        dbs = jnp.zeros((CHUNK, CHUNK), F32)
        for h in range(HEADS):
            dbs = dbs + jnp.where(lanes == h, jnp.sum(dmix[:, cols[h]], axis=1, keepdims=True), 0.0)
        dup, dvp, dbu, dbv, dg, db = vjp((du, dv))
        return [dup, dvp, dbu, dbv, dg, db, dws, dbs]

    return _rowcall(
        name, fn, [proj, proj, b_in, b_in, ln_g, ln_b, w_s, bias_full, dout],
        [_rows(CHUNK, D, 0), _rows(CHUNK, D, 1), pl.BlockSpec((1, D), lambda i: (0, 0)), pl.BlockSpec((1, D), lambda i: (0, 1)),
         _whole((1, D)), _whole((1, D)), _whole((HEADS, CHUNK, CHUNK)), _whole((CHUNK, D)), _rows(CHUNK)],
        2, [jax.ShapeDtypeStruct((t, D), BF)] * 2 + [_vec()] * 4
        + [jax.ShapeDtypeStruct((HEADS, CHUNK, CHUNK), F32), jax.ShapeDtypeStruct((CHUNK, CHUNK), F32)],
        [_rows(CHUNK)] * 2 + [_whole((1, D))] * 4 + [_whole((HEADS, CHUNK, CHUNK)), _whole((CHUNK, CHUNK))], t // CHUNK)


def _halo_before(tr, cb):
    return pl.BlockSpec((HALO, D), lambda i: (jnp.maximum(i * (tr // HALO) - 1, 0), cb))


def _halo_after(tr, cb, n_tiles):
    return pl.BlockSpec((HALO, D), lambda i: (jnp.minimum((i + 1) * (tr // HALO), n_tiles * (tr // HALO) - 1), cb))


def _ln_silu(z, g, b):
    return _silu(_layer_norm(z, g, b))


def _conv(name, proj, b_in, conv_w, conv_b, ln_g, ln_b):
    t = proj.shape[0]
    tr = min(t, 256)

    def fn(i, r, scr):
        zbuf = scr[0]
        bv, bg = r[4][...], r[5][...]
        z0 = (r[0][...] + bv) * jax.nn.sigmoid(r[1][...] + bg)
        before = (r[2][...] + bv) * jax.nn.sigmoid(r[3][...] + bg)
        zbuf[pl.ds(0, HALO), :] = jnp.where(i > 0, before, 0.0)
        zbuf[pl.ds(HALO, tr), :] = z0
        acc = jnp.zeros((tr, D), F32) + r[7][...]
        for k in range(KW):
            acc = acc + r[6][k:k + 1, :] * zbuf[pl.ds(HALO - (KW - 1) + k, tr), :]
        return [z0, acc, _ln_silu(acc, r[8][...], r[9][...])]

    return _rowcall(
        name, fn, [proj, proj, proj, proj, b_in, b_in, conv_w, conv_b, ln_g, ln_b],
        [_rows(tr, D, 2), _rows(tr, D, 3), _halo_before(tr, 2), _halo_before(tr, 3),
         pl.BlockSpec((1, D), lambda i: (0, 2)), pl.BlockSpec((1, D), lambda i: (0, 3)),
         _whole((HALO, D)), _whole((1, D)), _whole((1, D)), _whole((1, D))],
        3, [jax.ShapeDtypeStruct((t, D), F32), jax.ShapeDtypeStruct((t, D), F32), jax.ShapeDtypeStruct((t, D), BF)],
        [_rows(tr)] * 3, t // tr, [pltpu.VMEM((tr + HALO, D), F32)])


def _conv_bwd(name, proj, b_in, conv_w, ln_g, ln_b, z0, z1, dz3):
    t = proj.shape[0]
    tr = min(t, 256)
    n_tiles = t // tr

    def fn(i, r, scr):
        zbuf, dbuf = scr
        g, b = r[5][...], r[6][...]
        _, vjp = jax.vjp(_ln_silu, r[9][...], g, b)
        dz1, dg, db = vjp(r[11][...])
        _, vjp_after = jax.vjp(_ln_silu, r[10][...], g, b)
        dz1_after = vjp_after(r[12][...])[0]
        dbuf[pl.ds(0, tr), :] = dz1
        dbuf[pl.ds(tr, HALO), :] = jnp.where(i < n_tiles - 1, dz1_after, 0.0)
        zbuf[pl.ds(0, HALO), :] = jnp.where(i > 0, r[8][...], 0.0)
        zbuf[pl.ds(HALO, tr), :] = r[7][...]
        dz0 = jnp.zeros((tr, D), F32)
        dw_rows = []
        for k in range(KW):
            dz0 = dz0 + r[4][k:k + 1, :] * dbuf[pl.ds(KW - 1 - k, tr), :]
            dw_rows.append(jnp.sum(dz1 * zbuf[pl.ds(HALO - (KW - 1) + k, tr), :], axis=0, keepdims=True))
        dw_rows.append(jnp.zeros((HALO - KW, D), F32))
        a = r[0][...] + r[2][...]
        s = jax.nn.sigmoid(r[1][...] + r[3][...])
        dcv = dz0 * s
        dcg = dz0 * a * s * (1.0 - s)
        return [dcv, dcg, jnp.sum(dcv, axis=0, keepdims=True), jnp.sum(dcg, axis=0, keepdims=True),
                jnp.concatenate(dw_rows, axis=0), jnp.sum(dz1, axis=0, keepdims=True), dg, db]

    return _rowcall(
        name, fn, [proj, proj, b_in, b_in, conv_w, ln_g, ln_b, z0, z0, z1, z1, dz3, dz3],
        [_rows(tr, D, 2), _rows(tr, D, 3), pl.BlockSpec((1, D), lambda i: (0, 2)), pl.BlockSpec((1, D), lambda i: (0, 3)),
         _whole((HALO, D)), _whole((1, D)), _whole((1, D)),
         _rows(tr), _halo_before(tr, 0), _rows(tr), _halo_after(tr, 0, n_tiles), _rows(tr), _halo_after(tr, 0, n_tiles)],
        2, [jax.ShapeDtypeStruct((t, D), BF)] * 2 + [_vec(), _vec(), _vec(HALO), _vec(), _vec(), _vec()],
        [_rows(tr)] * 2 + [_whole((1, D))] * 2 + [_whole((HALO, D))] + [_whole((1, D))] * 3, n_tiles,
        [pltpu.VMEM((tr + HALO, D), F32), pltpu.VMEM((tr + HALO, D), F32)])


def _merge_fn(ga, gb, bga, bgb, ya, yb):
    return jax.nn.sigmoid(ga + bga) * ya + jax.nn.sigmoid(gb + bgb) * yb


def _merge(name, proj, b_in, ya, yb):
    t = proj.shape[0]
    tr = min(t, 256)

    def fn(i, r, _):
        return [_merge_fn(*[x[...] for x in r])]

    return _rowcall(
        name, fn, [proj, proj, b_in, b_in, ya, yb],
        [_rows(tr, D, 4), _rows(tr, D, 5), pl.BlockSpec((1, D), lambda i: (0, 4)), pl.BlockSpec((1, D), lambda i: (0, 5)),
         _rows(tr), _rows(tr)],
        1, [jax.ShapeDtypeStruct((t, D), BF)], [_rows(tr)], t // tr)[0]


def _merge_bwd(name, proj, b_in, ya, yb, dm):
    t = proj.shape[0]
    tr = min(t, 256)

    def fn(i, r, _):
        _, vjp = jax.vjp(_merge_fn, *[x[...] for x in r[:6]])
        dga, dgb, dbga, dbgb, dya, dyb = vjp(r[6][...])
        return [dga, dgb, dya, dyb, dbga, dbgb]

    return _rowcall(
        name, fn, [proj, proj, b_in, b_in, ya, yb, dm],
        [_rows(tr, D, 4), _rows(tr, D, 5), pl.BlockSpec((1, D), lambda i: (0, 4)), pl.BlockSpec((1, D), lambda i: (0, 5)),
         _rows(tr), _rows(tr), _rows(tr)],
        4, [jax.ShapeDtypeStruct((t, D), BF)] * 4 + [_vec(), _vec()], [_rows(tr)] * 4 + [_whole((1, D))] * 2, t // tr)


def _loss_head(name, x, gain, target):
    t = x.shape[0]
    tr = min(t, 256)

    def loss_fn(xv, g, tgt):
        y = xv * lax.rsqrt(jnp.mean(xv * xv, axis=-1, keepdims=True) + EPS) * g
        return 0.5 * jnp.sum(jnp.mean(jnp.square(y - tgt), axis=-1))

    def fn(i, r, _):
        loss, vjp = jax.vjp(loss_fn, r[0][...], r[1][...], r[2][...])
        dx, dg, _ = vjp(jnp.ones((), F32))
        return [dx, dg, jnp.zeros((1, D), F32) + loss]

    return _rowcall(name, fn, [x, gain, target], [_rows(tr), _whole((1, D)), _rows(tr)], 1,
                    [jax.ShapeDtypeStruct((t, D), F32), _vec(), _vec()], [_rows(tr), _whole((1, D)), _whole((1, D))], t // tr)


def _adamw(w, g, m, v):
    m = B1 * m + (1.0 - B1) * g
    v = B2 * v + (1.0 - B2) * jnp.square(g)
    m_hat = m / BC1
    v_hat = v / BC2
    delta = -LR * (m_hat / (jnp.sqrt(v_hat) + ADAM_EPS) + WD * w)
    return delta, m, v


def _ada_fwd(name, c_all, ada_w, ada_b):
    nc = ada_w.shape[1]

    def body(c_ref, w_ref, b_ref, o_ref):
        o_ref[...] = jnp.dot(_silu(c_ref[...]), w_ref[...], preferred_element_type=F32) + b_ref[...]

    return pl.pallas_call(body, name=name, out_shape=jax.ShapeDtypeStruct((NDEV, nc), F32),
                          compiler_params=_params(None))(c_all, ada_w, ada_b)


def _adamw_partials(name, parts, w, m, v):
    r, c = w.shape
    tr = min(r, 256)

    def fn(i, refs, _):
        g = refs[0][0].astype(F32)
        for s in range(1, NDEV):
            g = g + refs[0][s].astype(F32)
        delta, m_new, v_new = _adamw(refs[1][...], g, refs[2][...], refs[3][...])
        return [g, delta, m_new, v_new]

    spec = pl.BlockSpec((tr, c), lambda i: (i, 0))
    return _rowcall(name, fn, [parts, w, m, v], [pl.BlockSpec((NDEV, tr, c), lambda i: (0, i, 0)), spec, spec, spec], 4,
                    [jax.ShapeDtypeStruct((r, c), F32)] * 4, [spec] * 4, r // tr)


def _adamw_small(name, gathered, w, m, v):
    r = w.shape[0]

    def body(g_ref, w_ref, m_ref, v_ref, go_ref, d_ref, mo_ref, vo_ref):
        g = g_ref[pl.ds(0, r), :]
        for s in range(1, NDEV):
            g = g + g_ref[pl.ds(s * r, r), :]
        delta, m_new, v_new = _adamw(w_ref[...], g, m_ref[...], v_ref[...])
        go_ref[...] = g
        d_ref[...] = delta
        mo_ref[...] = m_new
        vo_ref[...] = v_new

    return pl.pallas_call(body, name=name, out_shape=[jax.ShapeDtypeStruct((r, D), F32)] * 4,
                          compiler_params=_params(None))(gathered, w, m, v)


def _adamw_ada(name, c_all_t, dmod, w, m, v):
    r, c = w.shape
    tr = 256

    def fn(i, refs, _):
        ca = _silu(refs[0][...])
        g = ca[:, 0:1] * refs[1][0:1, :]
        for b in range(1, NDEV):
            g = g + ca[:, b:b + 1] * refs[1][b:b + 1, :]
        delta, m_new, v_new = _adamw(refs[2][...], g, refs[3][...], refs[4][...])
        return [g, delta, m_new, v_new]

    spec = pl.BlockSpec((tr, c), lambda i: (i, 0))
    return _rowcall(name, fn, [c_all_t, dmod, w, m, v],
                    [pl.BlockSpec((tr, NDEV), lambda i: (i, 0)), pl.BlockSpec((NDEV, c), lambda i: (0, 0)), spec, spec, spec], 4,
                    [jax.ShapeDtypeStruct((r, c), F32)] * 4, [spec] * 4, r // tr)


def _ffn_fwd(tag, x, gain, sh, sc, g, wg, wu, wd):
    h = _norm_mod(f"{tag}_norm", x, gain, sc, sh)
    gate, up, act = _ffn_up(f"{tag}_up", h, wg, wu)

    def epilogue(f, xv, gv):
        return xv + 0.5 * gv * f, f

    x_out, f = _mm_nn(f"{tag}_down", act, wd.reshape(F, D), 512, D, 1024, extras=(x, g),
                      extra_specs=(pl.BlockSpec((512, D), lambda i, j, k: (i, 0)), pl.BlockSpec((1, D), lambda i, j, k: (0, 0))),
                      epilogue=epilogue, out_dtypes=[F32, BF])
    return x_out, (x, h, gate, up, act, f)


def _ffn_bwd(tag, dx_out, saved, gain, sh, sc, g, wg, wu, wd):
    x, h, gate, up, act, f = saved
    t = x.shape[0]
    tm = min(t, 1024)
    df, dg = _gate_bwd(f"{tag}_gate_bwd", dx_out, f, g, 0.5)

    def act_bwd(da, gv, uv):
        gv = gv.astype(F32)
        s = jax.nn.sigmoid(gv)
        return da * uv.astype(F32) * (s * (1.0 + gv * (1.0 - s))), da * (gv * s)

    blk = pl.BlockSpec((tm, F // NDEV), lambda i, j, k: (i, j))
    dgate, dup = _mm_nt(f"{tag}_dact", df, wd.reshape(F, D), tm, F // NDEV, out_dtypes=(BF, BF),
                        extras=(gate, up), extra_specs=(blk, blk), epilogue=act_bwd)
    dwd = _mm_tn(f"{tag}_dwd", act, df, 512, D).reshape(NDEV, F // NDEV, D)
    dh = _mm_nt_blocked(f"{tag}_dh", [dgate, dup], [wg, wu], tm)
    dwg = _mm_tn_blocked(f"{tag}_dwg", h, dgate)
    dwu = _mm_tn_blocked(f"{tag}_dwu", h, dup)
    dx, dgain, dsc, dsh = _norm_mod_bwd(f"{tag}_norm_bwd", x, gain, sc, sh, dh, dx_out)
    return dx, (dwg, dwu, dwd), (dgain, dsc, dsh, dg)


def kernel(x, c, ada_w, ada_b, norm_ffn1, ffn1_w_gate, ffn1_w_up, ffn1_w_down, norm_mix, mix_w_in, mix_b_in, sgu_ln_g, sgu_ln_b, sgu_w_s, sgu_b_s, conv_w, conv_b, conv_ln_g, conv_ln_b, w_branch_a, w_branch_b, w_out, norm_ffn2, ffn2_w_gate, ffn2_w_up, ffn2_w_down, norm_final, loss_target, m_ada_w, m_ada_b, m_norm_ffn1, m_ffn1_w_gate, m_ffn1_w_up, m_ffn1_w_down, m_norm_mix, m_mix_w_in, m_mix_b_in, m_sgu_ln_g, m_sgu_ln_b, m_sgu_w_s, m_sgu_b_s, m_conv_w, m_conv_b, m_conv_ln_g, m_conv_ln_b, m_w_branch_a, m_w_branch_b, m_w_out, m_norm_ffn2, m_ffn2_w_gate, m_ffn2_w_up, m_ffn2_w_down, m_norm_final, v_ada_w, v_ada_b, v_norm_ffn1, v_ffn1_w_gate, v_ffn1_w_up, v_ffn1_w_down, v_norm_mix, v_mix_w_in, v_mix_b_in, v_sgu_ln_g, v_sgu_ln_b, v_sgu_w_s, v_sgu_b_s, v_conv_w, v_conv_b, v_conv_ln_g, v_conv_ln_b, v_w_branch_a, v_w_branch_b, v_w_out, v_norm_ffn2, v_ffn2_w_gate, v_ffn2_w_up, v_ffn2_w_down, v_norm_final):
    me = 4 * lax.axis_index("x") + 2 * lax.axis_index("y") + lax.axis_index("c")
    t = x.shape[1]
    x0 = x.reshape(t, D)
    target = loss_target.reshape(t, D)
    big_names = ["ffn1_w_gate", "ffn1_w_up", "ffn1_w_down", "mix_w_in", "w_branch_a", "w_branch_b", "w_out",
                 "ffn2_w_gate", "ffn2_w_up", "ffn2_w_down"]
    given = dict(ffn1_w_gate=(ffn1_w_gate, m_ffn1_w_gate, v_ffn1_w_gate), ffn1_w_up=(ffn1_w_up, m_ffn1_w_up, v_ffn1_w_up),
                 ffn1_w_down=(ffn1_w_down, m_ffn1_w_down, v_ffn1_w_down), mix_w_in=(mix_w_in, m_mix_w_in, v_mix_w_in),
                 w_branch_a=(w_branch_a, m_w_branch_a, v_w_branch_a), w_branch_b=(w_branch_b, m_w_branch_b, v_w_branch_b),
                 w_out=(w_out, m_w_out, v_w_out), ffn2_w_gate=(ffn2_w_gate, m_ffn2_w_gate, v_ffn2_w_gate),
                 ffn2_w_up=(ffn2_w_up, m_ffn2_w_up, v_ffn2_w_up), ffn2_w_down=(ffn2_w_down, m_ffn2_w_down, v_ffn2_w_down))

    gathered = _allgather_hbm("gather_weights", [given[n][0][0].astype(BF) for n in big_names])
    w = dict(zip(big_names, gathered))
    small_in = jnp.concatenate([c.reshape(8, CHUNK), conv_w[0], jnp.zeros((1, CHUNK), F32)], axis=0)
    small_all = _allgather_vmem("gather_c_conv", small_in).reshape(NDEV, 40, CHUNK)
    c_all = small_all[:, :8, :].reshape(NDEV, D)
    conv_w_full = jnp.transpose(small_all[:, 8:, :], (1, 0, 2)).reshape(HALO, D)
    ada_cols = N_MOD * D // NDEV
    mod_part = _ada_fwd("ada_fwd", c_all, ada_w[0], lax.dynamic_slice(ada_b, (0, me * ada_cols), (1, ada_cols)))
    mod_all = _allgather_vmem("gather_mod", mod_part).reshape(NDEV, NDEV, ada_cols)
    mod = lax.dynamic_index_in_dim(mod_all, me, axis=1, keepdims=False).reshape(N_MOD, 1, D)
    sh1, sc1, g1, sh2, sc2, g2, sh3, sc3, g3 = [mod[i] for i in range(N_MOD)]

    x1, saved1 = _ffn_fwd("ffn1", x0, norm_ffn1, sh1, sc1, g1, w["ffn1_w_gate"], w["ffn1_w_up"], w["ffn1_w_down"])
    h2 = _norm_mod("mix_norm", x1, norm_mix, sc2, sh2)
    proj = _mm_nn_blocked("mix_in", h2, w["mix_w_in"], min(t, 1024))
    bias_full = jnp.repeat(sgu_b_s[0].T, CHUNK, axis=1)
    ua = _sgu("sgu", proj, mix_b_in, sgu_ln_g, sgu_ln_b, sgu_w_s[0], bias_full)
    z0, z1, z3 = _conv("conv", proj, mix_b_in, conv_w_full, conv_b, conv_ln_g, conv_ln_b)
    wa, wb, wo = [w[n].reshape(D, D) for n in ("w_branch_a", "w_branch_b", "w_out")]
    tm = min(t, 1024)
    ya = _mm_nn("branch_a", ua, wa, tm, 512, D)[0]
    yb = _mm_nn("branch_b", z3, wb, tm, 512, D)[0]
    merged = _merge("merge", proj, mix_b_in, ya, yb)

    def mix_epilogue(yv, xv, gv):
        return xv + gv * yv, yv

    x2, y = _mm_nn("mix_out", merged, wo, 512, D, D, extras=(x1, g2),
                   extra_specs=(pl.BlockSpec((512, D), lambda i, j, k: (i, 0)), pl.BlockSpec((1, D), lambda i, j, k: (0, 0))),
                   epilogue=mix_epilogue, out_dtypes=[F32, BF])
    x3, saved3 = _ffn_fwd("ffn2", x2, norm_ffn2, sh3, sc3, g3, w["ffn2_w_gate"], w["ffn2_w_up"], w["ffn2_w_down"])

    norm_final2 = norm_final.reshape(1, D)
    dx3, d_norm_final, loss_row = _loss_head("loss_head", x3, norm_final2, target)
    dx2, dw_ffn2, (d_norm_ffn2, dsc3, dsh3, dg3) = _ffn_bwd(
        "ffn2", dx3, saved3, norm_ffn2, sh3, sc3, g3, w["ffn2_w_gate"], w["ffn2_w_up"], w["ffn2_w_down"])
    dy, dg2 = _gate_bwd("mix_gate_bwd", dx2, y, g2, 1.0)
    dm = _mm_nt("mix_out_bwd", dy, wo, tm, 512)[0]
    dwo = _mm_tn("mix_dwo", merged, dy, 512, D)
    dga, dgb, dya, dyb, db_ga, db_gb = _merge_bwd("merge_bwd", proj, mix_b_in, ya, yb, dm)
    dua = _mm_nt("branch_a_bwd", dya, wa, tm, 512)[0]
    dwa = _mm_tn("branch_dwa", ua, dya, 512, D)
    dz3 = _mm_nt("branch_b_bwd", dyb, wb, tm, 512)[0]
    dwb = _mm_tn("branch_dwb", z3, dyb, 512, D)
    dup, dvp, db_u, db_v, d_sgu_g, d_sgu_b, d_ws, d_bs_t = _sgu_bwd(
        "sgu_bwd", proj, mix_b_in, sgu_ln_g, sgu_ln_b, sgu_w_s[0], bias_full, dua)
    dcv, dcg, db_cv, db_cg, d_cw, d_cb, d_cln_g, d_cln_b = _conv_bwd(
        "conv_bwd", proj, mix_b_in, conv_w_full, conv_ln_g, conv_ln_b, z0, z1, dz3)
    dproj = jnp.concatenate([dup, dvp, dcv, dcg, dga, dgb], axis=1)
    dh2 = _mm_nt_blocked("mix_in_bwd", [dproj], [w["mix_w_in"]], tm)
    dwin = _mm_tn_blocked("mix_dwin", h2, dproj)
    dx1, d_norm_mix, dsc2, dsh2 = _norm_mod_bwd("mix_norm_bwd", x1, norm_mix, sc2, sh2, dh2, dx2)
    dx0, dw_ffn1, (d_norm_ffn1, dsc1, dsh1, dg1) = _ffn_bwd(
        "ffn1", dx1, saved1, norm_ffn1, sh1, sc1, g1, w["ffn1_w_gate"], w["ffn1_w_up"], w["ffn1_w_down"])

    partials = dict(zip(big_names, [*dw_ffn1, dwin, dwa.reshape(NDEV, D // NDEV, D), dwb.reshape(NDEV, D // NDEV, D),
                                    dwo.reshape(NDEV, D // NDEV, D), *dw_ffn2]))
    received = _alltoall_hbm("exchange_grads", [partials[n] for n in big_names])
    big_out = {}
    for n, parts in zip(big_names, received):
        wv, mv, vv = given[n]
        big_out[n] = [o.reshape(wv.shape) for o in _adamw_partials(f"adamw_{n}", parts, wv[0], mv[0], vv[0])]

    d_bs = jnp.transpose(d_bs_t[:, :HEADS])
    pack_rows = [dsh1, dsc1, dg1, dsh2, dsc2, dg2, dsh3, dsc3, dg3,
                 d_norm_ffn1, d_norm_mix, d_norm_ffn2, d_norm_final,
                 db_u, db_v, db_cv, db_cg, db_ga, db_gb,
                 d_sgu_g, d_sgu_b, d_bs.reshape(1, D), d_cb, d_cln_g, d_cln_b,
                 d_ws.reshape(CHUNK, D), d_cw[:KW], loss_row, jnp.zeros((R_TOTAL - R_LOSS - 1, D), F32)]
    packed_all = _allgather_vmem("gather_small_grads", jnp.concatenate(pack_rows, axis=0))

    col0 = me * CHUNK

    def pack(ada_b_, n1, nm, n2, nf, b_in, lg, lb, bs, cb, clg, clb, ws, cw, fill):
        cw_full = lax.dynamic_update_slice(jnp.full((KW, D), fill, F32), cw[0], (0, col0))
        return jnp.concatenate([ada_b_.reshape(N_MOD, D), n1, nm, n2, nf.reshape(1, D), b_in.reshape(6, D), lg, lb, bs.reshape(1, D),
                                cb, clg, clb, ws.reshape(CHUNK, D), cw_full, jnp.full((R_TOTAL - R_LOSS, D), fill, F32)], axis=0)

    w_small = pack(ada_b, norm_ffn1, norm_mix, norm_ffn2, norm_final, mix_b_in, sgu_ln_g, sgu_ln_b, sgu_b_s, conv_b,
                   conv_ln_g, conv_ln_b, sgu_w_s, conv_w, 0.0)
    m_small = pack(m_ada_b, m_norm_ffn1, m_norm_mix, m_norm_ffn2, m_norm_final, m_mix_b_in, m_sgu_ln_g, m_sgu_ln_b, m_sgu_b_s,
                   m_conv_b, m_conv_ln_g, m_conv_ln_b, m_sgu_w_s, m_conv_w, 0.0)
    v_small = pack(v_ada_b, v_norm_ffn1, v_norm_mix, v_norm_ffn2, v_norm_final, v_mix_b_in, v_sgu_ln_g, v_sgu_ln_b, v_sgu_b_s,
                   v_conv_b, v_conv_ln_g, v_conv_ln_b, v_sgu_w_s, v_conv_w, 1.0)
    small_out = _adamw_small("adamw_small", packed_all, w_small, m_small, v_small)

    def unpack(p):
        return dict(
            ada_b=p[R_ADA_B:R_NORMS].reshape(1, N_MOD * D), norm_ffn1=p[9:10], norm_mix=p[10:11], norm_ffn2=p[11:12],
            norm_final=p[12], mix_b_in=p[R_BIN:R_SGU].reshape(1, D_IN), sgu_ln_g=p[19:20], sgu_ln_b=p[20:21],
            sgu_b_s=p[21].reshape(1, HEADS, CHUNK), conv_b=p[22:23], conv_ln_g=p[23:24], conv_ln_b=p[24:25],
            sgu_w_s=p[R_WS:R_CW].reshape(1, HEADS, CHUNK, CHUNK),
            conv_w=lax.dynamic_slice(p[R_CW:R_LOSS], (0, col0), (KW, CHUNK)).reshape(1, KW, CHUNK))

    small = [unpack(p) for p in small_out]
    loss = small_out[0][R_LOSS, 0]

    dmod_all = packed_all.reshape(NDEV, R_TOTAL, D)[:, :N_MOD, :].reshape(NDEV, N_MOD * D)
    dmod_cols = lax.dynamic_slice(dmod_all, (0, me * ada_cols), (NDEV, ada_cols))
    ada_out = [o.reshape(ada_w.shape) for o in _adamw_ada("adamw_ada_w", jnp.transpose(c_all), dmod_cols, ada_w[0], m_ada_w[0], v_ada_w[0])]

    order = ["ada_w", "ada_b", "norm_ffn1", "ffn1_w_gate", "ffn1_w_up", "ffn1_w_down", "norm_mix", "mix_w_in", "mix_b_in",
             "sgu_ln_g", "sgu_ln_b", "sgu_w_s", "sgu_b_s", "conv_w", "conv_b", "conv_ln_g", "conv_ln_b", "w_branch_a",
             "w_branch_b", "w_out", "norm_ffn2", "ffn2_w_gate", "ffn2_w_up", "ffn2_w_down", "norm_final"]

    def leaf(n, kind):
        if n == "ada_w":
            return ada_out[kind]
        if n in big_out:
            return big_out[n][kind]
        return small[kind][n]

    return (loss, dx0.reshape(x.shape), *[leaf(n, kind) for kind in range(4) for n in order])
```

```python
import jax
import jax.numpy as jnp
from jax import lax
from jax.experimental import pallas as pl
from jax.experimental.pallas import tpu as pltpu

D = 1024
F = 4 * D
D_IN = 6 * D
HEADS = 8
CHUNK = 128
KW = 31
HALO = 32
N_MOD = 9
NDEV = 8
N_CHIPS = 4
EPS = 1e-6
LR, B1, B2, ADAM_EPS, WD, STEP = 0.001, 0.9, 0.999, 1e-08, 0.01, 10
BC1 = 1.0 - B1 ** STEP
BC2 = 1.0 - B2 ** STEP
VMEM_LIMIT = 56 * 1024 * 1024
MESH = pl.DeviceIdType.MESH
HBM = pl.BlockSpec(memory_space=pltpu.HBM)
VMEM = pl.BlockSpec(memory_space=pltpu.VMEM)
BF = jnp.bfloat16
F32 = jnp.float32

NN = (((1,), (0,)), ((), ()))
NT = (((1,), (1,)), ((), ()))
TN = (((0,), (0,)), ((), ()))

R_ADA_B, R_NORMS, R_BIN, R_SGU, R_CONV, R_WS, R_CW, R_LOSS, R_TOTAL = 0, 9, 13, 19, 22, 25, 153, 184, 192


def _params(sem):
    return pltpu.CompilerParams(dimension_semantics=sem, vmem_limit_bytes=VMEM_LIMIT)


def _position():
    return lax.axis_index("x"), lax.axis_index("y"), lax.axis_index("c")


def _flip(pos, k):
    x, y, c = pos
    return (x ^ (k >> 2 & 1), y ^ (k >> 1 & 1), c ^ (k & 1))


def _index(pos):
    return 4 * pos[0] + 2 * pos[1] + pos[2]


def _allgather_vmem(name, shard):
    m_per, n = shard.shape

    def body(x_ref, out_ref, send_sems, recv_sems, local_sem):
        x, y, c = _position()
        me, sibling = (x, y, c), (x, y, 1 - c)
        chips = [(1 - x, y), (x, 1 - y), (1 - x, 1 - y)]

        def rows(pos):
            return out_ref.at[pl.ds(_index(pos) * m_per, m_per), :]

        def copy(k, block, to, src=None):
            return pltpu.make_async_remote_copy(
                src_ref=rows(block) if src is None else src, dst_ref=rows(block),
                send_sem=send_sems.at[k], recv_sem=recv_sems.at[k], device_id=to, device_id_type=MESH)

        mine = pltpu.make_async_copy(x_ref, rows(me), local_sem)
        mine.start()
        first = [copy(0, me, sibling, src=x_ref)]
        first += [copy(1 + j, me, (*chip, c), src=x_ref) for j, chip in enumerate(chips)]
        for cp in first:
            cp.start()
        passed = [copy(4 + j, (*chip, c), sibling) for j, chip in enumerate(chips)]
        for j, chip in enumerate(chips):
            copy(1 + j, (*chip, c), me).wait_recv()
            passed[j].start()
        copy(0, sibling, me).wait_recv()
        for j, chip in enumerate(chips):
            copy(4 + j, (*chip, 1 - c), me).wait_recv()
        for cp in first + passed:
            cp.wait_send()
        mine.wait()

    return pl.pallas_call(
        body, name=name,
        out_shape=jax.ShapeDtypeStruct((NDEV * m_per, n), shard.dtype),
        in_specs=[VMEM], out_specs=VMEM,
        scratch_shapes=[pltpu.SemaphoreType.DMA((7,)), pltpu.SemaphoreType.DMA((7,)), pltpu.SemaphoreType.DMA],
    )(shard)


class _Rider:
    def __init__(self, ins, out_shapes, sems, start, finish):
        self.ins, self.out_shapes, self.sems, self.start, self.finish = list(ins), list(out_shapes), list(sems), start, finish


def _gather_rider(shards):
    n = len(shards)

    def setup(ins, outs, sems):
        send_sems, recv_sems, local_sems = sems
        x, y, c = _position()
        me, sibling = (x, y, c), (x, y, 1 - c)
        chips = [(1 - x, y), (x, 1 - y), (1 - x, 1 - y)]

        def copy(a, k, block, to, own=False):
            slot = outs[a].at[_index(block)]
            return pltpu.make_async_remote_copy(
                src_ref=ins[a] if own else slot, dst_ref=slot,
                send_sem=send_sems.at[k, a], recv_sem=recv_sems.at[k, a], device_id=to, device_id_type=MESH)

        mine = [pltpu.make_async_copy(ins[a], outs[a].at[_index(me)], local_sems.at[a]) for a in range(n)]
        first = []
        for a in range(n):
            first.append(copy(a, 0, me, sibling, own=True))
            first += [copy(a, 1 + j, me, (*chip, c), own=True) for j, chip in enumerate(chips)]
        return me, sibling, chips, c, copy, mine, first

    def start(ins, outs, sems):
        *_, mine, first = setup(ins, outs, sems)
        for cp in mine + first:
            cp.start()

    def finish(ins, outs, sems):
        me, sibling, chips, c, copy, mine, first = setup(ins, outs, sems)
        passed = []
        for j, chip in enumerate(chips):
            for a in range(n):
                copy(a, 1 + j, (*chip, c), me).wait_recv()
                fwd = copy(a, 4 + j, (*chip, c), sibling)
                fwd.start()
                passed.append(fwd)
        for a in range(n):
            copy(a, 0, sibling, me).wait_recv()
            for j, chip in enumerate(chips):
                copy(a, 4 + j, (*chip, 1 - c), me).wait_recv()
        for cp in first + passed:
            cp.wait_send()
        for cp in mine:
            cp.wait()

    return _Rider(shards, [jax.ShapeDtypeStruct((NDEV, *s.shape), s.dtype) for s in shards],
                  [pltpu.SemaphoreType.DMA((7, n)), pltpu.SemaphoreType.DMA((7, n)), pltpu.SemaphoreType.DMA((n,))], start, finish)


def _pair_rider(parts):
    n = len(parts)

    def copies(ins, outs, sems):
        send_sems, recv_sems = sems
        x, y, c = _position()
        q = 2 * x + y
        return [pltpu.make_async_remote_copy(
            src_ref=ins[a].at[2 * (q ^ k) + (1 - c)], dst_ref=outs[a].at[k],
            send_sem=send_sems.at[k, a], recv_sem=recv_sems.at[k, a], device_id=(x, y, 1 - c), device_id_type=MESH)
            for a in range(n) for k in range(N_CHIPS)]

    def start(ins, outs, sems):
        for cp in copies(ins, outs, sems):
            cp.start()

    def finish(ins, outs, sems):
        for cp in copies(ins, outs, sems):
            cp.wait()

    return _Rider(parts, [jax.ShapeDtypeStruct((N_CHIPS, *p.shape[1:]), p.dtype) for p in parts],
                  [pltpu.SemaphoreType.DMA((N_CHIPS, n)), pltpu.SemaphoreType.DMA((N_CHIPS, n))], start, finish)


def _chip_rider(sums):
    n = len(sums)

    def copies(ins, outs, sems):
        send_sems, recv_sems = sems
        me = _position()
        return [pltpu.make_async_remote_copy(
            src_ref=ins[a].at[k], dst_ref=outs[a].at[k - 1],
            send_sem=send_sems.at[k - 1, a], recv_sem=recv_sems.at[k - 1, a], device_id=_flip(me, 2 * k), device_id_type=MESH)
            for a in range(n) for k in range(1, N_CHIPS)]

    def start(ins, outs, sems):
        for cp in copies(ins, outs, sems):
            cp.start()

    def finish(ins, outs, sems):
        for cp in copies(ins, outs, sems):
            cp.wait()

    return _Rider(sums, [jax.ShapeDtypeStruct((N_CHIPS - 1, *s.shape[1:]), s.dtype) for s in sums],
                  [pltpu.SemaphoreType.DMA((N_CHIPS - 1, n)), pltpu.SemaphoreType.DMA((N_CHIPS - 1, n))], start, finish)


def _grid_edge(grid, last):
    cond = None
    for d, n in enumerate(grid):
        here = pl.program_id(d) == (n - 1 if last else 0)
        cond = here if cond is None else jnp.logical_and(cond, here)
    return cond


def _call(name, compute, grid, ins, in_specs, out_shapes, out_specs, scratch_shapes, semantics, rider=None):
    r_ins = rider.ins if rider else []
    r_outs = rider.out_shapes if rider else []
    r_sems = rider.sems if rider else []
    n_in, n_rin, n_out, n_rout, n_scr = len(ins), len(r_ins), len(out_shapes), len(r_outs), len(scratch_shapes)
    cuts = [0, n_in, n_in + n_rin, n_in + n_rin + n_out, n_in + n_rin + n_out + n_rout, n_in + n_rin + n_out + n_rout + n_scr]

    def body(*refs):
        in_refs, rin_refs, out_refs, rout_refs, scr_refs = [refs[a:b] for a, b in zip(cuts[:-1], cuts[1:])]
        rsem_refs = refs[cuts[-1]:]
        if rider:
            @pl.when(_grid_edge(grid, last=False))
            def _():
                rider.start(rin_refs, rout_refs, rsem_refs)

        compute(in_refs, out_refs, scr_refs)
        if rider:
            @pl.when(_grid_edge(grid, last=True))
            def _():
                rider.finish(rin_refs, rout_refs, rsem_refs)

    res = pl.pallas_call(
        body, name=name, grid=grid,
        out_shape=list(out_shapes) + list(r_outs), in_specs=list(in_specs) + [HBM] * n_rin,
        out_specs=list(out_specs) + [HBM] * n_rout, scratch_shapes=list(scratch_shapes) + list(r_sems),
        compiler_params=_params(semantics),
    )(*ins, *r_ins)
    return (res[:n_out], res[n_out:]) if rider else res


def _exchange(name, rider):
    return _call(name, lambda *_: None, (1,), [], [], [], [], [], ("arbitrary",), rider)[1]


def _pair_add(name, part, from_sibling, slots):
    _, r, c = part.shape
    tr = min(r, 256)

    def body(s_ref, p_ref, r_ref, o_ref):
        o_ref[...] = (p_ref[...].astype(F32) + r_ref[...].astype(F32)).astype(o_ref.dtype)

    return pl.pallas_call(
        body, name=name,
        grid_spec=pltpu.PrefetchScalarGridSpec(
            num_scalar_prefetch=1, grid=(N_CHIPS, r // tr),
            in_specs=[pl.BlockSpec((None, tr, c), lambda k, i, s: (s[k], i, 0)), pl.BlockSpec((None, tr, c), lambda k, i, s: (k, i, 0))],
            out_specs=pl.BlockSpec((None, tr, c), lambda k, i, s: (k, i, 0))),
        out_shape=jax.ShapeDtypeStruct((N_CHIPS, r, c), part.dtype),
        compiler_params=_params(("arbitrary", "arbitrary")),
    )(slots, part, from_sibling)


def _mm(name, pairs, dims, grid, nk, out_shapes, out_specs, extras=(), extra_specs=(), epilogue=None, acc_shape=None, rider=None):
    n_pairs = len(pairs)

    def compute(ins, outs, scratch):
        def partial_sum():
            total = None
            for p in range(n_pairs):
                d = lax.dot_general(ins[2 * p][...], ins[2 * p + 1][...], dims, preferred_element_type=F32)
                total = d if total is None else total + d
            return total

        def finish(r):
            ex = [e[...] for e in ins[2 * n_pairs:]]
            res = epilogue(r, *ex) if epilogue is not None else (r,)
            for o, v in zip(outs, res):
                o[...] = v.astype(o.dtype)

        if nk == 1:
            finish(partial_sum())
        else:
            acc = scratch[0]
            k = pl.program_id(2)

            @pl.when(k == 0)
            def _():
                acc[...] = partial_sum()

            @pl.when(k > 0)
            def _():
                acc[...] += partial_sum()

            @pl.when(k == nk - 1)
            def _():
                finish(acc[...])

    operands, specs = [], []
    for a, a_spec, b, b_spec in pairs:
        operands += [a, b]
        specs += [a_spec, b_spec]
    return _call(name, compute, grid, operands + list(extras), specs + list(extra_specs), out_shapes, out_specs,
                 [pltpu.VMEM(acc_shape, F32)] if nk > 1 else [], ("parallel", "parallel", "arbitrary"), rider)


def _single(res, rider):
    return (res[0][0], res[1]) if rider else res[0]


def _silu(x):
    return x * jax.nn.sigmoid(x)


def _ffn_up(name, h, wg, wu, rider=None):
    t = h.shape[0]
    tm = min(t, 1024)
    nb = F // NDEV

    def compute(ins, outs, _):
        hv = ins[0][...]
        g = jnp.dot(hv, ins[1][...], preferred_element_type=F32)
        u = jnp.dot(hv, ins[2][...], preferred_element_type=F32)
        outs[0][...] = g.astype(BF)
        outs[1][...] = u.astype(BF)
        outs[2][...] = (_silu(g) * u).astype(BF)

    w_spec = pl.BlockSpec((None, D, nb), lambda i, j: (j, 0, 0))
    o_spec = pl.BlockSpec((tm, nb), lambda i, j: (i, j))
    return _call(name, compute, (t // tm, NDEV), [h, wg, wu], [pl.BlockSpec((tm, D), lambda i, j: (i, 0)), w_spec, w_spec],
                 [jax.ShapeDtypeStruct((t, F), BF)] * 3, [o_spec] * 3, [], ("parallel", "arbitrary"), rider)


def _mm_nn(name, a, b, tm, tn, tk, extras=(), extra_specs=(), epilogue=None, out_dtypes=(F32,), rider=None):
    m, kk = a.shape
    n = b.shape[1]
    nk = kk // tk
    return _mm(
        name, [(a, pl.BlockSpec((tm, tk), lambda i, j, k: (i, k)), b, pl.BlockSpec((tk, tn), lambda i, j, k: (k, j)))], NN,
        (m // tm, n // tn, nk), nk,
        [jax.ShapeDtypeStruct((m, n), dt) for dt in out_dtypes],
        [pl.BlockSpec((tm, tn), lambda i, j, k: (i, j))] * len(out_dtypes),
        extras, extra_specs, epilogue, (tm, tn), rider)


def _mm_nn_blocked(name, a, b3, tm, rider=None):
    m = a.shape[0]
    nb = b3.shape[2]
    return _single(_mm(
        name, [(a, pl.BlockSpec((tm, D), lambda i, j, k: (i, 0)), b3, pl.BlockSpec((None, D, nb), lambda i, j, k: (j, 0, 0)))], NN,
        (m // tm, NDEV, 1), 1,
        [jax.ShapeDtypeStruct((m, NDEV * nb), F32)], [pl.BlockSpec((tm, nb), lambda i, j, k: (i, j))], rider=rider), rider)


def _mm_nt(name, a, b, tm, tn, out_dtypes=(F32,), extras=(), extra_specs=(), epilogue=None, rider=None):
    m, kk = a.shape
    n = b.shape[0]
    return _mm(
        name, [(a, pl.BlockSpec((tm, kk), lambda i, j, k: (i, 0)), b, pl.BlockSpec((tn, kk), lambda i, j, k: (j, 0)))], NT,
        (m // tm, n // tn, 1), 1,
        [jax.ShapeDtypeStruct((m, n), dt) for dt in out_dtypes],
        [pl.BlockSpec((tm, tn), lambda i, j, k: (i, j))] * len(out_dtypes),
        extras, extra_specs, epilogue, rider=rider)


def _mm_nt_blocked(name, a_list, b3_list, tm, rider=None):
    m = a_list[0].shape[0]
    nb = b3_list[0].shape[2]
    pairs = [(a, pl.BlockSpec((tm, nb), lambda i, j, k: (i, k)), b3, pl.BlockSpec((None, D, nb), lambda i, j, k: (k, 0, 0)))
             for a, b3 in zip(a_list, b3_list)]
    return _single(_mm(name, pairs, NT, (m // tm, 1, NDEV), NDEV,
                       [jax.ShapeDtypeStruct((m, D), F32)], [pl.BlockSpec((tm, D), lambda i, j, k: (i, 0))],
                       acc_shape=(tm, D), rider=rider), rider)


def _mm_tn(name, a, b, tm, tn, rider=None):
    t, m = a.shape
    n = b.shape[1]
    return _single(_mm(
        name, [(a, pl.BlockSpec((t, tm), lambda i, j, k: (0, i)), b, pl.BlockSpec((t, tn), lambda i, j, k: (0, j)))], TN,
        (m // tm, n // tn, 1), 1,
        [jax.ShapeDtypeStruct((m, n), BF)], [pl.BlockSpec((tm, tn), lambda i, j, k: (i, j))], rider=rider), rider)


def _mm_tn_blocked(name, a, b, rider=None):
    t = a.shape[0]
    nb = b.shape[1] // NDEV
    return _single(_mm(
        name, [(a, pl.BlockSpec((t, D), lambda i, j, k: (0, 0)), b, pl.BlockSpec((t, nb), lambda i, j, k: (0, j)))], TN,
        (1, NDEV, 1), 1,
        [jax.ShapeDtypeStruct((NDEV, D, nb), BF)], [pl.BlockSpec((None, D, nb), lambda i, j, k: (j, 0, 0))], rider=rider), rider)


def _rowcall(name, fn, ins, in_specs, n_row_out, out_shapes, out_specs, grid, scratch_shapes=(), rider=None):
    def accumulate(o, v, i):
        @pl.when(i == 0)
        def _():
            o[...] = v.astype(o.dtype)

        @pl.when(i > 0)
        def _():
            o[...] += v.astype(o.dtype)

    def compute(in_refs, out_refs, scr):
        i = pl.program_id(0)
        vals = fn(i, in_refs, scr)
        for idx, (o, v) in enumerate(zip(out_refs, vals)):
            if idx < n_row_out:
                o[...] = v.astype(o.dtype)
            else:
                accumulate(o, v, i)

    return _call(name, compute, (grid,), ins, in_specs, out_shapes, out_specs, list(scratch_shapes), ("arbitrary",), rider)


def _rows(tr, w=D, cb=0):
    return pl.BlockSpec((tr, w), lambda i: (i, cb))


def _whole(shape):
    nd = len(shape)
    return pl.BlockSpec(shape, lambda i: (0,) * nd)


def _vec(n=1):
    return jax.ShapeDtypeStruct((n, D), F32)


def _rms_mod(x, gain, sc, sh):
    y = x * lax.rsqrt(jnp.mean(x * x, axis=-1, keepdims=True) + EPS)
    return (y * gain) * (1.0 + sc) + sh


def _layer_norm(x, g, b):
    mu = jnp.mean(x, axis=-1, keepdims=True)
    var = jnp.mean(jnp.square(x - mu), axis=-1, keepdims=True)
    return (x - mu) * lax.rsqrt(var + EPS) * g + b


def _norm_mod(name, x, gain, sc, sh):
    t = x.shape[0]
    tr = min(t, 256)

    def fn(i, r, _):
        return [_rms_mod(r[0][...], r[1][...], r[2][...], r[3][...])]

    return _rowcall(name, fn, [x, gain, sc, sh], [_rows(tr)] + [_whole((1, D))] * 3, 1,
                    [jax.ShapeDtypeStruct((t, D), BF)], [_rows(tr)], t // tr)[0]


def _norm_mod_bwd(name, x, gain, sc, sh, dh, dres, rider=None):
    t = x.shape[0]
    tr = min(t, 256)

    def fn(i, r, _):
        _, vjp = jax.vjp(_rms_mod, r[0][...], r[1][...], r[2][...], r[3][...])
        dx, dgain, dsc, dsh = vjp(r[4][...])
        return [dx + r[5][...], dgain, dsc, dsh]

    return _rowcall(name, fn, [x, gain, sc, sh, dh, dres], [_rows(tr)] + [_whole((1, D))] * 3 + [_rows(tr)] * 2, 1,
                    [jax.ShapeDtypeStruct((t, D), F32), _vec(), _vec(), _vec()],
                    [_rows(tr)] + [_whole((1, D))] * 3, t // tr, rider=rider)


def _gate_bwd(name, dx, f, g, scale):
    t = dx.shape[0]
    tr = min(t, 256)

    def fn(i, r, _):
        d = r[0][...]
        return [scale * r[2][...] * d, jnp.sum(scale * d * r[1][...].astype(F32), axis=0, keepdims=True)]

    return _rowcall(name, fn, [dx, f, g], [_rows(tr), _rows(tr), _whole((1, D))], 1,
                    [jax.ShapeDtypeStruct((t, D), BF), _vec()], [_rows(tr), _whole((1, D))], t // tr)


def _sgu_pre(up, vp, bu, bv, ln_g, ln_b):
    return jax.nn.gelu(up + bu), _layer_norm(jax.nn.gelu(vp + bv), ln_g, ln_b)


def _causal(w_ref, h):
    rows = lax.broadcasted_iota(jnp.int32, (CHUNK, CHUNK), 0)
    cols = lax.broadcasted_iota(jnp.int32, (CHUNK, CHUNK), 1)
    return jnp.where(cols <= rows, w_ref[h], 0.0)


def _sgu(name, proj, b_in, ln_g, ln_b, w_s, bias_full, rider=None):
    t = proj.shape[0]

    def fn(i, r, _):
        u, v = _sgu_pre(r[0][...], r[1][...], r[2][...], r[3][...], r[4][...], r[5][...])
        vb = v.astype(BF)
        mixed = [jnp.dot(_causal(r[6], h).astype(BF), vb[:, h * CHUNK:(h + 1) * CHUNK], preferred_element_type=F32)
                 for h in range(HEADS)]
        return [u * (jnp.concatenate(mixed, axis=1) + r[7][...])]

    return _rowcall(
        name, fn, [proj, proj, b_in, b_in, ln_g, ln_b, w_s, bias_full],
        [_rows(CHUNK, D, 0), _rows(CHUNK, D, 1), pl.BlockSpec((1, D), lambda i: (0, 0)), pl.BlockSpec((1, D), lambda i: (0, 1)),
         _whole((1, D)), _whole((1, D)), _whole((HEADS, CHUNK, CHUNK)), _whole((CHUNK, D))],
        1, [jax.ShapeDtypeStruct((t, D), BF)], [_rows(CHUNK)], t // CHUNK, rider=rider)


def _sgu_bwd(name, proj, b_in, ln_g, ln_b, w_s, bias_full, dout, rider=None):
    t = proj.shape[0]

    def fn(i, r, _):
        (u, v), vjp = jax.vjp(_sgu_pre, r[0][...], r[1][...], r[2][...], r[3][...], r[4][...], r[5][...])
        vb = v.astype(BF)
        d = r[8][...]
        masks = [_causal(r[6], h).astype(BF) for h in range(HEADS)]
        cols = [slice(h * CHUNK, (h + 1) * CHUNK) for h in range(HEADS)]
        mixed = jnp.concatenate([jnp.dot(masks[h], vb[:, cols[h]], preferred_element_type=F32) for h in range(HEADS)], axis=1)
        du = d * (mixed + r[7][...])
        dmix = d * u
        dmb = dmix.astype(BF)
        dv = jnp.concatenate([lax.dot_general(masks[h], dmb[:, cols[h]], TN, preferred_element_type=F32) for h in range(HEADS)], axis=1)
        rows = lax.broadcasted_iota(jnp.int32, (CHUNK, CHUNK), 0)
        lanes = lax.broadcasted_iota(jnp.int32, (CHUNK, CHUNK), 1)
        dws = jnp.stack([jnp.where(lanes <= rows, lax.dot_general(dmb[:, cols[h]], vb[:, cols[h]], NT, preferred_element_type=F32), 0.0)
                         for h in range(HEADS)])
        dbs = jnp.zeros((CHUNK, CHUNK), F32)
        for h in range(HEADS):
            dbs = dbs + jnp.where(lanes == h, jnp.sum(dmix[:, cols[h]], axis=1, keepdims=True), 0.0)
        dup, dvp, dbu, dbv, dg, db = vjp((du, dv))
        return [dup, dvp, dbu, dbv, dg, db, dws, dbs]

    return _rowcall(
        name, fn, [proj, proj, b_in, b_in, ln_g, ln_b, w_s, bias_full, dout],
        [_rows(CHUNK, D, 0), _rows(CHUNK, D, 1), pl.BlockSpec((1, D), lambda i: (0, 0)), pl.BlockSpec((1, D), lambda i: (0, 1)),
         _whole((1, D)), _whole((1, D)), _whole((HEADS, CHUNK, CHUNK)), _whole((CHUNK, D)), _rows(CHUNK)],
        2, [jax.ShapeDtypeStruct((t, D), BF)] * 2 + [_vec()] * 4
        + [jax.ShapeDtypeStruct((HEADS, CHUNK, CHUNK), F32), jax.ShapeDtypeStruct((CHUNK, CHUNK), F32)],
        [_rows(CHUNK)] * 2 + [_whole((1, D))] * 4 + [_whole((HEADS, CHUNK, CHUNK)), _whole((CHUNK, CHUNK))], t // CHUNK, rider=rider)


def _halo_before(tr, cb):
    return pl.BlockSpec((HALO, D), lambda i: (jnp.maximum(i * (tr // HALO) - 1, 0), cb))


def _halo_after(tr, cb, n_tiles):
    return pl.BlockSpec((HALO, D), lambda i: (jnp.minimum((i + 1) * (tr // HALO), n_tiles * (tr // HALO) - 1), cb))


def _ln_silu(z, g, b):
    return _silu(_layer_norm(z, g, b))


def _conv(name, proj, b_in, conv_w, conv_b, ln_g, ln_b, rider=None):
    t = proj.shape[0]
    tr = min(t, 256)

    def fn(i, r, scr):
        zbuf = scr[0]
        bv, bg = r[4][...], r[5][...]
        z0 = (r[0][...] + bv) * jax.nn.sigmoid(r[1][...] + bg)
        before = (r[2][...] + bv) * jax.nn.sigmoid(r[3][...] + bg)
        zbuf[pl.ds(0, HALO), :] = jnp.where(i > 0, before, 0.0)
        zbuf[pl.ds(HALO, tr), :] = z0
        acc = jnp.zeros((tr, D), F32) + r[7][...]
        for k in range(KW):
            acc = acc + r[6][k:k + 1, :] * zbuf[pl.ds(HALO - (KW - 1) + k, tr), :]
        return [z0, acc, _ln_silu(acc, r[8][...], r[9][...])]

    return _rowcall(
        name, fn, [proj, proj, proj, proj, b_in, b_in, conv_w, conv_b, ln_g, ln_b],
        [_rows(tr, D, 2), _rows(tr, D, 3), _halo_before(tr, 2), _halo_before(tr, 3),
         pl.BlockSpec((1, D), lambda i: (0, 2)), pl.BlockSpec((1, D), lambda i: (0, 3)),
         _whole((HALO, D)), _whole((1, D)), _whole((1, D)), _whole((1, D))],
        3, [jax.ShapeDtypeStruct((t, D), F32), jax.ShapeDtypeStruct((t, D), F32), jax.ShapeDtypeStruct((t, D), BF)],
        [_rows(tr)] * 3, t // tr, [pltpu.VMEM((tr + HALO, D), F32)], rider=rider)


def _conv_bwd(name, proj, b_in, conv_w, ln_g, ln_b, z0, z1, dz3, rider=None):
    t = proj.shape[0]
    tr = min(t, 256)
    n_tiles = t // tr

    def fn(i, r, scr):
        zbuf, dbuf = scr
        g, b = r[5][...], r[6][...]
        _, vjp = jax.vjp(_ln_silu, r[9][...], g, b)
        dz1, dg, db = vjp(r[11][...])
        _, vjp_after = jax.vjp(_ln_silu, r[10][...], g, b)
        dz1_after = vjp_after(r[12][...])[0]
        dbuf[pl.ds(0, tr), :] = dz1
        dbuf[pl.ds(tr, HALO), :] = jnp.where(i < n_tiles - 1, dz1_after, 0.0)
        zbuf[pl.ds(0, HALO), :] = jnp.where(i > 0, r[8][...], 0.0)
        zbuf[pl.ds(HALO, tr), :] = r[7][...]
        dz0 = jnp.zeros((tr, D), F32)
        dw_rows = []
        for k in range(KW):
            dz0 = dz0 + r[4][k:k + 1, :] * dbuf[pl.ds(KW - 1 - k, tr), :]
            dw_rows.append(jnp.sum(dz1 * zbuf[pl.ds(HALO - (KW - 1) + k, tr), :], axis=0, keepdims=True))
        dw_rows.append(jnp.zeros((HALO - KW, D), F32))
        a = r[0][...] + r[2][...]
        s = jax.nn.sigmoid(r[1][...] + r[3][...])
        dcv = dz0 * s
        dcg = dz0 * a * s * (1.0 - s)
        return [dcv, dcg, jnp.sum(dcv, axis=0, keepdims=True), jnp.sum(dcg, axis=0, keepdims=True),
                jnp.concatenate(dw_rows, axis=0), jnp.sum(dz1, axis=0, keepdims=True), dg, db]

    return _rowcall(
        name, fn, [proj, proj, b_in, b_in, conv_w, ln_g, ln_b, z0, z0, z1, z1, dz3, dz3],
        [_rows(tr, D, 2), _rows(tr, D, 3), pl.BlockSpec((1, D), lambda i: (0, 2)), pl.BlockSpec((1, D), lambda i: (0, 3)),
         _whole((HALO, D)), _whole((1, D)), _whole((1, D)),
         _rows(tr), _halo_before(tr, 0), _rows(tr), _halo_after(tr, 0, n_tiles), _rows(tr), _halo_after(tr, 0, n_tiles)],
        2, [jax.ShapeDtypeStruct((t, D), BF)] * 2 + [_vec(), _vec(), _vec(HALO), _vec(), _vec(), _vec()],
        [_rows(tr)] * 2 + [_whole((1, D))] * 2 + [_whole((HALO, D))] + [_whole((1, D))] * 3, n_tiles,
        [pltpu.VMEM((tr + HALO, D), F32), pltpu.VMEM((tr + HALO, D), F32)], rider=rider)


def _merge_fn(ga, gb, bga, bgb, ya, yb):
    return jax.nn.sigmoid(ga + bga) * ya + jax.nn.sigmoid(gb + bgb) * yb


def _merge(name, proj, b_in, ya, yb, rider=None):
    t = proj.shape[0]
    tr = min(t, 256)

    def fn(i, r, _):
        return [_merge_fn(*[x[...] for x in r])]

    return _rowcall(
        name, fn, [proj, proj, b_in, b_in, ya, yb],
        [_rows(tr, D, 4), _rows(tr, D, 5), pl.BlockSpec((1, D), lambda i: (0, 4)), pl.BlockSpec((1, D), lambda i: (0, 5)),
         _rows(tr), _rows(tr)],
        1, [jax.ShapeDtypeStruct((t, D), BF)], [_rows(tr)], t // tr, rider=rider)


def _merge_bwd(name, proj, b_in, ya, yb, dm):
    t = proj.shape[0]
    tr = min(t, 256)

    def fn(i, r, _):
        _, vjp = jax.vjp(_merge_fn, *[x[...] for x in r[:6]])
        dga, dgb, dbga, dbgb, dya, dyb = vjp(r[6][...])
        return [dga, dgb, dya, dyb, dbga, dbgb]

    return _rowcall(
        name, fn, [proj, proj, b_in, b_in, ya, yb, dm],
        [_rows(tr, D, 4), _rows(tr, D, 5), pl.BlockSpec((1, D), lambda i: (0, 4)), pl.BlockSpec((1, D), lambda i: (0, 5)),
         _rows(tr), _rows(tr), _rows(tr)],
        4, [jax.ShapeDtypeStruct((t, D), BF)] * 4 + [_vec(), _vec()], [_rows(tr)] * 4 + [_whole((1, D))] * 2, t // tr)


def _loss_head(name, x, gain, target):
    t = x.shape[0]
    tr = min(t, 256)

    def loss_fn(xv, g, tgt):
        y = xv * lax.rsqrt(jnp.mean(xv * xv, axis=-1, keepdims=True) + EPS) * g
        return 0.5 * jnp.sum(jnp.mean(jnp.square(y - tgt), axis=-1))

    def fn(i, r, _):
        loss, vjp = jax.vjp(loss_fn, r[0][...], r[1][...], r[2][...])
        dx, dg, _ = vjp(jnp.ones((), F32))
        return [dx, dg, jnp.zeros((1, D), F32) + loss]

    return _rowcall(name, fn, [x, gain, target], [_rows(tr), _whole((1, D)), _rows(tr)], 1,
                    [jax.ShapeDtypeStruct((t, D), F32), _vec(), _vec()], [_rows(tr), _whole((1, D)), _whole((1, D))], t // tr)


def _adamw(w, g, m, v):
    m = B1 * m + (1.0 - B1) * g
    v = B2 * v + (1.0 - B2) * jnp.square(g)
    m_hat = m / BC1
    v_hat = v / BC2
    delta = -LR * (m_hat / (jnp.sqrt(v_hat) + ADAM_EPS) + WD * w)
    return delta, m, v


def _ada_fwd(name, c_all, ada_w, ada_b):
    nc = ada_w.shape[1]

    def body(c_ref, w_ref, b_ref, o_ref):
        o_ref[...] = jnp.dot(_silu(c_ref[...]), w_ref[...], preferred_element_type=F32) + b_ref[...]

    return pl.pallas_call(body, name=name, out_shape=jax.ShapeDtypeStruct((NDEV, nc), F32),
                          compiler_params=_params(None))(c_all, ada_w, ada_b)


def _adamw_partials(name, chip_sum, received, w, m, v):
    r, c = w.shape
    tr = min(r, 256)

    def fn(i, refs, _):
        g = refs[0][...].astype(F32)
        for s in range(N_CHIPS - 1):
            g = g + refs[1][s].astype(F32)
        delta, m_new, v_new = _adamw(refs[2][...], g, refs[3][...], refs[4][...])
        return [g, delta, m_new, v_new]

    spec = pl.BlockSpec((tr, c), lambda i: (i, 0))
    return _rowcall(name, fn, [chip_sum, received, w, m, v],
                    [pl.BlockSpec((None, tr, c), lambda i: (0, i, 0)), pl.BlockSpec((N_CHIPS - 1, tr, c), lambda i: (0, i, 0)), spec, spec, spec],
                    4, [jax.ShapeDtypeStruct((r, c), F32)] * 4, [spec] * 4, r // tr)


def _adamw_small(name, gathered, w, m, v):
    r = w.shape[0]

    def body(g_ref, w_ref, m_ref, v_ref, go_ref, d_ref, mo_ref, vo_ref):
        g = g_ref[pl.ds(0, r), :]
        for s in range(1, NDEV):
            g = g + g_ref[pl.ds(s * r, r), :]
        delta, m_new, v_new = _adamw(w_ref[...], g, m_ref[...], v_ref[...])
        go_ref[...] = g
        d_ref[...] = delta
        mo_ref[...] = m_new
        vo_ref[...] = v_new

    return pl.pallas_call(body, name=name, out_shape=[jax.ShapeDtypeStruct((r, D), F32)] * 4,
                          compiler_params=_params(None))(gathered, w, m, v)


def _adamw_ada(name, c_all_t, dmod, w, m, v):
    r, c = w.shape
    tr = 256

    def fn(i, refs, _):
        ca = _silu(refs[0][...])
        g = ca[:, 0:1] * refs[1][0:1, :]
        for b in range(1, NDEV):
            g = g + ca[:, b:b + 1] * refs[1][b:b + 1, :]
        delta, m_new, v_new = _adamw(refs[2][...], g, refs[3][...], refs[4][...])
        return [g, delta, m_new, v_new]

    spec = pl.BlockSpec((tr, c), lambda i: (i, 0))
    return _rowcall(name, fn, [c_all_t, dmod, w, m, v],
                    [pl.BlockSpec((tr, NDEV), lambda i: (i, 0)), pl.BlockSpec((NDEV, c), lambda i: (0, 0)), spec, spec, spec], 4,
                    [jax.ShapeDtypeStruct((r, c), F32)] * 4, [spec] * 4, r // tr)


def _ffn_fwd(tag, x, gain, sh, sc, g, wg, wu, wd_shard, down_rider):
    t = x.shape[0]
    tm = min(t, 512)
    h = _norm_mod(f"{tag}_norm", x, gain, sc, sh)
    (gate, up, act), (wd,) = _ffn_up(f"{tag}_up", h, wg, wu, rider=_gather_rider([wd_shard]))

    def epilogue(f, xv, gv):
        return xv + 0.5 * gv * f, f

    res = _mm_nn(f"{tag}_down", act, wd.reshape(F, D), tm, D, 1024, extras=(x, g),
                 extra_specs=(pl.BlockSpec((tm, D), lambda i, j, k: (i, 0)), pl.BlockSpec((1, D), lambda i, j, k: (0, 0))),
                 epilogue=epilogue, out_dtypes=(F32, BF), rider=down_rider)
    (x_out, f), rode = res if down_rider else (res, None)
    return x_out, (x, h, gate, up, act, f), wd, rode


def _reduce_in_chip(tag, parts, slots):
    from_sibling = _exchange(f"{tag}_pair", _pair_rider(parts))
    return [_pair_add(f"{tag}_pair_add{a}", p, s, slots) for a, (p, s) in enumerate(zip(parts, from_sibling))]


def _ffn_bwd(tag, dx_out, saved, gain, sh, sc, g, wg, wu, wd, slots, ride_last):
    x, h, gate, up, act, f = saved
    t = x.shape[0]
    tm = min(t, 1024)
    df, dg = _gate_bwd(f"{tag}_gate_bwd", dx_out, f, g, 0.5)

    def act_bwd(da, gv, uv):
        gv = gv.astype(F32)
        s = jax.nn.sigmoid(gv)
        return da * uv.astype(F32) * (s * (1.0 + gv * (1.0 - s))), da * (gv * s)

    blk = pl.BlockSpec((tm, F // NDEV), lambda i, j, k: (i, j))
    dgate, dup = _mm_nt(f"{tag}_dact", df, wd.reshape(F, D), tm, F // NDEV, out_dtypes=(BF, BF),
                        extras=(gate, up), extra_specs=(blk, blk), epilogue=act_bwd)
    dwd = _mm_tn(f"{tag}_dwd", act, df, 512, D).reshape(NDEV, F // NDEV, D)
    (sum_d,) = _reduce_in_chip(f"{tag}_dwd", [dwd], slots)
    dwg, (got_d,) = _mm_tn_blocked(f"{tag}_dwg", h, dgate, rider=_chip_rider([sum_d]))
    dwu = _mm_tn_blocked(f"{tag}_dwu", h, dup)
    sum_g, sum_u = _reduce_in_chip(f"{tag}_dwgu", [dwg, dwu], slots)
    dh, (got_g,) = _mm_nt_blocked(f"{tag}_dh", [dgate, dup], [wg, wu], tm, rider=_chip_rider([sum_g]))
    if ride_last:
        (dx, dgain, dsc, dsh), (got_u,) = _norm_mod_bwd(f"{tag}_norm_bwd", x, gain, sc, sh, dh, dx_out, rider=_chip_rider([sum_u]))
    else:
        dx, dgain, dsc, dsh = _norm_mod_bwd(f"{tag}_norm_bwd", x, gain, sc, sh, dh, dx_out)
        got_u = None
    return dx, ((sum_g, got_g), (sum_u, got_u), (sum_d, got_d)), (dgain, dsc, dsh, dg)


def kernel(x, c, ada_w, ada_b, norm_ffn1, ffn1_w_gate, ffn1_w_up, ffn1_w_down, norm_mix, mix_w_in, mix_b_in, sgu_ln_g, sgu_ln_b, sgu_w_s, sgu_b_s, conv_w, conv_b, conv_ln_g, conv_ln_b, w_branch_a, w_branch_b, w_out, norm_ffn2, ffn2_w_gate, ffn2_w_up, ffn2_w_down, norm_final, loss_target, m_ada_w, m_ada_b, m_norm_ffn1, m_ffn1_w_gate, m_ffn1_w_up, m_ffn1_w_down, m_norm_mix, m_mix_w_in, m_mix_b_in, m_sgu_ln_g, m_sgu_ln_b, m_sgu_w_s, m_sgu_b_s, m_conv_w, m_conv_b, m_conv_ln_g, m_conv_ln_b, m_w_branch_a, m_w_branch_b, m_w_out, m_norm_ffn2, m_ffn2_w_gate, m_ffn2_w_up, m_ffn2_w_down, m_norm_final, v_ada_w, v_ada_b, v_norm_ffn1, v_ffn1_w_gate, v_ffn1_w_up, v_ffn1_w_down, v_norm_mix, v_mix_w_in, v_mix_b_in, v_sgu_ln_g, v_sgu_ln_b, v_sgu_w_s, v_sgu_b_s, v_conv_w, v_conv_b, v_conv_ln_g, v_conv_ln_b, v_w_branch_a, v_w_branch_b, v_w_out, v_norm_ffn2, v_ffn2_w_gate, v_ffn2_w_up, v_ffn2_w_down, v_norm_final):
    mx, my, mc = _position()
    me = 4 * mx + 2 * my + mc
    chip = 2 * mx + my
    slots = jnp.stack([2 * (chip ^ k) + mc for k in range(N_CHIPS)]).astype(jnp.int32)
    t = x.shape[1]
    tm = min(t, 1024)
    x0 = x.reshape(t, D)
    target = loss_target.reshape(t, D)
    given = dict(ffn1_w_gate=(ffn1_w_gate, m_ffn1_w_gate, v_ffn1_w_gate), ffn1_w_up=(ffn1_w_up, m_ffn1_w_up, v_ffn1_w_up),
                 ffn1_w_down=(ffn1_w_down, m_ffn1_w_down, v_ffn1_w_down), mix_w_in=(mix_w_in, m_mix_w_in, v_mix_w_in),
                 w_branch_a=(w_branch_a, m_w_branch_a, v_w_branch_a), w_branch_b=(w_branch_b, m_w_branch_b, v_w_branch_b),
                 w_out=(w_out, m_w_out, v_w_out), ffn2_w_gate=(ffn2_w_gate, m_ffn2_w_gate, v_ffn2_w_gate),
                 ffn2_w_up=(ffn2_w_up, m_ffn2_w_up, v_ffn2_w_up), ffn2_w_down=(ffn2_w_down, m_ffn2_w_down, v_ffn2_w_down))
    shard = {n: wmv[0][0].astype(BF) for n, wmv in given.items()}

    small_in = jnp.concatenate([c.reshape(8, CHUNK), conv_w[0], jnp.zeros((1, CHUNK), F32)], axis=0)
    small_all = _allgather_vmem("gather_c_conv", small_in).reshape(NDEV, 40, CHUNK)
    c_all = small_all[:, :8, :].reshape(NDEV, D)
    conv_w_full = jnp.transpose(small_all[:, 8:, :], (1, 0, 2)).reshape(HALO, D)
    ada_cols = N_MOD * D // NDEV
    mod_part = _ada_fwd("ada_fwd", c_all, ada_w[0], lax.dynamic_slice(ada_b, (0, me * ada_cols), (1, ada_cols)))
    mod_all = _allgather_vmem("gather_mod", mod_part).reshape(NDEV, NDEV, ada_cols)
    mod = lax.dynamic_index_in_dim(mod_all, me, axis=1, keepdims=False).reshape(N_MOD, 1, D)
    sh1, sc1, g1, sh2, sc2, g2, sh3, sc3, g3 = [mod[i] for i in range(N_MOD)]
    wg1, wu1 = _exchange("gather_ffn1", _gather_rider([shard["ffn1_w_gate"], shard["ffn1_w_up"]]))

    x1, saved1, wd1, (w_in,) = _ffn_fwd("ffn1", x0, norm_ffn1, sh1, sc1, g1, wg1, wu1, shard["ffn1_w_down"],
                                         _gather_rider([shard["mix_w_in"]]))
    h2 = _norm_mod("mix_norm", x1, norm_mix, sc2, sh2)
    proj, (wg2,) = _mm_nn_blocked("mix_in", h2, w_in, tm, rider=_gather_rider([shard["ffn2_w_gate"]]))
    bias_full = jnp.repeat(sgu_b_s[0].T, CHUNK, axis=1)
    (ua,), (wa3, wb3) = _sgu("sgu", proj, mix_b_in, sgu_ln_g, sgu_ln_b, sgu_w_s[0], bias_full,
                             rider=_gather_rider([shard["w_branch_a"], shard["w_branch_b"]]))
    (z0, z1, z3), (wu2,) = _conv("conv", proj, mix_b_in, conv_w_full, conv_b, conv_ln_g, conv_ln_b,
                                 rider=_gather_rider([shard["ffn2_w_up"]]))
    wa, wb = wa3.reshape(D, D), wb3.reshape(D, D)
    ya = _mm_nn("branch_a", ua, wa, tm, 512, D)[0]
    yb = _mm_nn("branch_b", z3, wb, tm, 512, D)[0]
    (merged,), (wo3,) = _merge("merge", proj, mix_b_in, ya, yb, rider=_gather_rider([shard["w_out"]]))
    wo = wo3.reshape(D, D)

    def mix_epilogue(yv, xv, gv):
        return xv + gv * yv, yv

    tmo = min(t, 512)
    x2, y = _mm_nn("mix_out", merged, wo, tmo, D, D, extras=(x1, g2),
                   extra_specs=(pl.BlockSpec((tmo, D), lambda i, j, k: (i, 0)), pl.BlockSpec((1, D), lambda i, j, k: (0, 0))),
                   epilogue=mix_epilogue, out_dtypes=(F32, BF))
    x3, saved3, wd2, _ = _ffn_fwd("ffn2", x2, norm_ffn2, sh3, sc3, g3, wg2, wu2, shard["ffn2_w_down"], None)

    norm_final2 = norm_final.reshape(1, D)
    dx3, d_norm_final, loss_row = _loss_head("loss_head", x3, norm_final2, target)
    dx2, (gr_g2, (sum_u2, _), gr_d2), (d_norm_ffn2, dsc3, dsh3, dg3) = _ffn_bwd(
        "ffn2", dx3, saved3, norm_ffn2, sh3, sc3, g3, wg2, wu2, wd2, slots, ride_last=False)
    dy, dg2 = _gate_bwd("mix_gate_bwd", dx2, y, g2, 1.0)
    dm = _mm_nt("mix_out_bwd", dy, wo, tm, 512)[0]
    dwo = _mm_tn("mix_dwo", merged, dy, 512, D).reshape(NDEV, D // NDEV, D)
    dga, dgb, dya, dyb, db_ga, db_gb = _merge_bwd("merge_bwd", proj, mix_b_in, ya, yb, dm)
    dua = _mm_nt("branch_a_bwd", dya, wa, tm, 512)[0]
    dwa = _mm_tn("branch_dwa", ua, dya, 512, D).reshape(NDEV, D // NDEV, D)
    dz3 = _mm_nt("branch_b_bwd", dyb, wb, tm, 512)[0]
    dwb = _mm_tn("branch_dwb", z3, dyb, 512, D).reshape(NDEV, D // NDEV, D)
    sum_a, sum_b, sum_o = _reduce_in_chip("mix_dw", [dwa, dwb, dwo], slots)
    (dup, dvp, db_u, db_v, d_sgu_g, d_sgu_b, d_ws, d_bs_t), (got_a, got_b, got_o) = _sgu_bwd(
        "sgu_bwd", proj, mix_b_in, sgu_ln_g, sgu_ln_b, sgu_w_s[0], bias_full, dua, rider=_chip_rider([sum_a, sum_b, sum_o]))
    (dcv, dcg, db_cv, db_cg, d_cw, d_cb, d_cln_g, d_cln_b), (got_u2,) = _conv_bwd(
        "conv_bwd", proj, mix_b_in, conv_w_full, conv_ln_g, conv_ln_b, z0, z1, dz3, rider=_chip_rider([sum_u2]))
    dproj = jnp.concatenate([dup, dvp, dcv, dcg, dga, dgb], axis=1)
    dwin = _mm_tn_blocked("mix_dwin", h2, dproj)
    (sum_in,) = _reduce_in_chip("mix_dwin", [dwin], slots)
    dh2, (got_in,) = _mm_nt_blocked("mix_in_bwd", [dproj], [w_in], tm, rider=_chip_rider([sum_in]))
    dx1, d_norm_mix, dsc2, dsh2 = _norm_mod_bwd("mix_norm_bwd", x1, norm_mix, sc2, sh2, dh2, dx2)
    dx0, (gr_g1, gr_u1, gr_d1), (d_norm_ffn1, dsc1, dsh1, dg1) = _ffn_bwd(
        "ffn1", dx1, saved1, norm_ffn1, sh1, sc1, g1, wg1, wu1, wd1, slots, ride_last=True)

    grads = dict(ffn1_w_gate=gr_g1, ffn1_w_up=gr_u1, ffn1_w_down=gr_d1, mix_w_in=(sum_in, got_in), w_branch_a=(sum_a, got_a),
                 w_branch_b=(sum_b, got_b), w_out=(sum_o, got_o), ffn2_w_gate=gr_g2, ffn2_w_up=(sum_u2, got_u2), ffn2_w_down=gr_d2)
    big_out = {}
    for n, (chip_sum, received) in grads.items():
        wv, mv, vv = given[n]
        big_out[n] = [o.reshape(wv.shape) for o in _adamw_partials(f"adamw_{n}", chip_sum, received, wv[0], mv[0], vv[0])]

    d_bs = jnp.transpose(d_bs_t[:, :HEADS])
    pack_rows = [dsh1, dsc1, dg1, dsh2, dsc2, dg2, dsh3, dsc3, dg3,
                 d_norm_ffn1, d_norm_mix, d_norm_ffn2, d_norm_final,
                 db_u, db_v, db_cv, db_cg, db_ga, db_gb,
                 d_sgu_g, d_sgu_b, d_bs.reshape(1, D), d_cb, d_cln_g, d_cln_b,
                 d_ws.reshape(CHUNK, D), d_cw[:KW], loss_row, jnp.zeros((R_TOTAL - R_LOSS - 1, D), F32)]
    packed_all = _allgather_vmem("gather_small_grads", jnp.concatenate(pack_rows, axis=0))

    col0 = me * CHUNK

    def pack(ada_b_, n1, nm, n2, nf, b_in, lg, lb, bs, cb, clg, clb, ws, cw, fill):
        cw_full = lax.dynamic_update_slice(jnp.full((KW, D), fill, F32), cw[0], (0, col0))
        return jnp.concatenate([ada_b_.reshape(N_MOD, D), n1, nm, n2, nf.reshape(1, D), b_in.reshape(6, D), lg, lb, bs.reshape(1, D),
                                cb, clg, clb, ws.reshape(CHUNK, D), cw_full, jnp.full((R_TOTAL - R_LOSS, D), fill, F32)], axis=0)

    w_small = pack(ada_b, norm_ffn1, norm_mix, norm_ffn2, norm_final, mix_b_in, sgu_ln_g, sgu_ln_b, sgu_b_s, conv_b,
                   conv_ln_g, conv_ln_b, sgu_w_s, conv_w, 0.0)
    m_small = pack(m_ada_b, m_norm_ffn1, m_norm_mix, m_norm_ffn2, m_norm_final, m_mix_b_in, m_sgu_ln_g, m_sgu_ln_b, m_sgu_b_s,
                   m_conv_b, m_conv_ln_g, m_conv_ln_b, m_sgu_w_s, m_conv_w, 0.0)
    v_small = pack(v_ada_b, v_norm_ffn1, v_norm_mix, v_norm_ffn2, v_norm_final, v_mix_b_in, v_sgu_ln_g, v_sgu_ln_b, v_sgu_b_s,
                   v_conv_b, v_conv_ln_g, v_conv_ln_b, v_sgu_w_s, v_conv_w, 1.0)
    small_out = _adamw_small("adamw_small", packed_all, w_small, m_small, v_small)

    def unpack(p):
        return dict(
            ada_b=p[R_ADA_B:R_NORMS].reshape(1, N_MOD * D), norm_ffn1=p[9:10], norm_mix=p[10:11], norm_ffn2=p[11:12],
            norm_final=p[12], mix_b_in=p[R_BIN:R_SGU].reshape(1, D_IN), sgu_ln_g=p[19:20], sgu_ln_b=p[20:21],
            sgu_b_s=p[21].reshape(1, HEADS, CHUNK), conv_b=p[22:23], conv_ln_g=p[23:24], conv_ln_b=p[24:25],
            sgu_w_s=p[R_WS:R_CW].reshape(1, HEADS, CHUNK, CHUNK),
            conv_w=lax.dynamic_slice(p[R_CW:R_LOSS], (0, col0), (KW, CHUNK)).reshape(1, KW, CHUNK))

    small = [unpack(p) for p in small_out]
    loss = small_out[0][R_LOSS, 0]

    dmod_all = packed_all.reshape(NDEV, R_TOTAL, D)[:, :N_MOD, :].reshape(NDEV, N_MOD * D)
    dmod_cols = lax.dynamic_slice(dmod_all, (0, me * ada_cols), (NDEV, ada_cols))
    ada_out = [o.reshape(ada_w.shape) for o in _adamw_ada("adamw_ada_w", jnp.transpose(c_all), dmod_cols, ada_w[0], m_ada_w[0], v_ada_w[0])]

    order = ["ada_w", "ada_b", "norm_ffn1", "ffn1_w_gate", "ffn1_w_up", "ffn1_w_down", "norm_mix", "mix_w_in", "mix_b_in",
             "sgu_ln_g", "sgu_ln_b", "sgu_w_s", "sgu_b_s", "conv_w", "conv_b", "conv_ln_g", "conv_ln_b", "w_branch_a",
             "w_branch_b", "w_out", "norm_ffn2", "ffn2_w_gate", "ffn2_w_up", "ffn2_w_down", "norm_final"]

    def leaf(n, kind):
        if n == "ada_w":
            return ada_out[kind]
        if n in big_out:
            return big_out[n][kind]
        return small[kind][n]

    return (loss, dx0.reshape(x.shape), *[leaf(n, kind) for kind in range(4) for n in order])
```

```python
import jax
import jax.numpy as jnp
from jax import lax
from jax.experimental import pallas as pl
from jax.experimental.pallas import tpu as pltpu

D = 1024
F = 4 * D
D_IN = 6 * D
HEADS = 8
CHUNK = 128
KW = 31
HALO = 32
N_MOD = 9
NDEV = 8
N_CHIPS = 4
EPS = 1e-6
LR, B1, B2, ADAM_EPS, WD, STEP = 0.001, 0.9, 0.999, 1e-08, 0.01, 10
BC1 = 1.0 - B1 ** STEP
BC2 = 1.0 - B2 ** STEP
VMEM_LIMIT = 56 * 1024 * 1024
MESH = pl.DeviceIdType.MESH
HBM = pl.BlockSpec(memory_space=pltpu.HBM)
VMEM = pl.BlockSpec(memory_space=pltpu.VMEM)
BF = jnp.bfloat16
F32 = jnp.float32

NN = (((1,), (0,)), ((), ()))
NT = (((1,), (1,)), ((), ()))
TN = (((0,), (0,)), ((), ()))

R_ADA_B, R_NORMS, R_BIN, R_SGU, R_CONV, R_WS, R_CW, R_LOSS, R_TOTAL = 0, 9, 13, 19, 22, 25, 153, 184, 192


def _params(sem):
    return pltpu.CompilerParams(dimension_semantics=sem, vmem_limit_bytes=VMEM_LIMIT)


def _position():
    return lax.axis_index("x"), lax.axis_index("y"), lax.axis_index("c")


def _flip(pos, k):
    x, y, c = pos
    return (x ^ (k >> 2 & 1), y ^ (k >> 1 & 1), c ^ (k & 1))


def _index(pos):
    return 4 * pos[0] + 2 * pos[1] + pos[2]


def _allgather_vmem(name, shard):
    m_per, n = shard.shape

    def body(x_ref, out_ref, send_sems, recv_sems, local_sem):
        x, y, c = _position()
        me, sibling = (x, y, c), (x, y, 1 - c)
        chips = [(1 - x, y), (x, 1 - y), (1 - x, 1 - y)]

        def rows(pos):
            return out_ref.at[pl.ds(_index(pos) * m_per, m_per), :]

        def copy(k, block, to, src=None):
            return pltpu.make_async_remote_copy(
                src_ref=rows(block) if src is None else src, dst_ref=rows(block),
                send_sem=send_sems.at[k], recv_sem=recv_sems.at[k], device_id=to, device_id_type=MESH)

        mine = pltpu.make_async_copy(x_ref, rows(me), local_sem)
        mine.start()
        first = [copy(0, me, sibling, src=x_ref)]
        first += [copy(1 + j, me, (*chip, c), src=x_ref) for j, chip in enumerate(chips)]
        for cp in first:
            cp.start()
        passed = [copy(4 + j, (*chip, c), sibling) for j, chip in enumerate(chips)]
        for j, chip in enumerate(chips):
            copy(1 + j, (*chip, c), me).wait_recv()
            passed[j].start()
        copy(0, sibling, me).wait_recv()
        for j, chip in enumerate(chips):
            copy(4 + j, (*chip, 1 - c), me).wait_recv()
        for cp in first + passed:
            cp.wait_send()
        mine.wait()

    return pl.pallas_call(
        body, name=name,
        out_shape=jax.ShapeDtypeStruct((NDEV * m_per, n), shard.dtype),
        in_specs=[VMEM], out_specs=VMEM,
        scratch_shapes=[pltpu.SemaphoreType.DMA((7,)), pltpu.SemaphoreType.DMA((7,)), pltpu.SemaphoreType.DMA],
    )(shard)


class _Rider:
    def __init__(self, ins, out_shapes, sems, start, finish):
        self.ins, self.out_shapes, self.sems, self.start, self.finish = list(ins), list(out_shapes), list(sems), start, finish


def _gather_rider(shards):
    n = len(shards)

    def setup(ins, outs, sems):
        send_sems, recv_sems, local_sems = sems
        x, y, c = _position()
        me, sibling = (x, y, c), (x, y, 1 - c)
        chips = [(1 - x, y), (x, 1 - y), (1 - x, 1 - y)]

        def copy(a, k, block, to, own=False):
            slot = outs[a].at[_index(block)]
            return pltpu.make_async_remote_copy(
                src_ref=ins[a] if own else slot, dst_ref=slot,
                send_sem=send_sems.at[k, a], recv_sem=recv_sems.at[k, a], device_id=to, device_id_type=MESH)

        mine = [pltpu.make_async_copy(ins[a], outs[a].at[_index(me)], local_sems.at[a]) for a in range(n)]
        first = []
        for a in range(n):
            first.append(copy(a, 0, me, sibling, own=True))
            first += [copy(a, 1 + j, me, (*chip, c), own=True) for j, chip in enumerate(chips)]
        return me, sibling, chips, c, copy, mine, first

    def start(ins, outs, sems):
        *_, mine, first = setup(ins, outs, sems)
        for cp in mine + first:
            cp.start()

    def finish(ins, outs, sems):
        me, sibling, chips, c, copy, mine, first = setup(ins, outs, sems)
        passed = []
        for j, chip in enumerate(chips):
            for a in range(n):
                copy(a, 1 + j, (*chip, c), me).wait_recv()
                fwd = copy(a, 4 + j, (*chip, c), sibling)
                fwd.start()
                passed.append(fwd)
        for a in range(n):
            copy(a, 0, sibling, me).wait_recv()
            for j, chip in enumerate(chips):
                copy(a, 4 + j, (*chip, 1 - c), me).wait_recv()
        for cp in first + passed:
            cp.wait_send()
        for cp in mine:
            cp.wait()

    return _Rider(shards, [jax.ShapeDtypeStruct((NDEV, *s.shape), s.dtype) for s in shards],
                  [pltpu.SemaphoreType.DMA((7, n)), pltpu.SemaphoreType.DMA((7, n)), pltpu.SemaphoreType.DMA((n,))], start, finish)


def _pair_rider(parts):
    n = len(parts)

    def copies(ins, outs, sems):
        send_sems, recv_sems = sems
        x, y, c = _position()
        q = 2 * x + y
        return [pltpu.make_async_remote_copy(
            src_ref=ins[a].at[2 * (q ^ k) + (1 - c)], dst_ref=outs[a].at[k],
            send_sem=send_sems.at[k, a], recv_sem=recv_sems.at[k, a], device_id=(x, y, 1 - c), device_id_type=MESH)
            for a in range(n) for k in range(N_CHIPS)]

    def start(ins, outs, sems):
        for cp in copies(ins, outs, sems):
            cp.start()

    def finish(ins, outs, sems):
        for cp in copies(ins, outs, sems):
            cp.wait()

    return _Rider(parts, [jax.ShapeDtypeStruct((N_CHIPS, *p.shape[1:]), p.dtype) for p in parts],
                  [pltpu.SemaphoreType.DMA((N_CHIPS, n)), pltpu.SemaphoreType.DMA((N_CHIPS, n))], start, finish)


NEIGHBOURS = (1, 2)
DIAGONAL = (3,)
OTHER_CHIPS = NEIGHBOURS + DIAGONAL


def _chip_rider(sums, ks=OTHER_CHIPS):
    n = len(sums)

    def copies(ins, outs, sems):
        send_sems, recv_sems = sems
        me = _position()
        return [pltpu.make_async_remote_copy(
            src_ref=ins[a].at[k], dst_ref=outs[a].at[j],
            send_sem=send_sems.at[j, a], recv_sem=recv_sems.at[j, a], device_id=_flip(me, 2 * k), device_id_type=MESH)
            for a in range(n) for j, k in enumerate(ks)]

    def start(ins, outs, sems):
        for cp in copies(ins, outs, sems):
            cp.start()

    def finish(ins, outs, sems):
        for cp in copies(ins, outs, sems):
            cp.wait()

    return _Rider(sums, [jax.ShapeDtypeStruct((len(ks), *s.shape[1:]), s.dtype) for s in sums],
                  [pltpu.SemaphoreType.DMA((len(ks), n)), pltpu.SemaphoreType.DMA((len(ks), n))], start, finish)


def _grid_edge(grid, last):
    cond = None
    for d, n in enumerate(grid):
        here = pl.program_id(d) == (n - 1 if last else 0)
        cond = here if cond is None else jnp.logical_and(cond, here)
    return cond


def _call(name, compute, grid, ins, in_specs, out_shapes, out_specs, scratch_shapes, semantics, rider=None):
    riders = [rider] if isinstance(rider, _Rider) else list(rider or [])
    n_in, n_out, n_scr = len(ins), len(out_shapes), len(scratch_shapes)
    n_rin, n_rout, n_rsem = [sum(len(part(r)) for r in riders) for part in (lambda r: r.ins, lambda r: r.out_shapes, lambda r: r.sems)]
    cuts = [0, n_in, n_in + n_rin, n_in + n_rin + n_out, n_in + n_rin + n_out + n_rout, n_in + n_rin + n_out + n_rout + n_scr]

    def body(*refs):
        in_refs, rin_refs, out_refs, rout_refs, scr_refs = [refs[a:b] for a, b in zip(cuts[:-1], cuts[1:])]
        rsem_refs = refs[cuts[-1]:]
        mine, at = [], [0, 0, 0]
        for r in riders:
            mine.append((r, rin_refs[at[0]:at[0] + len(r.ins)], rout_refs[at[1]:at[1] + len(r.out_shapes)],
                         rsem_refs[at[2]:at[2] + len(r.sems)]))
            at = [at[0] + len(r.ins), at[1] + len(r.out_shapes), at[2] + len(r.sems)]
        if riders:
            @pl.when(_grid_edge(grid, last=False))
            def _():
                for r, a, b, c in mine:
                    r.start(a, b, c)

        compute(in_refs, out_refs, scr_refs)
        if riders:
            @pl.when(_grid_edge(grid, last=True))
            def _():
                for r, a, b, c in mine:
                    r.finish(a, b, c)

    res = pl.pallas_call(
        body, name=name, grid=grid,
        out_shape=list(out_shapes) + [s for r in riders for s in r.out_shapes],
        in_specs=list(in_specs) + [HBM] * n_rin, out_specs=list(out_specs) + [HBM] * n_rout,
        scratch_shapes=list(scratch_shapes) + [s for r in riders for s in r.sems],
        compiler_params=_params(semantics),
    )(*ins, *[a for r in riders for a in r.ins])
    return (res[:n_out], res[n_out:]) if riders else res


def _exchange(name, rider):
    return _call(name, lambda *_: None, (1,), [], [], [], [], [], ("arbitrary",), rider)[1]


def _pair_add(name, parts, from_sibling, slots):
    n = len(parts)

    def body(s_ref, *refs):
        for a in range(n):
            refs[2 * n + a][...] = (refs[a][...].astype(F32) + refs[n + a][...].astype(F32)).astype(refs[2 * n + a].dtype)

    def slab(p, picked):
        _, r, c = p.shape
        return pl.BlockSpec((None, r, c), (lambda k, s: (s[k], 0, 0)) if picked else (lambda k, s: (k, 0, 0)))

    return pl.pallas_call(
        body, name=name,
        grid_spec=pltpu.PrefetchScalarGridSpec(
            num_scalar_prefetch=1, grid=(N_CHIPS,),
            in_specs=[slab(p, True) for p in parts] + [slab(p, False) for p in parts],
            out_specs=[slab(p, False) for p in parts]),
        out_shape=[jax.ShapeDtypeStruct((N_CHIPS, *p.shape[1:]), p.dtype) for p in parts],
        compiler_params=_params(("arbitrary",)),
    )(slots, *parts, *from_sibling)


def _mm(name, pairs, dims, grid, nk, out_shapes, out_specs, extras=(), extra_specs=(), epilogue=None, acc_shape=None, rider=None):
    n_pairs = len(pairs)

    def compute(ins, outs, scratch):
        def partial_sum():
            total = None
            for p in range(n_pairs):
                d = lax.dot_general(ins[2 * p][...], ins[2 * p + 1][...], dims, preferred_element_type=F32)
                total = d if total is None else total + d
            return total

        def finish(r):
            ex = [e[...] for e in ins[2 * n_pairs:]]
            res = epilogue(r, *ex) if epilogue is not None else (r,)
            for o, v in zip(outs, res):
                o[...] = v.astype(o.dtype)

        if nk == 1:
            finish(partial_sum())
        else:
            acc = scratch[0]
            k = pl.program_id(2)

            @pl.when(k == 0)
            def _():
                acc[...] = partial_sum()

            @pl.when(k > 0)
            def _():
                acc[...] += partial_sum()

            @pl.when(k == nk - 1)
            def _():
                finish(acc[...])

    operands, specs = [], []
    for a, a_spec, b, b_spec in pairs:
        operands += [a, b]
        specs += [a_spec, b_spec]
    return _call(name, compute, grid, operands + list(extras), specs + list(extra_specs), out_shapes, out_specs,
                 [pltpu.VMEM(acc_shape, F32)] if nk > 1 else [], ("parallel", "parallel", "arbitrary"), rider)


def _single(res, rider):
    return (res[0][0], res[1]) if rider else res[0]


def _silu(x):
    return x * jax.nn.sigmoid(x)


def _ffn_up(name, h, wg, wu, rider=None):
    t = h.shape[0]
    tm = min(t, 1024)
    nb = F // NDEV

    def compute(ins, outs, _):
        hv = ins[0][...]
        g = jnp.dot(hv, ins[1][...], preferred_element_type=F32)
        u = jnp.dot(hv, ins[2][...], preferred_element_type=F32)
        outs[0][...] = g.astype(BF)
        outs[1][...] = u.astype(BF)
        outs[2][...] = (_silu(g) * u).astype(BF)

    w_spec = pl.BlockSpec((None, D, nb), lambda i, j: (j, 0, 0))
    o_spec = pl.BlockSpec((tm, nb), lambda i, j: (i, j))
    return _call(name, compute, (t // tm, NDEV), [h, wg, wu], [pl.BlockSpec((tm, D), lambda i, j: (i, 0)), w_spec, w_spec],
                 [jax.ShapeDtypeStruct((t, F), BF)] * 3, [o_spec] * 3, [], ("parallel", "arbitrary"), rider)


def _mm_nn(name, a, b, tm, tn, tk, extras=(), extra_specs=(), epilogue=None, out_dtypes=(F32,), rider=None):
    m, kk = a.shape
    n = b.shape[1]
    nk = kk // tk
    return _mm(
        name, [(a, pl.BlockSpec((tm, tk), lambda i, j, k: (i, k)), b, pl.BlockSpec((tk, tn), lambda i, j, k: (k, j)))], NN,
        (m // tm, n // tn, nk), nk,
        [jax.ShapeDtypeStruct((m, n), dt) for dt in out_dtypes],
        [pl.BlockSpec((tm, tn), lambda i, j, k: (i, j))] * len(out_dtypes),
        extras, extra_specs, epilogue, (tm, tn), rider)


def _mm_nn_blocked(name, a, b3, tm, rider=None):
    m = a.shape[0]
    nb = b3.shape[2]
    return _single(_mm(
        name, [(a, pl.BlockSpec((tm, D), lambda i, j, k: (i, 0)), b3, pl.BlockSpec((None, D, nb), lambda i, j, k: (j, 0, 0)))], NN,
        (m // tm, NDEV, 1), 1,
        [jax.ShapeDtypeStruct((m, NDEV * nb), F32)], [pl.BlockSpec((tm, nb), lambda i, j, k: (i, j))], rider=rider), rider)


def _mm_nt(name, a, b, tm, tn, out_dtypes=(F32,), extras=(), extra_specs=(), epilogue=None, rider=None):
    m, kk = a.shape
    n = b.shape[0]
    return _mm(
        name, [(a, pl.BlockSpec((tm, kk), lambda i, j, k: (i, 0)), b, pl.BlockSpec((tn, kk), lambda i, j, k: (j, 0)))], NT,
        (m // tm, n // tn, 1), 1,
        [jax.ShapeDtypeStruct((m, n), dt) for dt in out_dtypes],
        [pl.BlockSpec((tm, tn), lambda i, j, k: (i, j))] * len(out_dtypes),
        extras, extra_specs, epilogue, rider=rider)


def _mm_nt_blocked(name, a_list, b3_list, tm, rider=None):
    m = a_list[0].shape[0]
    nb = b3_list[0].shape[2]
    pairs = [(a, pl.BlockSpec((tm, nb), lambda i, j, k: (i, k)), b3, pl.BlockSpec((None, D, nb), lambda i, j, k: (k, 0, 0)))
             for a, b3 in zip(a_list, b3_list)]
    return _single(_mm(name, pairs, NT, (m // tm, 1, NDEV), NDEV,
                       [jax.ShapeDtypeStruct((m, D), F32)], [pl.BlockSpec((tm, D), lambda i, j, k: (i, 0))],
                       acc_shape=(tm, D), rider=rider), rider)


def _mm_tn(name, a, b, tm, tn, rider=None):
    t, m = a.shape
    n = b.shape[1]
    return _single(_mm(
        name, [(a, pl.BlockSpec((t, tm), lambda i, j, k: (0, i)), b, pl.BlockSpec((t, tn), lambda i, j, k: (0, j)))], TN,
        (m // tm, n // tn, 1), 1,
        [jax.ShapeDtypeStruct((m, n), BF)], [pl.BlockSpec((tm, tn), lambda i, j, k: (i, j))], rider=rider), rider)


def _mm_tn_blocked(name, a, b, rider=None):
    t = a.shape[0]
    nb = b.shape[1] // NDEV
    return _single(_mm(
        name, [(a, pl.BlockSpec((t, D), lambda i, j, k: (0, 0)), b, pl.BlockSpec((t, nb), lambda i, j, k: (0, j)))], TN,
        (1, NDEV, 1), 1,
        [jax.ShapeDtypeStruct((NDEV, D, nb), BF)], [pl.BlockSpec((None, D, nb), lambda i, j, k: (j, 0, 0))], rider=rider), rider)


def _dw_gate_up(name, h, dgate, dup, rider=None):
    t = h.shape[0]
    nb = F // NDEV

    def compute(ins, outs, _):
        hv = ins[0][...]
        outs[0][...] = lax.dot_general(hv, ins[1][...], TN, preferred_element_type=F32).astype(BF)
        outs[1][...] = lax.dot_general(hv, ins[2][...], TN, preferred_element_type=F32).astype(BF)

    d_spec = pl.BlockSpec((t, nb), lambda j: (0, j))
    o_spec = pl.BlockSpec((None, D, nb), lambda j: (j, 0, 0))
    return _call(name, compute, (NDEV,), [h, dgate, dup], [pl.BlockSpec((t, D), lambda j: (0, 0)), d_spec, d_spec],
                 [jax.ShapeDtypeStruct((NDEV, D, nb), BF)] * 2, [o_spec] * 2, [], ("arbitrary",), rider)


def _rowcall(name, fn, ins, in_specs, n_row_out, out_shapes, out_specs, grid, scratch_shapes=(), rider=None):
    def accumulate(o, v, i):
        @pl.when(i == 0)
        def _():
            o[...] = v.astype(o.dtype)

        @pl.when(i > 0)
        def _():
            o[...] += v.astype(o.dtype)

    def compute(in_refs, out_refs, scr):
        i = pl.program_id(0)
        vals = fn(i, in_refs, scr)
        for idx, (o, v) in enumerate(zip(out_refs, vals)):
            if idx < n_row_out:
                o[...] = v.astype(o.dtype)
            else:
                accumulate(o, v, i)

    return _call(name, compute, (grid,), ins, in_specs, out_shapes, out_specs, list(scratch_shapes), ("arbitrary",), rider)


def _rows(tr, w=D, cb=0):
    return pl.BlockSpec((tr, w), lambda i: (i, cb))


def _whole(shape):
    nd = len(shape)
    return pl.BlockSpec(shape, lambda i: (0,) * nd)


def _vec(n=1):
    return jax.ShapeDtypeStruct((n, D), F32)


def _rms_mod(x, gain, sc, sh):
    y = x * lax.rsqrt(jnp.mean(x * x, axis=-1, keepdims=True) + EPS)
    return (y * gain) * (1.0 + sc) + sh


def _layer_norm(x, g, b):
    mu = jnp.mean(x, axis=-1, keepdims=True)
    var = jnp.mean(jnp.square(x - mu), axis=-1, keepdims=True)
    return (x - mu) * lax.rsqrt(var + EPS) * g + b


def _norm_mod(name, x, gain, sc, sh):
    t = x.shape[0]
    tr = min(t, 256)

    def fn(i, r, _):
        return [_rms_mod(r[0][...], r[1][...], r[2][...], r[3][...])]

    return _rowcall(name, fn, [x, gain, sc, sh], [_rows(tr)] + [_whole((1, D))] * 3, 1,
                    [jax.ShapeDtypeStruct((t, D), BF)], [_rows(tr)], t // tr)[0]


def _norm_mod_bwd(name, x, gain, sc, sh, dh, dres, rider=None):
    t = x.shape[0]
    tr = min(t, 256)

    def fn(i, r, _):
        _, vjp = jax.vjp(_rms_mod, r[0][...], r[1][...], r[2][...], r[3][...])
        dx, dgain, dsc, dsh = vjp(r[4][...])
        return [dx + r[5][...], dgain, dsc, dsh]

    return _rowcall(name, fn, [x, gain, sc, sh, dh, dres], [_rows(tr)] + [_whole((1, D))] * 3 + [_rows(tr)] * 2, 1,
                    [jax.ShapeDtypeStruct((t, D), F32), _vec(), _vec(), _vec()],
                    [_rows(tr)] + [_whole((1, D))] * 3, t // tr, rider=rider)


def _gate_bwd(name, dx, f, g, scale):
    t = dx.shape[0]
    tr = min(t, 256)

    def fn(i, r, _):
        d = r[0][...]
        return [scale * r[2][...] * d, jnp.sum(scale * d * r[1][...].astype(F32), axis=0, keepdims=True)]

    return _rowcall(name, fn, [dx, f, g], [_rows(tr), _rows(tr), _whole((1, D))], 1,
                    [jax.ShapeDtypeStruct((t, D), BF), _vec()], [_rows(tr), _whole((1, D))], t // tr)


def _sgu_pre(up, vp, bu, bv, ln_g, ln_b):
    return jax.nn.gelu(up + bu), _layer_norm(jax.nn.gelu(vp + bv), ln_g, ln_b)


def _causal(w_ref, h):
    rows = lax.broadcasted_iota(jnp.int32, (CHUNK, CHUNK), 0)
    cols = lax.broadcasted_iota(jnp.int32, (CHUNK, CHUNK), 1)
    return jnp.where(cols <= rows, w_ref[h], 0.0)


def _sgu(name, proj, b_in, ln_g, ln_b, w_s, bias_full, rider=None):
    t = proj.shape[0]

    def fn(i, r, _):
        u, v = _sgu_pre(r[0][...], r[1][...], r[2][...], r[3][...], r[4][...], r[5][...])
        vb = v.astype(BF)
        mixed = [jnp.dot(_causal(r[6], h).astype(BF), vb[:, h * CHUNK:(h + 1) * CHUNK], preferred_element_type=F32)
                 for h in range(HEADS)]
        return [u * (jnp.concatenate(mixed, axis=1) + r[7][...])]

    return _rowcall(
        name, fn, [proj, proj, b_in, b_in, ln_g, ln_b, w_s, bias_full],
        [_rows(CHUNK, D, 0), _rows(CHUNK, D, 1), pl.BlockSpec((1, D), lambda i: (0, 0)), pl.BlockSpec((1, D), lambda i: (0, 1)),
         _whole((1, D)), _whole((1, D)), _whole((HEADS, CHUNK, CHUNK)), _whole((CHUNK, D))],
        1, [jax.ShapeDtypeStruct((t, D), BF)], [_rows(CHUNK)], t // CHUNK, rider=rider)


def _sgu_bwd(name, proj, b_in, ln_g, ln_b, w_s, bias_full, dout, rider=None):
    t = proj.shape[0]

    def fn(i, r, _):
        (u, v), vjp = jax.vjp(_sgu_pre, r[0][...], r[1][...], r[2][...], r[3][...], r[4][...], r[5][...])
        vb = v.astype(BF)
        d = r[8][...]
        masks = [_causal(r[6], h).astype(BF) for h in range(HEADS)]
        cols = [slice(h * CHUNK, (h + 1) * CHUNK) for h in range(HEADS)]
        mixed = jnp.concatenate([jnp.dot(masks[h], vb[:, cols[h]], preferred_element_type=F32) for h in range(HEADS)], axis=1)
        du = d * (mixed + r[7][...])
        dmix = d * u
        dmb = dmix.astype(BF)
        dv = jnp.concatenate([lax.dot_general(masks[h], dmb[:, cols[h]], TN, preferred_element_type=F32) for h in range(HEADS)], axis=1)
        rows = lax.broadcasted_iota(jnp.int32, (CHUNK, CHUNK), 0)
        lanes = lax.broadcasted_iota(jnp.int32, (CHUNK, CHUNK), 1)
        dws = jnp.stack([jnp.where(lanes <= rows, lax.dot_general(dmb[:, cols[h]], vb[:, cols[h]], NT, preferred_element_type=F32), 0.0)
                         for h in range(HEADS)])
        dbs = jnp.zeros((CHUNK, CHUNK), F32)
        for h in range(HEADS):
            dbs = dbs + jnp.where(lanes == h, jnp.sum(dmix[:, cols[h]], axis=1, keepdims=True), 0.0)
        dup, dvp, dbu, dbv, dg, db = vjp((du, dv))
        return [dup, dvp, dbu, dbv, dg, db, dws, dbs]

    return _rowcall(
        name, fn, [proj, proj, b_in, b_in, ln_g, ln_b, w_s, bias_full, dout],
        [_rows(CHUNK, D, 0), _rows(CHUNK, D, 1), pl.BlockSpec((1, D), lambda i: (0, 0)), pl.BlockSpec((1, D), lambda i: (0, 1)),
         _whole((1, D)), _whole((1, D)), _whole((HEADS, CHUNK, CHUNK)), _whole((CHUNK, D)), _rows(CHUNK)],
        2, [jax.ShapeDtypeStruct((t, D), BF)] * 2 + [_vec()] * 4
        + [jax.ShapeDtypeStruct((HEADS, CHUNK, CHUNK), F32), jax.ShapeDtypeStruct((CHUNK, CHUNK), F32)],
        [_rows(CHUNK)] * 2 + [_whole((1, D))] * 4 + [_whole((HEADS, CHUNK, CHUNK)), _whole((CHUNK, CHUNK))], t // CHUNK, rider=rider)


def _halo_before(tr, cb):
    return pl.BlockSpec((HALO, D), lambda i: (jnp.maximum(i * (tr // HALO) - 1, 0), cb))


def _halo_after(tr, cb, n_tiles):
    return pl.BlockSpec((HALO, D), lambda i: (jnp.minimum((i + 1) * (tr // HALO), n_tiles * (tr // HALO) - 1), cb))


def _ln_silu(z, g, b):
    return _silu(_layer_norm(z, g, b))


def _conv(name, proj, b_in, conv_w, conv_b, ln_g, ln_b, rider=None):
    t = proj.shape[0]
    tr = min(t, 256)

    def fn(i, r, scr):
        zbuf = scr[0]
        bv, bg = r[4][...], r[5][...]
        z0 = (r[0][...] + bv) * jax.nn.sigmoid(r[1][...] + bg)
        before = (r[2][...] + bv) * jax.nn.sigmoid(r[3][...] + bg)
        zbuf[pl.ds(0, HALO), :] = jnp.where(i > 0, before, 0.0)
        zbuf[pl.ds(HALO, tr), :] = z0
        acc = jnp.zeros((tr, D), F32) + r[7][...]
        for k in range(KW):
            acc = acc + r[6][k:k + 1, :] * zbuf[pl.ds(HALO - (KW - 1) + k, tr), :]
        return [z0, acc, _ln_silu(acc, r[8][...], r[9][...])]

    return _rowcall(
        name, fn, [proj, proj, proj, proj, b_in, b_in, conv_w, conv_b, ln_g, ln_b],
        [_rows(tr, D, 2), _rows(tr, D, 3), _halo_before(tr, 2), _halo_before(tr, 3),
         pl.BlockSpec((1, D), lambda i: (0, 2)), pl.BlockSpec((1, D), lambda i: (0, 3)),
         _whole((HALO, D)), _whole((1, D)), _whole((1, D)), _whole((1, D))],
        3, [jax.ShapeDtypeStruct((t, D), F32), jax.ShapeDtypeStruct((t, D), F32), jax.ShapeDtypeStruct((t, D), BF)],
        [_rows(tr)] * 3, t // tr, [pltpu.VMEM((tr + HALO, D), F32)], rider=rider)


def _conv_bwd(name, proj, b_in, conv_w, ln_g, ln_b, z0, z1, dz3, rider=None):
    t = proj.shape[0]
    tr = min(t, 256)
    n_tiles = t // tr

    def fn(i, r, scr):
        zbuf, dbuf = scr
        g, b = r[5][...], r[6][...]
        _, vjp = jax.vjp(_ln_silu, r[9][...], g, b)
        dz1, dg, db = vjp(r[11][...])
        _, vjp_after = jax.vjp(_ln_silu, r[10][...], g, b)
        dz1_after = vjp_after(r[12][...])[0]
        dbuf[pl.ds(0, tr), :] = dz1
        dbuf[pl.ds(tr, HALO), :] = jnp.where(i < n_tiles - 1, dz1_after, 0.0)
        zbuf[pl.ds(0, HALO), :] = jnp.where(i > 0, r[8][...], 0.0)
        zbuf[pl.ds(HALO, tr), :] = r[7][...]
        dz0 = jnp.zeros((tr, D), F32)
        dw_rows = []
        for k in range(KW):
            dz0 = dz0 + r[4][k:k + 1, :] * dbuf[pl.ds(KW - 1 - k, tr), :]
            dw_rows.append(jnp.sum(dz1 * zbuf[pl.ds(HALO - (KW - 1) + k, tr), :], axis=0, keepdims=True))
        dw_rows.append(jnp.zeros((HALO - KW, D), F32))
        a = r[0][...] + r[2][...]
        s = jax.nn.sigmoid(r[1][...] + r[3][...])
        dcv = dz0 * s
        dcg = dz0 * a * s * (1.0 - s)
        return [dcv, dcg, jnp.sum(dcv, axis=0, keepdims=True), jnp.sum(dcg, axis=0, keepdims=True),
                jnp.concatenate(dw_rows, axis=0), jnp.sum(dz1, axis=0, keepdims=True), dg, db]

    return _rowcall(
        name, fn, [proj, proj, b_in, b_in, conv_w, ln_g, ln_b, z0, z0, z1, z1, dz3, dz3],
        [_rows(tr, D, 2), _rows(tr, D, 3), pl.BlockSpec((1, D), lambda i: (0, 2)), pl.BlockSpec((1, D), lambda i: (0, 3)),
         _whole((HALO, D)), _whole((1, D)), _whole((1, D)),
         _rows(tr), _halo_before(tr, 0), _rows(tr), _halo_after(tr, 0, n_tiles), _rows(tr), _halo_after(tr, 0, n_tiles)],
        2, [jax.ShapeDtypeStruct((t, D), BF)] * 2 + [_vec(), _vec(), _vec(HALO), _vec(), _vec(), _vec()],
        [_rows(tr)] * 2 + [_whole((1, D))] * 2 + [_whole((HALO, D))] + [_whole((1, D))] * 3, n_tiles,
        [pltpu.VMEM((tr + HALO, D), F32), pltpu.VMEM((tr + HALO, D), F32)], rider=rider)


def _merge_fn(ga, gb, bga, bgb, ya, yb):
    return jax.nn.sigmoid(ga + bga) * ya + jax.nn.sigmoid(gb + bgb) * yb


def _merge(name, proj, b_in, ya, yb, rider=None):
    t = proj.shape[0]
    tr = min(t, 256)

    def fn(i, r, _):
        return [_merge_fn(*[x[...] for x in r])]

    return _rowcall(
        name, fn, [proj, proj, b_in, b_in, ya, yb],
        [_rows(tr, D, 4), _rows(tr, D, 5), pl.BlockSpec((1, D), lambda i: (0, 4)), pl.BlockSpec((1, D), lambda i: (0, 5)),
         _rows(tr), _rows(tr)],
        1, [jax.ShapeDtypeStruct((t, D), BF)], [_rows(tr)], t // tr, rider=rider)


def _merge_bwd(name, proj, b_in, ya, yb, dm):
    t = proj.shape[0]
    tr = min(t, 256)

    def fn(i, r, _):
        _, vjp = jax.vjp(_merge_fn, *[x[...] for x in r[:6]])
        dga, dgb, dbga, dbgb, dya, dyb = vjp(r[6][...])
        return [dga, dgb, dya, dyb, dbga, dbgb]

    return _rowcall(
        name, fn, [proj, proj, b_in, b_in, ya, yb, dm],
        [_rows(tr, D, 4), _rows(tr, D, 5), pl.BlockSpec((1, D), lambda i: (0, 4)), pl.BlockSpec((1, D), lambda i: (0, 5)),
         _rows(tr), _rows(tr), _rows(tr)],
        4, [jax.ShapeDtypeStruct((t, D), BF)] * 4 + [_vec(), _vec()], [_rows(tr)] * 4 + [_whole((1, D))] * 2, t // tr)


def _loss_head(name, x, gain, target):
    t = x.shape[0]
    tr = min(t, 256)

    def loss_fn(xv, g, tgt):
        y = xv * lax.rsqrt(jnp.mean(xv * xv, axis=-1, keepdims=True) + EPS) * g
        return 0.5 * jnp.sum(jnp.mean(jnp.square(y - tgt), axis=-1))

    def fn(i, r, _):
        loss, vjp = jax.vjp(loss_fn, r[0][...], r[1][...], r[2][...])
        dx, dg, _ = vjp(jnp.ones((), F32))
        return [dx, dg, jnp.zeros((1, D), F32) + loss]

    return _rowcall(name, fn, [x, gain, target], [_rows(tr), _whole((1, D)), _rows(tr)], 1,
                    [jax.ShapeDtypeStruct((t, D), F32), _vec(), _vec()], [_rows(tr), _whole((1, D)), _whole((1, D))], t // tr)


def _adamw(w, g, m, v):
    m = B1 * m + (1.0 - B1) * g
    v = B2 * v + (1.0 - B2) * jnp.square(g)
    m_hat = m / BC1
    v_hat = v / BC2
    delta = -LR * (m_hat / (jnp.sqrt(v_hat) + ADAM_EPS) + WD * w)
    return delta, m, v


def _ada_fwd(name, c_all, ada_w, ada_b):
    nc = ada_w.shape[1]

    def body(c_ref, w_ref, b_ref, o_ref):
        o_ref[...] = jnp.dot(_silu(c_ref[...]), w_ref[...], preferred_element_type=F32) + b_ref[...]

    return pl.pallas_call(body, name=name, out_shape=jax.ShapeDtypeStruct((NDEV, nc), F32),
                          compiler_params=_params(None))(c_all, ada_w, ada_b)


ADAMW_ROWS = 64


def _adamw_group(name, items, rider=None):
    ins, in_specs, out_shapes, out_specs, plan = [], [], [], [], []
    first = 0
    for chip_sum, received, w, m, v in items:
        r, c = w.shape
        tr = min(r, ADAMW_ROWS)
        n = r // tr

        def tile(i, first=first, n=n):
            return jnp.clip(i - first, 0, n - 1)

        spec = pl.BlockSpec((tr, c), lambda i, tile=tile: (tile(i), 0))
        ins += [chip_sum, *received, w, m, v]
        in_specs += [pl.BlockSpec((None, tr, c), lambda i, tile=tile: (0, tile(i), 0))]
        in_specs += [pl.BlockSpec((g.shape[0], tr, c), lambda i, tile=tile: (0, tile(i), 0)) for g in received]
        in_specs += [spec] * 3
        out_shapes += [jax.ShapeDtypeStruct((r, c), F32)] * 4
        out_specs += [spec] * 4
        plan.append((first, n, [g.shape[0] for g in received]))
        first += n

    def compute(in_refs, out_refs, _):
        i = pl.program_id(0)
        at_in = at_out = 0
        for start, n, counts in plan:
            mine = in_refs[at_in:at_in + 4 + len(counts)]
            outs = out_refs[at_out:at_out + 4]
            at_in += 4 + len(counts)
            at_out += 4

            @pl.when(jnp.logical_and(i >= start, i < start + n))
            def _(mine=mine, outs=outs, counts=counts):
                g = mine[0][...].astype(F32)
                for j, count in enumerate(counts):
                    for s in range(count):
                        g = g + mine[1 + j][s].astype(F32)
                delta, m_new, v_new = _adamw(mine[-3][...], g, mine[-2][...], mine[-1][...])
                for o, val in zip(outs, (g, delta, m_new, v_new)):
                    o[...] = val

    res = _call(name, compute, (first,), ins, in_specs, out_shapes, out_specs, [], ("arbitrary",), rider)
    outs, rode = res if rider else (res, [])
    return [outs[4 * j:4 * j + 4] for j in range(len(items))], rode


def _adamw_small(name, gathered, w, m, v):
    r = w.shape[0]

    def body(g_ref, w_ref, m_ref, v_ref, go_ref, d_ref, mo_ref, vo_ref):
        g = g_ref[pl.ds(0, r), :]
        for s in range(1, NDEV):
            g = g + g_ref[pl.ds(s * r, r), :]
        delta, m_new, v_new = _adamw(w_ref[...], g, m_ref[...], v_ref[...])
        go_ref[...] = g
        d_ref[...] = delta
        mo_ref[...] = m_new
        vo_ref[...] = v_new

    return pl.pallas_call(body, name=name, out_shape=[jax.ShapeDtypeStruct((r, D), F32)] * 4,
                          compiler_params=_params(None))(gathered, w, m, v)


def _adamw_ada(name, c_all_t, dmod, w, m, v):
    r, c = w.shape
    tr = 256

    def fn(i, refs, _):
        ca = _silu(refs[0][...])
        g = ca[:, 0:1] * refs[1][0:1, :]
        for b in range(1, NDEV):
            g = g + ca[:, b:b + 1] * refs[1][b:b + 1, :]
        delta, m_new, v_new = _adamw(refs[2][...], g, refs[3][...], refs[4][...])
        return [g, delta, m_new, v_new]

    spec = pl.BlockSpec((tr, c), lambda i: (i, 0))
    return _rowcall(name, fn, [c_all_t, dmod, w, m, v],
                    [pl.BlockSpec((tr, NDEV), lambda i: (i, 0)), pl.BlockSpec((NDEV, c), lambda i: (0, 0)), spec, spec, spec], 4,
                    [jax.ShapeDtypeStruct((r, c), F32)] * 4, [spec] * 4, r // tr)


def _ffn_fwd(tag, x, gain, sh, sc, g, wg, wu, wd_shard, down_rider):
    t = x.shape[0]
    tm = min(t, 512)
    h = _norm_mod(f"{tag}_norm", x, gain, sc, sh)
    (gate, up, act), (wd,) = _ffn_up(f"{tag}_up", h, wg, wu, rider=_gather_rider([wd_shard]))

    def epilogue(f, xv, gv):
        return xv + 0.5 * gv * f, f

    res = _mm_nn(f"{tag}_down", act, wd.reshape(F, D), tm, D, 1024, extras=(x, g),
                 extra_specs=(pl.BlockSpec((tm, D), lambda i, j, k: (i, 0)), pl.BlockSpec((1, D), lambda i, j, k: (0, 0))),
                 epilogue=epilogue, out_dtypes=(F32, BF), rider=down_rider)
    (x_out, f), rode = res if down_rider else (res, None)
    return x_out, (x, h, gate, up, act, f), wd, rode


def _ffn_bwd(tag, dx_out, saved, gain, sh, sc, g, wg, wu, wd, slots, dact_rider=None, dwd_rider=None):
    x, h, gate, up, act, f = saved
    t = x.shape[0]
    tm = min(t, 1024)
    df, dg = _gate_bwd(f"{tag}_gate_bwd", dx_out, f, g, 0.5)

    def act_bwd(da, gv, uv):
        gv = gv.astype(F32)
        s = jax.nn.sigmoid(gv)
        return da * uv.astype(F32) * (s * (1.0 + gv * (1.0 - s))), da * (gv * s)

    blk = pl.BlockSpec((tm, F // NDEV), lambda i, j, k: (i, j))
    res = _mm_nt(f"{tag}_dact", df, wd.reshape(F, D), tm, F // NDEV, out_dtypes=(BF, BF),
                 extras=(gate, up), extra_specs=(blk, blk), epilogue=act_bwd, rider=dact_rider)
    (dgate, dup), rode_dact = res if dact_rider else (res, [])
    res = _mm_tn(f"{tag}_dwd", act, df, 512, D, rider=dwd_rider)
    dwd, rode_dwd = res if dwd_rider else (res, [])
    dwd = dwd.reshape(NDEV, F // NDEV, D)
    (dwg, dwu), (sib_d,) = _dw_gate_up(f"{tag}_dwgu", h, dgate, dup, rider=_pair_rider([dwd]))
    (sum_d,) = _pair_add(f"{tag}_dwd_add", [dwd], [sib_d], slots)
    dh, (sib_g, sib_u, got_d) = _mm_nt_blocked(f"{tag}_dh", [dgate, dup], [wg, wu], tm,
                                               rider=[_pair_rider([dwg, dwu]), _chip_rider([sum_d])])
    sum_g, sum_u = _pair_add(f"{tag}_dwgu_add", [dwg, dwu], [sib_g, sib_u], slots)
    dx, dgain, dsc, dsh = _norm_mod_bwd(f"{tag}_norm_bwd", x, gain, sc, sh, dh, dx_out)
    return dx, (sum_d, [got_d]), sum_g, sum_u, (dgain, dsc, dsh, dg), rode_dact, rode_dwd


def kernel(x, c, ada_w, ada_b, norm_ffn1, ffn1_w_gate, ffn1_w_up, ffn1_w_down, norm_mix, mix_w_in, mix_b_in, sgu_ln_g, sgu_ln_b, sgu_w_s, sgu_b_s, conv_w, conv_b, conv_ln_g, conv_ln_b, w_branch_a, w_branch_b, w_out, norm_ffn2, ffn2_w_gate, ffn2_w_up, ffn2_w_down, norm_final, loss_target, m_ada_w, m_ada_b, m_norm_ffn1, m_ffn1_w_gate, m_ffn1_w_up, m_ffn1_w_down, m_norm_mix, m_mix_w_in, m_mix_b_in, m_sgu_ln_g, m_sgu_ln_b, m_sgu_w_s, m_sgu_b_s, m_conv_w, m_conv_b, m_conv_ln_g, m_conv_ln_b, m_w_branch_a, m_w_branch_b, m_w_out, m_norm_ffn2, m_ffn2_w_gate, m_ffn2_w_up, m_ffn2_w_down, m_norm_final, v_ada_w, v_ada_b, v_norm_ffn1, v_ffn1_w_gate, v_ffn1_w_up, v_ffn1_w_down, v_norm_mix, v_mix_w_in, v_mix_b_in, v_sgu_ln_g, v_sgu_ln_b, v_sgu_w_s, v_sgu_b_s, v_conv_w, v_conv_b, v_conv_ln_g, v_conv_ln_b, v_w_branch_a, v_w_branch_b, v_w_out, v_norm_ffn2, v_ffn2_w_gate, v_ffn2_w_up, v_ffn2_w_down, v_norm_final):
    mx, my, mc = _position()
    me = 4 * mx + 2 * my + mc
    chip = 2 * mx + my
    slots = jnp.stack([2 * (chip ^ k) + mc for k in range(N_CHIPS)]).astype(jnp.int32)
    t = x.shape[1]
    tm = min(t, 1024)
    x0 = x.reshape(t, D)
    target = loss_target.reshape(t, D)
    given = dict(ffn1_w_gate=(ffn1_w_gate, m_ffn1_w_gate, v_ffn1_w_gate), ffn1_w_up=(ffn1_w_up, m_ffn1_w_up, v_ffn1_w_up),
                 ffn1_w_down=(ffn1_w_down, m_ffn1_w_down, v_ffn1_w_down), mix_w_in=(mix_w_in, m_mix_w_in, v_mix_w_in),
                 w_branch_a=(w_branch_a, m_w_branch_a, v_w_branch_a), w_branch_b=(w_branch_b, m_w_branch_b, v_w_branch_b),
                 w_out=(w_out, m_w_out, v_w_out), ffn2_w_gate=(ffn2_w_gate, m_ffn2_w_gate, v_ffn2_w_gate),
                 ffn2_w_up=(ffn2_w_up, m_ffn2_w_up, v_ffn2_w_up), ffn2_w_down=(ffn2_w_down, m_ffn2_w_down, v_ffn2_w_down))
    shard = {n: wmv[0][0].astype(BF) for n, wmv in given.items()}

    small_in = jnp.concatenate([c.reshape(8, CHUNK), conv_w[0], jnp.zeros((1, CHUNK), F32)], axis=0)
    small_all = _allgather_vmem("gather_c_conv", small_in).reshape(NDEV, 40, CHUNK)
    c_all = small_all[:, :8, :].reshape(NDEV, D)
    conv_w_full = jnp.transpose(small_all[:, 8:, :], (1, 0, 2)).reshape(HALO, D)
    ada_cols = N_MOD * D // NDEV
    mod_part = _ada_fwd("ada_fwd", c_all, ada_w[0], lax.dynamic_slice(ada_b, (0, me * ada_cols), (1, ada_cols)))
    mod_all = _allgather_vmem("gather_mod", mod_part).reshape(NDEV, NDEV, ada_cols)
    mod = lax.dynamic_index_in_dim(mod_all, me, axis=1, keepdims=False).reshape(N_MOD, 1, D)
    sh1, sc1, g1, sh2, sc2, g2, sh3, sc3, g3 = [mod[i] for i in range(N_MOD)]
    wg1, wu1 = _exchange("gather_ffn1", _gather_rider([shard["ffn1_w_gate"], shard["ffn1_w_up"]]))

    x1, saved1, wd1, (w_in,) = _ffn_fwd("ffn1", x0, norm_ffn1, sh1, sc1, g1, wg1, wu1, shard["ffn1_w_down"],
                                         _gather_rider([shard["mix_w_in"]]))
    h2 = _norm_mod("mix_norm", x1, norm_mix, sc2, sh2)
    proj, (wg2,) = _mm_nn_blocked("mix_in", h2, w_in, tm, rider=_gather_rider([shard["ffn2_w_gate"]]))
    bias_full = jnp.repeat(sgu_b_s[0].T, CHUNK, axis=1)
    (ua,), (wa3, wb3) = _sgu("sgu", proj, mix_b_in, sgu_ln_g, sgu_ln_b, sgu_w_s[0], bias_full,
                             rider=_gather_rider([shard["w_branch_a"], shard["w_branch_b"]]))
    (z0, z1, z3), (wu2,) = _conv("conv", proj, mix_b_in, conv_w_full, conv_b, conv_ln_g, conv_ln_b,
                                 rider=_gather_rider([shard["ffn2_w_up"]]))
    wa, wb = wa3.reshape(D, D), wb3.reshape(D, D)
    ya = _mm_nn("branch_a", ua, wa, tm, 512, D)[0]
    yb = _mm_nn("branch_b", z3, wb, tm, 512, D)[0]
    (merged,), (wo3,) = _merge("merge", proj, mix_b_in, ya, yb, rider=_gather_rider([shard["w_out"]]))
    wo = wo3.reshape(D, D)

    def mix_epilogue(yv, xv, gv):
        return xv + gv * yv, yv

    tmo = min(t, 512)
    x2, y = _mm_nn("mix_out", merged, wo, tmo, D, D, extras=(x1, g2),
                   extra_specs=(pl.BlockSpec((tmo, D), lambda i, j, k: (i, 0)), pl.BlockSpec((1, D), lambda i, j, k: (0, 0))),
                   epilogue=mix_epilogue, out_dtypes=(F32, BF))
    x3, saved3, wd2, _ = _ffn_fwd("ffn2", x2, norm_ffn2, sh3, sc3, g3, wg2, wu2, shard["ffn2_w_down"], None)

    norm_final2 = norm_final.reshape(1, D)
    dx3, d_norm_final, loss_row = _loss_head("loss_head", x3, norm_final2, target)
    dx2, down2, sum_g2, sum_u2, (d_norm_ffn2, dsc3, dsh3, dg3), _, _ = _ffn_bwd(
        "ffn2", dx3, saved3, norm_ffn2, sh3, sc3, g3, wg2, wu2, wd2, slots)
    dy, dg2 = _gate_bwd("mix_gate_bwd", dx2, y, g2, 1.0)
    dm = _mm_nt("mix_out_bwd", dy, wo, tm, 512)[0]
    dwo = _mm_tn("mix_dwo", merged, dy, 512, D).reshape(NDEV, D // NDEV, D)
    dga, dgb, dya, dyb, db_ga, db_gb = _merge_bwd("merge_bwd", proj, mix_b_in, ya, yb, dm)
    dua = _mm_nt("branch_a_bwd", dya, wa, tm, 512)[0]
    dwa = _mm_tn("branch_dwa", ua, dya, 512, D).reshape(NDEV, D // NDEV, D)
    dz3 = _mm_nt("branch_b_bwd", dyb, wb, tm, 512)[0]
    dwb = _mm_tn("branch_dwb", z3, dyb, 512, D).reshape(NDEV, D // NDEV, D)
    (dup, dvp, db_u, db_v, d_sgu_g, d_sgu_b, d_ws, d_bs_t), sib_abo = _sgu_bwd(
        "sgu_bwd", proj, mix_b_in, sgu_ln_g, sgu_ln_b, sgu_w_s[0], bias_full, dua, rider=_pair_rider([dwa, dwb, dwo]))
    sum_a, sum_b, sum_o = _pair_add("mix_dw_add", [dwa, dwb, dwo], sib_abo, slots)
    (dcv, dcg, db_cv, db_cg, d_cw, d_cb, d_cln_g, d_cln_b), (got_g2, got_u2) = _conv_bwd(
        "conv_bwd", proj, mix_b_in, conv_w_full, conv_ln_g, conv_ln_b, z0, z1, dz3, rider=_chip_rider([sum_g2, sum_u2]))
    dproj = jnp.concatenate([dup, dvp, dcv, dcg, dga, dgb], axis=1)
    dwin, (got_a, got_b, got_o) = _mm_tn_blocked("mix_dwin", h2, dproj, rider=_chip_rider([sum_a, sum_b, sum_o]))
    dh2, (sib_in,) = _mm_nt_blocked("mix_in_bwd", [dproj], [w_in], tm, rider=_pair_rider([dwin]))
    (sum_in,) = _pair_add("mix_dwin_add", [dwin], [sib_in], slots)
    dx1, d_norm_mix, dsc2, dsh2 = _norm_mod_bwd("mix_norm_bwd", x1, norm_mix, sc2, sh2, dh2, dx2)
    dx0, down1, sum_g1, sum_u1, (d_norm_ffn1, dsc1, dsh1, dg1), (got_in_near,), (got_in_far,) = _ffn_bwd(
        "ffn1", dx1, saved1, norm_ffn1, sh1, sc1, g1, wg1, wu1, wd1, slots,
        dact_rider=_chip_rider([sum_in], NEIGHBOURS), dwd_rider=_chip_rider([sum_in], DIAGONAL))

    d_bs = jnp.transpose(d_bs_t[:, :HEADS])
    pack_rows = [dsh1, dsc1, dg1, dsh2, dsc2, dg2, dsh3, dsc3, dg3,
                 d_norm_ffn1, d_norm_mix, d_norm_ffn2, d_norm_final,
                 db_u, db_v, db_cv, db_cg, db_ga, db_gb,
                 d_sgu_g, d_sgu_b, d_bs.reshape(1, D), d_cb, d_cln_g, d_cln_b,
                 d_ws.reshape(CHUNK, D), d_cw[:KW], loss_row, jnp.zeros((R_TOTAL - R_LOSS - 1, D), F32)]
    packed = jnp.concatenate(pack_rows, axis=0)
    grads = dict(ffn2_w_gate=(sum_g2, [got_g2]), ffn2_w_up=(sum_u2, [got_u2]), ffn2_w_down=down2,
                 mix_w_in=(sum_in, [got_in_near, got_in_far]), w_branch_a=(sum_a, [got_a]), w_branch_b=(sum_b, [got_b]),
                 w_out=(sum_o, [got_o]), ffn1_w_down=down1)
    done, (got_g1, got_u1, packed_all) = _adamw_group(
        "adamw_most", [(cs, got, *[a[0] for a in given[n]]) for n, (cs, got) in grads.items()],
        rider=[_chip_rider([sum_g1, sum_u1]), _gather_rider([packed])])
    last, _ = _adamw_group("adamw_ffn1_in", [(sum_g1, [got_g1], *[a[0] for a in given["ffn1_w_gate"]]),
                                            (sum_u1, [got_u1], *[a[0] for a in given["ffn1_w_up"]])])
    big_out = {n: [o.reshape(given[n][0].shape) for o in outs]
               for n, outs in zip([*grads, "ffn1_w_gate", "ffn1_w_up"], [*done, *last])}
    packed_all = packed_all.reshape(NDEV * R_TOTAL, D)

    col0 = me * CHUNK

    def pack(ada_b_, n1, nm, n2, nf, b_in, lg, lb, bs, cb, clg, clb, ws, cw, fill):
        cw_full = lax.dynamic_update_slice(jnp.full((KW, D), fill, F32), cw[0], (0, col0))
        return jnp.concatenate([ada_b_.reshape(N_MOD, D), n1, nm, n2, nf.reshape(1, D), b_in.reshape(6, D), lg, lb, bs.reshape(1, D),
                                cb, clg, clb, ws.reshape(CHUNK, D), cw_full, jnp.full((R_TOTAL - R_LOSS, D), fill, F32)], axis=0)

    w_small = pack(ada_b, norm_ffn1, norm_mix, norm_ffn2, norm_final, mix_b_in, sgu_ln_g, sgu_ln_b, sgu_b_s, conv_b,
                   conv_ln_g, conv_ln_b, sgu_w_s, conv_w, 0.0)
    m_small = pack(m_ada_b, m_norm_ffn1, m_norm_mix, m_norm_ffn2, m_norm_final, m_mix_b_in, m_sgu_ln_g, m_sgu_ln_b, m_sgu_b_s,
                   m_conv_b, m_conv_ln_g, m_conv_ln_b, m_sgu_w_s, m_conv_w, 0.0)
    v_small = pack(v_ada_b, v_norm_ffn1, v_norm_mix, v_norm_ffn2, v_norm_final, v_mix_b_in, v_sgu_ln_g, v_sgu_ln_b, v_sgu_b_s,
                   v_conv_b, v_conv_ln_g, v_conv_ln_b, v_sgu_w_s, v_conv_w, 1.0)
    small_out = _adamw_small("adamw_small", packed_all, w_small, m_small, v_small)

    def unpack(p):
        return dict(
            ada_b=p[R_ADA_B:R_NORMS].reshape(1, N_MOD * D), norm_ffn1=p[9:10], norm_mix=p[10:11], norm_ffn2=p[11:12],
            norm_final=p[12], mix_b_in=p[R_BIN:R_SGU].reshape(1, D_IN), sgu_ln_g=p[19:20], sgu_ln_b=p[20:21],
            sgu_b_s=p[21].reshape(1, HEADS, CHUNK), conv_b=p[22:23], conv_ln_g=p[23:24], conv_ln_b=p[24:25],
            sgu_w_s=p[R_WS:R_CW].reshape(1, HEADS, CHUNK, CHUNK),
            conv_w=lax.dynamic_slice(p[R_CW:R_LOSS], (0, col0), (KW, CHUNK)).reshape(1, KW, CHUNK))

    small = [unpack(p) for p in small_out]
    loss = small_out[0][R_LOSS, 0]

    dmod_all = packed_all.reshape(NDEV, R_TOTAL, D)[:, :N_MOD, :].reshape(NDEV, N_MOD * D)
    dmod_cols = lax.dynamic_slice(dmod_all, (0, me * ada_cols), (NDEV, ada_cols))
    ada_out = [o.reshape(ada_w.shape) for o in _adamw_ada("adamw_ada_w", jnp.transpose(c_all), dmod_cols, ada_w[0], m_ada_w[0], v_ada_w[0])]

    order = ["ada_w", "ada_b", "norm_ffn1", "ffn1_w_gate", "ffn1_w_up", "ffn1_w_down", "norm_mix", "mix_w_in", "mix_b_in",
             "sgu_ln_g", "sgu_ln_b", "sgu_w_s", "sgu_b_s", "conv_w", "conv_b", "conv_ln_g", "conv_ln_b", "w_branch_a",
             "w_branch_b", "w_out", "norm_ffn2", "ffn2_w_gate", "ffn2_w_up", "ffn2_w_down", "norm_final"]

    def leaf(n, kind):
        if n == "ada_w":
            return ada_out[kind]
        if n in big_out:
            return big_out[n][kind]
        return small[kind][n]

    return (loss, dx0.reshape(x.shape), *[leaf(n, kind) for kind in range(4) for n in order])
```

```python
import jax
import jax.numpy as jnp
from jax import lax
from jax.experimental import pallas as pl
from jax.experimental.pallas import tpu as pltpu

D = 1024
F = 4 * D
D_IN = 6 * D
HEADS = 8
CHUNK = 128
KW = 31
HALO = 32
N_MOD = 9
NDEV = 8
N_CHIPS = 4
EPS = 1e-6
LR, B1, B2, ADAM_EPS, WD, STEP = 0.001, 0.9, 0.999, 1e-08, 0.01, 10
BC1 = 1.0 - B1 ** STEP
BC2 = 1.0 - B2 ** STEP
VMEM_LIMIT = 56 * 1024 * 1024
MESH = pl.DeviceIdType.MESH
HBM = pl.BlockSpec(memory_space=pltpu.HBM)
VMEM = pl.BlockSpec(memory_space=pltpu.VMEM)
BF = jnp.bfloat16
F32 = jnp.float32

NN = (((1,), (0,)), ((), ()))
NT = (((1,), (1,)), ((), ()))
TN = (((0,), (0,)), ((), ()))

R_CW, R_LOSS, R_TOTAL = 25, 56, 64


def _params(sem):
    return pltpu.CompilerParams(dimension_semantics=sem, vmem_limit_bytes=VMEM_LIMIT)


def _position():
    return lax.axis_index("x"), lax.axis_index("y"), lax.axis_index("c")


def _flip(pos, k):
    x, y, c = pos
    return (x ^ (k >> 2 & 1), y ^ (k >> 1 & 1), c ^ (k & 1))


def _index(pos):
    return 4 * pos[0] + 2 * pos[1] + pos[2]


def _allgather_vmem(name, shard):
    m_per, n = shard.shape

    def body(x_ref, out_ref, send_sems, recv_sems, local_sem):
        x, y, c = _position()
        me, sibling = (x, y, c), (x, y, 1 - c)
        chips = [(1 - x, y), (x, 1 - y), (1 - x, 1 - y)]

        def rows(pos):
            return out_ref.at[pl.ds(_index(pos) * m_per, m_per), :]

        def copy(k, block, to, src=None):
            return pltpu.make_async_remote_copy(
                src_ref=rows(block) if src is None else src, dst_ref=rows(block),
                send_sem=send_sems.at[k], recv_sem=recv_sems.at[k], device_id=to, device_id_type=MESH)

        mine = pltpu.make_async_copy(x_ref, rows(me), local_sem)
        mine.start()
        first = [copy(0, me, sibling, src=x_ref)]
        first += [copy(1 + j, me, (*chip, c), src=x_ref) for j, chip in enumerate(chips)]
        for cp in first:
            cp.start()
        passed = [copy(4 + j, (*chip, c), sibling) for j, chip in enumerate(chips)]
        for j, chip in enumerate(chips):
            copy(1 + j, (*chip, c), me).wait_recv()
            passed[j].start()
        copy(0, sibling, me).wait_recv()
        for j, chip in enumerate(chips):
            copy(4 + j, (*chip, 1 - c), me).wait_recv()
        for cp in first + passed:
            cp.wait_send()
        mine.wait()

    return pl.pallas_call(
        body, name=name,
        out_shape=jax.ShapeDtypeStruct((NDEV * m_per, n), shard.dtype),
        in_specs=[VMEM], out_specs=VMEM,
        scratch_shapes=[pltpu.SemaphoreType.DMA((7,)), pltpu.SemaphoreType.DMA((7,)), pltpu.SemaphoreType.DMA],
    )(shard)


class _Rider:
    def __init__(self, ins, out_shapes, sems, start, finish, relay=None):
        self.ins, self.out_shapes, self.sems = list(ins), list(out_shapes), list(sems)
        self.start, self.finish, self.relay = start, finish, relay


def _gather_rider(shards):
    n = len(shards)

    def setup(ins, outs, sems):
        send_sems, recv_sems, local_sems = sems
        x, y, c = _position()
        me, sibling = (x, y, c), (x, y, 1 - c)
        chips = [(1 - x, y), (x, 1 - y), (1 - x, 1 - y)]

        def copy(a, k, block, to, own=False):
            slot = outs[a].at[_index(block)]
            return pltpu.make_async_remote_copy(
                src_ref=ins[a] if own else slot, dst_ref=slot,
                send_sem=send_sems.at[k, a], recv_sem=recv_sems.at[k, a], device_id=to, device_id_type=MESH)

        def own_copies():
            mine = [pltpu.make_async_copy(ins[a], outs[a].at[_index(me)], local_sems.at[a]) for a in range(n)]
            first = []
            for a in range(n):
                first.append(copy(a, 0, me, sibling, own=True))
                first += [copy(a, 1 + j, me, (*chip, c), own=True) for j, chip in enumerate(chips)]
            return mine, first

        return me, sibling, chips, c, copy, own_copies

    def start(ins, outs, sems):
        mine, first = setup(ins, outs, sems)[-1]()
        for cp in mine + first:
            cp.start()

    def relay(ins, outs, sems):
        me, sibling, chips, c, copy, _ = setup(ins, outs, sems)
        for j, chip in enumerate(chips):
            for a in range(n):
                copy(a, 1 + j, (*chip, c), me).wait_recv()
                copy(a, 4 + j, (*chip, c), sibling).start()

    def finish(ins, outs, sems):
        me, sibling, chips, c, copy, own_copies = setup(ins, outs, sems)
        mine, first = own_copies()
        passed = [copy(a, 4 + j, (*chip, c), sibling) for j, chip in enumerate(chips) for a in range(n)]
        for a in range(n):
            copy(a, 0, sibling, me).wait_recv()
            for j, chip in enumerate(chips):
                copy(a, 4 + j, (*chip, 1 - c), me).wait_recv()
        for cp in first + passed:
            cp.wait_send()
        for cp in mine:
            cp.wait()

    return _Rider(shards, [jax.ShapeDtypeStruct((NDEV, *s.shape), s.dtype) for s in shards],
                  [pltpu.SemaphoreType.DMA((7, n)), pltpu.SemaphoreType.DMA((7, n)), pltpu.SemaphoreType.DMA((n,))],
                  start, finish, relay)


def _pair_rider(parts):
    n = len(parts)

    def copies(ins, outs, sems):
        send_sems, recv_sems = sems
        x, y, c = _position()
        q = 2 * x + y
        return [pltpu.make_async_remote_copy(
            src_ref=ins[a].at[2 * (q ^ k) + (1 - c)], dst_ref=outs[a].at[k],
            send_sem=send_sems.at[k, a], recv_sem=recv_sems.at[k, a], device_id=(x, y, 1 - c), device_id_type=MESH)
            for a in range(n) for k in range(N_CHIPS)]

    def start(ins, outs, sems):
        for cp in copies(ins, outs, sems):
            cp.start()

    def finish(ins, outs, sems):
        for cp in copies(ins, outs, sems):
            cp.wait()

    return _Rider(parts, [jax.ShapeDtypeStruct((N_CHIPS, *p.shape[1:]), p.dtype) for p in parts],
                  [pltpu.SemaphoreType.DMA((N_CHIPS, n)), pltpu.SemaphoreType.DMA((N_CHIPS, n))], start, finish)


NEIGHBOURS = (1, 2)
DIAGONAL = (3,)
OTHER_CHIPS = NEIGHBOURS + DIAGONAL


def _chip_rider(sums, ks=OTHER_CHIPS):
    n = len(sums)

    def copies(ins, outs, sems):
        send_sems, recv_sems = sems
        me = _position()
        return [pltpu.make_async_remote_copy(
            src_ref=ins[a].at[k], dst_ref=outs[a].at[j],
            send_sem=send_sems.at[j, a], recv_sem=recv_sems.at[j, a], device_id=_flip(me, 2 * k), device_id_type=MESH)
            for a in range(n) for j, k in enumerate(ks)]

    def start(ins, outs, sems):
        for cp in copies(ins, outs, sems):
            cp.start()

    def finish(ins, outs, sems):
        for cp in copies(ins, outs, sems):
            cp.wait()

    return _Rider(sums, [jax.ShapeDtypeStruct((len(ks), *s.shape[1:]), s.dtype) for s in sums],
                  [pltpu.SemaphoreType.DMA((len(ks), n)), pltpu.SemaphoreType.DMA((len(ks), n))], start, finish)


def _grid_edge(grid, last):
    cond = None
    for d, n in enumerate(grid):
        here = pl.program_id(d) == (n - 1 if last else 0)
        cond = here if cond is None else jnp.logical_and(cond, here)
    return cond


def _call(name, compute, grid, ins, in_specs, out_shapes, out_specs, scratch_shapes, semantics, rider=None):
    riders = [rider] if isinstance(rider, _Rider) else list(rider or [])
    n_in, n_out, n_scr = len(ins), len(out_shapes), len(scratch_shapes)
    n_rin, n_rout, n_rsem = [sum(len(part(r)) for r in riders) for part in (lambda r: r.ins, lambda r: r.out_shapes, lambda r: r.sems)]
    cuts = [0, n_in, n_in + n_rin, n_in + n_rin + n_out, n_in + n_rin + n_out + n_rout, n_in + n_rin + n_out + n_rout + n_scr]

    def body(*refs):
        in_refs, rin_refs, out_refs, rout_refs, scr_refs = [refs[a:b] for a, b in zip(cuts[:-1], cuts[1:])]
        rsem_refs = refs[cuts[-1]:]
        mine, at = [], [0, 0, 0]
        for r in riders:
            mine.append((r, rin_refs[at[0]:at[0] + len(r.ins)], rout_refs[at[1]:at[1] + len(r.out_shapes)],
                         rsem_refs[at[2]:at[2] + len(r.sems)]))
            at = [at[0] + len(r.ins), at[1] + len(r.out_shapes), at[2] + len(r.sems)]
        if riders:
            @pl.when(_grid_edge(grid, last=False))
            def _():
                for r, a, b, c in mine:
                    r.start(a, b, c)

        if any(r.relay for r in riders):
            @pl.when(_grid_edge(grid, last=True))
            def _():
                for r, a, b, c in mine:
                    if r.relay:
                        r.relay(a, b, c)

        compute(in_refs, out_refs, scr_refs)
        if riders:
            @pl.when(_grid_edge(grid, last=True))
            def _():
                for r, a, b, c in mine:
                    r.finish(a, b, c)

    res = pl.pallas_call(
        body, name=name, grid=grid,
        out_shape=list(out_shapes) + [s for r in riders for s in r.out_shapes],
        in_specs=list(in_specs) + [HBM] * n_rin, out_specs=list(out_specs) + [HBM] * n_rout,
        scratch_shapes=list(scratch_shapes) + [s for r in riders for s in r.sems],
        compiler_params=_params(semantics),
    )(*ins, *[a for r in riders for a in r.ins])
    return (res[:n_out], res[n_out:]) if riders else res


def _exchange(name, rider):
    return _call(name, lambda *_: None, (1,), [], [], [], [], [], ("arbitrary",), rider)[1]


def _pair_add(name, parts, from_sibling, slots):
    n = len(parts)

    def body(s_ref, *refs):
        for a in range(n):
            refs[2 * n + a][...] = (refs[a][...].astype(F32) + refs[n + a][...].astype(F32)).astype(refs[2 * n + a].dtype)

    def slab(p, picked):
        _, r, c = p.shape
        return pl.BlockSpec((None, r, c), (lambda k, s: (s[k], 0, 0)) if picked else (lambda k, s: (k, 0, 0)))

    return pl.pallas_call(
        body, name=name,
        grid_spec=pltpu.PrefetchScalarGridSpec(
            num_scalar_prefetch=1, grid=(N_CHIPS,),
            in_specs=[slab(p, True) for p in parts] + [slab(p, False) for p in parts],
            out_specs=[slab(p, False) for p in parts]),
        out_shape=[jax.ShapeDtypeStruct((N_CHIPS, *p.shape[1:]), p.dtype) for p in parts],
        compiler_params=_params(("arbitrary",)),
    )(slots, *parts, *from_sibling)


def _mm(name, pairs, dims, grid, nk, out_shapes, out_specs, extras=(), extra_specs=(), epilogue=None, acc_shape=None, rider=None):
    n_pairs = len(pairs)

    def compute(ins, outs, scratch):
        def partial_sum():
            total = None
            for p in range(n_pairs):
                d = lax.dot_general(ins[2 * p][...], ins[2 * p + 1][...], dims, preferred_element_type=F32)
                total = d if total is None else total + d
            return total

        def finish(r):
            ex = [e[...] for e in ins[2 * n_pairs:]]
            res = epilogue(r, *ex) if epilogue is not None else (r,)
            for o, v in zip(outs, res):
                o[...] = v.astype(o.dtype)

        if nk == 1:
            finish(partial_sum())
        else:
            acc = scratch[0]
            k = pl.program_id(2)

            @pl.when(k == 0)
            def _():
                acc[...] = partial_sum()

            @pl.when(k > 0)
            def _():
                acc[...] += partial_sum()

            @pl.when(k == nk - 1)
            def _():
                finish(acc[...])

    operands, specs = [], []
    for a, a_spec, b, b_spec in pairs:
        operands += [a, b]
        specs += [a_spec, b_spec]
    return _call(name, compute, grid, operands + list(extras), specs + list(extra_specs), out_shapes, out_specs,
                 [pltpu.VMEM(acc_shape, F32)] if nk > 1 else [], ("parallel", "parallel", "arbitrary"), rider)


def _single(res, rider):
    return (res[0][0], res[1]) if rider else res[0]


def _silu(x):
    return x * jax.nn.sigmoid(x)


def _ffn_up(name, h, wg, wu, rider=None):
    t = h.shape[0]
    tm = min(t, 1024)
    nb = F // NDEV

    def compute(ins, outs, _):
        hv = ins[0][...]
        g = jnp.dot(hv, ins[1][...], preferred_element_type=F32)
        u = jnp.dot(hv, ins[2][...], preferred_element_type=F32)
        outs[0][...] = g.astype(BF)
        outs[1][...] = u.astype(BF)
        outs[2][...] = (_silu(g) * u).astype(BF)

    w_spec = pl.BlockSpec((None, D, nb), lambda i, j: (j, 0, 0))
    o_spec = pl.BlockSpec((tm, nb), lambda i, j: (i, j))
    return _call(name, compute, (t // tm, NDEV), [h, wg, wu], [pl.BlockSpec((tm, D), lambda i, j: (i, 0)), w_spec, w_spec],
                 [jax.ShapeDtypeStruct((t, F), BF)] * 3, [o_spec] * 3, [], ("parallel", "arbitrary"), rider)


def _mm_nn(name, a, b, tm, tn, tk, extras=(), extra_specs=(), epilogue=None, out_dtypes=(F32,), rider=None):
    m, kk = a.shape
    n = b.shape[1]
    nk = kk // tk
    return _mm(
        name, [(a, pl.BlockSpec((tm, tk), lambda i, j, k: (i, k)), b, pl.BlockSpec((tk, tn), lambda i, j, k: (k, j)))], NN,
        (m // tm, n // tn, nk), nk,
        [jax.ShapeDtypeStruct((m, n), dt) for dt in out_dtypes],
        [pl.BlockSpec((tm, tn), lambda i, j, k: (i, j))] * len(out_dtypes),
        extras, extra_specs, epilogue, (tm, tn), rider)


def _mm_nn_blocked(name, a, b3, tm, rider=None):
    m = a.shape[0]
    nb = b3.shape[2]
    return _single(_mm(
        name, [(a, pl.BlockSpec((tm, D), lambda i, j, k: (i, 0)), b3, pl.BlockSpec((None, D, nb), lambda i, j, k: (j, 0, 0)))], NN,
        (m // tm, NDEV, 1), 1,
        [jax.ShapeDtypeStruct((m, NDEV * nb), F32)], [pl.BlockSpec((tm, nb), lambda i, j, k: (i, j))], rider=rider), rider)


def _mm_nt(name, a, b, tm, tn, out_dtypes=(F32,), extras=(), extra_specs=(), epilogue=None, rider=None):
    m, kk = a.shape
    n = b.shape[0]
    return _mm(
        name, [(a, pl.BlockSpec((tm, kk), lambda i, j, k: (i, 0)), b, pl.BlockSpec((tn, kk), lambda i, j, k: (j, 0)))], NT,
        (m // tm, n // tn, 1), 1,
        [jax.ShapeDtypeStruct((m, n), dt) for dt in out_dtypes],
        [pl.BlockSpec((tm, tn), lambda i, j, k: (i, j))] * len(out_dtypes),
        extras, extra_specs, epilogue, rider=rider)


def _mm_nt_blocked(name, a_list, b3_list, tm, rider=None):
    m = a_list[0].shape[0]
    nb = b3_list[0].shape[2]
    pairs = [(a, pl.BlockSpec((tm, nb), lambda i, j, k: (i, k)), b3, pl.BlockSpec((None, D, nb), lambda i, j, k: (k, 0, 0)))
             for a, b3 in zip(a_list, b3_list)]
    return _single(_mm(name, pairs, NT, (m // tm, 1, NDEV), NDEV,
                       [jax.ShapeDtypeStruct((m, D), F32)], [pl.BlockSpec((tm, D), lambda i, j, k: (i, 0))],
                       acc_shape=(tm, D), rider=rider), rider)


def _mm_tn(name, a, b, tm, tn, rider=None):
    t, m = a.shape
    n = b.shape[1]
    return _single(_mm(
        name, [(a, pl.BlockSpec((t, tm), lambda i, j, k: (0, i)), b, pl.BlockSpec((t, tn), lambda i, j, k: (0, j)))], TN,
        (m // tm, n // tn, 1), 1,
        [jax.ShapeDtypeStruct((m, n), BF)], [pl.BlockSpec((tm, tn), lambda i, j, k: (i, j))], rider=rider), rider)


def _mm_tn_blocked(name, a, b, rider=None):
    t = a.shape[0]
    nb = b.shape[1] // NDEV
    return _single(_mm(
        name, [(a, pl.BlockSpec((t, D), lambda i, j, k: (0, 0)), b, pl.BlockSpec((t, nb), lambda i, j, k: (0, j)))], TN,
        (1, NDEV, 1), 1,
        [jax.ShapeDtypeStruct((NDEV, D, nb), BF)], [pl.BlockSpec((None, D, nb), lambda i, j, k: (j, 0, 0))], rider=rider), rider)


def _dw_gate_up(name, h, dgate, dup, rider=None):
    t = h.shape[0]
    nb = F // NDEV

    def compute(ins, outs, _):
        hv = ins[0][...]
        outs[0][...] = lax.dot_general(hv, ins[1][...], TN, preferred_element_type=F32).astype(BF)
        outs[1][...] = lax.dot_general(hv, ins[2][...], TN, preferred_element_type=F32).astype(BF)

    d_spec = pl.BlockSpec((t, nb), lambda j: (0, j))
    o_spec = pl.BlockSpec((None, D, nb), lambda j: (j, 0, 0))
    return _call(name, compute, (NDEV,), [h, dgate, dup], [pl.BlockSpec((t, D), lambda j: (0, 0)), d_spec, d_spec],
                 [jax.ShapeDtypeStruct((NDEV, D, nb), BF)] * 2, [o_spec] * 2, [], ("arbitrary",), rider)


def _rowcall(name, fn, ins, in_specs, n_row_out, out_shapes, out_specs, grid, scratch_shapes=(), rider=None):
    def accumulate(o, v, i):
        @pl.when(i == 0)
        def _():
            o[...] = v.astype(o.dtype)

        @pl.when(i > 0)
        def _():
            o[...] += v.astype(o.dtype)

    def compute(in_refs, out_refs, scr):
        i = pl.program_id(0)
        vals = fn(i, in_refs, scr)
        for idx, (o, v) in enumerate(zip(out_refs, vals)):
            if idx < n_row_out:
                o[...] = v.astype(o.dtype)
            else:
                accumulate(o, v, i)

    return _call(name, compute, (grid,), ins, in_specs, out_shapes, out_specs, list(scratch_shapes), ("arbitrary",), rider)


def _rows(tr, w=D, cb=0):
    return pl.BlockSpec((tr, w), lambda i: (i, cb))


def _whole(shape):
    nd = len(shape)
    return pl.BlockSpec(shape, lambda i: (0,) * nd)


def _vec(n=1):
    return jax.ShapeDtypeStruct((n, D), F32)


def _rms_mod(x, gain, sc, sh):
    y = x * lax.rsqrt(jnp.mean(x * x, axis=-1, keepdims=True) + EPS)
    return (y * gain) * (1.0 + sc) + sh


def _layer_norm(x, g, b):
    mu = jnp.mean(x, axis=-1, keepdims=True)
    var = jnp.mean(jnp.square(x - mu), axis=-1, keepdims=True)
    return (x - mu) * lax.rsqrt(var + EPS) * g + b


def _norm_mod(name, x, gain, sc, sh):
    t = x.shape[0]
    tr = min(t, 256)

    def fn(i, r, _):
        return [_rms_mod(r[0][...], r[1][...], r[2][...], r[3][...])]

    return _rowcall(name, fn, [x, gain, sc, sh], [_rows(tr)] + [_whole((1, D))] * 3, 1,
                    [jax.ShapeDtypeStruct((t, D), BF)], [_rows(tr)], t // tr)[0]


def _norm_mod_bwd(name, x, gain, sc, sh, dh, dres, rider=None):
    t = x.shape[0]
    tr = min(t, 256)

    def fn(i, r, _):
        _, vjp = jax.vjp(_rms_mod, r[0][...], r[1][...], r[2][...], r[3][...])
        dx, dgain, dsc, dsh = vjp(r[4][...])
        return [dx + r[5][...], dgain, dsc, dsh]

    return _rowcall(name, fn, [x, gain, sc, sh, dh, dres], [_rows(tr)] + [_whole((1, D))] * 3 + [_rows(tr)] * 2, 1,
                    [jax.ShapeDtypeStruct((t, D), F32), _vec(), _vec(), _vec()],
                    [_rows(tr)] + [_whole((1, D))] * 3, t // tr, rider=rider)


def _gate_bwd(name, dx, f, g, scale):
    t = dx.shape[0]
    tr = min(t, 256)

    def fn(i, r, _):
        d = r[0][...]
        return [scale * r[2][...] * d, jnp.sum(scale * d * r[1][...].astype(F32), axis=0, keepdims=True)]

    return _rowcall(name, fn, [dx, f, g], [_rows(tr), _rows(tr), _whole((1, D))], 1,
                    [jax.ShapeDtypeStruct((t, D), BF), _vec()], [_rows(tr), _whole((1, D))], t // tr)


def _sgu_pre(up, vp, bu, bv, ln_g, ln_b):
    return jax.nn.gelu(up + bu), _layer_norm(jax.nn.gelu(vp + bv), ln_g, ln_b)


def _causal(w_ref, h):
    rows = lax.broadcasted_iota(jnp.int32, (CHUNK, CHUNK), 0)
    cols = lax.broadcasted_iota(jnp.int32, (CHUNK, CHUNK), 1)
    return jnp.where(cols <= rows, w_ref[h], 0.0)


def _sgu(name, proj, b_in, ln_g, ln_b, w_s, bias_full, rider=None):
    t = proj.shape[0]

    def fn(i, r, _):
        u, v = _sgu_pre(r[0][...], r[1][...], r[2][...], r[3][...], r[4][...], r[5][...])
        vb = v.astype(BF)
        mixed = [jnp.dot(_causal(r[6], h).astype(BF), vb[:, h * CHUNK:(h + 1) * CHUNK], preferred_element_type=F32)
                 for h in range(HEADS)]
        return [u * (jnp.concatenate(mixed, axis=1) + r[7][...])]

    return _rowcall(
        name, fn, [proj, proj, b_in, b_in, ln_g, ln_b, w_s, bias_full],
        [_rows(CHUNK, D, 0), _rows(CHUNK, D, 1), pl.BlockSpec((1, D), lambda i: (0, 0)), pl.BlockSpec((1, D), lambda i: (0, 1)),
         _whole((1, D)), _whole((1, D)), _whole((HEADS, CHUNK, CHUNK)), _whole((CHUNK, D))],
        1, [jax.ShapeDtypeStruct((t, D), BF)], [_rows(CHUNK)], t // CHUNK, rider=rider)


def _sgu_bwd(name, proj, b_in, ln_g, ln_b, w_s, bias_full, dout, rider=None):
    t = proj.shape[0]

    def fn(i, r, _):
        (u, v), vjp = jax.vjp(_sgu_pre, r[0][...], r[1][...], r[2][...], r[3][...], r[4][...], r[5][...])
        vb = v.astype(BF)
        d = r[8][...]
        masks = [_causal(r[6], h).astype(BF) for h in range(HEADS)]
        cols = [slice(h * CHUNK, (h + 1) * CHUNK) for h in range(HEADS)]
        mixed = jnp.concatenate([jnp.dot(masks[h], vb[:, cols[h]], preferred_element_type=F32) for h in range(HEADS)], axis=1)
        du = d * (mixed + r[7][...])
        dmix = d * u
        dmb = dmix.astype(BF)
        dv = jnp.concatenate([lax.dot_general(masks[h], dmb[:, cols[h]], TN, preferred_element_type=F32) for h in range(HEADS)], axis=1)
        rows = lax.broadcasted_iota(jnp.int32, (CHUNK, CHUNK), 0)
        lanes = lax.broadcasted_iota(jnp.int32, (CHUNK, CHUNK), 1)
        dws = jnp.stack([jnp.where(lanes <= rows, lax.dot_general(dmb[:, cols[h]], vb[:, cols[h]], NT, preferred_element_type=F32), 0.0)
                         for h in range(HEADS)])
        dbs = jnp.zeros((CHUNK, CHUNK), F32)
        for h in range(HEADS):
            dbs = dbs + jnp.where(lanes == h, jnp.sum(dmix[:, cols[h]], axis=1, keepdims=True), 0.0)
        dup, dvp, dbu, dbv, dg, db = vjp((du, dv))
        return [dup, dvp, dbu, dbv, dg, db, dws, dbs]

    return _rowcall(
        name, fn, [proj, proj, b_in, b_in, ln_g, ln_b, w_s, bias_full, dout],
        [_rows(CHUNK, D, 0), _rows(CHUNK, D, 1), pl.BlockSpec((1, D), lambda i: (0, 0)), pl.BlockSpec((1, D), lambda i: (0, 1)),
         _whole((1, D)), _whole((1, D)), _whole((HEADS, CHUNK, CHUNK)), _whole((CHUNK, D)), _rows(CHUNK)],
        2, [jax.ShapeDtypeStruct((t, D), BF)] * 2 + [_vec()] * 4
        + [jax.ShapeDtypeStruct((HEADS, CHUNK, CHUNK), F32), jax.ShapeDtypeStruct((CHUNK, CHUNK), F32)],
        [_rows(CHUNK)] * 2 + [_whole((1, D))] * 4 + [_whole((HEADS, CHUNK, CHUNK)), _whole((CHUNK, CHUNK))], t // CHUNK, rider=rider)


def _halo_before(tr, cb):
    return pl.BlockSpec((HALO, D), lambda i: (jnp.maximum(i * (tr // HALO) - 1, 0), cb))


def _halo_after(tr, cb, n_tiles):
    return pl.BlockSpec((HALO, D), lambda i: (jnp.minimum((i + 1) * (tr // HALO), n_tiles * (tr // HALO) - 1), cb))


def _ln_silu(z, g, b):
    return _silu(_layer_norm(z, g, b))


def _conv(name, proj, b_in, conv_w, conv_b, ln_g, ln_b, rider=None):
    t = proj.shape[0]
    tr = min(t, 256)

    def fn(i, r, scr):
        zbuf = scr[0]
        bv, bg = r[4][...], r[5][...]
        z0 = (r[0][...] + bv) * jax.nn.sigmoid(r[1][...] + bg)
        before = (r[2][...] + bv) * jax.nn.sigmoid(r[3][...] + bg)
        zbuf[pl.ds(0, HALO), :] = jnp.where(i > 0, before, 0.0)
        zbuf[pl.ds(HALO, tr), :] = z0
        acc = jnp.zeros((tr, D), F32) + r[7][...]
        for k in range(KW):
            acc = acc + r[6][k:k + 1, :] * zbuf[pl.ds(HALO - (KW - 1) + k, tr), :]
        return [z0, acc, _ln_silu(acc, r[8][...], r[9][...])]

    return _rowcall(
        name, fn, [proj, proj, proj, proj, b_in, b_in, conv_w, conv_b, ln_g, ln_b],
        [_rows(tr, D, 2), _rows(tr, D, 3), _halo_before(tr, 2), _halo_before(tr, 3),
         pl.BlockSpec((1, D), lambda i: (0, 2)), pl.BlockSpec((1, D), lambda i: (0, 3)),
         _whole((HALO, D)), _whole((1, D)), _whole((1, D)), _whole((1, D))],
        3, [jax.ShapeDtypeStruct((t, D), F32), jax.ShapeDtypeStruct((t, D), F32), jax.ShapeDtypeStruct((t, D), BF)],
        [_rows(tr)] * 3, t // tr, [pltpu.VMEM((tr + HALO, D), F32)], rider=rider)


def _conv_bwd(name, proj, b_in, conv_w, ln_g, ln_b, z0, z1, dz3, rider=None):
    t = proj.shape[0]
    tr = min(t, 256)
    n_tiles = t // tr

    def fn(i, r, scr):
        zbuf, dbuf = scr
        g, b = r[5][...], r[6][...]
        _, vjp = jax.vjp(_ln_silu, r[9][...], g, b)
        dz1, dg, db = vjp(r[11][...])
        _, vjp_after = jax.vjp(_ln_silu, r[10][...], g, b)
        dz1_after = vjp_after(r[12][...])[0]
        dbuf[pl.ds(0, tr), :] = dz1
        dbuf[pl.ds(tr, HALO), :] = jnp.where(i < n_tiles - 1, dz1_after, 0.0)
        zbuf[pl.ds(0, HALO), :] = jnp.where(i > 0, r[8][...], 0.0)
        zbuf[pl.ds(HALO, tr), :] = r[7][...]
        dz0 = jnp.zeros((tr, D), F32)
        dw_rows = []
        for k in range(KW):
            dz0 = dz0 + r[4][k:k + 1, :] * dbuf[pl.ds(KW - 1 - k, tr), :]
            dw_rows.append(jnp.sum(dz1 * zbuf[pl.ds(HALO - (KW - 1) + k, tr), :], axis=0, keepdims=True))
        dw_rows.append(jnp.zeros((HALO - KW, D), F32))
        a = r[0][...] + r[2][...]
        s = jax.nn.sigmoid(r[1][...] + r[3][...])
        dcv = dz0 * s
        dcg = dz0 * a * s * (1.0 - s)
        return [dcv, dcg, jnp.sum(dcv, axis=0, keepdims=True), jnp.sum(dcg, axis=0, keepdims=True),
                jnp.concatenate(dw_rows, axis=0), jnp.sum(dz1, axis=0, keepdims=True), dg, db]

    return _rowcall(
        name, fn, [proj, proj, b_in, b_in, conv_w, ln_g, ln_b, z0, z0, z1, z1, dz3, dz3],
        [_rows(tr, D, 2), _rows(tr, D, 3), pl.BlockSpec((1, D), lambda i: (0, 2)), pl.BlockSpec((1, D), lambda i: (0, 3)),
         _whole((HALO, D)), _whole((1, D)), _whole((1, D)),
         _rows(tr), _halo_before(tr, 0), _rows(tr), _halo_after(tr, 0, n_tiles), _rows(tr), _halo_after(tr, 0, n_tiles)],
        2, [jax.ShapeDtypeStruct((t, D), BF)] * 2 + [_vec(), _vec(), _vec(HALO), _vec(), _vec(), _vec()],
        [_rows(tr)] * 2 + [_whole((1, D))] * 2 + [_whole((HALO, D))] + [_whole((1, D))] * 3, n_tiles,
        [pltpu.VMEM((tr + HALO, D), F32), pltpu.VMEM((tr + HALO, D), F32)], rider=rider)


def _merge_fn(ga, gb, bga, bgb, ya, yb):
    return jax.nn.sigmoid(ga + bga) * ya + jax.nn.sigmoid(gb + bgb) * yb


def _merge(name, proj, b_in, ya, yb, rider=None):
    t = proj.shape[0]
    tr = min(t, 256)

    def fn(i, r, _):
        return [_merge_fn(*[x[...] for x in r])]

    return _rowcall(
        name, fn, [proj, proj, b_in, b_in, ya, yb],
        [_rows(tr, D, 4), _rows(tr, D, 5), pl.BlockSpec((1, D), lambda i: (0, 4)), pl.BlockSpec((1, D), lambda i: (0, 5)),
         _rows(tr), _rows(tr)],
        1, [jax.ShapeDtypeStruct((t, D), BF)], [_rows(tr)], t // tr, rider=rider)


def _merge_bwd(name, proj, b_in, ya, yb, dm):
    t = proj.shape[0]
    tr = min(t, 256)

    def fn(i, r, _):
        _, vjp = jax.vjp(_merge_fn, *[x[...] for x in r[:6]])
        dga, dgb, dbga, dbgb, dya, dyb = vjp(r[6][...])
        return [dga, dgb, dya, dyb, dbga, dbgb]

    return _rowcall(
        name, fn, [proj, proj, b_in, b_in, ya, yb, dm],
        [_rows(tr, D, 4), _rows(tr, D, 5), pl.BlockSpec((1, D), lambda i: (0, 4)), pl.BlockSpec((1, D), lambda i: (0, 5)),
         _rows(tr), _rows(tr), _rows(tr)],
        4, [jax.ShapeDtypeStruct((t, D), BF)] * 4 + [_vec(), _vec()], [_rows(tr)] * 4 + [_whole((1, D))] * 2, t // tr)


def _loss_head(name, x, gain, target):
    t = x.shape[0]
    tr = min(t, 256)

    def loss_fn(xv, g, tgt):
        y = xv * lax.rsqrt(jnp.mean(xv * xv, axis=-1, keepdims=True) + EPS) * g
        return 0.5 * jnp.sum(jnp.mean(jnp.square(y - tgt), axis=-1))

    def fn(i, r, _):
        loss, vjp = jax.vjp(loss_fn, r[0][...], r[1][...], r[2][...])
        dx, dg, _ = vjp(jnp.ones((), F32))
        return [dx, dg, jnp.zeros((1, D), F32) + loss]

    return _rowcall(name, fn, [x, gain, target], [_rows(tr), _whole((1, D)), _rows(tr)], 1,
                    [jax.ShapeDtypeStruct((t, D), F32), _vec(), _vec()], [_rows(tr), _whole((1, D)), _whole((1, D))], t // tr)


def _adamw(w, g, m, v):
    m = B1 * m + (1.0 - B1) * g
    v = B2 * v + (1.0 - B2) * jnp.square(g)
    m_hat = m / BC1
    v_hat = v / BC2
    delta = -LR * (m_hat / (jnp.sqrt(v_hat) + ADAM_EPS) + WD * w)
    return delta, m, v


def _ada_fwd(name, c_all, ada_w, ada_b):
    nc = ada_w.shape[1]

    def body(c_ref, w_ref, b_ref, o_ref):
        o_ref[...] = jnp.dot(_silu(c_ref[...]), w_ref[...], preferred_element_type=F32) + b_ref[...]

    return pl.pallas_call(body, name=name, out_shape=jax.ShapeDtypeStruct((NDEV, nc), F32),
                          compiler_params=_params(None))(c_all, ada_w, ada_b)


ADAMW_ROWS = 64


def _adamw_group(name, items, rider=None):
    ins, in_specs, out_shapes, out_specs, plan = [], [], [], [], []
    first = 0
    for chip_sum, received, w, m, v in items:
        r, c = w.shape
        tr = min(r, ADAMW_ROWS)
        n = r // tr

        def tile(i, first=first, n=n):
            return jnp.clip(i - first, 0, n - 1)

        spec = pl.BlockSpec((tr, c), lambda i, tile=tile: (tile(i), 0))
        ins += [chip_sum, *received, w, m, v]
        in_specs += [pl.BlockSpec((None, tr, c), lambda i, tile=tile: (0, tile(i), 0))]
        in_specs += [pl.BlockSpec((g.shape[0], tr, c), lambda i, tile=tile: (0, tile(i), 0)) for g in received]
        in_specs += [spec] * 3
        out_shapes += [jax.ShapeDtypeStruct((r, c), F32)] * 4
        out_specs += [spec] * 4
        plan.append((first, n, [g.shape[0] for g in received]))
        first += n

    def compute(in_refs, out_refs, _):
        i = pl.program_id(0)
        at_in = at_out = 0
        for start, n, counts in plan:
            mine = in_refs[at_in:at_in + 4 + len(counts)]
            outs = out_refs[at_out:at_out + 4]
            at_in += 4 + len(counts)
            at_out += 4

            @pl.when(jnp.logical_and(i >= start, i < start + n))
            def _(mine=mine, outs=outs, counts=counts):
                g = mine[0][...].astype(F32)
                for j, count in enumerate(counts):
                    for s in range(count):
                        g = g + mine[1 + j][s].astype(F32)
                delta, m_new, v_new = _adamw(mine[-3][...], g, mine[-2][...], mine[-1][...])
                for o, val in zip(outs, (g, delta, m_new, v_new)):
                    o[...] = val

    res = _call(name, compute, (first,), ins, in_specs, out_shapes, out_specs, [], ("arbitrary",), rider)
    outs, rode = res if rider else (res, [])
    return [outs[4 * j:4 * j + 4] for j in range(len(items))], rode


def _adamw_small(name, packed_all, dws_all, vectors, w_s):
    n_vec = len(vectors)

    def body(*refs):
        p_ref, d_ref = refs[:2]
        param_refs = refs[2:2 + 3 * n_vec + 3]
        out_refs = refs[2 + 3 * n_vec + 3:-1]
        g_ref = refs[-1]
        g = p_ref[0]
        for s in range(1, NDEV):
            g = g + p_ref[s]
        g_ref[...] = g

        def update(gp, wmv, outs):
            delta, m_new, v_new = _adamw(wmv[0][...], gp, wmv[1][...], wmv[2][...])
            for o, val in zip(outs, (gp, delta, m_new, v_new)):
                o[...] = val

        for j, (row, rows, *_) in enumerate(vectors):
            pieces = [g_ref[pl.ds(row + r, 1), :] for r in range(rows)]
            update(pieces[0] if rows == 1 else jnp.concatenate(pieces, axis=1), param_refs[3 * j:3 * j + 3], out_refs[4 * j:4 * j + 4])
        gw = d_ref[0]
        for s in range(1, NDEV):
            gw = gw + d_ref[s]
        update(gw, param_refs[3 * n_vec:], out_refs[4 * n_vec:4 * n_vec + 4])
        out_refs[-2][...] = g_ref[pl.ds(R_CW, KW), :]
        out_refs[-1][...] = g_ref[pl.ds(R_LOSS, 1), :]

    params = [a for _, _, w, m, v in vectors for a in (w, m, v)] + list(w_s)
    out_shapes = [jax.ShapeDtypeStruct(w.shape, F32) for _, _, w, _, _ in vectors for _ in range(4)]
    out_shapes += [jax.ShapeDtypeStruct(w_s[0].shape, F32)] * 4 + [jax.ShapeDtypeStruct((KW, D), F32), _vec()]
    res = pl.pallas_call(body, name=name, out_shape=out_shapes, scratch_shapes=[pltpu.VMEM((R_TOTAL, D), F32)],
                         compiler_params=_params(None))(packed_all, dws_all, *params)
    return [res[4 * j:4 * j + 4] for j in range(n_vec + 1)], res[-2], res[-1]


def _adamw_plain(name, g, w, m, v):
    def body(g_ref, w_ref, m_ref, v_ref, d_ref, mo_ref, vo_ref):
        delta, m_new, v_new = _adamw(w_ref[...], g_ref[...], m_ref[...], v_ref[...])
        d_ref[...] = delta
        mo_ref[...] = m_new
        vo_ref[...] = v_new

    return pl.pallas_call(body, name=name, out_shape=[jax.ShapeDtypeStruct(w.shape, F32)] * 3,
                          compiler_params=_params(None))(g, w, m, v)


def _adamw_ada(name, c_all_t, dmod, w, m, v):
    r, c = w.shape
    tr = 256

    def fn(i, refs, _):
        ca = _silu(refs[0][...])
        g = ca[:, 0:1] * refs[1][0:1, :]
        for b in range(1, NDEV):
            g = g + ca[:, b:b + 1] * refs[1][b:b + 1, :]
        delta, m_new, v_new = _adamw(refs[2][...], g, refs[3][...], refs[4][...])
        return [g, delta, m_new, v_new]

    spec = pl.BlockSpec((tr, c), lambda i: (i, 0))
    return _rowcall(name, fn, [c_all_t, dmod, w, m, v],
                    [pl.BlockSpec((tr, NDEV), lambda i: (i, 0)), pl.BlockSpec((NDEV, c), lambda i: (0, 0)), spec, spec, spec], 4,
                    [jax.ShapeDtypeStruct((r, c), F32)] * 4, [spec] * 4, r // tr)


def _ffn_fwd(tag, x, gain, sh, sc, g, wg, wu, wd_shard, down_rider):
    t = x.shape[0]
    tm = min(t, 512)
    h = _norm_mod(f"{tag}_norm", x, gain, sc, sh)
    (gate, up, act), (wd,) = _ffn_up(f"{tag}_up", h, wg, wu, rider=_gather_rider([wd_shard]))

    def epilogue(f, xv, gv):
        return xv + 0.5 * gv * f, f

    res = _mm_nn(f"{tag}_down", act, wd.reshape(F, D), tm, D, 1024, extras=(x, g),
                 extra_specs=(pl.BlockSpec((tm, D), lambda i, j, k: (i, 0)), pl.BlockSpec((1, D), lambda i, j, k: (0, 0))),
                 epilogue=epilogue, out_dtypes=(F32, BF), rider=down_rider)
    (x_out, f), rode = res if down_rider else (res, None)
    return x_out, (x, h, gate, up, act, f), wd, rode


def _ffn_bwd(tag, dx_out, saved, gain, sh, sc, g, wg, wu, wd, slots, dact_rider=None, dwd_rider=None):
    x, h, gate, up, act, f = saved
    t = x.shape[0]
    tm = min(t, 1024)
    df, dg = _gate_bwd(f"{tag}_gate_bwd", dx_out, f, g, 0.5)

    def act_bwd(da, gv, uv):
        gv = gv.astype(F32)
        s = jax.nn.sigmoid(gv)
        return da * uv.astype(F32) * (s * (1.0 + gv * (1.0 - s))), da * (gv * s)

    blk = pl.BlockSpec((tm, F // NDEV), lambda i, j, k: (i, j))
    res = _mm_nt(f"{tag}_dact", df, wd.reshape(F, D), tm, F // NDEV, out_dtypes=(BF, BF),
                 extras=(gate, up), extra_specs=(blk, blk), epilogue=act_bwd, rider=dact_rider)
    (dgate, dup), rode_dact = res if dact_rider else (res, [])
    res = _mm_tn(f"{tag}_dwd", act, df, 512, D, rider=dwd_rider)
    dwd, rode_dwd = res if dwd_rider else (res, [])
    dwd = dwd.reshape(NDEV, F // NDEV, D)
    (dwg, dwu), (sib_d,) = _dw_gate_up(f"{tag}_dwgu", h, dgate, dup, rider=_pair_rider([dwd]))
    (sum_d,) = _pair_add(f"{tag}_dwd_add", [dwd], [sib_d], slots)
    dh, (sib_g, sib_u, got_d) = _mm_nt_blocked(f"{tag}_dh", [dgate, dup], [wg, wu], tm,
                                               rider=[_pair_rider([dwg, dwu]), _chip_rider([sum_d])])
    sum_g, sum_u = _pair_add(f"{tag}_dwgu_add", [dwg, dwu], [sib_g, sib_u], slots)
    dx, dgain, dsc, dsh = _norm_mod_bwd(f"{tag}_norm_bwd", x, gain, sc, sh, dh, dx_out)
    return dx, (sum_d, [got_d]), sum_g, sum_u, (dgain, dsc, dsh, dg), rode_dact, rode_dwd


def kernel(x, c, ada_w, ada_b, norm_ffn1, ffn1_w_gate, ffn1_w_up, ffn1_w_down, norm_mix, mix_w_in, mix_b_in, sgu_ln_g, sgu_ln_b, sgu_w_s, sgu_b_s, conv_w, conv_b, conv_ln_g, conv_ln_b, w_branch_a, w_branch_b, w_out, norm_ffn2, ffn2_w_gate, ffn2_w_up, ffn2_w_down, norm_final, loss_target, m_ada_w, m_ada_b, m_norm_ffn1, m_ffn1_w_gate, m_ffn1_w_up, m_ffn1_w_down, m_norm_mix, m_mix_w_in, m_mix_b_in, m_sgu_ln_g, m_sgu_ln_b, m_sgu_w_s, m_sgu_b_s, m_conv_w, m_conv_b, m_conv_ln_g, m_conv_ln_b, m_w_branch_a, m_w_branch_b, m_w_out, m_norm_ffn2, m_ffn2_w_gate, m_ffn2_w_up, m_ffn2_w_down, m_norm_final, v_ada_w, v_ada_b, v_norm_ffn1, v_ffn1_w_gate, v_ffn1_w_up, v_ffn1_w_down, v_norm_mix, v_mix_w_in, v_mix_b_in, v_sgu_ln_g, v_sgu_ln_b, v_sgu_w_s, v_sgu_b_s, v_conv_w, v_conv_b, v_conv_ln_g, v_conv_ln_b, v_w_branch_a, v_w_branch_b, v_w_out, v_norm_ffn2, v_ffn2_w_gate, v_ffn2_w_up, v_ffn2_w_down, v_norm_final):
    mx, my, mc = _position()
    me = 4 * mx + 2 * my + mc
    chip = 2 * mx + my
    slots = jnp.stack([2 * (chip ^ k) + mc for k in range(N_CHIPS)]).astype(jnp.int32)
    t = x.shape[1]
    tm = min(t, 1024)
    x0 = x.reshape(t, D)
    target = loss_target.reshape(t, D)
    given = dict(ffn1_w_gate=(ffn1_w_gate, m_ffn1_w_gate, v_ffn1_w_gate), ffn1_w_up=(ffn1_w_up, m_ffn1_w_up, v_ffn1_w_up),
                 ffn1_w_down=(ffn1_w_down, m_ffn1_w_down, v_ffn1_w_down), mix_w_in=(mix_w_in, m_mix_w_in, v_mix_w_in),
                 w_branch_a=(w_branch_a, m_w_branch_a, v_w_branch_a), w_branch_b=(w_branch_b, m_w_branch_b, v_w_branch_b),
                 w_out=(w_out, m_w_out, v_w_out), ffn2_w_gate=(ffn2_w_gate, m_ffn2_w_gate, v_ffn2_w_gate),
                 ffn2_w_up=(ffn2_w_up, m_ffn2_w_up, v_ffn2_w_up), ffn2_w_down=(ffn2_w_down, m_ffn2_w_down, v_ffn2_w_down))
    shard = {n: wmv[0][0].astype(BF) for n, wmv in given.items()}

    small_in = jnp.concatenate([c.reshape(8, CHUNK), conv_w[0], jnp.zeros((1, CHUNK), F32)], axis=0)
    small_all = _allgather_vmem("gather_c_conv", small_in).reshape(NDEV, 40, CHUNK)
    c_all = small_all[:, :8, :].reshape(NDEV, D)
    conv_w_full = jnp.transpose(small_all[:, 8:, :], (1, 0, 2)).reshape(HALO, D)
    ada_cols = N_MOD * D // NDEV
    mod_part = _ada_fwd("ada_fwd", c_all, ada_w[0], lax.dynamic_slice(ada_b, (0, me * ada_cols), (1, ada_cols)))
    mod_all = _allgather_vmem("gather_mod", mod_part).reshape(NDEV, NDEV, ada_cols)
    mod = lax.dynamic_index_in_dim(mod_all, me, axis=1, keepdims=False).reshape(N_MOD, 1, D)
    sh1, sc1, g1, sh2, sc2, g2, sh3, sc3, g3 = [mod[i] for i in range(N_MOD)]
    wg1, wu1 = _exchange("gather_ffn1", _gather_rider([shard["ffn1_w_gate"], shard["ffn1_w_up"]]))

    x1, saved1, wd1, (w_in,) = _ffn_fwd("ffn1", x0, norm_ffn1, sh1, sc1, g1, wg1, wu1, shard["ffn1_w_down"],
                                         _gather_rider([shard["mix_w_in"]]))
    h2 = _norm_mod("mix_norm", x1, norm_mix, sc2, sh2)
    proj, (wg2,) = _mm_nn_blocked("mix_in", h2, w_in, tm, rider=_gather_rider([shard["ffn2_w_gate"]]))
    bias_full = jnp.repeat(sgu_b_s[0].T, CHUNK, axis=1)
    (ua,), (wa3, wb3) = _sgu("sgu", proj, mix_b_in, sgu_ln_g, sgu_ln_b, sgu_w_s[0], bias_full,
                             rider=_gather_rider([shard["w_branch_a"], shard["w_branch_b"]]))
    (z0, z1, z3), (wu2,) = _conv("conv", proj, mix_b_in, conv_w_full, conv_b, conv_ln_g, conv_ln_b,
                                 rider=_gather_rider([shard["ffn2_w_up"]]))
    wa, wb = wa3.reshape(D, D), wb3.reshape(D, D)
    ya = _mm_nn("branch_a", ua, wa, tm, 512, D)[0]
    yb = _mm_nn("branch_b", z3, wb, tm, 512, D)[0]
    (merged,), (wo3,) = _merge("merge", proj, mix_b_in, ya, yb, rider=_gather_rider([shard["w_out"]]))
    wo = wo3.reshape(D, D)

    def mix_epilogue(yv, xv, gv):
        return xv + gv * yv, yv

    tmo = min(t, 512)
    x2, y = _mm_nn("mix_out", merged, wo, tmo, D, D, extras=(x1, g2),
                   extra_specs=(pl.BlockSpec((tmo, D), lambda i, j, k: (i, 0)), pl.BlockSpec((1, D), lambda i, j, k: (0, 0))),
                   epilogue=mix_epilogue, out_dtypes=(F32, BF))
    x3, saved3, wd2, _ = _ffn_fwd("ffn2", x2, norm_ffn2, sh3, sc3, g3, wg2, wu2, shard["ffn2_w_down"], None)

    norm_final2 = norm_final.reshape(1, D)
    dx3, d_norm_final, loss_row = _loss_head("loss_head", x3, norm_final2, target)
    dx2, down2, sum_g2, sum_u2, (d_norm_ffn2, dsc3, dsh3, dg3), _, _ = _ffn_bwd(
        "ffn2", dx3, saved3, norm_ffn2, sh3, sc3, g3, wg2, wu2, wd2, slots)
    dy, dg2 = _gate_bwd("mix_gate_bwd", dx2, y, g2, 1.0)
    dm = _mm_nt("mix_out_bwd", dy, wo, tm, 512)[0]
    dwo = _mm_tn("mix_dwo", merged, dy, 512, D).reshape(NDEV, D // NDEV, D)
    dga, dgb, dya, dyb, db_ga, db_gb = _merge_bwd("merge_bwd", proj, mix_b_in, ya, yb, dm)
    dua = _mm_nt("branch_a_bwd", dya, wa, tm, 512)[0]
    dwa = _mm_tn("branch_dwa", ua, dya, 512, D).reshape(NDEV, D // NDEV, D)
    dz3 = _mm_nt("branch_b_bwd", dyb, wb, tm, 512)[0]
    dwb = _mm_tn("branch_dwb", z3, dyb, 512, D).reshape(NDEV, D // NDEV, D)
    (dup, dvp, db_u, db_v, d_sgu_g, d_sgu_b, d_ws, d_bs_t), sib_abo = _sgu_bwd(
        "sgu_bwd", proj, mix_b_in, sgu_ln_g, sgu_ln_b, sgu_w_s[0], bias_full, dua, rider=_pair_rider([dwa, dwb, dwo]))
    sum_a, sum_b, sum_o = _pair_add("mix_dw_add", [dwa, dwb, dwo], sib_abo, slots)
    (dcv, dcg, db_cv, db_cg, d_cw, d_cb, d_cln_g, d_cln_b), (got_g2, got_u2) = _conv_bwd(
        "conv_bwd", proj, mix_b_in, conv_w_full, conv_ln_g, conv_ln_b, z0, z1, dz3, rider=_chip_rider([sum_g2, sum_u2]))
    dproj = jnp.concatenate([dup, dvp, dcv, dcg, dga, dgb], axis=1)
    dwin, (got_a, got_b, got_o) = _mm_tn_blocked("mix_dwin", h2, dproj, rider=_chip_rider([sum_a, sum_b, sum_o]))
    dh2, (sib_in,) = _mm_nt_blocked("mix_in_bwd", [dproj], [w_in], tm, rider=_pair_rider([dwin]))
    (sum_in,) = _pair_add("mix_dwin_add", [dwin], [sib_in], slots)
    dx1, d_norm_mix, dsc2, dsh2 = _norm_mod_bwd("mix_norm_bwd", x1, norm_mix, sc2, sh2, dh2, dx2)
    dx0, down1, sum_g1, sum_u1, (d_norm_ffn1, dsc1, dsh1, dg1), (got_in_near,), (got_in_far,) = _ffn_bwd(
        "ffn1", dx1, saved1, norm_ffn1, sh1, sc1, g1, wg1, wu1, wd1, slots,
        dact_rider=_chip_rider([sum_in], NEIGHBOURS), dwd_rider=_chip_rider([sum_in], DIAGONAL))

    d_bs = jnp.transpose(d_bs_t[:, :HEADS])
    pack_rows = [dsh1, dsc1, dg1, dsh2, dsc2, dg2, dsh3, dsc3, dg3,
                 d_norm_ffn1, d_norm_mix, d_norm_ffn2, d_norm_final,
                 db_u, db_v, db_cv, db_cg, db_ga, db_gb,
                 d_sgu_g, d_sgu_b, d_bs.reshape(1, D), d_cb, d_cln_g, d_cln_b,
                 d_cw[:KW], loss_row, jnp.zeros((R_TOTAL - R_LOSS - 1, D), F32)]
    packed = jnp.concatenate(pack_rows, axis=0)
    d_ws2 = d_ws.reshape(HEADS * CHUNK, CHUNK)
    grads = dict(ffn2_w_gate=(sum_g2, [got_g2]), ffn2_w_up=(sum_u2, [got_u2]), ffn2_w_down=down2,
                 mix_w_in=(sum_in, [got_in_near, got_in_far]), w_branch_a=(sum_a, [got_a]), w_branch_b=(sum_b, [got_b]),
                 w_out=(sum_o, [got_o]), ffn1_w_down=down1)
    done, (got_g1, got_u1, packed_all, dws_all) = _adamw_group(
        "adamw_most", [(cs, got, *[a[0] for a in given[n]]) for n, (cs, got) in grads.items()],
        rider=[_chip_rider([sum_g1, sum_u1]), _gather_rider([packed, d_ws2])])
    last, _ = _adamw_group("adamw_ffn1_in", [(sum_g1, [got_g1], *[a[0] for a in given["ffn1_w_gate"]]),
                                            (sum_u1, [got_u1], *[a[0] for a in given["ffn1_w_up"]])])
    big_out = {n: [o.reshape(given[n][0].shape) for o in outs]
               for n, outs in zip([*grads, "ffn1_w_gate", "ffn1_w_up"], [*done, *last])}

    flat = lambda a: a.reshape(1, -1)
    vectors = [("ada_b", 0, 9, ada_b, m_ada_b, v_ada_b), ("norm_ffn1", 9, 1, norm_ffn1, m_norm_ffn1, v_norm_ffn1),
               ("norm_mix", 10, 1, norm_mix, m_norm_mix, v_norm_mix), ("norm_ffn2", 11, 1, norm_ffn2, m_norm_ffn2, v_norm_ffn2),
               ("norm_final", 12, 1, norm_final, m_norm_final, v_norm_final), ("mix_b_in", 13, 6, mix_b_in, m_mix_b_in, v_mix_b_in),
               ("sgu_ln_g", 19, 1, sgu_ln_g, m_sgu_ln_g, v_sgu_ln_g), ("sgu_ln_b", 20, 1, sgu_ln_b, m_sgu_ln_b, v_sgu_ln_b),
               ("sgu_b_s", 21, 1, sgu_b_s, m_sgu_b_s, v_sgu_b_s), ("conv_b", 22, 1, conv_b, m_conv_b, v_conv_b),
               ("conv_ln_g", 23, 1, conv_ln_g, m_conv_ln_g, v_conv_ln_g), ("conv_ln_b", 24, 1, conv_ln_b, m_conv_ln_b, v_conv_ln_b)]
    small_out, d_cw_all, loss_sum = _adamw_small(
        "adamw_small", packed_all, dws_all, [(row, rows, flat(wv), flat(mv), flat(vv)) for _, row, rows, wv, mv, vv in vectors],
        [a.reshape(HEADS * CHUNK, CHUNK) for a in (sgu_w_s, m_sgu_w_s, v_sgu_w_s)])
    small = {n: [o.reshape(wv.shape) for o in outs] for (n, _, _, wv, _, _), outs in zip(vectors, small_out)}
    small["sgu_w_s"] = [o.reshape(sgu_w_s.shape) for o in small_out[-1]]
    g_cw = lax.dynamic_slice(d_cw_all, (0, me * CHUNK), (KW, CHUNK))
    small["conv_w"] = [o.reshape(conv_w.shape) for o in (g_cw, *_adamw_plain("adamw_conv_w", g_cw, conv_w[0], m_conv_w[0], v_conv_w[0]))]
    loss = loss_sum[0, 0]

    dmod_all = packed_all[:, :N_MOD, :].reshape(NDEV, N_MOD * D)
    dmod_cols = lax.dynamic_slice(dmod_all, (0, me * ada_cols), (NDEV, ada_cols))
    ada_out = [o.reshape(ada_w.shape) for o in _adamw_ada("adamw_ada_w", jnp.transpose(c_all), dmod_cols, ada_w[0], m_ada_w[0], v_ada_w[0])]

    order = ["ada_w", "ada_b", "norm_ffn1", "ffn1_w_gate", "ffn1_w_up", "ffn1_w_down", "norm_mix", "mix_w_in", "mix_b_in",
             "sgu_ln_g", "sgu_ln_b", "sgu_w_s", "sgu_b_s", "conv_w", "conv_b", "conv_ln_g", "conv_ln_b", "w_branch_a",
             "w_branch_b", "w_out", "norm_ffn2", "ffn2_w_gate", "ffn2_w_up", "ffn2_w_down", "norm_final"]

    def leaf(n, kind):
        if n == "ada_w":
            return ada_out[kind]
        if n in big_out:
            return big_out[n][kind]
        return small[n][kind]

    return (loss, dx0.reshape(x.shape), *[leaf(n, kind) for kind in range(4) for n in order])
```

```python
import jax
import jax.numpy as jnp
from jax import lax
from jax.experimental import pallas as pl
from jax.experimental.pallas import tpu as pltpu

D = 1024
F = 4 * D
D_IN = 6 * D
HEADS = 8
CHUNK = 128
KW = 31
HALO = 32
N_MOD = 9
NDEV = 8
N_CHIPS = 4
EPS = 1e-6
LR, B1, B2, ADAM_EPS, WD, STEP = 0.001, 0.9, 0.999, 1e-08, 0.01, 10
BC1 = 1.0 - B1 ** STEP
BC2 = 1.0 - B2 ** STEP
VMEM_LIMIT = 56 * 1024 * 1024
MESH = pl.DeviceIdType.MESH
HBM = pl.BlockSpec(memory_space=pltpu.HBM)
VMEM = pl.BlockSpec(memory_space=pltpu.VMEM)
BF = jnp.bfloat16
F32 = jnp.float32

NN = (((1,), (0,)), ((), ()))
NT = (((1,), (1,)), ((), ()))
TN = (((0,), (0,)), ((), ()))

R_CW, R_LOSS, R_TOTAL = 25, 56, 64
R_LATE = 16


def _params(sem):
    return pltpu.CompilerParams(dimension_semantics=sem, vmem_limit_bytes=VMEM_LIMIT)


def _position():
    return lax.axis_index("x"), lax.axis_index("y"), lax.axis_index("c")


def _flip(pos, k):
    x, y, c = pos
    return (x ^ (k >> 2 & 1), y ^ (k >> 1 & 1), c ^ (k & 1))


def _index(pos):
    return 4 * pos[0] + 2 * pos[1] + pos[2]


def _allgather_vmem(name, shard):
    m_per, n = shard.shape

    def body(x_ref, out_ref, send_sems, recv_sems, local_sem):
        x, y, c = _position()
        me, sibling = (x, y, c), (x, y, 1 - c)
        chips = [(1 - x, y), (x, 1 - y), (1 - x, 1 - y)]

        def rows(pos):
            return out_ref.at[pl.ds(_index(pos) * m_per, m_per), :]

        def copy(k, block, to, src=None):
            return pltpu.make_async_remote_copy(
                src_ref=rows(block) if src is None else src, dst_ref=rows(block),
                send_sem=send_sems.at[k], recv_sem=recv_sems.at[k], device_id=to, device_id_type=MESH)

        mine = pltpu.make_async_copy(x_ref, rows(me), local_sem)
        mine.start()
        first = [copy(0, me, sibling, src=x_ref)]
        first += [copy(1 + j, me, (*chip, c), src=x_ref) for j, chip in enumerate(chips)]
        for cp in first:
            cp.start()
        passed = [copy(4 + j, (*chip, c), sibling) for j, chip in enumerate(chips)]
        for j, chip in enumerate(chips):
            copy(1 + j, (*chip, c), me).wait_recv()
            passed[j].start()
        copy(0, sibling, me).wait_recv()
        for j, chip in enumerate(chips):
            copy(4 + j, (*chip, 1 - c), me).wait_recv()
        for cp in first + passed:
            cp.wait_send()
        mine.wait()

    return pl.pallas_call(
        body, name=name,
        out_shape=jax.ShapeDtypeStruct((NDEV * m_per, n), shard.dtype),
        in_specs=[VMEM], out_specs=VMEM,
        scratch_shapes=[pltpu.SemaphoreType.DMA((7,)), pltpu.SemaphoreType.DMA((7,)), pltpu.SemaphoreType.DMA],
    )(shard)


class _Rider:
    def __init__(self, ins, out_shapes, sems, start, finish, relay=None):
        self.ins, self.out_shapes, self.sems = list(ins), list(out_shapes), list(sems)
        self.start, self.finish, self.relay = start, finish, relay


def _gather_rider(shards):
    n = len(shards)

    def setup(ins, outs, sems):
        send_sems, recv_sems, local_sems = sems
        x, y, c = _position()
        me, sibling = (x, y, c), (x, y, 1 - c)
        chips = [(1 - x, y), (x, 1 - y), (1 - x, 1 - y)]

        def copy(a, k, block, to, own=False):
            slot = outs[a].at[_index(block)]
            return pltpu.make_async_remote_copy(
                src_ref=ins[a] if own else slot, dst_ref=slot,
                send_sem=send_sems.at[k, a], recv_sem=recv_sems.at[k, a], device_id=to, device_id_type=MESH)

        def own_copies():
            mine = [pltpu.make_async_copy(ins[a], outs[a].at[_index(me)], local_sems.at[a]) for a in range(n)]
            first = []
            for a in range(n):
                first.append(copy(a, 0, me, sibling, own=True))
                first += [copy(a, 1 + j, me, (*chip, c), own=True) for j, chip in enumerate(chips)]
            return mine, first

        return me, sibling, chips, c, copy, own_copies

    def start(ins, outs, sems):
        mine, first = setup(ins, outs, sems)[-1]()
        for cp in mine + first:
            cp.start()

    def relay(ins, outs, sems):
        me, sibling, chips, c, copy, _ = setup(ins, outs, sems)
        for j, chip in enumerate(chips):
            for a in range(n):
                copy(a, 1 + j, (*chip, c), me).wait_recv()
                copy(a, 4 + j, (*chip, c), sibling).start()

    def finish(ins, outs, sems):
        me, sibling, chips, c, copy, own_copies = setup(ins, outs, sems)
        mine, first = own_copies()
        passed = [copy(a, 4 + j, (*chip, c), sibling) for j, chip in enumerate(chips) for a in range(n)]
        for a in range(n):
            copy(a, 0, sibling, me).wait_recv()
            for j, chip in enumerate(chips):
                copy(a, 4 + j, (*chip, 1 - c), me).wait_recv()
        for cp in first + passed:
            cp.wait_send()
        for cp in mine:
            cp.wait()

    return _Rider(shards, [jax.ShapeDtypeStruct((NDEV, *s.shape), s.dtype) for s in shards],
                  [pltpu.SemaphoreType.DMA((7, n)), pltpu.SemaphoreType.DMA((7, n)), pltpu.SemaphoreType.DMA((n,))],
                  start, finish, relay)


def _pair_rider(parts):
    n = len(parts)

    def copies(ins, outs, sems):
        send_sems, recv_sems = sems
        x, y, c = _position()
        q = 2 * x + y
        return [pltpu.make_async_remote_copy(
            src_ref=ins[a].at[2 * (q ^ k) + (1 - c)], dst_ref=outs[a].at[k],
            send_sem=send_sems.at[k, a], recv_sem=recv_sems.at[k, a], device_id=(x, y, 1 - c), device_id_type=MESH)
            for a in range(n) for k in range(N_CHIPS)]

    def start(ins, outs, sems):
        for cp in copies(ins, outs, sems):
            cp.start()

    def finish(ins, outs, sems):
        for cp in copies(ins, outs, sems):
            cp.wait()

    return _Rider(parts, [jax.ShapeDtypeStruct((N_CHIPS, *p.shape[1:]), p.dtype) for p in parts],
                  [pltpu.SemaphoreType.DMA((N_CHIPS, n)), pltpu.SemaphoreType.DMA((N_CHIPS, n))], start, finish)


NEIGHBOURS = (1, 2)
DIAGONAL = (3,)
OTHER_CHIPS = NEIGHBOURS + DIAGONAL


def _chip_rider(sums, ks=OTHER_CHIPS):
    n = len(sums)

    def copies(ins, outs, sems):
        send_sems, recv_sems = sems
        me = _position()
        return [pltpu.make_async_remote_copy(
            src_ref=ins[a].at[k], dst_ref=outs[a].at[j],
            send_sem=send_sems.at[j, a], recv_sem=recv_sems.at[j, a], device_id=_flip(me, 2 * k), device_id_type=MESH)
            for a in range(n) for j, k in enumerate(ks)]

    def start(ins, outs, sems):
        for cp in copies(ins, outs, sems):
            cp.start()

    def finish(ins, outs, sems):
        for cp in copies(ins, outs, sems):
            cp.wait()

    return _Rider(sums, [jax.ShapeDtypeStruct((len(ks), *s.shape[1:]), s.dtype) for s in sums],
                  [pltpu.SemaphoreType.DMA((len(ks), n)), pltpu.SemaphoreType.DMA((len(ks), n))], start, finish)


def _grid_edge(grid, last):
    cond = None
    for d, n in enumerate(grid):
        here = pl.program_id(d) == (n - 1 if last else 0)
        cond = here if cond is None else jnp.logical_and(cond, here)
    return cond


def _call(name, compute, grid, ins, in_specs, out_shapes, out_specs, scratch_shapes, semantics, rider=None, aliases=None):
    riders = [rider] if isinstance(rider, _Rider) else list(rider or [])
    n_in, n_out, n_scr = len(ins), len(out_shapes), len(scratch_shapes)
    n_rin, n_rout, n_rsem = [sum(len(part(r)) for r in riders) for part in (lambda r: r.ins, lambda r: r.out_shapes, lambda r: r.sems)]
    cuts = [0, n_in, n_in + n_rin, n_in + n_rin + n_out, n_in + n_rin + n_out + n_rout, n_in + n_rin + n_out + n_rout + n_scr]

    def body(*refs):
        in_refs, rin_refs, out_refs, rout_refs, scr_refs = [refs[a:b] for a, b in zip(cuts[:-1], cuts[1:])]
        rsem_refs = refs[cuts[-1]:]
        mine, at = [], [0, 0, 0]
        for r in riders:
            mine.append((r, rin_refs[at[0]:at[0] + len(r.ins)], rout_refs[at[1]:at[1] + len(r.out_shapes)],
                         rsem_refs[at[2]:at[2] + len(r.sems)]))
            at = [at[0] + len(r.ins), at[1] + len(r.out_shapes), at[2] + len(r.sems)]
        if riders:
            @pl.when(_grid_edge(grid, last=False))
            def _():
                for r, a, b, c in mine:
                    r.start(a, b, c)

        if any(r.relay for r in riders):
            @pl.when(_grid_edge(grid, last=True))
            def _():
                for r, a, b, c in mine:
                    if r.relay:
                        r.relay(a, b, c)

        compute(in_refs, out_refs, scr_refs)
        if riders:
            @pl.when(_grid_edge(grid, last=True))
            def _():
                for r, a, b, c in mine:
                    r.finish(a, b, c)

    res = pl.pallas_call(
        body, name=name, grid=grid,
        out_shape=list(out_shapes) + [s for r in riders for s in r.out_shapes],
        in_specs=list(in_specs) + [HBM] * n_rin, out_specs=list(out_specs) + [HBM] * n_rout,
        scratch_shapes=list(scratch_shapes) + [s for r in riders for s in r.sems],
        input_output_aliases=aliases or {}, compiler_params=_params(semantics),
    )(*ins, *[a for r in riders for a in r.ins])
    return (res[:n_out], res[n_out:]) if riders else res


def _exchange(name, rider):
    return _call(name, lambda *_: None, (1,), [], [], [], [], [], ("arbitrary",), rider)[1]


def _pair_add(name, parts, from_sibling, slots):
    n = len(parts)

    def body(s_ref, *refs):
        for a in range(n):
            refs[2 * n + a][...] = (refs[a][...].astype(F32) + refs[n + a][...].astype(F32)).astype(refs[2 * n + a].dtype)

    def slab(p, picked):
        _, r, c = p.shape
        return pl.BlockSpec((None, r, c), (lambda k, s: (s[k], 0, 0)) if picked else (lambda k, s: (k, 0, 0)))

    return pl.pallas_call(
        body, name=name,
        grid_spec=pltpu.PrefetchScalarGridSpec(
            num_scalar_prefetch=1, grid=(N_CHIPS,),
            in_specs=[slab(p, True) for p in parts] + [slab(p, False) for p in parts],
            out_specs=[slab(p, False) for p in parts]),
        out_shape=[jax.ShapeDtypeStruct((N_CHIPS, *p.shape[1:]), p.dtype) for p in parts],
        compiler_params=_params(("arbitrary",)),
    )(slots, *parts, *from_sibling)


def _mm(name, pairs, dims, grid, nk, out_shapes, out_specs, extras=(), extra_specs=(), epilogue=None, acc_shape=None, rider=None):
    n_pairs = len(pairs)

    def compute(ins, outs, scratch):
        def partial_sum():
            total = None
            for p in range(n_pairs):
                d = lax.dot_general(ins[2 * p][...], ins[2 * p + 1][...], dims, preferred_element_type=F32)
                total = d if total is None else total + d
            return total

        def finish(r):
            ex = [e[...] for e in ins[2 * n_pairs:]]
            res = epilogue(r, *ex) if epilogue is not None else (r,)
            for o, v in zip(outs, res):
                o[...] = v.astype(o.dtype)

        if nk == 1:
            finish(partial_sum())
        else:
            acc = scratch[0]
            k = pl.program_id(2)

            @pl.when(k == 0)
            def _():
                acc[...] = partial_sum()

            @pl.when(k > 0)
            def _():
                acc[...] += partial_sum()

            @pl.when(k == nk - 1)
            def _():
                finish(acc[...])

    operands, specs = [], []
    for a, a_spec, b, b_spec in pairs:
        operands += [a, b]
        specs += [a_spec, b_spec]
    return _call(name, compute, grid, operands + list(extras), specs + list(extra_specs), out_shapes, out_specs,
                 [pltpu.VMEM(acc_shape, F32)] if nk > 1 else [], ("parallel", "parallel", "arbitrary"), rider)


def _single(res, rider):
    return (res[0][0], res[1]) if rider else res[0]


def _silu(x):
    return x * jax.nn.sigmoid(x)


def _ffn_up(name, h, wg, wu, rider=None):
    t = h.shape[0]
    tm = min(t, 1024)
    nb = F // NDEV

    def compute(ins, outs, _):
        hv = ins[0][...]
        g = jnp.dot(hv, ins[1][...], preferred_element_type=F32)
        u = jnp.dot(hv, ins[2][...], preferred_element_type=F32)
        outs[0][...] = g.astype(BF)
        outs[1][...] = u.astype(BF)
        outs[2][...] = (_silu(g) * u).astype(BF)

    w_spec = pl.BlockSpec((None, D, nb), lambda i, j: (j, 0, 0))
    o_spec = pl.BlockSpec((tm, nb), lambda i, j: (i, j))
    return _call(name, compute, (t // tm, NDEV), [h, wg, wu], [pl.BlockSpec((tm, D), lambda i, j: (i, 0)), w_spec, w_spec],
                 [jax.ShapeDtypeStruct((t, F), BF)] * 3, [o_spec] * 3, [], ("parallel", "arbitrary"), rider)


def _mm_nn(name, a, b, tm, tn, tk, extras=(), extra_specs=(), epilogue=None, out_dtypes=(F32,), rider=None):
    m, kk = a.shape
    n = b.shape[1]
    nk = kk // tk
    return _mm(
        name, [(a, pl.BlockSpec((tm, tk), lambda i, j, k: (i, k)), b, pl.BlockSpec((tk, tn), lambda i, j, k: (k, j)))], NN,
        (m // tm, n // tn, nk), nk,
        [jax.ShapeDtypeStruct((m, n), dt) for dt in out_dtypes],
        [pl.BlockSpec((tm, tn), lambda i, j, k: (i, j))] * len(out_dtypes),
        extras, extra_specs, epilogue, (tm, tn), rider)


def _mm_nn_blocked(name, a, b3, tm, rider=None):
    m = a.shape[0]
    nb = b3.shape[2]
    return _single(_mm(
        name, [(a, pl.BlockSpec((tm, D), lambda i, j, k: (i, 0)), b3, pl.BlockSpec((None, D, nb), lambda i, j, k: (j, 0, 0)))], NN,
        (m // tm, NDEV, 1), 1,
        [jax.ShapeDtypeStruct((m, NDEV * nb), F32)], [pl.BlockSpec((tm, nb), lambda i, j, k: (i, j))], rider=rider), rider)


def _mm_nt(name, a, b, tm, tn, out_dtypes=(F32,), extras=(), extra_specs=(), epilogue=None, rider=None):
    m, kk = a.shape
    n = b.shape[0]
    return _mm(
        name, [(a, pl.BlockSpec((tm, kk), lambda i, j, k: (i, 0)), b, pl.BlockSpec((tn, kk), lambda i, j, k: (j, 0)))], NT,
        (m // tm, n // tn, 1), 1,
        [jax.ShapeDtypeStruct((m, n), dt) for dt in out_dtypes],
        [pl.BlockSpec((tm, tn), lambda i, j, k: (i, j))] * len(out_dtypes),
        extras, extra_specs, epilogue, rider=rider)


def _mm_nt_blocked(name, a_list, b3_list, tm, rider=None):
    m = a_list[0].shape[0]
    nb = b3_list[0].shape[2]
    pairs = [(a, pl.BlockSpec((tm, nb), lambda i, j, k: (i, k)), b3, pl.BlockSpec((None, D, nb), lambda i, j, k: (k, 0, 0)))
             for a, b3 in zip(a_list, b3_list)]
    return _single(_mm(name, pairs, NT, (m // tm, 1, NDEV), NDEV,
                       [jax.ShapeDtypeStruct((m, D), F32)], [pl.BlockSpec((tm, D), lambda i, j, k: (i, 0))],
                       acc_shape=(tm, D), rider=rider), rider)


def _mm_tn(name, a, b, tm, tn, rider=None):
    t, m = a.shape
    n = b.shape[1]
    return _single(_mm(
        name, [(a, pl.BlockSpec((t, tm), lambda i, j, k: (0, i)), b, pl.BlockSpec((t, tn), lambda i, j, k: (0, j)))], TN,
        (m // tm, n // tn, 1), 1,
        [jax.ShapeDtypeStruct((m, n), BF)], [pl.BlockSpec((tm, tn), lambda i, j, k: (i, j))], rider=rider), rider)


def _mm_tn_blocked(name, a, b, rider=None):
    t = a.shape[0]
    nb = b.shape[1] // NDEV
    return _single(_mm(
        name, [(a, pl.BlockSpec((t, D), lambda i, j, k: (0, 0)), b, pl.BlockSpec((t, nb), lambda i, j, k: (0, j)))], TN,
        (1, NDEV, 1), 1,
        [jax.ShapeDtypeStruct((NDEV, D, nb), BF)], [pl.BlockSpec((None, D, nb), lambda i, j, k: (j, 0, 0))], rider=rider), rider)


def _dw_gate_up(name, h, dgate, dup, rider=None):
    t = h.shape[0]
    nb = F // NDEV

    def compute(ins, outs, _):
        hv = ins[0][...]
        outs[0][...] = lax.dot_general(hv, ins[1][...], TN, preferred_element_type=F32).astype(BF)
        outs[1][...] = lax.dot_general(hv, ins[2][...], TN, preferred_element_type=F32).astype(BF)

    d_spec = pl.BlockSpec((t, nb), lambda j: (0, j))
    o_spec = pl.BlockSpec((None, D, nb), lambda j: (j, 0, 0))
    return _call(name, compute, (NDEV,), [h, dgate, dup], [pl.BlockSpec((t, D), lambda j: (0, 0)), d_spec, d_spec],
                 [jax.ShapeDtypeStruct((NDEV, D, nb), BF)] * 2, [o_spec] * 2, [], ("arbitrary",), rider)


def _rowcall(name, fn, ins, in_specs, n_row_out, out_shapes, out_specs, grid, scratch_shapes=(), rider=None, aliases=None):
    def accumulate(o, v, i):
        @pl.when(i == 0)
        def _():
            o[...] = v.astype(o.dtype)

        @pl.when(i > 0)
        def _():
            o[...] += v.astype(o.dtype)

    def compute(in_refs, out_refs, scr):
        i = pl.program_id(0)
        vals = fn(i, in_refs, scr)
        for idx, (o, v) in enumerate(zip(out_refs, vals)):
            if idx < n_row_out:
                o[...] = v.astype(o.dtype)
            else:
                accumulate(o, v, i)

    return _call(name, compute, (grid,), ins, in_specs, out_shapes, out_specs, list(scratch_shapes), ("arbitrary",), rider, aliases)


def _rows(tr, w=D, cb=0):
    return pl.BlockSpec((tr, w), lambda i: (i, cb))


def _whole(shape):
    nd = len(shape)
    return pl.BlockSpec(shape, lambda i: (0,) * nd)


def _vec(n=1):
    return jax.ShapeDtypeStruct((n, D), F32)


def _rms_mod(x, gain, sc, sh):
    y = x * lax.rsqrt(jnp.mean(x * x, axis=-1, keepdims=True) + EPS)
    return (y * gain) * (1.0 + sc) + sh


def _layer_norm(x, g, b):
    mu = jnp.mean(x, axis=-1, keepdims=True)
    var = jnp.mean(jnp.square(x - mu), axis=-1, keepdims=True)
    return (x - mu) * lax.rsqrt(var + EPS) * g + b


def _norm_mod(name, x, gain, sc, sh):
    t = x.shape[0]
    tr = min(t, 256)

    def fn(i, r, _):
        return [_rms_mod(r[0][...], r[1][...], r[2][...], r[3][...])]

    return _rowcall(name, fn, [x, gain, sc, sh], [_rows(tr)] + [_whole((1, D))] * 3, 1,
                    [jax.ShapeDtypeStruct((t, D), BF)], [_rows(tr)], t // tr)[0]


def _norm_mod_bwd(name, x, gain, sc, sh, dh, dres, rider=None):
    t = x.shape[0]
    tr = min(t, 256)

    def fn(i, r, _):
        _, vjp = jax.vjp(_rms_mod, r[0][...], r[1][...], r[2][...], r[3][...])
        dx, dgain, dsc, dsh = vjp(r[4][...])
        return [dx + r[5][...], dgain, dsc, dsh]

    return _rowcall(name, fn, [x, gain, sc, sh, dh, dres], [_rows(tr)] + [_whole((1, D))] * 3 + [_rows(tr)] * 2, 1,
                    [jax.ShapeDtypeStruct((t, D), F32), _vec(), _vec(), _vec()],
                    [_rows(tr)] + [_whole((1, D))] * 3, t // tr, rider=rider)


def _gate_bwd(name, dx, f, g, scale):
    t = dx.shape[0]
    tr = min(t, 256)

    def fn(i, r, _):
        d = r[0][...]
        return [scale * r[2][...] * d, jnp.sum(scale * d * r[1][...].astype(F32), axis=0, keepdims=True)]

    return _rowcall(name, fn, [dx, f, g], [_rows(tr), _rows(tr), _whole((1, D))], 1,
                    [jax.ShapeDtypeStruct((t, D), BF), _vec()], [_rows(tr), _whole((1, D))], t // tr)


def _sgu_pre(up, vp, bu, bv, ln_g, ln_b):
    return jax.nn.gelu(up + bu), _layer_norm(jax.nn.gelu(vp + bv), ln_g, ln_b)


def _causal(w_ref, h):
    rows = lax.broadcasted_iota(jnp.int32, (CHUNK, CHUNK), 0)
    cols = lax.broadcasted_iota(jnp.int32, (CHUNK, CHUNK), 1)
    return jnp.where(cols <= rows, w_ref[h], 0.0)


def _sgu(name, proj, b_in, ln_g, ln_b, w_s, bias_full, rider=None):
    t = proj.shape[0]

    def fn(i, r, _):
        u, v = _sgu_pre(r[0][...], r[1][...], r[2][...], r[3][...], r[4][...], r[5][...])
        vb = v.astype(BF)
        mixed = [jnp.dot(_causal(r[6], h).astype(BF), vb[:, h * CHUNK:(h + 1) * CHUNK], preferred_element_type=F32)
                 for h in range(HEADS)]
        return [u * (jnp.concatenate(mixed, axis=1) + r[7][...])]

    return _rowcall(
        name, fn, [proj, proj, b_in, b_in, ln_g, ln_b, w_s, bias_full],
        [_rows(CHUNK, D, 0), _rows(CHUNK, D, 1), pl.BlockSpec((1, D), lambda i: (0, 0)), pl.BlockSpec((1, D), lambda i: (0, 1)),
         _whole((1, D)), _whole((1, D)), _whole((HEADS, CHUNK, CHUNK)), _whole((CHUNK, D))],
        1, [jax.ShapeDtypeStruct((t, D), BF)], [_rows(CHUNK)], t // CHUNK, rider=rider)


def _sgu_bwd(name, proj, b_in, ln_g, ln_b, w_s, bias_full, dout, dproj, rider=None):
    t = proj.shape[0]

    def fn(i, r, _):
        (u, v), vjp = jax.vjp(_sgu_pre, r[0][...], r[1][...], r[2][...], r[3][...], r[4][...], r[5][...])
        vb = v.astype(BF)
        d = r[8][...]
        masks = [_causal(r[6], h).astype(BF) for h in range(HEADS)]
        cols = [slice(h * CHUNK, (h + 1) * CHUNK) for h in range(HEADS)]
        mixed = jnp.concatenate([jnp.dot(masks[h], vb[:, cols[h]], preferred_element_type=F32) for h in range(HEADS)], axis=1)
        du = d * (mixed + r[7][...])
        dmix = d * u
        dmb = dmix.astype(BF)
        dv = jnp.concatenate([lax.dot_general(masks[h], dmb[:, cols[h]], TN, preferred_element_type=F32) for h in range(HEADS)], axis=1)
        rows = lax.broadcasted_iota(jnp.int32, (CHUNK, CHUNK), 0)
        lanes = lax.broadcasted_iota(jnp.int32, (CHUNK, CHUNK), 1)
        dws = jnp.stack([jnp.where(lanes <= rows, lax.dot_general(dmb[:, cols[h]], vb[:, cols[h]], NT, preferred_element_type=F32), 0.0)
                         for h in range(HEADS)])
        dbs = jnp.zeros((CHUNK, CHUNK), F32)
        for h in range(HEADS):
            dbs = dbs + jnp.where(lanes == h, jnp.sum(dmix[:, cols[h]], axis=1, keepdims=True), 0.0)
        dup, dvp, dbu, dbv, dg, db = vjp((du, dv))
        return [jnp.concatenate([dup, dvp], axis=1), dbu, dbv, dg, db, dws, dbs]

    return _rowcall(
        name, fn, [proj, proj, b_in, b_in, ln_g, ln_b, w_s, bias_full, dout, dproj],
        [_rows(CHUNK, D, 0), _rows(CHUNK, D, 1), pl.BlockSpec((1, D), lambda i: (0, 0)), pl.BlockSpec((1, D), lambda i: (0, 1)),
         _whole((1, D)), _whole((1, D)), _whole((HEADS, CHUNK, CHUNK)), _whole((CHUNK, D)), _rows(CHUNK),
         pl.BlockSpec(memory_space=pl.ANY)],
        1, [jax.ShapeDtypeStruct(dproj.shape, dproj.dtype)] + [_vec()] * 4
        + [jax.ShapeDtypeStruct((HEADS, CHUNK, CHUNK), F32), jax.ShapeDtypeStruct((CHUNK, CHUNK), F32)],
        [pl.BlockSpec((CHUNK, 2 * D), lambda i: (i, 0))] + [_whole((1, D))] * 4 + [_whole((HEADS, CHUNK, CHUNK)), _whole((CHUNK, CHUNK))],
        t // CHUNK, rider=rider, aliases={9: 0})


def _halo_before(tr, cb):
    return pl.BlockSpec((HALO, D), lambda i: (jnp.maximum(i * (tr // HALO) - 1, 0), cb))


def _halo_after(tr, cb, n_tiles):
    return pl.BlockSpec((HALO, D), lambda i: (jnp.minimum((i + 1) * (tr // HALO), n_tiles * (tr // HALO) - 1), cb))


def _ln_silu(z, g, b):
    return _silu(_layer_norm(z, g, b))


SUBLANES = 8
LANES = 128
CONV_STRIP = 16
DW_STRIP = 32


def _shifted_copies(buf, copies, rows):
    for b in range(1, SUBLANES):
        copies[b - 1, pl.ds(0, rows), :] = buf[pl.ds(b, rows), :]


def _shifted(buf, copies, offset, start, rows, lanes=slice(None)):
    at = pl.ds(pl.multiple_of(start + SUBLANES * (offset // SUBLANES), SUBLANES), rows)
    return buf[at, lanes] if offset % SUBLANES == 0 else copies[offset % SUBLANES - 1, at, lanes]


def _accumulate(o, v, i):
    @pl.when(i == 0)
    def _():
        o[...] = v.astype(o.dtype)

    @pl.when(i > 0)
    def _():
        o[...] += v.astype(o.dtype)


def _conv(name, proj, b_in, conv_w, conv_b, ln_g, ln_b, rider=None):
    t = proj.shape[0]
    tr = min(t, 256)

    def compute(r, outs, scr):
        zbuf, zs = scr
        i = pl.program_id(0)
        bv, bg = r[4][...], r[5][...]
        z0 = (r[0][...] + bv) * jax.nn.sigmoid(r[1][...] + bg)
        before = (r[2][...] + bv) * jax.nn.sigmoid(r[3][...] + bg)
        zbuf[pl.ds(0, HALO), :] = jnp.where(i > 0, before, 0.0)
        zbuf[pl.ds(HALO, tr), :] = z0
        outs[0][...] = z0
        _shifted_copies(zbuf, zs, tr + HALO - SUBLANES)

        def strip(s, carry):
            r0 = s * CONV_STRIP
            acc = jnp.zeros((CONV_STRIP, D), F32) + r[7][...]
            for k in range(KW):
                acc = acc + r[6][k:k + 1, :] * _shifted(zbuf, zs, HALO - (KW - 1) + k, r0, CONV_STRIP)
            outs[1][pl.ds(pl.multiple_of(r0, SUBLANES), CONV_STRIP), :] = acc
            return carry

        lax.fori_loop(0, tr // CONV_STRIP, strip, 0)
        outs[2][...] = _ln_silu(outs[1][...], r[8][...], r[9][...]).astype(BF)

    return _call(
        name, compute, (t // tr,), [proj, proj, proj, proj, b_in, b_in, conv_w, conv_b, ln_g, ln_b],
        [_rows(tr, D, 2), _rows(tr, D, 3), _halo_before(tr, 2), _halo_before(tr, 3),
         pl.BlockSpec((1, D), lambda i: (0, 2)), pl.BlockSpec((1, D), lambda i: (0, 3)),
         _whole((HALO, D)), _whole((1, D)), _whole((1, D)), _whole((1, D))],
        [jax.ShapeDtypeStruct((t, D), F32), jax.ShapeDtypeStruct((t, D), F32), jax.ShapeDtypeStruct((t, D), BF)],
        [_rows(tr)] * 3, [pltpu.VMEM((tr + HALO, D), F32), pltpu.VMEM((SUBLANES - 1, tr + HALO, D), F32)], ("arbitrary",), rider)


def _conv_bwd(name, proj, b_in, conv_w, ln_g, ln_b, z0, z1, dz3, dproj, rider=None):
    t = proj.shape[0]
    tr = min(t, 256)
    n_tiles = t // tr

    def compute(r, outs, scr):
        zbuf, dbuf, zs, ds, dwacc = scr
        i = pl.program_id(0)
        g, b = r[5][...], r[6][...]
        _, vjp = jax.vjp(_ln_silu, r[9][...], g, b)
        dz1, dg, db = vjp(r[11][...])
        _, vjp_after = jax.vjp(_ln_silu, r[10][...], g, b)
        dz1_after = vjp_after(r[12][...])[0]
        dbuf[pl.ds(0, tr), :] = dz1
        dbuf[pl.ds(tr, HALO), :] = jnp.where(i < n_tiles - 1, dz1_after, 0.0)
        zbuf[pl.ds(0, HALO), :] = jnp.where(i > 0, r[8][...], 0.0)
        zbuf[pl.ds(HALO, tr), :] = r[7][...]
        _shifted_copies(dbuf, ds, tr + HALO - SUBLANES)
        _shifted_copies(zbuf, zs, tr + HALO - SUBLANES)

        def dz0_strip(s, carry):
            r0 = s * CONV_STRIP
            at = pl.ds(pl.multiple_of(r0, CONV_STRIP), CONV_STRIP)
            acc = jnp.zeros((CONV_STRIP, D), F32)
            for k in range(KW):
                acc = acc + r[4][k:k + 1, :] * _shifted(dbuf, ds, KW - 1 - k, r0, CONV_STRIP)
            a = r[0][at, :] + r[2][...]
            sg = jax.nn.sigmoid(r[1][at, :] + r[3][...])
            dcv = acc * sg
            dcg = acc * a * sg * (1.0 - sg)
            outs[0][at, :] = jnp.concatenate([dcv, dcg], axis=1).astype(BF)
            return carry[0] + jnp.sum(dcv, axis=0, keepdims=True), carry[1] + jnp.sum(dcg, axis=0, keepdims=True)

        zero_row = jnp.zeros((1, D), F32)
        dbv, dbg = lax.fori_loop(0, tr // CONV_STRIP, dz0_strip, (zero_row, zero_row))

        for lb in range(D // LANES):
            lanes = slice(lb * LANES, (lb + 1) * LANES)

            def dw_strip(s, accs, lanes=lanes):
                r0 = s * DW_STRIP
                dz = dbuf[pl.ds(pl.multiple_of(r0, SUBLANES), DW_STRIP), lanes]
                out = []
                for k in range(KW):
                    prod = dz * _shifted(zbuf, zs, HALO - (KW - 1) + k, r0, DW_STRIP, lanes)
                    part = prod[0:SUBLANES]
                    for q in range(1, DW_STRIP // SUBLANES):
                        part = part + prod[q * SUBLANES:(q + 1) * SUBLANES]
                    out.append(accs[k] + part)
                return tuple(out)

            accs = lax.fori_loop(0, tr // DW_STRIP, dw_strip, tuple(jnp.zeros((SUBLANES, LANES), F32) for _ in range(KW)))
            for k in range(KW):
                dwacc[pl.ds(k * SUBLANES, SUBLANES), lanes] = accs[k]
        dw_rows = [jnp.sum(dwacc[pl.ds(k * SUBLANES, SUBLANES), :], axis=0, keepdims=True) for k in range(KW)]
        dw_rows.append(jnp.zeros((HALO - KW, D), F32))
        for o, v in zip(outs[1:], (dbv, dbg, jnp.concatenate(dw_rows, axis=0), jnp.sum(dz1, axis=0, keepdims=True), dg, db)):
            _accumulate(o, v, i)

    wide = pl.BlockSpec((tr, 2 * D), lambda i: (i, 1))
    return _call(
        name, compute, (n_tiles,), [proj, proj, b_in, b_in, conv_w, ln_g, ln_b, z0, z0, z1, z1, dz3, dz3, dproj],
        [_rows(tr, D, 2), _rows(tr, D, 3), pl.BlockSpec((1, D), lambda i: (0, 2)), pl.BlockSpec((1, D), lambda i: (0, 3)),
         _whole((HALO, D)), _whole((1, D)), _whole((1, D)),
         _rows(tr), _halo_before(tr, 0), _rows(tr), _halo_after(tr, 0, n_tiles), _rows(tr), _halo_after(tr, 0, n_tiles),
         pl.BlockSpec(memory_space=pl.ANY)],
        [jax.ShapeDtypeStruct(dproj.shape, dproj.dtype), _vec(), _vec(), _vec(HALO), _vec(), _vec(), _vec()],
        [wide] + [_whole((1, D))] * 2 + [_whole((HALO, D))] + [_whole((1, D))] * 3,
        [pltpu.VMEM((tr + HALO, D), F32), pltpu.VMEM((tr + HALO, D), F32),
         pltpu.VMEM((SUBLANES - 1, tr + HALO, D), F32), pltpu.VMEM((SUBLANES - 1, tr + HALO, D), F32),
         pltpu.VMEM((HALO * SUBLANES, D), F32)],
        ("arbitrary",), rider, aliases={13: 0})


def _merge_fn(ga, gb, bga, bgb, ya, yb):
    return jax.nn.sigmoid(ga + bga) * ya + jax.nn.sigmoid(gb + bgb) * yb


def _merge(name, proj, b_in, ya, yb, rider=None):
    t = proj.shape[0]
    tr = min(t, 256)

    def fn(i, r, _):
        return [_merge_fn(*[x[...] for x in r])]

    return _rowcall(
        name, fn, [proj, proj, b_in, b_in, ya, yb],
        [_rows(tr, D, 4), _rows(tr, D, 5), pl.BlockSpec((1, D), lambda i: (0, 4)), pl.BlockSpec((1, D), lambda i: (0, 5)),
         _rows(tr), _rows(tr)],
        1, [jax.ShapeDtypeStruct((t, D), BF)], [_rows(tr)], t // tr, rider=rider)


def _merge_bwd(name, proj, b_in, ya, yb, dm):
    t = proj.shape[0]
    tr = min(t, 256)

    def fn(i, r, _):
        _, vjp = jax.vjp(_merge_fn, *[x[...] for x in r[:6]])
        dga, dgb, dbga, dbgb, dya, dyb = vjp(r[6][...])
        return [jnp.concatenate([dga, dgb], axis=1), dya, dyb, dbga, dbgb]

    return _rowcall(
        name, fn, [proj, proj, b_in, b_in, ya, yb, dm],
        [_rows(tr, D, 4), _rows(tr, D, 5), pl.BlockSpec((1, D), lambda i: (0, 4)), pl.BlockSpec((1, D), lambda i: (0, 5)),
         _rows(tr), _rows(tr), _rows(tr)],
        3, [jax.ShapeDtypeStruct((t, D_IN), BF)] + [jax.ShapeDtypeStruct((t, D), BF)] * 2 + [_vec(), _vec()],
        [pl.BlockSpec((tr, 2 * D), lambda i: (i, 2))] + [_rows(tr)] * 2 + [_whole((1, D))] * 2, t // tr)


def _loss_head(name, x, gain, target):
    t = x.shape[0]
    tr = min(t, 256)

    def loss_fn(xv, g, tgt):
        y = xv * lax.rsqrt(jnp.mean(xv * xv, axis=-1, keepdims=True) + EPS) * g
        return 0.5 * jnp.sum(jnp.mean(jnp.square(y - tgt), axis=-1))

    def fn(i, r, _):
        loss, vjp = jax.vjp(loss_fn, r[0][...], r[1][...], r[2][...])
        dx, dg, _ = vjp(jnp.ones((), F32))
        return [dx, dg, jnp.zeros((1, D), F32) + loss]

    return _rowcall(name, fn, [x, gain, target], [_rows(tr), _whole((1, D)), _rows(tr)], 1,
                    [jax.ShapeDtypeStruct((t, D), F32), _vec(), _vec()], [_rows(tr), _whole((1, D)), _whole((1, D))], t // tr)


def _adamw(w, g, m, v):
    m = B1 * m + (1.0 - B1) * g
    v = B2 * v + (1.0 - B2) * jnp.square(g)
    m_hat = m / BC1
    v_hat = v / BC2
    delta = -LR * (m_hat / (jnp.sqrt(v_hat) + ADAM_EPS) + WD * w)
    return delta, m, v


def _ada_fwd(name, c_all, ada_w, ada_b):
    nc = ada_w.shape[1]

    def body(c_ref, w_ref, b_ref, o_ref):
        o_ref[...] = jnp.dot(_silu(c_ref[...]), w_ref[...], preferred_element_type=F32) + b_ref[...]

    return pl.pallas_call(body, name=name, out_shape=jax.ShapeDtypeStruct((NDEV, nc), F32),
                          compiler_params=_params(None))(c_all, ada_w, ada_b)


ADAMW_ROWS = 64


def _adamw_group(name, items, rider=None):
    ins, in_specs, out_shapes, out_specs, plan = [], [], [], [], []
    first = 0
    for chip_sum, received, w, m, v in items:
        r, c = w.shape
        tr = min(r, ADAMW_ROWS)
        n = r // tr

        def tile(i, first=first, n=n):
            return jnp.clip(i - first, 0, n - 1)

        spec = pl.BlockSpec((tr, c), lambda i, tile=tile: (tile(i), 0))
        ins += [chip_sum, *received, w, m, v]
        in_specs += [pl.BlockSpec((None, tr, c), lambda i, tile=tile: (0, tile(i), 0))]
        in_specs += [pl.BlockSpec((g.shape[0], tr, c), lambda i, tile=tile: (0, tile(i), 0)) for g in received]
        in_specs += [spec] * 3
        out_shapes += [jax.ShapeDtypeStruct((r, c), F32)] * 4
        out_specs += [spec] * 4
        plan.append((first, n, [g.shape[0] for g in received]))
        first += n

    def compute(in_refs, out_refs, _):
        i = pl.program_id(0)
        at_in = at_out = 0
        for start, n, counts in plan:
            mine = in_refs[at_in:at_in + 4 + len(counts)]
            outs = out_refs[at_out:at_out + 4]
            at_in += 4 + len(counts)
            at_out += 4

            @pl.when(jnp.logical_and(i >= start, i < start + n))
            def _(mine=mine, outs=outs, counts=counts):
                g = mine[0][...].astype(F32)
                for j, count in enumerate(counts):
                    for s in range(count):
                        g = g + mine[1 + j][s].astype(F32)
                delta, m_new, v_new = _adamw(mine[-3][...], g, mine[-2][...], mine[-1][...])
                for o, val in zip(outs, (g, delta, m_new, v_new)):
                    o[...] = val

    res = _call(name, compute, (first,), ins, in_specs, out_shapes, out_specs, [], ("arbitrary",), rider)
    outs, rode = res if rider else (res, [])
    return [outs[4 * j:4 * j + 4] for j in range(len(items))], rode


def _adamw_small(name, packed_all, late_all, dws_all, vectors, w_s):
    n_vec = len(vectors)

    def body(*refs):
        p_ref, l_ref, d_ref = refs[:3]
        param_refs = refs[3:3 + 3 * n_vec + 3]
        out_refs = refs[3 + 3 * n_vec + 3:-1]
        g_ref = refs[-1]
        g = p_ref[0]
        late = l_ref[0]
        for s in range(1, NDEV):
            g = g + p_ref[s]
            late = late + l_ref[s]
        g_ref[...] = g
        g_ref[pl.ds(0, R_LATE), :] += late

        def update(gp, wmv, outs):
            delta, m_new, v_new = _adamw(wmv[0][...], gp, wmv[1][...], wmv[2][...])
            for o, val in zip(outs, (gp, delta, m_new, v_new)):
                o[...] = val

        for j, (row, rows, *_) in enumerate(vectors):
            pieces = [g_ref[pl.ds(row + r, 1), :] for r in range(rows)]
            update(pieces[0] if rows == 1 else jnp.concatenate(pieces, axis=1), param_refs[3 * j:3 * j + 3], out_refs[4 * j:4 * j + 4])
        gw = d_ref[0]
        for s in range(1, NDEV):
            gw = gw + d_ref[s]
        update(gw, param_refs[3 * n_vec:], out_refs[4 * n_vec:4 * n_vec + 4])
        out_refs[-2][...] = g_ref[pl.ds(R_CW, KW), :]
        out_refs[-1][...] = g_ref[pl.ds(R_LOSS, 1), :]

    params = [a for _, _, w, m, v in vectors for a in (w, m, v)] + list(w_s)
    out_shapes = [jax.ShapeDtypeStruct(w.shape, F32) for _, _, w, _, _ in vectors for _ in range(4)]
    out_shapes += [jax.ShapeDtypeStruct(w_s[0].shape, F32)] * 4 + [jax.ShapeDtypeStruct((KW, D), F32), _vec()]
    res = pl.pallas_call(body, name=name, out_shape=out_shapes, scratch_shapes=[pltpu.VMEM((R_TOTAL, D), F32)],
                         compiler_params=_params(None))(packed_all, late_all, dws_all, *params)
    return [res[4 * j:4 * j + 4] for j in range(n_vec + 1)], res[-2], res[-1]


def _adamw_plain(name, g, w, m, v):
    def body(g_ref, w_ref, m_ref, v_ref, d_ref, mo_ref, vo_ref):
        delta, m_new, v_new = _adamw(w_ref[...], g_ref[...], m_ref[...], v_ref[...])
        d_ref[...] = delta
        mo_ref[...] = m_new
        vo_ref[...] = v_new

    return pl.pallas_call(body, name=name, out_shape=[jax.ShapeDtypeStruct(w.shape, F32)] * 3,
                          compiler_params=_params(None))(g, w, m, v)


def _adamw_ada(name, c_all_t, dmod, dmod_late, w, m, v):
    r, c = w.shape
    tr = 256

    def fn(i, refs, _):
        ca = _silu(refs[0][...])
        dm = refs[1][...] + refs[2][...]
        g = ca[:, 0:1] * dm[0:1, :]
        for b in range(1, NDEV):
            g = g + ca[:, b:b + 1] * dm[b:b + 1, :]
        delta, m_new, v_new = _adamw(refs[3][...], g, refs[4][...], refs[5][...])
        return [g, delta, m_new, v_new]

    spec = pl.BlockSpec((tr, c), lambda i: (i, 0))
    whole = pl.BlockSpec((NDEV, c), lambda i: (0, 0))
    return _rowcall(name, fn, [c_all_t, dmod, dmod_late, w, m, v],
                    [pl.BlockSpec((tr, NDEV), lambda i: (i, 0)), whole, whole, spec, spec, spec], 4,
                    [jax.ShapeDtypeStruct((r, c), F32)] * 4, [spec] * 4, r // tr)


def _ffn_fwd(tag, x, gain, sh, sc, g, wg, wu, wd_shard, down_rider):
    t = x.shape[0]
    tm = min(t, 512)
    h = _norm_mod(f"{tag}_norm", x, gain, sc, sh)
    (gate, up, act), (wd,) = _ffn_up(f"{tag}_up", h, wg, wu, rider=_gather_rider([wd_shard]))

    def epilogue(f, xv, gv):
        return xv + 0.5 * gv * f, f

    res = _mm_nn(f"{tag}_down", act, wd.reshape(F, D), tm, D, 1024, extras=(x, g),
                 extra_specs=(pl.BlockSpec((tm, D), lambda i, j, k: (i, 0)), pl.BlockSpec((1, D), lambda i, j, k: (0, 0))),
                 epilogue=epilogue, out_dtypes=(F32, BF), rider=down_rider)
    (x_out, f), rode = res if down_rider else (res, None)
    return x_out, (x, h, gate, up, act, f), wd, rode


def _ffn_bwd(tag, dx_out, saved, gain, sh, sc, g, wg, wu, wd, slots, dact_rider=None, dwd_rider=None, dwgu_rider=None):
    x, h, gate, up, act, f = saved
    t = x.shape[0]
    tm = min(t, 1024)
    df, dg = _gate_bwd(f"{tag}_gate_bwd", dx_out, f, g, 0.5)

    def act_bwd(da, gv, uv):
        gv = gv.astype(F32)
        s = jax.nn.sigmoid(gv)
        return da * uv.astype(F32) * (s * (1.0 + gv * (1.0 - s))), da * (gv * s)

    blk = pl.BlockSpec((tm, F // NDEV), lambda i, j, k: (i, j))
    res = _mm_nt(f"{tag}_dact", df, wd.reshape(F, D), tm, F // NDEV, out_dtypes=(BF, BF),
                 extras=(gate, up), extra_specs=(blk, blk), epilogue=act_bwd, rider=dact_rider)
    (dgate, dup), rode_dact = res if dact_rider else (res, [])
    res = _mm_tn(f"{tag}_dwd", act, df, 512, D, rider=dwd_rider)
    dwd, rode_dwd = res if dwd_rider else (res, [])
    dwd = dwd.reshape(NDEV, F // NDEV, D)
    (dwg, dwu), (sib_d, *rode_dwgu) = _dw_gate_up(f"{tag}_dwgu", h, dgate, dup,
                                                  rider=[_pair_rider([dwd])] + ([dwgu_rider] if dwgu_rider else []))
    (sum_d,) = _pair_add(f"{tag}_dwd_add", [dwd], [sib_d], slots)
    dh, (sib_g, sib_u, got_d) = _mm_nt_blocked(f"{tag}_dh", [dgate, dup], [wg, wu], tm,
                                               rider=[_pair_rider([dwg, dwu]), _chip_rider([sum_d])])
    sum_g, sum_u = _pair_add(f"{tag}_dwgu_add", [dwg, dwu], [sib_g, sib_u], slots)
    dx, dgain, dsc, dsh = _norm_mod_bwd(f"{tag}_norm_bwd", x, gain, sc, sh, dh, dx_out)
    return dx, (sum_d, [got_d]), sum_g, sum_u, (dgain, dsc, dsh, dg), rode_dact, rode_dwd, rode_dwgu


def kernel(x, c, ada_w, ada_b, norm_ffn1, ffn1_w_gate, ffn1_w_up, ffn1_w_down, norm_mix, mix_w_in, mix_b_in, sgu_ln_g, sgu_ln_b, sgu_w_s, sgu_b_s, conv_w, conv_b, conv_ln_g, conv_ln_b, w_branch_a, w_branch_b, w_out, norm_ffn2, ffn2_w_gate, ffn2_w_up, ffn2_w_down, norm_final, loss_target, m_ada_w, m_ada_b, m_norm_ffn1, m_ffn1_w_gate, m_ffn1_w_up, m_ffn1_w_down, m_norm_mix, m_mix_w_in, m_mix_b_in, m_sgu_ln_g, m_sgu_ln_b, m_sgu_w_s, m_sgu_b_s, m_conv_w, m_conv_b, m_conv_ln_g, m_conv_ln_b, m_w_branch_a, m_w_branch_b, m_w_out, m_norm_ffn2, m_ffn2_w_gate, m_ffn2_w_up, m_ffn2_w_down, m_norm_final, v_ada_w, v_ada_b, v_norm_ffn1, v_ffn1_w_gate, v_ffn1_w_up, v_ffn1_w_down, v_norm_mix, v_mix_w_in, v_mix_b_in, v_sgu_ln_g, v_sgu_ln_b, v_sgu_w_s, v_sgu_b_s, v_conv_w, v_conv_b, v_conv_ln_g, v_conv_ln_b, v_w_branch_a, v_w_branch_b, v_w_out, v_norm_ffn2, v_ffn2_w_gate, v_ffn2_w_up, v_ffn2_w_down, v_norm_final):
    mx, my, mc = _position()
    me = 4 * mx + 2 * my + mc
    chip = 2 * mx + my
    slots = jnp.stack([2 * (chip ^ k) + mc for k in range(N_CHIPS)]).astype(jnp.int32)
    t = x.shape[1]
    tm = min(t, 1024)
    x0 = x.reshape(t, D)
    target = loss_target.reshape(t, D)
    given = dict(ffn1_w_gate=(ffn1_w_gate, m_ffn1_w_gate, v_ffn1_w_gate), ffn1_w_up=(ffn1_w_up, m_ffn1_w_up, v_ffn1_w_up),
                 ffn1_w_down=(ffn1_w_down, m_ffn1_w_down, v_ffn1_w_down), mix_w_in=(mix_w_in, m_mix_w_in, v_mix_w_in),
                 w_branch_a=(w_branch_a, m_w_branch_a, v_w_branch_a), w_branch_b=(w_branch_b, m_w_branch_b, v_w_branch_b),
                 w_out=(w_out, m_w_out, v_w_out), ffn2_w_gate=(ffn2_w_gate, m_ffn2_w_gate, v_ffn2_w_gate),
                 ffn2_w_up=(ffn2_w_up, m_ffn2_w_up, v_ffn2_w_up), ffn2_w_down=(ffn2_w_down, m_ffn2_w_down, v_ffn2_w_down))
    shard = {n: wmv[0][0].astype(BF) for n, wmv in given.items()}

    small_in = jnp.concatenate([c.reshape(8, CHUNK), conv_w[0], jnp.zeros((1, CHUNK), F32)], axis=0)
    small_all = _allgather_vmem("gather_c_conv", small_in).reshape(NDEV, 40, CHUNK)
    c_all = small_all[:, :8, :].reshape(NDEV, D)
    conv_w_full = jnp.transpose(small_all[:, 8:, :], (1, 0, 2)).reshape(HALO, D)
    ada_cols = N_MOD * D // NDEV
    mod_part = _ada_fwd("ada_fwd", c_all, ada_w[0], lax.dynamic_slice(ada_b, (0, me * ada_cols), (1, ada_cols)))
    mod_all = _allgather_vmem("gather_mod", mod_part).reshape(NDEV, NDEV, ada_cols)
    mod = lax.dynamic_index_in_dim(mod_all, me, axis=1, keepdims=False).reshape(N_MOD, 1, D)
    sh1, sc1, g1, sh2, sc2, g2, sh3, sc3, g3 = [mod[i] for i in range(N_MOD)]
    wg1, wu1 = _exchange("gather_ffn1", _gather_rider([shard["ffn1_w_gate"], shard["ffn1_w_up"]]))

    x1, saved1, wd1, (w_in,) = _ffn_fwd("ffn1", x0, norm_ffn1, sh1, sc1, g1, wg1, wu1, shard["ffn1_w_down"],
                                         _gather_rider([shard["mix_w_in"]]))
    h2 = _norm_mod("mix_norm", x1, norm_mix, sc2, sh2)
    proj, (wg2,) = _mm_nn_blocked("mix_in", h2, w_in, tm, rider=_gather_rider([shard["ffn2_w_gate"]]))
    bias_full = jnp.repeat(sgu_b_s[0].T, CHUNK, axis=1)
    (ua,), (wa3, wb3) = _sgu("sgu", proj, mix_b_in, sgu_ln_g, sgu_ln_b, sgu_w_s[0], bias_full,
                             rider=_gather_rider([shard["w_branch_a"], shard["w_branch_b"]]))
    (z0, z1, z3), (wu2,) = _conv("conv", proj, mix_b_in, conv_w_full, conv_b, conv_ln_g, conv_ln_b,
                                 rider=_gather_rider([shard["ffn2_w_up"]]))
    wa, wb = wa3.reshape(D, D), wb3.reshape(D, D)
    ya = _mm_nn("branch_a", ua, wa, tm, 512, D)[0]
    yb = _mm_nn("branch_b", z3, wb, tm, 512, D)[0]
    (merged,), (wo3,) = _merge("merge", proj, mix_b_in, ya, yb, rider=_gather_rider([shard["w_out"]]))
    wo = wo3.reshape(D, D)

    def mix_epilogue(yv, xv, gv):
        return xv + gv * yv, yv

    tmo = min(t, 512)
    x2, y = _mm_nn("mix_out", merged, wo, tmo, D, D, extras=(x1, g2),
                   extra_specs=(pl.BlockSpec((tmo, D), lambda i, j, k: (i, 0)), pl.BlockSpec((1, D), lambda i, j, k: (0, 0))),
                   epilogue=mix_epilogue, out_dtypes=(F32, BF))
    x3, saved3, wd2, _ = _ffn_fwd("ffn2", x2, norm_ffn2, sh3, sc3, g3, wg2, wu2, shard["ffn2_w_down"], None)

    norm_final2 = norm_final.reshape(1, D)
    dx3, d_norm_final, loss_row = _loss_head("loss_head", x3, norm_final2, target)
    dx2, down2, sum_g2, sum_u2, (d_norm_ffn2, dsc3, dsh3, dg3), _, _, _ = _ffn_bwd(
        "ffn2", dx3, saved3, norm_ffn2, sh3, sc3, g3, wg2, wu2, wd2, slots)
    dy, dg2 = _gate_bwd("mix_gate_bwd", dx2, y, g2, 1.0)
    dm = _mm_nt("mix_out_bwd", dy, wo, tm, 512)[0]
    dwo = _mm_tn("mix_dwo", merged, dy, 512, D).reshape(NDEV, D // NDEV, D)
    dproj, dya, dyb, db_ga, db_gb = _merge_bwd("merge_bwd", proj, mix_b_in, ya, yb, dm)
    dua = _mm_nt("branch_a_bwd", dya, wa, tm, 512)[0]
    dwa = _mm_tn("branch_dwa", ua, dya, 512, D).reshape(NDEV, D // NDEV, D)
    dz3 = _mm_nt("branch_b_bwd", dyb, wb, tm, 512)[0]
    dwb = _mm_tn("branch_dwb", z3, dyb, 512, D).reshape(NDEV, D // NDEV, D)
    (dproj, db_u, db_v, d_sgu_g, d_sgu_b, d_ws, d_bs_t), sib_abo = _sgu_bwd(
        "sgu_bwd", proj, mix_b_in, sgu_ln_g, sgu_ln_b, sgu_w_s[0], bias_full, dua, dproj, rider=_pair_rider([dwa, dwb, dwo]))
    sum_a, sum_b, sum_o = _pair_add("mix_dw_add", [dwa, dwb, dwo], sib_abo, slots)
    (dproj, db_cv, db_cg, d_cw, d_cb, d_cln_g, d_cln_b), (got_g2, got_u2) = _conv_bwd(
        "conv_bwd", proj, mix_b_in, conv_w_full, conv_ln_g, conv_ln_b, z0, z1, dz3, dproj, rider=_chip_rider([sum_g2, sum_u2]))
    dwin, (got_a, got_b, got_o) = _mm_tn_blocked("mix_dwin", h2, dproj, rider=_chip_rider([sum_a, sum_b, sum_o]))
    dh2, (sib_in,) = _mm_nt_blocked("mix_in_bwd", [dproj], [w_in], tm, rider=_pair_rider([dwin]))
    (sum_in,) = _pair_add("mix_dwin_add", [dwin], [sib_in], slots)
    dx1, d_norm_mix, dsc2, dsh2 = _norm_mod_bwd("mix_norm_bwd", x1, norm_mix, sc2, sh2, dh2, dx2)

    d_bs = jnp.transpose(d_bs_t[:, :HEADS])
    zero = jnp.zeros((1, D), F32)
    pack_rows = [zero, zero, zero, dsh2, dsc2, dg2, dsh3, dsc3, dg3,
                 zero, d_norm_mix, d_norm_ffn2, d_norm_final,
                 db_u, db_v, db_cv, db_cg, db_ga, db_gb,
                 d_sgu_g, d_sgu_b, d_bs.reshape(1, D), d_cb, d_cln_g, d_cln_b,
                 d_cw[:KW], loss_row, jnp.zeros((R_TOTAL - R_LOSS - 1, D), F32)]
    packed = jnp.concatenate(pack_rows, axis=0)
    d_ws2 = d_ws.reshape(HEADS * CHUNK, CHUNK)
    dx0, down1, sum_g1, sum_u1, (d_norm_ffn1, dsc1, dsh1, dg1), (got_in_near,), (packed_all, dws_all), (got_in_far,) = _ffn_bwd(
        "ffn1", dx1, saved1, norm_ffn1, sh1, sc1, g1, wg1, wu1, wd1, slots,
        dact_rider=_chip_rider([sum_in], NEIGHBOURS), dwd_rider=_gather_rider([packed, d_ws2]),
        dwgu_rider=_chip_rider([sum_in], DIAGONAL))
    packed_late = jnp.concatenate([dsh1, dsc1, dg1, jnp.zeros((6, D), F32), d_norm_ffn1, jnp.zeros((R_LATE - 10, D), F32)], axis=0)
    grads = dict(ffn2_w_gate=(sum_g2, [got_g2]), ffn2_w_up=(sum_u2, [got_u2]), ffn2_w_down=down2,
                 mix_w_in=(sum_in, [got_in_near, got_in_far]), w_branch_a=(sum_a, [got_a]), w_branch_b=(sum_b, [got_b]),
                 w_out=(sum_o, [got_o]), ffn1_w_down=down1)
    done, (got_g1, got_u1, late_all) = _adamw_group(
        "adamw_most", [(cs, got, *[a[0] for a in given[n]]) for n, (cs, got) in grads.items()],
        rider=[_chip_rider([sum_g1, sum_u1]), _gather_rider([packed_late])])
    last, _ = _adamw_group("adamw_ffn1_in", [(sum_g1, [got_g1], *[a[0] for a in given["ffn1_w_gate"]]),
                                            (sum_u1, [got_u1], *[a[0] for a in given["ffn1_w_up"]])])
    big_out = {n: [o.reshape(given[n][0].shape) for o in outs]
               for n, outs in zip([*grads, "ffn1_w_gate", "ffn1_w_up"], [*done, *last])}

    flat = lambda a: a.reshape(1, -1)
    vectors = [("ada_b", 0, 9, ada_b, m_ada_b, v_ada_b), ("norm_ffn1", 9, 1, norm_ffn1, m_norm_ffn1, v_norm_ffn1),
               ("norm_mix", 10, 1, norm_mix, m_norm_mix, v_norm_mix), ("norm_ffn2", 11, 1, norm_ffn2, m_norm_ffn2, v_norm_ffn2),
               ("norm_final", 12, 1, norm_final, m_norm_final, v_norm_final), ("mix_b_in", 13, 6, mix_b_in, m_mix_b_in, v_mix_b_in),
               ("sgu_ln_g", 19, 1, sgu_ln_g, m_sgu_ln_g, v_sgu_ln_g), ("sgu_ln_b", 20, 1, sgu_ln_b, m_sgu_ln_b, v_sgu_ln_b),
               ("sgu_b_s", 21, 1, sgu_b_s, m_sgu_b_s, v_sgu_b_s), ("conv_b", 22, 1, conv_b, m_conv_b, v_conv_b),
               ("conv_ln_g", 23, 1, conv_ln_g, m_conv_ln_g, v_conv_ln_g), ("conv_ln_b", 24, 1, conv_ln_b, m_conv_ln_b, v_conv_ln_b)]
    small_out, d_cw_all, loss_sum = _adamw_small(
        "adamw_small", packed_all, late_all, dws_all, [(row, rows, flat(wv), flat(mv), flat(vv)) for _, row, rows, wv, mv, vv in vectors],
        [a.reshape(HEADS * CHUNK, CHUNK) for a in (sgu_w_s, m_sgu_w_s, v_sgu_w_s)])
    small = {n: [o.reshape(wv.shape) for o in outs] for (n, _, _, wv, _, _), outs in zip(vectors, small_out)}
    small["sgu_w_s"] = [o.reshape(sgu_w_s.shape) for o in small_out[-1]]
    g_cw = lax.dynamic_slice(d_cw_all, (0, me * CHUNK), (KW, CHUNK))
    small["conv_w"] = [o.reshape(conv_w.shape) for o in (g_cw, *_adamw_plain("adamw_conv_w", g_cw, conv_w[0], m_conv_w[0], v_conv_w[0]))]
    loss = loss_sum[0, 0]

    dmod_cols = [lax.dynamic_slice(a[:, :N_MOD, :].reshape(NDEV, N_MOD * D), (0, me * ada_cols), (NDEV, ada_cols))
                 for a in (packed_all, late_all)]
    ada_out = [o.reshape(ada_w.shape) for o in _adamw_ada("adamw_ada_w", jnp.transpose(c_all), *dmod_cols, ada_w[0], m_ada_w[0], v_ada_w[0])]

    order = ["ada_w", "ada_b", "norm_ffn1", "ffn1_w_gate", "ffn1_w_up", "ffn1_w_down", "norm_mix", "mix_w_in", "mix_b_in",
             "sgu_ln_g", "sgu_ln_b", "sgu_w_s", "sgu_b_s", "conv_w", "conv_b", "conv_ln_g", "conv_ln_b", "w_branch_a",
             "w_branch_b", "w_out", "norm_ffn2", "ffn2_w_gate", "ffn2_w_up", "ffn2_w_down", "norm_final"]

    def leaf(n, kind):
        if n == "ada_w":
            return ada_out[kind]
        if n in big_out:
            return big_out[n][kind]
        return small[n][kind]

    return (loss, dx0.reshape(x.shape), *[leaf(n, kind) for kind in range(4) for n in order])
```

```python
import jax
import jax.numpy as jnp
from jax import lax
from jax.experimental import pallas as pl
from jax.experimental.pallas import tpu as pltpu

D = 1024
F = 4 * D
D_IN = 6 * D
HEADS = 8
CHUNK = 128
KW = 31
HALO = 32
N_MOD = 9
NDEV = 8
N_CHIPS = 4
EPS = 1e-6
LR, B1, B2, ADAM_EPS, WD, STEP = 0.001, 0.9, 0.999, 1e-08, 0.01, 10
BC1 = 1.0 - B1 ** STEP
BC2 = 1.0 - B2 ** STEP
VMEM_LIMIT = 56 * 1024 * 1024
MESH = pl.DeviceIdType.MESH
HBM = pl.BlockSpec(memory_space=pltpu.HBM)
VMEM = pl.BlockSpec(memory_space=pltpu.VMEM)
BF = jnp.bfloat16
F32 = jnp.float32

NN = (((1,), (0,)), ((), ()))
NT = (((1,), (1,)), ((), ()))
TN = (((0,), (0,)), ((), ()))

R_CW, R_LOSS, R_TOTAL = 25, 56, 64
R_LATE = 16


def _params(sem):
    return pltpu.CompilerParams(dimension_semantics=sem, vmem_limit_bytes=VMEM_LIMIT)


def _position():
    return lax.axis_index("x"), lax.axis_index("y"), lax.axis_index("c")


def _flip(pos, k):
    x, y, c = pos
    return (x ^ (k >> 2 & 1), y ^ (k >> 1 & 1), c ^ (k & 1))


def _index(pos):
    return 4 * pos[0] + 2 * pos[1] + pos[2]


def _gather_rows(x_ref, out_ref, send_sems, recv_sems, local_sem):
    m_per = x_ref.shape[0]
    x, y, c = _position()
    me, sibling = (x, y, c), (x, y, 1 - c)
    chips = [(1 - x, y), (x, 1 - y), (1 - x, 1 - y)]

    def rows(pos):
        return out_ref.at[pl.ds(_index(pos) * m_per, m_per), :]

    def copy(k, block, to, src=None):
        return pltpu.make_async_remote_copy(
            src_ref=rows(block) if src is None else src, dst_ref=rows(block),
            send_sem=send_sems.at[k], recv_sem=recv_sems.at[k], device_id=to, device_id_type=MESH)

    mine = pltpu.make_async_copy(x_ref, rows(me), local_sem)
    mine.start()
    first = [copy(0, me, sibling, src=x_ref)]
    first += [copy(1 + j, me, (*chip, c), src=x_ref) for j, chip in enumerate(chips)]
    for cp in first:
        cp.start()
    passed = [copy(4 + j, (*chip, c), sibling) for j, chip in enumerate(chips)]
    for j, chip in enumerate(chips):
        copy(1 + j, (*chip, c), me).wait_recv()
        passed[j].start()
    copy(0, sibling, me).wait_recv()
    for j, chip in enumerate(chips):
        copy(4 + j, (*chip, 1 - c), me).wait_recv()
    for cp in first + passed:
        cp.wait_send()
    mine.wait()


def _prologue(name, c_rows, taps, ada_w, ada_b, shards):
    rider = _gather_rider(shards)
    n = len(shards)
    nc = ada_w.shape[1]

    def body(*refs):
        c_ref, taps_ref, w_ref, b_ref = refs[:4]
        shard_refs = refs[4:4 + n]
        c_all_ref, taps_all_ref, mod_all_ref = refs[4 + n:7 + n]
        gathered_refs = refs[7 + n:7 + 2 * n]
        c_buf, mod_part, sems = refs[7 + 2 * n], refs[8 + 2 * n], refs[9 + 2 * n:]
        rider.start(shard_refs, gathered_refs, sems[9:])
        _gather_rows(c_ref, c_buf, *sems[0:3])
        c_all = jnp.concatenate([c_buf[pl.ds(d * SUBLANES, 1), :] for d in range(NDEV)], axis=0)
        c_all_ref[...] = c_all
        mod_part[...] = jnp.dot(_silu(c_all), w_ref[...], preferred_element_type=F32) + b_ref[...]
        _gather_rows(taps_ref, taps_all_ref, *sems[3:6])
        _gather_rows(mod_part, mod_all_ref, *sems[6:9])
        rider.relay(shard_refs, gathered_refs, sems[9:])
        rider.finish(shard_refs, gathered_refs, sems[9:])

    small_sems = [pltpu.SemaphoreType.DMA((7,)), pltpu.SemaphoreType.DMA((7,)), pltpu.SemaphoreType.DMA] * 3
    res = pl.pallas_call(
        body, name=name,
        out_shape=[jax.ShapeDtypeStruct((NDEV, D), F32), jax.ShapeDtypeStruct((NDEV * taps.shape[0], taps.shape[1]), F32),
                   jax.ShapeDtypeStruct((NDEV * NDEV, nc), F32)] + rider.out_shapes,
        in_specs=[VMEM] * 4 + [HBM] * n, out_specs=[VMEM] * 3 + [HBM] * n,
        scratch_shapes=[pltpu.VMEM((NDEV * SUBLANES, D), F32), pltpu.VMEM((NDEV, nc), F32)] + small_sems + rider.sems,
        compiler_params=_params(None),
    )(c_rows, taps, ada_w, ada_b, *shards)
    return res[0], res[1], res[2], res[3:]


class _Rider:
    def __init__(self, ins, out_shapes, sems, start, finish, relay=None):
        self.ins, self.out_shapes, self.sems = list(ins), list(out_shapes), list(sems)
        self.start, self.finish, self.relay = start, finish, relay


def _gather_rider(shards):
    n = len(shards)

    def setup(ins, outs, sems):
        send_sems, recv_sems, local_sems = sems
        x, y, c = _position()
        me, sibling = (x, y, c), (x, y, 1 - c)
        chips = [(1 - x, y), (x, 1 - y), (1 - x, 1 - y)]

        def copy(a, k, block, to, own=False):
            slot = outs[a].at[_index(block)]
            return pltpu.make_async_remote_copy(
                src_ref=ins[a] if own else slot, dst_ref=slot,
                send_sem=send_sems.at[k, a], recv_sem=recv_sems.at[k, a], device_id=to, device_id_type=MESH)

        def own_copies():
            mine = [pltpu.make_async_copy(ins[a], outs[a].at[_index(me)], local_sems.at[a]) for a in range(n)]
            first = []
            for a in range(n):
                first.append(copy(a, 0, me, sibling, own=True))
                first += [copy(a, 1 + j, me, (*chip, c), own=True) for j, chip in enumerate(chips)]
            return mine, first

        return me, sibling, chips, c, copy, own_copies

    def start(ins, outs, sems):
        mine, first = setup(ins, outs, sems)[-1]()
        for cp in mine + first:
            cp.start()

    def relay(ins, outs, sems):
        me, sibling, chips, c, copy, _ = setup(ins, outs, sems)
        for j, chip in enumerate(chips):
            for a in range(n):
                copy(a, 1 + j, (*chip, c), me).wait_recv()
                copy(a, 4 + j, (*chip, c), sibling).start()

    def finish(ins, outs, sems):
        me, sibling, chips, c, copy, own_copies = setup(ins, outs, sems)
        mine, first = own_copies()
        passed = [copy(a, 4 + j, (*chip, c), sibling) for j, chip in enumerate(chips) for a in range(n)]
        for a in range(n):
            copy(a, 0, sibling, me).wait_recv()
            for j, chip in enumerate(chips):
                copy(a, 4 + j, (*chip, 1 - c), me).wait_recv()
        for cp in first + passed:
            cp.wait_send()
        for cp in mine:
            cp.wait()

    return _Rider(shards, [jax.ShapeDtypeStruct((NDEV, *s.shape), s.dtype) for s in shards],
                  [pltpu.SemaphoreType.DMA((7, n)), pltpu.SemaphoreType.DMA((7, n)), pltpu.SemaphoreType.DMA((n,))],
                  start, finish, relay)


def _pair_rider(parts):
    n = len(parts)

    def copies(ins, outs, sems):
        send_sems, recv_sems = sems
        x, y, c = _position()
        q = 2 * x + y
        return [pltpu.make_async_remote_copy(
            src_ref=ins[a].at[2 * (q ^ k) + (1 - c)], dst_ref=outs[a].at[k],
            send_sem=send_sems.at[k, a], recv_sem=recv_sems.at[k, a], device_id=(x, y, 1 - c), device_id_type=MESH)
            for a in range(n) for k in range(N_CHIPS)]

    def start(ins, outs, sems):
        for cp in copies(ins, outs, sems):
            cp.start()

    def finish(ins, outs, sems):
        for cp in copies(ins, outs, sems):
            cp.wait()

    return _Rider(parts, [jax.ShapeDtypeStruct((N_CHIPS, *p.shape[1:]), p.dtype) for p in parts],
                  [pltpu.SemaphoreType.DMA((N_CHIPS, n)), pltpu.SemaphoreType.DMA((N_CHIPS, n))], start, finish)


NEIGHBOURS = (1, 2)
DIAGONAL = (3,)
OTHER_CHIPS = NEIGHBOURS + DIAGONAL


def _chip_rider(sums, ks=OTHER_CHIPS):
    n = len(sums)

    def copies(ins, outs, sems):
        send_sems, recv_sems = sems
        me = _position()
        return [pltpu.make_async_remote_copy(
            src_ref=ins[a].at[k], dst_ref=outs[a].at[j],
            send_sem=send_sems.at[j, a], recv_sem=recv_sems.at[j, a], device_id=_flip(me, 2 * k), device_id_type=MESH)
            for a in range(n) for j, k in enumerate(ks)]

    def start(ins, outs, sems):
        for cp in copies(ins, outs, sems):
            cp.start()

    def finish(ins, outs, sems):
        for cp in copies(ins, outs, sems):
            cp.wait()

    return _Rider(sums, [jax.ShapeDtypeStruct((len(ks), *s.shape[1:]), s.dtype) for s in sums],
                  [pltpu.SemaphoreType.DMA((len(ks), n)), pltpu.SemaphoreType.DMA((len(ks), n))], start, finish)


def _grid_edge(grid, last):
    cond = None
    for d, n in enumerate(grid):
        here = pl.program_id(d) == (n - 1 if last else 0)
        cond = here if cond is None else jnp.logical_and(cond, here)
    return cond


def _call(name, compute, grid, ins, in_specs, out_shapes, out_specs, scratch_shapes, semantics, rider=None, aliases=None):
    riders = [rider] if isinstance(rider, _Rider) else list(rider or [])
    n_in, n_out, n_scr = len(ins), len(out_shapes), len(scratch_shapes)
    n_rin, n_rout, n_rsem = [sum(len(part(r)) for r in riders) for part in (lambda r: r.ins, lambda r: r.out_shapes, lambda r: r.sems)]
    cuts = [0, n_in, n_in + n_rin, n_in + n_rin + n_out, n_in + n_rin + n_out + n_rout, n_in + n_rin + n_out + n_rout + n_scr]

    def body(*refs):
        in_refs, rin_refs, out_refs, rout_refs, scr_refs = [refs[a:b] for a, b in zip(cuts[:-1], cuts[1:])]
        rsem_refs = refs[cuts[-1]:]
        mine, at = [], [0, 0, 0]
        for r in riders:
            mine.append((r, rin_refs[at[0]:at[0] + len(r.ins)], rout_refs[at[1]:at[1] + len(r.out_shapes)],
                         rsem_refs[at[2]:at[2] + len(r.sems)]))
            at = [at[0] + len(r.ins), at[1] + len(r.out_shapes), at[2] + len(r.sems)]
        if riders:
            @pl.when(_grid_edge(grid, last=False))
            def _():
                for r, a, b, c in mine:
                    r.start(a, b, c)

        if any(r.relay for r in riders):
            @pl.when(_grid_edge(grid, last=True))
            def _():
                for r, a, b, c in mine:
                    if r.relay:
                        r.relay(a, b, c)

        compute(in_refs, out_refs, scr_refs)
        if riders:
            @pl.when(_grid_edge(grid, last=True))
            def _():
                for r, a, b, c in mine:
                    r.finish(a, b, c)

    res = pl.pallas_call(
        body, name=name, grid=grid,
        out_shape=list(out_shapes) + [s for r in riders for s in r.out_shapes],
        in_specs=list(in_specs) + [HBM] * n_rin, out_specs=list(out_specs) + [HBM] * n_rout,
        scratch_shapes=list(scratch_shapes) + [s for r in riders for s in r.sems],
        input_output_aliases=aliases or {}, compiler_params=_params(semantics),
    )(*ins, *[a for r in riders for a in r.ins])
    return (res[:n_out], res[n_out:]) if riders else res


def _pair_add(name, parts, from_sibling, slots):
    n = len(parts)

    def body(s_ref, *refs):
        for a in range(n):
            refs[2 * n + a][...] = (refs[a][...].astype(F32) + refs[n + a][...].astype(F32)).astype(refs[2 * n + a].dtype)

    def slab(p, picked):
        _, r, c = p.shape
        return pl.BlockSpec((None, r, c), (lambda k, s: (s[k], 0, 0)) if picked else (lambda k, s: (k, 0, 0)))

    return pl.pallas_call(
        body, name=name,
        grid_spec=pltpu.PrefetchScalarGridSpec(
            num_scalar_prefetch=1, grid=(N_CHIPS,),
            in_specs=[slab(p, True) for p in parts] + [slab(p, False) for p in parts],
            out_specs=[slab(p, False) for p in parts]),
        out_shape=[jax.ShapeDtypeStruct((N_CHIPS, *p.shape[1:]), p.dtype) for p in parts],
        compiler_params=_params(("arbitrary",)),
    )(slots, *parts, *from_sibling)


def _mm(name, pairs, dims, grid, nk, out_shapes, out_specs, extras=(), extra_specs=(), epilogue=None, acc_shape=None, rider=None):
    n_pairs = len(pairs)

    def compute(ins, outs, scratch):
        def partial_sum():
            total = None
            for p in range(n_pairs):
                d = lax.dot_general(ins[2 * p][...], ins[2 * p + 1][...], dims, preferred_element_type=F32)
                total = d if total is None else total + d
            return total

        def finish(r):
            ex = [e[...] for e in ins[2 * n_pairs:]]
            res = epilogue(r, *ex) if epilogue is not None else (r,)
            for o, v in zip(outs, res):
                o[...] = v.astype(o.dtype)

        if nk == 1:
            finish(partial_sum())
        else:
            acc = scratch[0]
            k = pl.program_id(2)

            @pl.when(k == 0)
            def _():
                acc[...] = partial_sum()

            @pl.when(k > 0)
            def _():
                acc[...] += partial_sum()

            @pl.when(k == nk - 1)
            def _():
                finish(acc[...])

    operands, specs = [], []
    for a, a_spec, b, b_spec in pairs:
        operands += [a, b]
        specs += [a_spec, b_spec]
    return _call(name, compute, grid, operands + list(extras), specs + list(extra_specs), out_shapes, out_specs,
                 [pltpu.VMEM(acc_shape, F32)] if nk > 1 else [], ("parallel", "parallel", "arbitrary"), rider)


def _single(res, rider):
    return (res[0][0], res[1]) if rider else res[0]


def _silu(x):
    return x * jax.nn.sigmoid(x)


def _ffn_up(name, h, wg, wu, rider=None):
    t = h.shape[0]
    tm = min(t, 1024)
    nb = F // NDEV

    def compute(ins, outs, _):
        hv = ins[0][...]
        g = jnp.dot(hv, ins[1][...], preferred_element_type=F32)
        u = jnp.dot(hv, ins[2][...], preferred_element_type=F32)
        outs[0][...] = g.astype(BF)
        outs[1][...] = u.astype(BF)
        outs[2][...] = (_silu(g) * u).astype(BF)

    w_spec = pl.BlockSpec((None, D, nb), lambda i, j: (j, 0, 0))
    o_spec = pl.BlockSpec((tm, nb), lambda i, j: (i, j))
    return _call(name, compute, (t // tm, NDEV), [h, wg, wu], [pl.BlockSpec((tm, D), lambda i, j: (i, 0)), w_spec, w_spec],
                 [jax.ShapeDtypeStruct((t, F), BF)] * 3, [o_spec] * 3, [], ("parallel", "arbitrary"), rider)


def _mm_nn(name, a, b, tm, tn, tk, extras=(), extra_specs=(), epilogue=None, out_dtypes=(F32,), rider=None):
    m, kk = a.shape
    n = b.shape[1]
    nk = kk // tk
    return _mm(
        name, [(a, pl.BlockSpec((tm, tk), lambda i, j, k: (i, k)), b, pl.BlockSpec((tk, tn), lambda i, j, k: (k, j)))], NN,
        (m // tm, n // tn, nk), nk,
        [jax.ShapeDtypeStruct((m, n), dt) for dt in out_dtypes],
        [pl.BlockSpec((tm, tn), lambda i, j, k: (i, j))] * len(out_dtypes),
        extras, extra_specs, epilogue, (tm, tn), rider)


def _mm_nn_blocked(name, a, b3, tm, rider=None):
    m = a.shape[0]
    nb = b3.shape[2]
    return _single(_mm(
        name, [(a, pl.BlockSpec((tm, D), lambda i, j, k: (i, 0)), b3, pl.BlockSpec((None, D, nb), lambda i, j, k: (j, 0, 0)))], NN,
        (m // tm, NDEV, 1), 1,
        [jax.ShapeDtypeStruct((m, NDEV * nb), F32)], [pl.BlockSpec((tm, nb), lambda i, j, k: (i, j))], rider=rider), rider)


def _mm_nt(name, a, b, tm, tn, out_dtypes=(F32,), extras=(), extra_specs=(), epilogue=None, rider=None):
    m, kk = a.shape
    n = b.shape[0]
    return _mm(
        name, [(a, pl.BlockSpec((tm, kk), lambda i, j, k: (i, 0)), b, pl.BlockSpec((tn, kk), lambda i, j, k: (j, 0)))], NT,
        (m // tm, n // tn, 1), 1,
        [jax.ShapeDtypeStruct((m, n), dt) for dt in out_dtypes],
        [pl.BlockSpec((tm, tn), lambda i, j, k: (i, j))] * len(out_dtypes),
        extras, extra_specs, epilogue, rider=rider)


def _mm_nt_blocked(name, a_list, b3_list, tm, rider=None):
    m = a_list[0].shape[0]
    nb = b3_list[0].shape[2]
    pairs = [(a, pl.BlockSpec((tm, nb), lambda i, j, k: (i, k)), b3, pl.BlockSpec((None, D, nb), lambda i, j, k: (k, 0, 0)))
             for a, b3 in zip(a_list, b3_list)]
    return _single(_mm(name, pairs, NT, (m // tm, 1, NDEV), NDEV,
                       [jax.ShapeDtypeStruct((m, D), F32)], [pl.BlockSpec((tm, D), lambda i, j, k: (i, 0))],
                       acc_shape=(tm, D), rider=rider), rider)


def _mm_tn(name, a, b, tm, tn, rider=None):
    t, m = a.shape
    n = b.shape[1]
    return _single(_mm(
        name, [(a, pl.BlockSpec((t, tm), lambda i, j, k: (0, i)), b, pl.BlockSpec((t, tn), lambda i, j, k: (0, j)))], TN,
        (m // tm, n // tn, 1), 1,
        [jax.ShapeDtypeStruct((m, n), BF)], [pl.BlockSpec((tm, tn), lambda i, j, k: (i, j))], rider=rider), rider)


def _mm_tn_blocked(name, a, b, rider=None):
    t = a.shape[0]
    nb = b.shape[1] // NDEV
    return _single(_mm(
        name, [(a, pl.BlockSpec((t, D), lambda i, j, k: (0, 0)), b, pl.BlockSpec((t, nb), lambda i, j, k: (0, j)))], TN,
        (1, NDEV, 1), 1,
        [jax.ShapeDtypeStruct((NDEV, D, nb), BF)], [pl.BlockSpec((None, D, nb), lambda i, j, k: (j, 0, 0))], rider=rider), rider)


def _dw_gate_up(name, h, dgate, dup, rider=None):
    t = h.shape[0]
    nb = F // NDEV

    def compute(ins, outs, _):
        hv = ins[0][...]
        outs[0][...] = lax.dot_general(hv, ins[1][...], TN, preferred_element_type=F32).astype(BF)
        outs[1][...] = lax.dot_general(hv, ins[2][...], TN, preferred_element_type=F32).astype(BF)

    d_spec = pl.BlockSpec((t, nb), lambda j: (0, j))
    o_spec = pl.BlockSpec((None, D, nb), lambda j: (j, 0, 0))
    return _call(name, compute, (NDEV,), [h, dgate, dup], [pl.BlockSpec((t, D), lambda j: (0, 0)), d_spec, d_spec],
                 [jax.ShapeDtypeStruct((NDEV, D, nb), BF)] * 2, [o_spec] * 2, [], ("arbitrary",), rider)


def _rowcall(name, fn, ins, in_specs, n_row_out, out_shapes, out_specs, grid, scratch_shapes=(), rider=None, aliases=None):
    def accumulate(o, v, i):
        @pl.when(i == 0)
        def _():
            o[...] = v.astype(o.dtype)

        @pl.when(i > 0)
        def _():
            o[...] += v.astype(o.dtype)

    def compute(in_refs, out_refs, scr):
        i = pl.program_id(0)
        vals = fn(i, in_refs, scr)
        for idx, (o, v) in enumerate(zip(out_refs, vals)):
            if idx < n_row_out:
                o[...] = v.astype(o.dtype)
            else:
                accumulate(o, v, i)

    return _call(name, compute, (grid,), ins, in_specs, out_shapes, out_specs, list(scratch_shapes), ("arbitrary",), rider, aliases)


def _rows(tr, w=D, cb=0):
    return pl.BlockSpec((tr, w), lambda i: (i, cb))


def _whole(shape):
    nd = len(shape)
    return pl.BlockSpec(shape, lambda i: (0,) * nd)


def _vec(n=1):
    return jax.ShapeDtypeStruct((n, D), F32)


def _rms_mod(x, gain, sc, sh):
    y = x * lax.rsqrt(jnp.mean(x * x, axis=-1, keepdims=True) + EPS)
    return (y * gain) * (1.0 + sc) + sh


def _layer_norm(x, g, b):
    mu = jnp.mean(x, axis=-1, keepdims=True)
    var = jnp.mean(jnp.square(x - mu), axis=-1, keepdims=True)
    return (x - mu) * lax.rsqrt(var + EPS) * g + b


def _norm_mod(name, x, gain, sc, sh):
    t = x.shape[0]
    tr = min(t, 256)

    def fn(i, r, _):
        return [_rms_mod(r[0][...], r[1][...], r[2][...], r[3][...])]

    return _rowcall(name, fn, [x, gain, sc, sh], [_rows(tr)] + [_whole((1, D))] * 3, 1,
                    [jax.ShapeDtypeStruct((t, D), BF)], [_rows(tr)], t // tr)[0]


def _gate_grads(dx, f, g, scale):
    return scale * g * dx, jnp.sum(scale * dx * f.astype(F32), axis=0, keepdims=True)


def _norm_mod_bwd(name, x, gain, sc, sh, dh, dres, below=None):
    t = x.shape[0]
    tr = min(t, 256)

    def fn(i, r, _):
        _, vjp = jax.vjp(_rms_mod, r[0][...], r[1][...], r[2][...], r[3][...])
        dx, dgain, dsc, dsh = vjp(r[4][...])
        dx = dx + r[5][...]
        if below is None:
            return [dx, dgain, dsc, dsh]
        df, dg = _gate_grads(dx, r[6][...], r[7][...], below[2])
        return [dx, df, dgain, dsc, dsh, dg]

    ins, specs = [x, gain, sc, sh, dh, dres], [_rows(tr)] + [_whole((1, D))] * 3 + [_rows(tr)] * 2
    outs, out_specs = [jax.ShapeDtypeStruct((t, D), F32)], [_rows(tr)]
    if below is not None:
        ins, specs = ins + [below[0], below[1]], specs + [_rows(tr), _whole((1, D))]
        outs, out_specs = outs + [jax.ShapeDtypeStruct((t, D), BF)], out_specs + [_rows(tr)]
    n_vec = 3 if below is None else 4
    return _rowcall(name, fn, ins, specs, len(outs), outs + [_vec()] * n_vec, out_specs + [_whole((1, D))] * n_vec, t // tr)


def _sgu_pre(up, vp, bu, bv, ln_g, ln_b):
    return jax.nn.gelu(up + bu), _layer_norm(jax.nn.gelu(vp + bv), ln_g, ln_b)


def _causal(w_ref, h):
    rows = lax.broadcasted_iota(jnp.int32, (CHUNK, CHUNK), 0)
    cols = lax.broadcasted_iota(jnp.int32, (CHUNK, CHUNK), 1)
    return jnp.where(cols <= rows, w_ref[h], 0.0)


def _sgu(name, proj, b_in, ln_g, ln_b, w_s, bias_full, rider=None):
    t = proj.shape[0]

    def fn(i, r, _):
        u, v = _sgu_pre(r[0][...], r[1][...], r[2][...], r[3][...], r[4][...], r[5][...])
        vb = v.astype(BF)
        mixed = [jnp.dot(_causal(r[6], h).astype(BF), vb[:, h * CHUNK:(h + 1) * CHUNK], preferred_element_type=F32)
                 for h in range(HEADS)]
        return [u * (jnp.concatenate(mixed, axis=1) + r[7][...])]

    return _rowcall(
        name, fn, [proj, proj, b_in, b_in, ln_g, ln_b, w_s, bias_full],
        [_rows(CHUNK, D, 0), _rows(CHUNK, D, 1), pl.BlockSpec((1, D), lambda i: (0, 0)), pl.BlockSpec((1, D), lambda i: (0, 1)),
         _whole((1, D)), _whole((1, D)), _whole((HEADS, CHUNK, CHUNK)), _whole((CHUNK, D))],
        1, [jax.ShapeDtypeStruct((t, D), BF)], [_rows(CHUNK)], t // CHUNK, rider=rider)


def _sgu_bwd(name, proj, b_in, ln_g, ln_b, w_s, bias_full, dout, dproj, rider=None):
    t = proj.shape[0]

    def fn(i, r, _):
        (u, v), vjp = jax.vjp(_sgu_pre, r[0][...], r[1][...], r[2][...], r[3][...], r[4][...], r[5][...])
        vb = v.astype(BF)
        d = r[8][...]
        masks = [_causal(r[6], h).astype(BF) for h in range(HEADS)]
        cols = [slice(h * CHUNK, (h + 1) * CHUNK) for h in range(HEADS)]
        mixed = jnp.concatenate([jnp.dot(masks[h], vb[:, cols[h]], preferred_element_type=F32) for h in range(HEADS)], axis=1)
        du = d * (mixed + r[7][...])
        dmix = d * u
        dmb = dmix.astype(BF)
        dv = jnp.concatenate([lax.dot_general(masks[h], dmb[:, cols[h]], TN, preferred_element_type=F32) for h in range(HEADS)], axis=1)
        rows = lax.broadcasted_iota(jnp.int32, (CHUNK, CHUNK), 0)
        lanes = lax.broadcasted_iota(jnp.int32, (CHUNK, CHUNK), 1)
        dws = jnp.stack([jnp.where(lanes <= rows, lax.dot_general(dmb[:, cols[h]], vb[:, cols[h]], NT, preferred_element_type=F32), 0.0)
                         for h in range(HEADS)])
        dbs = jnp.zeros((CHUNK, CHUNK), F32)
        for h in range(HEADS):
            dbs = dbs + jnp.where(lanes == h, jnp.sum(dmix[:, cols[h]], axis=1, keepdims=True), 0.0)
        dup, dvp, dbu, dbv, dg, db = vjp((du, dv))
        return [jnp.concatenate([dup, dvp], axis=1), dbu, dbv, dg, db, dws, dbs]

    return _rowcall(
        name, fn, [proj, proj, b_in, b_in, ln_g, ln_b, w_s, bias_full, dout, dproj],
        [_rows(CHUNK, D, 0), _rows(CHUNK, D, 1), pl.BlockSpec((1, D), lambda i: (0, 0)), pl.BlockSpec((1, D), lambda i: (0, 1)),
         _whole((1, D)), _whole((1, D)), _whole((HEADS, CHUNK, CHUNK)), _whole((CHUNK, D)), _rows(CHUNK),
         pl.BlockSpec(memory_space=pl.ANY)],
        1, [jax.ShapeDtypeStruct(dproj.shape, dproj.dtype)] + [_vec()] * 4
        + [jax.ShapeDtypeStruct((HEADS, CHUNK, CHUNK), F32), jax.ShapeDtypeStruct((CHUNK, CHUNK), F32)],
        [pl.BlockSpec((CHUNK, 2 * D), lambda i: (i, 0))] + [_whole((1, D))] * 4 + [_whole((HEADS, CHUNK, CHUNK)), _whole((CHUNK, CHUNK))],
        t // CHUNK, rider=rider, aliases={9: 0})


def _halo_before(tr, cb):
    return pl.BlockSpec((HALO, D), lambda i: (jnp.maximum(i * (tr // HALO) - 1, 0), cb))


def _halo_after(tr, cb, n_tiles):
    return pl.BlockSpec((HALO, D), lambda i: (jnp.minimum((i + 1) * (tr // HALO), n_tiles * (tr // HALO) - 1), cb))


def _ln_silu(z, g, b):
    return _silu(_layer_norm(z, g, b))


SUBLANES = 8
LANES = 128
CONV_STRIP = 16
DW_STRIP = 32


def _shifted_copies(buf, copies, rows):
    for b in range(1, SUBLANES):
        copies[b - 1, pl.ds(0, rows), :] = buf[pl.ds(b, rows), :]


def _shifted(buf, copies, offset, start, rows, lanes=slice(None)):
    at = pl.ds(pl.multiple_of(start + SUBLANES * (offset // SUBLANES), SUBLANES), rows)
    return buf[at, lanes] if offset % SUBLANES == 0 else copies[offset % SUBLANES - 1, at, lanes]


def _accumulate(o, v, i):
    @pl.when(i == 0)
    def _():
        o[...] = v.astype(o.dtype)

    @pl.when(i > 0)
    def _():
        o[...] += v.astype(o.dtype)


def _conv(name, proj, b_in, conv_w, conv_b, ln_g, ln_b, rider=None):
    t = proj.shape[0]
    tr = min(t, 256)

    def compute(r, outs, scr):
        zbuf, zs = scr
        i = pl.program_id(0)
        bv, bg = r[4][...], r[5][...]
        z0 = (r[0][...] + bv) * jax.nn.sigmoid(r[1][...] + bg)
        before = (r[2][...] + bv) * jax.nn.sigmoid(r[3][...] + bg)
        zbuf[pl.ds(0, HALO), :] = jnp.where(i > 0, before, 0.0)
        zbuf[pl.ds(HALO, tr), :] = z0
        outs[0][...] = z0
        _shifted_copies(zbuf, zs, tr + HALO - SUBLANES)

        def strip(s, carry):
            r0 = s * CONV_STRIP
            acc = jnp.zeros((CONV_STRIP, D), F32) + r[7][...]
            for k in range(KW):
                acc = acc + r[6][k:k + 1, :] * _shifted(zbuf, zs, HALO - (KW - 1) + k, r0, CONV_STRIP)
            outs[1][pl.ds(pl.multiple_of(r0, SUBLANES), CONV_STRIP), :] = acc
            return carry

        lax.fori_loop(0, tr // CONV_STRIP, strip, 0)
        outs[2][...] = _ln_silu(outs[1][...], r[8][...], r[9][...]).astype(BF)

    return _call(
        name, compute, (t // tr,), [proj, proj, proj, proj, b_in, b_in, conv_w, conv_b, ln_g, ln_b],
        [_rows(tr, D, 2), _rows(tr, D, 3), _halo_before(tr, 2), _halo_before(tr, 3),
         pl.BlockSpec((1, D), lambda i: (0, 2)), pl.BlockSpec((1, D), lambda i: (0, 3)),
         _whole((HALO, D)), _whole((1, D)), _whole((1, D)), _whole((1, D))],
        [jax.ShapeDtypeStruct((t, D), F32), jax.ShapeDtypeStruct((t, D), F32), jax.ShapeDtypeStruct((t, D), BF)],
        [_rows(tr)] * 3, [pltpu.VMEM((tr + HALO, D), F32), pltpu.VMEM((SUBLANES - 1, tr + HALO, D), F32)], ("arbitrary",), rider)


def _conv_bwd(name, proj, b_in, conv_w, ln_g, ln_b, z0, z1, dz3, dproj, rider=None):
    t = proj.shape[0]
    tr = min(t, 256)
    n_tiles = t // tr

    def compute(r, outs, scr):
        zbuf, dbuf, zs, ds, dwacc = scr
        i = pl.program_id(0)
        g, b = r[5][...], r[6][...]
        _, vjp = jax.vjp(_ln_silu, r[9][...], g, b)
        dz1, dg, db = vjp(r[11][...])
        _, vjp_after = jax.vjp(_ln_silu, r[10][...], g, b)
        dz1_after = vjp_after(r[12][...])[0]
        dbuf[pl.ds(0, tr), :] = dz1
        dbuf[pl.ds(tr, HALO), :] = jnp.where(i < n_tiles - 1, dz1_after, 0.0)
        zbuf[pl.ds(0, HALO), :] = jnp.where(i > 0, r[8][...], 0.0)
        zbuf[pl.ds(HALO, tr), :] = r[7][...]
        _shifted_copies(dbuf, ds, tr + HALO - SUBLANES)
        _shifted_copies(zbuf, zs, tr + HALO - SUBLANES)

        def dz0_strip(s, carry):
            r0 = s * CONV_STRIP
            at = pl.ds(pl.multiple_of(r0, CONV_STRIP), CONV_STRIP)
            acc = jnp.zeros((CONV_STRIP, D), F32)
            for k in range(KW):
                acc = acc + r[4][k:k + 1, :] * _shifted(dbuf, ds, KW - 1 - k, r0, CONV_STRIP)
            a = r[0][at, :] + r[2][...]
            sg = jax.nn.sigmoid(r[1][at, :] + r[3][...])
            dcv = acc * sg
            dcg = acc * a * sg * (1.0 - sg)
            outs[0][at, :] = jnp.concatenate([dcv, dcg], axis=1).astype(BF)
            return carry[0] + jnp.sum(dcv, axis=0, keepdims=True), carry[1] + jnp.sum(dcg, axis=0, keepdims=True)

        zero_row = jnp.zeros((1, D), F32)
        dbv, dbg = lax.fori_loop(0, tr // CONV_STRIP, dz0_strip, (zero_row, zero_row))

        for lb in range(D // LANES):
            lanes = slice(lb * LANES, (lb + 1) * LANES)

            def dw_strip(s, accs, lanes=lanes):
                r0 = s * DW_STRIP
                dz = dbuf[pl.ds(pl.multiple_of(r0, SUBLANES), DW_STRIP), lanes]
                out = []
                for k in range(KW):
                    prod = dz * _shifted(zbuf, zs, HALO - (KW - 1) + k, r0, DW_STRIP, lanes)
                    part = prod[0:SUBLANES]
                    for q in range(1, DW_STRIP // SUBLANES):
                        part = part + prod[q * SUBLANES:(q + 1) * SUBLANES]
                    out.append(accs[k] + part)
                return tuple(out)

            accs = lax.fori_loop(0, tr // DW_STRIP, dw_strip, tuple(jnp.zeros((SUBLANES, LANES), F32) for _ in range(KW)))
            for k in range(KW):
                dwacc[pl.ds(k * SUBLANES, SUBLANES), lanes] = accs[k]
        dw_rows = [jnp.sum(dwacc[pl.ds(k * SUBLANES, SUBLANES), :], axis=0, keepdims=True) for k in range(KW)]
        dw_rows.append(jnp.zeros((HALO - KW, D), F32))
        for o, v in zip(outs[1:], (dbv, dbg, jnp.concatenate(dw_rows, axis=0), jnp.sum(dz1, axis=0, keepdims=True), dg, db)):
            _accumulate(o, v, i)

    wide = pl.BlockSpec((tr, 2 * D), lambda i: (i, 1))
    return _call(
        name, compute, (n_tiles,), [proj, proj, b_in, b_in, conv_w, ln_g, ln_b, z0, z0, z1, z1, dz3, dz3, dproj],
        [_rows(tr, D, 2), _rows(tr, D, 3), pl.BlockSpec((1, D), lambda i: (0, 2)), pl.BlockSpec((1, D), lambda i: (0, 3)),
         _whole((HALO, D)), _whole((1, D)), _whole((1, D)),
         _rows(tr), _halo_before(tr, 0), _rows(tr), _halo_after(tr, 0, n_tiles), _rows(tr), _halo_after(tr, 0, n_tiles),
         pl.BlockSpec(memory_space=pl.ANY)],
        [jax.ShapeDtypeStruct(dproj.shape, dproj.dtype), _vec(), _vec(), _vec(HALO), _vec(), _vec(), _vec()],
        [wide] + [_whole((1, D))] * 2 + [_whole((HALO, D))] + [_whole((1, D))] * 3,
        [pltpu.VMEM((tr + HALO, D), F32), pltpu.VMEM((tr + HALO, D), F32),
         pltpu.VMEM((SUBLANES - 1, tr + HALO, D), F32), pltpu.VMEM((SUBLANES - 1, tr + HALO, D), F32),
         pltpu.VMEM((HALO * SUBLANES, D), F32)],
        ("arbitrary",), rider, aliases={13: 0})


def _merge_fn(ga, gb, bga, bgb, ya, yb):
    return jax.nn.sigmoid(ga + bga) * ya + jax.nn.sigmoid(gb + bgb) * yb


def _merge(name, proj, b_in, ya, yb, rider=None):
    t = proj.shape[0]
    tr = min(t, 256)

    def fn(i, r, _):
        return [_merge_fn(*[x[...] for x in r])]

    return _rowcall(
        name, fn, [proj, proj, b_in, b_in, ya, yb],
        [_rows(tr, D, 4), _rows(tr, D, 5), pl.BlockSpec((1, D), lambda i: (0, 4)), pl.BlockSpec((1, D), lambda i: (0, 5)),
         _rows(tr), _rows(tr)],
        1, [jax.ShapeDtypeStruct((t, D), BF)], [_rows(tr)], t // tr, rider=rider)


def _merge_bwd(name, proj, b_in, ya, yb, dm):
    t = proj.shape[0]
    tr = min(t, 256)

    def fn(i, r, _):
        _, vjp = jax.vjp(_merge_fn, *[x[...] for x in r[:6]])
        dga, dgb, dbga, dbgb, dya, dyb = vjp(r[6][...])
        return [jnp.concatenate([dga, dgb], axis=1), dya, dyb, dbga, dbgb]

    return _rowcall(
        name, fn, [proj, proj, b_in, b_in, ya, yb, dm],
        [_rows(tr, D, 4), _rows(tr, D, 5), pl.BlockSpec((1, D), lambda i: (0, 4)), pl.BlockSpec((1, D), lambda i: (0, 5)),
         _rows(tr), _rows(tr), _rows(tr)],
        3, [jax.ShapeDtypeStruct((t, D_IN), BF)] + [jax.ShapeDtypeStruct((t, D), BF)] * 2 + [_vec(), _vec()],
        [pl.BlockSpec((tr, 2 * D), lambda i: (i, 2))] + [_rows(tr)] * 2 + [_whole((1, D))] * 2, t // tr)


def _loss_head(name, x, gain, target, f, g, scale):
    t = x.shape[0]
    tr = min(t, 256)

    def loss_fn(xv, gn, tgt):
        y = xv * lax.rsqrt(jnp.mean(xv * xv, axis=-1, keepdims=True) + EPS) * gn
        return 0.5 * jnp.sum(jnp.mean(jnp.square(y - tgt), axis=-1))

    def fn(i, r, _):
        loss, vjp = jax.vjp(loss_fn, r[0][...], r[1][...], r[2][...])
        dx, dgain, _ = vjp(jnp.ones((), F32))
        df, dg = _gate_grads(dx, r[3][...], r[4][...], scale)
        return [dx, df, dgain, jnp.zeros((1, D), F32) + loss, dg]

    return _rowcall(name, fn, [x, gain, target, f, g], [_rows(tr), _whole((1, D)), _rows(tr), _rows(tr), _whole((1, D))], 2,
                    [jax.ShapeDtypeStruct((t, D), F32), jax.ShapeDtypeStruct((t, D), BF), _vec(), _vec(), _vec()],
                    [_rows(tr)] * 2 + [_whole((1, D))] * 3, t // tr)


def _adamw(w, g, m, v):
    m = B1 * m + (1.0 - B1) * g
    v = B2 * v + (1.0 - B2) * jnp.square(g)
    m_hat = m / BC1
    v_hat = v / BC2
    delta = -LR * (m_hat / (jnp.sqrt(v_hat) + ADAM_EPS) + WD * w)
    return delta, m, v


ADAMW_ROWS = 64


def _adamw_group(name, items, rider=None):
    ins, in_specs, out_shapes, out_specs, plan = [], [], [], [], []
    first = 0
    for chip_sum, received, w, m, v in items:
        r, c = w.shape
        tr = min(r, ADAMW_ROWS)
        n = r // tr

        def tile(i, first=first, n=n):
            return jnp.clip(i - first, 0, n - 1)

        spec = pl.BlockSpec((tr, c), lambda i, tile=tile: (tile(i), 0))
        ins += [chip_sum, *received, w, m, v]
        in_specs += [pl.BlockSpec((None, tr, c), lambda i, tile=tile: (0, tile(i), 0))]
        in_specs += [pl.BlockSpec((g.shape[0], tr, c), lambda i, tile=tile: (0, tile(i), 0)) for g in received]
        in_specs += [spec] * 3
        out_shapes += [jax.ShapeDtypeStruct((r, c), F32)] * 4
        out_specs += [spec] * 4
        plan.append((first, n, [g.shape[0] for g in received]))
        first += n

    def compute(in_refs, out_refs, _):
        i = pl.program_id(0)
        at_in = at_out = 0
        for start, n, counts in plan:
            mine = in_refs[at_in:at_in + 4 + len(counts)]
            outs = out_refs[at_out:at_out + 4]
            at_in += 4 + len(counts)
            at_out += 4

            @pl.when(jnp.logical_and(i >= start, i < start + n))
            def _(mine=mine, outs=outs, counts=counts):
                g = mine[0][...].astype(F32)
                for j, count in enumerate(counts):
                    for s in range(count):
                        g = g + mine[1 + j][s].astype(F32)
                delta, m_new, v_new = _adamw(mine[-3][...], g, mine[-2][...], mine[-1][...])
                for o, val in zip(outs, (g, delta, m_new, v_new)):
                    o[...] = val

    res = _call(name, compute, (first,), ins, in_specs, out_shapes, out_specs, [], ("arbitrary",), rider)
    outs, rode = res if rider else (res, [])
    return [outs[4 * j:4 * j + 4] for j in range(len(items))], rode


def _adamw_small(name, packed_all, late_all, dws_all, vectors, w_s):
    n_vec = len(vectors)

    def body(*refs):
        p_ref, l_ref, d_ref = refs[:3]
        param_refs = refs[3:3 + 3 * n_vec + 3]
        out_refs = refs[3 + 3 * n_vec + 3:-1]
        g_ref = refs[-1]
        g = p_ref[0]
        late = l_ref[0]
        for s in range(1, NDEV):
            g = g + p_ref[s]
            late = late + l_ref[s]
        g_ref[...] = g
        g_ref[pl.ds(0, R_LATE), :] += late

        def update(gp, wmv, outs):
            delta, m_new, v_new = _adamw(wmv[0][...], gp, wmv[1][...], wmv[2][...])
            for o, val in zip(outs, (gp, delta, m_new, v_new)):
                o[...] = val

        for j, (row, rows, *_) in enumerate(vectors):
            pieces = [g_ref[pl.ds(row + r, 1), :] for r in range(rows)]
            update(pieces[0] if rows == 1 else jnp.concatenate(pieces, axis=1), param_refs[3 * j:3 * j + 3], out_refs[4 * j:4 * j + 4])
        gw = d_ref[0]
        for s in range(1, NDEV):
            gw = gw + d_ref[s]
        update(gw, param_refs[3 * n_vec:], out_refs[4 * n_vec:4 * n_vec + 4])
        out_refs[-2][...] = g_ref[pl.ds(R_CW, KW), :]
        out_refs[-1][...] = g_ref[pl.ds(R_LOSS, 1), :]

    params = [a for _, _, w, m, v in vectors for a in (w, m, v)] + list(w_s)
    out_shapes = [jax.ShapeDtypeStruct(w.shape, F32) for _, _, w, _, _ in vectors for _ in range(4)]
    out_shapes += [jax.ShapeDtypeStruct(w_s[0].shape, F32)] * 4 + [jax.ShapeDtypeStruct((KW, D), F32), _vec()]
    res = pl.pallas_call(body, name=name, out_shape=out_shapes, scratch_shapes=[pltpu.VMEM((R_TOTAL, D), F32)],
                         compiler_params=_params(None))(packed_all, late_all, dws_all, *params)
    return [res[4 * j:4 * j + 4] for j in range(n_vec + 1)], res[-2], res[-1]


def _adamw_plain(name, g, w, m, v):
    def body(g_ref, w_ref, m_ref, v_ref, d_ref, mo_ref, vo_ref):
        delta, m_new, v_new = _adamw(w_ref[...], g_ref[...], m_ref[...], v_ref[...])
        d_ref[...] = delta
        mo_ref[...] = m_new
        vo_ref[...] = v_new

    return pl.pallas_call(body, name=name, out_shape=[jax.ShapeDtypeStruct(w.shape, F32)] * 3,
                          compiler_params=_params(None))(g, w, m, v)


def _adamw_ada(name, c_all_t, dmod, dmod_late, w, m, v):
    r, c = w.shape
    tr = 256

    def fn(i, refs, _):
        ca = _silu(refs[0][...])
        dm = refs[1][...] + refs[2][...]
        g = ca[:, 0:1] * dm[0:1, :]
        for b in range(1, NDEV):
            g = g + ca[:, b:b + 1] * dm[b:b + 1, :]
        delta, m_new, v_new = _adamw(refs[3][...], g, refs[4][...], refs[5][...])
        return [g, delta, m_new, v_new]

    spec = pl.BlockSpec((tr, c), lambda i: (i, 0))
    whole = pl.BlockSpec((NDEV, c), lambda i: (0, 0))
    return _rowcall(name, fn, [c_all_t, dmod, dmod_late, w, m, v],
                    [pl.BlockSpec((tr, NDEV), lambda i: (i, 0)), whole, whole, spec, spec, spec], 4,
                    [jax.ShapeDtypeStruct((r, c), F32)] * 4, [spec] * 4, r // tr)


def _ffn_fwd(tag, x, h, g, wg, wu, wd_shard, down_rider, next_norm=None):
    t = x.shape[0]
    tm = min(t, 512)
    (gate, up, act), (wd,) = _ffn_up(f"{tag}_up", h, wg, wu, rider=_gather_rider([wd_shard]))
    row = pl.BlockSpec((1, D), lambda i, j, k: (0, 0))

    def epilogue(f, xv, gv, *norm):
        x_out = xv + 0.5 * gv * f
        return (x_out, f, _rms_mod(x_out, *norm)) if norm else (x_out, f)

    res = _mm_nn(f"{tag}_down", act, wd.reshape(F, D), tm, D, 1024, extras=(x, g, *(next_norm or ())),
                 extra_specs=(pl.BlockSpec((tm, D), lambda i, j, k: (i, 0)), row, *([row] * 3 if next_norm else [])),
                 epilogue=epilogue, out_dtypes=(F32, BF, BF) if next_norm else (F32, BF), rider=down_rider)
    (x_out, f, *h_next), rode = res if down_rider else (res, None)
    return x_out, (h_next[0] if next_norm else None), (x, h, gate, up, act, f), wd, rode


def _ffn_bwd(tag, dx_out, df, saved, gain, sh, sc, wg, wu, wd, slots, dact_rider=None, dwd_rider=None, dwgu_rider=None,
             below=None):
    x, h, gate, up, act, f = saved
    t = x.shape[0]
    tm = min(t, 1024)

    def act_bwd(da, gv, uv):
        gv = gv.astype(F32)
        s = jax.nn.sigmoid(gv)
        return da * uv.astype(F32) * (s * (1.0 + gv * (1.0 - s))), da * (gv * s)

    blk = pl.BlockSpec((tm, F // NDEV), lambda i, j, k: (i, j))
    res = _mm_nt(f"{tag}_dact", df, wd.reshape(F, D), tm, F // NDEV, out_dtypes=(BF, BF),
                 extras=(gate, up), extra_specs=(blk, blk), epilogue=act_bwd, rider=dact_rider)
    (dgate, dup), rode_dact = res if dact_rider else (res, [])
    res = _mm_tn(f"{tag}_dwd", act, df, 512, D, rider=dwd_rider)
    dwd, rode_dwd = res if dwd_rider else (res, [])
    dwd = dwd.reshape(NDEV, F // NDEV, D)
    (dwg, dwu), (sib_d, *rode_dwgu) = _dw_gate_up(f"{tag}_dwgu", h, dgate, dup,
                                                  rider=[_pair_rider([dwd])] + ([dwgu_rider] if dwgu_rider else []))
    (sum_d,) = _pair_add(f"{tag}_dwd_add", [dwd], [sib_d], slots)
    dh, (sib_g, sib_u, got_d) = _mm_nt_blocked(f"{tag}_dh", [dgate, dup], [wg, wu], tm,
                                               rider=[_pair_rider([dwg, dwu]), _chip_rider([sum_d])])
    sum_g, sum_u = _pair_add(f"{tag}_dwgu_add", [dwg, dwu], [sib_g, sib_u], slots)
    normed = _norm_mod_bwd(f"{tag}_norm_bwd", x, gain, sc, sh, dh, dx_out, below=below)
    return normed, (sum_d, [got_d]), sum_g, sum_u, rode_dact, rode_dwd, rode_dwgu


def kernel(x, c, ada_w, ada_b, norm_ffn1, ffn1_w_gate, ffn1_w_up, ffn1_w_down, norm_mix, mix_w_in, mix_b_in, sgu_ln_g, sgu_ln_b, sgu_w_s, sgu_b_s, conv_w, conv_b, conv_ln_g, conv_ln_b, w_branch_a, w_branch_b, w_out, norm_ffn2, ffn2_w_gate, ffn2_w_up, ffn2_w_down, norm_final, loss_target, m_ada_w, m_ada_b, m_norm_ffn1, m_ffn1_w_gate, m_ffn1_w_up, m_ffn1_w_down, m_norm_mix, m_mix_w_in, m_mix_b_in, m_sgu_ln_g, m_sgu_ln_b, m_sgu_w_s, m_sgu_b_s, m_conv_w, m_conv_b, m_conv_ln_g, m_conv_ln_b, m_w_branch_a, m_w_branch_b, m_w_out, m_norm_ffn2, m_ffn2_w_gate, m_ffn2_w_up, m_ffn2_w_down, m_norm_final, v_ada_w, v_ada_b, v_norm_ffn1, v_ffn1_w_gate, v_ffn1_w_up, v_ffn1_w_down, v_norm_mix, v_mix_w_in, v_mix_b_in, v_sgu_ln_g, v_sgu_ln_b, v_sgu_w_s, v_sgu_b_s, v_conv_w, v_conv_b, v_conv_ln_g, v_conv_ln_b, v_w_branch_a, v_w_branch_b, v_w_out, v_norm_ffn2, v_ffn2_w_gate, v_ffn2_w_up, v_ffn2_w_down, v_norm_final):
    mx, my, mc = _position()
    me = 4 * mx + 2 * my + mc
    chip = 2 * mx + my
    slots = jnp.stack([2 * (chip ^ k) + mc for k in range(N_CHIPS)]).astype(jnp.int32)
    t = x.shape[1]
    tm = min(t, 1024)
    x0 = x.reshape(t, D)
    target = loss_target.reshape(t, D)
    given = dict(ffn1_w_gate=(ffn1_w_gate, m_ffn1_w_gate, v_ffn1_w_gate), ffn1_w_up=(ffn1_w_up, m_ffn1_w_up, v_ffn1_w_up),
                 ffn1_w_down=(ffn1_w_down, m_ffn1_w_down, v_ffn1_w_down), mix_w_in=(mix_w_in, m_mix_w_in, v_mix_w_in),
                 w_branch_a=(w_branch_a, m_w_branch_a, v_w_branch_a), w_branch_b=(w_branch_b, m_w_branch_b, v_w_branch_b),
                 w_out=(w_out, m_w_out, v_w_out), ffn2_w_gate=(ffn2_w_gate, m_ffn2_w_gate, v_ffn2_w_gate),
                 ffn2_w_up=(ffn2_w_up, m_ffn2_w_up, v_ffn2_w_up), ffn2_w_down=(ffn2_w_down, m_ffn2_w_down, v_ffn2_w_down))
    shard = {n: wmv[0][0].astype(BF) for n, wmv in given.items()}

    ada_cols = N_MOD * D // NDEV
    c_all, taps_all, mod_all, (wg1, wu1) = _prologue(
        "prologue", jnp.pad(c, ((0, SUBLANES - 1), (0, 0))), jnp.pad(conv_w[0], ((0, HALO - KW), (0, 0))), ada_w[0],
        lax.dynamic_slice(ada_b, (0, me * ada_cols), (1, ada_cols)), [shard["ffn1_w_gate"], shard["ffn1_w_up"]])
    conv_w_full = jnp.transpose(taps_all.reshape(NDEV, HALO, CHUNK), (1, 0, 2)).reshape(HALO, D)
    mod = lax.dynamic_index_in_dim(mod_all.reshape(NDEV, NDEV, ada_cols), me, axis=1, keepdims=False).reshape(N_MOD, 1, D)
    sh1, sc1, g1, sh2, sc2, g2, sh3, sc3, g3 = [mod[i] for i in range(N_MOD)]

    h1 = _norm_mod("ffn1_norm", x0, norm_ffn1, sc1, sh1)
    x1, h2, saved1, wd1, (w_in,) = _ffn_fwd("ffn1", x0, h1, g1, wg1, wu1, shard["ffn1_w_down"],
                                             _gather_rider([shard["mix_w_in"]]), next_norm=(norm_mix, sc2, sh2))
    proj, (wg2,) = _mm_nn_blocked("mix_in", h2, w_in, tm, rider=_gather_rider([shard["ffn2_w_gate"]]))
    bias_full = jnp.repeat(sgu_b_s[0].T, CHUNK, axis=1)
    (ua,), (wa3, wb3) = _sgu("sgu", proj, mix_b_in, sgu_ln_g, sgu_ln_b, sgu_w_s[0], bias_full,
                             rider=_gather_rider([shard["w_branch_a"], shard["w_branch_b"]]))
    (z0, z1, z3), (wu2,) = _conv("conv", proj, mix_b_in, conv_w_full, conv_b, conv_ln_g, conv_ln_b,
                                 rider=_gather_rider([shard["ffn2_w_up"]]))
    wa, wb = wa3.reshape(D, D), wb3.reshape(D, D)
    ya = _mm_nn("branch_a", ua, wa, tm, 512, D)[0]
    yb = _mm_nn("branch_b", z3, wb, tm, 512, D)[0]
    (merged,), (wo3,) = _merge("merge", proj, mix_b_in, ya, yb, rider=_gather_rider([shard["w_out"]]))
    wo = wo3.reshape(D, D)

    def mix_epilogue(yv, xv, gv, gain, sc, sh):
        x_out = xv + gv * yv
        return x_out, yv, _rms_mod(x_out, gain, sc, sh)

    tmo = min(t, 512)
    row = pl.BlockSpec((1, D), lambda i, j, k: (0, 0))
    x2, y, h3 = _mm_nn("mix_out", merged, wo, tmo, D, D, extras=(x1, g2, norm_ffn2, sc3, sh3),
                       extra_specs=(pl.BlockSpec((tmo, D), lambda i, j, k: (i, 0)), row, row, row, row),
                       epilogue=mix_epilogue, out_dtypes=(F32, BF, BF))
    x3, _, saved3, wd2, _ = _ffn_fwd("ffn2", x2, h3, g3, wg2, wu2, shard["ffn2_w_down"], None)

    norm_final2 = norm_final.reshape(1, D)
    dx3, df3, d_norm_final, loss_row, dg3 = _loss_head("loss_head", x3, norm_final2, target, saved3[-1], g3, 0.5)
    (dx2, dy, d_norm_ffn2, dsc3, dsh3, dg2), down2, sum_g2, sum_u2, _, _, _ = _ffn_bwd(
        "ffn2", dx3, df3, saved3, norm_ffn2, sh3, sc3, wg2, wu2, wd2, slots, below=(y, g2, 1.0))
    dm = _mm_nt("mix_out_bwd", dy, wo, tm, 512)[0]
    dwo = _mm_tn("mix_dwo", merged, dy, 512, D).reshape(NDEV, D // NDEV, D)
    dproj, dya, dyb, db_ga, db_gb = _merge_bwd("merge_bwd", proj, mix_b_in, ya, yb, dm)
    dua = _mm_nt("branch_a_bwd", dya, wa, tm, 512)[0]
    dwa = _mm_tn("branch_dwa", ua, dya, 512, D).reshape(NDEV, D // NDEV, D)
    dz3 = _mm_nt("branch_b_bwd", dyb, wb, tm, 512)[0]
    dwb = _mm_tn("branch_dwb", z3, dyb, 512, D).reshape(NDEV, D // NDEV, D)
    (dproj, db_u, db_v, d_sgu_g, d_sgu_b, d_ws, d_bs_t), sib_abo = _sgu_bwd(
        "sgu_bwd", proj, mix_b_in, sgu_ln_g, sgu_ln_b, sgu_w_s[0], bias_full, dua, dproj, rider=_pair_rider([dwa, dwb, dwo]))
    sum_a, sum_b, sum_o = _pair_add("mix_dw_add", [dwa, dwb, dwo], sib_abo, slots)
    (dproj, db_cv, db_cg, d_cw, d_cb, d_cln_g, d_cln_b), (got_g2, got_u2) = _conv_bwd(
        "conv_bwd", proj, mix_b_in, conv_w_full, conv_ln_g, conv_ln_b, z0, z1, dz3, dproj, rider=_chip_rider([sum_g2, sum_u2]))
    dwin, (got_a, got_b, got_o) = _mm_tn_blocked("mix_dwin", h2, dproj, rider=_chip_rider([sum_a, sum_b, sum_o]))
    dh2, (sib_in,) = _mm_nt_blocked("mix_in_bwd", [dproj], [w_in], tm, rider=_pair_rider([dwin]))
    (sum_in,) = _pair_add("mix_dwin_add", [dwin], [sib_in], slots)
    dx1, df1, d_norm_mix, dsc2, dsh2, dg1 = _norm_mod_bwd("mix_norm_bwd", x1, norm_mix, sc2, sh2, dh2, dx2,
                                                          below=(saved1[-1], g1, 0.5))

    d_bs = jnp.transpose(d_bs_t[:, :HEADS])
    zero = jnp.zeros((1, D), F32)
    pack_rows = [zero, zero, dg1, dsh2, dsc2, dg2, dsh3, dsc3, dg3,
                 zero, d_norm_mix, d_norm_ffn2, d_norm_final,
                 db_u, db_v, db_cv, db_cg, db_ga, db_gb,
                 d_sgu_g, d_sgu_b, d_bs.reshape(1, D), d_cb, d_cln_g, d_cln_b,
                 d_cw[:KW], loss_row, jnp.zeros((R_TOTAL - R_LOSS - 1, D), F32)]
    packed = jnp.concatenate(pack_rows, axis=0)
    d_ws2 = d_ws.reshape(HEADS * CHUNK, CHUNK)
    (dx0, d_norm_ffn1, dsc1, dsh1), down1, sum_g1, sum_u1, (got_in_near,), (packed_all, dws_all), (got_in_far,) = _ffn_bwd(
        "ffn1", dx1, df1, saved1, norm_ffn1, sh1, sc1, wg1, wu1, wd1, slots,
        dact_rider=_chip_rider([sum_in], NEIGHBOURS), dwd_rider=_gather_rider([packed, d_ws2]),
        dwgu_rider=_chip_rider([sum_in], DIAGONAL))
    packed_late = jnp.concatenate([dsh1, dsc1, jnp.zeros((7, D), F32), d_norm_ffn1, jnp.zeros((R_LATE - 10, D), F32)], axis=0)
    grads = dict(ffn2_w_gate=(sum_g2, [got_g2]), ffn2_w_up=(sum_u2, [got_u2]), ffn2_w_down=down2,
                 mix_w_in=(sum_in, [got_in_near, got_in_far]), w_branch_a=(sum_a, [got_a]), w_branch_b=(sum_b, [got_b]),
                 w_out=(sum_o, [got_o]), ffn1_w_down=down1)
    done, (got_g1, got_u1, late_all) = _adamw_group(
        "adamw_most", [(cs, got, *[a[0] for a in given[n]]) for n, (cs, got) in grads.items()],
        rider=[_chip_rider([sum_g1, sum_u1]), _gather_rider([packed_late])])
    last, _ = _adamw_group("adamw_ffn1_in", [(sum_g1, [got_g1], *[a[0] for a in given["ffn1_w_gate"]]),
                                            (sum_u1, [got_u1], *[a[0] for a in given["ffn1_w_up"]])])
    big_out = {n: [o.reshape(given[n][0].shape) for o in outs]
               for n, outs in zip([*grads, "ffn1_w_gate", "ffn1_w_up"], [*done, *last])}

    flat = lambda a: a.reshape(1, -1)
    vectors = [("ada_b", 0, 9, ada_b, m_ada_b, v_ada_b), ("norm_ffn1", 9, 1, norm_ffn1, m_norm_ffn1, v_norm_ffn1),
               ("norm_mix", 10, 1, norm_mix, m_norm_mix, v_norm_mix), ("norm_ffn2", 11, 1, norm_ffn2, m_norm_ffn2, v_norm_ffn2),
               ("norm_final", 12, 1, norm_final, m_norm_final, v_norm_final), ("mix_b_in", 13, 6, mix_b_in, m_mix_b_in, v_mix_b_in),
               ("sgu_ln_g", 19, 1, sgu_ln_g, m_sgu_ln_g, v_sgu_ln_g), ("sgu_ln_b", 20, 1, sgu_ln_b, m_sgu_ln_b, v_sgu_ln_b),
               ("sgu_b_s", 21, 1, sgu_b_s, m_sgu_b_s, v_sgu_b_s), ("conv_b", 22, 1, conv_b, m_conv_b, v_conv_b),
               ("conv_ln_g", 23, 1, conv_ln_g, m_conv_ln_g, v_conv_ln_g), ("conv_ln_b", 24, 1, conv_ln_b, m_conv_ln_b, v_conv_ln_b)]
    small_out, d_cw_all, loss_sum = _adamw_small(
        "adamw_small", packed_all, late_all, dws_all, [(row, rows, flat(wv), flat(mv), flat(vv)) for _, row, rows, wv, mv, vv in vectors],
        [a.reshape(HEADS * CHUNK, CHUNK) for a in (sgu_w_s, m_sgu_w_s, v_sgu_w_s)])
    small = {n: [o.reshape(wv.shape) for o in outs] for (n, _, _, wv, _, _), outs in zip(vectors, small_out)}
    small["sgu_w_s"] = [o.reshape(sgu_w_s.shape) for o in small_out[-1]]
    g_cw = lax.dynamic_slice(d_cw_all, (0, me * CHUNK), (KW, CHUNK))
    small["conv_w"] = [o.reshape(conv_w.shape) for o in (g_cw, *_adamw_plain("adamw_conv_w", g_cw, conv_w[0], m_conv_w[0], v_conv_w[0]))]
    loss = loss_sum[0, 0]

    dmod_cols = [lax.dynamic_slice(a[:, :N_MOD, :].reshape(NDEV, N_MOD * D), (0, me * ada_cols), (NDEV, ada_cols))
                 for a in (packed_all, late_all)]
    ada_out = [o.reshape(ada_w.shape) for o in _adamw_ada("adamw_ada_w", jnp.transpose(c_all), *dmod_cols, ada_w[0], m_ada_w[0], v_ada_w[0])]

    order = ["ada_w", "ada_b", "norm_ffn1", "ffn1_w_gate", "ffn1_w_up", "ffn1_w_down", "norm_mix", "mix_w_in", "mix_b_in",
             "sgu_ln_g", "sgu_ln_b", "sgu_w_s", "sgu_b_s", "conv_w", "conv_b", "conv_ln_g", "conv_ln_b", "w_branch_a",
             "w_branch_b", "w_out", "norm_ffn2", "ffn2_w_gate", "ffn2_w_up", "ffn2_w_down", "norm_final"]

    def leaf(n, kind):
        if n == "ada_w":
            return ada_out[kind]
        if n in big_out:
            return big_out[n][kind]
        return small[n][kind]

    return (loss, dx0.reshape(x.shape), *[leaf(n, kind) for kind in range(4) for n in order])
```

```python
import jax
import jax.numpy as jnp
from jax import lax
from jax.experimental import pallas as pl
from jax.experimental.pallas import tpu as pltpu

D = 1024
F = 4 * D
D_IN = 6 * D
HEADS = 8
CHUNK = 128
KW = 31
HALO = 32
N_MOD = 9
NDEV = 8
N_CHIPS = 4
EPS = 1e-6
LR, B1, B2, ADAM_EPS, WD, STEP = 0.001, 0.9, 0.999, 1e-08, 0.01, 10
BC1 = 1.0 - B1 ** STEP
BC2 = 1.0 - B2 ** STEP
VMEM_LIMIT = 56 * 1024 * 1024
MESH = pl.DeviceIdType.MESH
HBM = pl.BlockSpec(memory_space=pltpu.HBM)
VMEM = pl.BlockSpec(memory_space=pltpu.VMEM)
BF = jnp.bfloat16
F32 = jnp.float32

NN = (((1,), (0,)), ((), ()))
NT = (((1,), (1,)), ((), ()))
TN = (((0,), (0,)), ((), ()))

R_CW, R_LOSS, R_TOTAL = 25, 56, 64
R_LATE = 16


def _params(sem):
    return pltpu.CompilerParams(dimension_semantics=sem, vmem_limit_bytes=VMEM_LIMIT)


def _position():
    return lax.axis_index("x"), lax.axis_index("y"), lax.axis_index("c")


def _flip(pos, k):
    x, y, c = pos
    return (x ^ (k >> 2 & 1), y ^ (k >> 1 & 1), c ^ (k & 1))


def _index(pos):
    return 4 * pos[0] + 2 * pos[1] + pos[2]


def _gather_rows(x_ref, out_ref, send_sems, recv_sems, local_sem):
    m_per = x_ref.shape[0]
    x, y, c = _position()
    me, sibling = (x, y, c), (x, y, 1 - c)
    chips = [(1 - x, y), (x, 1 - y), (1 - x, 1 - y)]

    def rows(pos):
        return out_ref.at[pl.ds(_index(pos) * m_per, m_per), :]

    def copy(k, block, to, src=None):
        return pltpu.make_async_remote_copy(
            src_ref=rows(block) if src is None else src, dst_ref=rows(block),
            send_sem=send_sems.at[k], recv_sem=recv_sems.at[k], device_id=to, device_id_type=MESH)

    mine = pltpu.make_async_copy(x_ref, rows(me), local_sem)
    mine.start()
    first = [copy(0, me, sibling, src=x_ref)]
    first += [copy(1 + j, me, (*chip, c), src=x_ref) for j, chip in enumerate(chips)]
    for cp in first:
        cp.start()
    passed = [copy(4 + j, (*chip, c), sibling) for j, chip in enumerate(chips)]
    for j, chip in enumerate(chips):
        copy(1 + j, (*chip, c), me).wait_recv()
        passed[j].start()
    copy(0, sibling, me).wait_recv()
    for j, chip in enumerate(chips):
        copy(4 + j, (*chip, 1 - c), me).wait_recv()
    for cp in first + passed:
        cp.wait_send()
    mine.wait()


def _prologue(name, c_rows, taps, ada_w, ada_b, shards):
    rider = _gather_rider(shards)
    n = len(shards)
    nc = ada_w.shape[1]

    def body(*refs):
        c_ref, taps_ref, w_ref, b_ref = refs[:4]
        shard_refs = refs[4:4 + n]
        c_all_ref, taps_all_ref, mod_all_ref = refs[4 + n:7 + n]
        gathered_refs = refs[7 + n:7 + 2 * n]
        c_buf, mod_part, sems = refs[7 + 2 * n], refs[8 + 2 * n], refs[9 + 2 * n:]
        rider.start(shard_refs, gathered_refs, sems[9:])
        _gather_rows(c_ref, c_buf, *sems[0:3])
        c_all = jnp.concatenate([c_buf[pl.ds(d * SUBLANES, 1), :] for d in range(NDEV)], axis=0)
        c_all_ref[...] = c_all
        mod_part[...] = jnp.dot(_silu(c_all), w_ref[...], preferred_element_type=F32) + b_ref[...]
        _gather_rows(taps_ref, taps_all_ref, *sems[3:6])
        _gather_rows(mod_part, mod_all_ref, *sems[6:9])
        rider.mid(shard_refs, gathered_refs, sems[9:])
        rider.relay(shard_refs, gathered_refs, sems[9:])
        rider.finish(shard_refs, gathered_refs, sems[9:])

    small_sems = [pltpu.SemaphoreType.DMA((7,)), pltpu.SemaphoreType.DMA((7,)), pltpu.SemaphoreType.DMA] * 3
    res = pl.pallas_call(
        body, name=name,
        out_shape=[jax.ShapeDtypeStruct((NDEV, D), F32), jax.ShapeDtypeStruct((NDEV * taps.shape[0], taps.shape[1]), F32),
                   jax.ShapeDtypeStruct((NDEV * NDEV, nc), F32)] + rider.out_shapes,
        in_specs=[VMEM] * 4 + [HBM] * n, out_specs=[VMEM] * 3 + [HBM] * n,
        scratch_shapes=[pltpu.VMEM((NDEV * SUBLANES, D), F32), pltpu.VMEM((NDEV, nc), F32)] + small_sems + rider.sems,
        compiler_params=_params(None),
    )(c_rows, taps, ada_w, ada_b, *shards)
    return res[0], res[1], res[2], res[3:]


class _Rider:
    def __init__(self, ins, out_shapes, sems, start, finish, mid=None, relay=None):
        self.ins, self.out_shapes, self.sems = list(ins), list(out_shapes), list(sems)
        self.start, self.finish, self.mid, self.relay = start, finish, mid, relay


def _gather_rider(shards):
    n = len(shards)

    def setup(ins, outs, sems):
        send_sems, recv_sems, local_sems = sems
        x, y, c = _position()
        places = dict(me=(x, y, c), sibling=(x, y, 1 - c), xn=(1 - x, y, c), yn=(x, 1 - y, c), diagonal=(1 - x, 1 - y, c),
                      passed_on=(x ^ c, y ^ (1 - c), c), passed_to=(x ^ (1 - c), y ^ c, c))

        def copy(a, k, block, to, own=False):
            slot = outs[a].at[_index(block)]
            return pltpu.make_async_remote_copy(
                src_ref=ins[a] if own else slot, dst_ref=slot,
                send_sem=send_sems.at[k, a], recv_sem=recv_sems.at[k, a], device_id=to, device_id_type=MESH)

        def local(a):
            return pltpu.make_async_copy(ins[a], outs[a].at[_index(places["me"])], local_sems.at[a])

        return places, copy, local

    def start(ins, outs, sems):
        p, copy, local = setup(ins, outs, sems)
        for a in range(n):
            local(a).start()
            for k, to in enumerate(("sibling", "xn", "yn")):
                copy(a, k, p["me"], p[to], own=True).start()

    def mid(ins, outs, sems):
        p, copy, _ = setup(ins, outs, sems)
        for a in range(n):
            copy(a, 1, p["xn"], p["me"]).wait_recv()
            copy(a, 2, p["yn"], p["me"]).wait_recv()
            copy(a, 3, p["passed_on"], p["passed_to"]).start()
            copy(a, 4, p["xn"], p["sibling"]).start()
            copy(a, 5, p["yn"], p["sibling"]).start()

    def relay(ins, outs, sems):
        p, copy, _ = setup(ins, outs, sems)
        for a in range(n):
            copy(a, 3, p["diagonal"], p["me"]).wait_recv()
            copy(a, 6, p["diagonal"], p["sibling"]).start()

    def finish(ins, outs, sems):
        p, copy, local = setup(ins, outs, sems)
        x, y, c = p["me"]
        for a in range(n):
            for k, block in ((0, (x, y, 1 - c)), (4, (1 - x, y, 1 - c)), (5, (x, 1 - y, 1 - c)), (6, (1 - x, 1 - y, 1 - c))):
                copy(a, k, block, p["me"]).wait_recv()
            for k, to in enumerate(("sibling", "xn", "yn")):
                copy(a, k, p["me"], p[to], own=True).wait_send()
            copy(a, 3, p["passed_on"], p["passed_to"]).wait_send()
            for k, block in ((4, "xn"), (5, "yn"), (6, "diagonal")):
                copy(a, k, p[block], p["sibling"]).wait_send()
            local(a).wait()

    return _Rider(shards, [jax.ShapeDtypeStruct((NDEV, *s.shape), s.dtype) for s in shards],
                  [pltpu.SemaphoreType.DMA((7, n)), pltpu.SemaphoreType.DMA((7, n)), pltpu.SemaphoreType.DMA((n,))],
                  start, finish, mid, relay)


def _pair_rider(parts):
    n = len(parts)

    def copies(ins, outs, sems):
        send_sems, recv_sems = sems
        x, y, c = _position()
        q = 2 * x + y
        return [pltpu.make_async_remote_copy(
            src_ref=ins[a].at[2 * (q ^ k) + (1 - c)], dst_ref=outs[a].at[k],
            send_sem=send_sems.at[k, a], recv_sem=recv_sems.at[k, a], device_id=(x, y, 1 - c), device_id_type=MESH)
            for a in range(n) for k in range(N_CHIPS)]

    def start(ins, outs, sems):
        for cp in copies(ins, outs, sems):
            cp.start()

    def finish(ins, outs, sems):
        for cp in copies(ins, outs, sems):
            cp.wait()

    return _Rider(parts, [jax.ShapeDtypeStruct((N_CHIPS, *p.shape[1:]), p.dtype) for p in parts],
                  [pltpu.SemaphoreType.DMA((N_CHIPS, n)), pltpu.SemaphoreType.DMA((N_CHIPS, n))], start, finish)


NEIGHBOURS = (1, 2)
DIAGONAL = (3,)
OTHER_CHIPS = NEIGHBOURS + DIAGONAL


def _chip_rider(sums, ks=OTHER_CHIPS):
    n = len(sums)

    def copies(ins, outs, sems):
        send_sems, recv_sems = sems
        me = _position()
        return [pltpu.make_async_remote_copy(
            src_ref=ins[a].at[k], dst_ref=outs[a].at[j],
            send_sem=send_sems.at[j, a], recv_sem=recv_sems.at[j, a], device_id=_flip(me, 2 * k), device_id_type=MESH)
            for a in range(n) for j, k in enumerate(ks)]

    def start(ins, outs, sems):
        for cp in copies(ins, outs, sems):
            cp.start()

    def finish(ins, outs, sems):
        for cp in copies(ins, outs, sems):
            cp.wait()

    return _Rider(sums, [jax.ShapeDtypeStruct((len(ks), *s.shape[1:]), s.dtype) for s in sums],
                  [pltpu.SemaphoreType.DMA((len(ks), n)), pltpu.SemaphoreType.DMA((len(ks), n))], start, finish)


def _grid_edge(grid, last):
    cond = None
    for d, n in enumerate(grid):
        here = pl.program_id(d) == (n - 1 if last else 0)
        cond = here if cond is None else jnp.logical_and(cond, here)
    return cond


def _call(name, compute, grid, ins, in_specs, out_shapes, out_specs, scratch_shapes, semantics, rider=None, aliases=None):
    riders = [rider] if isinstance(rider, _Rider) else list(rider or [])
    n_in, n_out, n_scr = len(ins), len(out_shapes), len(scratch_shapes)
    n_rin, n_rout, n_rsem = [sum(len(part(r)) for r in riders) for part in (lambda r: r.ins, lambda r: r.out_shapes, lambda r: r.sems)]
    cuts = [0, n_in, n_in + n_rin, n_in + n_rin + n_out, n_in + n_rin + n_out + n_rout, n_in + n_rin + n_out + n_rout + n_scr]

    def body(*refs):
        in_refs, rin_refs, out_refs, rout_refs, scr_refs = [refs[a:b] for a, b in zip(cuts[:-1], cuts[1:])]
        rsem_refs = refs[cuts[-1]:]
        mine, at = [], [0, 0, 0]
        for r in riders:
            mine.append((r, rin_refs[at[0]:at[0] + len(r.ins)], rout_refs[at[1]:at[1] + len(r.out_shapes)],
                         rsem_refs[at[2]:at[2] + len(r.sems)]))
            at = [at[0] + len(r.ins), at[1] + len(r.out_shapes), at[2] + len(r.sems)]
        if riders:
            @pl.when(_grid_edge(grid, last=False))
            def _():
                for r, a, b, c in mine:
                    r.start(a, b, c)

        if any(r.mid for r in riders):
            step, steps = 0, 1
            for d, size in enumerate(grid):
                step, steps = step * size + pl.program_id(d), steps * size

            @pl.when(step == steps * 5 // 8)
            def _():
                for r, a, b, c in mine:
                    if r.mid:
                        r.mid(a, b, c)

        if any(r.relay for r in riders):
            @pl.when(_grid_edge(grid, last=True))
            def _():
                for r, a, b, c in mine:
                    if r.relay:
                        r.relay(a, b, c)

        compute(in_refs, out_refs, scr_refs)
        if riders:
            @pl.when(_grid_edge(grid, last=True))
            def _():
                for r, a, b, c in mine:
                    r.finish(a, b, c)

    res = pl.pallas_call(
        body, name=name, grid=grid,
        out_shape=list(out_shapes) + [s for r in riders for s in r.out_shapes],
        in_specs=list(in_specs) + [HBM] * n_rin, out_specs=list(out_specs) + [HBM] * n_rout,
        scratch_shapes=list(scratch_shapes) + [s for r in riders for s in r.sems],
        input_output_aliases=aliases or {}, compiler_params=_params(semantics),
    )(*ins, *[a for r in riders for a in r.ins])
    return (res[:n_out], res[n_out:]) if riders else res


def _pair_add(name, parts, from_sibling, slots):
    n = len(parts)

    def body(s_ref, *refs):
        for a in range(n):
            refs[2 * n + a][...] = (refs[a][...].astype(F32) + refs[n + a][...].astype(F32)).astype(refs[2 * n + a].dtype)

    def slab(p, picked):
        _, r, c = p.shape
        return pl.BlockSpec((None, r, c), (lambda k, s: (s[k], 0, 0)) if picked else (lambda k, s: (k, 0, 0)))

    return pl.pallas_call(
        body, name=name,
        grid_spec=pltpu.PrefetchScalarGridSpec(
            num_scalar_prefetch=1, grid=(N_CHIPS,),
            in_specs=[slab(p, True) for p in parts] + [slab(p, False) for p in parts],
            out_specs=[slab(p, False) for p in parts]),
        out_shape=[jax.ShapeDtypeStruct((N_CHIPS, *p.shape[1:]), p.dtype) for p in parts],
        compiler_params=_params(("arbitrary",)),
    )(slots, *parts, *from_sibling)


def _mm(name, pairs, dims, grid, nk, out_shapes, out_specs, extras=(), extra_specs=(), epilogue=None, acc_shape=None, rider=None):
    n_pairs = len(pairs)

    def compute(ins, outs, scratch):
        def partial_sum():
            total = None
            for p in range(n_pairs):
                d = lax.dot_general(ins[2 * p][...], ins[2 * p + 1][...], dims, preferred_element_type=F32)
                total = d if total is None else total + d
            return total

        def finish(r):
            ex = [e[...] for e in ins[2 * n_pairs:]]
            res = epilogue(r, *ex) if epilogue is not None else (r,)
            for o, v in zip(outs, res):
                o[...] = v.astype(o.dtype)

        if nk == 1:
            finish(partial_sum())
        else:
            acc = scratch[0]
            k = pl.program_id(2)

            @pl.when(k == 0)
            def _():
                acc[...] = partial_sum()

            @pl.when(k > 0)
            def _():
                acc[...] += partial_sum()

            @pl.when(k == nk - 1)
            def _():
                finish(acc[...])

    operands, specs = [], []
    for a, a_spec, b, b_spec in pairs:
        operands += [a, b]
        specs += [a_spec, b_spec]
    return _call(name, compute, grid, operands + list(extras), specs + list(extra_specs), out_shapes, out_specs,
                 [pltpu.VMEM(acc_shape, F32)] if nk > 1 else [], ("parallel", "parallel", "arbitrary"), rider)


def _single(res, rider):
    return (res[0][0], res[1]) if rider else res[0]


def _silu(x):
    return x * jax.nn.sigmoid(x)


def _ffn_up(name, h, wg, wu, rider=None):
    t = h.shape[0]
    tm = min(t, 1024)
    nb = F // NDEV

    def compute(ins, outs, _):
        hv = ins[0][...]
        g = jnp.dot(hv, ins[1][...], preferred_element_type=F32)
        u = jnp.dot(hv, ins[2][...], preferred_element_type=F32)
        outs[0][...] = g.astype(BF)
        outs[1][...] = u.astype(BF)
        outs[2][...] = (_silu(g) * u).astype(BF)

    w_spec = pl.BlockSpec((None, D, nb), lambda i, j: (j, 0, 0))
    o_spec = pl.BlockSpec((tm, nb), lambda i, j: (i, j))
    return _call(name, compute, (t // tm, NDEV), [h, wg, wu], [pl.BlockSpec((tm, D), lambda i, j: (i, 0)), w_spec, w_spec],
                 [jax.ShapeDtypeStruct((t, F), BF)] * 3, [o_spec] * 3, [], ("parallel", "arbitrary"), rider)


def _mm_nn(name, a, b, tm, tn, tk, extras=(), extra_specs=(), epilogue=None, out_dtypes=(F32,), rider=None):
    m, kk = a.shape
    n = b.shape[1]
    nk = kk // tk
    return _mm(
        name, [(a, pl.BlockSpec((tm, tk), lambda i, j, k: (i, k)), b, pl.BlockSpec((tk, tn), lambda i, j, k: (k, j)))], NN,
        (m // tm, n // tn, nk), nk,
        [jax.ShapeDtypeStruct((m, n), dt) for dt in out_dtypes],
        [pl.BlockSpec((tm, tn), lambda i, j, k: (i, j))] * len(out_dtypes),
        extras, extra_specs, epilogue, (tm, tn), rider)


def _mm_nn_blocked(name, a, b3, tm, rider=None):
    m = a.shape[0]
    nb = b3.shape[2]
    return _single(_mm(
        name, [(a, pl.BlockSpec((tm, D), lambda i, j, k: (i, 0)), b3, pl.BlockSpec((None, D, nb), lambda i, j, k: (j, 0, 0)))], NN,
        (m // tm, NDEV, 1), 1,
        [jax.ShapeDtypeStruct((m, NDEV * nb), F32)], [pl.BlockSpec((tm, nb), lambda i, j, k: (i, j))], rider=rider), rider)


def _mm_nt(name, a, b, tm, tn, out_dtypes=(F32,), extras=(), extra_specs=(), epilogue=None, rider=None):
    m, kk = a.shape
    n = b.shape[0]
    return _mm(
        name, [(a, pl.BlockSpec((tm, kk), lambda i, j, k: (i, 0)), b, pl.BlockSpec((tn, kk), lambda i, j, k: (j, 0)))], NT,
        (m // tm, n // tn, 1), 1,
        [jax.ShapeDtypeStruct((m, n), dt) for dt in out_dtypes],
        [pl.BlockSpec((tm, tn), lambda i, j, k: (i, j))] * len(out_dtypes),
        extras, extra_specs, epilogue, rider=rider)


def _mm_nt_blocked(name, a_list, b3_list, tm, rider=None):
    m = a_list[0].shape[0]
    nb = b3_list[0].shape[2]
    pairs = [(a, pl.BlockSpec((tm, nb), lambda i, j, k: (i, k)), b3, pl.BlockSpec((None, D, nb), lambda i, j, k: (k, 0, 0)))
             for a, b3 in zip(a_list, b3_list)]
    return _single(_mm(name, pairs, NT, (m // tm, 1, NDEV), NDEV,
                       [jax.ShapeDtypeStruct((m, D), F32)], [pl.BlockSpec((tm, D), lambda i, j, k: (i, 0))],
                       acc_shape=(tm, D), rider=rider), rider)


def _mm_tn(name, a, b, tm, tn, rider=None):
    t, m = a.shape
    n = b.shape[1]
    return _single(_mm(
        name, [(a, pl.BlockSpec((t, tm), lambda i, j, k: (0, i)), b, pl.BlockSpec((t, tn), lambda i, j, k: (0, j)))], TN,
        (m // tm, n // tn, 1), 1,
        [jax.ShapeDtypeStruct((m, n), BF)], [pl.BlockSpec((tm, tn), lambda i, j, k: (i, j))], rider=rider), rider)


def _mm_tn_blocked(name, a, b, rider=None):
    t = a.shape[0]
    nb = b.shape[1] // NDEV
    return _single(_mm(
        name, [(a, pl.BlockSpec((t, D), lambda i, j, k: (0, 0)), b, pl.BlockSpec((t, nb), lambda i, j, k: (0, j)))], TN,
        (1, NDEV, 1), 1,
        [jax.ShapeDtypeStruct((NDEV, D, nb), BF)], [pl.BlockSpec((None, D, nb), lambda i, j, k: (j, 0, 0))], rider=rider), rider)


def _dw_gate_up(name, h, dgate, dup, rider=None):
    t = h.shape[0]
    nb = F // NDEV

    def compute(ins, outs, _):
        hv = ins[0][...]
        outs[0][...] = lax.dot_general(hv, ins[1][...], TN, preferred_element_type=F32).astype(BF)
        outs[1][...] = lax.dot_general(hv, ins[2][...], TN, preferred_element_type=F32).astype(BF)

    d_spec = pl.BlockSpec((t, nb), lambda j: (0, j))
    o_spec = pl.BlockSpec((None, D, nb), lambda j: (j, 0, 0))
    return _call(name, compute, (NDEV,), [h, dgate, dup], [pl.BlockSpec((t, D), lambda j: (0, 0)), d_spec, d_spec],
                 [jax.ShapeDtypeStruct((NDEV, D, nb), BF)] * 2, [o_spec] * 2, [], ("arbitrary",), rider)


def _rowcall(name, fn, ins, in_specs, n_row_out, out_shapes, out_specs, grid, scratch_shapes=(), rider=None, aliases=None):
    def accumulate(o, v, i):
        @pl.when(i == 0)
        def _():
            o[...] = v.astype(o.dtype)

        @pl.when(i > 0)
        def _():
            o[...] += v.astype(o.dtype)

    def compute(in_refs, out_refs, scr):
        i = pl.program_id(0)
        vals = fn(i, in_refs, scr)
        for idx, (o, v) in enumerate(zip(out_refs, vals)):
            if idx < n_row_out:
                o[...] = v.astype(o.dtype)
            else:
                accumulate(o, v, i)

    return _call(name, compute, (grid,), ins, in_specs, out_shapes, out_specs, list(scratch_shapes), ("arbitrary",), rider, aliases)


def _rows(tr, w=D, cb=0):
    return pl.BlockSpec((tr, w), lambda i: (i, cb))


def _whole(shape):
    nd = len(shape)
    return pl.BlockSpec(shape, lambda i: (0,) * nd)


def _vec(n=1):
    return jax.ShapeDtypeStruct((n, D), F32)


def _rms_mod(x, gain, sc, sh):
    y = x * lax.rsqrt(jnp.mean(x * x, axis=-1, keepdims=True) + EPS)
    return (y * gain) * (1.0 + sc) + sh


def _layer_norm(x, g, b):
    mu = jnp.mean(x, axis=-1, keepdims=True)
    var = jnp.mean(jnp.square(x - mu), axis=-1, keepdims=True)
    return (x - mu) * lax.rsqrt(var + EPS) * g + b


def _norm_mod(name, x, gain, sc, sh):
    t = x.shape[0]
    tr = min(t, 256)

    def fn(i, r, _):
        return [_rms_mod(r[0][...], r[1][...], r[2][...], r[3][...])]

    return _rowcall(name, fn, [x, gain, sc, sh], [_rows(tr)] + [_whole((1, D))] * 3, 1,
                    [jax.ShapeDtypeStruct((t, D), BF)], [_rows(tr)], t // tr)[0]


def _gate_grads(dx, f, g, scale):
    return scale * g * dx, jnp.sum(scale * dx * f.astype(F32), axis=0, keepdims=True)


def _norm_mod_bwd(name, x, gain, sc, sh, dh, dres, below=None):
    t = x.shape[0]
    tr = min(t, 256)

    def fn(i, r, _):
        _, vjp = jax.vjp(_rms_mod, r[0][...], r[1][...], r[2][...], r[3][...])
        dx, dgain, dsc, dsh = vjp(r[4][...])
        dx = dx + r[5][...]
        if below is None:
            return [dx, dgain, dsc, dsh]
        df, dg = _gate_grads(dx, r[6][...], r[7][...], below[2])
        return [dx, df, dgain, dsc, dsh, dg]

    ins, specs = [x, gain, sc, sh, dh, dres], [_rows(tr)] + [_whole((1, D))] * 3 + [_rows(tr)] * 2
    outs, out_specs = [jax.ShapeDtypeStruct((t, D), F32)], [_rows(tr)]
    if below is not None:
        ins, specs = ins + [below[0], below[1]], specs + [_rows(tr), _whole((1, D))]
        outs, out_specs = outs + [jax.ShapeDtypeStruct((t, D), BF)], out_specs + [_rows(tr)]
    n_vec = 3 if below is None else 4
    return _rowcall(name, fn, ins, specs, len(outs), outs + [_vec()] * n_vec, out_specs + [_whole((1, D))] * n_vec, t // tr)


def _sgu_pre(up, vp, bu, bv, ln_g, ln_b):
    return jax.nn.gelu(up + bu), _layer_norm(jax.nn.gelu(vp + bv), ln_g, ln_b)


def _causal(w_ref, h):
    rows = lax.broadcasted_iota(jnp.int32, (CHUNK, CHUNK), 0)
    cols = lax.broadcasted_iota(jnp.int32, (CHUNK, CHUNK), 1)
    return jnp.where(cols <= rows, w_ref[h], 0.0)


def _sgu(name, proj, b_in, ln_g, ln_b, w_s, bias_full, rider=None):
    t = proj.shape[0]

    def fn(i, r, _):
        u, v = _sgu_pre(r[0][...], r[1][...], r[2][...], r[3][...], r[4][...], r[5][...])
        vb = v.astype(BF)
        mixed = [jnp.dot(_causal(r[6], h).astype(BF), vb[:, h * CHUNK:(h + 1) * CHUNK], preferred_element_type=F32)
                 for h in range(HEADS)]
        return [u * (jnp.concatenate(mixed, axis=1) + r[7][...])]

    return _rowcall(
        name, fn, [proj, proj, b_in, b_in, ln_g, ln_b, w_s, bias_full],
        [_rows(CHUNK, D, 0), _rows(CHUNK, D, 1), pl.BlockSpec((1, D), lambda i: (0, 0)), pl.BlockSpec((1, D), lambda i: (0, 1)),
         _whole((1, D)), _whole((1, D)), _whole((HEADS, CHUNK, CHUNK)), _whole((CHUNK, D))],
        1, [jax.ShapeDtypeStruct((t, D), BF)], [_rows(CHUNK)], t // CHUNK, rider=rider)


def _sgu_bwd(name, proj, b_in, ln_g, ln_b, w_s, bias_full, dout, dproj, rider=None):
    t = proj.shape[0]

    def fn(i, r, _):
        (u, v), vjp = jax.vjp(_sgu_pre, r[0][...], r[1][...], r[2][...], r[3][...], r[4][...], r[5][...])
        vb = v.astype(BF)
        d = r[8][...]
        masks = [_causal(r[6], h).astype(BF) for h in range(HEADS)]
        cols = [slice(h * CHUNK, (h + 1) * CHUNK) for h in range(HEADS)]
        mixed = jnp.concatenate([jnp.dot(masks[h], vb[:, cols[h]], preferred_element_type=F32) for h in range(HEADS)], axis=1)
        du = d * (mixed + r[7][...])
        dmix = d * u
        dmb = dmix.astype(BF)
        dv = jnp.concatenate([lax.dot_general(masks[h], dmb[:, cols[h]], TN, preferred_element_type=F32) for h in range(HEADS)], axis=1)
        rows = lax.broadcasted_iota(jnp.int32, (CHUNK, CHUNK), 0)
        lanes = lax.broadcasted_iota(jnp.int32, (CHUNK, CHUNK), 1)
        dws = jnp.stack([jnp.where(lanes <= rows, lax.dot_general(dmb[:, cols[h]], vb[:, cols[h]], NT, preferred_element_type=F32), 0.0)
                         for h in range(HEADS)])
        dbs = jnp.zeros((CHUNK, CHUNK), F32)
        for h in range(HEADS):
            dbs = dbs + jnp.where(lanes == h, jnp.sum(dmix[:, cols[h]], axis=1, keepdims=True), 0.0)
        dup, dvp, dbu, dbv, dg, db = vjp((du, dv))
        return [jnp.concatenate([dup, dvp], axis=1), dbu, dbv, dg, db, dws, dbs]

    return _rowcall(
        name, fn, [proj, proj, b_in, b_in, ln_g, ln_b, w_s, bias_full, dout, dproj],
        [_rows(CHUNK, D, 0), _rows(CHUNK, D, 1), pl.BlockSpec((1, D), lambda i: (0, 0)), pl.BlockSpec((1, D), lambda i: (0, 1)),
         _whole((1, D)), _whole((1, D)), _whole((HEADS, CHUNK, CHUNK)), _whole((CHUNK, D)), _rows(CHUNK),
         pl.BlockSpec(memory_space=pl.ANY)],
        1, [jax.ShapeDtypeStruct(dproj.shape, dproj.dtype)] + [_vec()] * 4
        + [jax.ShapeDtypeStruct((HEADS, CHUNK, CHUNK), F32), jax.ShapeDtypeStruct((CHUNK, CHUNK), F32)],
        [pl.BlockSpec((CHUNK, 2 * D), lambda i: (i, 0))] + [_whole((1, D))] * 4 + [_whole((HEADS, CHUNK, CHUNK)), _whole((CHUNK, CHUNK))],
        t // CHUNK, rider=rider, aliases={9: 0})


def _halo_before(tr, cb):
    return pl.BlockSpec((HALO, D), lambda i: (jnp.maximum(i * (tr // HALO) - 1, 0), cb))


def _halo_after(tr, cb, n_tiles):
    return pl.BlockSpec((HALO, D), lambda i: (jnp.minimum((i + 1) * (tr // HALO), n_tiles * (tr // HALO) - 1), cb))


def _ln_silu(z, g, b):
    return _silu(_layer_norm(z, g, b))


SUBLANES = 8
LANES = 128
CONV_STRIP = 16
DW_STRIP = 32


def _shifted_copies(buf, copies, rows):
    for b in range(1, SUBLANES):
        copies[b - 1, pl.ds(0, rows), :] = buf[pl.ds(b, rows), :]


def _shifted(buf, copies, offset, start, rows, lanes=slice(None)):
    at = pl.ds(pl.multiple_of(start + SUBLANES * (offset // SUBLANES), SUBLANES), rows)
    return buf[at, lanes] if offset % SUBLANES == 0 else copies[offset % SUBLANES - 1, at, lanes]


def _accumulate(o, v, i):
    @pl.when(i == 0)
    def _():
        o[...] = v.astype(o.dtype)

    @pl.when(i > 0)
    def _():
        o[...] += v.astype(o.dtype)


def _conv(name, proj, b_in, conv_w, conv_b, ln_g, ln_b, rider=None):
    t = proj.shape[0]
    tr = min(t, 256)

    def compute(r, outs, scr):
        zbuf, zs = scr
        i = pl.program_id(0)
        bv, bg = r[4][...], r[5][...]
        z0 = (r[0][...] + bv) * jax.nn.sigmoid(r[1][...] + bg)
        before = (r[2][...] + bv) * jax.nn.sigmoid(r[3][...] + bg)
        zbuf[pl.ds(0, HALO), :] = jnp.where(i > 0, before, 0.0)
        zbuf[pl.ds(HALO, tr), :] = z0
        outs[0][...] = z0
        _shifted_copies(zbuf, zs, tr + HALO - SUBLANES)

        def strip(s, carry):
            r0 = s * CONV_STRIP
            acc = jnp.zeros((CONV_STRIP, D), F32) + r[7][...]
            for k in range(KW):
                acc = acc + r[6][k:k + 1, :] * _shifted(zbuf, zs, HALO - (KW - 1) + k, r0, CONV_STRIP)
            outs[1][pl.ds(pl.multiple_of(r0, SUBLANES), CONV_STRIP), :] = acc
            return carry

        lax.fori_loop(0, tr // CONV_STRIP, strip, 0)
        outs[2][...] = _ln_silu(outs[1][...], r[8][...], r[9][...]).astype(BF)

    return _call(
        name, compute, (t // tr,), [proj, proj, proj, proj, b_in, b_in, conv_w, conv_b, ln_g, ln_b],
        [_rows(tr, D, 2), _rows(tr, D, 3), _halo_before(tr, 2), _halo_before(tr, 3),
         pl.BlockSpec((1, D), lambda i: (0, 2)), pl.BlockSpec((1, D), lambda i: (0, 3)),
         _whole((HALO, D)), _whole((1, D)), _whole((1, D)), _whole((1, D))],
        [jax.ShapeDtypeStruct((t, D), F32), jax.ShapeDtypeStruct((t, D), F32), jax.ShapeDtypeStruct((t, D), BF)],
        [_rows(tr)] * 3, [pltpu.VMEM((tr + HALO, D), F32), pltpu.VMEM((SUBLANES - 1, tr + HALO, D), F32)], ("arbitrary",), rider)


def _conv_bwd(name, proj, b_in, conv_w, ln_g, ln_b, z0, z1, dz3, dproj, rider=None):
    t = proj.shape[0]
    tr = min(t, 256)
    n_tiles = t // tr

    def compute(r, outs, scr):
        zbuf, dbuf, zs, ds, dwacc = scr
        i = pl.program_id(0)
        g, b = r[5][...], r[6][...]
        _, vjp = jax.vjp(_ln_silu, r[9][...], g, b)
        dz1, dg, db = vjp(r[11][...])
        _, vjp_after = jax.vjp(_ln_silu, r[10][...], g, b)
        dz1_after = vjp_after(r[12][...])[0]
        dbuf[pl.ds(0, tr), :] = dz1
        dbuf[pl.ds(tr, HALO), :] = jnp.where(i < n_tiles - 1, dz1_after, 0.0)
        zbuf[pl.ds(0, HALO), :] = jnp.where(i > 0, r[8][...], 0.0)
        zbuf[pl.ds(HALO, tr), :] = r[7][...]
        _shifted_copies(dbuf, ds, tr + HALO - SUBLANES)
        _shifted_copies(zbuf, zs, tr + HALO - SUBLANES)

        def dz0_strip(s, carry):
            r0 = s * CONV_STRIP
            at = pl.ds(pl.multiple_of(r0, CONV_STRIP), CONV_STRIP)
            acc = jnp.zeros((CONV_STRIP, D), F32)
            for k in range(KW):
                acc = acc + r[4][k:k + 1, :] * _shifted(dbuf, ds, KW - 1 - k, r0, CONV_STRIP)
            a = r[0][at, :] + r[2][...]
            sg = jax.nn.sigmoid(r[1][at, :] + r[3][...])
            dcv = acc * sg
            dcg = acc * a * sg * (1.0 - sg)
            outs[0][at, :] = jnp.concatenate([dcv, dcg], axis=1).astype(BF)
            return carry[0] + jnp.sum(dcv, axis=0, keepdims=True), carry[1] + jnp.sum(dcg, axis=0, keepdims=True)

        zero_row = jnp.zeros((1, D), F32)
        dbv, dbg = lax.fori_loop(0, tr // CONV_STRIP, dz0_strip, (zero_row, zero_row))

        for lb in range(D // LANES):
            lanes = slice(lb * LANES, (lb + 1) * LANES)

            def dw_strip(s, accs, lanes=lanes):
                r0 = s * DW_STRIP
                dz = dbuf[pl.ds(pl.multiple_of(r0, SUBLANES), DW_STRIP), lanes]
                out = []
                for k in range(KW):
                    prod = dz * _shifted(zbuf, zs, HALO - (KW - 1) + k, r0, DW_STRIP, lanes)
                    part = prod[0:SUBLANES]
                    for q in range(1, DW_STRIP // SUBLANES):
                        part = part + prod[q * SUBLANES:(q + 1) * SUBLANES]
                    out.append(accs[k] + part)
                return tuple(out)

            accs = lax.fori_loop(0, tr // DW_STRIP, dw_strip, tuple(jnp.zeros((SUBLANES, LANES), F32) for _ in range(KW)))
            for k in range(KW):
                dwacc[pl.ds(k * SUBLANES, SUBLANES), lanes] = accs[k]
        dw_rows = [jnp.sum(dwacc[pl.ds(k * SUBLANES, SUBLANES), :], axis=0, keepdims=True) for k in range(KW)]
        dw_rows.append(jnp.zeros((HALO - KW, D), F32))
        for o, v in zip(outs[1:], (dbv, dbg, jnp.concatenate(dw_rows, axis=0), jnp.sum(dz1, axis=0, keepdims=True), dg, db)):
            _accumulate(o, v, i)

    wide = pl.BlockSpec((tr, 2 * D), lambda i: (i, 1))
    return _call(
        name, compute, (n_tiles,), [proj, proj, b_in, b_in, conv_w, ln_g, ln_b, z0, z0, z1, z1, dz3, dz3, dproj],
        [_rows(tr, D, 2), _rows(tr, D, 3), pl.BlockSpec((1, D), lambda i: (0, 2)), pl.BlockSpec((1, D), lambda i: (0, 3)),
         _whole((HALO, D)), _whole((1, D)), _whole((1, D)),
         _rows(tr), _halo_before(tr, 0), _rows(tr), _halo_after(tr, 0, n_tiles), _rows(tr), _halo_after(tr, 0, n_tiles),
         pl.BlockSpec(memory_space=pl.ANY)],
        [jax.ShapeDtypeStruct(dproj.shape, dproj.dtype), _vec(), _vec(), _vec(HALO), _vec(), _vec(), _vec()],
        [wide] + [_whole((1, D))] * 2 + [_whole((HALO, D))] + [_whole((1, D))] * 3,
        [pltpu.VMEM((tr + HALO, D), F32), pltpu.VMEM((tr + HALO, D), F32),
         pltpu.VMEM((SUBLANES - 1, tr + HALO, D), F32), pltpu.VMEM((SUBLANES - 1, tr + HALO, D), F32),
         pltpu.VMEM((HALO * SUBLANES, D), F32)],
        ("arbitrary",), rider, aliases={13: 0})


def _merge_fn(ga, gb, bga, bgb, ya, yb):
    return jax.nn.sigmoid(ga + bga) * ya + jax.nn.sigmoid(gb + bgb) * yb


def _merge(name, proj, b_in, ya, yb, rider=None):
    t = proj.shape[0]
    tr = min(t, 256)

    def fn(i, r, _):
        return [_merge_fn(*[x[...] for x in r])]

    return _rowcall(
        name, fn, [proj, proj, b_in, b_in, ya, yb],
        [_rows(tr, D, 4), _rows(tr, D, 5), pl.BlockSpec((1, D), lambda i: (0, 4)), pl.BlockSpec((1, D), lambda i: (0, 5)),
         _rows(tr), _rows(tr)],
        1, [jax.ShapeDtypeStruct((t, D), BF)], [_rows(tr)], t // tr, rider=rider)


def _merge_bwd(name, proj, b_in, ya, yb, dm):
    t = proj.shape[0]
    tr = min(t, 256)

    def fn(i, r, _):
        _, vjp = jax.vjp(_merge_fn, *[x[...] for x in r[:6]])
        dga, dgb, dbga, dbgb, dya, dyb = vjp(r[6][...])
        return [jnp.concatenate([dga, dgb], axis=1), dya, dyb, dbga, dbgb]

    return _rowcall(
        name, fn, [proj, proj, b_in, b_in, ya, yb, dm],
        [_rows(tr, D, 4), _rows(tr, D, 5), pl.BlockSpec((1, D), lambda i: (0, 4)), pl.BlockSpec((1, D), lambda i: (0, 5)),
         _rows(tr), _rows(tr), _rows(tr)],
        3, [jax.ShapeDtypeStruct((t, D_IN), BF)] + [jax.ShapeDtypeStruct((t, D), BF)] * 2 + [_vec(), _vec()],
        [pl.BlockSpec((tr, 2 * D), lambda i: (i, 2))] + [_rows(tr)] * 2 + [_whole((1, D))] * 2, t // tr)


def _loss_head(name, x, gain, target, f, g, scale):
    t = x.shape[0]
    tr = min(t, 256)

    def loss_fn(xv, gn, tgt):
        y = xv * lax.rsqrt(jnp.mean(xv * xv, axis=-1, keepdims=True) + EPS) * gn
        return 0.5 * jnp.sum(jnp.mean(jnp.square(y - tgt), axis=-1))

    def fn(i, r, _):
        loss, vjp = jax.vjp(loss_fn, r[0][...], r[1][...], r[2][...])
        dx, dgain, _ = vjp(jnp.ones((), F32))
        df, dg = _gate_grads(dx, r[3][...], r[4][...], scale)
        return [dx, df, dgain, jnp.zeros((1, D), F32) + loss, dg]

    return _rowcall(name, fn, [x, gain, target, f, g], [_rows(tr), _whole((1, D)), _rows(tr), _rows(tr), _whole((1, D))], 2,
                    [jax.ShapeDtypeStruct((t, D), F32), jax.ShapeDtypeStruct((t, D), BF), _vec(), _vec(), _vec()],
                    [_rows(tr)] * 2 + [_whole((1, D))] * 3, t // tr)


def _adamw(w, g, m, v):
    m = B1 * m + (1.0 - B1) * g
    v = B2 * v + (1.0 - B2) * jnp.square(g)
    m_hat = m / BC1
    v_hat = v / BC2
    delta = -LR * (m_hat / (jnp.sqrt(v_hat) + ADAM_EPS) + WD * w)
    return delta, m, v


ADAMW_ROWS = 64


def _adamw_group(name, items, rider=None):
    ins, in_specs, out_shapes, out_specs, plan = [], [], [], [], []
    first = 0
    for chip_sum, received, w, m, v in items:
        r, c = w.shape
        tr = min(r, ADAMW_ROWS)
        n = r // tr

        def tile(i, first=first, n=n):
            return jnp.clip(i - first, 0, n - 1)

        spec = pl.BlockSpec((tr, c), lambda i, tile=tile: (tile(i), 0))
        ins += [chip_sum, *received, w, m, v]
        in_specs += [pl.BlockSpec((None, tr, c), lambda i, tile=tile: (0, tile(i), 0))]
        in_specs += [pl.BlockSpec((g.shape[0], tr, c), lambda i, tile=tile: (0, tile(i), 0)) for g in received]
        in_specs += [spec] * 3
        out_shapes += [jax.ShapeDtypeStruct((r, c), F32)] * 4
        out_specs += [spec] * 4
        plan.append((first, n, [g.shape[0] for g in received]))
        first += n

    def compute(in_refs, out_refs, _):
        i = pl.program_id(0)
        at_in = at_out = 0
        for start, n, counts in plan:
            mine = in_refs[at_in:at_in + 4 + len(counts)]
            outs = out_refs[at_out:at_out + 4]
            at_in += 4 + len(counts)
            at_out += 4

            @pl.when(jnp.logical_and(i >= start, i < start + n))
            def _(mine=mine, outs=outs, counts=counts):
                g = mine[0][...].astype(F32)
                for j, count in enumerate(counts):
                    for s in range(count):
                        g = g + mine[1 + j][s].astype(F32)
                delta, m_new, v_new = _adamw(mine[-3][...], g, mine[-2][...], mine[-1][...])
                for o, val in zip(outs, (g, delta, m_new, v_new)):
                    o[...] = val

    res = _call(name, compute, (first,), ins, in_specs, out_shapes, out_specs, [], ("arbitrary",), rider)
    outs, rode = res if rider else (res, [])
    return [outs[4 * j:4 * j + 4] for j in range(len(items))], rode


def _adamw_small(name, packed_all, late_all, dws_all, vectors, w_s):
    n_vec = len(vectors)

    def body(*refs):
        p_ref, l_ref, d_ref = refs[:3]
        param_refs = refs[3:3 + 3 * n_vec + 3]
        out_refs = refs[3 + 3 * n_vec + 3:-1]
        g_ref = refs[-1]
        g = p_ref[0]
        late = l_ref[0]
        for s in range(1, NDEV):
            g = g + p_ref[s]
            late = late + l_ref[s]
        g_ref[...] = g
        g_ref[pl.ds(0, R_LATE), :] += late

        def update(gp, wmv, outs):
            delta, m_new, v_new = _adamw(wmv[0][...], gp, wmv[1][...], wmv[2][...])
            for o, val in zip(outs, (gp, delta, m_new, v_new)):
                o[...] = val

        for j, (row, rows, *_) in enumerate(vectors):
            pieces = [g_ref[pl.ds(row + r, 1), :] for r in range(rows)]
            update(pieces[0] if rows == 1 else jnp.concatenate(pieces, axis=1), param_refs[3 * j:3 * j + 3], out_refs[4 * j:4 * j + 4])
        gw = d_ref[0]
        for s in range(1, NDEV):
            gw = gw + d_ref[s]
        update(gw, param_refs[3 * n_vec:], out_refs[4 * n_vec:4 * n_vec + 4])
        out_refs[-2][...] = g_ref[pl.ds(R_CW, KW), :]
        out_refs[-1][...] = g_ref[pl.ds(R_LOSS, 1), :]

    params = [a for _, _, w, m, v in vectors for a in (w, m, v)] + list(w_s)
    out_shapes = [jax.ShapeDtypeStruct(w.shape, F32) for _, _, w, _, _ in vectors for _ in range(4)]
    out_shapes += [jax.ShapeDtypeStruct(w_s[0].shape, F32)] * 4 + [jax.ShapeDtypeStruct((KW, D), F32), _vec()]
    res = pl.pallas_call(body, name=name, out_shape=out_shapes, scratch_shapes=[pltpu.VMEM((R_TOTAL, D), F32)],
                         compiler_params=_params(None))(packed_all, late_all, dws_all, *params)
    return [res[4 * j:4 * j + 4] for j in range(n_vec + 1)], res[-2], res[-1]


def _adamw_plain(name, g, w, m, v):
    def body(g_ref, w_ref, m_ref, v_ref, d_ref, mo_ref, vo_ref):
        delta, m_new, v_new = _adamw(w_ref[...], g_ref[...], m_ref[...], v_ref[...])
        d_ref[...] = delta
        mo_ref[...] = m_new
        vo_ref[...] = v_new

    return pl.pallas_call(body, name=name, out_shape=[jax.ShapeDtypeStruct(w.shape, F32)] * 3,
                          compiler_params=_params(None))(g, w, m, v)


def _adamw_ada(name, c_all_t, dmod, dmod_late, w, m, v):
    r, c = w.shape
    tr = 256

    def fn(i, refs, _):
        ca = _silu(refs[0][...])
        dm = refs[1][...] + refs[2][...]
        g = ca[:, 0:1] * dm[0:1, :]
        for b in range(1, NDEV):
            g = g + ca[:, b:b + 1] * dm[b:b + 1, :]
        delta, m_new, v_new = _adamw(refs[3][...], g, refs[4][...], refs[5][...])
        return [g, delta, m_new, v_new]

    spec = pl.BlockSpec((tr, c), lambda i: (i, 0))
    whole = pl.BlockSpec((NDEV, c), lambda i: (0, 0))
    return _rowcall(name, fn, [c_all_t, dmod, dmod_late, w, m, v],
                    [pl.BlockSpec((tr, NDEV), lambda i: (i, 0)), whole, whole, spec, spec, spec], 4,
                    [jax.ShapeDtypeStruct((r, c), F32)] * 4, [spec] * 4, r // tr)


def _ffn_fwd(tag, x, h, g, wg, wu, wd_shard, down_rider, next_norm=None):
    t = x.shape[0]
    tm = min(t, 512)
    (gate, up, act), (wd,) = _ffn_up(f"{tag}_up", h, wg, wu, rider=_gather_rider([wd_shard]))
    row = pl.BlockSpec((1, D), lambda i, j, k: (0, 0))

    def epilogue(f, xv, gv, *norm):
        x_out = xv + 0.5 * gv * f
        return (x_out, f, _rms_mod(x_out, *norm)) if norm else (x_out, f)

    res = _mm_nn(f"{tag}_down", act, wd.reshape(F, D), tm, D, 1024, extras=(x, g, *(next_norm or ())),
                 extra_specs=(pl.BlockSpec((tm, D), lambda i, j, k: (i, 0)), row, *([row] * 3 if next_norm else [])),
                 epilogue=epilogue, out_dtypes=(F32, BF, BF) if next_norm else (F32, BF), rider=down_rider)
    (x_out, f, *h_next), rode = res if down_rider else (res, None)
    return x_out, (h_next[0] if next_norm else None), (x, h, gate, up, act, f), wd, rode


def _ffn_bwd(tag, dx_out, df, saved, gain, sh, sc, wg, wu, wd, slots, dact_rider=None, dwd_rider=None, dwgu_rider=None,
             below=None):
    x, h, gate, up, act, f = saved
    t = x.shape[0]
    tm = min(t, 1024)

    def act_bwd(da, gv, uv):
        gv = gv.astype(F32)
        s = jax.nn.sigmoid(gv)
        return da * uv.astype(F32) * (s * (1.0 + gv * (1.0 - s))), da * (gv * s)

    blk = pl.BlockSpec((tm, F // NDEV), lambda i, j, k: (i, j))
    res = _mm_nt(f"{tag}_dact", df, wd.reshape(F, D), tm, F // NDEV, out_dtypes=(BF, BF),
                 extras=(gate, up), extra_specs=(blk, blk), epilogue=act_bwd, rider=dact_rider)
    (dgate, dup), rode_dact = res if dact_rider else (res, [])
    res = _mm_tn(f"{tag}_dwd", act, df, 512, D, rider=dwd_rider)
    dwd, rode_dwd = res if dwd_rider else (res, [])
    dwd = dwd.reshape(NDEV, F // NDEV, D)
    (dwg, dwu), (sib_d, *rode_dwgu) = _dw_gate_up(f"{tag}_dwgu", h, dgate, dup,
                                                  rider=[_pair_rider([dwd])] + ([dwgu_rider] if dwgu_rider else []))
    (sum_d,) = _pair_add(f"{tag}_dwd_add", [dwd], [sib_d], slots)
    dh, (sib_g, sib_u, got_d) = _mm_nt_blocked(f"{tag}_dh", [dgate, dup], [wg, wu], tm,
                                               rider=[_pair_rider([dwg, dwu]), _chip_rider([sum_d])])
    sum_g, sum_u = _pair_add(f"{tag}_dwgu_add", [dwg, dwu], [sib_g, sib_u], slots)
    normed = _norm_mod_bwd(f"{tag}_norm_bwd", x, gain, sc, sh, dh, dx_out, below=below)
    return normed, (sum_d, [got_d]), sum_g, sum_u, rode_dact, rode_dwd, rode_dwgu


def kernel(x, c, ada_w, ada_b, norm_ffn1, ffn1_w_gate, ffn1_w_up, ffn1_w_down, norm_mix, mix_w_in, mix_b_in, sgu_ln_g, sgu_ln_b, sgu_w_s, sgu_b_s, conv_w, conv_b, conv_ln_g, conv_ln_b, w_branch_a, w_branch_b, w_out, norm_ffn2, ffn2_w_gate, ffn2_w_up, ffn2_w_down, norm_final, loss_target, m_ada_w, m_ada_b, m_norm_ffn1, m_ffn1_w_gate, m_ffn1_w_up, m_ffn1_w_down, m_norm_mix, m_mix_w_in, m_mix_b_in, m_sgu_ln_g, m_sgu_ln_b, m_sgu_w_s, m_sgu_b_s, m_conv_w, m_conv_b, m_conv_ln_g, m_conv_ln_b, m_w_branch_a, m_w_branch_b, m_w_out, m_norm_ffn2, m_ffn2_w_gate, m_ffn2_w_up, m_ffn2_w_down, m_norm_final, v_ada_w, v_ada_b, v_norm_ffn1, v_ffn1_w_gate, v_ffn1_w_up, v_ffn1_w_down, v_norm_mix, v_mix_w_in, v_mix_b_in, v_sgu_ln_g, v_sgu_ln_b, v_sgu_w_s, v_sgu_b_s, v_conv_w, v_conv_b, v_conv_ln_g, v_conv_ln_b, v_w_branch_a, v_w_branch_b, v_w_out, v_norm_ffn2, v_ffn2_w_gate, v_ffn2_w_up, v_ffn2_w_down, v_norm_final):
    mx, my, mc = _position()
    me = 4 * mx + 2 * my + mc
    chip = 2 * mx + my
    slots = jnp.stack([2 * (chip ^ k) + mc for k in range(N_CHIPS)]).astype(jnp.int32)
    t = x.shape[1]
    tm = min(t, 1024)
    x0 = x.reshape(t, D)
    target = loss_target.reshape(t, D)
    given = dict(ffn1_w_gate=(ffn1_w_gate, m_ffn1_w_gate, v_ffn1_w_gate), ffn1_w_up=(ffn1_w_up, m_ffn1_w_up, v_ffn1_w_up),
                 ffn1_w_down=(ffn1_w_down, m_ffn1_w_down, v_ffn1_w_down), mix_w_in=(mix_w_in, m_mix_w_in, v_mix_w_in),
                 w_branch_a=(w_branch_a, m_w_branch_a, v_w_branch_a), w_branch_b=(w_branch_b, m_w_branch_b, v_w_branch_b),
                 w_out=(w_out, m_w_out, v_w_out), ffn2_w_gate=(ffn2_w_gate, m_ffn2_w_gate, v_ffn2_w_gate),
                 ffn2_w_up=(ffn2_w_up, m_ffn2_w_up, v_ffn2_w_up), ffn2_w_down=(ffn2_w_down, m_ffn2_w_down, v_ffn2_w_down))
    shard = {n: wmv[0][0].astype(BF) for n, wmv in given.items()}

    ada_cols = N_MOD * D // NDEV
    c_all, taps_all, mod_all, (wg1, wu1) = _prologue(
        "prologue", jnp.pad(c, ((0, SUBLANES - 1), (0, 0))), jnp.pad(conv_w[0], ((0, HALO - KW), (0, 0))), ada_w[0],
        lax.dynamic_slice(ada_b, (0, me * ada_cols), (1, ada_cols)), [shard["ffn1_w_gate"], shard["ffn1_w_up"]])
    conv_w_full = jnp.transpose(taps_all.reshape(NDEV, HALO, CHUNK), (1, 0, 2)).reshape(HALO, D)
    mod = lax.dynamic_index_in_dim(mod_all.reshape(NDEV, NDEV, ada_cols), me, axis=1, keepdims=False).reshape(N_MOD, 1, D)
    sh1, sc1, g1, sh2, sc2, g2, sh3, sc3, g3 = [mod[i] for i in range(N_MOD)]

    h1 = _norm_mod("ffn1_norm", x0, norm_ffn1, sc1, sh1)
    x1, h2, saved1, wd1, (w_in,) = _ffn_fwd("ffn1", x0, h1, g1, wg1, wu1, shard["ffn1_w_down"],
                                             _gather_rider([shard["mix_w_in"]]), next_norm=(norm_mix, sc2, sh2))
    proj, (wg2,) = _mm_nn_blocked("mix_in", h2, w_in, tm, rider=_gather_rider([shard["ffn2_w_gate"]]))
    bias_full = jnp.repeat(sgu_b_s[0].T, CHUNK, axis=1)
    (ua,), (wa3, wb3) = _sgu("sgu", proj, mix_b_in, sgu_ln_g, sgu_ln_b, sgu_w_s[0], bias_full,
                             rider=_gather_rider([shard["w_branch_a"], shard["w_branch_b"]]))
    (z0, z1, z3), (wu2,) = _conv("conv", proj, mix_b_in, conv_w_full, conv_b, conv_ln_g, conv_ln_b,
                                 rider=_gather_rider([shard["ffn2_w_up"]]))
    wa, wb = wa3.reshape(D, D), wb3.reshape(D, D)
    ya = _mm_nn("branch_a", ua, wa, tm, 512, D)[0]
    yb = _mm_nn("branch_b", z3, wb, tm, 512, D)[0]
    (merged,), (wo3,) = _merge("merge", proj, mix_b_in, ya, yb, rider=_gather_rider([shard["w_out"]]))
    wo = wo3.reshape(D, D)

    def mix_epilogue(yv, xv, gv, gain, sc, sh):
        x_out = xv + gv * yv
        return x_out, yv, _rms_mod(x_out, gain, sc, sh)

    tmo = min(t, 512)
    row = pl.BlockSpec((1, D), lambda i, j, k: (0, 0))
    x2, y, h3 = _mm_nn("mix_out", merged, wo, tmo, D, D, extras=(x1, g2, norm_ffn2, sc3, sh3),
                       extra_specs=(pl.BlockSpec((tmo, D), lambda i, j, k: (i, 0)), row, row, row, row),
                       epilogue=mix_epilogue, out_dtypes=(F32, BF, BF))
    x3, _, saved3, wd2, _ = _ffn_fwd("ffn2", x2, h3, g3, wg2, wu2, shard["ffn2_w_down"], None)

    norm_final2 = norm_final.reshape(1, D)
    dx3, df3, d_norm_final, loss_row, dg3 = _loss_head("loss_head", x3, norm_final2, target, saved3[-1], g3, 0.5)
    (dx2, dy, d_norm_ffn2, dsc3, dsh3, dg2), down2, sum_g2, sum_u2, _, _, _ = _ffn_bwd(
        "ffn2", dx3, df3, saved3, norm_ffn2, sh3, sc3, wg2, wu2, wd2, slots, below=(y, g2, 1.0))
    dm = _mm_nt("mix_out_bwd", dy, wo, tm, 512)[0]
    dwo = _mm_tn("mix_dwo", merged, dy, 512, D).reshape(NDEV, D // NDEV, D)
    dproj, dya, dyb, db_ga, db_gb = _merge_bwd("merge_bwd", proj, mix_b_in, ya, yb, dm)
    dua = _mm_nt("branch_a_bwd", dya, wa, tm, 512)[0]
    dwa = _mm_tn("branch_dwa", ua, dya, 512, D).reshape(NDEV, D // NDEV, D)
    dz3 = _mm_nt("branch_b_bwd", dyb, wb, tm, 512)[0]
    dwb = _mm_tn("branch_dwb", z3, dyb, 512, D).reshape(NDEV, D // NDEV, D)
    (dproj, db_u, db_v, d_sgu_g, d_sgu_b, d_ws, d_bs_t), sib_abo = _sgu_bwd(
        "sgu_bwd", proj, mix_b_in, sgu_ln_g, sgu_ln_b, sgu_w_s[0], bias_full, dua, dproj, rider=_pair_rider([dwa, dwb, dwo]))
    sum_a, sum_b, sum_o = _pair_add("mix_dw_add", [dwa, dwb, dwo], sib_abo, slots)
    (dproj, db_cv, db_cg, d_cw, d_cb, d_cln_g, d_cln_b), (got_g2, got_u2) = _conv_bwd(
        "conv_bwd", proj, mix_b_in, conv_w_full, conv_ln_g, conv_ln_b, z0, z1, dz3, dproj, rider=_chip_rider([sum_g2, sum_u2]))
    dwin, (got_a, got_b, got_o) = _mm_tn_blocked("mix_dwin", h2, dproj, rider=_chip_rider([sum_a, sum_b, sum_o]))
    dh2, (sib_in,) = _mm_nt_blocked("mix_in_bwd", [dproj], [w_in], tm, rider=_pair_rider([dwin]))
    (sum_in,) = _pair_add("mix_dwin_add", [dwin], [sib_in], slots)
    dx1, df1, d_norm_mix, dsc2, dsh2, dg1 = _norm_mod_bwd("mix_norm_bwd", x1, norm_mix, sc2, sh2, dh2, dx2,
                                                          below=(saved1[-1], g1, 0.5))

    d_bs = jnp.transpose(d_bs_t[:, :HEADS])
    zero = jnp.zeros((1, D), F32)
    pack_rows = [zero, zero, dg1, dsh2, dsc2, dg2, dsh3, dsc3, dg3,
                 zero, d_norm_mix, d_norm_ffn2, d_norm_final,
                 db_u, db_v, db_cv, db_cg, db_ga, db_gb,
                 d_sgu_g, d_sgu_b, d_bs.reshape(1, D), d_cb, d_cln_g, d_cln_b,
                 d_cw[:KW], loss_row, jnp.zeros((R_TOTAL - R_LOSS - 1, D), F32)]
    packed = jnp.concatenate(pack_rows, axis=0)
    d_ws2 = d_ws.reshape(HEADS * CHUNK, CHUNK)
    (dx0, d_norm_ffn1, dsc1, dsh1), down1, sum_g1, sum_u1, (got_in_near,), (packed_all, dws_all), (got_in_far,) = _ffn_bwd(
        "ffn1", dx1, df1, saved1, norm_ffn1, sh1, sc1, wg1, wu1, wd1, slots,
        dact_rider=_chip_rider([sum_in], NEIGHBOURS), dwd_rider=_gather_rider([packed, d_ws2]),
        dwgu_rider=_chip_rider([sum_in], DIAGONAL))
    packed_late = jnp.concatenate([dsh1, dsc1, jnp.zeros((7, D), F32), d_norm_ffn1, jnp.zeros((R_LATE - 10, D), F32)], axis=0)
    grads = dict(ffn2_w_gate=(sum_g2, [got_g2]), ffn2_w_up=(sum_u2, [got_u2]), ffn2_w_down=down2,
                 mix_w_in=(sum_in, [got_in_near, got_in_far]), w_branch_a=(sum_a, [got_a]), w_branch_b=(sum_b, [got_b]),
                 w_out=(sum_o, [got_o]), ffn1_w_down=down1)
    done, (got_g1, got_u1, late_all) = _adamw_group(
        "adamw_most", [(cs, got, *[a[0] for a in given[n]]) for n, (cs, got) in grads.items()],
        rider=[_chip_rider([sum_g1, sum_u1]), _gather_rider([packed_late])])
    last, _ = _adamw_group("adamw_ffn1_in", [(sum_g1, [got_g1], *[a[0] for a in given["ffn1_w_gate"]]),
                                            (sum_u1, [got_u1], *[a[0] for a in given["ffn1_w_up"]])])
    big_out = {n: [o.reshape(given[n][0].shape) for o in outs]
               for n, outs in zip([*grads, "ffn1_w_gate", "ffn1_w_up"], [*done, *last])}

    flat = lambda a: a.reshape(1, -1)
    vectors = [("ada_b", 0, 9, ada_b, m_ada_b, v_ada_b), ("norm_ffn1", 9, 1, norm_ffn1, m_norm_ffn1, v_norm_ffn1),
               ("norm_mix", 10, 1, norm_mix, m_norm_mix, v_norm_mix), ("norm_ffn2", 11, 1, norm_ffn2, m_norm_ffn2, v_norm_ffn2),
               ("norm_final", 12, 1, norm_final, m_norm_final, v_norm_final), ("mix_b_in", 13, 6, mix_b_in, m_mix_b_in, v_mix_b_in),
               ("sgu_ln_g", 19, 1, sgu_ln_g, m_sgu_ln_g, v_sgu_ln_g), ("sgu_ln_b", 20, 1, sgu_ln_b, m_sgu_ln_b, v_sgu_ln_b),
               ("sgu_b_s", 21, 1, sgu_b_s, m_sgu_b_s, v_sgu_b_s), ("conv_b", 22, 1, conv_b, m_conv_b, v_conv_b),
               ("conv_ln_g", 23, 1, conv_ln_g, m_conv_ln_g, v_conv_ln_g), ("conv_ln_b", 24, 1, conv_ln_b, m_conv_ln_b, v_conv_ln_b)]
    small_out, d_cw_all, loss_sum = _adamw_small(
        "adamw_small", packed_all, late_all, dws_all, [(row, rows, flat(wv), flat(mv), flat(vv)) for _, row, rows, wv, mv, vv in vectors],
        [a.reshape(HEADS * CHUNK, CHUNK) for a in (sgu_w_s, m_sgu_w_s, v_sgu_w_s)])
    small = {n: [o.reshape(wv.shape) for o in outs] for (n, _, _, wv, _, _), outs in zip(vectors, small_out)}
    small["sgu_w_s"] = [o.reshape(sgu_w_s.shape) for o in small_out[-1]]
    g_cw = lax.dynamic_slice(d_cw_all, (0, me * CHUNK), (KW, CHUNK))
    small["conv_w"] = [o.reshape(conv_w.shape) for o in (g_cw, *_adamw_plain("adamw_conv_w", g_cw, conv_w[0], m_conv_w[0], v_conv_w[0]))]
    loss = loss_sum[0, 0]

    dmod_cols = [lax.dynamic_slice(a[:, :N_MOD, :].reshape(NDEV, N_MOD * D), (0, me * ada_cols), (NDEV, ada_cols))
                 for a in (packed_all, late_all)]
    ada_out = [o.reshape(ada_w.shape) for o in _adamw_ada("adamw_ada_w", jnp.transpose(c_all), *dmod_cols, ada_w[0], m_ada_w[0], v_ada_w[0])]

    order = ["ada_w", "ada_b", "norm_ffn1", "ffn1_w_gate", "ffn1_w_up", "ffn1_w_down", "norm_mix", "mix_w_in", "mix_b_in",
             "sgu_ln_g", "sgu_ln_b", "sgu_w_s", "sgu_b_s", "conv_w", "conv_b", "conv_ln_g", "conv_ln_b", "w_branch_a",
             "w_branch_b", "w_out", "norm_ffn2", "ffn2_w_gate", "ffn2_w_up", "ffn2_w_down", "norm_final"]

    def leaf(n, kind):
        if n == "ada_w":
            return ada_out[kind]
        if n in big_out:
            return big_out[n][kind]
        return small[n][kind]

    return (loss, dx0.reshape(x.shape), *[leaf(n, kind) for kind in range(4) for n in order])
```

```python
import jax
import jax.numpy as jnp
from jax import lax
from jax.experimental import pallas as pl
from jax.experimental.pallas import tpu as pltpu

D = 1024
F = 4 * D
D_IN = 6 * D
HEADS = 8
CHUNK = 128
KW = 31
HALO = 32
N_MOD = 9
NDEV = 8
N_CHIPS = 4
EPS = 1e-6
LR, B1, B2, ADAM_EPS, WD, STEP = 0.001, 0.9, 0.999, 1e-08, 0.01, 10
BC1 = 1.0 - B1 ** STEP
BC2 = 1.0 - B2 ** STEP
VMEM_LIMIT = 56 * 1024 * 1024
MESH = pl.DeviceIdType.MESH
HBM = pl.BlockSpec(memory_space=pltpu.HBM)
VMEM = pl.BlockSpec(memory_space=pltpu.VMEM)
BF = jnp.bfloat16
F32 = jnp.float32

NN = (((1,), (0,)), ((), ()))
NT = (((1,), (1,)), ((), ()))
TN = (((0,), (0,)), ((), ()))

R_CW, R_LOSS, R_TOTAL = 25, 56, 64
R_LATE = 16


def _params(sem):
    return pltpu.CompilerParams(dimension_semantics=sem, vmem_limit_bytes=VMEM_LIMIT)


def _position():
    return lax.axis_index("x"), lax.axis_index("y"), lax.axis_index("c")


def _flip(pos, k):
    x, y, c = pos
    return (x ^ (k >> 2 & 1), y ^ (k >> 1 & 1), c ^ (k & 1))


def _index(pos):
    return 4 * pos[0] + 2 * pos[1] + pos[2]


def _gather_rows(x_ref, out_ref, send_sems, recv_sems, local_sem):
    m_per = x_ref.shape[0]
    x, y, c = _position()
    me, sibling = (x, y, c), (x, y, 1 - c)
    chips = [(1 - x, y), (x, 1 - y), (1 - x, 1 - y)]

    def rows(pos):
        return out_ref.at[pl.ds(_index(pos) * m_per, m_per), :]

    def copy(k, block, to, src=None):
        return pltpu.make_async_remote_copy(
            src_ref=rows(block) if src is None else src, dst_ref=rows(block),
            send_sem=send_sems.at[k], recv_sem=recv_sems.at[k], device_id=to, device_id_type=MESH)

    mine = pltpu.make_async_copy(x_ref, rows(me), local_sem)
    mine.start()
    first = [copy(0, me, sibling, src=x_ref)]
    first += [copy(1 + j, me, (*chip, c), src=x_ref) for j, chip in enumerate(chips)]
    for cp in first:
        cp.start()
    passed = [copy(4 + j, (*chip, c), sibling) for j, chip in enumerate(chips)]
    for j, chip in enumerate(chips):
        copy(1 + j, (*chip, c), me).wait_recv()
        passed[j].start()
    copy(0, sibling, me).wait_recv()
    for j, chip in enumerate(chips):
        copy(4 + j, (*chip, 1 - c), me).wait_recv()
    for cp in first + passed:
        cp.wait_send()
    mine.wait()


def _prologue(name, c_rows, taps, ada_w, ada_b, shards):
    rider = _gather_rider(shards)
    n = len(shards)
    nc = ada_w.shape[1]

    def body(*refs):
        c_ref, taps_ref, w_ref, b_ref = refs[:4]
        shard_refs = refs[4:4 + n]
        c_all_ref, taps_all_ref, mod_all_ref = refs[4 + n:7 + n]
        gathered_refs = refs[7 + n:7 + 2 * n]
        c_buf, mod_part, sems = refs[7 + 2 * n], refs[8 + 2 * n], refs[9 + 2 * n:]
        rider.start(shard_refs, gathered_refs, sems[9:])
        _gather_rows(c_ref, c_buf, *sems[0:3])
        c_all = jnp.concatenate([c_buf[pl.ds(d * SUBLANES, 1), :] for d in range(NDEV)], axis=0)
        c_all_ref[...] = c_all
        mod_part[...] = jnp.dot(_silu(c_all), w_ref[...], preferred_element_type=F32) + b_ref[...]
        _gather_rows(taps_ref, taps_all_ref, *sems[3:6])
        _gather_rows(mod_part, mod_all_ref, *sems[6:9])
        rider.mid(shard_refs, gathered_refs, sems[9:])
        rider.relay(shard_refs, gathered_refs, sems[9:])
        rider.finish(shard_refs, gathered_refs, sems[9:])

    small_sems = [pltpu.SemaphoreType.DMA((7,)), pltpu.SemaphoreType.DMA((7,)), pltpu.SemaphoreType.DMA] * 3
    res = pl.pallas_call(
        body, name=name,
        out_shape=[jax.ShapeDtypeStruct((NDEV, D), F32), jax.ShapeDtypeStruct((NDEV * taps.shape[0], taps.shape[1]), F32),
                   jax.ShapeDtypeStruct((NDEV * NDEV, nc), F32)] + rider.out_shapes,
        in_specs=[VMEM] * 4 + [HBM] * n, out_specs=[VMEM] * 3 + [HBM] * n,
        scratch_shapes=[pltpu.VMEM((NDEV * SUBLANES, D), F32), pltpu.VMEM((NDEV, nc), F32)] + small_sems + rider.sems,
        compiler_params=_params(None),
    )(c_rows, taps, ada_w, ada_b, *shards)
    return res[0], res[1], res[2], res[3:]


class _Rider:
    def __init__(self, ins, out_shapes, sems, start, finish, mid=None, relay=None):
        self.ins, self.out_shapes, self.sems = list(ins), list(out_shapes), list(sems)
        self.start, self.finish, self.mid, self.relay = start, finish, mid, relay


def _gather_rider(shards):
    n = len(shards)

    def setup(ins, outs, sems):
        send_sems, recv_sems, local_sems = sems
        x, y, c = _position()
        places = dict(me=(x, y, c), sibling=(x, y, 1 - c), xn=(1 - x, y, c), yn=(x, 1 - y, c), diagonal=(1 - x, 1 - y, c),
                      passed_on=(x ^ c, y ^ (1 - c), c), passed_to=(x ^ (1 - c), y ^ c, c))

        def copy(a, k, block, to, own=False):
            slot = outs[a].at[_index(block)]
            return pltpu.make_async_remote_copy(
                src_ref=ins[a] if own else slot, dst_ref=slot,
                send_sem=send_sems.at[k, a], recv_sem=recv_sems.at[k, a], device_id=to, device_id_type=MESH)

        def local(a):
            return pltpu.make_async_copy(ins[a], outs[a].at[_index(places["me"])], local_sems.at[a])

        return places, copy, local

    def start(ins, outs, sems):
        p, copy, local = setup(ins, outs, sems)
        for a in range(n):
            local(a).start()
            for k, to in enumerate(("sibling", "xn", "yn")):
                copy(a, k, p["me"], p[to], own=True).start()

    def mid(ins, outs, sems):
        p, copy, _ = setup(ins, outs, sems)
        for a in range(n):
            copy(a, 1, p["xn"], p["me"]).wait_recv()
            copy(a, 2, p["yn"], p["me"]).wait_recv()
            copy(a, 3, p["passed_on"], p["passed_to"]).start()
            copy(a, 4, p["xn"], p["sibling"]).start()
            copy(a, 5, p["yn"], p["sibling"]).start()

    def relay(ins, outs, sems):
        p, copy, _ = setup(ins, outs, sems)
        for a in range(n):
            copy(a, 3, p["diagonal"], p["me"]).wait_recv()
            copy(a, 6, p["diagonal"], p["sibling"]).start()

    def finish(ins, outs, sems):
        p, copy, local = setup(ins, outs, sems)
        x, y, c = p["me"]
        for a in range(n):
            for k, block in ((0, (x, y, 1 - c)), (4, (1 - x, y, 1 - c)), (5, (x, 1 - y, 1 - c)), (6, (1 - x, 1 - y, 1 - c))):
                copy(a, k, block, p["me"]).wait_recv()
            for k, to in enumerate(("sibling", "xn", "yn")):
                copy(a, k, p["me"], p[to], own=True).wait_send()
            copy(a, 3, p["passed_on"], p["passed_to"]).wait_send()
            for k, block in ((4, "xn"), (5, "yn"), (6, "diagonal")):
                copy(a, k, p[block], p["sibling"]).wait_send()
            local(a).wait()

    return _Rider(shards, [jax.ShapeDtypeStruct((NDEV, *s.shape), s.dtype) for s in shards],
                  [pltpu.SemaphoreType.DMA((7, n)), pltpu.SemaphoreType.DMA((7, n)), pltpu.SemaphoreType.DMA((n,))],
                  start, finish, mid, relay)


def _pair_rider(parts):
    n = len(parts)

    def copies(ins, outs, sems):
        send_sems, recv_sems = sems
        x, y, c = _position()
        q = 2 * x + y
        return [pltpu.make_async_remote_copy(
            src_ref=ins[a].at[2 * (q ^ k) + (1 - c)], dst_ref=outs[a].at[k],
            send_sem=send_sems.at[k, a], recv_sem=recv_sems.at[k, a], device_id=(x, y, 1 - c), device_id_type=MESH)
            for a in range(n) for k in range(N_CHIPS)]

    def start(ins, outs, sems):
        for cp in copies(ins, outs, sems):
            cp.start()

    def finish(ins, outs, sems):
        for cp in copies(ins, outs, sems):
            cp.wait()

    return _Rider(parts, [jax.ShapeDtypeStruct((N_CHIPS, *p.shape[1:]), p.dtype) for p in parts],
                  [pltpu.SemaphoreType.DMA((N_CHIPS, n)), pltpu.SemaphoreType.DMA((N_CHIPS, n))], start, finish)


NEIGHBOURS = (1, 2)
DIAGONAL = (3,)
OTHER_CHIPS = NEIGHBOURS + DIAGONAL


def _chip_rider(sums, ks=OTHER_CHIPS):
    n = len(sums)

    def copies(ins, outs, sems):
        send_sems, recv_sems = sems
        me = _position()
        return [pltpu.make_async_remote_copy(
            src_ref=ins[a].at[k], dst_ref=outs[a].at[j],
            send_sem=send_sems.at[j, a], recv_sem=recv_sems.at[j, a], device_id=_flip(me, 2 * k), device_id_type=MESH)
            for a in range(n) for j, k in enumerate(ks)]

    def start(ins, outs, sems):
        for cp in copies(ins, outs, sems):
            cp.start()

    def finish(ins, outs, sems):
        for cp in copies(ins, outs, sems):
            cp.wait()

    return _Rider(sums, [jax.ShapeDtypeStruct((len(ks), *s.shape[1:]), s.dtype) for s in sums],
                  [pltpu.SemaphoreType.DMA((len(ks), n)), pltpu.SemaphoreType.DMA((len(ks), n))], start, finish)


def _grid_edge(grid, last):
    cond = None
    for d, n in enumerate(grid):
        here = pl.program_id(d) == (n - 1 if last else 0)
        cond = here if cond is None else jnp.logical_and(cond, here)
    return cond


def _call(name, compute, grid, ins, in_specs, out_shapes, out_specs, scratch_shapes, semantics, rider=None, aliases=None):
    riders = [rider] if isinstance(rider, _Rider) else list(rider or [])
    n_in, n_out, n_scr = len(ins), len(out_shapes), len(scratch_shapes)
    n_rin, n_rout, n_rsem = [sum(len(part(r)) for r in riders) for part in (lambda r: r.ins, lambda r: r.out_shapes, lambda r: r.sems)]
    cuts = [0, n_in, n_in + n_rin, n_in + n_rin + n_out, n_in + n_rin + n_out + n_rout, n_in + n_rin + n_out + n_rout + n_scr]

    def body(*refs):
        in_refs, rin_refs, out_refs, rout_refs, scr_refs = [refs[a:b] for a, b in zip(cuts[:-1], cuts[1:])]
        rsem_refs = refs[cuts[-1]:]
        mine, at = [], [0, 0, 0]
        for r in riders:
            mine.append((r, rin_refs[at[0]:at[0] + len(r.ins)], rout_refs[at[1]:at[1] + len(r.out_shapes)],
                         rsem_refs[at[2]:at[2] + len(r.sems)]))
            at = [at[0] + len(r.ins), at[1] + len(r.out_shapes), at[2] + len(r.sems)]
        if riders:
            @pl.when(_grid_edge(grid, last=False))
            def _():
                for r, a, b, c in mine:
                    r.start(a, b, c)

        if any(r.mid for r in riders):
            step, steps = 0, 1
            for d, size in enumerate(grid):
                step, steps = step * size + pl.program_id(d), steps * size

            @pl.when(step == steps * 5 // 8)
            def _():
                for r, a, b, c in mine:
                    if r.mid:
                        r.mid(a, b, c)

        if any(r.relay for r in riders):
            @pl.when(_grid_edge(grid, last=True))
            def _():
                for r, a, b, c in mine:
                    if r.relay:
                        r.relay(a, b, c)

        compute(in_refs, out_refs, scr_refs)
        if riders:
            @pl.when(_grid_edge(grid, last=True))
            def _():
                for r, a, b, c in mine:
                    r.finish(a, b, c)

    res = pl.pallas_call(
        body, name=name, grid=grid,
        out_shape=list(out_shapes) + [s for r in riders for s in r.out_shapes],
        in_specs=list(in_specs) + [HBM] * n_rin, out_specs=list(out_specs) + [HBM] * n_rout,
        scratch_shapes=list(scratch_shapes) + [s for r in riders for s in r.sems],
        input_output_aliases=aliases or {}, compiler_params=_params(semantics),
    )(*ins, *[a for r in riders for a in r.ins])
    return (res[:n_out], res[n_out:]) if riders else res


def _pair_add(name, parts, from_sibling, slots):
    n = len(parts)

    def body(s_ref, *refs):
        for a in range(n):
            refs[2 * n + a][...] = (refs[a][...].astype(F32) + refs[n + a][...].astype(F32)).astype(refs[2 * n + a].dtype)

    def slab(p, picked):
        _, r, c = p.shape
        return pl.BlockSpec((None, r, c), (lambda k, s: (s[k], 0, 0)) if picked else (lambda k, s: (k, 0, 0)))

    return pl.pallas_call(
        body, name=name,
        grid_spec=pltpu.PrefetchScalarGridSpec(
            num_scalar_prefetch=1, grid=(N_CHIPS,),
            in_specs=[slab(p, True) for p in parts] + [slab(p, False) for p in parts],
            out_specs=[slab(p, False) for p in parts]),
        out_shape=[jax.ShapeDtypeStruct((N_CHIPS, *p.shape[1:]), p.dtype) for p in parts],
        compiler_params=_params(("arbitrary",)),
    )(slots, *parts, *from_sibling)


def _mm(name, pairs, dims, grid, nk, out_shapes, out_specs, extras=(), extra_specs=(), epilogue=None, acc_shape=None, rider=None):
    n_pairs = len(pairs)

    def compute(ins, outs, scratch):
        def partial_sum():
            total = None
            for p in range(n_pairs):
                d = lax.dot_general(ins[2 * p][...], ins[2 * p + 1][...], dims, preferred_element_type=F32)
                total = d if total is None else total + d
            return total

        def finish(r):
            ex = [e[...] for e in ins[2 * n_pairs:]]
            res = epilogue(r, *ex) if epilogue is not None else (r,)
            for o, v in zip(outs, res):
                o[...] = v.astype(o.dtype)

        if nk == 1:
            finish(partial_sum())
        else:
            acc = scratch[0]
            k = pl.program_id(2)

            @pl.when(k == 0)
            def _():
                acc[...] = partial_sum()

            @pl.when(k > 0)
            def _():
                acc[...] += partial_sum()

            @pl.when(k == nk - 1)
            def _():
                finish(acc[...])

    operands, specs = [], []
    for a, a_spec, b, b_spec in pairs:
        operands += [a, b]
        specs += [a_spec, b_spec]
    return _call(name, compute, grid, operands + list(extras), specs + list(extra_specs), out_shapes, out_specs,
                 [pltpu.VMEM(acc_shape, F32)] if nk > 1 else [], ("parallel", "parallel", "arbitrary"), rider)


def _single(res, rider):
    return (res[0][0], res[1]) if rider else res[0]


def _silu(x):
    return x * jax.nn.sigmoid(x)


def _ffn_up(name, h, wg, wu, rider=None):
    t = h.shape[0]
    tm = min(t, 1024)
    nb = F // NDEV

    def compute(ins, outs, _):
        hv = ins[0][...]
        g = jnp.dot(hv, ins[1][...], preferred_element_type=F32)
        u = jnp.dot(hv, ins[2][...], preferred_element_type=F32)
        outs[0][...] = g.astype(BF)
        outs[1][...] = u.astype(BF)
        outs[2][...] = (_silu(g) * u).astype(BF)

    w_spec = pl.BlockSpec((None, D, nb), lambda i, j: (j, 0, 0))
    o_spec = pl.BlockSpec((tm, nb), lambda i, j: (i, j))
    return _call(name, compute, (t // tm, NDEV), [h, wg, wu], [pl.BlockSpec((tm, D), lambda i, j: (i, 0)), w_spec, w_spec],
                 [jax.ShapeDtypeStruct((t, F), BF)] * 3, [o_spec] * 3, [], ("parallel", "arbitrary"), rider)


def _mm_nn(name, a, b, tm, tn, tk, extras=(), extra_specs=(), epilogue=None, out_dtypes=(F32,), rider=None):
    m, kk = a.shape
    n = b.shape[1]
    nk = kk // tk
    return _mm(
        name, [(a, pl.BlockSpec((tm, tk), lambda i, j, k: (i, k)), b, pl.BlockSpec((tk, tn), lambda i, j, k: (k, j)))], NN,
        (m // tm, n // tn, nk), nk,
        [jax.ShapeDtypeStruct((m, n), dt) for dt in out_dtypes],
        [pl.BlockSpec((tm, tn), lambda i, j, k: (i, j))] * len(out_dtypes),
        extras, extra_specs, epilogue, (tm, tn), rider)


def _mm_nn_blocked(name, a, b3, tm, rider=None):
    m = a.shape[0]
    nb = b3.shape[2]
    return _single(_mm(
        name, [(a, pl.BlockSpec((tm, D), lambda i, j, k: (i, 0)), b3, pl.BlockSpec((None, D, nb), lambda i, j, k: (j, 0, 0)))], NN,
        (m // tm, NDEV, 1), 1,
        [jax.ShapeDtypeStruct((m, NDEV * nb), F32)], [pl.BlockSpec((tm, nb), lambda i, j, k: (i, j))], rider=rider), rider)


def _mm_nt(name, a, b, tm, tn, out_dtypes=(F32,), extras=(), extra_specs=(), epilogue=None, rider=None):
    m, kk = a.shape
    n = b.shape[0]
    return _mm(
        name, [(a, pl.BlockSpec((tm, kk), lambda i, j, k: (i, 0)), b, pl.BlockSpec((tn, kk), lambda i, j, k: (j, 0)))], NT,
        (m // tm, n // tn, 1), 1,
        [jax.ShapeDtypeStruct((m, n), dt) for dt in out_dtypes],
        [pl.BlockSpec((tm, tn), lambda i, j, k: (i, j))] * len(out_dtypes),
        extras, extra_specs, epilogue, rider=rider)


def _mm_nt_blocked(name, a_list, b3_list, tm, rider=None):
    m = a_list[0].shape[0]
    nb = b3_list[0].shape[2]
    pairs = [(a, pl.BlockSpec((tm, nb), lambda i, j, k: (i, k)), b3, pl.BlockSpec((None, D, nb), lambda i, j, k: (k, 0, 0)))
             for a, b3 in zip(a_list, b3_list)]
    return _single(_mm(name, pairs, NT, (m // tm, 1, NDEV), NDEV,
                       [jax.ShapeDtypeStruct((m, D), F32)], [pl.BlockSpec((tm, D), lambda i, j, k: (i, 0))],
                       acc_shape=(tm, D), rider=rider), rider)


def _mm_tn(name, a, b, tm, tn, rider=None):
    t, m = a.shape
    n = b.shape[1]
    return _single(_mm(
        name, [(a, pl.BlockSpec((t, tm), lambda i, j, k: (0, i)), b, pl.BlockSpec((t, tn), lambda i, j, k: (0, j)))], TN,
        (m // tm, n // tn, 1), 1,
        [jax.ShapeDtypeStruct((m, n), BF)], [pl.BlockSpec((tm, tn), lambda i, j, k: (i, j))], rider=rider), rider)


def _mm_tn_blocked(name, a, b, rider=None):
    t = a.shape[0]
    nb = b.shape[1] // NDEV
    return _single(_mm(
        name, [(a, pl.BlockSpec((t, D), lambda i, j, k: (0, 0)), b, pl.BlockSpec((t, nb), lambda i, j, k: (0, j)))], TN,
        (1, NDEV, 1), 1,
        [jax.ShapeDtypeStruct((NDEV, D, nb), BF)], [pl.BlockSpec((None, D, nb), lambda i, j, k: (j, 0, 0))], rider=rider), rider)


def _dw_gate_up(name, h, dgate, dup, rider=None):
    t = h.shape[0]
    nb = F // NDEV

    def compute(ins, outs, _):
        hv = ins[0][...]
        outs[0][...] = lax.dot_general(hv, ins[1][...], TN, preferred_element_type=F32).astype(BF)
        outs[1][...] = lax.dot_general(hv, ins[2][...], TN, preferred_element_type=F32).astype(BF)

    d_spec = pl.BlockSpec((t, nb), lambda j: (0, j))
    o_spec = pl.BlockSpec((None, D, nb), lambda j: (j, 0, 0))
    return _call(name, compute, (NDEV,), [h, dgate, dup], [pl.BlockSpec((t, D), lambda j: (0, 0)), d_spec, d_spec],
                 [jax.ShapeDtypeStruct((NDEV, D, nb), BF)] * 2, [o_spec] * 2, [], ("arbitrary",), rider)


def _rowcall(name, fn, ins, in_specs, n_row_out, out_shapes, out_specs, grid, scratch_shapes=(), rider=None, aliases=None):
    def accumulate(o, v, i):
        @pl.when(i == 0)
        def _():
            o[...] = v.astype(o.dtype)

        @pl.when(i > 0)
        def _():
            o[...] += v.astype(o.dtype)

    def compute(in_refs, out_refs, scr):
        i = pl.program_id(0)
        vals = fn(i, in_refs, scr)
        for idx, (o, v) in enumerate(zip(out_refs, vals)):
            if idx < n_row_out:
                o[...] = v.astype(o.dtype)
            else:
                accumulate(o, v, i)

    return _call(name, compute, (grid,), ins, in_specs, out_shapes, out_specs, list(scratch_shapes), ("arbitrary",), rider, aliases)


def _rows(tr, w=D, cb=0):
    return pl.BlockSpec((tr, w), lambda i: (i, cb))


def _whole(shape):
    nd = len(shape)
    return pl.BlockSpec(shape, lambda i: (0,) * nd)


def _vec(n=1):
    return jax.ShapeDtypeStruct((n, D), F32)


def _rms_mod(x, gain, sc, sh):
    y = x * lax.rsqrt(jnp.mean(x * x, axis=-1, keepdims=True) + EPS)
    return (y * gain) * (1.0 + sc) + sh


def _layer_norm(x, g, b):
    mu = jnp.mean(x, axis=-1, keepdims=True)
    var = jnp.mean(jnp.square(x - mu), axis=-1, keepdims=True)
    return (x - mu) * lax.rsqrt(var + EPS) * g + b


def _norm_mod(name, x, gain, sc, sh):
    t = x.shape[0]
    tr = min(t, 256)

    def fn(i, r, _):
        return [_rms_mod(r[0][...], r[1][...], r[2][...], r[3][...])]

    return _rowcall(name, fn, [x, gain, sc, sh], [_rows(tr)] + [_whole((1, D))] * 3, 1,
                    [jax.ShapeDtypeStruct((t, D), BF)], [_rows(tr)], t // tr)[0]


def _gate_grads(dx, f, g, scale):
    return scale * g * dx, jnp.sum(scale * dx * f.astype(F32), axis=0, keepdims=True)


def _norm_mod_bwd(name, x, gain, sc, sh, dh, dres, below=None):
    t = x.shape[0]
    tr = min(t, 256)

    def fn(i, r, _):
        _, vjp = jax.vjp(_rms_mod, r[0][...], r[1][...], r[2][...], r[3][...])
        dx, dgain, dsc, dsh = vjp(r[4][...])
        dx = dx + r[5][...]
        if below is None:
            return [dx, dgain, dsc, dsh]
        df, dg = _gate_grads(dx, r[6][...], r[7][...], below[2])
        return [dx, df, dgain, dsc, dsh, dg]

    ins, specs = [x, gain, sc, sh, dh, dres], [_rows(tr)] + [_whole((1, D))] * 3 + [_rows(tr)] * 2
    outs, out_specs = [jax.ShapeDtypeStruct((t, D), F32)], [_rows(tr)]
    if below is not None:
        ins, specs = ins + [below[0], below[1]], specs + [_rows(tr), _whole((1, D))]
        outs, out_specs = outs + [jax.ShapeDtypeStruct((t, D), BF)], out_specs + [_rows(tr)]
    n_vec = 3 if below is None else 4
    return _rowcall(name, fn, ins, specs, len(outs), outs + [_vec()] * n_vec, out_specs + [_whole((1, D))] * n_vec, t // tr)


def _sgu_pre(up, vp, bu, bv, ln_g, ln_b):
    return jax.nn.gelu(up + bu), _layer_norm(jax.nn.gelu(vp + bv), ln_g, ln_b)


def _causal(w_ref, h):
    rows = lax.broadcasted_iota(jnp.int32, (CHUNK, CHUNK), 0)
    cols = lax.broadcasted_iota(jnp.int32, (CHUNK, CHUNK), 1)
    return jnp.where(cols <= rows, w_ref[h], 0.0)


def _sgu(name, proj, b_in, ln_g, ln_b, w_s, bias_full, rider=None):
    t = proj.shape[0]

    def fn(i, r, _):
        u, v = _sgu_pre(r[0][...], r[1][...], r[2][...], r[3][...], r[4][...], r[5][...])
        vb = v.astype(BF)
        mixed = [jnp.dot(_causal(r[6], h).astype(BF), vb[:, h * CHUNK:(h + 1) * CHUNK], preferred_element_type=F32)
                 for h in range(HEADS)]
        return [u * (jnp.concatenate(mixed, axis=1) + r[7][...])]

    return _rowcall(
        name, fn, [proj, proj, b_in, b_in, ln_g, ln_b, w_s, bias_full],
        [_rows(CHUNK, D, 0), _rows(CHUNK, D, 1), pl.BlockSpec((1, D), lambda i: (0, 0)), pl.BlockSpec((1, D), lambda i: (0, 1)),
         _whole((1, D)), _whole((1, D)), _whole((HEADS, CHUNK, CHUNK)), _whole((CHUNK, D))],
        1, [jax.ShapeDtypeStruct((t, D), BF)], [_rows(CHUNK)], t // CHUNK, rider=rider)


def _sgu_bwd(name, proj, b_in, ln_g, ln_b, w_s, bias_full, dout, dproj, rider=None):
    t = proj.shape[0]

    def fn(i, r, _):
        (u, v), vjp = jax.vjp(_sgu_pre, r[0][...], r[1][...], r[2][...], r[3][...], r[4][...], r[5][...])
        vb = v.astype(BF)
        d = r[8][...]
        masks = [_causal(r[6], h).astype(BF) for h in range(HEADS)]
        cols = [slice(h * CHUNK, (h + 1) * CHUNK) for h in range(HEADS)]
        mixed = jnp.concatenate([jnp.dot(masks[h], vb[:, cols[h]], preferred_element_type=F32) for h in range(HEADS)], axis=1)
        du = d * (mixed + r[7][...])
        dmix = d * u
        dmb = dmix.astype(BF)
        dv = jnp.concatenate([lax.dot_general(masks[h], dmb[:, cols[h]], TN, preferred_element_type=F32) for h in range(HEADS)], axis=1)
        rows = lax.broadcasted_iota(jnp.int32, (CHUNK, CHUNK), 0)
        lanes = lax.broadcasted_iota(jnp.int32, (CHUNK, CHUNK), 1)
        dws = jnp.stack([jnp.where(lanes <= rows, lax.dot_general(dmb[:, cols[h]], vb[:, cols[h]], NT, preferred_element_type=F32), 0.0)
                         for h in range(HEADS)])
        dbs = jnp.zeros((CHUNK, CHUNK), F32)
        for h in range(HEADS):
            dbs = dbs + jnp.where(lanes == h, jnp.sum(dmix[:, cols[h]], axis=1, keepdims=True), 0.0)
        dup, dvp, dbu, dbv, dg, db = vjp((du, dv))
        return [jnp.concatenate([dup, dvp], axis=1), dbu, dbv, dg, db, dws, dbs]

    return _rowcall(
        name, fn, [proj, proj, b_in, b_in, ln_g, ln_b, w_s, bias_full, dout, dproj],
        [_rows(CHUNK, D, 0), _rows(CHUNK, D, 1), pl.BlockSpec((1, D), lambda i: (0, 0)), pl.BlockSpec((1, D), lambda i: (0, 1)),
         _whole((1, D)), _whole((1, D)), _whole((HEADS, CHUNK, CHUNK)), _whole((CHUNK, D)), _rows(CHUNK),
         pl.BlockSpec(memory_space=pl.ANY)],
        1, [jax.ShapeDtypeStruct(dproj.shape, dproj.dtype)] + [_vec()] * 4
        + [jax.ShapeDtypeStruct((HEADS, CHUNK, CHUNK), F32), jax.ShapeDtypeStruct((CHUNK, CHUNK), F32)],
        [pl.BlockSpec((CHUNK, 2 * D), lambda i: (i, 0))] + [_whole((1, D))] * 4 + [_whole((HEADS, CHUNK, CHUNK)), _whole((CHUNK, CHUNK))],
        t // CHUNK, rider=rider, aliases={9: 0})


def _halo_before(tr, cb):
    return pl.BlockSpec((HALO, D), lambda i: (jnp.maximum(i * (tr // HALO) - 1, 0), cb))


def _halo_after(tr, cb, n_tiles):
    return pl.BlockSpec((HALO, D), lambda i: (jnp.minimum((i + 1) * (tr // HALO), n_tiles * (tr // HALO) - 1), cb))


def _ln_silu(z, g, b):
    return _silu(_layer_norm(z, g, b))


SUBLANES = 8
LANES = 128
CONV_STRIP = 16
DW_STRIP = 32


def _shifted_copies(buf, copies, rows):
    for b in range(1, SUBLANES):
        copies[b - 1, pl.ds(0, rows), :] = buf[pl.ds(b, rows), :]


def _shifted(buf, copies, offset, start, rows, lanes=slice(None)):
    at = pl.ds(pl.multiple_of(start + SUBLANES * (offset // SUBLANES), SUBLANES), rows)
    return buf[at, lanes] if offset % SUBLANES == 0 else copies[offset % SUBLANES - 1, at, lanes]


def _accumulate(o, v, i):
    @pl.when(i == 0)
    def _():
        o[...] = v.astype(o.dtype)

    @pl.when(i > 0)
    def _():
        o[...] += v.astype(o.dtype)


def _conv(name, proj, b_in, conv_w, conv_b, ln_g, ln_b, rider=None):
    t = proj.shape[0]
    tr = min(t, 256)

    def compute(r, outs, scr):
        zbuf, zs = scr
        i = pl.program_id(0)
        bv, bg = r[4][...], r[5][...]
        z0 = (r[0][...] + bv) * jax.nn.sigmoid(r[1][...] + bg)
        before = (r[2][...] + bv) * jax.nn.sigmoid(r[3][...] + bg)
        zbuf[pl.ds(0, HALO), :] = jnp.where(i > 0, before, 0.0)
        zbuf[pl.ds(HALO, tr), :] = z0
        outs[0][...] = z0
        _shifted_copies(zbuf, zs, tr + HALO - SUBLANES)

        def strip(s, carry):
            r0 = s * CONV_STRIP
            acc = jnp.zeros((CONV_STRIP, D), F32) + r[7][...]
            for k in range(KW):
                acc = acc + r[6][k:k + 1, :] * _shifted(zbuf, zs, HALO - (KW - 1) + k, r0, CONV_STRIP)
            outs[1][pl.ds(pl.multiple_of(r0, SUBLANES), CONV_STRIP), :] = acc
            return carry

        lax.fori_loop(0, tr // CONV_STRIP, strip, 0)
        outs[2][...] = _ln_silu(outs[1][...], r[8][...], r[9][...]).astype(BF)

    return _call(
        name, compute, (t // tr,), [proj, proj, proj, proj, b_in, b_in, conv_w, conv_b, ln_g, ln_b],
        [_rows(tr, D, 2), _rows(tr, D, 3), _halo_before(tr, 2), _halo_before(tr, 3),
         pl.BlockSpec((1, D), lambda i: (0, 2)), pl.BlockSpec((1, D), lambda i: (0, 3)),
         _whole((HALO, D)), _whole((1, D)), _whole((1, D)), _whole((1, D))],
        [jax.ShapeDtypeStruct((t, D), F32), jax.ShapeDtypeStruct((t, D), F32), jax.ShapeDtypeStruct((t, D), BF)],
        [_rows(tr)] * 3, [pltpu.VMEM((tr + HALO, D), F32), pltpu.VMEM((SUBLANES - 1, tr + HALO, D), F32)], ("arbitrary",), rider)


def _conv_bwd(name, proj, b_in, conv_w, ln_g, ln_b, z0, z1, dz3, dproj, rider=None):
    t = proj.shape[0]
    tr = min(t, 256)
    n_tiles = t // tr

    def compute(r, outs, scr):
        zbuf, dbuf, zs, ds, dwacc = scr
        i = pl.program_id(0)
        g, b = r[5][...], r[6][...]
        _, vjp = jax.vjp(_ln_silu, r[9][...], g, b)
        dz1, dg, db = vjp(r[11][...])
        _, vjp_after = jax.vjp(_ln_silu, r[10][...], g, b)
        dz1_after = vjp_after(r[12][...])[0]
        dbuf[pl.ds(0, tr), :] = dz1
        dbuf[pl.ds(tr, HALO), :] = jnp.where(i < n_tiles - 1, dz1_after, 0.0)
        zbuf[pl.ds(0, HALO), :] = jnp.where(i > 0, r[8][...], 0.0)
        zbuf[pl.ds(HALO, tr), :] = r[7][...]
        _shifted_copies(dbuf, ds, tr + HALO - SUBLANES)
        _shifted_copies(zbuf, zs, tr + HALO - SUBLANES)

        def dz0_strip(s, carry):
            r0 = s * CONV_STRIP
            at = pl.ds(pl.multiple_of(r0, CONV_STRIP), CONV_STRIP)
            acc = jnp.zeros((CONV_STRIP, D), F32)
            for k in range(KW):
                acc = acc + r[4][k:k + 1, :] * _shifted(dbuf, ds, KW - 1 - k, r0, CONV_STRIP)
            a = r[0][at, :] + r[2][...]
            sg = jax.nn.sigmoid(r[1][at, :] + r[3][...])
            dcv = acc * sg
            dcg = acc * a * sg * (1.0 - sg)
            outs[0][at, :] = jnp.concatenate([dcv, dcg], axis=1).astype(BF)
            return carry[0] + jnp.sum(dcv, axis=0, keepdims=True), carry[1] + jnp.sum(dcg, axis=0, keepdims=True)

        zero_row = jnp.zeros((1, D), F32)
        dbv, dbg = lax.fori_loop(0, tr // CONV_STRIP, dz0_strip, (zero_row, zero_row))

        for lb in range(D // LANES):
            lanes = slice(lb * LANES, (lb + 1) * LANES)

            def dw_strip(s, accs, lanes=lanes):
                r0 = s * DW_STRIP
                dz = dbuf[pl.ds(pl.multiple_of(r0, SUBLANES), DW_STRIP), lanes]
                out = []
                for k in range(KW):
                    prod = dz * _shifted(zbuf, zs, HALO - (KW - 1) + k, r0, DW_STRIP, lanes)
                    part = prod[0:SUBLANES]
                    for q in range(1, DW_STRIP // SUBLANES):
                        part = part + prod[q * SUBLANES:(q + 1) * SUBLANES]
                    out.append(accs[k] + part)
                return tuple(out)

            accs = lax.fori_loop(0, tr // DW_STRIP, dw_strip, tuple(jnp.zeros((SUBLANES, LANES), F32) for _ in range(KW)))
            for k in range(KW):
                dwacc[pl.ds(k * SUBLANES, SUBLANES), lanes] = accs[k]
        dw_rows = [jnp.sum(dwacc[pl.ds(k * SUBLANES, SUBLANES), :], axis=0, keepdims=True) for k in range(KW)]
        dw_rows.append(jnp.zeros((HALO - KW, D), F32))
        for o, v in zip(outs[1:], (dbv, dbg, jnp.concatenate(dw_rows, axis=0), jnp.sum(dz1, axis=0, keepdims=True), dg, db)):
            _accumulate(o, v, i)

    wide = pl.BlockSpec((tr, 2 * D), lambda i: (i, 1))
    return _call(
        name, compute, (n_tiles,), [proj, proj, b_in, b_in, conv_w, ln_g, ln_b, z0, z0, z1, z1, dz3, dz3, dproj],
        [_rows(tr, D, 2), _rows(tr, D, 3), pl.BlockSpec((1, D), lambda i: (0, 2)), pl.BlockSpec((1, D), lambda i: (0, 3)),
         _whole((HALO, D)), _whole((1, D)), _whole((1, D)),
         _rows(tr), _halo_before(tr, 0), _rows(tr), _halo_after(tr, 0, n_tiles), _rows(tr), _halo_after(tr, 0, n_tiles),
         pl.BlockSpec(memory_space=pl.ANY)],
        [jax.ShapeDtypeStruct(dproj.shape, dproj.dtype), _vec(), _vec(), _vec(HALO), _vec(), _vec(), _vec()],
        [wide] + [_whole((1, D))] * 2 + [_whole((HALO, D))] + [_whole((1, D))] * 3,
        [pltpu.VMEM((tr + HALO, D), F32), pltpu.VMEM((tr + HALO, D), F32),
         pltpu.VMEM((SUBLANES - 1, tr + HALO, D), F32), pltpu.VMEM((SUBLANES - 1, tr + HALO, D), F32),
         pltpu.VMEM((HALO * SUBLANES, D), F32)],
        ("arbitrary",), rider, aliases={13: 0})


def _merge_fn(ga, gb, bga, bgb, ya, yb):
    return jax.nn.sigmoid(ga + bga) * ya + jax.nn.sigmoid(gb + bgb) * yb


def _merge(name, proj, b_in, ya, yb, rider=None):
    t = proj.shape[0]
    tr = min(t, 256)

    def fn(i, r, _):
        return [_merge_fn(*[x[...] for x in r])]

    return _rowcall(
        name, fn, [proj, proj, b_in, b_in, ya, yb],
        [_rows(tr, D, 4), _rows(tr, D, 5), pl.BlockSpec((1, D), lambda i: (0, 4)), pl.BlockSpec((1, D), lambda i: (0, 5)),
         _rows(tr), _rows(tr)],
        1, [jax.ShapeDtypeStruct((t, D), BF)], [_rows(tr)], t // tr, rider=rider)


def _merge_bwd(name, proj, b_in, ya, yb, dm, rider=None):
    t = proj.shape[0]
    tr = min(t, 256)

    def fn(i, r, _):
        _, vjp = jax.vjp(_merge_fn, *[x[...] for x in r[:6]])
        dga, dgb, dbga, dbgb, dya, dyb = vjp(r[6][...])
        return [jnp.concatenate([dga, dgb], axis=1), dya, dyb, dbga, dbgb]

    return _rowcall(
        name, fn, [proj, proj, b_in, b_in, ya, yb, dm],
        [_rows(tr, D, 4), _rows(tr, D, 5), pl.BlockSpec((1, D), lambda i: (0, 4)), pl.BlockSpec((1, D), lambda i: (0, 5)),
         _rows(tr), _rows(tr), _rows(tr)],
        3, [jax.ShapeDtypeStruct((t, D_IN), BF)] + [jax.ShapeDtypeStruct((t, D), BF)] * 2 + [_vec(), _vec()],
        [pl.BlockSpec((tr, 2 * D), lambda i: (i, 2))] + [_rows(tr)] * 2 + [_whole((1, D))] * 2, t // tr, rider=rider)


def _loss_head(name, x, gain, target, f, g, scale):
    t = x.shape[0]
    tr = min(t, 256)

    def loss_fn(xv, gn, tgt):
        y = xv * lax.rsqrt(jnp.mean(xv * xv, axis=-1, keepdims=True) + EPS) * gn
        return 0.5 * jnp.sum(jnp.mean(jnp.square(y - tgt), axis=-1))

    def fn(i, r, _):
        loss, vjp = jax.vjp(loss_fn, r[0][...], r[1][...], r[2][...])
        dx, dgain, _ = vjp(jnp.ones((), F32))
        df, dg = _gate_grads(dx, r[3][...], r[4][...], scale)
        return [dx, df, dgain, jnp.zeros((1, D), F32) + loss, dg]

    return _rowcall(name, fn, [x, gain, target, f, g], [_rows(tr), _whole((1, D)), _rows(tr), _rows(tr), _whole((1, D))], 2,
                    [jax.ShapeDtypeStruct((t, D), F32), jax.ShapeDtypeStruct((t, D), BF), _vec(), _vec(), _vec()],
                    [_rows(tr)] * 2 + [_whole((1, D))] * 3, t // tr)


def _adamw(w, g, m, v):
    m = B1 * m + (1.0 - B1) * g
    v = B2 * v + (1.0 - B2) * jnp.square(g)
    m_hat = m / BC1
    v_hat = v / BC2
    delta = -LR * (m_hat / (jnp.sqrt(v_hat) + ADAM_EPS) + WD * w)
    return delta, m, v


ADAMW_ROWS = 64


def _adamw_group(name, items, rider=None):
    ins, in_specs, out_shapes, out_specs, plan = [], [], [], [], []
    first = 0
    for chip_sum, received, w, m, v in items:
        r, c = w.shape
        tr = min(r, ADAMW_ROWS)
        n = r // tr

        def tile(i, first=first, n=n):
            return jnp.clip(i - first, 0, n - 1)

        spec = pl.BlockSpec((tr, c), lambda i, tile=tile: (tile(i), 0))
        ins += [chip_sum, *received, w, m, v]
        in_specs += [pl.BlockSpec((None, tr, c), lambda i, tile=tile: (0, tile(i), 0))]
        in_specs += [pl.BlockSpec((g.shape[0], tr, c), lambda i, tile=tile: (0, tile(i), 0)) for g in received]
        in_specs += [spec] * 3
        out_shapes += [jax.ShapeDtypeStruct((r, c), F32)] * 4
        out_specs += [spec] * 4
        plan.append((first, n, [g.shape[0] for g in received]))
        first += n

    def compute(in_refs, out_refs, _):
        i = pl.program_id(0)
        at_in = at_out = 0
        for start, n, counts in plan:
            mine = in_refs[at_in:at_in + 4 + len(counts)]
            outs = out_refs[at_out:at_out + 4]
            at_in += 4 + len(counts)
            at_out += 4

            @pl.when(jnp.logical_and(i >= start, i < start + n))
            def _(mine=mine, outs=outs, counts=counts):
                g = mine[0][...].astype(F32)
                for j, count in enumerate(counts):
                    for s in range(count):
                        g = g + mine[1 + j][s].astype(F32)
                delta, m_new, v_new = _adamw(mine[-3][...], g, mine[-2][...], mine[-1][...])
                for o, val in zip(outs, (g, delta, m_new, v_new)):
                    o[...] = val

    res = _call(name, compute, (first,), ins, in_specs, out_shapes, out_specs, [], ("arbitrary",), rider)
    outs, rode = res if rider else (res, [])
    return [outs[4 * j:4 * j + 4] for j in range(len(items))], rode


def _adamw_small(name, packed_all, late_all, dws_all, vectors, w_s):
    n_vec = len(vectors)

    def body(*refs):
        p_ref, l_ref, d_ref = refs[:3]
        param_refs = refs[3:3 + 3 * n_vec + 3]
        out_refs = refs[3 + 3 * n_vec + 3:-1]
        g_ref = refs[-1]
        g = p_ref[0]
        late = l_ref[0]
        for s in range(1, NDEV):
            g = g + p_ref[s]
            late = late + l_ref[s]
        g_ref[...] = g
        g_ref[pl.ds(0, R_LATE), :] += late

        def update(gp, wmv, outs):
            delta, m_new, v_new = _adamw(wmv[0][...], gp, wmv[1][...], wmv[2][...])
            for o, val in zip(outs, (gp, delta, m_new, v_new)):
                o[...] = val

        for j, (row, rows, *_) in enumerate(vectors):
            pieces = [g_ref[pl.ds(row + r, 1), :] for r in range(rows)]
            update(pieces[0] if rows == 1 else jnp.concatenate(pieces, axis=1), param_refs[3 * j:3 * j + 3], out_refs[4 * j:4 * j + 4])
        gw = d_ref[0]
        for s in range(1, NDEV):
            gw = gw + d_ref[s]
        update(gw, param_refs[3 * n_vec:], out_refs[4 * n_vec:4 * n_vec + 4])
        out_refs[-2][...] = g_ref[pl.ds(R_CW, KW), :]
        out_refs[-1][...] = g_ref[pl.ds(R_LOSS, 1), :]

    params = [a for _, _, w, m, v in vectors for a in (w, m, v)] + list(w_s)
    out_shapes = [jax.ShapeDtypeStruct(w.shape, F32) for _, _, w, _, _ in vectors for _ in range(4)]
    out_shapes += [jax.ShapeDtypeStruct(w_s[0].shape, F32)] * 4 + [jax.ShapeDtypeStruct((KW, D), F32), _vec()]
    res = pl.pallas_call(body, name=name, out_shape=out_shapes, scratch_shapes=[pltpu.VMEM((R_TOTAL, D), F32)],
                         compiler_params=_params(None))(packed_all, late_all, dws_all, *params)
    return [res[4 * j:4 * j + 4] for j in range(n_vec + 1)], res[-2], res[-1]


def _adamw_plain(name, g, w, m, v):
    def body(g_ref, w_ref, m_ref, v_ref, d_ref, mo_ref, vo_ref):
        delta, m_new, v_new = _adamw(w_ref[...], g_ref[...], m_ref[...], v_ref[...])
        d_ref[...] = delta
        mo_ref[...] = m_new
        vo_ref[...] = v_new

    return pl.pallas_call(body, name=name, out_shape=[jax.ShapeDtypeStruct(w.shape, F32)] * 3,
                          compiler_params=_params(None))(g, w, m, v)


def _adamw_ada(name, c_all_t, dmod, dmod_late, w, m, v):
    r, c = w.shape
    tr = 256

    def fn(i, refs, _):
        ca = _silu(refs[0][...])
        dm = refs[1][...] + refs[2][...]
        g = ca[:, 0:1] * dm[0:1, :]
        for b in range(1, NDEV):
            g = g + ca[:, b:b + 1] * dm[b:b + 1, :]
        delta, m_new, v_new = _adamw(refs[3][...], g, refs[4][...], refs[5][...])
        return [g, delta, m_new, v_new]

    spec = pl.BlockSpec((tr, c), lambda i: (i, 0))
    whole = pl.BlockSpec((NDEV, c), lambda i: (0, 0))
    return _rowcall(name, fn, [c_all_t, dmod, dmod_late, w, m, v],
                    [pl.BlockSpec((tr, NDEV), lambda i: (i, 0)), whole, whole, spec, spec, spec], 4,
                    [jax.ShapeDtypeStruct((r, c), F32)] * 4, [spec] * 4, r // tr)


def _ffn_fwd(tag, x, h, g, wg, wu, wd_shard, down_rider, next_norm=None):
    t = x.shape[0]
    tm = min(t, 512)
    (gate, up, act), (wd,) = _ffn_up(f"{tag}_up", h, wg, wu, rider=_gather_rider([wd_shard]))
    row = pl.BlockSpec((1, D), lambda i, j, k: (0, 0))

    def epilogue(f, xv, gv, *norm):
        x_out = xv + 0.5 * gv * f
        return (x_out, f, _rms_mod(x_out, *norm)) if norm else (x_out, f)

    res = _mm_nn(f"{tag}_down", act, wd.reshape(F, D), tm, D, 1024, extras=(x, g, *(next_norm or ())),
                 extra_specs=(pl.BlockSpec((tm, D), lambda i, j, k: (i, 0)), row, *([row] * 3 if next_norm else [])),
                 epilogue=epilogue, out_dtypes=(F32, BF, BF) if next_norm else (F32, BF), rider=down_rider)
    (x_out, f, *h_next), rode = res if down_rider else (res, None)
    return x_out, (h_next[0] if next_norm else None), (x, h, gate, up, act, f), wd, rode


def _ffn_bwd(tag, dx_out, df, saved, gain, sh, sc, wg, wu, wd, slots, dact_rider=None, dwd_rider=None, dwgu_rider=None,
             below=None):
    x, h, gate, up, act, f = saved
    t = x.shape[0]
    tm = min(t, 1024)

    def act_bwd(da, gv, uv):
        gv = gv.astype(F32)
        s = jax.nn.sigmoid(gv)
        return da * uv.astype(F32) * (s * (1.0 + gv * (1.0 - s))), da * (gv * s)

    blk = pl.BlockSpec((tm, F // NDEV), lambda i, j, k: (i, j))
    res = _mm_nt(f"{tag}_dact", df, wd.reshape(F, D), tm, F // NDEV, out_dtypes=(BF, BF),
                 extras=(gate, up), extra_specs=(blk, blk), epilogue=act_bwd, rider=dact_rider)
    (dgate, dup), rode_dact = res if dact_rider else (res, [])
    res = _mm_tn(f"{tag}_dwd", act, df, 512, D, rider=dwd_rider)
    dwd, rode_dwd = res if dwd_rider else (res, [])
    dwd = dwd.reshape(NDEV, F // NDEV, D)
    (dwg, dwu), (sib_d, *rode_dwgu) = _dw_gate_up(f"{tag}_dwgu", h, dgate, dup,
                                                  rider=[_pair_rider([dwd])] + ([dwgu_rider] if dwgu_rider else []))
    (sum_d,) = _pair_add(f"{tag}_dwd_add", [dwd], [sib_d], slots)
    dh, (sib_g, sib_u, got_d) = _mm_nt_blocked(f"{tag}_dh", [dgate, dup], [wg, wu], tm,
                                               rider=[_pair_rider([dwg, dwu]), _chip_rider([sum_d])])
    sum_g, sum_u = _pair_add(f"{tag}_dwgu_add", [dwg, dwu], [sib_g, sib_u], slots)
    normed = _norm_mod_bwd(f"{tag}_norm_bwd", x, gain, sc, sh, dh, dx_out, below=below)
    return normed, (sum_d, [got_d]), sum_g, sum_u, rode_dact, rode_dwd, rode_dwgu


def kernel(x, c, ada_w, ada_b, norm_ffn1, ffn1_w_gate, ffn1_w_up, ffn1_w_down, norm_mix, mix_w_in, mix_b_in, sgu_ln_g, sgu_ln_b, sgu_w_s, sgu_b_s, conv_w, conv_b, conv_ln_g, conv_ln_b, w_branch_a, w_branch_b, w_out, norm_ffn2, ffn2_w_gate, ffn2_w_up, ffn2_w_down, norm_final, loss_target, m_ada_w, m_ada_b, m_norm_ffn1, m_ffn1_w_gate, m_ffn1_w_up, m_ffn1_w_down, m_norm_mix, m_mix_w_in, m_mix_b_in, m_sgu_ln_g, m_sgu_ln_b, m_sgu_w_s, m_sgu_b_s, m_conv_w, m_conv_b, m_conv_ln_g, m_conv_ln_b, m_w_branch_a, m_w_branch_b, m_w_out, m_norm_ffn2, m_ffn2_w_gate, m_ffn2_w_up, m_ffn2_w_down, m_norm_final, v_ada_w, v_ada_b, v_norm_ffn1, v_ffn1_w_gate, v_ffn1_w_up, v_ffn1_w_down, v_norm_mix, v_mix_w_in, v_mix_b_in, v_sgu_ln_g, v_sgu_ln_b, v_sgu_w_s, v_sgu_b_s, v_conv_w, v_conv_b, v_conv_ln_g, v_conv_ln_b, v_w_branch_a, v_w_branch_b, v_w_out, v_norm_ffn2, v_ffn2_w_gate, v_ffn2_w_up, v_ffn2_w_down, v_norm_final):
    mx, my, mc = _position()
    me = 4 * mx + 2 * my + mc
    chip = 2 * mx + my
    slots = jnp.stack([2 * (chip ^ k) + mc for k in range(N_CHIPS)]).astype(jnp.int32)
    t = x.shape[1]
    tm = min(t, 1024)
    x0 = x.reshape(t, D)
    target = loss_target.reshape(t, D)
    given = dict(ffn1_w_gate=(ffn1_w_gate, m_ffn1_w_gate, v_ffn1_w_gate), ffn1_w_up=(ffn1_w_up, m_ffn1_w_up, v_ffn1_w_up),
                 ffn1_w_down=(ffn1_w_down, m_ffn1_w_down, v_ffn1_w_down), mix_w_in=(mix_w_in, m_mix_w_in, v_mix_w_in),
                 w_branch_a=(w_branch_a, m_w_branch_a, v_w_branch_a), w_branch_b=(w_branch_b, m_w_branch_b, v_w_branch_b),
                 w_out=(w_out, m_w_out, v_w_out), ffn2_w_gate=(ffn2_w_gate, m_ffn2_w_gate, v_ffn2_w_gate),
                 ffn2_w_up=(ffn2_w_up, m_ffn2_w_up, v_ffn2_w_up), ffn2_w_down=(ffn2_w_down, m_ffn2_w_down, v_ffn2_w_down))
    shard = {n: wmv[0][0].astype(BF) for n, wmv in given.items()}

    ada_cols = N_MOD * D // NDEV
    c_all, taps_all, mod_all, (wg1, wu1) = _prologue(
        "prologue", jnp.pad(c, ((0, SUBLANES - 1), (0, 0))), jnp.pad(conv_w[0], ((0, HALO - KW), (0, 0))), ada_w[0],
        lax.dynamic_slice(ada_b, (0, me * ada_cols), (1, ada_cols)), [shard["ffn1_w_gate"], shard["ffn1_w_up"]])
    conv_w_full = jnp.transpose(taps_all.reshape(NDEV, HALO, CHUNK), (1, 0, 2)).reshape(HALO, D)
    mod = lax.dynamic_index_in_dim(mod_all.reshape(NDEV, NDEV, ada_cols), me, axis=1, keepdims=False).reshape(N_MOD, 1, D)
    sh1, sc1, g1, sh2, sc2, g2, sh3, sc3, g3 = [mod[i] for i in range(N_MOD)]

    h1 = _norm_mod("ffn1_norm", x0, norm_ffn1, sc1, sh1)
    x1, h2, saved1, wd1, (w_in,) = _ffn_fwd("ffn1", x0, h1, g1, wg1, wu1, shard["ffn1_w_down"],
                                             _gather_rider([shard["mix_w_in"]]), next_norm=(norm_mix, sc2, sh2))
    proj, (wg2, wa3, wb3) = _mm_nn_blocked(
        "mix_in", h2, w_in, tm, rider=_gather_rider([shard["ffn2_w_gate"], shard["w_branch_a"], shard["w_branch_b"]]))
    bias_full = jnp.repeat(sgu_b_s[0].T, CHUNK, axis=1)
    (ua,) = _sgu("sgu", proj, mix_b_in, sgu_ln_g, sgu_ln_b, sgu_w_s[0], bias_full)
    (z0, z1, z3), (wu2, wo3) = _conv("conv", proj, mix_b_in, conv_w_full, conv_b, conv_ln_g, conv_ln_b,
                                     rider=_gather_rider([shard["ffn2_w_up"], shard["w_out"]]))
    wa, wb = wa3.reshape(D, D), wb3.reshape(D, D)
    ya = _mm_nn("branch_a", ua, wa, tm, 512, D)[0]
    yb = _mm_nn("branch_b", z3, wb, tm, 512, D)[0]
    (merged,) = _merge("merge", proj, mix_b_in, ya, yb)
    wo = wo3.reshape(D, D)

    def mix_epilogue(yv, xv, gv, gain, sc, sh):
        x_out = xv + gv * yv
        return x_out, yv, _rms_mod(x_out, gain, sc, sh)

    tmo = min(t, 512)
    row = pl.BlockSpec((1, D), lambda i, j, k: (0, 0))
    x2, y, h3 = _mm_nn("mix_out", merged, wo, tmo, D, D, extras=(x1, g2, norm_ffn2, sc3, sh3),
                       extra_specs=(pl.BlockSpec((tmo, D), lambda i, j, k: (i, 0)), row, row, row, row),
                       epilogue=mix_epilogue, out_dtypes=(F32, BF, BF))
    x3, _, saved3, wd2, _ = _ffn_fwd("ffn2", x2, h3, g3, wg2, wu2, shard["ffn2_w_down"], None)

    norm_final2 = norm_final.reshape(1, D)
    dx3, df3, d_norm_final, loss_row, dg3 = _loss_head("loss_head", x3, norm_final2, target, saved3[-1], g3, 0.5)
    (dx2, dy, d_norm_ffn2, dsc3, dsh3, dg2), down2, sum_g2, sum_u2, _, _, _ = _ffn_bwd(
        "ffn2", dx3, df3, saved3, norm_ffn2, sh3, sc3, wg2, wu2, wd2, slots, below=(y, g2, 1.0))
    dm = _mm_nt("mix_out_bwd", dy, wo, tm, 512)[0]
    dwo = _mm_tn("mix_dwo", merged, dy, 512, D).reshape(NDEV, D // NDEV, D)
    (dproj, dya, dyb, db_ga, db_gb), (got_g2_near,) = _merge_bwd("merge_bwd", proj, mix_b_in, ya, yb, dm,
                                                                 rider=_chip_rider([sum_g2], NEIGHBOURS))
    dua = _mm_nt("branch_a_bwd", dya, wa, tm, 512)[0]
    dwa = _mm_tn("branch_dwa", ua, dya, 512, D).reshape(NDEV, D // NDEV, D)
    dz3 = _mm_nt("branch_b_bwd", dyb, wb, tm, 512)[0]
    dwb = _mm_tn("branch_dwb", z3, dyb, 512, D).reshape(NDEV, D // NDEV, D)
    (dproj, db_u, db_v, d_sgu_g, d_sgu_b, d_ws, d_bs_t), (*sib_abo, got_g2_far) = _sgu_bwd(
        "sgu_bwd", proj, mix_b_in, sgu_ln_g, sgu_ln_b, sgu_w_s[0], bias_full, dua, dproj,
        rider=[_pair_rider([dwa, dwb, dwo]), _chip_rider([sum_g2], DIAGONAL)])
    sum_a, sum_b, sum_o = _pair_add("mix_dw_add", [dwa, dwb, dwo], sib_abo, slots)
    (dproj, db_cv, db_cg, d_cw, d_cb, d_cln_g, d_cln_b), (got_u2,) = _conv_bwd(
        "conv_bwd", proj, mix_b_in, conv_w_full, conv_ln_g, conv_ln_b, z0, z1, dz3, dproj, rider=_chip_rider([sum_u2]))
    dwin, (got_a, got_b, got_o) = _mm_tn_blocked("mix_dwin", h2, dproj, rider=_chip_rider([sum_a, sum_b, sum_o]))
    dh2, (sib_in,) = _mm_nt_blocked("mix_in_bwd", [dproj], [w_in], tm, rider=_pair_rider([dwin]))
    (sum_in,) = _pair_add("mix_dwin_add", [dwin], [sib_in], slots)
    dx1, df1, d_norm_mix, dsc2, dsh2, dg1 = _norm_mod_bwd("mix_norm_bwd", x1, norm_mix, sc2, sh2, dh2, dx2,
                                                          below=(saved1[-1], g1, 0.5))

    d_bs = jnp.transpose(d_bs_t[:, :HEADS])
    zero = jnp.zeros((1, D), F32)
    pack_rows = [zero, zero, dg1, dsh2, dsc2, dg2, dsh3, dsc3, dg3,
                 zero, d_norm_mix, d_norm_ffn2, d_norm_final,
                 db_u, db_v, db_cv, db_cg, db_ga, db_gb,
                 d_sgu_g, d_sgu_b, d_bs.reshape(1, D), d_cb, d_cln_g, d_cln_b,
                 d_cw[:KW], loss_row, jnp.zeros((R_TOTAL - R_LOSS - 1, D), F32)]
    packed = jnp.concatenate(pack_rows, axis=0)
    d_ws2 = d_ws.reshape(HEADS * CHUNK, CHUNK)
    (dx0, d_norm_ffn1, dsc1, dsh1), down1, sum_g1, sum_u1, (got_in_near,), (packed_all, dws_all), (got_in_far,) = _ffn_bwd(
        "ffn1", dx1, df1, saved1, norm_ffn1, sh1, sc1, wg1, wu1, wd1, slots,
        dact_rider=_chip_rider([sum_in], NEIGHBOURS), dwd_rider=_gather_rider([packed, d_ws2]),
        dwgu_rider=_chip_rider([sum_in], DIAGONAL))
    packed_late = jnp.concatenate([dsh1, dsc1, jnp.zeros((7, D), F32), d_norm_ffn1, jnp.zeros((R_LATE - 10, D), F32)], axis=0)
    grads = dict(ffn2_w_gate=(sum_g2, [got_g2_near, got_g2_far]), ffn2_w_up=(sum_u2, [got_u2]), ffn2_w_down=down2,
                 mix_w_in=(sum_in, [got_in_near, got_in_far]), w_branch_a=(sum_a, [got_a]), w_branch_b=(sum_b, [got_b]),
                 w_out=(sum_o, [got_o]), ffn1_w_down=down1)
    done, (late_all, got_g1, got_u1) = _adamw_group(
        "adamw_most", [(cs, got, *[a[0] for a in given[n]]) for n, (cs, got) in grads.items()],
        rider=[_gather_rider([packed_late]), _chip_rider([sum_g1, sum_u1])])
    last, _ = _adamw_group("adamw_ffn1_in", [(sum_g1, [got_g1], *[a[0] for a in given["ffn1_w_gate"]]),
                                            (sum_u1, [got_u1], *[a[0] for a in given["ffn1_w_up"]])])
    big_out = {n: [o.reshape(given[n][0].shape) for o in outs]
               for n, outs in zip([*grads, "ffn1_w_gate", "ffn1_w_up"], [*done, *last])}

    flat = lambda a: a.reshape(1, -1)
    vectors = [("ada_b", 0, 9, ada_b, m_ada_b, v_ada_b), ("norm_ffn1", 9, 1, norm_ffn1, m_norm_ffn1, v_norm_ffn1),
               ("norm_mix", 10, 1, norm_mix, m_norm_mix, v_norm_mix), ("norm_ffn2", 11, 1, norm_ffn2, m_norm_ffn2, v_norm_ffn2),
               ("norm_final", 12, 1, norm_final, m_norm_final, v_norm_final), ("mix_b_in", 13, 6, mix_b_in, m_mix_b_in, v_mix_b_in),
               ("sgu_ln_g", 19, 1, sgu_ln_g, m_sgu_ln_g, v_sgu_ln_g), ("sgu_ln_b", 20, 1, sgu_ln_b, m_sgu_ln_b, v_sgu_ln_b),
               ("sgu_b_s", 21, 1, sgu_b_s, m_sgu_b_s, v_sgu_b_s), ("conv_b", 22, 1, conv_b, m_conv_b, v_conv_b),
               ("conv_ln_g", 23, 1, conv_ln_g, m_conv_ln_g, v_conv_ln_g), ("conv_ln_b", 24, 1, conv_ln_b, m_conv_ln_b, v_conv_ln_b)]
    small_out, d_cw_all, loss_sum = _adamw_small(
        "adamw_small", packed_all, late_all, dws_all, [(row, rows, flat(wv), flat(mv), flat(vv)) for _, row, rows, wv, mv, vv in vectors],
        [a.reshape(HEADS * CHUNK, CHUNK) for a in (sgu_w_s, m_sgu_w_s, v_sgu_w_s)])
    small = {n: [o.reshape(wv.shape) for o in outs] for (n, _, _, wv, _, _), outs in zip(vectors, small_out)}
    small["sgu_w_s"] = [o.reshape(sgu_w_s.shape) for o in small_out[-1]]
    g_cw = lax.dynamic_slice(d_cw_all, (0, me * CHUNK), (KW, CHUNK))
    small["conv_w"] = [o.reshape(conv_w.shape) for o in (g_cw, *_adamw_plain("adamw_conv_w", g_cw, conv_w[0], m_conv_w[0], v_conv_w[0]))]
    loss = loss_sum[0, 0]

    dmod_cols = [lax.dynamic_slice(a[:, :N_MOD, :].reshape(NDEV, N_MOD * D), (0, me * ada_cols), (NDEV, ada_cols))
                 for a in (packed_all, late_all)]
    ada_out = [o.reshape(ada_w.shape) for o in _adamw_ada("adamw_ada_w", jnp.transpose(c_all), *dmod_cols, ada_w[0], m_ada_w[0], v_ada_w[0])]

    order = ["ada_w", "ada_b", "norm_ffn1", "ffn1_w_gate", "ffn1_w_up", "ffn1_w_down", "norm_mix", "mix_w_in", "mix_b_in",
             "sgu_ln_g", "sgu_ln_b", "sgu_w_s", "sgu_b_s", "conv_w", "conv_b", "conv_ln_g", "conv_ln_b", "w_branch_a",
             "w_branch_b", "w_out", "norm_ffn2", "ffn2_w_gate", "ffn2_w_up", "ffn2_w_down", "norm_final"]

    def leaf(n, kind):
        if n == "ada_w":
            return ada_out[kind]
        if n in big_out:
            return big_out[n][kind]
        return small[n][kind]

    return (loss, dx0.reshape(x.shape), *[leaf(n, kind) for kind in range(4) for n in order])
```

```python
import jax
import jax.numpy as jnp
from jax import lax
from jax.experimental import pallas as pl
from jax.experimental.pallas import tpu as pltpu

D = 1024
F = 4 * D
D_IN = 6 * D
HEADS = 8
CHUNK = 128
KW = 31
HALO = 32
N_MOD = 9
NDEV = 8
N_CHIPS = 4
EPS = 1e-6
LR, B1, B2, ADAM_EPS, WD, STEP = 0.001, 0.9, 0.999, 1e-08, 0.01, 10
BC1 = 1.0 - B1 ** STEP
BC2 = 1.0 - B2 ** STEP
VMEM_LIMIT = 56 * 1024 * 1024
MXU_COLS = 256
MESH = pl.DeviceIdType.MESH
HBM = pl.BlockSpec(memory_space=pltpu.HBM)
VMEM = pl.BlockSpec(memory_space=pltpu.VMEM)
BF = jnp.bfloat16
F32 = jnp.float32

NN = (((1,), (0,)), ((), ()))
NT = (((1,), (1,)), ((), ()))
TN = (((0,), (0,)), ((), ()))

R_CW, R_LOSS, R_TOTAL = 25, 56, 64
R_LATE = 16


def _params(sem):
    return pltpu.CompilerParams(dimension_semantics=sem, vmem_limit_bytes=VMEM_LIMIT)


def _position():
    return lax.axis_index("x"), lax.axis_index("y"), lax.axis_index("c")


def _flip(pos, k):
    x, y, c = pos
    return (x ^ (k >> 2 & 1), y ^ (k >> 1 & 1), c ^ (k & 1))


def _index(pos):
    return 4 * pos[0] + 2 * pos[1] + pos[2]


def _gather_rows(x_ref, out_ref, send_sems, recv_sems, local_sem):
    m_per = x_ref.shape[0]
    x, y, c = _position()
    me, sibling = (x, y, c), (x, y, 1 - c)
    chips = [(1 - x, y), (x, 1 - y), (1 - x, 1 - y)]

    def rows(pos):
        return out_ref.at[pl.ds(_index(pos) * m_per, m_per), :]

    def copy(k, block, to, src=None):
        return pltpu.make_async_remote_copy(
            src_ref=rows(block) if src is None else src, dst_ref=rows(block),
            send_sem=send_sems.at[k], recv_sem=recv_sems.at[k], device_id=to, device_id_type=MESH)

    mine = pltpu.make_async_copy(x_ref, rows(me), local_sem)
    mine.start()
    first = [copy(0, me, sibling, src=x_ref)]
    first += [copy(1 + j, me, (*chip, c), src=x_ref) for j, chip in enumerate(chips)]
    for cp in first:
        cp.start()
    passed = [copy(4 + j, (*chip, c), sibling) for j, chip in enumerate(chips)]
    for j, chip in enumerate(chips):
        copy(1 + j, (*chip, c), me).wait_recv()
        passed[j].start()
    copy(0, sibling, me).wait_recv()
    for j, chip in enumerate(chips):
        copy(4 + j, (*chip, 1 - c), me).wait_recv()
    for cp in first + passed:
        cp.wait_send()
    mine.wait()


def _prologue(name, c_rows, taps, ada_w, ada_b, shards):
    rider = _gather_rider(shards)
    n = len(shards)
    nc = ada_w.shape[1]

    def body(*refs):
        c_ref, taps_ref, w_ref, b_ref = refs[:4]
        shard_refs = refs[4:4 + n]
        c_all_ref, taps_all_ref, mod_all_ref = refs[4 + n:7 + n]
        gathered_refs = refs[7 + n:7 + 2 * n]
        c_buf, mod_part, sems = refs[7 + 2 * n], refs[8 + 2 * n], refs[9 + 2 * n:]
        rider.start(shard_refs, gathered_refs, sems[9:])
        _gather_rows(c_ref, c_buf, *sems[0:3])
        c_all = jnp.concatenate([c_buf[pl.ds(d * SUBLANES, 1), :] for d in range(NDEV)], axis=0)
        c_all_ref[...] = c_all
        mod_part[...] = jnp.dot(_silu(c_all), w_ref[...], preferred_element_type=F32) + b_ref[...]
        _gather_rows(taps_ref, taps_all_ref, *sems[3:6])
        _gather_rows(mod_part, mod_all_ref, *sems[6:9])
        rider.mid(shard_refs, gathered_refs, sems[9:])
        rider.relay(shard_refs, gathered_refs, sems[9:])
        rider.finish(shard_refs, gathered_refs, sems[9:])

    small_sems = [pltpu.SemaphoreType.DMA((7,)), pltpu.SemaphoreType.DMA((7,)), pltpu.SemaphoreType.DMA] * 3
    res = pl.pallas_call(
        body, name=name,
        out_shape=[jax.ShapeDtypeStruct((NDEV, D), F32), jax.ShapeDtypeStruct((NDEV * taps.shape[0], taps.shape[1]), F32),
                   jax.ShapeDtypeStruct((NDEV * NDEV, nc), F32)] + rider.out_shapes,
        in_specs=[VMEM] * 4 + [HBM] * n, out_specs=[VMEM] * 3 + [HBM] * n,
        scratch_shapes=[pltpu.VMEM((NDEV * SUBLANES, D), F32), pltpu.VMEM((NDEV, nc), F32)] + small_sems + rider.sems,
        compiler_params=_params(None),
    )(c_rows, taps, ada_w, ada_b, *shards)
    return res[0], res[1], res[2], res[3:]


class _Rider:
    def __init__(self, ins, out_shapes, sems, start, finish, mid=None, relay=None):
        self.ins, self.out_shapes, self.sems = list(ins), list(out_shapes), list(sems)
        self.start, self.finish, self.mid, self.relay = start, finish, mid, relay


def _gather_rider(shards):
    n = len(shards)

    def setup(ins, outs, sems):
        send_sems, recv_sems, local_sems = sems
        x, y, c = _position()
        places = dict(me=(x, y, c), sibling=(x, y, 1 - c), xn=(1 - x, y, c), yn=(x, 1 - y, c), diagonal=(1 - x, 1 - y, c),
                      passed_on=(x ^ c, y ^ (1 - c), c), passed_to=(x ^ (1 - c), y ^ c, c))

        def copy(a, k, block, to, own=False):
            slot = outs[a].at[_index(block)]
            return pltpu.make_async_remote_copy(
                src_ref=ins[a] if own else slot, dst_ref=slot,
                send_sem=send_sems.at[k, a], recv_sem=recv_sems.at[k, a], device_id=to, device_id_type=MESH)

        def local(a):
            return pltpu.make_async_copy(ins[a], outs[a].at[_index(places["me"])], local_sems.at[a])

        return places, copy, local

    def start(ins, outs, sems):
        p, copy, local = setup(ins, outs, sems)
        for a in range(n):
            local(a).start()
            for k, to in enumerate(("sibling", "xn", "yn")):
                copy(a, k, p["me"], p[to], own=True).start()

    def mid(ins, outs, sems):
        p, copy, _ = setup(ins, outs, sems)
        for a in range(n):
            copy(a, 1, p["xn"], p["me"]).wait_recv()
            copy(a, 2, p["yn"], p["me"]).wait_recv()
            copy(a, 3, p["passed_on"], p["passed_to"]).start()
            copy(a, 4, p["xn"], p["sibling"]).start()
            copy(a, 5, p["yn"], p["sibling"]).start()

    def relay(ins, outs, sems):
        p, copy, _ = setup(ins, outs, sems)
        for a in range(n):
            copy(a, 3, p["diagonal"], p["me"]).wait_recv()
            copy(a, 6, p["diagonal"], p["sibling"]).start()

    def finish(ins, outs, sems):
        p, copy, local = setup(ins, outs, sems)
        x, y, c = p["me"]
        for a in range(n):
            for k, block in ((0, (x, y, 1 - c)), (4, (1 - x, y, 1 - c)), (5, (x, 1 - y, 1 - c)), (6, (1 - x, 1 - y, 1 - c))):
                copy(a, k, block, p["me"]).wait_recv()
            for k, to in enumerate(("sibling", "xn", "yn")):
                copy(a, k, p["me"], p[to], own=True).wait_send()
            copy(a, 3, p["passed_on"], p["passed_to"]).wait_send()
            for k, block in ((4, "xn"), (5, "yn"), (6, "diagonal")):
                copy(a, k, p[block], p["sibling"]).wait_send()
            local(a).wait()

    return _Rider(shards, [jax.ShapeDtypeStruct((NDEV, *s.shape), s.dtype) for s in shards],
                  [pltpu.SemaphoreType.DMA((7, n)), pltpu.SemaphoreType.DMA((7, n)), pltpu.SemaphoreType.DMA((n,))],
                  start, finish, mid, relay)


def _pair_rider(parts):
    n = len(parts)

    def copies(ins, outs, sems):
        send_sems, recv_sems = sems
        x, y, c = _position()
        q = 2 * x + y
        return [pltpu.make_async_remote_copy(
            src_ref=ins[a].at[2 * (q ^ k) + (1 - c)], dst_ref=outs[a].at[k],
            send_sem=send_sems.at[k, a], recv_sem=recv_sems.at[k, a], device_id=(x, y, 1 - c), device_id_type=MESH)
            for a in range(n) for k in range(N_CHIPS)]

    def start(ins, outs, sems):
        for cp in copies(ins, outs, sems):
            cp.start()

    def finish(ins, outs, sems):
        for cp in copies(ins, outs, sems):
            cp.wait()

    return _Rider(parts, [jax.ShapeDtypeStruct((N_CHIPS, *p.shape[1:]), p.dtype) for p in parts],
                  [pltpu.SemaphoreType.DMA((N_CHIPS, n)), pltpu.SemaphoreType.DMA((N_CHIPS, n))], start, finish)


NEIGHBOURS = (1, 2)
DIAGONAL = (3,)
OTHER_CHIPS = NEIGHBOURS + DIAGONAL


def _chip_rider(sums, ks=OTHER_CHIPS):
    n = len(sums)

    def copies(ins, outs, sems):
        send_sems, recv_sems = sems
        me = _position()
        return [pltpu.make_async_remote_copy(
            src_ref=ins[a].at[k], dst_ref=outs[a].at[j],
            send_sem=send_sems.at[j, a], recv_sem=recv_sems.at[j, a], device_id=_flip(me, 2 * k), device_id_type=MESH)
            for a in range(n) for j, k in enumerate(ks)]

    def start(ins, outs, sems):
        for cp in copies(ins, outs, sems):
            cp.start()

    def finish(ins, outs, sems):
        for cp in copies(ins, outs, sems):
            cp.wait()

    return _Rider(sums, [jax.ShapeDtypeStruct((len(ks), *s.shape[1:]), s.dtype) for s in sums],
                  [pltpu.SemaphoreType.DMA((len(ks), n)), pltpu.SemaphoreType.DMA((len(ks), n))], start, finish)


def _grid_edge(grid, last):
    cond = None
    for d, n in enumerate(grid):
        here = pl.program_id(d) == (n - 1 if last else 0)
        cond = here if cond is None else jnp.logical_and(cond, here)
    return cond


def _call(name, compute, grid, ins, in_specs, out_shapes, out_specs, scratch_shapes, semantics, rider=None, aliases=None):
    riders = [rider] if isinstance(rider, _Rider) else list(rider or [])
    n_in, n_out, n_scr = len(ins), len(out_shapes), len(scratch_shapes)
    n_rin, n_rout, n_rsem = [sum(len(part(r)) for r in riders) for part in (lambda r: r.ins, lambda r: r.out_shapes, lambda r: r.sems)]
    cuts = [0, n_in, n_in + n_rin, n_in + n_rin + n_out, n_in + n_rin + n_out + n_rout, n_in + n_rin + n_out + n_rout + n_scr]

    def body(*refs):
        in_refs, rin_refs, out_refs, rout_refs, scr_refs = [refs[a:b] for a, b in zip(cuts[:-1], cuts[1:])]
        rsem_refs = refs[cuts[-1]:]
        mine, at = [], [0, 0, 0]
        for r in riders:
            mine.append((r, rin_refs[at[0]:at[0] + len(r.ins)], rout_refs[at[1]:at[1] + len(r.out_shapes)],
                         rsem_refs[at[2]:at[2] + len(r.sems)]))
            at = [at[0] + len(r.ins), at[1] + len(r.out_shapes), at[2] + len(r.sems)]
        if riders:
            @pl.when(_grid_edge(grid, last=False))
            def _():
                for r, a, b, c in mine:
                    r.start(a, b, c)

        if any(r.mid for r in riders):
            step, steps = 0, 1
            for d, size in enumerate(grid):
                step, steps = step * size + pl.program_id(d), steps * size

            @pl.when(step == steps * 5 // 8)
            def _():
                for r, a, b, c in mine:
                    if r.mid:
                        r.mid(a, b, c)

        if any(r.relay for r in riders):
            @pl.when(_grid_edge(grid, last=True))
            def _():
                for r, a, b, c in mine:
                    if r.relay:
                        r.relay(a, b, c)

        compute(in_refs, out_refs, scr_refs)
        if riders:
            @pl.when(_grid_edge(grid, last=True))
            def _():
                for r, a, b, c in mine:
                    r.finish(a, b, c)

    res = pl.pallas_call(
        body, name=name, grid=grid,
        out_shape=list(out_shapes) + [s for r in riders for s in r.out_shapes],
        in_specs=list(in_specs) + [HBM] * n_rin, out_specs=list(out_specs) + [HBM] * n_rout,
        scratch_shapes=list(scratch_shapes) + [s for r in riders for s in r.sems],
        input_output_aliases=aliases or {}, compiler_params=_params(semantics),
    )(*ins, *[a for r in riders for a in r.ins])
    return (res[:n_out], res[n_out:]) if riders else res


def _pair_add(name, parts, from_sibling, slots):
    n = len(parts)

    def body(s_ref, *refs):
        for a in range(n):
            refs[2 * n + a][...] = (refs[a][...].astype(F32) + refs[n + a][...].astype(F32)).astype(refs[2 * n + a].dtype)

    def slab(p, picked):
        _, r, c = p.shape
        return pl.BlockSpec((None, r, c), (lambda k, s: (s[k], 0, 0)) if picked else (lambda k, s: (k, 0, 0)))

    return pl.pallas_call(
        body, name=name,
        grid_spec=pltpu.PrefetchScalarGridSpec(
            num_scalar_prefetch=1, grid=(N_CHIPS,),
            in_specs=[slab(p, True) for p in parts] + [slab(p, False) for p in parts],
            out_specs=[slab(p, False) for p in parts]),
        out_shape=[jax.ShapeDtypeStruct((N_CHIPS, *p.shape[1:]), p.dtype) for p in parts],
        compiler_params=_params(("arbitrary",)),
    )(slots, *parts, *from_sibling)


def _mm(name, pairs, dims, grid, nk, out_shapes, out_specs, extras=(), extra_specs=(), epilogue=None, acc_shape=None, rider=None,
        col_chunks=1):
    n_pairs = len(pairs)

    def compute(ins, outs, scratch):
        def partial_sum():
            total = None
            for p in range(n_pairs):
                d = lax.dot_general(ins[2 * p][...], ins[2 * p + 1][...], dims, preferred_element_type=F32)
                total = d if total is None else total + d
            return total

        def finish(r):
            ex = [e[...] for e in ins[2 * n_pairs:]]
            res = epilogue(r, *ex) if epilogue is not None else (r,)
            for o, v in zip(outs, res):
                o[...] = v.astype(o.dtype)

        if nk == 1 and col_chunks > 1:
            av = ins[0][...]
            width = outs[0].shape[1] // col_chunks
            for q in range(col_chunks):
                cols = pl.ds(q * width, width)
                r = lax.dot_general(av, ins[1][cols, :] if dims == NT else ins[1][:, cols], dims, preferred_element_type=F32)
                ex = [e[:, cols] if e.shape[1] == outs[0].shape[1] else e[...] for e in ins[2:]]
                for o, v in zip(outs, epilogue(r, *ex)):
                    o[:, cols] = v.astype(o.dtype)
        elif nk == 1:
            finish(partial_sum())
        else:
            acc = scratch[0]
            k = pl.program_id(2)

            @pl.when(k == 0)
            def _():
                acc[...] = partial_sum()

            @pl.when(k > 0)
            def _():
                acc[...] += partial_sum()

            @pl.when(k == nk - 1)
            def _():
                finish(acc[...])

    operands, specs = [], []
    for a, a_spec, b, b_spec in pairs:
        operands += [a, b]
        specs += [a_spec, b_spec]
    return _call(name, compute, grid, operands + list(extras), specs + list(extra_specs), out_shapes, out_specs,
                 [pltpu.VMEM(acc_shape, F32)] if nk > 1 else [], ("parallel", "parallel", "arbitrary"), rider)


def _single(res, rider):
    return (res[0][0], res[1]) if rider else res[0]


def _silu(x):
    return x * jax.nn.sigmoid(x)


def _ffn_up(name, h, wg, wu, rider=None):
    t = h.shape[0]
    tm = min(t, 1024)
    nb = F // NDEV

    def compute(ins, outs, _):
        hv = ins[0][...]
        for q in range(nb // MXU_COLS):
            cols = pl.ds(q * MXU_COLS, MXU_COLS)
            g = jnp.dot(hv, ins[1][:, cols], preferred_element_type=F32)
            u = jnp.dot(hv, ins[2][:, cols], preferred_element_type=F32)
            outs[0][:, cols] = g.astype(BF)
            outs[1][:, cols] = u.astype(BF)
            outs[2][:, cols] = (_silu(g) * u).astype(BF)

    w_spec = pl.BlockSpec((None, D, nb), lambda i, j: (j, 0, 0))
    o_spec = pl.BlockSpec((tm, nb), lambda i, j: (i, j))
    return _call(name, compute, (t // tm, NDEV), [h, wg, wu], [pl.BlockSpec((tm, D), lambda i, j: (i, 0)), w_spec, w_spec],
                 [jax.ShapeDtypeStruct((t, F), BF)] * 3, [o_spec] * 3, [], ("parallel", "arbitrary"), rider)


def _mm_nn(name, a, b, tm, tn, tk, extras=(), extra_specs=(), epilogue=None, out_dtypes=(F32,), rider=None):
    m, kk = a.shape
    n = b.shape[1]
    nk = kk // tk
    return _mm(
        name, [(a, pl.BlockSpec((tm, tk), lambda i, j, k: (i, k)), b, pl.BlockSpec((tk, tn), lambda i, j, k: (k, j)))], NN,
        (m // tm, n // tn, nk), nk,
        [jax.ShapeDtypeStruct((m, n), dt) for dt in out_dtypes],
        [pl.BlockSpec((tm, tn), lambda i, j, k: (i, j))] * len(out_dtypes),
        extras, extra_specs, epilogue, (tm, tn), rider)


def _mm_nn_blocked(name, a, b3, tm, rider=None):
    m = a.shape[0]
    nb = b3.shape[2]
    return _single(_mm(
        name, [(a, pl.BlockSpec((tm, D), lambda i, j, k: (i, 0)), b3, pl.BlockSpec((None, D, nb), lambda i, j, k: (j, 0, 0)))], NN,
        (m // tm, NDEV, 1), 1,
        [jax.ShapeDtypeStruct((m, NDEV * nb), F32)], [pl.BlockSpec((tm, nb), lambda i, j, k: (i, j))], rider=rider), rider)


def _mm_nt(name, a, b, tm, tn, out_dtypes=(F32,), extras=(), extra_specs=(), epilogue=None, rider=None, col_chunks=1):
    m, kk = a.shape
    n = b.shape[0]
    return _mm(
        name, [(a, pl.BlockSpec((tm, kk), lambda i, j, k: (i, 0)), b, pl.BlockSpec((tn, kk), lambda i, j, k: (j, 0)))], NT,
        (m // tm, n // tn, 1), 1,
        [jax.ShapeDtypeStruct((m, n), dt) for dt in out_dtypes],
        [pl.BlockSpec((tm, tn), lambda i, j, k: (i, j))] * len(out_dtypes),
        extras, extra_specs, epilogue, rider=rider, col_chunks=col_chunks)


def _mm_nt_blocked(name, a_list, b3_list, tm, rider=None):
    m = a_list[0].shape[0]
    nb = b3_list[0].shape[2]
    pairs = [(a, pl.BlockSpec((tm, nb), lambda i, j, k: (i, k)), b3, pl.BlockSpec((None, D, nb), lambda i, j, k: (k, 0, 0)))
             for a, b3 in zip(a_list, b3_list)]
    return _single(_mm(name, pairs, NT, (m // tm, 1, NDEV), NDEV,
                       [jax.ShapeDtypeStruct((m, D), F32)], [pl.BlockSpec((tm, D), lambda i, j, k: (i, 0))],
                       acc_shape=(tm, D), rider=rider), rider)


def _mm_tn(name, a, b, tm, tn, rider=None):
    t, m = a.shape
    n = b.shape[1]
    return _single(_mm(
        name, [(a, pl.BlockSpec((t, tm), lambda i, j, k: (0, i)), b, pl.BlockSpec((t, tn), lambda i, j, k: (0, j)))], TN,
        (m // tm, n // tn, 1), 1,
        [jax.ShapeDtypeStruct((m, n), BF)], [pl.BlockSpec((tm, tn), lambda i, j, k: (i, j))], rider=rider), rider)


def _mm_tn_blocked(name, a, b, rider=None):
    t = a.shape[0]
    nb = b.shape[1] // NDEV
    return _single(_mm(
        name, [(a, pl.BlockSpec((t, D), lambda i, j, k: (0, 0)), b, pl.BlockSpec((t, nb), lambda i, j, k: (0, j)))], TN,
        (1, NDEV, 1), 1,
        [jax.ShapeDtypeStruct((NDEV, D, nb), BF)], [pl.BlockSpec((None, D, nb), lambda i, j, k: (j, 0, 0))], rider=rider), rider)


def _dw_gate_up(name, h, dgate, dup, rider=None):
    t = h.shape[0]
    nb = F // NDEV

    def compute(ins, outs, _):
        hv = ins[0][...]
        outs[0][...] = lax.dot_general(hv, ins[1][...], TN, preferred_element_type=F32).astype(BF)
        outs[1][...] = lax.dot_general(hv, ins[2][...], TN, preferred_element_type=F32).astype(BF)

    d_spec = pl.BlockSpec((t, nb), lambda j: (0, j))
    o_spec = pl.BlockSpec((None, D, nb), lambda j: (j, 0, 0))
    return _call(name, compute, (NDEV,), [h, dgate, dup], [pl.BlockSpec((t, D), lambda j: (0, 0)), d_spec, d_spec],
                 [jax.ShapeDtypeStruct((NDEV, D, nb), BF)] * 2, [o_spec] * 2, [], ("arbitrary",), rider)


def _rowcall(name, fn, ins, in_specs, n_row_out, out_shapes, out_specs, grid, scratch_shapes=(), rider=None, aliases=None):
    def accumulate(o, v, i):
        @pl.when(i == 0)
        def _():
            o[...] = v.astype(o.dtype)

        @pl.when(i > 0)
        def _():
            o[...] += v.astype(o.dtype)

    def compute(in_refs, out_refs, scr):
        i = pl.program_id(0)
        vals = fn(i, in_refs, scr)
        for idx, (o, v) in enumerate(zip(out_refs, vals)):
            if idx < n_row_out:
                o[...] = v.astype(o.dtype)
            else:
                accumulate(o, v, i)

    return _call(name, compute, (grid,), ins, in_specs, out_shapes, out_specs, list(scratch_shapes), ("arbitrary",), rider, aliases)


def _rows(tr, w=D, cb=0):
    return pl.BlockSpec((tr, w), lambda i: (i, cb))


def _whole(shape):
    nd = len(shape)
    return pl.BlockSpec(shape, lambda i: (0,) * nd)


def _vec(n=1):
    return jax.ShapeDtypeStruct((n, D), F32)


def _rms_mod(x, gain, sc, sh):
    y = x * lax.rsqrt(jnp.mean(x * x, axis=-1, keepdims=True) + EPS)
    return (y * gain) * (1.0 + sc) + sh


def _layer_norm(x, g, b):
    mu = jnp.mean(x, axis=-1, keepdims=True)
    var = jnp.mean(jnp.square(x - mu), axis=-1, keepdims=True)
    return (x - mu) * lax.rsqrt(var + EPS) * g + b


def _norm_mod(name, x, gain, sc, sh):
    t = x.shape[0]
    tr = min(t, 256)

    def fn(i, r, _):
        return [_rms_mod(r[0][...], r[1][...], r[2][...], r[3][...])]

    return _rowcall(name, fn, [x, gain, sc, sh], [_rows(tr)] + [_whole((1, D))] * 3, 1,
                    [jax.ShapeDtypeStruct((t, D), BF)], [_rows(tr)], t // tr)[0]


def _gate_grads(dx, f, g, scale):
    return scale * g * dx, jnp.sum(scale * dx * f.astype(F32), axis=0, keepdims=True)


def _norm_mod_bwd(name, x, gain, sc, sh, dh, dres, below=None):
    t = x.shape[0]
    tr = min(t, 256)

    def fn(i, r, _):
        _, vjp = jax.vjp(_rms_mod, r[0][...], r[1][...], r[2][...], r[3][...])
        dx, dgain, dsc, dsh = vjp(r[4][...])
        dx = dx + r[5][...]
        if below is None:
            return [dx, dgain, dsc, dsh]
        df, dg = _gate_grads(dx, r[6][...], r[7][...], below[2])
        return [dx, df, dgain, dsc, dsh, dg]

    ins, specs = [x, gain, sc, sh, dh, dres], [_rows(tr)] + [_whole((1, D))] * 3 + [_rows(tr)] * 2
    outs, out_specs = [jax.ShapeDtypeStruct((t, D), F32)], [_rows(tr)]
    if below is not None:
        ins, specs = ins + [below[0], below[1]], specs + [_rows(tr), _whole((1, D))]
        outs, out_specs = outs + [jax.ShapeDtypeStruct((t, D), BF)], out_specs + [_rows(tr)]
    n_vec = 3 if below is None else 4
    return _rowcall(name, fn, ins, specs, len(outs), outs + [_vec()] * n_vec, out_specs + [_whole((1, D))] * n_vec, t // tr)


def _sgu_pre(up, vp, bu, bv, ln_g, ln_b):
    return jax.nn.gelu(up + bu), _layer_norm(jax.nn.gelu(vp + bv), ln_g, ln_b)


def _causal(w_ref, h):
    rows = lax.broadcasted_iota(jnp.int32, (CHUNK, CHUNK), 0)
    cols = lax.broadcasted_iota(jnp.int32, (CHUNK, CHUNK), 1)
    return jnp.where(cols <= rows, w_ref[h], 0.0)


def _sgu(name, proj, b_in, ln_g, ln_b, w_s, bias_full, rider=None):
    t = proj.shape[0]

    def fn(i, r, _):
        u, v = _sgu_pre(r[0][...], r[1][...], r[2][...], r[3][...], r[4][...], r[5][...])
        vb = v.astype(BF)
        mixed = [jnp.dot(_causal(r[6], h).astype(BF), vb[:, h * CHUNK:(h + 1) * CHUNK], preferred_element_type=F32)
                 for h in range(HEADS)]
        return [u * (jnp.concatenate(mixed, axis=1) + r[7][...])]

    return _rowcall(
        name, fn, [proj, proj, b_in, b_in, ln_g, ln_b, w_s, bias_full],
        [_rows(CHUNK, D, 0), _rows(CHUNK, D, 1), pl.BlockSpec((1, D), lambda i: (0, 0)), pl.BlockSpec((1, D), lambda i: (0, 1)),
         _whole((1, D)), _whole((1, D)), _whole((HEADS, CHUNK, CHUNK)), _whole((CHUNK, D))],
        1, [jax.ShapeDtypeStruct((t, D), BF)], [_rows(CHUNK)], t // CHUNK, rider=rider)


def _sgu_bwd(name, proj, b_in, ln_g, ln_b, w_s, bias_full, dout, dproj, rider=None):
    t = proj.shape[0]

    def fn(i, r, _):
        (u, v), vjp = jax.vjp(_sgu_pre, r[0][...], r[1][...], r[2][...], r[3][...], r[4][...], r[5][...])
        vb = v.astype(BF)
        d = r[8][...]
        masks = [_causal(r[6], h).astype(BF) for h in range(HEADS)]
        cols = [slice(h * CHUNK, (h + 1) * CHUNK) for h in range(HEADS)]
        mixed = jnp.concatenate([jnp.dot(masks[h], vb[:, cols[h]], preferred_element_type=F32) for h in range(HEADS)], axis=1)
        du = d * (mixed + r[7][...])
        dmix = d * u
        dmb = dmix.astype(BF)
        dv = jnp.concatenate([lax.dot_general(masks[h], dmb[:, cols[h]], TN, preferred_element_type=F32) for h in range(HEADS)], axis=1)
        rows = lax.broadcasted_iota(jnp.int32, (CHUNK, CHUNK), 0)
        lanes = lax.broadcasted_iota(jnp.int32, (CHUNK, CHUNK), 1)
        dws = jnp.stack([jnp.where(lanes <= rows, lax.dot_general(dmb[:, cols[h]], vb[:, cols[h]], NT, preferred_element_type=F32), 0.0)
                         for h in range(HEADS)])
        dbs = jnp.zeros((CHUNK, CHUNK), F32)
        for h in range(HEADS):
            dbs = dbs + jnp.where(lanes == h, jnp.sum(dmix[:, cols[h]], axis=1, keepdims=True), 0.0)
        dup, dvp, dbu, dbv, dg, db = vjp((du, dv))
        return [jnp.concatenate([dup, dvp], axis=1), dbu, dbv, dg, db, dws, dbs]

    return _rowcall(
        name, fn, [proj, proj, b_in, b_in, ln_g, ln_b, w_s, bias_full, dout, dproj],
        [_rows(CHUNK, D, 0), _rows(CHUNK, D, 1), pl.BlockSpec((1, D), lambda i: (0, 0)), pl.BlockSpec((1, D), lambda i: (0, 1)),
         _whole((1, D)), _whole((1, D)), _whole((HEADS, CHUNK, CHUNK)), _whole((CHUNK, D)), _rows(CHUNK),
         pl.BlockSpec(memory_space=pl.ANY)],
        1, [jax.ShapeDtypeStruct(dproj.shape, dproj.dtype)] + [_vec()] * 4
        + [jax.ShapeDtypeStruct((HEADS, CHUNK, CHUNK), F32), jax.ShapeDtypeStruct((CHUNK, CHUNK), F32)],
        [pl.BlockSpec((CHUNK, 2 * D), lambda i: (i, 0))] + [_whole((1, D))] * 4 + [_whole((HEADS, CHUNK, CHUNK)), _whole((CHUNK, CHUNK))],
        t // CHUNK, rider=rider, aliases={9: 0})


def _halo_before(tr, cb):
    return pl.BlockSpec((HALO, D), lambda i: (jnp.maximum(i * (tr // HALO) - 1, 0), cb))


def _halo_after(tr, cb, n_tiles):
    return pl.BlockSpec((HALO, D), lambda i: (jnp.minimum((i + 1) * (tr // HALO), n_tiles * (tr // HALO) - 1), cb))


def _ln_silu(z, g, b):
    return _silu(_layer_norm(z, g, b))


SUBLANES = 8
LANES = 128
CONV_STRIP = 16
DW_STRIP = 32


def _shifted_copies(buf, copies, rows):
    for b in range(1, SUBLANES):
        copies[b - 1, pl.ds(0, rows), :] = buf[pl.ds(b, rows), :]


def _shifted(buf, copies, offset, start, rows, lanes=slice(None)):
    at = pl.ds(pl.multiple_of(start + SUBLANES * (offset // SUBLANES), SUBLANES), rows)
    return buf[at, lanes] if offset % SUBLANES == 0 else copies[offset % SUBLANES - 1, at, lanes]


def _accumulate(o, v, i):
    @pl.when(i == 0)
    def _():
        o[...] = v.astype(o.dtype)

    @pl.when(i > 0)
    def _():
        o[...] += v.astype(o.dtype)


def _conv(name, proj, b_in, conv_w, conv_b, ln_g, ln_b, rider=None):
    t = proj.shape[0]
    tr = min(t, 256)

    def compute(r, outs, scr):
        zbuf, zs = scr
        i = pl.program_id(0)
        bv, bg = r[4][...], r[5][...]
        z0 = (r[0][...] + bv) * jax.nn.sigmoid(r[1][...] + bg)
        before = (r[2][...] + bv) * jax.nn.sigmoid(r[3][...] + bg)
        zbuf[pl.ds(0, HALO), :] = jnp.where(i > 0, before, 0.0)
        zbuf[pl.ds(HALO, tr), :] = z0
        outs[0][...] = z0
        _shifted_copies(zbuf, zs, tr + HALO - SUBLANES)

        def strip(s, carry):
            r0 = s * CONV_STRIP
            acc = jnp.zeros((CONV_STRIP, D), F32) + r[7][...]
            for k in range(KW):
                acc = acc + r[6][k:k + 1, :] * _shifted(zbuf, zs, HALO - (KW - 1) + k, r0, CONV_STRIP)
            outs[1][pl.ds(pl.multiple_of(r0, SUBLANES), CONV_STRIP), :] = acc
            return carry

        lax.fori_loop(0, tr // CONV_STRIP, strip, 0)
        outs[2][...] = _ln_silu(outs[1][...], r[8][...], r[9][...]).astype(BF)

    return _call(
        name, compute, (t // tr,), [proj, proj, proj, proj, b_in, b_in, conv_w, conv_b, ln_g, ln_b],
        [_rows(tr, D, 2), _rows(tr, D, 3), _halo_before(tr, 2), _halo_before(tr, 3),
         pl.BlockSpec((1, D), lambda i: (0, 2)), pl.BlockSpec((1, D), lambda i: (0, 3)),
         _whole((HALO, D)), _whole((1, D)), _whole((1, D)), _whole((1, D))],
        [jax.ShapeDtypeStruct((t, D), F32), jax.ShapeDtypeStruct((t, D), F32), jax.ShapeDtypeStruct((t, D), BF)],
        [_rows(tr)] * 3, [pltpu.VMEM((tr + HALO, D), F32), pltpu.VMEM((SUBLANES - 1, tr + HALO, D), F32)], ("arbitrary",), rider)


def _conv_bwd(name, proj, b_in, conv_w, ln_g, ln_b, z0, z1, dz3, dproj, rider=None):
    t = proj.shape[0]
    tr = min(t, 256)
    n_tiles = t // tr

    def compute(r, outs, scr):
        zbuf, dbuf, zs, ds, dwacc = scr
        i = pl.program_id(0)
        g, b = r[5][...], r[6][...]
        _, vjp = jax.vjp(_ln_silu, r[9][...], g, b)
        dz1, dg, db = vjp(r[11][...])
        _, vjp_after = jax.vjp(_ln_silu, r[10][...], g, b)
        dz1_after = vjp_after(r[12][...])[0]
        dbuf[pl.ds(0, tr), :] = dz1
        dbuf[pl.ds(tr, HALO), :] = jnp.where(i < n_tiles - 1, dz1_after, 0.0)
        zbuf[pl.ds(0, HALO), :] = jnp.where(i > 0, r[8][...], 0.0)
        zbuf[pl.ds(HALO, tr), :] = r[7][...]
        _shifted_copies(dbuf, ds, tr + HALO - SUBLANES)
        _shifted_copies(zbuf, zs, tr + HALO - SUBLANES)

        def dz0_strip(s, carry):
            r0 = s * CONV_STRIP
            at = pl.ds(pl.multiple_of(r0, CONV_STRIP), CONV_STRIP)
            acc = jnp.zeros((CONV_STRIP, D), F32)
            for k in range(KW):
                acc = acc + r[4][k:k + 1, :] * _shifted(dbuf, ds, KW - 1 - k, r0, CONV_STRIP)
            a = r[0][at, :] + r[2][...]
            sg = jax.nn.sigmoid(r[1][at, :] + r[3][...])
            dcv = acc * sg
            dcg = acc * a * sg * (1.0 - sg)
            outs[0][at, :] = jnp.concatenate([dcv, dcg], axis=1).astype(BF)
            return carry[0] + jnp.sum(dcv, axis=0, keepdims=True), carry[1] + jnp.sum(dcg, axis=0, keepdims=True)

        zero_row = jnp.zeros((1, D), F32)
        dbv, dbg = lax.fori_loop(0, tr // CONV_STRIP, dz0_strip, (zero_row, zero_row))

        for lb in range(D // LANES):
            lanes = slice(lb * LANES, (lb + 1) * LANES)

            def dw_strip(s, accs, lanes=lanes):
                r0 = s * DW_STRIP
                dz = dbuf[pl.ds(pl.multiple_of(r0, SUBLANES), DW_STRIP), lanes]
                out = []
                for k in range(KW):
                    prod = dz * _shifted(zbuf, zs, HALO - (KW - 1) + k, r0, DW_STRIP, lanes)
                    part = prod[0:SUBLANES]
                    for q in range(1, DW_STRIP // SUBLANES):
                        part = part + prod[q * SUBLANES:(q + 1) * SUBLANES]
                    out.append(accs[k] + part)
                return tuple(out)

            accs = lax.fori_loop(0, tr // DW_STRIP, dw_strip, tuple(jnp.zeros((SUBLANES, LANES), F32) for _ in range(KW)))
            for k in range(KW):
                dwacc[pl.ds(k * SUBLANES, SUBLANES), lanes] = accs[k]
        dw_rows = [jnp.sum(dwacc[pl.ds(k * SUBLANES, SUBLANES), :], axis=0, keepdims=True) for k in range(KW)]
        dw_rows.append(jnp.zeros((HALO - KW, D), F32))
        for o, v in zip(outs[1:], (dbv, dbg, jnp.concatenate(dw_rows, axis=0), jnp.sum(dz1, axis=0, keepdims=True), dg, db)):
            _accumulate(o, v, i)

    wide = pl.BlockSpec((tr, 2 * D), lambda i: (i, 1))
    return _call(
        name, compute, (n_tiles,), [proj, proj, b_in, b_in, conv_w, ln_g, ln_b, z0, z0, z1, z1, dz3, dz3, dproj],
        [_rows(tr, D, 2), _rows(tr, D, 3), pl.BlockSpec((1, D), lambda i: (0, 2)), pl.BlockSpec((1, D), lambda i: (0, 3)),
         _whole((HALO, D)), _whole((1, D)), _whole((1, D)),
         _rows(tr), _halo_before(tr, 0), _rows(tr), _halo_after(tr, 0, n_tiles), _rows(tr), _halo_after(tr, 0, n_tiles),
         pl.BlockSpec(memory_space=pl.ANY)],
        [jax.ShapeDtypeStruct(dproj.shape, dproj.dtype), _vec(), _vec(), _vec(HALO), _vec(), _vec(), _vec()],
        [wide] + [_whole((1, D))] * 2 + [_whole((HALO, D))] + [_whole((1, D))] * 3,
        [pltpu.VMEM((tr + HALO, D), F32), pltpu.VMEM((tr + HALO, D), F32),
         pltpu.VMEM((SUBLANES - 1, tr + HALO, D), F32), pltpu.VMEM((SUBLANES - 1, tr + HALO, D), F32),
         pltpu.VMEM((HALO * SUBLANES, D), F32)],
        ("arbitrary",), rider, aliases={13: 0})


def _merge_fn(ga, gb, bga, bgb, ya, yb):
    return jax.nn.sigmoid(ga + bga) * ya + jax.nn.sigmoid(gb + bgb) * yb


def _merge(name, proj, b_in, ya, yb, rider=None):
    t = proj.shape[0]
    tr = min(t, 256)

    def fn(i, r, _):
        return [_merge_fn(*[x[...] for x in r])]

    return _rowcall(
        name, fn, [proj, proj, b_in, b_in, ya, yb],
        [_rows(tr, D, 4), _rows(tr, D, 5), pl.BlockSpec((1, D), lambda i: (0, 4)), pl.BlockSpec((1, D), lambda i: (0, 5)),
         _rows(tr), _rows(tr)],
        1, [jax.ShapeDtypeStruct((t, D), BF)], [_rows(tr)], t // tr, rider=rider)


def _merge_bwd(name, proj, b_in, ya, yb, dm, rider=None):
    t = proj.shape[0]
    tr = min(t, 256)

    def fn(i, r, _):
        _, vjp = jax.vjp(_merge_fn, *[x[...] for x in r[:6]])
        dga, dgb, dbga, dbgb, dya, dyb = vjp(r[6][...])
        return [jnp.concatenate([dga, dgb], axis=1), dya, dyb, dbga, dbgb]

    return _rowcall(
        name, fn, [proj, proj, b_in, b_in, ya, yb, dm],
        [_rows(tr, D, 4), _rows(tr, D, 5), pl.BlockSpec((1, D), lambda i: (0, 4)), pl.BlockSpec((1, D), lambda i: (0, 5)),
         _rows(tr), _rows(tr), _rows(tr)],
        3, [jax.ShapeDtypeStruct((t, D_IN), BF)] + [jax.ShapeDtypeStruct((t, D), BF)] * 2 + [_vec(), _vec()],
        [pl.BlockSpec((tr, 2 * D), lambda i: (i, 2))] + [_rows(tr)] * 2 + [_whole((1, D))] * 2, t // tr, rider=rider)


def _loss_head(name, x, gain, target, f, g, scale):
    t = x.shape[0]
    tr = min(t, 256)

    def loss_fn(xv, gn, tgt):
        y = xv * lax.rsqrt(jnp.mean(xv * xv, axis=-1, keepdims=True) + EPS) * gn
        return 0.5 * jnp.sum(jnp.mean(jnp.square(y - tgt), axis=-1))

    def fn(i, r, _):
        loss, vjp = jax.vjp(loss_fn, r[0][...], r[1][...], r[2][...])
        dx, dgain, _ = vjp(jnp.ones((), F32))
        df, dg = _gate_grads(dx, r[3][...], r[4][...], scale)
        return [dx, df, dgain, jnp.zeros((1, D), F32) + loss, dg]

    return _rowcall(name, fn, [x, gain, target, f, g], [_rows(tr), _whole((1, D)), _rows(tr), _rows(tr), _whole((1, D))], 2,
                    [jax.ShapeDtypeStruct((t, D), F32), jax.ShapeDtypeStruct((t, D), BF), _vec(), _vec(), _vec()],
                    [_rows(tr)] * 2 + [_whole((1, D))] * 3, t // tr)


def _adamw(w, g, m, v):
    m = B1 * m + (1.0 - B1) * g
    v = B2 * v + (1.0 - B2) * jnp.square(g)
    m_hat = m / BC1
    v_hat = v / BC2
    delta = -LR * (m_hat / (jnp.sqrt(v_hat) + ADAM_EPS) + WD * w)
    return delta, m, v


ADAMW_ROWS = 64


def _adamw_group(name, items, rider=None, rows=ADAMW_ROWS):
    ins, in_specs, out_shapes, out_specs, plan = [], [], [], [], []
    first = 0
    for chip_sum, received, w, m, v in items:
        r, c = w.shape
        tr = min(r, rows)
        n = r // tr

        def tile(i, first=first, n=n):
            return jnp.clip(i - first, 0, n - 1)

        spec = pl.BlockSpec((tr, c), lambda i, tile=tile: (tile(i), 0))
        ins += [chip_sum, *received, w, m, v]
        in_specs += [pl.BlockSpec((None, tr, c), lambda i, tile=tile: (0, tile(i), 0))]
        in_specs += [pl.BlockSpec((g.shape[0], tr, c), lambda i, tile=tile: (0, tile(i), 0)) for g in received]
        in_specs += [spec] * 3
        out_shapes += [jax.ShapeDtypeStruct((r, c), F32)] * 4
        out_specs += [spec] * 4
        plan.append((first, n, [g.shape[0] for g in received]))
        first += n

    def compute(in_refs, out_refs, _):
        i = pl.program_id(0)
        at_in = at_out = 0
        for start, n, counts in plan:
            mine = in_refs[at_in:at_in + 4 + len(counts)]
            outs = out_refs[at_out:at_out + 4]
            at_in += 4 + len(counts)
            at_out += 4

            @pl.when(jnp.logical_and(i >= start, i < start + n))
            def _(mine=mine, outs=outs, counts=counts):
                g = mine[0][...].astype(F32)
                for j, count in enumerate(counts):
                    for s in range(count):
                        g = g + mine[1 + j][s].astype(F32)
                delta, m_new, v_new = _adamw(mine[-3][...], g, mine[-2][...], mine[-1][...])
                for o, val in zip(outs, (g, delta, m_new, v_new)):
                    o[...] = val

    res = _call(name, compute, (first,), ins, in_specs, out_shapes, out_specs, [], ("arbitrary",), rider)
    outs, rode = res if rider else (res, [])
    return [outs[4 * j:4 * j + 4] for j in range(len(items))], rode


def _adamw_small(name, packed_all, late_all, dws_all, vectors, w_s):
    n_vec = len(vectors)

    def body(*refs):
        p_ref, l_ref, d_ref = refs[:3]
        param_refs = refs[3:3 + 3 * n_vec + 3]
        out_refs = refs[3 + 3 * n_vec + 3:-1]
        g_ref = refs[-1]
        g = p_ref[0]
        late = l_ref[0]
        for s in range(1, NDEV):
            g = g + p_ref[s]
            late = late + l_ref[s]
        g_ref[...] = g
        g_ref[pl.ds(0, R_LATE), :] += late

        def update(gp, wmv, outs):
            delta, m_new, v_new = _adamw(wmv[0][...], gp, wmv[1][...], wmv[2][...])
            for o, val in zip(outs, (gp, delta, m_new, v_new)):
                o[...] = val

        for j, (row, rows, *_) in enumerate(vectors):
            pieces = [g_ref[pl.ds(row + r, 1), :] for r in range(rows)]
            update(pieces[0] if rows == 1 else jnp.concatenate(pieces, axis=1), param_refs[3 * j:3 * j + 3], out_refs[4 * j:4 * j + 4])
        gw = d_ref[0]
        for s in range(1, NDEV):
            gw = gw + d_ref[s]
        update(gw, param_refs[3 * n_vec:], out_refs[4 * n_vec:4 * n_vec + 4])
        out_refs[-2][...] = g_ref[pl.ds(R_CW, KW), :]
        out_refs[-1][...] = g_ref[pl.ds(R_LOSS, 1), :]

    params = [a for _, _, w, m, v in vectors for a in (w, m, v)] + list(w_s)
    out_shapes = [jax.ShapeDtypeStruct(w.shape, F32) for _, _, w, _, _ in vectors for _ in range(4)]
    out_shapes += [jax.ShapeDtypeStruct(w_s[0].shape, F32)] * 4 + [jax.ShapeDtypeStruct((KW, D), F32), _vec()]
    res = pl.pallas_call(body, name=name, out_shape=out_shapes, scratch_shapes=[pltpu.VMEM((R_TOTAL, D), F32)],
                         compiler_params=_params(None))(packed_all, late_all, dws_all, *params)
    return [res[4 * j:4 * j + 4] for j in range(n_vec + 1)], res[-2], res[-1]


def _adamw_plain(name, g, w, m, v):
    def body(g_ref, w_ref, m_ref, v_ref, d_ref, mo_ref, vo_ref):
        delta, m_new, v_new = _adamw(w_ref[...], g_ref[...], m_ref[...], v_ref[...])
        d_ref[...] = delta
        mo_ref[...] = m_new
        vo_ref[...] = v_new

    return pl.pallas_call(body, name=name, out_shape=[jax.ShapeDtypeStruct(w.shape, F32)] * 3,
                          compiler_params=_params(None))(g, w, m, v)


def _adamw_ada(name, c_all_t, dmod, dmod_late, w, m, v):
    r, c = w.shape
    tr = 256

    def fn(i, refs, _):
        ca = _silu(refs[0][...])
        dm = refs[1][...] + refs[2][...]
        g = ca[:, 0:1] * dm[0:1, :]
        for b in range(1, NDEV):
            g = g + ca[:, b:b + 1] * dm[b:b + 1, :]
        delta, m_new, v_new = _adamw(refs[3][...], g, refs[4][...], refs[5][...])
        return [g, delta, m_new, v_new]

    spec = pl.BlockSpec((tr, c), lambda i: (i, 0))
    whole = pl.BlockSpec((NDEV, c), lambda i: (0, 0))
    return _rowcall(name, fn, [c_all_t, dmod, dmod_late, w, m, v],
                    [pl.BlockSpec((tr, NDEV), lambda i: (i, 0)), whole, whole, spec, spec, spec], 4,
                    [jax.ShapeDtypeStruct((r, c), F32)] * 4, [spec] * 4, r // tr)


def _ffn_fwd(tag, x, h, g, wg, wu, wd_shard, down_rider, next_norm=None):
    t = x.shape[0]
    tm = min(t, 512)
    (gate, up, act), (wd,) = _ffn_up(f"{tag}_up", h, wg, wu, rider=_gather_rider([wd_shard]))
    row = pl.BlockSpec((1, D), lambda i, j, k: (0, 0))

    def epilogue(f, xv, gv, *norm):
        x_out = xv + 0.5 * gv * f
        return (x_out, f, _rms_mod(x_out, *norm)) if norm else (x_out, f)

    res = _mm_nn(f"{tag}_down", act, wd.reshape(F, D), tm, D, 1024, extras=(x, g, *(next_norm or ())),
                 extra_specs=(pl.BlockSpec((tm, D), lambda i, j, k: (i, 0)), row, *([row] * 3 if next_norm else [])),
                 epilogue=epilogue, out_dtypes=(F32, BF, BF) if next_norm else (F32, BF), rider=down_rider)
    (x_out, f, *h_next), rode = res if down_rider else (res, None)
    return x_out, (h_next[0] if next_norm else None), (x, h, gate, up, act, f), wd, rode


def _ffn_bwd(tag, dx_out, df, saved, gain, sh, sc, wg, wu, wd, slots, dact_rider=None, dwd_rider=None, dwgu_rider=None,
             below=None):
    x, h, gate, up, act, f = saved
    t = x.shape[0]
    tm = min(t, 1024)

    def act_bwd(da, gv, uv):
        gv = gv.astype(F32)
        s = jax.nn.sigmoid(gv)
        return da * uv.astype(F32) * (s * (1.0 + gv * (1.0 - s))), da * (gv * s)

    blk = pl.BlockSpec((tm, F // NDEV), lambda i, j, k: (i, j))
    res = _mm_nt(f"{tag}_dact", df, wd.reshape(F, D), tm, F // NDEV, out_dtypes=(BF, BF),
                 extras=(gate, up), extra_specs=(blk, blk), epilogue=act_bwd, rider=dact_rider, col_chunks=2)
    (dgate, dup), rode_dact = res if dact_rider else (res, [])
    res = _mm_tn(f"{tag}_dwd", act, df, 512, D, rider=dwd_rider)
    dwd, rode_dwd = res if dwd_rider else (res, [])
    dwd = dwd.reshape(NDEV, F // NDEV, D)
    (dwg, dwu), (sib_d, *rode_dwgu) = _dw_gate_up(f"{tag}_dwgu", h, dgate, dup,
                                                  rider=[_pair_rider([dwd])] + ([dwgu_rider] if dwgu_rider else []))
    (sum_d,) = _pair_add(f"{tag}_dwd_add", [dwd], [sib_d], slots)
    dh, (sib_g, sib_u, got_d) = _mm_nt_blocked(f"{tag}_dh", [dgate, dup], [wg, wu], tm,
                                               rider=[_pair_rider([dwg, dwu]), _chip_rider([sum_d])])
    sum_g, sum_u = _pair_add(f"{tag}_dwgu_add", [dwg, dwu], [sib_g, sib_u], slots)
    normed = _norm_mod_bwd(f"{tag}_norm_bwd", x, gain, sc, sh, dh, dx_out, below=below)
    return normed, (sum_d, [got_d]), sum_g, sum_u, rode_dact, rode_dwd, rode_dwgu


def kernel(x, c, ada_w, ada_b, norm_ffn1, ffn1_w_gate, ffn1_w_up, ffn1_w_down, norm_mix, mix_w_in, mix_b_in, sgu_ln_g, sgu_ln_b, sgu_w_s, sgu_b_s, conv_w, conv_b, conv_ln_g, conv_ln_b, w_branch_a, w_branch_b, w_out, norm_ffn2, ffn2_w_gate, ffn2_w_up, ffn2_w_down, norm_final, loss_target, m_ada_w, m_ada_b, m_norm_ffn1, m_ffn1_w_gate, m_ffn1_w_up, m_ffn1_w_down, m_norm_mix, m_mix_w_in, m_mix_b_in, m_sgu_ln_g, m_sgu_ln_b, m_sgu_w_s, m_sgu_b_s, m_conv_w, m_conv_b, m_conv_ln_g, m_conv_ln_b, m_w_branch_a, m_w_branch_b, m_w_out, m_norm_ffn2, m_ffn2_w_gate, m_ffn2_w_up, m_ffn2_w_down, m_norm_final, v_ada_w, v_ada_b, v_norm_ffn1, v_ffn1_w_gate, v_ffn1_w_up, v_ffn1_w_down, v_norm_mix, v_mix_w_in, v_mix_b_in, v_sgu_ln_g, v_sgu_ln_b, v_sgu_w_s, v_sgu_b_s, v_conv_w, v_conv_b, v_conv_ln_g, v_conv_ln_b, v_w_branch_a, v_w_branch_b, v_w_out, v_norm_ffn2, v_ffn2_w_gate, v_ffn2_w_up, v_ffn2_w_down, v_norm_final):
    mx, my, mc = _position()
    me = 4 * mx + 2 * my + mc
    chip = 2 * mx + my
    slots = jnp.stack([2 * (chip ^ k) + mc for k in range(N_CHIPS)]).astype(jnp.int32)
    t = x.shape[1]
    tm = min(t, 1024)
    x0 = x.reshape(t, D)
    target = loss_target.reshape(t, D)
    given = dict(ffn1_w_gate=(ffn1_w_gate, m_ffn1_w_gate, v_ffn1_w_gate), ffn1_w_up=(ffn1_w_up, m_ffn1_w_up, v_ffn1_w_up),
                 ffn1_w_down=(ffn1_w_down, m_ffn1_w_down, v_ffn1_w_down), mix_w_in=(mix_w_in, m_mix_w_in, v_mix_w_in),
                 w_branch_a=(w_branch_a, m_w_branch_a, v_w_branch_a), w_branch_b=(w_branch_b, m_w_branch_b, v_w_branch_b),
                 w_out=(w_out, m_w_out, v_w_out), ffn2_w_gate=(ffn2_w_gate, m_ffn2_w_gate, v_ffn2_w_gate),
                 ffn2_w_up=(ffn2_w_up, m_ffn2_w_up, v_ffn2_w_up), ffn2_w_down=(ffn2_w_down, m_ffn2_w_down, v_ffn2_w_down))
    shard = {n: wmv[0][0].astype(BF) for n, wmv in given.items()}

    ada_cols = N_MOD * D // NDEV
    c_all, taps_all, mod_all, (wg1, wu1) = _prologue(
        "prologue", jnp.pad(c, ((0, SUBLANES - 1), (0, 0))), jnp.pad(conv_w[0], ((0, HALO - KW), (0, 0))), ada_w[0],
        lax.dynamic_slice(ada_b, (0, me * ada_cols), (1, ada_cols)), [shard["ffn1_w_gate"], shard["ffn1_w_up"]])
    conv_w_full = jnp.transpose(taps_all.reshape(NDEV, HALO, CHUNK), (1, 0, 2)).reshape(HALO, D)
    mod = lax.dynamic_index_in_dim(mod_all.reshape(NDEV, NDEV, ada_cols), me, axis=1, keepdims=False).reshape(N_MOD, 1, D)
    sh1, sc1, g1, sh2, sc2, g2, sh3, sc3, g3 = [mod[i] for i in range(N_MOD)]

    h1 = _norm_mod("ffn1_norm", x0, norm_ffn1, sc1, sh1)
    x1, h2, saved1, wd1, (w_in,) = _ffn_fwd("ffn1", x0, h1, g1, wg1, wu1, shard["ffn1_w_down"],
                                             _gather_rider([shard["mix_w_in"]]), next_norm=(norm_mix, sc2, sh2))
    proj, (wg2, wa3, wb3) = _mm_nn_blocked(
        "mix_in", h2, w_in, tm, rider=_gather_rider([shard["ffn2_w_gate"], shard["w_branch_a"], shard["w_branch_b"]]))
    bias_full = jnp.repeat(sgu_b_s[0].T, CHUNK, axis=1)
    (ua,) = _sgu("sgu", proj, mix_b_in, sgu_ln_g, sgu_ln_b, sgu_w_s[0], bias_full)
    (z0, z1, z3), (wu2, wo3) = _conv("conv", proj, mix_b_in, conv_w_full, conv_b, conv_ln_g, conv_ln_b,
                                     rider=_gather_rider([shard["ffn2_w_up"], shard["w_out"]]))
    wa, wb = wa3.reshape(D, D), wb3.reshape(D, D)
    ya = _mm_nn("branch_a", ua, wa, tm, 512, D)[0]
    yb = _mm_nn("branch_b", z3, wb, tm, 512, D)[0]
    (merged,) = _merge("merge", proj, mix_b_in, ya, yb)
    wo = wo3.reshape(D, D)

    def mix_epilogue(yv, xv, gv, gain, sc, sh):
        x_out = xv + gv * yv
        return x_out, yv, _rms_mod(x_out, gain, sc, sh)

    tmo = min(t, 512)
    row = pl.BlockSpec((1, D), lambda i, j, k: (0, 0))
    x2, y, h3 = _mm_nn("mix_out", merged, wo, tmo, D, D, extras=(x1, g2, norm_ffn2, sc3, sh3),
                       extra_specs=(pl.BlockSpec((tmo, D), lambda i, j, k: (i, 0)), row, row, row, row),
                       epilogue=mix_epilogue, out_dtypes=(F32, BF, BF))
    x3, _, saved3, wd2, _ = _ffn_fwd("ffn2", x2, h3, g3, wg2, wu2, shard["ffn2_w_down"], None)

    norm_final2 = norm_final.reshape(1, D)
    dx3, df3, d_norm_final, loss_row, dg3 = _loss_head("loss_head", x3, norm_final2, target, saved3[-1], g3, 0.5)
    (dx2, dy, d_norm_ffn2, dsc3, dsh3, dg2), down2, sum_g2, sum_u2, _, _, _ = _ffn_bwd(
        "ffn2", dx3, df3, saved3, norm_ffn2, sh3, sc3, wg2, wu2, wd2, slots, below=(y, g2, 1.0))
    dm = _mm_nt("mix_out_bwd", dy, wo, tm, 512)[0]
    dwo = _mm_tn("mix_dwo", merged, dy, 512, D).reshape(NDEV, D // NDEV, D)
    (dproj, dya, dyb, db_ga, db_gb), (got_g2_near,) = _merge_bwd("merge_bwd", proj, mix_b_in, ya, yb, dm,
                                                                 rider=_chip_rider([sum_g2], NEIGHBOURS))
    dua = _mm_nt("branch_a_bwd", dya, wa, tm, 512)[0]
    dwa = _mm_tn("branch_dwa", ua, dya, 512, D).reshape(NDEV, D // NDEV, D)
    dz3 = _mm_nt("branch_b_bwd", dyb, wb, tm, 512)[0]
    dwb = _mm_tn("branch_dwb", z3, dyb, 512, D).reshape(NDEV, D // NDEV, D)
    (dproj, db_u, db_v, d_sgu_g, d_sgu_b, d_ws, d_bs_t), (*sib_abo, got_g2_far) = _sgu_bwd(
        "sgu_bwd", proj, mix_b_in, sgu_ln_g, sgu_ln_b, sgu_w_s[0], bias_full, dua, dproj,
        rider=[_pair_rider([dwa, dwb, dwo]), _chip_rider([sum_g2], DIAGONAL)])
    sum_a, sum_b, sum_o = _pair_add("mix_dw_add", [dwa, dwb, dwo], sib_abo, slots)
    (dproj, db_cv, db_cg, d_cw, d_cb, d_cln_g, d_cln_b), (got_u2,) = _conv_bwd(
        "conv_bwd", proj, mix_b_in, conv_w_full, conv_ln_g, conv_ln_b, z0, z1, dz3, dproj, rider=_chip_rider([sum_u2]))
    dwin, (got_a, got_b, got_o) = _mm_tn_blocked("mix_dwin", h2, dproj, rider=_chip_rider([sum_a, sum_b, sum_o]))

    d_bs = jnp.transpose(d_bs_t[:, :HEADS])
    zero = jnp.zeros((1, D), F32)
    pack_rows = [zero, zero, zero, zero, zero, dg2, dsh3, dsc3, dg3,
                 zero, zero, d_norm_ffn2, d_norm_final,
                 db_u, db_v, db_cv, db_cg, db_ga, db_gb,
                 d_sgu_g, d_sgu_b, d_bs.reshape(1, D), d_cb, d_cln_g, d_cln_b,
                 d_cw[:KW], loss_row, jnp.zeros((R_TOTAL - R_LOSS - 1, D), F32)]
    packed = jnp.concatenate(pack_rows, axis=0)
    d_ws2 = d_ws.reshape(HEADS * CHUNK, CHUNK)
    dh2, (sib_in, packed_all, dws_all) = _mm_nt_blocked("mix_in_bwd", [dproj], [w_in], tm,
                                                        rider=[_pair_rider([dwin]), _gather_rider([packed, d_ws2])])
    (sum_in,) = _pair_add("mix_dwin_add", [dwin], [sib_in], slots)
    dx1, df1, d_norm_mix, dsc2, dsh2, dg1 = _norm_mod_bwd("mix_norm_bwd", x1, norm_mix, sc2, sh2, dh2, dx2,
                                                          below=(saved1[-1], g1, 0.5))
    (dx0, d_norm_ffn1, dsc1, dsh1), down1, sum_g1, sum_u1, (got_in_near,), _, (got_in_far,) = _ffn_bwd(
        "ffn1", dx1, df1, saved1, norm_ffn1, sh1, sc1, wg1, wu1, wd1, slots,
        dact_rider=_chip_rider([sum_in], NEIGHBOURS), dwgu_rider=_chip_rider([sum_in], DIAGONAL))
    packed_late = jnp.concatenate([dsh1, dsc1, dg1, dsh2, dsc2, jnp.zeros((4, D), F32), d_norm_ffn1, d_norm_mix,
                                   jnp.zeros((R_LATE - 11, D), F32)], axis=0)
    grads = dict(ffn2_w_gate=(sum_g2, [got_g2_near, got_g2_far]), ffn2_w_up=(sum_u2, [got_u2]), ffn2_w_down=down2,
                 mix_w_in=(sum_in, [got_in_near, got_in_far]), w_branch_a=(sum_a, [got_a]), w_branch_b=(sum_b, [got_b]),
                 w_out=(sum_o, [got_o]), ffn1_w_down=down1)
    done, (late_all, got_g1, got_u1) = _adamw_group(
        "adamw_most", [(cs, got, *[a[0] for a in given[n]]) for n, (cs, got) in grads.items()],
        rider=[_gather_rider([packed_late]), _chip_rider([sum_g1, sum_u1])])
    last, _ = _adamw_group("adamw_ffn1_in", [(sum_g1, [got_g1], *[a[0] for a in given["ffn1_w_gate"]]),
                                            (sum_u1, [got_u1], *[a[0] for a in given["ffn1_w_up"]])], rows=256)
    big_out = {n: [o.reshape(given[n][0].shape) for o in outs]
               for n, outs in zip([*grads, "ffn1_w_gate", "ffn1_w_up"], [*done, *last])}

    flat = lambda a: a.reshape(1, -1)
    vectors = [("ada_b", 0, 9, ada_b, m_ada_b, v_ada_b), ("norm_ffn1", 9, 1, norm_ffn1, m_norm_ffn1, v_norm_ffn1),
               ("norm_mix", 10, 1, norm_mix, m_norm_mix, v_norm_mix), ("norm_ffn2", 11, 1, norm_ffn2, m_norm_ffn2, v_norm_ffn2),
               ("norm_final", 12, 1, norm_final, m_norm_final, v_norm_final), ("mix_b_in", 13, 6, mix_b_in, m_mix_b_in, v_mix_b_in),
               ("sgu_ln_g", 19, 1, sgu_ln_g, m_sgu_ln_g, v_sgu_ln_g), ("sgu_ln_b", 20, 1, sgu_ln_b, m_sgu_ln_b, v_sgu_ln_b),
               ("sgu_b_s", 21, 1, sgu_b_s, m_sgu_b_s, v_sgu_b_s), ("conv_b", 22, 1, conv_b, m_conv_b, v_conv_b),
               ("conv_ln_g", 23, 1, conv_ln_g, m_conv_ln_g, v_conv_ln_g), ("conv_ln_b", 24, 1, conv_ln_b, m_conv_ln_b, v_conv_ln_b)]
    small_out, d_cw_all, loss_sum = _adamw_small(
        "adamw_small", packed_all, late_all, dws_all, [(row, rows, flat(wv), flat(mv), flat(vv)) for _, row, rows, wv, mv, vv in vectors],
        [a.reshape(HEADS * CHUNK, CHUNK) for a in (sgu_w_s, m_sgu_w_s, v_sgu_w_s)])
    small = {n: [o.reshape(wv.shape) for o in outs] for (n, _, _, wv, _, _), outs in zip(vectors, small_out)}
    small["sgu_w_s"] = [o.reshape(sgu_w_s.shape) for o in small_out[-1]]
    g_cw = lax.dynamic_slice(d_cw_all, (0, me * CHUNK), (KW, CHUNK))
    small["conv_w"] = [o.reshape(conv_w.shape) for o in (g_cw, *_adamw_plain("adamw_conv_w", g_cw, conv_w[0], m_conv_w[0], v_conv_w[0]))]
    loss = loss_sum[0, 0]

    dmod_cols = [lax.dynamic_slice(a[:, :N_MOD, :].reshape(NDEV, N_MOD * D), (0, me * ada_cols), (NDEV, ada_cols))
                 for a in (packed_all, late_all)]
    ada_out = [o.reshape(ada_w.shape) for o in _adamw_ada("adamw_ada_w", jnp.transpose(c_all), *dmod_cols, ada_w[0], m_ada_w[0], v_ada_w[0])]

    order = ["ada_w", "ada_b", "norm_ffn1", "ffn1_w_gate", "ffn1_w_up", "ffn1_w_down", "norm_mix", "mix_w_in", "mix_b_in",
             "sgu_ln_g", "sgu_ln_b", "sgu_w_s", "sgu_b_s", "conv_w", "conv_b", "conv_ln_g", "conv_ln_b", "w_branch_a",
             "w_branch_b", "w_out", "norm_ffn2", "ffn2_w_gate", "ffn2_w_up", "ffn2_w_down", "norm_final"]

    def leaf(n, kind):
        if n == "ada_w":
            return ada_out[kind]
        if n in big_out:
            return big_out[n][kind]
        return small[n][kind]

    return (loss, dx0.reshape(x.shape), *[leaf(n, kind) for kind in range(4) for n in order])
```

```python
import jax
import jax.numpy as jnp
from jax import lax
from jax.experimental import pallas as pl
from jax.experimental.pallas import tpu as pltpu

D = 1024
F = 4 * D
D_IN = 6 * D
HEADS = 8
CHUNK = 128
KW = 31
HALO = 32
N_MOD = 9
NDEV = 8
N_CHIPS = 4
EPS = 1e-6
LR, B1, B2, ADAM_EPS, WD, STEP = 0.001, 0.9, 0.999, 1e-08, 0.01, 10
BC1 = 1.0 - B1 ** STEP
BC2 = 1.0 - B2 ** STEP
VMEM_LIMIT = 56 * 1024 * 1024
MESH = pl.DeviceIdType.MESH
HBM = pl.BlockSpec(memory_space=pltpu.HBM)
VMEM = pl.BlockSpec(memory_space=pltpu.VMEM)
BF = jnp.bfloat16
F32 = jnp.float32

NN = (((1,), (0,)), ((), ()))
NT = (((1,), (1,)), ((), ()))
TN = (((0,), (0,)), ((), ()))

R_CW, R_LOSS, R_TOTAL = 25, 56, 64
R_LATE = 16


def _params(sem):
    return pltpu.CompilerParams(dimension_semantics=sem, vmem_limit_bytes=VMEM_LIMIT)


def _position():
    return lax.axis_index("x"), lax.axis_index("y"), lax.axis_index("c")


def _flip(pos, k):
    x, y, c = pos
    return (x ^ (k >> 2 & 1), y ^ (k >> 1 & 1), c ^ (k & 1))


def _index(pos):
    return 4 * pos[0] + 2 * pos[1] + pos[2]


def _gather_rows(x_ref, out_ref, send_sems, recv_sems, local_sem):
    m_per = x_ref.shape[0]
    x, y, c = _position()
    me, sibling = (x, y, c), (x, y, 1 - c)
    chips = [(1 - x, y), (x, 1 - y), (1 - x, 1 - y)]

    def rows(pos):
        return out_ref.at[pl.ds(_index(pos) * m_per, m_per), :]

    def copy(k, block, to, src=None):
        return pltpu.make_async_remote_copy(
            src_ref=rows(block) if src is None else src, dst_ref=rows(block),
            send_sem=send_sems.at[k], recv_sem=recv_sems.at[k], device_id=to, device_id_type=MESH)

    mine = pltpu.make_async_copy(x_ref, rows(me), local_sem)
    mine.start()
    first = [copy(0, me, sibling, src=x_ref)]
    first += [copy(1 + j, me, (*chip, c), src=x_ref) for j, chip in enumerate(chips)]
    for cp in first:
        cp.start()
    passed = [copy(4 + j, (*chip, c), sibling) for j, chip in enumerate(chips)]
    for j, chip in enumerate(chips):
        copy(1 + j, (*chip, c), me).wait_recv()
        passed[j].start()
    copy(0, sibling, me).wait_recv()
    for j, chip in enumerate(chips):
        copy(4 + j, (*chip, 1 - c), me).wait_recv()
    for cp in first + passed:
        cp.wait_send()
    mine.wait()


def _prologue(name, c_rows, taps, ada_w, ada_b, shards):
    rider = _gather_rider(shards)
    n = len(shards)
    nc = ada_w.shape[1]

    def body(*refs):
        c_ref, taps_ref, w_ref, b_ref = refs[:4]
        shard_refs = refs[4:4 + n]
        c_all_ref, taps_all_ref, mod_all_ref = refs[4 + n:7 + n]
        gathered_refs = refs[7 + n:7 + 2 * n]
        c_buf, mod_part, sems = refs[7 + 2 * n], refs[8 + 2 * n], refs[9 + 2 * n:]
        rider.start(shard_refs, gathered_refs, sems[9:])
        _gather_rows(c_ref, c_buf, *sems[0:3])
        c_all = jnp.concatenate([c_buf[pl.ds(d * SUBLANES, 1), :] for d in range(NDEV)], axis=0)
        c_all_ref[...] = c_all
        mod_part[...] = jnp.dot(_silu(c_all), w_ref[...], preferred_element_type=F32) + b_ref[...]
        _gather_rows(taps_ref, taps_all_ref, *sems[3:6])
        _gather_rows(mod_part, mod_all_ref, *sems[6:9])
        rider.mid(shard_refs, gathered_refs, sems[9:])
        rider.relay(shard_refs, gathered_refs, sems[9:])
        rider.finish(shard_refs, gathered_refs, sems[9:])

    small_sems = [pltpu.SemaphoreType.DMA((7,)), pltpu.SemaphoreType.DMA((7,)), pltpu.SemaphoreType.DMA] * 3
    res = pl.pallas_call(
        body, name=name,
        out_shape=[jax.ShapeDtypeStruct((NDEV, D), F32), jax.ShapeDtypeStruct((NDEV * taps.shape[0], taps.shape[1]), F32),
                   jax.ShapeDtypeStruct((NDEV * NDEV, nc), F32)] + rider.out_shapes,
        in_specs=[VMEM] * 4 + [HBM] * n, out_specs=[VMEM] * 3 + [HBM] * n,
        scratch_shapes=[pltpu.VMEM((NDEV * SUBLANES, D), F32), pltpu.VMEM((NDEV, nc), F32)] + small_sems + rider.sems,
        compiler_params=_params(None),
    )(c_rows, taps, ada_w, ada_b, *shards)
    return res[0], res[1], res[2], res[3:]


class _Rider:
    def __init__(self, ins, out_shapes, sems, start, finish, mid=None, relay=None):
        self.ins, self.out_shapes, self.sems = list(ins), list(out_shapes), list(sems)
        self.start, self.finish, self.mid, self.relay = start, finish, mid, relay


def _gather_rider(shards):
    n = len(shards)

    def setup(ins, outs, sems):
        send_sems, recv_sems, local_sems = sems
        x, y, c = _position()
        places = dict(me=(x, y, c), sibling=(x, y, 1 - c), xn=(1 - x, y, c), yn=(x, 1 - y, c), diagonal=(1 - x, 1 - y, c),
                      passed_on=(x ^ c, y ^ (1 - c), c), passed_to=(x ^ (1 - c), y ^ c, c))

        def copy(a, k, block, to, own=False):
            slot = outs[a].at[_index(block)]
            return pltpu.make_async_remote_copy(
                src_ref=ins[a] if own else slot, dst_ref=slot,
                send_sem=send_sems.at[k, a], recv_sem=recv_sems.at[k, a], device_id=to, device_id_type=MESH)

        def local(a):
            return pltpu.make_async_copy(ins[a], outs[a].at[_index(places["me"])], local_sems.at[a])

        return places, copy, local

    def start(ins, outs, sems):
        p, copy, local = setup(ins, outs, sems)
        for a in range(n):
            local(a).start()
            for k, to in enumerate(("sibling", "xn", "yn")):
                copy(a, k, p["me"], p[to], own=True).start()

    def mid(ins, outs, sems):
        p, copy, _ = setup(ins, outs, sems)
        for a in range(n):
            copy(a, 1, p["xn"], p["me"]).wait_recv()
            copy(a, 2, p["yn"], p["me"]).wait_recv()
            copy(a, 3, p["passed_on"], p["passed_to"]).start()
            copy(a, 4, p["xn"], p["sibling"]).start()
            copy(a, 5, p["yn"], p["sibling"]).start()

    def relay(ins, outs, sems):
        p, copy, _ = setup(ins, outs, sems)
        for a in range(n):
            copy(a, 3, p["diagonal"], p["me"]).wait_recv()
            copy(a, 6, p["diagonal"], p["sibling"]).start()

    def finish(ins, outs, sems):
        p, copy, local = setup(ins, outs, sems)
        x, y, c = p["me"]
        for a in range(n):
            for k, block in ((0, (x, y, 1 - c)), (4, (1 - x, y, 1 - c)), (5, (x, 1 - y, 1 - c)), (6, (1 - x, 1 - y, 1 - c))):
                copy(a, k, block, p["me"]).wait_recv()
            for k, to in enumerate(("sibling", "xn", "yn")):
                copy(a, k, p["me"], p[to], own=True).wait_send()
            copy(a, 3, p["passed_on"], p["passed_to"]).wait_send()
            for k, block in ((4, "xn"), (5, "yn"), (6, "diagonal")):
                copy(a, k, p[block], p["sibling"]).wait_send()
            local(a).wait()

    return _Rider(shards, [jax.ShapeDtypeStruct((NDEV, *s.shape), s.dtype) for s in shards],
                  [pltpu.SemaphoreType.DMA((7, n)), pltpu.SemaphoreType.DMA((7, n)), pltpu.SemaphoreType.DMA((n,))],
                  start, finish, mid, relay)


def _pair_rider(parts):
    n = len(parts)

    def copies(ins, outs, sems):
        send_sems, recv_sems = sems
        x, y, c = _position()
        q = 2 * x + y
        return [pltpu.make_async_remote_copy(
            src_ref=ins[a].at[2 * (q ^ k) + (1 - c)], dst_ref=outs[a].at[k],
            send_sem=send_sems.at[k, a], recv_sem=recv_sems.at[k, a], device_id=(x, y, 1 - c), device_id_type=MESH)
            for a in range(n) for k in range(N_CHIPS)]

    def start(ins, outs, sems):
        for cp in copies(ins, outs, sems):
            cp.start()

    def finish(ins, outs, sems):
        for cp in copies(ins, outs, sems):
            cp.wait()

    return _Rider(parts, [jax.ShapeDtypeStruct((N_CHIPS, *p.shape[1:]), p.dtype) for p in parts],
                  [pltpu.SemaphoreType.DMA((N_CHIPS, n)), pltpu.SemaphoreType.DMA((N_CHIPS, n))], start, finish)


NEIGHBOURS = (1, 2)
DIAGONAL = (3,)
OTHER_CHIPS = NEIGHBOURS + DIAGONAL


def _chip_rider(sums, ks=OTHER_CHIPS):
    n = len(sums)

    def copies(ins, outs, sems):
        send_sems, recv_sems = sems
        me = _position()
        return [pltpu.make_async_remote_copy(
            src_ref=ins[a].at[k], dst_ref=outs[a].at[j],
            send_sem=send_sems.at[j, a], recv_sem=recv_sems.at[j, a], device_id=_flip(me, 2 * k), device_id_type=MESH)
            for a in range(n) for j, k in enumerate(ks)]

    def start(ins, outs, sems):
        for cp in copies(ins, outs, sems):
            cp.start()

    def finish(ins, outs, sems):
        for cp in copies(ins, outs, sems):
            cp.wait()

    return _Rider(sums, [jax.ShapeDtypeStruct((len(ks), *s.shape[1:]), s.dtype) for s in sums],
                  [pltpu.SemaphoreType.DMA((len(ks), n)), pltpu.SemaphoreType.DMA((len(ks), n))], start, finish)


def _grid_edge(grid, last):
    cond = None
    for d, n in enumerate(grid):
        here = pl.program_id(d) == (n - 1 if last else 0)
        cond = here if cond is None else jnp.logical_and(cond, here)
    return cond


def _call(name, compute, grid, ins, in_specs, out_shapes, out_specs, scratch_shapes, semantics, rider=None, aliases=None):
    riders = [rider] if isinstance(rider, _Rider) else list(rider or [])
    n_in, n_out, n_scr = len(ins), len(out_shapes), len(scratch_shapes)
    n_rin, n_rout, n_rsem = [sum(len(part(r)) for r in riders) for part in (lambda r: r.ins, lambda r: r.out_shapes, lambda r: r.sems)]
    cuts = [0, n_in, n_in + n_rin, n_in + n_rin + n_out, n_in + n_rin + n_out + n_rout, n_in + n_rin + n_out + n_rout + n_scr]

    def body(*refs):
        in_refs, rin_refs, out_refs, rout_refs, scr_refs = [refs[a:b] for a, b in zip(cuts[:-1], cuts[1:])]
        rsem_refs = refs[cuts[-1]:]
        mine, at = [], [0, 0, 0]
        for r in riders:
            mine.append((r, rin_refs[at[0]:at[0] + len(r.ins)], rout_refs[at[1]:at[1] + len(r.out_shapes)],
                         rsem_refs[at[2]:at[2] + len(r.sems)]))
            at = [at[0] + len(r.ins), at[1] + len(r.out_shapes), at[2] + len(r.sems)]
        if riders:
            @pl.when(_grid_edge(grid, last=False))
            def _():
                for r, a, b, c in mine:
                    r.start(a, b, c)

        if any(r.mid for r in riders):
            step, steps = 0, 1
            for d, size in enumerate(grid):
                step, steps = step * size + pl.program_id(d), steps * size

            @pl.when(step == steps * 5 // 8)
            def _():
                for r, a, b, c in mine:
                    if r.mid:
                        r.mid(a, b, c)

        if any(r.relay for r in riders):
            @pl.when(_grid_edge(grid, last=True))
            def _():
                for r, a, b, c in mine:
                    if r.relay:
                        r.relay(a, b, c)

        compute(in_refs, out_refs, scr_refs)
        if riders:
            @pl.when(_grid_edge(grid, last=True))
            def _():
                for r, a, b, c in mine:
                    r.finish(a, b, c)

    res = pl.pallas_call(
        body, name=name, grid=grid,
        out_shape=list(out_shapes) + [s for r in riders for s in r.out_shapes],
        in_specs=list(in_specs) + [HBM] * n_rin, out_specs=list(out_specs) + [HBM] * n_rout,
        scratch_shapes=list(scratch_shapes) + [s for r in riders for s in r.sems],
        input_output_aliases=aliases or {}, compiler_params=_params(semantics),
    )(*ins, *[a for r in riders for a in r.ins])
    return (res[:n_out], res[n_out:]) if riders else res


def _pair_add(name, parts, from_sibling, slots):
    n = len(parts)

    def body(s_ref, *refs):
        for a in range(n):
            refs[2 * n + a][...] = (refs[a][...].astype(F32) + refs[n + a][...].astype(F32)).astype(refs[2 * n + a].dtype)

    def slab(p, picked):
        _, r, c = p.shape
        return pl.BlockSpec((None, r, c), (lambda k, s: (s[k], 0, 0)) if picked else (lambda k, s: (k, 0, 0)))

    return pl.pallas_call(
        body, name=name,
        grid_spec=pltpu.PrefetchScalarGridSpec(
            num_scalar_prefetch=1, grid=(N_CHIPS,),
            in_specs=[slab(p, True) for p in parts] + [slab(p, False) for p in parts],
            out_specs=[slab(p, False) for p in parts]),
        out_shape=[jax.ShapeDtypeStruct((N_CHIPS, *p.shape[1:]), p.dtype) for p in parts],
        compiler_params=_params(("arbitrary",)),
    )(slots, *parts, *from_sibling)


def _mm(name, pairs, dims, grid, nk, out_shapes, out_specs, extras=(), extra_specs=(), epilogue=None, acc_shape=None, rider=None):
    n_pairs = len(pairs)

    def compute(ins, outs, scratch):
        def partial_sum():
            total = None
            for p in range(n_pairs):
                d = lax.dot_general(ins[2 * p][...], ins[2 * p + 1][...], dims, preferred_element_type=F32)
                total = d if total is None else total + d
            return total

        def finish(r):
            ex = [e[...] for e in ins[2 * n_pairs:]]
            res = epilogue(r, *ex) if epilogue is not None else (r,)
            for o, v in zip(outs, res):
                o[...] = v.astype(o.dtype)

        if nk == 1:
            finish(partial_sum())
        else:
            acc = scratch[0]
            k = pl.program_id(2)

            @pl.when(k == 0)
            def _():
                acc[...] = partial_sum()

            @pl.when(k > 0)
            def _():
                acc[...] += partial_sum()

            @pl.when(k == nk - 1)
            def _():
                finish(acc[...])

    operands, specs = [], []
    for a, a_spec, b, b_spec in pairs:
        operands += [a, b]
        specs += [a_spec, b_spec]
    return _call(name, compute, grid, operands + list(extras), specs + list(extra_specs), out_shapes, out_specs,
                 [pltpu.VMEM(acc_shape, F32)] if nk > 1 else [], ("parallel", "parallel", "arbitrary"), rider)


def _single(res, rider):
    return (res[0][0], res[1]) if rider else res[0]


def _silu(x):
    return x * jax.nn.sigmoid(x)


def _ffn_up(name, h, wg, wu, rider=None):
    t = h.shape[0]
    tm = min(t, 1024)
    nb = F // NDEV

    def compute(ins, outs, _):
        hv = ins[0][...]
        g = jnp.dot(hv, ins[1][...], preferred_element_type=F32)
        u = jnp.dot(hv, ins[2][...], preferred_element_type=F32)
        outs[0][...] = g.astype(BF)
        outs[1][...] = u.astype(BF)
        outs[2][...] = (_silu(g) * u).astype(BF)

    w_spec = pl.BlockSpec((None, D, nb), lambda i, j: (j, 0, 0))
    o_spec = pl.BlockSpec((tm, nb), lambda i, j: (i, j))
    return _call(name, compute, (t // tm, NDEV), [h, wg, wu], [pl.BlockSpec((tm, D), lambda i, j: (i, 0)), w_spec, w_spec],
                 [jax.ShapeDtypeStruct((t, F), BF)] * 3, [o_spec] * 3, [], ("parallel", "arbitrary"), rider)


def _mm_nn(name, a, b, tm, tn, tk, extras=(), extra_specs=(), epilogue=None, out_dtypes=(F32,), rider=None):
    m, kk = a.shape
    n = b.shape[1]
    nk = kk // tk
    return _mm(
        name, [(a, pl.BlockSpec((tm, tk), lambda i, j, k: (i, k)), b, pl.BlockSpec((tk, tn), lambda i, j, k: (k, j)))], NN,
        (m // tm, n // tn, nk), nk,
        [jax.ShapeDtypeStruct((m, n), dt) for dt in out_dtypes],
        [pl.BlockSpec((tm, tn), lambda i, j, k: (i, j))] * len(out_dtypes),
        extras, extra_specs, epilogue, (tm, tn), rider)


def _mm_nn_blocked(name, a, b3, tm, rider=None):
    m = a.shape[0]
    nb = b3.shape[2]
    return _single(_mm(
        name, [(a, pl.BlockSpec((tm, D), lambda i, j, k: (i, 0)), b3, pl.BlockSpec((None, D, nb), lambda i, j, k: (j, 0, 0)))], NN,
        (m // tm, NDEV, 1), 1,
        [jax.ShapeDtypeStruct((m, NDEV * nb), F32)], [pl.BlockSpec((tm, nb), lambda i, j, k: (i, j))], rider=rider), rider)


def _mm_nt(name, a, b, tm, tn, out_dtypes=(F32,), extras=(), extra_specs=(), epilogue=None, rider=None):
    m, kk = a.shape
    n = b.shape[0]
    return _mm(
        name, [(a, pl.BlockSpec((tm, kk), lambda i, j, k: (i, 0)), b, pl.BlockSpec((tn, kk), lambda i, j, k: (j, 0)))], NT,
        (m // tm, n // tn, 1), 1,
        [jax.ShapeDtypeStruct((m, n), dt) for dt in out_dtypes],
        [pl.BlockSpec((tm, tn), lambda i, j, k: (i, j))] * len(out_dtypes),
        extras, extra_specs, epilogue, rider=rider)


def _mm_nt_blocked(name, a_list, b3_list, tm, rider=None):
    m = a_list[0].shape[0]
    nb = b3_list[0].shape[2]
    pairs = [(a, pl.BlockSpec((tm, nb), lambda i, j, k: (i, k)), b3, pl.BlockSpec((None, D, nb), lambda i, j, k: (k, 0, 0)))
             for a, b3 in zip(a_list, b3_list)]
    return _single(_mm(name, pairs, NT, (m // tm, 1, NDEV), NDEV,
                       [jax.ShapeDtypeStruct((m, D), F32)], [pl.BlockSpec((tm, D), lambda i, j, k: (i, 0))],
                       acc_shape=(tm, D), rider=rider), rider)


def _mm_tn(name, a, b, tm, tn, rider=None):
    t, m = a.shape
    n = b.shape[1]
    return _single(_mm(
        name, [(a, pl.BlockSpec((t, tm), lambda i, j, k: (0, i)), b, pl.BlockSpec((t, tn), lambda i, j, k: (0, j)))], TN,
        (m // tm, n // tn, 1), 1,
        [jax.ShapeDtypeStruct((m, n), BF)], [pl.BlockSpec((tm, tn), lambda i, j, k: (i, j))], rider=rider), rider)


def _mm_tn_blocked(name, a, b, rider=None):
    t = a.shape[0]
    nb = b.shape[1] // NDEV
    return _single(_mm(
        name, [(a, pl.BlockSpec((t, D), lambda i, j, k: (0, 0)), b, pl.BlockSpec((t, nb), lambda i, j, k: (0, j)))], TN,
        (1, NDEV, 1), 1,
        [jax.ShapeDtypeStruct((NDEV, D, nb), BF)], [pl.BlockSpec((None, D, nb), lambda i, j, k: (j, 0, 0))], rider=rider), rider)


def _dw_gate_up(name, h, dgate, dup, rider=None):
    t = h.shape[0]
    nb = F // NDEV

    def compute(ins, outs, _):
        hv = ins[0][...]
        outs[0][...] = lax.dot_general(hv, ins[1][...], TN, preferred_element_type=F32).astype(BF)
        outs[1][...] = lax.dot_general(hv, ins[2][...], TN, preferred_element_type=F32).astype(BF)

    d_spec = pl.BlockSpec((t, nb), lambda j: (0, j))
    o_spec = pl.BlockSpec((None, D, nb), lambda j: (j, 0, 0))
    return _call(name, compute, (NDEV,), [h, dgate, dup], [pl.BlockSpec((t, D), lambda j: (0, 0)), d_spec, d_spec],
                 [jax.ShapeDtypeStruct((NDEV, D, nb), BF)] * 2, [o_spec] * 2, [], ("arbitrary",), rider)


def _rowcall(name, fn, ins, in_specs, n_row_out, out_shapes, out_specs, grid, scratch_shapes=(), rider=None, aliases=None):
    def accumulate(o, v, i):
        @pl.when(i == 0)
        def _():
            o[...] = v.astype(o.dtype)

        @pl.when(i > 0)
        def _():
            o[...] += v.astype(o.dtype)

    def compute(in_refs, out_refs, scr):
        i = pl.program_id(0)
        vals = fn(i, in_refs, scr)
        for idx, (o, v) in enumerate(zip(out_refs, vals)):
            if idx < n_row_out:
                o[...] = v.astype(o.dtype)
            else:
                accumulate(o, v, i)

    return _call(name, compute, (grid,), ins, in_specs, out_shapes, out_specs, list(scratch_shapes), ("arbitrary",), rider, aliases)


def _rows(tr, w=D, cb=0):
    return pl.BlockSpec((tr, w), lambda i: (i, cb))


def _whole(shape):
    nd = len(shape)
    return pl.BlockSpec(shape, lambda i: (0,) * nd)


def _vec(n=1):
    return jax.ShapeDtypeStruct((n, D), F32)


def _rms_mod(x, gain, sc, sh):
    y = x * lax.rsqrt(jnp.mean(x * x, axis=-1, keepdims=True) + EPS)
    return (y * gain) * (1.0 + sc) + sh


def _layer_norm(x, g, b):
    mu = jnp.mean(x, axis=-1, keepdims=True)
    var = jnp.mean(jnp.square(x - mu), axis=-1, keepdims=True)
    return (x - mu) * lax.rsqrt(var + EPS) * g + b


def _norm_mod(name, x, gain, sc, sh):
    t = x.shape[0]
    tr = min(t, 256)

    def fn(i, r, _):
        return [_rms_mod(r[0][...], r[1][...], r[2][...], r[3][...])]

    return _rowcall(name, fn, [x, gain, sc, sh], [_rows(tr)] + [_whole((1, D))] * 3, 1,
                    [jax.ShapeDtypeStruct((t, D), BF)], [_rows(tr)], t // tr)[0]


def _gate_grads(dx, f, g, scale):
    return scale * g * dx, jnp.sum(scale * dx * f.astype(F32), axis=0, keepdims=True)


def _norm_mod_bwd(name, x, gain, sc, sh, dh, dres, below=None):
    t = x.shape[0]
    tr = min(t, 256)

    def fn(i, r, _):
        _, vjp = jax.vjp(_rms_mod, r[0][...], r[1][...], r[2][...], r[3][...])
        dx, dgain, dsc, dsh = vjp(r[4][...])
        dx = dx + r[5][...]
        if below is None:
            return [dx, dgain, dsc, dsh]
        df, dg = _gate_grads(dx, r[6][...], r[7][...], below[2])
        return [dx, df, dgain, dsc, dsh, dg]

    ins, specs = [x, gain, sc, sh, dh, dres], [_rows(tr)] + [_whole((1, D))] * 3 + [_rows(tr)] * 2
    outs, out_specs = [jax.ShapeDtypeStruct((t, D), F32)], [_rows(tr)]
    if below is not None:
        ins, specs = ins + [below[0], below[1]], specs + [_rows(tr), _whole((1, D))]
        outs, out_specs = outs + [jax.ShapeDtypeStruct((t, D), BF)], out_specs + [_rows(tr)]
    n_vec = 3 if below is None else 4
    return _rowcall(name, fn, ins, specs, len(outs), outs + [_vec()] * n_vec, out_specs + [_whole((1, D))] * n_vec, t // tr)


def _sgu_pre(up, vp, bu, bv, ln_g, ln_b):
    return jax.nn.gelu(up + bu), _layer_norm(jax.nn.gelu(vp + bv), ln_g, ln_b)


def _causal(w_ref, h):
    rows = lax.broadcasted_iota(jnp.int32, (CHUNK, CHUNK), 0)
    cols = lax.broadcasted_iota(jnp.int32, (CHUNK, CHUNK), 1)
    return jnp.where(cols <= rows, w_ref[h], 0.0)


def _sgu(name, proj, b_in, ln_g, ln_b, w_s, bias_full, rider=None):
    t = proj.shape[0]

    def fn(i, r, _):
        u, v = _sgu_pre(r[0][...], r[1][...], r[2][...], r[3][...], r[4][...], r[5][...])
        vb = v.astype(BF)
        mixed = [jnp.dot(_causal(r[6], h).astype(BF), vb[:, h * CHUNK:(h + 1) * CHUNK], preferred_element_type=F32)
                 for h in range(HEADS)]
        return [u * (jnp.concatenate(mixed, axis=1) + r[7][...])]

    return _rowcall(
        name, fn, [proj, proj, b_in, b_in, ln_g, ln_b, w_s, bias_full],
        [_rows(CHUNK, D, 0), _rows(CHUNK, D, 1), pl.BlockSpec((1, D), lambda i: (0, 0)), pl.BlockSpec((1, D), lambda i: (0, 1)),
         _whole((1, D)), _whole((1, D)), _whole((HEADS, CHUNK, CHUNK)), _whole((CHUNK, D))],
        1, [jax.ShapeDtypeStruct((t, D), BF)], [_rows(CHUNK)], t // CHUNK, rider=rider)


def _sgu_bwd(name, proj, b_in, ln_g, ln_b, w_s, bias_full, dout, dproj, rider=None):
    t = proj.shape[0]

    def fn(i, r, _):
        (u, v), vjp = jax.vjp(_sgu_pre, r[0][...], r[1][...], r[2][...], r[3][...], r[4][...], r[5][...])
        vb = v.astype(BF)
        d = r[8][...]
        masks = [_causal(r[6], h).astype(BF) for h in range(HEADS)]
        cols = [slice(h * CHUNK, (h + 1) * CHUNK) for h in range(HEADS)]
        mixed = jnp.concatenate([jnp.dot(masks[h], vb[:, cols[h]], preferred_element_type=F32) for h in range(HEADS)], axis=1)
        du = d * (mixed + r[7][...])
        dmix = d * u
        dmb = dmix.astype(BF)
        dv = jnp.concatenate([lax.dot_general(masks[h], dmb[:, cols[h]], TN, preferred_element_type=F32) for h in range(HEADS)], axis=1)
        rows = lax.broadcasted_iota(jnp.int32, (CHUNK, CHUNK), 0)
        lanes = lax.broadcasted_iota(jnp.int32, (CHUNK, CHUNK), 1)
        dws = jnp.stack([jnp.where(lanes <= rows, lax.dot_general(dmb[:, cols[h]], vb[:, cols[h]], NT, preferred_element_type=F32), 0.0)
                         for h in range(HEADS)])
        dbs = jnp.zeros((CHUNK, CHUNK), F32)
        for h in range(HEADS):
            dbs = dbs + jnp.where(lanes == h, jnp.sum(dmix[:, cols[h]], axis=1, keepdims=True), 0.0)
        dup, dvp, dbu, dbv, dg, db = vjp((du, dv))
        return [jnp.concatenate([dup, dvp], axis=1), dbu, dbv, dg, db, dws, dbs]

    return _rowcall(
        name, fn, [proj, proj, b_in, b_in, ln_g, ln_b, w_s, bias_full, dout, dproj],
        [_rows(CHUNK, D, 0), _rows(CHUNK, D, 1), pl.BlockSpec((1, D), lambda i: (0, 0)), pl.BlockSpec((1, D), lambda i: (0, 1)),
         _whole((1, D)), _whole((1, D)), _whole((HEADS, CHUNK, CHUNK)), _whole((CHUNK, D)), _rows(CHUNK),
         pl.BlockSpec(memory_space=pl.ANY)],
        1, [jax.ShapeDtypeStruct(dproj.shape, dproj.dtype)] + [_vec()] * 4
        + [jax.ShapeDtypeStruct((HEADS, CHUNK, CHUNK), F32), jax.ShapeDtypeStruct((CHUNK, CHUNK), F32)],
        [pl.BlockSpec((CHUNK, 2 * D), lambda i: (i, 0))] + [_whole((1, D))] * 4 + [_whole((HEADS, CHUNK, CHUNK)), _whole((CHUNK, CHUNK))],
        t // CHUNK, rider=rider, aliases={9: 0})


def _halo_before(tr, cb):
    return pl.BlockSpec((HALO, D), lambda i: (jnp.maximum(i * (tr // HALO) - 1, 0), cb))


def _halo_after(tr, cb, n_tiles):
    return pl.BlockSpec((HALO, D), lambda i: (jnp.minimum((i + 1) * (tr // HALO), n_tiles * (tr // HALO) - 1), cb))


def _ln_silu(z, g, b):
    return _silu(_layer_norm(z, g, b))


SUBLANES = 8
LANES = 128
CONV_STRIP = 16
DW_STRIP = 32


def _shifted_copies(buf, copies, rows):
    for b in range(1, SUBLANES):
        copies[b - 1, pl.ds(0, rows), :] = buf[pl.ds(b, rows), :]


def _shifted(buf, copies, offset, start, rows, lanes=slice(None)):
    at = pl.ds(pl.multiple_of(start + SUBLANES * (offset // SUBLANES), SUBLANES), rows)
    return buf[at, lanes] if offset % SUBLANES == 0 else copies[offset % SUBLANES - 1, at, lanes]


def _accumulate(o, v, i):
    @pl.when(i == 0)
    def _():
        o[...] = v.astype(o.dtype)

    @pl.when(i > 0)
    def _():
        o[...] += v.astype(o.dtype)


def _conv(name, proj, b_in, conv_w, conv_b, ln_g, ln_b, rider=None):
    t = proj.shape[0]
    tr = min(t, 256)

    def compute(r, outs, scr):
        zbuf, zs = scr
        i = pl.program_id(0)
        bv, bg = r[4][...], r[5][...]
        z0 = (r[0][...] + bv) * jax.nn.sigmoid(r[1][...] + bg)
        before = (r[2][...] + bv) * jax.nn.sigmoid(r[3][...] + bg)
        zbuf[pl.ds(0, HALO), :] = jnp.where(i > 0, before, 0.0)
        zbuf[pl.ds(HALO, tr), :] = z0
        outs[0][...] = z0
        _shifted_copies(zbuf, zs, tr + HALO - SUBLANES)

        ln_g_row, ln_b_row = r[8][...], r[9][...]

        def strip(s, carry):
            r0 = s * CONV_STRIP
            at = pl.ds(pl.multiple_of(r0, CONV_STRIP), CONV_STRIP)
            acc = jnp.zeros((CONV_STRIP, D), F32) + r[7][...]
            for k in range(KW):
                acc = acc + r[6][k:k + 1, :] * _shifted(zbuf, zs, HALO - (KW - 1) + k, r0, CONV_STRIP)
            outs[1][at, :] = acc
            outs[2][at, :] = _ln_silu(acc, ln_g_row, ln_b_row).astype(BF)
            return carry

        lax.fori_loop(0, tr // CONV_STRIP, strip, 0)

    return _call(
        name, compute, (t // tr,), [proj, proj, proj, proj, b_in, b_in, conv_w, conv_b, ln_g, ln_b],
        [_rows(tr, D, 2), _rows(tr, D, 3), _halo_before(tr, 2), _halo_before(tr, 3),
         pl.BlockSpec((1, D), lambda i: (0, 2)), pl.BlockSpec((1, D), lambda i: (0, 3)),
         _whole((HALO, D)), _whole((1, D)), _whole((1, D)), _whole((1, D))],
        [jax.ShapeDtypeStruct((t, D), F32), jax.ShapeDtypeStruct((t, D), F32), jax.ShapeDtypeStruct((t, D), BF)],
        [_rows(tr)] * 3, [pltpu.VMEM((tr + HALO, D), F32), pltpu.VMEM((SUBLANES - 1, tr + HALO, D), F32)], ("arbitrary",), rider)


def _conv_bwd(name, proj, b_in, conv_w, ln_g, ln_b, z0, z1, dz3, dproj, rider=None):
    t = proj.shape[0]
    tr = min(t, 256)
    n_tiles = t // tr

    def compute(r, outs, scr):
        zbuf, dbuf, zs, ds, dwacc = scr
        i = pl.program_id(0)
        g, b = r[5][...], r[6][...]
        zero_row = jnp.zeros((1, D), F32)

        def ln_strip(s, carry):
            at = pl.ds(pl.multiple_of(s * CONV_STRIP, CONV_STRIP), CONV_STRIP)
            _, vjp = jax.vjp(_ln_silu, r[9][at, :], g, b)
            dz1_s, dg_s, db_s = vjp(r[11][at, :])
            dbuf[at, :] = dz1_s
            return carry[0] + dg_s, carry[1] + db_s, carry[2] + jnp.sum(dz1_s, axis=0, keepdims=True)

        dg, db, dcb = lax.fori_loop(0, tr // CONV_STRIP, ln_strip, (zero_row, zero_row, zero_row))
        _, vjp_after = jax.vjp(_ln_silu, r[10][...], g, b)
        dz1_after = vjp_after(r[12][...])[0]
        dbuf[pl.ds(tr, HALO), :] = jnp.where(i < n_tiles - 1, dz1_after, 0.0)
        zbuf[pl.ds(0, HALO), :] = jnp.where(i > 0, r[8][...], 0.0)
        zbuf[pl.ds(HALO, tr), :] = r[7][...]
        _shifted_copies(dbuf, ds, tr + HALO - SUBLANES)
        _shifted_copies(zbuf, zs, tr + HALO - SUBLANES)

        def dz0_strip(s, carry):
            r0 = s * CONV_STRIP
            at = pl.ds(pl.multiple_of(r0, CONV_STRIP), CONV_STRIP)
            acc = jnp.zeros((CONV_STRIP, D), F32)
            for k in range(KW):
                acc = acc + r[4][k:k + 1, :] * _shifted(dbuf, ds, KW - 1 - k, r0, CONV_STRIP)
            a = r[0][at, :] + r[2][...]
            sg = jax.nn.sigmoid(r[1][at, :] + r[3][...])
            dcv = acc * sg
            dcg = acc * a * sg * (1.0 - sg)
            outs[0][at, :] = jnp.concatenate([dcv, dcg], axis=1).astype(BF)
            return carry[0] + jnp.sum(dcv, axis=0, keepdims=True), carry[1] + jnp.sum(dcg, axis=0, keepdims=True)

        dbv, dbg = lax.fori_loop(0, tr // CONV_STRIP, dz0_strip, (zero_row, zero_row))

        for lb in range(D // LANES):
            lanes = slice(lb * LANES, (lb + 1) * LANES)

            def dw_strip(s, accs, lanes=lanes):
                r0 = s * DW_STRIP
                dz = dbuf[pl.ds(pl.multiple_of(r0, SUBLANES), DW_STRIP), lanes]
                out = []
                for k in range(KW):
                    prod = dz * _shifted(zbuf, zs, HALO - (KW - 1) + k, r0, DW_STRIP, lanes)
                    part = prod[0:SUBLANES]
                    for q in range(1, DW_STRIP // SUBLANES):
                        part = part + prod[q * SUBLANES:(q + 1) * SUBLANES]
                    out.append(accs[k] + part)
                return tuple(out)

            accs = lax.fori_loop(0, tr // DW_STRIP, dw_strip, tuple(jnp.zeros((SUBLANES, LANES), F32) for _ in range(KW)))
            for k in range(KW):
                dwacc[pl.ds(k * SUBLANES, SUBLANES), lanes] = accs[k]
        dw_rows = [jnp.sum(dwacc[pl.ds(k * SUBLANES, SUBLANES), :], axis=0, keepdims=True) for k in range(KW)]
        dw_rows.append(jnp.zeros((HALO - KW, D), F32))
        for o, v in zip(outs[1:], (dbv, dbg, jnp.concatenate(dw_rows, axis=0), dcb, dg, db)):
            _accumulate(o, v, i)

    wide = pl.BlockSpec((tr, 2 * D), lambda i: (i, 1))
    return _call(
        name, compute, (n_tiles,), [proj, proj, b_in, b_in, conv_w, ln_g, ln_b, z0, z0, z1, z1, dz3, dz3, dproj],
        [_rows(tr, D, 2), _rows(tr, D, 3), pl.BlockSpec((1, D), lambda i: (0, 2)), pl.BlockSpec((1, D), lambda i: (0, 3)),
         _whole((HALO, D)), _whole((1, D)), _whole((1, D)),
         _rows(tr), _halo_before(tr, 0), _rows(tr), _halo_after(tr, 0, n_tiles), _rows(tr), _halo_after(tr, 0, n_tiles),
         pl.BlockSpec(memory_space=pl.ANY)],
        [jax.ShapeDtypeStruct(dproj.shape, dproj.dtype), _vec(), _vec(), _vec(HALO), _vec(), _vec(), _vec()],
        [wide] + [_whole((1, D))] * 2 + [_whole((HALO, D))] + [_whole((1, D))] * 3,
        [pltpu.VMEM((tr + HALO, D), F32), pltpu.VMEM((tr + HALO, D), F32),
         pltpu.VMEM((SUBLANES - 1, tr + HALO, D), F32), pltpu.VMEM((SUBLANES - 1, tr + HALO, D), F32),
         pltpu.VMEM((HALO * SUBLANES, D), F32)],
        ("arbitrary",), rider, aliases={13: 0})


def _merge_fn(ga, gb, bga, bgb, ya, yb):
    return jax.nn.sigmoid(ga + bga) * ya + jax.nn.sigmoid(gb + bgb) * yb


def _merge(name, proj, b_in, ya, yb, rider=None):
    t = proj.shape[0]
    tr = min(t, 256)

    def fn(i, r, _):
        return [_merge_fn(*[x[...] for x in r])]

    return _rowcall(
        name, fn, [proj, proj, b_in, b_in, ya, yb],
        [_rows(tr, D, 4), _rows(tr, D, 5), pl.BlockSpec((1, D), lambda i: (0, 4)), pl.BlockSpec((1, D), lambda i: (0, 5)),
         _rows(tr), _rows(tr)],
        1, [jax.ShapeDtypeStruct((t, D), BF)], [_rows(tr)], t // tr, rider=rider)


def _merge_bwd(name, proj, b_in, ya, yb, dm, rider=None):
    t = proj.shape[0]
    tr = min(t, 256)

    def fn(i, r, _):
        _, vjp = jax.vjp(_merge_fn, *[x[...] for x in r[:6]])
        dga, dgb, dbga, dbgb, dya, dyb = vjp(r[6][...])
        return [jnp.concatenate([dga, dgb], axis=1), dya, dyb, dbga, dbgb]

    return _rowcall(
        name, fn, [proj, proj, b_in, b_in, ya, yb, dm],
        [_rows(tr, D, 4), _rows(tr, D, 5), pl.BlockSpec((1, D), lambda i: (0, 4)), pl.BlockSpec((1, D), lambda i: (0, 5)),
         _rows(tr), _rows(tr), _rows(tr)],
        3, [jax.ShapeDtypeStruct((t, D_IN), BF)] + [jax.ShapeDtypeStruct((t, D), BF)] * 2 + [_vec(), _vec()],
        [pl.BlockSpec((tr, 2 * D), lambda i: (i, 2))] + [_rows(tr)] * 2 + [_whole((1, D))] * 2, t // tr, rider=rider)


def _loss_head(name, x, gain, target, f, g, scale):
    t = x.shape[0]
    tr = min(t, 256)

    def loss_fn(xv, gn, tgt):
        y = xv * lax.rsqrt(jnp.mean(xv * xv, axis=-1, keepdims=True) + EPS) * gn
        return 0.5 * jnp.sum(jnp.mean(jnp.square(y - tgt), axis=-1))

    def fn(i, r, _):
        loss, vjp = jax.vjp(loss_fn, r[0][...], r[1][...], r[2][...])
        dx, dgain, _ = vjp(jnp.ones((), F32))
        df, dg = _gate_grads(dx, r[3][...], r[4][...], scale)
        return [dx, df, dgain, jnp.zeros((1, D), F32) + loss, dg]

    return _rowcall(name, fn, [x, gain, target, f, g], [_rows(tr), _whole((1, D)), _rows(tr), _rows(tr), _whole((1, D))], 2,
                    [jax.ShapeDtypeStruct((t, D), F32), jax.ShapeDtypeStruct((t, D), BF), _vec(), _vec(), _vec()],
                    [_rows(tr)] * 2 + [_whole((1, D))] * 3, t // tr)


def _adamw(w, g, m, v):
    m = B1 * m + (1.0 - B1) * g
    v = B2 * v + (1.0 - B2) * jnp.square(g)
    m_hat = m / BC1
    v_hat = v / BC2
    delta = -LR * (m_hat / (jnp.sqrt(v_hat) + ADAM_EPS) + WD * w)
    return delta, m, v


ADAMW_ROWS = 64


def _adamw_group(name, items, rider=None, rows=ADAMW_ROWS):
    ins, in_specs, out_shapes, out_specs, plan = [], [], [], [], []
    first = 0
    for chip_sum, received, w, m, v in items:
        r, c = w.shape
        tr = min(r, rows)
        n = r // tr

        def tile(i, first=first, n=n):
            return jnp.clip(i - first, 0, n - 1)

        spec = pl.BlockSpec((tr, c), lambda i, tile=tile: (tile(i), 0))
        ins += [chip_sum, *received, w, m, v]
        in_specs += [pl.BlockSpec((None, tr, c), lambda i, tile=tile: (0, tile(i), 0))]
        in_specs += [pl.BlockSpec((g.shape[0], tr, c), lambda i, tile=tile: (0, tile(i), 0)) for g in received]
        in_specs += [spec] * 3
        out_shapes += [jax.ShapeDtypeStruct((r, c), F32)] * 4
        out_specs += [spec] * 4
        plan.append((first, n, [g.shape[0] for g in received]))
        first += n

    def compute(in_refs, out_refs, _):
        i = pl.program_id(0)
        at_in = at_out = 0
        for start, n, counts in plan:
            mine = in_refs[at_in:at_in + 4 + len(counts)]
            outs = out_refs[at_out:at_out + 4]
            at_in += 4 + len(counts)
            at_out += 4

            @pl.when(jnp.logical_and(i >= start, i < start + n))
            def _(mine=mine, outs=outs, counts=counts):
                g = mine[0][...].astype(F32)
                for j, count in enumerate(counts):
                    for s in range(count):
                        g = g + mine[1 + j][s].astype(F32)
                delta, m_new, v_new = _adamw(mine[-3][...], g, mine[-2][...], mine[-1][...])
                for o, val in zip(outs, (g, delta, m_new, v_new)):
                    o[...] = val

    res = _call(name, compute, (first,), ins, in_specs, out_shapes, out_specs, [], ("arbitrary",), rider)
    outs, rode = res if rider else (res, [])
    return [outs[4 * j:4 * j + 4] for j in range(len(items))], rode


def _adamw_small(name, packed_all, late_all, dws_all, vectors, w_s):
    n_vec = len(vectors)

    def body(*refs):
        p_ref, l_ref, d_ref = refs[:3]
        param_refs = refs[3:3 + 3 * n_vec + 3]
        out_refs = refs[3 + 3 * n_vec + 3:-1]
        g_ref = refs[-1]
        g = p_ref[0]
        late = l_ref[0]
        for s in range(1, NDEV):
            g = g + p_ref[s]
            late = late + l_ref[s]
        g_ref[...] = g
        g_ref[pl.ds(0, R_LATE), :] += late

        def update(gp, wmv, outs):
            delta, m_new, v_new = _adamw(wmv[0][...], gp, wmv[1][...], wmv[2][...])
            for o, val in zip(outs, (gp, delta, m_new, v_new)):
                o[...] = val

        for j, (row, rows, *_) in enumerate(vectors):
            pieces = [g_ref[pl.ds(row + r, 1), :] for r in range(rows)]
            update(pieces[0] if rows == 1 else jnp.concatenate(pieces, axis=1), param_refs[3 * j:3 * j + 3], out_refs[4 * j:4 * j + 4])
        gw = d_ref[0]
        for s in range(1, NDEV):
            gw = gw + d_ref[s]
        update(gw, param_refs[3 * n_vec:], out_refs[4 * n_vec:4 * n_vec + 4])
        out_refs[-2][...] = g_ref[pl.ds(R_CW, KW), :]
        out_refs[-1][...] = g_ref[pl.ds(R_LOSS, 1), :]

    params = [a for _, _, w, m, v in vectors for a in (w, m, v)] + list(w_s)
    out_shapes = [jax.ShapeDtypeStruct(w.shape, F32) for _, _, w, _, _ in vectors for _ in range(4)]
    out_shapes += [jax.ShapeDtypeStruct(w_s[0].shape, F32)] * 4 + [jax.ShapeDtypeStruct((KW, D), F32), _vec()]
    res = pl.pallas_call(body, name=name, out_shape=out_shapes, scratch_shapes=[pltpu.VMEM((R_TOTAL, D), F32)],
                         compiler_params=_params(None))(packed_all, late_all, dws_all, *params)
    return [res[4 * j:4 * j + 4] for j in range(n_vec + 1)], res[-2], res[-1]


def _adamw_plain(name, g, w, m, v):
    def body(g_ref, w_ref, m_ref, v_ref, d_ref, mo_ref, vo_ref):
        delta, m_new, v_new = _adamw(w_ref[...], g_ref[...], m_ref[...], v_ref[...])
        d_ref[...] = delta
        mo_ref[...] = m_new
        vo_ref[...] = v_new

    return pl.pallas_call(body, name=name, out_shape=[jax.ShapeDtypeStruct(w.shape, F32)] * 3,
                          compiler_params=_params(None))(g, w, m, v)


def _adamw_ada(name, c_all_t, dmod, dmod_late, w, m, v):
    r, c = w.shape
    tr = 256

    def fn(i, refs, _):
        ca = _silu(refs[0][...])
        dm = refs[1][...] + refs[2][...]
        g = ca[:, 0:1] * dm[0:1, :]
        for b in range(1, NDEV):
            g = g + ca[:, b:b + 1] * dm[b:b + 1, :]
        delta, m_new, v_new = _adamw(refs[3][...], g, refs[4][...], refs[5][...])
        return [g, delta, m_new, v_new]

    spec = pl.BlockSpec((tr, c), lambda i: (i, 0))
    whole = pl.BlockSpec((NDEV, c), lambda i: (0, 0))
    return _rowcall(name, fn, [c_all_t, dmod, dmod_late, w, m, v],
                    [pl.BlockSpec((tr, NDEV), lambda i: (i, 0)), whole, whole, spec, spec, spec], 4,
                    [jax.ShapeDtypeStruct((r, c), F32)] * 4, [spec] * 4, r // tr)


def _ffn_fwd(tag, x, h, g, wg, wu, wd_shard, down_rider, next_norm=None):
    t = x.shape[0]
    tm = min(t, 512)
    (gate, up, act), (wd,) = _ffn_up(f"{tag}_up", h, wg, wu, rider=_gather_rider([wd_shard]))
    row = pl.BlockSpec((1, D), lambda i, j, k: (0, 0))

    def epilogue(f, xv, gv, *norm):
        x_out = xv + 0.5 * gv * f
        return (x_out, f, _rms_mod(x_out, *norm)) if norm else (x_out, f)

    res = _mm_nn(f"{tag}_down", act, wd.reshape(F, D), tm, D, 1024, extras=(x, g, *(next_norm or ())),
                 extra_specs=(pl.BlockSpec((tm, D), lambda i, j, k: (i, 0)), row, *([row] * 3 if next_norm else [])),
                 epilogue=epilogue, out_dtypes=(F32, BF, BF) if next_norm else (F32, BF), rider=down_rider)
    (x_out, f, *h_next), rode = res if down_rider else (res, None)
    return x_out, (h_next[0] if next_norm else None), (x, h, gate, up, act, f), wd, rode


def _ffn_bwd(tag, dx_out, df, saved, gain, sh, sc, wg, wu, wd, slots, dact_rider=None, dwd_rider=None, dwgu_rider=None,
             below=None):
    x, h, gate, up, act, f = saved
    t = x.shape[0]
    tm = min(t, 1024)

    def act_bwd(da, gv, uv):
        gv = gv.astype(F32)
        s = jax.nn.sigmoid(gv)
        return da * uv.astype(F32) * (s * (1.0 + gv * (1.0 - s))), da * (gv * s)

    blk = pl.BlockSpec((tm, F // NDEV), lambda i, j, k: (i, j))
    res = _mm_nt(f"{tag}_dact", df, wd.reshape(F, D), tm, F // NDEV, out_dtypes=(BF, BF),
                 extras=(gate, up), extra_specs=(blk, blk), epilogue=act_bwd, rider=dact_rider)
    (dgate, dup), rode_dact = res if dact_rider else (res, [])
    res = _mm_tn(f"{tag}_dwd", act, df, 512, D, rider=dwd_rider)
    dwd, rode_dwd = res if dwd_rider else (res, [])
    dwd = dwd.reshape(NDEV, F // NDEV, D)
    (dwg, dwu), (sib_d, *rode_dwgu) = _dw_gate_up(f"{tag}_dwgu", h, dgate, dup,
                                                  rider=[_pair_rider([dwd])] + ([dwgu_rider] if dwgu_rider else []))
    (sum_d,) = _pair_add(f"{tag}_dwd_add", [dwd], [sib_d], slots)
    dh, (sib_g, sib_u, got_d) = _mm_nt_blocked(f"{tag}_dh", [dgate, dup], [wg, wu], tm,
                                               rider=[_pair_rider([dwg, dwu]), _chip_rider([sum_d])])
    sum_g, sum_u = _pair_add(f"{tag}_dwgu_add", [dwg, dwu], [sib_g, sib_u], slots)
    normed = _norm_mod_bwd(f"{tag}_norm_bwd", x, gain, sc, sh, dh, dx_out, below=below)
    return normed, (sum_d, [got_d]), sum_g, sum_u, rode_dact, rode_dwd, rode_dwgu


def kernel(x, c, ada_w, ada_b, norm_ffn1, ffn1_w_gate, ffn1_w_up, ffn1_w_down, norm_mix, mix_w_in, mix_b_in, sgu_ln_g, sgu_ln_b, sgu_w_s, sgu_b_s, conv_w, conv_b, conv_ln_g, conv_ln_b, w_branch_a, w_branch_b, w_out, norm_ffn2, ffn2_w_gate, ffn2_w_up, ffn2_w_down, norm_final, loss_target, m_ada_w, m_ada_b, m_norm_ffn1, m_ffn1_w_gate, m_ffn1_w_up, m_ffn1_w_down, m_norm_mix, m_mix_w_in, m_mix_b_in, m_sgu_ln_g, m_sgu_ln_b, m_sgu_w_s, m_sgu_b_s, m_conv_w, m_conv_b, m_conv_ln_g, m_conv_ln_b, m_w_branch_a, m_w_branch_b, m_w_out, m_norm_ffn2, m_ffn2_w_gate, m_ffn2_w_up, m_ffn2_w_down, m_norm_final, v_ada_w, v_ada_b, v_norm_ffn1, v_ffn1_w_gate, v_ffn1_w_up, v_ffn1_w_down, v_norm_mix, v_mix_w_in, v_mix_b_in, v_sgu_ln_g, v_sgu_ln_b, v_sgu_w_s, v_sgu_b_s, v_conv_w, v_conv_b, v_conv_ln_g, v_conv_ln_b, v_w_branch_a, v_w_branch_b, v_w_out, v_norm_ffn2, v_ffn2_w_gate, v_ffn2_w_up, v_ffn2_w_down, v_norm_final):
    mx, my, mc = _position()
    me = 4 * mx + 2 * my + mc
    chip = 2 * mx + my
    slots = jnp.stack([2 * (chip ^ k) + mc for k in range(N_CHIPS)]).astype(jnp.int32)
    t = x.shape[1]
    tm = min(t, 1024)
    x0 = x.reshape(t, D)
    target = loss_target.reshape(t, D)
    given = dict(ffn1_w_gate=(ffn1_w_gate, m_ffn1_w_gate, v_ffn1_w_gate), ffn1_w_up=(ffn1_w_up, m_ffn1_w_up, v_ffn1_w_up),
                 ffn1_w_down=(ffn1_w_down, m_ffn1_w_down, v_ffn1_w_down), mix_w_in=(mix_w_in, m_mix_w_in, v_mix_w_in),
                 w_branch_a=(w_branch_a, m_w_branch_a, v_w_branch_a), w_branch_b=(w_branch_b, m_w_branch_b, v_w_branch_b),
                 w_out=(w_out, m_w_out, v_w_out), ffn2_w_gate=(ffn2_w_gate, m_ffn2_w_gate, v_ffn2_w_gate),
                 ffn2_w_up=(ffn2_w_up, m_ffn2_w_up, v_ffn2_w_up), ffn2_w_down=(ffn2_w_down, m_ffn2_w_down, v_ffn2_w_down))
    shard = {n: wmv[0][0].astype(BF) for n, wmv in given.items()}

    ada_cols = N_MOD * D // NDEV
    c_all, taps_all, mod_all, (wg1, wu1) = _prologue(
        "prologue", jnp.pad(c, ((0, SUBLANES - 1), (0, 0))), jnp.pad(conv_w[0], ((0, HALO - KW), (0, 0))), ada_w[0],
        lax.dynamic_slice(ada_b, (0, me * ada_cols), (1, ada_cols)), [shard["ffn1_w_gate"], shard["ffn1_w_up"]])
    conv_w_full = jnp.transpose(taps_all.reshape(NDEV, HALO, CHUNK), (1, 0, 2)).reshape(HALO, D)
    mod = lax.dynamic_index_in_dim(mod_all.reshape(NDEV, NDEV, ada_cols), me, axis=1, keepdims=False).reshape(N_MOD, 1, D)
    sh1, sc1, g1, sh2, sc2, g2, sh3, sc3, g3 = [mod[i] for i in range(N_MOD)]

    h1 = _norm_mod("ffn1_norm", x0, norm_ffn1, sc1, sh1)
    x1, h2, saved1, wd1, (w_in,) = _ffn_fwd("ffn1", x0, h1, g1, wg1, wu1, shard["ffn1_w_down"],
                                             _gather_rider([shard["mix_w_in"]]), next_norm=(norm_mix, sc2, sh2))
    proj, (wg2, wa3, wb3) = _mm_nn_blocked(
        "mix_in", h2, w_in, tm, rider=_gather_rider([shard["ffn2_w_gate"], shard["w_branch_a"], shard["w_branch_b"]]))
    bias_full = jnp.repeat(sgu_b_s[0].T, CHUNK, axis=1)
    (ua,) = _sgu("sgu", proj, mix_b_in, sgu_ln_g, sgu_ln_b, sgu_w_s[0], bias_full)
    (z0, z1, z3), (wu2, wo3) = _conv("conv", proj, mix_b_in, conv_w_full, conv_b, conv_ln_g, conv_ln_b,
                                     rider=_gather_rider([shard["ffn2_w_up"], shard["w_out"]]))
    wa, wb = wa3.reshape(D, D), wb3.reshape(D, D)
    ts = min(t, 512)
    ya = _mm_nn("branch_a", ua, wa, ts, D, D)[0]
    yb = _mm_nn("branch_b", z3, wb, ts, D, D)[0]
    (merged,) = _merge("merge", proj, mix_b_in, ya, yb)
    wo = wo3.reshape(D, D)

    def mix_epilogue(yv, xv, gv, gain, sc, sh):
        x_out = xv + gv * yv
        return x_out, yv, _rms_mod(x_out, gain, sc, sh)

    tmo = min(t, 512)
    row = pl.BlockSpec((1, D), lambda i, j, k: (0, 0))
    x2, y, h3 = _mm_nn("mix_out", merged, wo, tmo, D, D, extras=(x1, g2, norm_ffn2, sc3, sh3),
                       extra_specs=(pl.BlockSpec((tmo, D), lambda i, j, k: (i, 0)), row, row, row, row),
                       epilogue=mix_epilogue, out_dtypes=(F32, BF, BF))
    x3, _, saved3, wd2, _ = _ffn_fwd("ffn2", x2, h3, g3, wg2, wu2, shard["ffn2_w_down"], None)

    norm_final2 = norm_final.reshape(1, D)
    dx3, df3, d_norm_final, loss_row, dg3 = _loss_head("loss_head", x3, norm_final2, target, saved3[-1], g3, 0.5)
    (dx2, dy, d_norm_ffn2, dsc3, dsh3, dg2), down2, sum_g2, sum_u2, _, _, _ = _ffn_bwd(
        "ffn2", dx3, df3, saved3, norm_ffn2, sh3, sc3, wg2, wu2, wd2, slots, below=(y, g2, 1.0))
    dm = _mm_nt("mix_out_bwd", dy, wo, ts, D)[0]
    dwo = _mm_tn("mix_dwo", merged, dy, 256, D).reshape(NDEV, D // NDEV, D)
    (dproj, dya, dyb, db_ga, db_gb), (got_g2_near,) = _merge_bwd("merge_bwd", proj, mix_b_in, ya, yb, dm,
                                                                 rider=_chip_rider([sum_g2], NEIGHBOURS))
    dua = _mm_nt("branch_a_bwd", dya, wa, ts, D)[0]
    dwa = _mm_tn("branch_dwa", ua, dya, 256, D).reshape(NDEV, D // NDEV, D)
    dz3 = _mm_nt("branch_b_bwd", dyb, wb, ts, D)[0]
    dwb = _mm_tn("branch_dwb", z3, dyb, 256, D).reshape(NDEV, D // NDEV, D)
    (dproj, db_u, db_v, d_sgu_g, d_sgu_b, d_ws, d_bs_t), (*sib_abo, got_g2_far) = _sgu_bwd(
        "sgu_bwd", proj, mix_b_in, sgu_ln_g, sgu_ln_b, sgu_w_s[0], bias_full, dua, dproj,
        rider=[_pair_rider([dwa, dwb, dwo]), _chip_rider([sum_g2], DIAGONAL)])
    sum_a, sum_b, sum_o = _pair_add("mix_dw_add", [dwa, dwb, dwo], sib_abo, slots)
    (dproj, db_cv, db_cg, d_cw, d_cb, d_cln_g, d_cln_b), (got_u2,) = _conv_bwd(
        "conv_bwd", proj, mix_b_in, conv_w_full, conv_ln_g, conv_ln_b, z0, z1, dz3, dproj, rider=_chip_rider([sum_u2]))
    dwin, (got_a, got_b, got_o) = _mm_tn_blocked("mix_dwin", h2, dproj, rider=_chip_rider([sum_a, sum_b, sum_o]))

    d_bs = jnp.transpose(d_bs_t[:, :HEADS])
    zero = jnp.zeros((1, D), F32)
    pack_rows = [zero, zero, zero, zero, zero, dg2, dsh3, dsc3, dg3,
                 zero, zero, d_norm_ffn2, d_norm_final,
                 db_u, db_v, db_cv, db_cg, db_ga, db_gb,
                 d_sgu_g, d_sgu_b, d_bs.reshape(1, D), d_cb, d_cln_g, d_cln_b,
                 d_cw[:KW], loss_row, jnp.zeros((R_TOTAL - R_LOSS - 1, D), F32)]
    packed = jnp.concatenate(pack_rows, axis=0)
    d_ws2 = d_ws.reshape(HEADS * CHUNK, CHUNK)
    dh2, (sib_in, packed_all, dws_all) = _mm_nt_blocked("mix_in_bwd", [dproj], [w_in], tm,
                                                        rider=[_pair_rider([dwin]), _gather_rider([packed, d_ws2])])
    (sum_in,) = _pair_add("mix_dwin_add", [dwin], [sib_in], slots)
    dx1, df1, d_norm_mix, dsc2, dsh2, dg1 = _norm_mod_bwd("mix_norm_bwd", x1, norm_mix, sc2, sh2, dh2, dx2,
                                                          below=(saved1[-1], g1, 0.5))
    (dx0, d_norm_ffn1, dsc1, dsh1), down1, sum_g1, sum_u1, (got_in_near,), _, (got_in_far,) = _ffn_bwd(
        "ffn1", dx1, df1, saved1, norm_ffn1, sh1, sc1, wg1, wu1, wd1, slots,
        dact_rider=_chip_rider([sum_in], NEIGHBOURS), dwgu_rider=_chip_rider([sum_in], DIAGONAL))
    packed_late = jnp.concatenate([dsh1, dsc1, dg1, dsh2, dsc2, jnp.zeros((4, D), F32), d_norm_ffn1, d_norm_mix,
                                   jnp.zeros((R_LATE - 11, D), F32)], axis=0)
    grads = dict(ffn2_w_gate=(sum_g2, [got_g2_near, got_g2_far]), ffn2_w_up=(sum_u2, [got_u2]), ffn2_w_down=down2,
                 mix_w_in=(sum_in, [got_in_near, got_in_far]), w_branch_a=(sum_a, [got_a]), w_branch_b=(sum_b, [got_b]),
                 w_out=(sum_o, [got_o]), ffn1_w_down=down1)
    done, (late_all, got_g1, got_u1) = _adamw_group(
        "adamw_most", [(cs, got, *[a[0] for a in given[n]]) for n, (cs, got) in grads.items()],
        rider=[_gather_rider([packed_late]), _chip_rider([sum_g1, sum_u1])])
    last, _ = _adamw_group("adamw_ffn1_in", [(sum_g1, [got_g1], *[a[0] for a in given["ffn1_w_gate"]]),
                                            (sum_u1, [got_u1], *[a[0] for a in given["ffn1_w_up"]])], rows=256)
    big_out = {n: [o.reshape(given[n][0].shape) for o in outs]
               for n, outs in zip([*grads, "ffn1_w_gate", "ffn1_w_up"], [*done, *last])}

    flat = lambda a: a.reshape(1, -1)
    vectors = [("ada_b", 0, 9, ada_b, m_ada_b, v_ada_b), ("norm_ffn1", 9, 1, norm_ffn1, m_norm_ffn1, v_norm_ffn1),
               ("norm_mix", 10, 1, norm_mix, m_norm_mix, v_norm_mix), ("norm_ffn2", 11, 1, norm_ffn2, m_norm_ffn2, v_norm_ffn2),
               ("norm_final", 12, 1, norm_final, m_norm_final, v_norm_final), ("mix_b_in", 13, 6, mix_b_in, m_mix_b_in, v_mix_b_in),
               ("sgu_ln_g", 19, 1, sgu_ln_g, m_sgu_ln_g, v_sgu_ln_g), ("sgu_ln_b", 20, 1, sgu_ln_b, m_sgu_ln_b, v_sgu_ln_b),
               ("sgu_b_s", 21, 1, sgu_b_s, m_sgu_b_s, v_sgu_b_s), ("conv_b", 22, 1, conv_b, m_conv_b, v_conv_b),
               ("conv_ln_g", 23, 1, conv_ln_g, m_conv_ln_g, v_conv_ln_g), ("conv_ln_b", 24, 1, conv_ln_b, m_conv_ln_b, v_conv_ln_b)]
    small_out, d_cw_all, loss_sum = _adamw_small(
        "adamw_small", packed_all, late_all, dws_all, [(row, rows, flat(wv), flat(mv), flat(vv)) for _, row, rows, wv, mv, vv in vectors],
        [a.reshape(HEADS * CHUNK, CHUNK) for a in (sgu_w_s, m_sgu_w_s, v_sgu_w_s)])
    small = {n: [o.reshape(wv.shape) for o in outs] for (n, _, _, wv, _, _), outs in zip(vectors, small_out)}
    small["sgu_w_s"] = [o.reshape(sgu_w_s.shape) for o in small_out[-1]]
    g_cw = lax.dynamic_slice(d_cw_all, (0, me * CHUNK), (KW, CHUNK))
    small["conv_w"] = [o.reshape(conv_w.shape) for o in (g_cw, *_adamw_plain("adamw_conv_w", g_cw, conv_w[0], m_conv_w[0], v_conv_w[0]))]
    loss = loss_sum[0, 0]

    dmod_cols = [lax.dynamic_slice(a[:, :N_MOD, :].reshape(NDEV, N_MOD * D), (0, me * ada_cols), (NDEV, ada_cols))
                 for a in (packed_all, late_all)]
    ada_out = [o.reshape(ada_w.shape) for o in _adamw_ada("adamw_ada_w", jnp.transpose(c_all), *dmod_cols, ada_w[0], m_ada_w[0], v_ada_w[0])]

    order = ["ada_w", "ada_b", "norm_ffn1", "ffn1_w_gate", "ffn1_w_up", "ffn1_w_down", "norm_mix", "mix_w_in", "mix_b_in",
             "sgu_ln_g", "sgu_ln_b", "sgu_w_s", "sgu_b_s", "conv_w", "conv_b", "conv_ln_g", "conv_ln_b", "w_branch_a",
             "w_branch_b", "w_out", "norm_ffn2", "ffn2_w_gate", "ffn2_w_up", "ffn2_w_down", "norm_final"]

    def leaf(n, kind):
        if n == "ada_w":
            return ada_out[kind]
        if n in big_out:
            return big_out[n][kind]
        return small[n][kind]

    return (loss, dx0.reshape(x.shape), *[leaf(n, kind) for kind in range(4) for n in order])
```

```python
import jax
import jax.numpy as jnp
from jax import lax
from jax.experimental import pallas as pl
from jax.experimental.pallas import tpu as pltpu

D = 1024
F = 4 * D
D_IN = 6 * D
HEADS = 8
CHUNK = 128
KW = 31
HALO = 32
N_MOD = 9
NDEV = 8
N_CHIPS = 4
EPS = 1e-6
LR, B1, B2, ADAM_EPS, WD, STEP = 0.001, 0.9, 0.999, 1e-08, 0.01, 10
BC1 = 1.0 - B1 ** STEP
BC2 = 1.0 - B2 ** STEP
VMEM_LIMIT = 56 * 1024 * 1024
MESH = pl.DeviceIdType.MESH
HBM = pl.BlockSpec(memory_space=pltpu.HBM)
VMEM = pl.BlockSpec(memory_space=pltpu.VMEM)
BF = jnp.bfloat16
F32 = jnp.float32

NN = (((1,), (0,)), ((), ()))
NT = (((1,), (1,)), ((), ()))
TN = (((0,), (0,)), ((), ()))

R_CW, R_LOSS, R_TOTAL = 25, 56, 64
R_LATE = 16


def _params(sem):
    return pltpu.CompilerParams(dimension_semantics=sem, vmem_limit_bytes=VMEM_LIMIT)


def _position():
    return lax.axis_index("x"), lax.axis_index("y"), lax.axis_index("c")


def _flip(pos, k):
    x, y, c = pos
    return (x ^ (k >> 2 & 1), y ^ (k >> 1 & 1), c ^ (k & 1))


def _index(pos):
    return 4 * pos[0] + 2 * pos[1] + pos[2]


def _gather_rows(x_ref, out_ref, send_sems, recv_sems, local_sem):
    m_per = x_ref.shape[0]
    x, y, c = _position()
    me, sibling = (x, y, c), (x, y, 1 - c)
    chips = [(1 - x, y), (x, 1 - y), (1 - x, 1 - y)]

    def rows(pos):
        return out_ref.at[pl.ds(_index(pos) * m_per, m_per), :]

    def copy(k, block, to, src=None):
        return pltpu.make_async_remote_copy(
            src_ref=rows(block) if src is None else src, dst_ref=rows(block),
            send_sem=send_sems.at[k], recv_sem=recv_sems.at[k], device_id=to, device_id_type=MESH)

    mine = pltpu.make_async_copy(x_ref, rows(me), local_sem)
    mine.start()
    first = [copy(0, me, sibling, src=x_ref)]
    first += [copy(1 + j, me, (*chip, c), src=x_ref) for j, chip in enumerate(chips)]
    for cp in first:
        cp.start()
    passed = [copy(4 + j, (*chip, c), sibling) for j, chip in enumerate(chips)]
    for j, chip in enumerate(chips):
        copy(1 + j, (*chip, c), me).wait_recv()
        passed[j].start()
    copy(0, sibling, me).wait_recv()
    for j, chip in enumerate(chips):
        copy(4 + j, (*chip, 1 - c), me).wait_recv()
    for cp in first + passed:
        cp.wait_send()
    mine.wait()


def _prologue(name, c_rows, taps, ada_w, ada_b, shards):
    rider = _gather_rider(shards)
    n = len(shards)
    nc = ada_w.shape[1]

    def body(*refs):
        c_ref, taps_ref, w_ref, b_ref = refs[:4]
        shard_refs = refs[4:4 + n]
        c_all_ref, taps_all_ref, mod_all_ref = refs[4 + n:7 + n]
        gathered_refs = refs[7 + n:7 + 2 * n]
        c_buf, mod_part, sems = refs[7 + 2 * n], refs[8 + 2 * n], refs[9 + 2 * n:]
        rider.start(shard_refs, gathered_refs, sems[9:])
        _gather_rows(c_ref, c_buf, *sems[0:3])
        c_all = jnp.concatenate([c_buf[pl.ds(d * SUBLANES, 1), :] for d in range(NDEV)], axis=0)
        c_all_ref[...] = c_all
        mod_part[...] = jnp.dot(_silu(c_all), w_ref[...], preferred_element_type=F32) + b_ref[...]
        _gather_rows(taps_ref, taps_all_ref, *sems[3:6])
        _gather_rows(mod_part, mod_all_ref, *sems[6:9])
        rider.mid(shard_refs, gathered_refs, sems[9:])
        rider.relay(shard_refs, gathered_refs, sems[9:])
        rider.finish(shard_refs, gathered_refs, sems[9:])

    small_sems = [pltpu.SemaphoreType.DMA((7,)), pltpu.SemaphoreType.DMA((7,)), pltpu.SemaphoreType.DMA] * 3
    res = pl.pallas_call(
        body, name=name,
        out_shape=[jax.ShapeDtypeStruct((NDEV, D), F32), jax.ShapeDtypeStruct((NDEV * taps.shape[0], taps.shape[1]), F32),
                   jax.ShapeDtypeStruct((NDEV * NDEV, nc), F32)] + rider.out_shapes,
        in_specs=[VMEM] * 4 + [HBM] * n, out_specs=[VMEM] * 3 + [HBM] * n,
        scratch_shapes=[pltpu.VMEM((NDEV * SUBLANES, D), F32), pltpu.VMEM((NDEV, nc), F32)] + small_sems + rider.sems,
        compiler_params=_params(None),
    )(c_rows, taps, ada_w, ada_b, *shards)
    return res[0], res[1], res[2], res[3:]


class _Rider:
    def __init__(self, ins, out_shapes, sems, start, finish, mid=None, relay=None):
        self.ins, self.out_shapes, self.sems = list(ins), list(out_shapes), list(sems)
        self.start, self.finish, self.mid, self.relay = start, finish, mid, relay


def _gather_rider(shards):
    n = len(shards)

    def setup(ins, outs, sems):
        send_sems, recv_sems, local_sems = sems
        x, y, c = _position()
        places = dict(me=(x, y, c), sibling=(x, y, 1 - c), xn=(1 - x, y, c), yn=(x, 1 - y, c), diagonal=(1 - x, 1 - y, c),
                      passed_on=(x ^ c, y ^ (1 - c), c), passed_to=(x ^ (1 - c), y ^ c, c))

        def copy(a, k, block, to, own=False):
            slot = outs[a].at[_index(block)]
            return pltpu.make_async_remote_copy(
                src_ref=ins[a] if own else slot, dst_ref=slot,
                send_sem=send_sems.at[k, a], recv_sem=recv_sems.at[k, a], device_id=to, device_id_type=MESH)

        def local(a):
            return pltpu.make_async_copy(ins[a], outs[a].at[_index(places["me"])], local_sems.at[a])

        return places, copy, local

    def start(ins, outs, sems):
        p, copy, local = setup(ins, outs, sems)
        for a in range(n):
            local(a).start()
            for k, to in enumerate(("sibling", "xn", "yn")):
                copy(a, k, p["me"], p[to], own=True).start()

    def mid(ins, outs, sems):
        p, copy, _ = setup(ins, outs, sems)
        for a in range(n):
            copy(a, 1, p["xn"], p["me"]).wait_recv()
            copy(a, 2, p["yn"], p["me"]).wait_recv()
            copy(a, 3, p["passed_on"], p["passed_to"]).start()
            copy(a, 4, p["xn"], p["sibling"]).start()
            copy(a, 5, p["yn"], p["sibling"]).start()

    def relay(ins, outs, sems):
        p, copy, _ = setup(ins, outs, sems)
        for a in range(n):
            copy(a, 3, p["diagonal"], p["me"]).wait_recv()
            copy(a, 6, p["diagonal"], p["sibling"]).start()

    def finish(ins, outs, sems):
        p, copy, local = setup(ins, outs, sems)
        x, y, c = p["me"]
        for a in range(n):
            for k, block in ((0, (x, y, 1 - c)), (4, (1 - x, y, 1 - c)), (5, (x, 1 - y, 1 - c)), (6, (1 - x, 1 - y, 1 - c))):
                copy(a, k, block, p["me"]).wait_recv()
            for k, to in enumerate(("sibling", "xn", "yn")):
                copy(a, k, p["me"], p[to], own=True).wait_send()
            copy(a, 3, p["passed_on"], p["passed_to"]).wait_send()
            for k, block in ((4, "xn"), (5, "yn"), (6, "diagonal")):
                copy(a, k, p[block], p["sibling"]).wait_send()
            local(a).wait()

    return _Rider(shards, [jax.ShapeDtypeStruct((NDEV, *s.shape), s.dtype) for s in shards],
                  [pltpu.SemaphoreType.DMA((7, n)), pltpu.SemaphoreType.DMA((7, n)), pltpu.SemaphoreType.DMA((n,))],
                  start, finish, mid, relay)


def _pair_rider(parts):
    n = len(parts)

    def copies(ins, outs, sems):
        send_sems, recv_sems = sems
        x, y, c = _position()
        q = 2 * x + y
        return [pltpu.make_async_remote_copy(
            src_ref=ins[a].at[2 * (q ^ k) + (1 - c)], dst_ref=outs[a].at[k],
            send_sem=send_sems.at[k, a], recv_sem=recv_sems.at[k, a], device_id=(x, y, 1 - c), device_id_type=MESH)
            for a in range(n) for k in range(N_CHIPS)]

    def start(ins, outs, sems):
        for cp in copies(ins, outs, sems):
            cp.start()

    def finish(ins, outs, sems):
        for cp in copies(ins, outs, sems):
            cp.wait()

    return _Rider(parts, [jax.ShapeDtypeStruct((N_CHIPS, *p.shape[1:]), p.dtype) for p in parts],
                  [pltpu.SemaphoreType.DMA((N_CHIPS, n)), pltpu.SemaphoreType.DMA((N_CHIPS, n))], start, finish)


NEIGHBOURS = (1, 2)
DIAGONAL = (3,)
OTHER_CHIPS = NEIGHBOURS + DIAGONAL


def _chip_rider(sums, ks=OTHER_CHIPS):
    n = len(sums)

    def copies(ins, outs, sems):
        send_sems, recv_sems = sems
        me = _position()
        return [pltpu.make_async_remote_copy(
            src_ref=ins[a].at[k], dst_ref=outs[a].at[j],
            send_sem=send_sems.at[j, a], recv_sem=recv_sems.at[j, a], device_id=_flip(me, 2 * k), device_id_type=MESH)
            for a in range(n) for j, k in enumerate(ks)]

    def start(ins, outs, sems):
        for cp in copies(ins, outs, sems):
            cp.start()

    def finish(ins, outs, sems):
        for cp in copies(ins, outs, sems):
            cp.wait()

    return _Rider(sums, [jax.ShapeDtypeStruct((len(ks), *s.shape[1:]), s.dtype) for s in sums],
                  [pltpu.SemaphoreType.DMA((len(ks), n)), pltpu.SemaphoreType.DMA((len(ks), n))], start, finish)


def _grid_edge(grid, last):
    cond = None
    for d, n in enumerate(grid):
        here = pl.program_id(d) == (n - 1 if last else 0)
        cond = here if cond is None else jnp.logical_and(cond, here)
    return cond


def _call(name, compute, grid, ins, in_specs, out_shapes, out_specs, scratch_shapes, semantics, rider=None, aliases=None):
    riders = [rider] if isinstance(rider, _Rider) else list(rider or [])
    n_in, n_out, n_scr = len(ins), len(out_shapes), len(scratch_shapes)
    n_rin, n_rout, n_rsem = [sum(len(part(r)) for r in riders) for part in (lambda r: r.ins, lambda r: r.out_shapes, lambda r: r.sems)]
    cuts = [0, n_in, n_in + n_rin, n_in + n_rin + n_out, n_in + n_rin + n_out + n_rout, n_in + n_rin + n_out + n_rout + n_scr]

    def body(*refs):
        in_refs, rin_refs, out_refs, rout_refs, scr_refs = [refs[a:b] for a, b in zip(cuts[:-1], cuts[1:])]
        rsem_refs = refs[cuts[-1]:]
        mine, at = [], [0, 0, 0]
        for r in riders:
            mine.append((r, rin_refs[at[0]:at[0] + len(r.ins)], rout_refs[at[1]:at[1] + len(r.out_shapes)],
                         rsem_refs[at[2]:at[2] + len(r.sems)]))
            at = [at[0] + len(r.ins), at[1] + len(r.out_shapes), at[2] + len(r.sems)]
        if riders:
            @pl.when(_grid_edge(grid, last=False))
            def _():
                for r, a, b, c in mine:
                    r.start(a, b, c)

        if any(r.mid for r in riders):
            step, steps = 0, 1
            for d, size in enumerate(grid):
                step, steps = step * size + pl.program_id(d), steps * size

            @pl.when(step == steps * 5 // 8)
            def _():
                for r, a, b, c in mine:
                    if r.mid:
                        r.mid(a, b, c)

        if any(r.relay for r in riders):
            @pl.when(_grid_edge(grid, last=True))
            def _():
                for r, a, b, c in mine:
                    if r.relay:
                        r.relay(a, b, c)

        compute(in_refs, out_refs, scr_refs)
        if riders:
            @pl.when(_grid_edge(grid, last=True))
            def _():
                for r, a, b, c in mine:
                    r.finish(a, b, c)

    res = pl.pallas_call(
        body, name=name, grid=grid,
        out_shape=list(out_shapes) + [s for r in riders for s in r.out_shapes],
        in_specs=list(in_specs) + [HBM] * n_rin, out_specs=list(out_specs) + [HBM] * n_rout,
        scratch_shapes=list(scratch_shapes) + [s for r in riders for s in r.sems],
        input_output_aliases=aliases or {}, compiler_params=_params(semantics),
    )(*ins, *[a for r in riders for a in r.ins])
    return (res[:n_out], res[n_out:]) if riders else res


def _pair_add(name, parts, from_sibling, slots):
    n = len(parts)

    def body(s_ref, *refs):
        for a in range(n):
            refs[2 * n + a][...] = (refs[a][...].astype(F32) + refs[n + a][...].astype(F32)).astype(refs[2 * n + a].dtype)

    def slab(p, picked):
        _, r, c = p.shape
        return pl.BlockSpec((None, r, c), (lambda k, s: (s[k], 0, 0)) if picked else (lambda k, s: (k, 0, 0)))

    return pl.pallas_call(
        body, name=name,
        grid_spec=pltpu.PrefetchScalarGridSpec(
            num_scalar_prefetch=1, grid=(N_CHIPS,),
            in_specs=[slab(p, True) for p in parts] + [slab(p, False) for p in parts],
            out_specs=[slab(p, False) for p in parts]),
        out_shape=[jax.ShapeDtypeStruct((N_CHIPS, *p.shape[1:]), p.dtype) for p in parts],
        compiler_params=_params(("arbitrary",)),
    )(slots, *parts, *from_sibling)


def _mm(name, pairs, dims, grid, nk, out_shapes, out_specs, extras=(), extra_specs=(), epilogue=None, acc_shape=None, rider=None):
    n_pairs = len(pairs)

    def compute(ins, outs, scratch):
        def partial_sum():
            total = None
            for p in range(n_pairs):
                d = lax.dot_general(ins[2 * p][...], ins[2 * p + 1][...], dims, preferred_element_type=F32)
                total = d if total is None else total + d
            return total

        def finish(r):
            ex = [e[...] for e in ins[2 * n_pairs:]]
            res = epilogue(r, *ex) if epilogue is not None else (r,)
            for o, v in zip(outs, res):
                o[...] = v.astype(o.dtype)

        if nk == 1:
            finish(partial_sum())
        else:
            acc = scratch[0]
            k = pl.program_id(2)

            @pl.when(k == 0)
            def _():
                acc[...] = partial_sum()

            @pl.when(k > 0)
            def _():
                acc[...] += partial_sum()

            @pl.when(k == nk - 1)
            def _():
                finish(acc[...])

    operands, specs = [], []
    for a, a_spec, b, b_spec in pairs:
        operands += [a, b]
        specs += [a_spec, b_spec]
    return _call(name, compute, grid, operands + list(extras), specs + list(extra_specs), out_shapes, out_specs,
                 [pltpu.VMEM(acc_shape, F32)] if nk > 1 else [], ("parallel", "parallel", "arbitrary"), rider)


def _single(res, rider):
    return (res[0][0], res[1]) if rider else res[0]


def _silu(x):
    return x * jax.nn.sigmoid(x)


def _ffn_up(name, h, wg, wu, rider=None):
    t = h.shape[0]
    tm = min(t, 1024)
    nb = F // NDEV

    def compute(ins, outs, _):
        hv = ins[0][...]
        g = jnp.dot(hv, ins[1][...], preferred_element_type=F32)
        u = jnp.dot(hv, ins[2][...], preferred_element_type=F32)
        outs[0][...] = g.astype(BF)
        outs[1][...] = u.astype(BF)
        outs[2][...] = (_silu(g) * u).astype(BF)

    w_spec = pl.BlockSpec((None, D, nb), lambda i, j: (j, 0, 0))
    o_spec = pl.BlockSpec((tm, nb), lambda i, j: (i, j))
    return _call(name, compute, (t // tm, NDEV), [h, wg, wu], [pl.BlockSpec((tm, D), lambda i, j: (i, 0)), w_spec, w_spec],
                 [jax.ShapeDtypeStruct((t, F), BF)] * 3, [o_spec] * 3, [], ("parallel", "arbitrary"), rider)


def _mm_nn(name, a, b, tm, tn, tk, extras=(), extra_specs=(), epilogue=None, out_dtypes=(F32,), rider=None):
    m, kk = a.shape
    n = b.shape[1]
    nk = kk // tk
    return _mm(
        name, [(a, pl.BlockSpec((tm, tk), lambda i, j, k: (i, k)), b, pl.BlockSpec((tk, tn), lambda i, j, k: (k, j)))], NN,
        (m // tm, n // tn, nk), nk,
        [jax.ShapeDtypeStruct((m, n), dt) for dt in out_dtypes],
        [pl.BlockSpec((tm, tn), lambda i, j, k: (i, j))] * len(out_dtypes),
        extras, extra_specs, epilogue, (tm, tn), rider)


def _mm_nn_blocked(name, a, b3, tm, rider=None):
    m = a.shape[0]
    nb = b3.shape[2]
    return _single(_mm(
        name, [(a, pl.BlockSpec((tm, D), lambda i, j, k: (i, 0)), b3, pl.BlockSpec((None, D, nb), lambda i, j, k: (j, 0, 0)))], NN,
        (m // tm, NDEV, 1), 1,
        [jax.ShapeDtypeStruct((m, NDEV * nb), F32)], [pl.BlockSpec((tm, nb), lambda i, j, k: (i, j))], rider=rider), rider)


def _mm_nt(name, a, b, tm, tn, out_dtypes=(F32,), extras=(), extra_specs=(), epilogue=None, rider=None):
    m, kk = a.shape
    n = b.shape[0]
    return _mm(
        name, [(a, pl.BlockSpec((tm, kk), lambda i, j, k: (i, 0)), b, pl.BlockSpec((tn, kk), lambda i, j, k: (j, 0)))], NT,
        (m // tm, n // tn, 1), 1,
        [jax.ShapeDtypeStruct((m, n), dt) for dt in out_dtypes],
        [pl.BlockSpec((tm, tn), lambda i, j, k: (i, j))] * len(out_dtypes),
        extras, extra_specs, epilogue, rider=rider)


def _mm_nt_blocked(name, a_list, b3_list, tm, rider=None):
    m = a_list[0].shape[0]
    nb = b3_list[0].shape[2]
    pairs = [(a, pl.BlockSpec((tm, nb), lambda i, j, k: (i, k)), b3, pl.BlockSpec((None, D, nb), lambda i, j, k: (k, 0, 0)))
             for a, b3 in zip(a_list, b3_list)]
    return _single(_mm(name, pairs, NT, (m // tm, 1, NDEV), NDEV,
                       [jax.ShapeDtypeStruct((m, D), F32)], [pl.BlockSpec((tm, D), lambda i, j, k: (i, 0))],
                       acc_shape=(tm, D), rider=rider), rider)


def _mm_tn(name, a, b, tm, tn, rider=None):
    t, m = a.shape
    n = b.shape[1]
    return _single(_mm(
        name, [(a, pl.BlockSpec((t, tm), lambda i, j, k: (0, i)), b, pl.BlockSpec((t, tn), lambda i, j, k: (0, j)))], TN,
        (m // tm, n // tn, 1), 1,
        [jax.ShapeDtypeStruct((m, n), BF)], [pl.BlockSpec((tm, tn), lambda i, j, k: (i, j))], rider=rider), rider)


def _mm_tn_blocked(name, a, b, rider=None):
    t = a.shape[0]
    nb = b.shape[1] // NDEV
    return _single(_mm(
        name, [(a, pl.BlockSpec((t, D), lambda i, j, k: (0, 0)), b, pl.BlockSpec((t, nb), lambda i, j, k: (0, j)))], TN,
        (1, NDEV, 1), 1,
        [jax.ShapeDtypeStruct((NDEV, D, nb), BF)], [pl.BlockSpec((None, D, nb), lambda i, j, k: (j, 0, 0))], rider=rider), rider)


def _dw_gate_up(name, h, dgate, dup, rider=None):
    t = h.shape[0]
    nb = F // NDEV

    def compute(ins, outs, _):
        hv = ins[0][...]
        outs[0][...] = lax.dot_general(hv, ins[1][...], TN, preferred_element_type=F32).astype(BF)
        outs[1][...] = lax.dot_general(hv, ins[2][...], TN, preferred_element_type=F32).astype(BF)

    d_spec = pl.BlockSpec((t, nb), lambda j: (0, j))
    o_spec = pl.BlockSpec((None, D, nb), lambda j: (j, 0, 0))
    return _call(name, compute, (NDEV,), [h, dgate, dup], [pl.BlockSpec((t, D), lambda j: (0, 0)), d_spec, d_spec],
                 [jax.ShapeDtypeStruct((NDEV, D, nb), BF)] * 2, [o_spec] * 2, [], ("arbitrary",), rider)


def _rowcall(name, fn, ins, in_specs, n_row_out, out_shapes, out_specs, grid, scratch_shapes=(), rider=None, aliases=None):
    def accumulate(o, v, i):
        @pl.when(i == 0)
        def _():
            o[...] = v.astype(o.dtype)

        @pl.when(i > 0)
        def _():
            o[...] += v.astype(o.dtype)

    def compute(in_refs, out_refs, scr):
        i = pl.program_id(0)
        vals = fn(i, in_refs, scr)
        for idx, (o, v) in enumerate(zip(out_refs, vals)):
            if idx < n_row_out:
                o[...] = v.astype(o.dtype)
            else:
                accumulate(o, v, i)

    return _call(name, compute, (grid,), ins, in_specs, out_shapes, out_specs, list(scratch_shapes), ("arbitrary",), rider, aliases)


def _rows(tr, w=D, cb=0):
    return pl.BlockSpec((tr, w), lambda i: (i, cb))


def _whole(shape):
    nd = len(shape)
    return pl.BlockSpec(shape, lambda i: (0,) * nd)


def _vec(n=1):
    return jax.ShapeDtypeStruct((n, D), F32)


def _rms_mod(x, gain, sc, sh):
    y = x * lax.rsqrt(jnp.mean(x * x, axis=-1, keepdims=True) + EPS)
    return (y * gain) * (1.0 + sc) + sh


def _layer_norm(x, g, b):
    mu = jnp.mean(x, axis=-1, keepdims=True)
    var = jnp.mean(jnp.square(x - mu), axis=-1, keepdims=True)
    return (x - mu) * lax.rsqrt(var + EPS) * g + b


def _norm_mod(name, x, gain, sc, sh):
    t = x.shape[0]
    tr = min(t, 256)

    def fn(i, r, _):
        return [_rms_mod(r[0][...], r[1][...], r[2][...], r[3][...])]

    return _rowcall(name, fn, [x, gain, sc, sh], [_rows(tr)] + [_whole((1, D))] * 3, 1,
                    [jax.ShapeDtypeStruct((t, D), BF)], [_rows(tr)], t // tr)[0]


def _gate_grads(dx, f, g, scale):
    return scale * g * dx, jnp.sum(scale * dx * f.astype(F32), axis=0, keepdims=True)


def _norm_mod_bwd(name, x, gain, sc, sh, dh, dres, below=None, rider=None):
    t = x.shape[0]
    tr = min(t, 256)

    def fn(i, r, _):
        _, vjp = jax.vjp(_rms_mod, r[0][...], r[1][...], r[2][...], r[3][...])
        dx, dgain, dsc, dsh = vjp(r[4][...])
        dx = dx + r[5][...]
        if below is None:
            return [dx, dgain, dsc, dsh]
        df, dg = _gate_grads(dx, r[6][...], r[7][...], below[2])
        return [dx, df, dgain, dsc, dsh, dg]

    ins, specs = [x, gain, sc, sh, dh, dres], [_rows(tr)] + [_whole((1, D))] * 3 + [_rows(tr)] * 2
    outs, out_specs = [jax.ShapeDtypeStruct((t, D), F32)], [_rows(tr)]
    if below is not None:
        ins, specs = ins + [below[0], below[1]], specs + [_rows(tr), _whole((1, D))]
        outs, out_specs = outs + [jax.ShapeDtypeStruct((t, D), BF)], out_specs + [_rows(tr)]
    n_vec = 3 if below is None else 4
    return _rowcall(name, fn, ins, specs, len(outs), outs + [_vec()] * n_vec, out_specs + [_whole((1, D))] * n_vec, t // tr,
                    rider=rider)


def _sgu_pre(up, vp, bu, bv, ln_g, ln_b):
    return jax.nn.gelu(up + bu), _layer_norm(jax.nn.gelu(vp + bv), ln_g, ln_b)


def _causal(w_ref, h):
    rows = lax.broadcasted_iota(jnp.int32, (CHUNK, CHUNK), 0)
    cols = lax.broadcasted_iota(jnp.int32, (CHUNK, CHUNK), 1)
    return jnp.where(cols <= rows, w_ref[h], 0.0)


def _sgu(name, proj, b_in, ln_g, ln_b, w_s, bias_full, rider=None):
    t = proj.shape[0]

    def fn(i, r, _):
        u, v = _sgu_pre(r[0][...], r[1][...], r[2][...], r[3][...], r[4][...], r[5][...])
        vb = v.astype(BF)
        mixed = [jnp.dot(_causal(r[6], h).astype(BF), vb[:, h * CHUNK:(h + 1) * CHUNK], preferred_element_type=F32)
                 for h in range(HEADS)]
        return [u * (jnp.concatenate(mixed, axis=1) + r[7][...])]

    return _rowcall(
        name, fn, [proj, proj, b_in, b_in, ln_g, ln_b, w_s, bias_full],
        [_rows(CHUNK, D, 0), _rows(CHUNK, D, 1), pl.BlockSpec((1, D), lambda i: (0, 0)), pl.BlockSpec((1, D), lambda i: (0, 1)),
         _whole((1, D)), _whole((1, D)), _whole((HEADS, CHUNK, CHUNK)), _whole((CHUNK, D))],
        1, [jax.ShapeDtypeStruct((t, D), BF)], [_rows(CHUNK)], t // CHUNK, rider=rider)


def _sgu_bwd(name, proj, b_in, ln_g, ln_b, w_s, bias_full, dout, dproj, rider=None):
    t = proj.shape[0]

    def fn(i, r, _):
        (u, v), vjp = jax.vjp(_sgu_pre, r[0][...], r[1][...], r[2][...], r[3][...], r[4][...], r[5][...])
        vb = v.astype(BF)
        d = r[8][...]
        masks = [_causal(r[6], h).astype(BF) for h in range(HEADS)]
        cols = [slice(h * CHUNK, (h + 1) * CHUNK) for h in range(HEADS)]
        mixed = jnp.concatenate([jnp.dot(masks[h], vb[:, cols[h]], preferred_element_type=F32) for h in range(HEADS)], axis=1)
        du = d * (mixed + r[7][...])
        dmix = d * u
        dmb = dmix.astype(BF)
        dv = jnp.concatenate([lax.dot_general(masks[h], dmb[:, cols[h]], TN, preferred_element_type=F32) for h in range(HEADS)], axis=1)
        rows = lax.broadcasted_iota(jnp.int32, (CHUNK, CHUNK), 0)
        lanes = lax.broadcasted_iota(jnp.int32, (CHUNK, CHUNK), 1)
        dws = jnp.stack([jnp.where(lanes <= rows, lax.dot_general(dmb[:, cols[h]], vb[:, cols[h]], NT, preferred_element_type=F32), 0.0)
                         for h in range(HEADS)])
        dbs = jnp.zeros((CHUNK, CHUNK), F32)
        for h in range(HEADS):
            dbs = dbs + jnp.where(lanes == h, jnp.sum(dmix[:, cols[h]], axis=1, keepdims=True), 0.0)
        dup, dvp, dbu, dbv, dg, db = vjp((du, dv))
        return [jnp.concatenate([dup, dvp], axis=1), dbu, dbv, dg, db, dws, dbs]

    return _rowcall(
        name, fn, [proj, proj, b_in, b_in, ln_g, ln_b, w_s, bias_full, dout, dproj],
        [_rows(CHUNK, D, 0), _rows(CHUNK, D, 1), pl.BlockSpec((1, D), lambda i: (0, 0)), pl.BlockSpec((1, D), lambda i: (0, 1)),
         _whole((1, D)), _whole((1, D)), _whole((HEADS, CHUNK, CHUNK)), _whole((CHUNK, D)), _rows(CHUNK),
         pl.BlockSpec(memory_space=pl.ANY)],
        1, [jax.ShapeDtypeStruct(dproj.shape, dproj.dtype)] + [_vec()] * 4
        + [jax.ShapeDtypeStruct((HEADS, CHUNK, CHUNK), F32), jax.ShapeDtypeStruct((CHUNK, CHUNK), F32)],
        [pl.BlockSpec((CHUNK, 2 * D), lambda i: (i, 0))] + [_whole((1, D))] * 4 + [_whole((HEADS, CHUNK, CHUNK)), _whole((CHUNK, CHUNK))],
        t // CHUNK, rider=rider, aliases={9: 0})


def _halo_before(tr, cb):
    return pl.BlockSpec((HALO, D), lambda i: (jnp.maximum(i * (tr // HALO) - 1, 0), cb))


def _halo_after(tr, cb, n_tiles):
    return pl.BlockSpec((HALO, D), lambda i: (jnp.minimum((i + 1) * (tr // HALO), n_tiles * (tr // HALO) - 1), cb))


def _ln_silu(z, g, b):
    return _silu(_layer_norm(z, g, b))


SUBLANES = 8
LANES = 128
CONV_STRIP = 16
DW_STRIP = 32


def _shifted_copies(buf, copies, rows):
    for b in range(1, SUBLANES):
        copies[b - 1, pl.ds(0, rows), :] = buf[pl.ds(b, rows), :]


def _shifted(buf, copies, offset, start, rows, lanes=slice(None)):
    at = pl.ds(pl.multiple_of(start + SUBLANES * (offset // SUBLANES), SUBLANES), rows)
    return buf[at, lanes] if offset % SUBLANES == 0 else copies[offset % SUBLANES - 1, at, lanes]


def _accumulate(o, v, i):
    @pl.when(i == 0)
    def _():
        o[...] = v.astype(o.dtype)

    @pl.when(i > 0)
    def _():
        o[...] += v.astype(o.dtype)


def _conv(name, proj, b_in, conv_w, conv_b, ln_g, ln_b, rider=None):
    t = proj.shape[0]
    tr = min(t, 256)

    def compute(r, outs, scr):
        zbuf, zs = scr
        i = pl.program_id(0)
        bv, bg = r[4][...], r[5][...]
        z0 = (r[0][...] + bv) * jax.nn.sigmoid(r[1][...] + bg)
        before = (r[2][...] + bv) * jax.nn.sigmoid(r[3][...] + bg)
        zbuf[pl.ds(0, HALO), :] = jnp.where(i > 0, before, 0.0)
        zbuf[pl.ds(HALO, tr), :] = z0
        outs[0][...] = z0
        _shifted_copies(zbuf, zs, tr + HALO - SUBLANES)

        def strip(s, carry):
            r0 = s * CONV_STRIP
            acc = jnp.zeros((CONV_STRIP, D), F32) + r[7][...]
            for k in range(KW):
                acc = acc + r[6][k:k + 1, :] * _shifted(zbuf, zs, HALO - (KW - 1) + k, r0, CONV_STRIP)
            outs[1][pl.ds(pl.multiple_of(r0, SUBLANES), CONV_STRIP), :] = acc
            return carry

        lax.fori_loop(0, tr // CONV_STRIP, strip, 0)
        outs[2][...] = _ln_silu(outs[1][...], r[8][...], r[9][...]).astype(BF)

    return _call(
        name, compute, (t // tr,), [proj, proj, proj, proj, b_in, b_in, conv_w, conv_b, ln_g, ln_b],
        [_rows(tr, D, 2), _rows(tr, D, 3), _halo_before(tr, 2), _halo_before(tr, 3),
         pl.BlockSpec((1, D), lambda i: (0, 2)), pl.BlockSpec((1, D), lambda i: (0, 3)),
         _whole((HALO, D)), _whole((1, D)), _whole((1, D)), _whole((1, D))],
        [jax.ShapeDtypeStruct((t, D), F32), jax.ShapeDtypeStruct((t, D), F32), jax.ShapeDtypeStruct((t, D), BF)],
        [_rows(tr)] * 3, [pltpu.VMEM((tr + HALO, D), F32), pltpu.VMEM((SUBLANES - 1, tr + HALO, D), F32)], ("arbitrary",), rider)


def _conv_bwd(name, proj, b_in, conv_w, ln_g, ln_b, z0, z1, dz3, dproj, rider=None):
    t = proj.shape[0]
    tr = min(t, 256)
    n_tiles = t // tr

    def compute(r, outs, scr):
        zbuf, dbuf, zs, ds, dwacc = scr
        i = pl.program_id(0)
        g, b = r[5][...], r[6][...]
        zero_row = jnp.zeros((1, D), F32)
        _, vjp = jax.vjp(_ln_silu, r[9][...], g, b)
        dz1, dg, db = vjp(r[11][...])
        dcb = jnp.sum(dz1, axis=0, keepdims=True)
        _, vjp_after = jax.vjp(_ln_silu, r[10][...], g, b)
        dz1_after = vjp_after(r[12][...])[0]
        dbuf[pl.ds(0, tr), :] = dz1
        dbuf[pl.ds(tr, HALO), :] = jnp.where(i < n_tiles - 1, dz1_after, 0.0)
        zbuf[pl.ds(0, HALO), :] = jnp.where(i > 0, r[8][...], 0.0)
        zbuf[pl.ds(HALO, tr), :] = r[7][...]
        _shifted_copies(dbuf, ds, tr + HALO - SUBLANES)
        _shifted_copies(zbuf, zs, tr + HALO - SUBLANES)

        def dz0_strip(s, carry):
            r0 = s * CONV_STRIP
            at = pl.ds(pl.multiple_of(r0, CONV_STRIP), CONV_STRIP)
            acc = jnp.zeros((CONV_STRIP, D), F32)
            for k in range(KW):
                acc = acc + r[4][k:k + 1, :] * _shifted(dbuf, ds, KW - 1 - k, r0, CONV_STRIP)
            a = r[0][at, :] + r[2][...]
            sg = jax.nn.sigmoid(r[1][at, :] + r[3][...])
            dcv = acc * sg
            dcg = acc * a * sg * (1.0 - sg)
            outs[0][at, :] = jnp.concatenate([dcv, dcg], axis=1).astype(BF)
            return carry[0] + jnp.sum(dcv, axis=0, keepdims=True), carry[1] + jnp.sum(dcg, axis=0, keepdims=True)

        dbv, dbg = lax.fori_loop(0, tr // CONV_STRIP, dz0_strip, (zero_row, zero_row))

        for lb in range(D // LANES):
            lanes = slice(lb * LANES, (lb + 1) * LANES)

            def dw_strip(s, accs, lanes=lanes):
                r0 = s * DW_STRIP
                dz = dbuf[pl.ds(pl.multiple_of(r0, SUBLANES), DW_STRIP), lanes]
                out = []
                for k in range(KW):
                    prod = dz * _shifted(zbuf, zs, HALO - (KW - 1) + k, r0, DW_STRIP, lanes)
                    part = prod[0:SUBLANES]
                    for q in range(1, DW_STRIP // SUBLANES):
                        part = part + prod[q * SUBLANES:(q + 1) * SUBLANES]
                    out.append(accs[k] + part)
                return tuple(out)

            accs = lax.fori_loop(0, tr // DW_STRIP, dw_strip, tuple(jnp.zeros((SUBLANES, LANES), F32) for _ in range(KW)))
            for k in range(KW):
                dwacc[pl.ds(k * SUBLANES, SUBLANES), lanes] = accs[k]
        dw_rows = [jnp.sum(dwacc[pl.ds(k * SUBLANES, SUBLANES), :], axis=0, keepdims=True) for k in range(KW)]
        dw_rows.append(jnp.zeros((HALO - KW, D), F32))
        for o, v in zip(outs[1:], (dbv, dbg, jnp.concatenate(dw_rows, axis=0), dcb, dg, db)):
            _accumulate(o, v, i)

    wide = pl.BlockSpec((tr, 2 * D), lambda i: (i, 1))
    return _call(
        name, compute, (n_tiles,), [proj, proj, b_in, b_in, conv_w, ln_g, ln_b, z0, z0, z1, z1, dz3, dz3, dproj],
        [_rows(tr, D, 2), _rows(tr, D, 3), pl.BlockSpec((1, D), lambda i: (0, 2)), pl.BlockSpec((1, D), lambda i: (0, 3)),
         _whole((HALO, D)), _whole((1, D)), _whole((1, D)),
         _rows(tr), _halo_before(tr, 0), _rows(tr), _halo_after(tr, 0, n_tiles), _rows(tr), _halo_after(tr, 0, n_tiles),
         pl.BlockSpec(memory_space=pl.ANY)],
        [jax.ShapeDtypeStruct(dproj.shape, dproj.dtype), _vec(), _vec(), _vec(HALO), _vec(), _vec(), _vec()],
        [wide] + [_whole((1, D))] * 2 + [_whole((HALO, D))] + [_whole((1, D))] * 3,
        [pltpu.VMEM((tr + HALO, D), F32), pltpu.VMEM((tr + HALO, D), F32),
         pltpu.VMEM((SUBLANES - 1, tr + HALO, D), F32), pltpu.VMEM((SUBLANES - 1, tr + HALO, D), F32),
         pltpu.VMEM((HALO * SUBLANES, D), F32)],
        ("arbitrary",), rider, aliases={13: 0})


def _merge_fn(ga, gb, bga, bgb, ya, yb):
    return jax.nn.sigmoid(ga + bga) * ya + jax.nn.sigmoid(gb + bgb) * yb


def _merge(name, proj, b_in, ya, yb, rider=None):
    t = proj.shape[0]
    tr = min(t, 256)

    def fn(i, r, _):
        return [_merge_fn(*[x[...] for x in r])]

    return _rowcall(
        name, fn, [proj, proj, b_in, b_in, ya, yb],
        [_rows(tr, D, 4), _rows(tr, D, 5), pl.BlockSpec((1, D), lambda i: (0, 4)), pl.BlockSpec((1, D), lambda i: (0, 5)),
         _rows(tr), _rows(tr)],
        1, [jax.ShapeDtypeStruct((t, D), BF)], [_rows(tr)], t // tr, rider=rider)


def _merge_bwd(name, proj, b_in, ya, yb, dm, rider=None):
    t = proj.shape[0]
    tr = min(t, 256)

    def fn(i, r, _):
        _, vjp = jax.vjp(_merge_fn, *[x[...] for x in r[:6]])
        dga, dgb, dbga, dbgb, dya, dyb = vjp(r[6][...])
        return [jnp.concatenate([dga, dgb], axis=1), dya, dyb, dbga, dbgb]

    return _rowcall(
        name, fn, [proj, proj, b_in, b_in, ya, yb, dm],
        [_rows(tr, D, 4), _rows(tr, D, 5), pl.BlockSpec((1, D), lambda i: (0, 4)), pl.BlockSpec((1, D), lambda i: (0, 5)),
         _rows(tr), _rows(tr), _rows(tr)],
        3, [jax.ShapeDtypeStruct((t, D_IN), BF)] + [jax.ShapeDtypeStruct((t, D), BF)] * 2 + [_vec(), _vec()],
        [pl.BlockSpec((tr, 2 * D), lambda i: (i, 2))] + [_rows(tr)] * 2 + [_whole((1, D))] * 2, t // tr, rider=rider)


def _loss_head(name, x, gain, target, f, g, scale):
    t = x.shape[0]
    tr = min(t, 256)

    def loss_fn(xv, gn, tgt):
        y = xv * lax.rsqrt(jnp.mean(xv * xv, axis=-1, keepdims=True) + EPS) * gn
        return 0.5 * jnp.sum(jnp.mean(jnp.square(y - tgt), axis=-1))

    def fn(i, r, _):
        loss, vjp = jax.vjp(loss_fn, r[0][...], r[1][...], r[2][...])
        dx, dgain, _ = vjp(jnp.ones((), F32))
        df, dg = _gate_grads(dx, r[3][...], r[4][...], scale)
        return [dx, df, dgain, jnp.zeros((1, D), F32) + loss, dg]

    return _rowcall(name, fn, [x, gain, target, f, g], [_rows(tr), _whole((1, D)), _rows(tr), _rows(tr), _whole((1, D))], 2,
                    [jax.ShapeDtypeStruct((t, D), F32), jax.ShapeDtypeStruct((t, D), BF), _vec(), _vec(), _vec()],
                    [_rows(tr)] * 2 + [_whole((1, D))] * 3, t // tr)


def _adamw(w, g, m, v):
    m = B1 * m + (1.0 - B1) * g
    v = B2 * v + (1.0 - B2) * jnp.square(g)
    m_hat = m / BC1
    v_hat = v / BC2
    delta = -LR * (m_hat / (jnp.sqrt(v_hat) + ADAM_EPS) + WD * w)
    return delta, m, v


ADAMW_ROWS = 64


def _adamw_group(name, items, rider=None, rows=ADAMW_ROWS):
    ins, in_specs, out_shapes, out_specs, plan = [], [], [], [], []
    first = 0
    for chip_sum, received, w, m, v in items:
        r, c = w.shape
        tr = min(r, rows)
        n = r // tr

        def tile(i, first=first, n=n):
            return jnp.clip(i - first, 0, n - 1)

        spec = pl.BlockSpec((tr, c), lambda i, tile=tile: (tile(i), 0))
        ins += [chip_sum, *received, w, m, v]
        in_specs += [pl.BlockSpec((None, tr, c), lambda i, tile=tile: (0, tile(i), 0))]
        in_specs += [pl.BlockSpec((g.shape[0], tr, c), lambda i, tile=tile: (0, tile(i), 0)) for g in received]
        in_specs += [spec] * 3
        out_shapes += [jax.ShapeDtypeStruct((r, c), F32)] * 4
        out_specs += [spec] * 4
        plan.append((first, n, [g.shape[0] for g in received]))
        first += n

    def compute(in_refs, out_refs, _):
        i = pl.program_id(0)
        at_in = at_out = 0
        for start, n, counts in plan:
            mine = in_refs[at_in:at_in + 4 + len(counts)]
            outs = out_refs[at_out:at_out + 4]
            at_in += 4 + len(counts)
            at_out += 4

            @pl.when(jnp.logical_and(i >= start, i < start + n))
            def _(mine=mine, outs=outs, counts=counts):
                g = mine[0][...].astype(F32)
                for j, count in enumerate(counts):
                    for s in range(count):
                        g = g + mine[1 + j][s].astype(F32)
                delta, m_new, v_new = _adamw(mine[-3][...], g, mine[-2][...], mine[-1][...])
                for o, val in zip(outs, (g, delta, m_new, v_new)):
                    o[...] = val

    res = _call(name, compute, (first,), ins, in_specs, out_shapes, out_specs, [], ("arbitrary",), rider)
    outs, rode = res if rider else (res, [])
    return [outs[4 * j:4 * j + 4] for j in range(len(items))], rode


def _adamw_small(name, packed_all, late_all, dws_all, vectors, w_s):
    n_vec = len(vectors)

    def body(*refs):
        p_ref, l_ref, d_ref = refs[:3]
        param_refs = refs[3:3 + 3 * n_vec + 3]
        out_refs = refs[3 + 3 * n_vec + 3:-1]
        g_ref = refs[-1]
        g = p_ref[0]
        late = l_ref[0]
        for s in range(1, NDEV):
            g = g + p_ref[s]
            late = late + l_ref[s]
        g_ref[...] = g
        g_ref[pl.ds(0, R_LATE), :] += late

        def update(gp, wmv, outs):
            delta, m_new, v_new = _adamw(wmv[0][...], gp, wmv[1][...], wmv[2][...])
            for o, val in zip(outs, (gp, delta, m_new, v_new)):
                o[...] = val

        for j, (row, rows, *_) in enumerate(vectors):
            pieces = [g_ref[pl.ds(row + r, 1), :] for r in range(rows)]
            update(pieces[0] if rows == 1 else jnp.concatenate(pieces, axis=1), param_refs[3 * j:3 * j + 3], out_refs[4 * j:4 * j + 4])
        gw = d_ref[0]
        for s in range(1, NDEV):
            gw = gw + d_ref[s]
        update(gw, param_refs[3 * n_vec:], out_refs[4 * n_vec:4 * n_vec + 4])
        out_refs[-2][...] = g_ref[pl.ds(R_CW, KW), :]
        out_refs[-1][...] = g_ref[pl.ds(R_LOSS, 1), :]

    params = [a for _, _, w, m, v in vectors for a in (w, m, v)] + list(w_s)
    out_shapes = [jax.ShapeDtypeStruct(w.shape, F32) for _, _, w, _, _ in vectors for _ in range(4)]
    out_shapes += [jax.ShapeDtypeStruct(w_s[0].shape, F32)] * 4 + [jax.ShapeDtypeStruct((KW, D), F32), _vec()]
    res = pl.pallas_call(body, name=name, out_shape=out_shapes, scratch_shapes=[pltpu.VMEM((R_TOTAL, D), F32)],
                         compiler_params=_params(None))(packed_all, late_all, dws_all, *params)
    return [res[4 * j:4 * j + 4] for j in range(n_vec + 1)], res[-2], res[-1]


def _adamw_plain(name, g, w, m, v):
    def body(g_ref, w_ref, m_ref, v_ref, d_ref, mo_ref, vo_ref):
        delta, m_new, v_new = _adamw(w_ref[...], g_ref[...], m_ref[...], v_ref[...])
        d_ref[...] = delta
        mo_ref[...] = m_new
        vo_ref[...] = v_new

    return pl.pallas_call(body, name=name, out_shape=[jax.ShapeDtypeStruct(w.shape, F32)] * 3,
                          compiler_params=_params(None))(g, w, m, v)


def _adamw_ada(name, c_all_t, dmod, dmod_late, w, m, v):
    r, c = w.shape
    tr = 256

    def fn(i, refs, _):
        ca = _silu(refs[0][...])
        dm = refs[1][...] + refs[2][...]
        g = ca[:, 0:1] * dm[0:1, :]
        for b in range(1, NDEV):
            g = g + ca[:, b:b + 1] * dm[b:b + 1, :]
        delta, m_new, v_new = _adamw(refs[3][...], g, refs[4][...], refs[5][...])
        return [g, delta, m_new, v_new]

    spec = pl.BlockSpec((tr, c), lambda i: (i, 0))
    whole = pl.BlockSpec((NDEV, c), lambda i: (0, 0))
    return _rowcall(name, fn, [c_all_t, dmod, dmod_late, w, m, v],
                    [pl.BlockSpec((tr, NDEV), lambda i: (i, 0)), whole, whole, spec, spec, spec], 4,
                    [jax.ShapeDtypeStruct((r, c), F32)] * 4, [spec] * 4, r // tr)


def _ffn_fwd(tag, x, h, g, wg, wu, wd_shard, down_rider, next_norm=None):
    t = x.shape[0]
    tm = min(t, 1024)
    (gate, up, act), (wd,) = _ffn_up(f"{tag}_up", h, wg, wu, rider=_gather_rider([wd_shard]))
    row = pl.BlockSpec((1, D), lambda i, j, k: (0, 0))

    def epilogue(f, xv, gv, *norm):
        x_out = xv + 0.5 * gv * f
        return (x_out, f, _rms_mod(x_out, *norm)) if norm else (x_out, f)

    res = _mm_nn(f"{tag}_down", act, wd.reshape(F, D), tm, D, 1024, extras=(x, g, *(next_norm or ())),
                 extra_specs=(pl.BlockSpec((tm, D), lambda i, j, k: (i, 0)), row, *([row] * 3 if next_norm else [])),
                 epilogue=epilogue, out_dtypes=(F32, BF, BF) if next_norm else (F32, BF), rider=down_rider)
    (x_out, f, *h_next), rode = res if down_rider else (res, None)
    return x_out, (h_next[0] if next_norm else None), (x, h, gate, up, act, f), wd, rode


def _ffn_bwd(tag, dx_out, df, saved, gain, sh, sc, wg, wu, wd, slots, dact_rider=None, dwd_rider=None, dwgu_rider=None,
             below=None):
    x, h, gate, up, act, f = saved
    t = x.shape[0]
    tm = min(t, 1024)

    def act_bwd(da, gv, uv):
        gv = gv.astype(F32)
        s = jax.nn.sigmoid(gv)
        return da * uv.astype(F32) * (s * (1.0 + gv * (1.0 - s))), da * (gv * s)

    blk = pl.BlockSpec((tm, F // NDEV), lambda i, j, k: (i, j))
    res = _mm_nt(f"{tag}_dact", df, wd.reshape(F, D), tm, F // NDEV, out_dtypes=(BF, BF),
                 extras=(gate, up), extra_specs=(blk, blk), epilogue=act_bwd, rider=dact_rider)
    (dgate, dup), rode_dact = res if dact_rider else (res, [])
    res = _mm_tn(f"{tag}_dwd", act, df, 512, D, rider=dwd_rider)
    dwd, rode_dwd = res if dwd_rider else (res, [])
    dwd = dwd.reshape(NDEV, F // NDEV, D)
    (dwg, dwu), (sib_d, *rode_dwgu) = _dw_gate_up(f"{tag}_dwgu", h, dgate, dup,
                                                  rider=[_pair_rider([dwd])] + ([dwgu_rider] if dwgu_rider else []))
    (sum_d,) = _pair_add(f"{tag}_dwd_add", [dwd], [sib_d], slots)
    dh, (sib_g, sib_u, got_d) = _mm_nt_blocked(f"{tag}_dh", [dgate, dup], [wg, wu], t,
                                               rider=[_pair_rider([dwg, dwu]), _chip_rider([sum_d])])
    sum_g, sum_u = _pair_add(f"{tag}_dwgu_add", [dwg, dwu], [sib_g, sib_u], slots)
    normed = _norm_mod_bwd(f"{tag}_norm_bwd", x, gain, sc, sh, dh, dx_out, below=below)
    return normed, (sum_d, [got_d]), sum_g, sum_u, rode_dact, rode_dwd, rode_dwgu


def _ffn_bwd_last(tag, dx_out, df, saved, gain, sh, sc, wg, wu, wd, slots, dact_rider, dwgu_rider):
    x, h, gate, up, act, f = saved
    t = x.shape[0]
    tm = min(t, 1024)

    def act_bwd(da, gv, uv):
        gv = gv.astype(F32)
        s = jax.nn.sigmoid(gv)
        return da * uv.astype(F32) * (s * (1.0 + gv * (1.0 - s))), da * (gv * s)

    blk = pl.BlockSpec((tm, F // NDEV), lambda i, j, k: (i, j))
    (dgate, dup), rode_dact = _mm_nt(f"{tag}_dact", df, wd.reshape(F, D), tm, F // NDEV, out_dtypes=(BF, BF),
                                     extras=(gate, up), extra_specs=(blk, blk), epilogue=act_bwd, rider=dact_rider)
    (dwg, dwu), rode_dwgu = _dw_gate_up(f"{tag}_dwgu", h, dgate, dup, rider=dwgu_rider)
    dwd, (sib_g, sib_u) = _mm_tn(f"{tag}_dwd", act, df, 512, D, rider=_pair_rider([dwg, dwu]))
    dwd = dwd.reshape(NDEV, F // NDEV, D)
    sum_g, sum_u = _pair_add(f"{tag}_dwgu_add", [dwg, dwu], [sib_g, sib_u], slots)
    dh, (sib_d, got_g) = _mm_nt_blocked(f"{tag}_dh", [dgate, dup], [wg, wu], t,
                                        rider=[_pair_rider([dwd]), _chip_rider([sum_g])])
    (sum_d,) = _pair_add(f"{tag}_dwd_add", [dwd], [sib_d], slots)
    normed, (got_u_near,) = _norm_mod_bwd(f"{tag}_norm_bwd", x, gain, sc, sh, dh, dx_out,
                                          rider=_chip_rider([sum_u], NEIGHBOURS))
    return normed, (sum_g, [got_g]), (sum_u, [got_u_near]), sum_d, rode_dact, rode_dwgu


def kernel(x, c, ada_w, ada_b, norm_ffn1, ffn1_w_gate, ffn1_w_up, ffn1_w_down, norm_mix, mix_w_in, mix_b_in, sgu_ln_g, sgu_ln_b, sgu_w_s, sgu_b_s, conv_w, conv_b, conv_ln_g, conv_ln_b, w_branch_a, w_branch_b, w_out, norm_ffn2, ffn2_w_gate, ffn2_w_up, ffn2_w_down, norm_final, loss_target, m_ada_w, m_ada_b, m_norm_ffn1, m_ffn1_w_gate, m_ffn1_w_up, m_ffn1_w_down, m_norm_mix, m_mix_w_in, m_mix_b_in, m_sgu_ln_g, m_sgu_ln_b, m_sgu_w_s, m_sgu_b_s, m_conv_w, m_conv_b, m_conv_ln_g, m_conv_ln_b, m_w_branch_a, m_w_branch_b, m_w_out, m_norm_ffn2, m_ffn2_w_gate, m_ffn2_w_up, m_ffn2_w_down, m_norm_final, v_ada_w, v_ada_b, v_norm_ffn1, v_ffn1_w_gate, v_ffn1_w_up, v_ffn1_w_down, v_norm_mix, v_mix_w_in, v_mix_b_in, v_sgu_ln_g, v_sgu_ln_b, v_sgu_w_s, v_sgu_b_s, v_conv_w, v_conv_b, v_conv_ln_g, v_conv_ln_b, v_w_branch_a, v_w_branch_b, v_w_out, v_norm_ffn2, v_ffn2_w_gate, v_ffn2_w_up, v_ffn2_w_down, v_norm_final):
    mx, my, mc = _position()
    me = 4 * mx + 2 * my + mc
    chip = 2 * mx + my
    slots = jnp.stack([2 * (chip ^ k) + mc for k in range(N_CHIPS)]).astype(jnp.int32)
    t = x.shape[1]
    tm = min(t, 1024)
    x0 = x.reshape(t, D)
    target = loss_target.reshape(t, D)
    given = dict(ffn1_w_gate=(ffn1_w_gate, m_ffn1_w_gate, v_ffn1_w_gate), ffn1_w_up=(ffn1_w_up, m_ffn1_w_up, v_ffn1_w_up),
                 ffn1_w_down=(ffn1_w_down, m_ffn1_w_down, v_ffn1_w_down), mix_w_in=(mix_w_in, m_mix_w_in, v_mix_w_in),
                 w_branch_a=(w_branch_a, m_w_branch_a, v_w_branch_a), w_branch_b=(w_branch_b, m_w_branch_b, v_w_branch_b),
                 w_out=(w_out, m_w_out, v_w_out), ffn2_w_gate=(ffn2_w_gate, m_ffn2_w_gate, v_ffn2_w_gate),
                 ffn2_w_up=(ffn2_w_up, m_ffn2_w_up, v_ffn2_w_up), ffn2_w_down=(ffn2_w_down, m_ffn2_w_down, v_ffn2_w_down))
    shard = {n: wmv[0][0].astype(BF) for n, wmv in given.items()}

    ada_cols = N_MOD * D // NDEV
    c_all, taps_all, mod_all, (wg1, wu1) = _prologue(
        "prologue", jnp.pad(c, ((0, SUBLANES - 1), (0, 0))), jnp.pad(conv_w[0], ((0, HALO - KW), (0, 0))), ada_w[0],
        lax.dynamic_slice(ada_b, (0, me * ada_cols), (1, ada_cols)), [shard["ffn1_w_gate"], shard["ffn1_w_up"]])
    conv_w_full = jnp.transpose(taps_all.reshape(NDEV, HALO, CHUNK), (1, 0, 2)).reshape(HALO, D)
    mod = lax.dynamic_index_in_dim(mod_all.reshape(NDEV, NDEV, ada_cols), me, axis=1, keepdims=False).reshape(N_MOD, 1, D)
    sh1, sc1, g1, sh2, sc2, g2, sh3, sc3, g3 = [mod[i] for i in range(N_MOD)]

    h1 = _norm_mod("ffn1_norm", x0, norm_ffn1, sc1, sh1)
    x1, h2, saved1, wd1, (w_in,) = _ffn_fwd("ffn1", x0, h1, g1, wg1, wu1, shard["ffn1_w_down"],
                                             _gather_rider([shard["mix_w_in"]]), next_norm=(norm_mix, sc2, sh2))
    proj, (wg2, wa3, wb3) = _mm_nn_blocked(
        "mix_in", h2, w_in, tm, rider=_gather_rider([shard["ffn2_w_gate"], shard["w_branch_a"], shard["w_branch_b"]]))
    bias_full = jnp.repeat(sgu_b_s[0].T, CHUNK, axis=1)
    (ua,) = _sgu("sgu", proj, mix_b_in, sgu_ln_g, sgu_ln_b, sgu_w_s[0], bias_full)
    (z0, z1, z3), (wu2, wo3) = _conv("conv", proj, mix_b_in, conv_w_full, conv_b, conv_ln_g, conv_ln_b,
                                     rider=_gather_rider([shard["ffn2_w_up"], shard["w_out"]]))
    wa, wb = wa3.reshape(D, D), wb3.reshape(D, D)
    ts = min(t, 512)
    ya = _mm_nn("branch_a", ua, wa, ts, D, D)[0]
    yb = _mm_nn("branch_b", z3, wb, ts, D, D)[0]
    (merged,) = _merge("merge", proj, mix_b_in, ya, yb)
    wo = wo3.reshape(D, D)

    def mix_epilogue(yv, xv, gv, gain, sc, sh):
        x_out = xv + gv * yv
        return x_out, yv, _rms_mod(x_out, gain, sc, sh)

    tmo = min(t, 512)
    row = pl.BlockSpec((1, D), lambda i, j, k: (0, 0))
    x2, y, h3 = _mm_nn("mix_out", merged, wo, tmo, D, D, extras=(x1, g2, norm_ffn2, sc3, sh3),
                       extra_specs=(pl.BlockSpec((tmo, D), lambda i, j, k: (i, 0)), row, row, row, row),
                       epilogue=mix_epilogue, out_dtypes=(F32, BF, BF))
    x3, _, saved3, wd2, _ = _ffn_fwd("ffn2", x2, h3, g3, wg2, wu2, shard["ffn2_w_down"], None)

    norm_final2 = norm_final.reshape(1, D)
    dx3, df3, d_norm_final, loss_row, dg3 = _loss_head("loss_head", x3, norm_final2, target, saved3[-1], g3, 0.5)
    (dx2, dy, d_norm_ffn2, dsc3, dsh3, dg2), down2, sum_g2, sum_u2, _, _, _ = _ffn_bwd(
        "ffn2", dx3, df3, saved3, norm_ffn2, sh3, sc3, wg2, wu2, wd2, slots, below=(y, g2, 1.0))
    dm = _mm_nt("mix_out_bwd", dy, wo, ts, D)[0]
    dwo = _mm_tn("mix_dwo", merged, dy, 512, D).reshape(NDEV, D // NDEV, D)
    (dproj, dya, dyb, db_ga, db_gb), (got_g2_near,) = _merge_bwd("merge_bwd", proj, mix_b_in, ya, yb, dm,
                                                                 rider=_chip_rider([sum_g2], NEIGHBOURS))
    dua = _mm_nt("branch_a_bwd", dya, wa, ts, D)[0]
    dwa = _mm_tn("branch_dwa", ua, dya, 512, D).reshape(NDEV, D // NDEV, D)
    dz3 = _mm_nt("branch_b_bwd", dyb, wb, ts, D)[0]
    dwb = _mm_tn("branch_dwb", z3, dyb, 512, D).reshape(NDEV, D // NDEV, D)
    (dproj, db_u, db_v, d_sgu_g, d_sgu_b, d_ws, d_bs_t), (*sib_abo, got_g2_far) = _sgu_bwd(
        "sgu_bwd", proj, mix_b_in, sgu_ln_g, sgu_ln_b, sgu_w_s[0], bias_full, dua, dproj,
        rider=[_pair_rider([dwa, dwb, dwo]), _chip_rider([sum_g2], DIAGONAL)])
    sum_a, sum_b, sum_o = _pair_add("mix_dw_add", [dwa, dwb, dwo], sib_abo, slots)
    (dproj, db_cv, db_cg, d_cw, d_cb, d_cln_g, d_cln_b), (got_u2,) = _conv_bwd(
        "conv_bwd", proj, mix_b_in, conv_w_full, conv_ln_g, conv_ln_b, z0, z1, dz3, dproj, rider=_chip_rider([sum_u2]))
    dwin, (got_a, got_b, got_o) = _mm_tn_blocked("mix_dwin", h2, dproj, rider=_chip_rider([sum_a, sum_b, sum_o]))

    d_bs = jnp.transpose(d_bs_t[:, :HEADS])
    zero = jnp.zeros((1, D), F32)
    pack_rows = [zero, zero, zero, zero, zero, dg2, dsh3, dsc3, dg3,
                 zero, zero, d_norm_ffn2, d_norm_final,
                 db_u, db_v, db_cv, db_cg, db_ga, db_gb,
                 d_sgu_g, d_sgu_b, d_bs.reshape(1, D), d_cb, d_cln_g, d_cln_b,
                 d_cw[:KW], loss_row, jnp.zeros((R_TOTAL - R_LOSS - 1, D), F32)]
    packed = jnp.concatenate(pack_rows, axis=0)
    d_ws2 = d_ws.reshape(HEADS * CHUNK, CHUNK)
    dh2, (sib_in, packed_all, dws_all) = _mm_nt_blocked("mix_in_bwd", [dproj], [w_in], tm,
                                                        rider=[_pair_rider([dwin]), _gather_rider([packed, d_ws2])])
    (sum_in,) = _pair_add("mix_dwin_add", [dwin], [sib_in], slots)
    dx1, df1, d_norm_mix, dsc2, dsh2, dg1 = _norm_mod_bwd("mix_norm_bwd", x1, norm_mix, sc2, sh2, dh2, dx2,
                                                          below=(saved1[-1], g1, 0.5))
    (dx0, d_norm_ffn1, dsc1, dsh1), gate1, (sum_u1, got_u1), sum_d1, (got_in_near,), (got_in_far,) = _ffn_bwd_last(
        "ffn1", dx1, df1, saved1, norm_ffn1, sh1, sc1, wg1, wu1, wd1, slots,
        dact_rider=_chip_rider([sum_in], NEIGHBOURS), dwgu_rider=_chip_rider([sum_in], DIAGONAL))
    packed_late = jnp.concatenate([dsh1, dsc1, dg1, dsh2, dsc2, jnp.zeros((4, D), F32), d_norm_ffn1, d_norm_mix,
                                   jnp.zeros((R_LATE - 11, D), F32)], axis=0)
    grads = dict(ffn2_w_gate=(sum_g2, [got_g2_near, got_g2_far]), ffn2_w_up=(sum_u2, [got_u2]), ffn2_w_down=down2,
                 mix_w_in=(sum_in, [got_in_near, got_in_far]), w_branch_a=(sum_a, [got_a]), w_branch_b=(sum_b, [got_b]),
                 w_out=(sum_o, [got_o]), ffn1_w_gate=gate1)
    done, (late_all, got_u1_far, got_d1) = _adamw_group(
        "adamw_most", [(cs, got, *[a[0] for a in given[n]]) for n, (cs, got) in grads.items()],
        rider=[_gather_rider([packed_late]), _chip_rider([sum_u1], DIAGONAL), _chip_rider([sum_d1])])
    last, _ = _adamw_group("adamw_last", [(sum_u1, [*got_u1, got_u1_far], *[a[0] for a in given["ffn1_w_up"]]),
                                         (sum_d1, [got_d1], *[a[0] for a in given["ffn1_w_down"]])], rows=256)
    big_out = {n: [o.reshape(given[n][0].shape) for o in outs]
               for n, outs in zip([*grads, "ffn1_w_up", "ffn1_w_down"], [*done, *last])}

    flat = lambda a: a.reshape(1, -1)
    vectors = [("ada_b", 0, 9, ada_b, m_ada_b, v_ada_b), ("norm_ffn1", 9, 1, norm_ffn1, m_norm_ffn1, v_norm_ffn1),
               ("norm_mix", 10, 1, norm_mix, m_norm_mix, v_norm_mix), ("norm_ffn2", 11, 1, norm_ffn2, m_norm_ffn2, v_norm_ffn2),
               ("norm_final", 12, 1, norm_final, m_norm_final, v_norm_final), ("mix_b_in", 13, 6, mix_b_in, m_mix_b_in, v_mix_b_in),
               ("sgu_ln_g", 19, 1, sgu_ln_g, m_sgu_ln_g, v_sgu_ln_g), ("sgu_ln_b", 20, 1, sgu_ln_b, m_sgu_ln_b, v_sgu_ln_b),
               ("sgu_b_s", 21, 1, sgu_b_s, m_sgu_b_s, v_sgu_b_s), ("conv_b", 22, 1, conv_b, m_conv_b, v_conv_b),
               ("conv_ln_g", 23, 1, conv_ln_g, m_conv_ln_g, v_conv_ln_g), ("conv_ln_b", 24, 1, conv_ln_b, m_conv_ln_b, v_conv_ln_b)]
    small_out, d_cw_all, loss_sum = _adamw_small(
        "adamw_small", packed_all, late_all, dws_all, [(row, rows, flat(wv), flat(mv), flat(vv)) for _, row, rows, wv, mv, vv in vectors],
        [a.reshape(HEADS * CHUNK, CHUNK) for a in (sgu_w_s, m_sgu_w_s, v_sgu_w_s)])
    small = {n: [o.reshape(wv.shape) for o in outs] for (n, _, _, wv, _, _), outs in zip(vectors, small_out)}
    small["sgu_w_s"] = [o.reshape(sgu_w_s.shape) for o in small_out[-1]]
    g_cw = lax.dynamic_slice(d_cw_all, (0, me * CHUNK), (KW, CHUNK))
    small["conv_w"] = [o.reshape(conv_w.shape) for o in (g_cw, *_adamw_plain("adamw_conv_w", g_cw, conv_w[0], m_conv_w[0], v_conv_w[0]))]
    loss = loss_sum[0, 0]

    dmod_cols = [lax.dynamic_slice(a[:, :N_MOD, :].reshape(NDEV, N_MOD * D), (0, me * ada_cols), (NDEV, ada_cols))
                 for a in (packed_all, late_all)]
    ada_out = [o.reshape(ada_w.shape) for o in _adamw_ada("adamw_ada_w", jnp.transpose(c_all), *dmod_cols, ada_w[0], m_ada_w[0], v_ada_w[0])]

    order = ["ada_w", "ada_b", "norm_ffn1", "ffn1_w_gate", "ffn1_w_up", "ffn1_w_down", "norm_mix", "mix_w_in", "mix_b_in",
             "sgu_ln_g", "sgu_ln_b", "sgu_w_s", "sgu_b_s", "conv_w", "conv_b", "conv_ln_g", "conv_ln_b", "w_branch_a",
             "w_branch_b", "w_out", "norm_ffn2", "ffn2_w_gate", "ffn2_w_up", "ffn2_w_down", "norm_final"]

    def leaf(n, kind):
        if n == "ada_w":
            return ada_out[kind]
        if n in big_out:
            return big_out[n][kind]
        return small[n][kind]

    return (loss, dx0.reshape(x.shape), *[leaf(n, kind) for kind in range(4) for n in order])
```

```python
import jax
import jax.numpy as jnp
from jax import lax
from jax.experimental import pallas as pl
from jax.experimental.pallas import tpu as pltpu

D = 1024
F = 4 * D
D_IN = 6 * D
HEADS = 8
CHUNK = 128
KW = 31
HALO = 32
N_MOD = 9
NDEV = 8
N_CHIPS = 4
EPS = 1e-6
LR, B1, B2, ADAM_EPS, WD, STEP = 0.001, 0.9, 0.999, 1e-08, 0.01, 10
BC1 = 1.0 - B1 ** STEP
BC2 = 1.0 - B2 ** STEP
VMEM_LIMIT = 56 * 1024 * 1024
MESH = pl.DeviceIdType.MESH
HBM = pl.BlockSpec(memory_space=pltpu.HBM)
VMEM = pl.BlockSpec(memory_space=pltpu.VMEM)
BF = jnp.bfloat16
F32 = jnp.float32

NN = (((1,), (0,)), ((), ()))
NT = (((1,), (1,)), ((), ()))
TN = (((0,), (0,)), ((), ()))

R_CW, R_LOSS, R_TOTAL = 25, 56, 64
R_LATE = 16


def _params(sem):
    return pltpu.CompilerParams(dimension_semantics=sem, vmem_limit_bytes=VMEM_LIMIT)


def _position():
    return lax.axis_index("x"), lax.axis_index("y"), lax.axis_index("c")


def _flip(pos, k):
    x, y, c = pos
    return (x ^ (k >> 2 & 1), y ^ (k >> 1 & 1), c ^ (k & 1))


def _index(pos):
    return 4 * pos[0] + 2 * pos[1] + pos[2]


def _gather_rows(x_ref, out_ref, send_sems, recv_sems, local_sem):
    m_per = x_ref.shape[0]
    x, y, c = _position()
    me, sibling = (x, y, c), (x, y, 1 - c)
    chips = [(1 - x, y), (x, 1 - y), (1 - x, 1 - y)]

    def rows(pos):
        return out_ref.at[pl.ds(_index(pos) * m_per, m_per), :]

    def copy(k, block, to, src=None):
        return pltpu.make_async_remote_copy(
            src_ref=rows(block) if src is None else src, dst_ref=rows(block),
            send_sem=send_sems.at[k], recv_sem=recv_sems.at[k], device_id=to, device_id_type=MESH)

    mine = pltpu.make_async_copy(x_ref, rows(me), local_sem)
    mine.start()
    first = [copy(0, me, sibling, src=x_ref)]
    first += [copy(1 + j, me, (*chip, c), src=x_ref) for j, chip in enumerate(chips)]
    for cp in first:
        cp.start()
    passed = [copy(4 + j, (*chip, c), sibling) for j, chip in enumerate(chips)]
    for j, chip in enumerate(chips):
        copy(1 + j, (*chip, c), me).wait_recv()
        passed[j].start()
    copy(0, sibling, me).wait_recv()
    for j, chip in enumerate(chips):
        copy(4 + j, (*chip, 1 - c), me).wait_recv()
    for cp in first + passed:
        cp.wait_send()
    mine.wait()


def _prologue(name, c_rows, taps, ada_w, ada_b, shards):
    rider = _gather_rider(shards)
    n = len(shards)
    nc = ada_w.shape[1]

    def body(*refs):
        c_ref, taps_ref, w_ref, b_ref = refs[:4]
        shard_refs = refs[4:4 + n]
        c_all_ref, taps_all_ref, mod_all_ref = refs[4 + n:7 + n]
        gathered_refs = refs[7 + n:7 + 2 * n]
        c_buf, mod_part, sems = refs[7 + 2 * n], refs[8 + 2 * n], refs[9 + 2 * n:]
        rider.start(shard_refs, gathered_refs, sems[9:])
        _gather_rows(c_ref, c_buf, *sems[0:3])
        c_all = jnp.concatenate([c_buf[pl.ds(d * SUBLANES, 1), :] for d in range(NDEV)], axis=0)
        c_all_ref[...] = c_all
        mod_part[...] = jnp.dot(_silu(c_all), w_ref[...], preferred_element_type=F32) + b_ref[...]
        _gather_rows(taps_ref, taps_all_ref, *sems[3:6])
        _gather_rows(mod_part, mod_all_ref, *sems[6:9])
        rider.mid(shard_refs, gathered_refs, sems[9:])
        rider.relay(shard_refs, gathered_refs, sems[9:])
        rider.finish(shard_refs, gathered_refs, sems[9:])

    small_sems = [pltpu.SemaphoreType.DMA((7,)), pltpu.SemaphoreType.DMA((7,)), pltpu.SemaphoreType.DMA] * 3
    res = pl.pallas_call(
        body, name=name,
        out_shape=[jax.ShapeDtypeStruct((NDEV, D), F32), jax.ShapeDtypeStruct((NDEV * taps.shape[0], taps.shape[1]), F32),
                   jax.ShapeDtypeStruct((NDEV * NDEV, nc), F32)] + rider.out_shapes,
        in_specs=[VMEM] * 4 + [HBM] * n, out_specs=[VMEM] * 3 + [HBM] * n,
        scratch_shapes=[pltpu.VMEM((NDEV * SUBLANES, D), F32), pltpu.VMEM((NDEV, nc), F32)] + small_sems + rider.sems,
        compiler_params=_params(None),
    )(c_rows, taps, ada_w, ada_b, *shards)
    return res[0], res[1], res[2], res[3:]


class _Rider:
    def __init__(self, ins, out_shapes, sems, start, finish, mid=None, relay=None):
        self.ins, self.out_shapes, self.sems = list(ins), list(out_shapes), list(sems)
        self.start, self.finish, self.mid, self.relay = start, finish, mid, relay


def _gather_rider(shards):
    n = len(shards)

    def setup(ins, outs, sems):
        send_sems, recv_sems, local_sems = sems
        x, y, c = _position()
        places = dict(me=(x, y, c), sibling=(x, y, 1 - c), xn=(1 - x, y, c), yn=(x, 1 - y, c), diagonal=(1 - x, 1 - y, c),
                      passed_on=(x ^ c, y ^ (1 - c), c), passed_to=(x ^ (1 - c), y ^ c, c))

        def copy(a, k, block, to, own=False):
            slot = outs[a].at[_index(block)]
            return pltpu.make_async_remote_copy(
                src_ref=ins[a] if own else slot, dst_ref=slot,
                send_sem=send_sems.at[k, a], recv_sem=recv_sems.at[k, a], device_id=to, device_id_type=MESH)

        def local(a):
            return pltpu.make_async_copy(ins[a], outs[a].at[_index(places["me"])], local_sems.at[a])

        return places, copy, local

    def start(ins, outs, sems):
        p, copy, local = setup(ins, outs, sems)
        for a in range(n):
            local(a).start()
            for k, to in enumerate(("sibling", "xn", "yn")):
                copy(a, k, p["me"], p[to], own=True).start()

    def mid(ins, outs, sems):
        p, copy, _ = setup(ins, outs, sems)
        for a in range(n):
            copy(a, 1, p["xn"], p["me"]).wait_recv()
            copy(a, 2, p["yn"], p["me"]).wait_recv()
            copy(a, 3, p["passed_on"], p["passed_to"]).start()
            copy(a, 4, p["xn"], p["sibling"]).start()
            copy(a, 5, p["yn"], p["sibling"]).start()

    def relay(ins, outs, sems):
        p, copy, _ = setup(ins, outs, sems)
        for a in range(n):
            copy(a, 3, p["diagonal"], p["me"]).wait_recv()
            copy(a, 6, p["diagonal"], p["sibling"]).start()

    def finish(ins, outs, sems):
        p, copy, local = setup(ins, outs, sems)
        x, y, c = p["me"]
        for a in range(n):
            for k, block in ((0, (x, y, 1 - c)), (4, (1 - x, y, 1 - c)), (5, (x, 1 - y, 1 - c)), (6, (1 - x, 1 - y, 1 - c))):
                copy(a, k, block, p["me"]).wait_recv()
            for k, to in enumerate(("sibling", "xn", "yn")):
                copy(a, k, p["me"], p[to], own=True).wait_send()
            copy(a, 3, p["passed_on"], p["passed_to"]).wait_send()
            for k, block in ((4, "xn"), (5, "yn"), (6, "diagonal")):
                copy(a, k, p[block], p["sibling"]).wait_send()
            local(a).wait()

    return _Rider(shards, [jax.ShapeDtypeStruct((NDEV, *s.shape), s.dtype) for s in shards],
                  [pltpu.SemaphoreType.DMA((7, n)), pltpu.SemaphoreType.DMA((7, n)), pltpu.SemaphoreType.DMA((n,))],
                  start, finish, mid, relay)


def _pair_rider(parts):
    n = len(parts)

    def copies(ins, outs, sems):
        send_sems, recv_sems = sems
        x, y, c = _position()
        q = 2 * x + y
        return [pltpu.make_async_remote_copy(
            src_ref=ins[a].at[2 * (q ^ k) + (1 - c)], dst_ref=outs[a].at[k],
            send_sem=send_sems.at[k, a], recv_sem=recv_sems.at[k, a], device_id=(x, y, 1 - c), device_id_type=MESH)
            for a in range(n) for k in range(N_CHIPS)]

    def start(ins, outs, sems):
        for cp in copies(ins, outs, sems):
            cp.start()

    def finish(ins, outs, sems):
        for cp in copies(ins, outs, sems):
            cp.wait()

    return _Rider(parts, [jax.ShapeDtypeStruct((N_CHIPS, *p.shape[1:]), p.dtype) for p in parts],
                  [pltpu.SemaphoreType.DMA((N_CHIPS, n)), pltpu.SemaphoreType.DMA((N_CHIPS, n))], start, finish)


NEIGHBOURS = (1, 2)
DIAGONAL = (3,)
OTHER_CHIPS = NEIGHBOURS + DIAGONAL


def _chip_rider(sums, ks=OTHER_CHIPS):
    n = len(sums)

    def copies(ins, outs, sems):
        send_sems, recv_sems = sems
        me = _position()
        return [pltpu.make_async_remote_copy(
            src_ref=ins[a].at[k], dst_ref=outs[a].at[j],
            send_sem=send_sems.at[j, a], recv_sem=recv_sems.at[j, a], device_id=_flip(me, 2 * k), device_id_type=MESH)
            for a in range(n) for j, k in enumerate(ks)]

    def start(ins, outs, sems):
        for cp in copies(ins, outs, sems):
            cp.start()

    def finish(ins, outs, sems):
        for cp in copies(ins, outs, sems):
            cp.wait()

    return _Rider(sums, [jax.ShapeDtypeStruct((len(ks), *s.shape[1:]), s.dtype) for s in sums],
                  [pltpu.SemaphoreType.DMA((len(ks), n)), pltpu.SemaphoreType.DMA((len(ks), n))], start, finish)


def _grid_edge(grid, last):
    cond = None
    for d, n in enumerate(grid):
        here = pl.program_id(d) == (n - 1 if last else 0)
        cond = here if cond is None else jnp.logical_and(cond, here)
    return cond


def _call(name, compute, grid, ins, in_specs, out_shapes, out_specs, scratch_shapes, semantics, rider=None, aliases=None):
    riders = [rider] if isinstance(rider, _Rider) else list(rider or [])
    n_in, n_out, n_scr = len(ins), len(out_shapes), len(scratch_shapes)
    n_rin, n_rout, n_rsem = [sum(len(part(r)) for r in riders) for part in (lambda r: r.ins, lambda r: r.out_shapes, lambda r: r.sems)]
    cuts = [0, n_in, n_in + n_rin, n_in + n_rin + n_out, n_in + n_rin + n_out + n_rout, n_in + n_rin + n_out + n_rout + n_scr]

    def body(*refs):
        in_refs, rin_refs, out_refs, rout_refs, scr_refs = [refs[a:b] for a, b in zip(cuts[:-1], cuts[1:])]
        rsem_refs = refs[cuts[-1]:]
        mine, at = [], [0, 0, 0]
        for r in riders:
            mine.append((r, rin_refs[at[0]:at[0] + len(r.ins)], rout_refs[at[1]:at[1] + len(r.out_shapes)],
                         rsem_refs[at[2]:at[2] + len(r.sems)]))
            at = [at[0] + len(r.ins), at[1] + len(r.out_shapes), at[2] + len(r.sems)]
        if riders:
            @pl.when(_grid_edge(grid, last=False))
            def _():
                for r, a, b, c in mine:
                    r.start(a, b, c)

        if any(r.mid for r in riders):
            step, steps = 0, 1
            for d, size in enumerate(grid):
                step, steps = step * size + pl.program_id(d), steps * size

            @pl.when(step == steps * 5 // 8)
            def _():
                for r, a, b, c in mine:
                    if r.mid:
                        r.mid(a, b, c)

        if any(r.relay for r in riders):
            @pl.when(_grid_edge(grid, last=True))
            def _():
                for r, a, b, c in mine:
                    if r.relay:
                        r.relay(a, b, c)

        compute(in_refs, out_refs, scr_refs)
        if riders:
            @pl.when(_grid_edge(grid, last=True))
            def _():
                for r, a, b, c in mine:
                    r.finish(a, b, c)

    res = pl.pallas_call(
        body, name=name, grid=grid,
        out_shape=list(out_shapes) + [s for r in riders for s in r.out_shapes],
        in_specs=list(in_specs) + [HBM] * n_rin, out_specs=list(out_specs) + [HBM] * n_rout,
        scratch_shapes=list(scratch_shapes) + [s for r in riders for s in r.sems],
        input_output_aliases=aliases or {}, compiler_params=_params(semantics),
    )(*ins, *[a for r in riders for a in r.ins])
    return (res[:n_out], res[n_out:]) if riders else res


def _pair_add(name, parts, from_sibling, slots):
    n = len(parts)

    def body(s_ref, *refs):
        for a in range(n):
            refs[2 * n + a][...] = (refs[a][...].astype(F32) + refs[n + a][...].astype(F32)).astype(refs[2 * n + a].dtype)

    def slab(p, picked):
        _, r, c = p.shape
        return pl.BlockSpec((None, r, c), (lambda k, s: (s[k], 0, 0)) if picked else (lambda k, s: (k, 0, 0)))

    return pl.pallas_call(
        body, name=name,
        grid_spec=pltpu.PrefetchScalarGridSpec(
            num_scalar_prefetch=1, grid=(N_CHIPS,),
            in_specs=[slab(p, True) for p in parts] + [slab(p, False) for p in parts],
            out_specs=[slab(p, False) for p in parts]),
        out_shape=[jax.ShapeDtypeStruct((N_CHIPS, *p.shape[1:]), p.dtype) for p in parts],
        compiler_params=_params(("arbitrary",)),
    )(slots, *parts, *from_sibling)


def _mm(name, pairs, dims, grid, nk, out_shapes, out_specs, extras=(), extra_specs=(), epilogue=None, acc_shape=None, rider=None):
    n_pairs = len(pairs)

    def compute(ins, outs, scratch):
        def partial_sum():
            total = None
            for p in range(n_pairs):
                d = lax.dot_general(ins[2 * p][...], ins[2 * p + 1][...], dims, preferred_element_type=F32)
                total = d if total is None else total + d
            return total

        def finish(r):
            ex = [e[...] for e in ins[2 * n_pairs:]]
            res = epilogue(r, *ex) if epilogue is not None else (r,)
            for o, v in zip(outs, res):
                o[...] = v.astype(o.dtype)

        if nk == 1:
            finish(partial_sum())
        else:
            acc = scratch[0]
            k = pl.program_id(2)

            @pl.when(k == 0)
            def _():
                acc[...] = partial_sum()

            @pl.when(k > 0)
            def _():
                acc[...] += partial_sum()

            @pl.when(k == nk - 1)
            def _():
                finish(acc[...])

    operands, specs = [], []
    for a, a_spec, b, b_spec in pairs:
        operands += [a, b]
        specs += [a_spec, b_spec]
    return _call(name, compute, grid, operands + list(extras), specs + list(extra_specs), out_shapes, out_specs,
                 [pltpu.VMEM(acc_shape, F32)] if nk > 1 else [], ("parallel", "parallel", "arbitrary"), rider)


def _single(res, rider):
    return (res[0][0], res[1]) if rider else res[0]


def _silu(x):
    return x * jax.nn.sigmoid(x)


def _ffn_up(name, h, wg, wu, rider=None):
    t = h.shape[0]
    tm = min(t, 1024)
    nb = F // NDEV

    def compute(ins, outs, _):
        hv = ins[0][...]
        g = jnp.dot(hv, ins[1][...], preferred_element_type=F32)
        u = jnp.dot(hv, ins[2][...], preferred_element_type=F32)
        outs[0][...] = g.astype(BF)
        outs[1][...] = u.astype(BF)
        outs[2][...] = (_silu(g) * u).astype(BF)

    w_spec = pl.BlockSpec((None, D, nb), lambda i, j: (j, 0, 0))
    o_spec = pl.BlockSpec((tm, nb), lambda i, j: (i, j))
    return _call(name, compute, (t // tm, NDEV), [h, wg, wu], [pl.BlockSpec((tm, D), lambda i, j: (i, 0)), w_spec, w_spec],
                 [jax.ShapeDtypeStruct((t, F), BF)] * 3, [o_spec] * 3, [], ("parallel", "arbitrary"), rider)


def _mm_nn(name, a, b, tm, tn, tk, extras=(), extra_specs=(), epilogue=None, out_dtypes=(F32,), rider=None):
    m, kk = a.shape
    n = b.shape[1]
    nk = kk // tk
    return _mm(
        name, [(a, pl.BlockSpec((tm, tk), lambda i, j, k: (i, k)), b, pl.BlockSpec((tk, tn), lambda i, j, k: (k, j)))], NN,
        (m // tm, n // tn, nk), nk,
        [jax.ShapeDtypeStruct((m, n), dt) for dt in out_dtypes],
        [pl.BlockSpec((tm, tn), lambda i, j, k: (i, j))] * len(out_dtypes),
        extras, extra_specs, epilogue, (tm, tn), rider)


def _mm_nn_blocked(name, a, b3, tm, rider=None):
    m = a.shape[0]
    nb = b3.shape[2]
    return _single(_mm(
        name, [(a, pl.BlockSpec((tm, D), lambda i, j, k: (i, 0)), b3, pl.BlockSpec((None, D, nb), lambda i, j, k: (j, 0, 0)))], NN,
        (m // tm, NDEV, 1), 1,
        [jax.ShapeDtypeStruct((m, NDEV * nb), F32)], [pl.BlockSpec((tm, nb), lambda i, j, k: (i, j))], rider=rider), rider)


def _mm_nt(name, a, b, tm, tn, out_dtypes=(F32,), extras=(), extra_specs=(), epilogue=None, rider=None):
    m, kk = a.shape
    n = b.shape[0]
    return _mm(
        name, [(a, pl.BlockSpec((tm, kk), lambda i, j, k: (i, 0)), b, pl.BlockSpec((tn, kk), lambda i, j, k: (j, 0)))], NT,
        (m // tm, n // tn, 1), 1,
        [jax.ShapeDtypeStruct((m, n), dt) for dt in out_dtypes],
        [pl.BlockSpec((tm, tn), lambda i, j, k: (i, j))] * len(out_dtypes),
        extras, extra_specs, epilogue, rider=rider)


def _mm_nt_blocked(name, a_list, b3_list, tm, rider=None):
    m = a_list[0].shape[0]
    nb = b3_list[0].shape[2]
    pairs = [(a, pl.BlockSpec((tm, nb), lambda i, j, k: (i, k)), b3, pl.BlockSpec((None, D, nb), lambda i, j, k: (k, 0, 0)))
             for a, b3 in zip(a_list, b3_list)]
    return _single(_mm(name, pairs, NT, (m // tm, 1, NDEV), NDEV,
                       [jax.ShapeDtypeStruct((m, D), F32)], [pl.BlockSpec((tm, D), lambda i, j, k: (i, 0))],
                       acc_shape=(tm, D), rider=rider), rider)


def _mm_tn(name, a, b, tm, tn, rider=None):
    t, m = a.shape
    n = b.shape[1]
    return _single(_mm(
        name, [(a, pl.BlockSpec((t, tm), lambda i, j, k: (0, i)), b, pl.BlockSpec((t, tn), lambda i, j, k: (0, j)))], TN,
        (m // tm, n // tn, 1), 1,
        [jax.ShapeDtypeStruct((m, n), BF)], [pl.BlockSpec((tm, tn), lambda i, j, k: (i, j))], rider=rider), rider)


def _mm_tn_blocked(name, a, b, rider=None):
    t = a.shape[0]
    nb = b.shape[1] // NDEV
    return _single(_mm(
        name, [(a, pl.BlockSpec((t, D), lambda i, j, k: (0, 0)), b, pl.BlockSpec((t, nb), lambda i, j, k: (0, j)))], TN,
        (1, NDEV, 1), 1,
        [jax.ShapeDtypeStruct((NDEV, D, nb), BF)], [pl.BlockSpec((None, D, nb), lambda i, j, k: (j, 0, 0))], rider=rider), rider)


def _dw_gate_up(name, h, dgate, dup, rider=None):
    t = h.shape[0]
    nb = F // NDEV

    def compute(ins, outs, _):
        hv = ins[0][...]
        outs[0][...] = lax.dot_general(hv, ins[1][...], TN, preferred_element_type=F32).astype(BF)
        outs[1][...] = lax.dot_general(hv, ins[2][...], TN, preferred_element_type=F32).astype(BF)

    d_spec = pl.BlockSpec((t, nb), lambda j: (0, j))
    o_spec = pl.BlockSpec((None, D, nb), lambda j: (j, 0, 0))
    return _call(name, compute, (NDEV,), [h, dgate, dup], [pl.BlockSpec((t, D), lambda j: (0, 0)), d_spec, d_spec],
                 [jax.ShapeDtypeStruct((NDEV, D, nb), BF)] * 2, [o_spec] * 2, [], ("arbitrary",), rider)


def _swiglu_bwd(da, gate, up):
    gate = gate.astype(F32)
    s = jax.nn.sigmoid(gate)
    return da * up.astype(F32) * (s * (1.0 + gate * (1.0 - s))), da * (gate * s)


def _ffn_dact_dh(name, df, wd, gate, up, wg, wu, rider=None):
    t = df.shape[0]
    tm = min(t, 1024)
    nb = F // NDEV

    def compute(ins, outs, scr):
        acc = scr[0]
        j = pl.program_id(1)
        da = lax.dot_general(ins[0][...], ins[1][...], NT, preferred_element_type=F32)
        dgate, dup = _swiglu_bwd(da, ins[2][...], ins[3][...])
        dgate, dup = dgate.astype(BF), dup.astype(BF)
        outs[0][...] = dgate
        outs[1][...] = dup
        part = (lax.dot_general(dgate, ins[4][...], NT, preferred_element_type=F32)
                + lax.dot_general(dup, ins[5][...], NT, preferred_element_type=F32))

        @pl.when(j == 0)
        def _():
            acc[...] = part

        @pl.when(j > 0)
        def _():
            acc[...] += part

        @pl.when(j == NDEV - 1)
        def _():
            outs[2][...] = acc[...]

    blk = pl.BlockSpec((tm, nb), lambda i, j: (i, j))
    w3 = pl.BlockSpec((None, D, nb), lambda i, j: (j, 0, 0))
    row = pl.BlockSpec((tm, D), lambda i, j: (i, 0))
    return _call(name, compute, (t // tm, NDEV), [df, wd, gate, up, wg, wu],
                 [row, pl.BlockSpec((nb, D), lambda i, j: (j, 0)), blk, blk, w3, w3],
                 [jax.ShapeDtypeStruct((t, F), BF)] * 2 + [jax.ShapeDtypeStruct((t, D), F32)], [blk, blk, row],
                 [pltpu.VMEM((tm, D), F32)], ("parallel", "arbitrary"), rider)


def _rowcall(name, fn, ins, in_specs, n_row_out, out_shapes, out_specs, grid, scratch_shapes=(), rider=None, aliases=None):
    def accumulate(o, v, i):
        @pl.when(i == 0)
        def _():
            o[...] = v.astype(o.dtype)

        @pl.when(i > 0)
        def _():
            o[...] += v.astype(o.dtype)

    def compute(in_refs, out_refs, scr):
        i = pl.program_id(0)
        vals = fn(i, in_refs, scr)
        for idx, (o, v) in enumerate(zip(out_refs, vals)):
            if idx < n_row_out:
                o[...] = v.astype(o.dtype)
            else:
                accumulate(o, v, i)

    return _call(name, compute, (grid,), ins, in_specs, out_shapes, out_specs, list(scratch_shapes), ("arbitrary",), rider, aliases)


def _rows(tr, w=D, cb=0):
    return pl.BlockSpec((tr, w), lambda i: (i, cb))


def _whole(shape):
    nd = len(shape)
    return pl.BlockSpec(shape, lambda i: (0,) * nd)


def _vec(n=1):
    return jax.ShapeDtypeStruct((n, D), F32)


def _rms_mod(x, gain, sc, sh):
    y = x * lax.rsqrt(jnp.mean(x * x, axis=-1, keepdims=True) + EPS)
    return (y * gain) * (1.0 + sc) + sh


def _layer_norm(x, g, b):
    mu = jnp.mean(x, axis=-1, keepdims=True)
    var = jnp.mean(jnp.square(x - mu), axis=-1, keepdims=True)
    return (x - mu) * lax.rsqrt(var + EPS) * g + b


def _norm_mod(name, x, gain, sc, sh):
    t = x.shape[0]
    tr = min(t, 256)

    def fn(i, r, _):
        return [_rms_mod(r[0][...], r[1][...], r[2][...], r[3][...])]

    return _rowcall(name, fn, [x, gain, sc, sh], [_rows(tr)] + [_whole((1, D))] * 3, 1,
                    [jax.ShapeDtypeStruct((t, D), BF)], [_rows(tr)], t // tr)[0]


def _gate_grads(dx, f, g, scale):
    return scale * g * dx, jnp.sum(scale * dx * f.astype(F32), axis=0, keepdims=True)


def _norm_mod_bwd(name, x, gain, sc, sh, dh, dres, below=None, rider=None):
    t = x.shape[0]
    tr = min(t, 256)

    def fn(i, r, _):
        _, vjp = jax.vjp(_rms_mod, r[0][...], r[1][...], r[2][...], r[3][...])
        dx, dgain, dsc, dsh = vjp(r[4][...])
        dx = dx + r[5][...]
        if below is None:
            return [dx, dgain, dsc, dsh]
        df, dg = _gate_grads(dx, r[6][...], r[7][...], below[2])
        return [dx, df, dgain, dsc, dsh, dg]

    ins, specs = [x, gain, sc, sh, dh, dres], [_rows(tr)] + [_whole((1, D))] * 3 + [_rows(tr)] * 2
    outs, out_specs = [jax.ShapeDtypeStruct((t, D), F32)], [_rows(tr)]
    if below is not None:
        ins, specs = ins + [below[0], below[1]], specs + [_rows(tr), _whole((1, D))]
        outs, out_specs = outs + [jax.ShapeDtypeStruct((t, D), BF)], out_specs + [_rows(tr)]
    n_vec = 3 if below is None else 4
    return _rowcall(name, fn, ins, specs, len(outs), outs + [_vec()] * n_vec, out_specs + [_whole((1, D))] * n_vec, t // tr,
                    rider=rider)


def _sgu_pre(up, vp, bu, bv, ln_g, ln_b):
    return jax.nn.gelu(up + bu), _layer_norm(jax.nn.gelu(vp + bv), ln_g, ln_b)


def _causal(w_ref, h):
    rows = lax.broadcasted_iota(jnp.int32, (CHUNK, CHUNK), 0)
    cols = lax.broadcasted_iota(jnp.int32, (CHUNK, CHUNK), 1)
    return jnp.where(cols <= rows, w_ref[h], 0.0)


def _sgu(name, proj, b_in, ln_g, ln_b, w_s, bias_full, rider=None):
    t = proj.shape[0]

    def fn(i, r, _):
        u, v = _sgu_pre(r[0][...], r[1][...], r[2][...], r[3][...], r[4][...], r[5][...])
        vb = v.astype(BF)
        mixed = [jnp.dot(_causal(r[6], h).astype(BF), vb[:, h * CHUNK:(h + 1) * CHUNK], preferred_element_type=F32)
                 for h in range(HEADS)]
        return [u * (jnp.concatenate(mixed, axis=1) + r[7][...])]

    return _rowcall(
        name, fn, [proj, proj, b_in, b_in, ln_g, ln_b, w_s, bias_full],
        [_rows(CHUNK, D, 0), _rows(CHUNK, D, 1), pl.BlockSpec((1, D), lambda i: (0, 0)), pl.BlockSpec((1, D), lambda i: (0, 1)),
         _whole((1, D)), _whole((1, D)), _whole((HEADS, CHUNK, CHUNK)), _whole((CHUNK, D))],
        1, [jax.ShapeDtypeStruct((t, D), BF)], [_rows(CHUNK)], t // CHUNK, rider=rider)


def _sgu_bwd(name, proj, b_in, ln_g, ln_b, w_s, bias_full, dout, dproj, rider=None):
    t = proj.shape[0]

    def fn(i, r, _):
        (u, v), vjp = jax.vjp(_sgu_pre, r[0][...], r[1][...], r[2][...], r[3][...], r[4][...], r[5][...])
        vb = v.astype(BF)
        d = r[8][...]
        masks = [_causal(r[6], h).astype(BF) for h in range(HEADS)]
        cols = [slice(h * CHUNK, (h + 1) * CHUNK) for h in range(HEADS)]
        mixed = jnp.concatenate([jnp.dot(masks[h], vb[:, cols[h]], preferred_element_type=F32) for h in range(HEADS)], axis=1)
        du = d * (mixed + r[7][...])
        dmix = d * u
        dmb = dmix.astype(BF)
        dv = jnp.concatenate([lax.dot_general(masks[h], dmb[:, cols[h]], TN, preferred_element_type=F32) for h in range(HEADS)], axis=1)
        rows = lax.broadcasted_iota(jnp.int32, (CHUNK, CHUNK), 0)
        lanes = lax.broadcasted_iota(jnp.int32, (CHUNK, CHUNK), 1)
        dws = jnp.stack([jnp.where(lanes <= rows, lax.dot_general(dmb[:, cols[h]], vb[:, cols[h]], NT, preferred_element_type=F32), 0.0)
                         for h in range(HEADS)])
        dbs = jnp.zeros((CHUNK, CHUNK), F32)
        for h in range(HEADS):
            dbs = dbs + jnp.where(lanes == h, jnp.sum(dmix[:, cols[h]], axis=1, keepdims=True), 0.0)
        dup, dvp, dbu, dbv, dg, db = vjp((du, dv))
        return [jnp.concatenate([dup, dvp], axis=1), dbu, dbv, dg, db, dws, dbs]

    return _rowcall(
        name, fn, [proj, proj, b_in, b_in, ln_g, ln_b, w_s, bias_full, dout, dproj],
        [_rows(CHUNK, D, 0), _rows(CHUNK, D, 1), pl.BlockSpec((1, D), lambda i: (0, 0)), pl.BlockSpec((1, D), lambda i: (0, 1)),
         _whole((1, D)), _whole((1, D)), _whole((HEADS, CHUNK, CHUNK)), _whole((CHUNK, D)), _rows(CHUNK),
         pl.BlockSpec(memory_space=pl.ANY)],
        1, [jax.ShapeDtypeStruct(dproj.shape, dproj.dtype)] + [_vec()] * 4
        + [jax.ShapeDtypeStruct((HEADS, CHUNK, CHUNK), F32), jax.ShapeDtypeStruct((CHUNK, CHUNK), F32)],
        [pl.BlockSpec((CHUNK, 2 * D), lambda i: (i, 0))] + [_whole((1, D))] * 4 + [_whole((HEADS, CHUNK, CHUNK)), _whole((CHUNK, CHUNK))],
        t // CHUNK, rider=rider, aliases={9: 0})


def _halo_before(tr, cb):
    return pl.BlockSpec((HALO, D), lambda i: (jnp.maximum(i * (tr // HALO) - 1, 0), cb))


def _halo_after(tr, cb, n_tiles):
    return pl.BlockSpec((HALO, D), lambda i: (jnp.minimum((i + 1) * (tr // HALO), n_tiles * (tr // HALO) - 1), cb))


def _ln_silu(z, g, b):
    return _silu(_layer_norm(z, g, b))


SUBLANES = 8
LANES = 128
CONV_STRIP = 16
DW_STRIP = 32


def _shifted_copies(buf, copies, rows):
    for b in range(1, SUBLANES):
        copies[b - 1, pl.ds(0, rows), :] = buf[pl.ds(b, rows), :]


def _shifted(buf, copies, offset, start, rows, lanes=slice(None)):
    at = pl.ds(pl.multiple_of(start + SUBLANES * (offset // SUBLANES), SUBLANES), rows)
    return buf[at, lanes] if offset % SUBLANES == 0 else copies[offset % SUBLANES - 1, at, lanes]


def _accumulate(o, v, i):
    @pl.when(i == 0)
    def _():
        o[...] = v.astype(o.dtype)

    @pl.when(i > 0)
    def _():
        o[...] += v.astype(o.dtype)


def _conv(name, proj, b_in, conv_w, conv_b, ln_g, ln_b, rider=None):
    t = proj.shape[0]
    tr = min(t, 256)

    def compute(r, outs, scr):
        zbuf, zs = scr
        i = pl.program_id(0)
        bv, bg = r[4][...], r[5][...]
        z0 = (r[0][...] + bv) * jax.nn.sigmoid(r[1][...] + bg)
        before = (r[2][...] + bv) * jax.nn.sigmoid(r[3][...] + bg)
        zbuf[pl.ds(0, HALO), :] = jnp.where(i > 0, before, 0.0)
        zbuf[pl.ds(HALO, tr), :] = z0
        outs[0][...] = z0
        _shifted_copies(zbuf, zs, tr + HALO - SUBLANES)

        def strip(s, carry):
            r0 = s * CONV_STRIP
            acc = jnp.zeros((CONV_STRIP, D), F32) + r[7][...]
            for k in range(KW):
                acc = acc + r[6][k:k + 1, :] * _shifted(zbuf, zs, HALO - (KW - 1) + k, r0, CONV_STRIP)
            outs[1][pl.ds(pl.multiple_of(r0, SUBLANES), CONV_STRIP), :] = acc
            return carry

        lax.fori_loop(0, tr // CONV_STRIP, strip, 0)
        outs[2][...] = _ln_silu(outs[1][...], r[8][...], r[9][...]).astype(BF)

    return _call(
        name, compute, (t // tr,), [proj, proj, proj, proj, b_in, b_in, conv_w, conv_b, ln_g, ln_b],
        [_rows(tr, D, 2), _rows(tr, D, 3), _halo_before(tr, 2), _halo_before(tr, 3),
         pl.BlockSpec((1, D), lambda i: (0, 2)), pl.BlockSpec((1, D), lambda i: (0, 3)),
         _whole((HALO, D)), _whole((1, D)), _whole((1, D)), _whole((1, D))],
        [jax.ShapeDtypeStruct((t, D), F32), jax.ShapeDtypeStruct((t, D), F32), jax.ShapeDtypeStruct((t, D), BF)],
        [_rows(tr)] * 3, [pltpu.VMEM((tr + HALO, D), F32), pltpu.VMEM((SUBLANES - 1, tr + HALO, D), F32)], ("arbitrary",), rider)


def _conv_bwd(name, proj, b_in, conv_w, ln_g, ln_b, z0, z1, dz3, dproj, rider=None):
    t = proj.shape[0]
    tr = min(t, 256)
    n_tiles = t // tr

    def compute(r, outs, scr):
        zbuf, dbuf, zs, ds, dwacc = scr
        i = pl.program_id(0)
        g, b = r[5][...], r[6][...]
        zero_row = jnp.zeros((1, D), F32)
        _, vjp = jax.vjp(_ln_silu, r[9][...], g, b)
        dz1, dg, db = vjp(r[11][...])
        dcb = jnp.sum(dz1, axis=0, keepdims=True)
        _, vjp_after = jax.vjp(_ln_silu, r[10][...], g, b)
        dz1_after = vjp_after(r[12][...])[0]
        dbuf[pl.ds(0, tr), :] = dz1
        dbuf[pl.ds(tr, HALO), :] = jnp.where(i < n_tiles - 1, dz1_after, 0.0)
        zbuf[pl.ds(0, HALO), :] = jnp.where(i > 0, r[8][...], 0.0)
        zbuf[pl.ds(HALO, tr), :] = r[7][...]
        _shifted_copies(dbuf, ds, tr + HALO - SUBLANES)
        _shifted_copies(zbuf, zs, tr + HALO - SUBLANES)

        def dz0_strip(s, carry):
            r0 = s * CONV_STRIP
            at = pl.ds(pl.multiple_of(r0, CONV_STRIP), CONV_STRIP)
            acc = jnp.zeros((CONV_STRIP, D), F32)
            for k in range(KW):
                acc = acc + r[4][k:k + 1, :] * _shifted(dbuf, ds, KW - 1 - k, r0, CONV_STRIP)
            a = r[0][at, :] + r[2][...]
            sg = jax.nn.sigmoid(r[1][at, :] + r[3][...])
            dcv = acc * sg
            dcg = acc * a * sg * (1.0 - sg)
            outs[0][at, :] = jnp.concatenate([dcv, dcg], axis=1).astype(BF)
            return carry[0] + jnp.sum(dcv, axis=0, keepdims=True), carry[1] + jnp.sum(dcg, axis=0, keepdims=True)

        dbv, dbg = lax.fori_loop(0, tr // CONV_STRIP, dz0_strip, (zero_row, zero_row))

        for lb in range(D // LANES):
            lanes = slice(lb * LANES, (lb + 1) * LANES)

            def dw_strip(s, accs, lanes=lanes):
                r0 = s * DW_STRIP
                dz = dbuf[pl.ds(pl.multiple_of(r0, SUBLANES), DW_STRIP), lanes]
                out = []
                for k in range(KW):
                    prod = dz * _shifted(zbuf, zs, HALO - (KW - 1) + k, r0, DW_STRIP, lanes)
                    part = prod[0:SUBLANES]
                    for q in range(1, DW_STRIP // SUBLANES):
                        part = part + prod[q * SUBLANES:(q + 1) * SUBLANES]
                    out.append(accs[k] + part)
                return tuple(out)

            accs = lax.fori_loop(0, tr // DW_STRIP, dw_strip, tuple(jnp.zeros((SUBLANES, LANES), F32) for _ in range(KW)))
            for k in range(KW):
                dwacc[pl.ds(k * SUBLANES, SUBLANES), lanes] = accs[k]
        dw_rows = [jnp.sum(dwacc[pl.ds(k * SUBLANES, SUBLANES), :], axis=0, keepdims=True) for k in range(KW)]
        dw_rows.append(jnp.zeros((HALO - KW, D), F32))
        for o, v in zip(outs[1:], (dbv, dbg, jnp.concatenate(dw_rows, axis=0), dcb, dg, db)):
            _accumulate(o, v, i)

    wide = pl.BlockSpec((tr, 2 * D), lambda i: (i, 1))
    return _call(
        name, compute, (n_tiles,), [proj, proj, b_in, b_in, conv_w, ln_g, ln_b, z0, z0, z1, z1, dz3, dz3, dproj],
        [_rows(tr, D, 2), _rows(tr, D, 3), pl.BlockSpec((1, D), lambda i: (0, 2)), pl.BlockSpec((1, D), lambda i: (0, 3)),
         _whole((HALO, D)), _whole((1, D)), _whole((1, D)),
         _rows(tr), _halo_before(tr, 0), _rows(tr), _halo_after(tr, 0, n_tiles), _rows(tr), _halo_after(tr, 0, n_tiles),
         pl.BlockSpec(memory_space=pl.ANY)],
        [jax.ShapeDtypeStruct(dproj.shape, dproj.dtype), _vec(), _vec(), _vec(HALO), _vec(), _vec(), _vec()],
        [wide] + [_whole((1, D))] * 2 + [_whole((HALO, D))] + [_whole((1, D))] * 3,
        [pltpu.VMEM((tr + HALO, D), F32), pltpu.VMEM((tr + HALO, D), F32),
         pltpu.VMEM((SUBLANES - 1, tr + HALO, D), F32), pltpu.VMEM((SUBLANES - 1, tr + HALO, D), F32),
         pltpu.VMEM((HALO * SUBLANES, D), F32)],
        ("arbitrary",), rider, aliases={13: 0})


def _merge_fn(ga, gb, bga, bgb, ya, yb):
    return jax.nn.sigmoid(ga + bga) * ya + jax.nn.sigmoid(gb + bgb) * yb


def _merge(name, proj, b_in, ya, yb, rider=None):
    t = proj.shape[0]
    tr = min(t, 256)

    def fn(i, r, _):
        return [_merge_fn(*[x[...] for x in r])]

    return _rowcall(
        name, fn, [proj, proj, b_in, b_in, ya, yb],
        [_rows(tr, D, 4), _rows(tr, D, 5), pl.BlockSpec((1, D), lambda i: (0, 4)), pl.BlockSpec((1, D), lambda i: (0, 5)),
         _rows(tr), _rows(tr)],
        1, [jax.ShapeDtypeStruct((t, D), BF)], [_rows(tr)], t // tr, rider=rider)


def _merge_bwd(name, proj, b_in, ya, yb, dm, rider=None):
    t = proj.shape[0]
    tr = min(t, 256)

    def fn(i, r, _):
        _, vjp = jax.vjp(_merge_fn, *[x[...] for x in r[:6]])
        dga, dgb, dbga, dbgb, dya, dyb = vjp(r[6][...])
        return [jnp.concatenate([dga, dgb], axis=1), dya, dyb, dbga, dbgb]

    return _rowcall(
        name, fn, [proj, proj, b_in, b_in, ya, yb, dm],
        [_rows(tr, D, 4), _rows(tr, D, 5), pl.BlockSpec((1, D), lambda i: (0, 4)), pl.BlockSpec((1, D), lambda i: (0, 5)),
         _rows(tr), _rows(tr), _rows(tr)],
        3, [jax.ShapeDtypeStruct((t, D_IN), BF)] + [jax.ShapeDtypeStruct((t, D), BF)] * 2 + [_vec(), _vec()],
        [pl.BlockSpec((tr, 2 * D), lambda i: (i, 2))] + [_rows(tr)] * 2 + [_whole((1, D))] * 2, t // tr, rider=rider)


def _loss_head(name, x, gain, target, f, g, scale):
    t = x.shape[0]
    tr = min(t, 256)

    def loss_fn(xv, gn, tgt):
        y = xv * lax.rsqrt(jnp.mean(xv * xv, axis=-1, keepdims=True) + EPS) * gn
        return 0.5 * jnp.sum(jnp.mean(jnp.square(y - tgt), axis=-1))

    def fn(i, r, _):
        loss, vjp = jax.vjp(loss_fn, r[0][...], r[1][...], r[2][...])
        dx, dgain, _ = vjp(jnp.ones((), F32))
        df, dg = _gate_grads(dx, r[3][...], r[4][...], scale)
        return [dx, df, dgain, jnp.zeros((1, D), F32) + loss, dg]

    return _rowcall(name, fn, [x, gain, target, f, g], [_rows(tr), _whole((1, D)), _rows(tr), _rows(tr), _whole((1, D))], 2,
                    [jax.ShapeDtypeStruct((t, D), F32), jax.ShapeDtypeStruct((t, D), BF), _vec(), _vec(), _vec()],
                    [_rows(tr)] * 2 + [_whole((1, D))] * 3, t // tr)


def _adamw(w, g, m, v):
    m = B1 * m + (1.0 - B1) * g
    v = B2 * v + (1.0 - B2) * jnp.square(g)
    m_hat = m / BC1
    v_hat = v / BC2
    delta = -LR * (m_hat / (jnp.sqrt(v_hat) + ADAM_EPS) + WD * w)
    return delta, m, v


ADAMW_ROWS = 64


def _adamw_group(name, items, rider=None, rows=ADAMW_ROWS):
    ins, in_specs, out_shapes, out_specs, plan = [], [], [], [], []
    first = 0
    for chip_sum, received, w, m, v in items:
        r, c = w.shape
        tr = min(r, rows)
        n = r // tr

        def tile(i, first=first, n=n):
            return jnp.clip(i - first, 0, n - 1)

        spec = pl.BlockSpec((tr, c), lambda i, tile=tile: (tile(i), 0))
        ins += [chip_sum, *received, w, m, v]
        in_specs += [pl.BlockSpec((None, tr, c), lambda i, tile=tile: (0, tile(i), 0))]
        in_specs += [pl.BlockSpec((g.shape[0], tr, c), lambda i, tile=tile: (0, tile(i), 0)) for g in received]
        in_specs += [spec] * 3
        out_shapes += [jax.ShapeDtypeStruct((r, c), F32)] * 4
        out_specs += [spec] * 4
        plan.append((first, n, [g.shape[0] for g in received]))
        first += n

    def compute(in_refs, out_refs, _):
        i = pl.program_id(0)
        at_in = at_out = 0
        for start, n, counts in plan:
            mine = in_refs[at_in:at_in + 4 + len(counts)]
            outs = out_refs[at_out:at_out + 4]
            at_in += 4 + len(counts)
            at_out += 4

            @pl.when(jnp.logical_and(i >= start, i < start + n))
            def _(mine=mine, outs=outs, counts=counts):
                g = mine[0][...].astype(F32)
                for j, count in enumerate(counts):
                    for s in range(count):
                        g = g + mine[1 + j][s].astype(F32)
                delta, m_new, v_new = _adamw(mine[-3][...], g, mine[-2][...], mine[-1][...])
                for o, val in zip(outs, (g, delta, m_new, v_new)):
                    o[...] = val

    res = _call(name, compute, (first,), ins, in_specs, out_shapes, out_specs, [], ("arbitrary",), rider)
    outs, rode = res if rider else (res, [])
    return [outs[4 * j:4 * j + 4] for j in range(len(items))], rode


def _adamw_small(name, packed_all, late_all, dws_all, vectors, w_s):
    n_vec = len(vectors)

    def body(*refs):
        p_ref, l_ref, d_ref = refs[:3]
        param_refs = refs[3:3 + 3 * n_vec + 3]
        out_refs = refs[3 + 3 * n_vec + 3:-1]
        g_ref = refs[-1]
        g = p_ref[0]
        late = l_ref[0]
        for s in range(1, NDEV):
            g = g + p_ref[s]
            late = late + l_ref[s]
        g_ref[...] = g
        g_ref[pl.ds(0, R_LATE), :] += late

        def update(gp, wmv, outs):
            delta, m_new, v_new = _adamw(wmv[0][...], gp, wmv[1][...], wmv[2][...])
            for o, val in zip(outs, (gp, delta, m_new, v_new)):
                o[...] = val

        for j, (row, rows, *_) in enumerate(vectors):
            pieces = [g_ref[pl.ds(row + r, 1), :] for r in range(rows)]
            update(pieces[0] if rows == 1 else jnp.concatenate(pieces, axis=1), param_refs[3 * j:3 * j + 3], out_refs[4 * j:4 * j + 4])
        gw = d_ref[0]
        for s in range(1, NDEV):
            gw = gw + d_ref[s]
        update(gw, param_refs[3 * n_vec:], out_refs[4 * n_vec:4 * n_vec + 4])
        out_refs[-2][...] = g_ref[pl.ds(R_CW, KW), :]
        out_refs[-1][...] = g_ref[pl.ds(R_LOSS, 1), :]

    params = [a for _, _, w, m, v in vectors for a in (w, m, v)] + list(w_s)
    out_shapes = [jax.ShapeDtypeStruct(w.shape, F32) for _, _, w, _, _ in vectors for _ in range(4)]
    out_shapes += [jax.ShapeDtypeStruct(w_s[0].shape, F32)] * 4 + [jax.ShapeDtypeStruct((KW, D), F32), _vec()]
    res = pl.pallas_call(body, name=name, out_shape=out_shapes, scratch_shapes=[pltpu.VMEM((R_TOTAL, D), F32)],
                         compiler_params=_params(None))(packed_all, late_all, dws_all, *params)
    return [res[4 * j:4 * j + 4] for j in range(n_vec + 1)], res[-2], res[-1]


def _adamw_plain(name, g, w, m, v):
    def body(g_ref, w_ref, m_ref, v_ref, d_ref, mo_ref, vo_ref):
        delta, m_new, v_new = _adamw(w_ref[...], g_ref[...], m_ref[...], v_ref[...])
        d_ref[...] = delta
        mo_ref[...] = m_new
        vo_ref[...] = v_new

    return pl.pallas_call(body, name=name, out_shape=[jax.ShapeDtypeStruct(w.shape, F32)] * 3,
                          compiler_params=_params(None))(g, w, m, v)


def _adamw_ada(name, c_all_t, dmod, dmod_late, w, m, v):
    r, c = w.shape
    tr = 256

    def fn(i, refs, _):
        ca = _silu(refs[0][...])
        dm = refs[1][...] + refs[2][...]
        g = ca[:, 0:1] * dm[0:1, :]
        for b in range(1, NDEV):
            g = g + ca[:, b:b + 1] * dm[b:b + 1, :]
        delta, m_new, v_new = _adamw(refs[3][...], g, refs[4][...], refs[5][...])
        return [g, delta, m_new, v_new]

    spec = pl.BlockSpec((tr, c), lambda i: (i, 0))
    whole = pl.BlockSpec((NDEV, c), lambda i: (0, 0))
    return _rowcall(name, fn, [c_all_t, dmod, dmod_late, w, m, v],
                    [pl.BlockSpec((tr, NDEV), lambda i: (i, 0)), whole, whole, spec, spec, spec], 4,
                    [jax.ShapeDtypeStruct((r, c), F32)] * 4, [spec] * 4, r // tr)


def _ffn_fwd(tag, x, h, g, wg, wu, wd_shard, down_rider, next_norm=None):
    t = x.shape[0]
    tm = min(t, 512 if down_rider else 1024)
    (gate, up, act), (wd,) = _ffn_up(f"{tag}_up", h, wg, wu, rider=_gather_rider([wd_shard]))
    row = pl.BlockSpec((1, D), lambda i, j, k: (0, 0))

    def epilogue(f, xv, gv, *norm):
        x_out = xv + 0.5 * gv * f
        return (x_out, f, _rms_mod(x_out, *norm)) if norm else (x_out, f)

    res = _mm_nn(f"{tag}_down", act, wd.reshape(F, D), tm, D, 1024, extras=(x, g, *(next_norm or ())),
                 extra_specs=(pl.BlockSpec((tm, D), lambda i, j, k: (i, 0)), row, *([row] * 3 if next_norm else [])),
                 epilogue=epilogue, out_dtypes=(F32, BF, BF) if next_norm else (F32, BF), rider=down_rider)
    (x_out, f, *h_next), rode = res if down_rider else (res, None)
    return x_out, (h_next[0] if next_norm else None), (x, h, gate, up, act, f), wd, rode


def _ffn_bwd(tag, dx_out, df, saved, gain, sh, sc, wg, wu, wd, slots, dact_rider=None, dwd_rider=None, dwgu_rider=None,
             below=None, fuse_dh=False):
    x, h, gate, up, act, f = saved
    t = x.shape[0]
    tm = min(t, 1024)
    if fuse_dh:
        dgate, dup, dh = _ffn_dact_dh(f"{tag}_dact_dh", df, wd.reshape(F, D), gate, up, wg, wu)
        dwd = _mm_tn(f"{tag}_dwd", act, df, 512, D).reshape(NDEV, F // NDEV, D)
        (dwg, dwu), (sib_d,) = _dw_gate_up(f"{tag}_dwgu", h, dgate, dup, rider=_pair_rider([dwd]))
        (sum_d,) = _pair_add(f"{tag}_dwd_add", [dwd], [sib_d], slots)
        normed, (sib_g, sib_u) = _norm_mod_bwd(f"{tag}_norm_bwd", x, gain, sc, sh, dh, dx_out, below=below,
                                               rider=_pair_rider([dwg, dwu]))
        sum_g, sum_u = _pair_add(f"{tag}_dwgu_add", [dwg, dwu], [sib_g, sib_u], slots)
        return normed, (sum_d, None), sum_g, sum_u, [], [], []

    blk = pl.BlockSpec((tm, F // NDEV), lambda i, j, k: (i, j))
    res = _mm_nt(f"{tag}_dact", df, wd.reshape(F, D), tm, F // NDEV, out_dtypes=(BF, BF),
                 extras=(gate, up), extra_specs=(blk, blk), epilogue=_swiglu_bwd, rider=dact_rider)
    (dgate, dup), rode_dact = res if dact_rider else (res, [])
    res = _mm_tn(f"{tag}_dwd", act, df, 512, D, rider=dwd_rider)
    dwd, rode_dwd = res if dwd_rider else (res, [])
    dwd = dwd.reshape(NDEV, F // NDEV, D)
    (dwg, dwu), (sib_d, *rode_dwgu) = _dw_gate_up(f"{tag}_dwgu", h, dgate, dup,
                                                  rider=[_pair_rider([dwd])] + ([dwgu_rider] if dwgu_rider else []))
    (sum_d,) = _pair_add(f"{tag}_dwd_add", [dwd], [sib_d], slots)
    dh, (sib_g, sib_u, got_d) = _mm_nt_blocked(f"{tag}_dh", [dgate, dup], [wg, wu], tm,
                                               rider=[_pair_rider([dwg, dwu]), _chip_rider([sum_d])])
    sum_g, sum_u = _pair_add(f"{tag}_dwgu_add", [dwg, dwu], [sib_g, sib_u], slots)
    normed = _norm_mod_bwd(f"{tag}_norm_bwd", x, gain, sc, sh, dh, dx_out, below=below)
    return normed, (sum_d, [got_d]), sum_g, sum_u, rode_dact, rode_dwd, rode_dwgu


def kernel(x, c, ada_w, ada_b, norm_ffn1, ffn1_w_gate, ffn1_w_up, ffn1_w_down, norm_mix, mix_w_in, mix_b_in, sgu_ln_g, sgu_ln_b, sgu_w_s, sgu_b_s, conv_w, conv_b, conv_ln_g, conv_ln_b, w_branch_a, w_branch_b, w_out, norm_ffn2, ffn2_w_gate, ffn2_w_up, ffn2_w_down, norm_final, loss_target, m_ada_w, m_ada_b, m_norm_ffn1, m_ffn1_w_gate, m_ffn1_w_up, m_ffn1_w_down, m_norm_mix, m_mix_w_in, m_mix_b_in, m_sgu_ln_g, m_sgu_ln_b, m_sgu_w_s, m_sgu_b_s, m_conv_w, m_conv_b, m_conv_ln_g, m_conv_ln_b, m_w_branch_a, m_w_branch_b, m_w_out, m_norm_ffn2, m_ffn2_w_gate, m_ffn2_w_up, m_ffn2_w_down, m_norm_final, v_ada_w, v_ada_b, v_norm_ffn1, v_ffn1_w_gate, v_ffn1_w_up, v_ffn1_w_down, v_norm_mix, v_mix_w_in, v_mix_b_in, v_sgu_ln_g, v_sgu_ln_b, v_sgu_w_s, v_sgu_b_s, v_conv_w, v_conv_b, v_conv_ln_g, v_conv_ln_b, v_w_branch_a, v_w_branch_b, v_w_out, v_norm_ffn2, v_ffn2_w_gate, v_ffn2_w_up, v_ffn2_w_down, v_norm_final):
    mx, my, mc = _position()
    me = 4 * mx + 2 * my + mc
    chip = 2 * mx + my
    slots = jnp.stack([2 * (chip ^ k) + mc for k in range(N_CHIPS)]).astype(jnp.int32)
    t = x.shape[1]
    tm = min(t, 1024)
    x0 = x.reshape(t, D)
    target = loss_target.reshape(t, D)
    given = dict(ffn1_w_gate=(ffn1_w_gate, m_ffn1_w_gate, v_ffn1_w_gate), ffn1_w_up=(ffn1_w_up, m_ffn1_w_up, v_ffn1_w_up),
                 ffn1_w_down=(ffn1_w_down, m_ffn1_w_down, v_ffn1_w_down), mix_w_in=(mix_w_in, m_mix_w_in, v_mix_w_in),
                 w_branch_a=(w_branch_a, m_w_branch_a, v_w_branch_a), w_branch_b=(w_branch_b, m_w_branch_b, v_w_branch_b),
                 w_out=(w_out, m_w_out, v_w_out), ffn2_w_gate=(ffn2_w_gate, m_ffn2_w_gate, v_ffn2_w_gate),
                 ffn2_w_up=(ffn2_w_up, m_ffn2_w_up, v_ffn2_w_up), ffn2_w_down=(ffn2_w_down, m_ffn2_w_down, v_ffn2_w_down))
    shard = {n: wmv[0][0].astype(BF) for n, wmv in given.items()}

    ada_cols = N_MOD * D // NDEV
    c_all, taps_all, mod_all, (wg1, wu1) = _prologue(
        "prologue", jnp.pad(c, ((0, SUBLANES - 1), (0, 0))), jnp.pad(conv_w[0], ((0, HALO - KW), (0, 0))), ada_w[0],
        lax.dynamic_slice(ada_b, (0, me * ada_cols), (1, ada_cols)), [shard["ffn1_w_gate"], shard["ffn1_w_up"]])
    conv_w_full = jnp.transpose(taps_all.reshape(NDEV, HALO, CHUNK), (1, 0, 2)).reshape(HALO, D)
    mod = lax.dynamic_index_in_dim(mod_all.reshape(NDEV, NDEV, ada_cols), me, axis=1, keepdims=False).reshape(N_MOD, 1, D)
    sh1, sc1, g1, sh2, sc2, g2, sh3, sc3, g3 = [mod[i] for i in range(N_MOD)]

    h1 = _norm_mod("ffn1_norm", x0, norm_ffn1, sc1, sh1)
    x1, h2, saved1, wd1, (w_in,) = _ffn_fwd("ffn1", x0, h1, g1, wg1, wu1, shard["ffn1_w_down"],
                                             _gather_rider([shard["mix_w_in"]]), next_norm=(norm_mix, sc2, sh2))
    proj, (wg2, wa3, wb3) = _mm_nn_blocked(
        "mix_in", h2, w_in, tm, rider=_gather_rider([shard["ffn2_w_gate"], shard["w_branch_a"], shard["w_branch_b"]]))
    bias_full = jnp.repeat(sgu_b_s[0].T, CHUNK, axis=1)
    (ua,) = _sgu("sgu", proj, mix_b_in, sgu_ln_g, sgu_ln_b, sgu_w_s[0], bias_full)
    (z0, z1, z3), (wu2, wo3) = _conv("conv", proj, mix_b_in, conv_w_full, conv_b, conv_ln_g, conv_ln_b,
                                     rider=_gather_rider([shard["ffn2_w_up"], shard["w_out"]]))
    wa, wb = wa3.reshape(D, D), wb3.reshape(D, D)
    ts = min(t, 512)
    ya = _mm_nn("branch_a", ua, wa, ts, D, D)[0]
    yb = _mm_nn("branch_b", z3, wb, ts, D, D)[0]
    (merged,) = _merge("merge", proj, mix_b_in, ya, yb)
    wo = wo3.reshape(D, D)

    def mix_epilogue(yv, xv, gv, gain, sc, sh):
        x_out = xv + gv * yv
        return x_out, yv, _rms_mod(x_out, gain, sc, sh)

    tmo = min(t, 512)
    row = pl.BlockSpec((1, D), lambda i, j, k: (0, 0))
    x2, y, h3 = _mm_nn("mix_out", merged, wo, tmo, D, D, extras=(x1, g2, norm_ffn2, sc3, sh3),
                       extra_specs=(pl.BlockSpec((tmo, D), lambda i, j, k: (i, 0)), row, row, row, row),
                       epilogue=mix_epilogue, out_dtypes=(F32, BF, BF))
    x3, _, saved3, wd2, _ = _ffn_fwd("ffn2", x2, h3, g3, wg2, wu2, shard["ffn2_w_down"], None)

    norm_final2 = norm_final.reshape(1, D)
    dx3, df3, d_norm_final, loss_row, dg3 = _loss_head("loss_head", x3, norm_final2, target, saved3[-1], g3, 0.5)
    (dx2, dy, d_norm_ffn2, dsc3, dsh3, dg2), (sum_d2, _), sum_g2, sum_u2, _, _, _ = _ffn_bwd(
        "ffn2", dx3, df3, saved3, norm_ffn2, sh3, sc3, wg2, wu2, wd2, slots, below=(y, g2, 1.0), fuse_dh=True)
    dm = _mm_nt("mix_out_bwd", dy, wo, ts, D)[0]
    dwo = _mm_tn("mix_dwo", merged, dy, 512, D).reshape(NDEV, D // NDEV, D)
    (dproj, dya, dyb, db_ga, db_gb), (got_g2_near,) = _merge_bwd("merge_bwd", proj, mix_b_in, ya, yb, dm,
                                                                 rider=_chip_rider([sum_g2], NEIGHBOURS))
    dua = _mm_nt("branch_a_bwd", dya, wa, ts, D)[0]
    dwa = _mm_tn("branch_dwa", ua, dya, 512, D).reshape(NDEV, D // NDEV, D)
    dz3 = _mm_nt("branch_b_bwd", dyb, wb, ts, D)[0]
    dwb = _mm_tn("branch_dwb", z3, dyb, 512, D).reshape(NDEV, D // NDEV, D)
    (dproj, db_u, db_v, d_sgu_g, d_sgu_b, d_ws, d_bs_t), (*sib_abo, got_g2_far) = _sgu_bwd(
        "sgu_bwd", proj, mix_b_in, sgu_ln_g, sgu_ln_b, sgu_w_s[0], bias_full, dua, dproj,
        rider=[_pair_rider([dwa, dwb, dwo]), _chip_rider([sum_g2], DIAGONAL)])
    sum_a, sum_b, sum_o = _pair_add("mix_dw_add", [dwa, dwb, dwo], sib_abo, slots)
    (dproj, db_cv, db_cg, d_cw, d_cb, d_cln_g, d_cln_b), (got_u2, got_d2) = _conv_bwd(
        "conv_bwd", proj, mix_b_in, conv_w_full, conv_ln_g, conv_ln_b, z0, z1, dz3, dproj, rider=_chip_rider([sum_u2, sum_d2]))
    dwin, (got_a, got_b, got_o) = _mm_tn_blocked("mix_dwin", h2, dproj, rider=_chip_rider([sum_a, sum_b, sum_o]))

    d_bs = jnp.transpose(d_bs_t[:, :HEADS])
    zero = jnp.zeros((1, D), F32)
    pack_rows = [zero, zero, zero, zero, zero, dg2, dsh3, dsc3, dg3,
                 zero, zero, d_norm_ffn2, d_norm_final,
                 db_u, db_v, db_cv, db_cg, db_ga, db_gb,
                 d_sgu_g, d_sgu_b, d_bs.reshape(1, D), d_cb, d_cln_g, d_cln_b,
                 d_cw[:KW], loss_row, jnp.zeros((R_TOTAL - R_LOSS - 1, D), F32)]
    packed = jnp.concatenate(pack_rows, axis=0)
    d_ws2 = d_ws.reshape(HEADS * CHUNK, CHUNK)
    dh2, (sib_in, packed_all, dws_all) = _mm_nt_blocked("mix_in_bwd", [dproj], [w_in], tm,
                                                        rider=[_pair_rider([dwin]), _gather_rider([packed, d_ws2])])
    (sum_in,) = _pair_add("mix_dwin_add", [dwin], [sib_in], slots)
    dx1, df1, d_norm_mix, dsc2, dsh2, dg1 = _norm_mod_bwd("mix_norm_bwd", x1, norm_mix, sc2, sh2, dh2, dx2,
                                                          below=(saved1[-1], g1, 0.5))
    (dx0, d_norm_ffn1, dsc1, dsh1), down1, sum_g1, sum_u1, (got_in_near,), _, (got_in_far,) = _ffn_bwd(
        "ffn1", dx1, df1, saved1, norm_ffn1, sh1, sc1, wg1, wu1, wd1, slots,
        dact_rider=_chip_rider([sum_in], NEIGHBOURS), dwgu_rider=_chip_rider([sum_in], DIAGONAL))
    packed_late = jnp.concatenate([dsh1, dsc1, dg1, dsh2, dsc2, jnp.zeros((4, D), F32), d_norm_ffn1, d_norm_mix,
                                   jnp.zeros((R_LATE - 11, D), F32)], axis=0)
    grads = dict(ffn2_w_gate=(sum_g2, [got_g2_near, got_g2_far]), ffn2_w_up=(sum_u2, [got_u2]), ffn2_w_down=(sum_d2, [got_d2]),
                 mix_w_in=(sum_in, [got_in_near, got_in_far]), w_branch_a=(sum_a, [got_a]), w_branch_b=(sum_b, [got_b]),
                 w_out=(sum_o, [got_o]), ffn1_w_down=down1)
    done, (late_all, got_g1, got_u1) = _adamw_group(
        "adamw_most", [(cs, got, *[a[0] for a in given[n]]) for n, (cs, got) in grads.items()],
        rider=[_gather_rider([packed_late]), _chip_rider([sum_g1, sum_u1])])
    last, _ = _adamw_group("adamw_ffn1_in", [(sum_g1, [got_g1], *[a[0] for a in given["ffn1_w_gate"]]),
                                            (sum_u1, [got_u1], *[a[0] for a in given["ffn1_w_up"]])], rows=256)
    big_out = {n: [o.reshape(given[n][0].shape) for o in outs]
               for n, outs in zip([*grads, "ffn1_w_gate", "ffn1_w_up"], [*done, *last])}

    flat = lambda a: a.reshape(1, -1)
    vectors = [("ada_b", 0, 9, ada_b, m_ada_b, v_ada_b), ("norm_ffn1", 9, 1, norm_ffn1, m_norm_ffn1, v_norm_ffn1),
               ("norm_mix", 10, 1, norm_mix, m_norm_mix, v_norm_mix), ("norm_ffn2", 11, 1, norm_ffn2, m_norm_ffn2, v_norm_ffn2),
               ("norm_final", 12, 1, norm_final, m_norm_final, v_norm_final), ("mix_b_in", 13, 6, mix_b_in, m_mix_b_in, v_mix_b_in),
               ("sgu_ln_g", 19, 1, sgu_ln_g, m_sgu_ln_g, v_sgu_ln_g), ("sgu_ln_b", 20, 1, sgu_ln_b, m_sgu_ln_b, v_sgu_ln_b),
               ("sgu_b_s", 21, 1, sgu_b_s, m_sgu_b_s, v_sgu_b_s), ("conv_b", 22, 1, conv_b, m_conv_b, v_conv_b),
               ("conv_ln_g", 23, 1, conv_ln_g, m_conv_ln_g, v_conv_ln_g), ("conv_ln_b", 24, 1, conv_ln_b, m_conv_ln_b, v_conv_ln_b)]
    small_out, d_cw_all, loss_sum = _adamw_small(
        "adamw_small", packed_all, late_all, dws_all, [(row, rows, flat(wv), flat(mv), flat(vv)) for _, row, rows, wv, mv, vv in vectors],
        [a.reshape(HEADS * CHUNK, CHUNK) for a in (sgu_w_s, m_sgu_w_s, v_sgu_w_s)])
    small = {n: [o.reshape(wv.shape) for o in outs] for (n, _, _, wv, _, _), outs in zip(vectors, small_out)}
    small["sgu_w_s"] = [o.reshape(sgu_w_s.shape) for o in small_out[-1]]
    g_cw = lax.dynamic_slice(d_cw_all, (0, me * CHUNK), (KW, CHUNK))
    small["conv_w"] = [o.reshape(conv_w.shape) for o in (g_cw, *_adamw_plain("adamw_conv_w", g_cw, conv_w[0], m_conv_w[0], v_conv_w[0]))]
    loss = loss_sum[0, 0]

    dmod_cols = [lax.dynamic_slice(a[:, :N_MOD, :].reshape(NDEV, N_MOD * D), (0, me * ada_cols), (NDEV, ada_cols))
                 for a in (packed_all, late_all)]
    ada_out = [o.reshape(ada_w.shape) for o in _adamw_ada("adamw_ada_w", jnp.transpose(c_all), *dmod_cols, ada_w[0], m_ada_w[0], v_ada_w[0])]

    order = ["ada_w", "ada_b", "norm_ffn1", "ffn1_w_gate", "ffn1_w_up", "ffn1_w_down", "norm_mix", "mix_w_in", "mix_b_in",
             "sgu_ln_g", "sgu_ln_b", "sgu_w_s", "sgu_b_s", "conv_w", "conv_b", "conv_ln_g", "conv_ln_b", "w_branch_a",
             "w_branch_b", "w_out", "norm_ffn2", "ffn2_w_gate", "ffn2_w_up", "ffn2_w_down", "norm_final"]

    def leaf(n, kind):
        if n == "ada_w":
            return ada_out[kind]
        if n in big_out:
            return big_out[n][kind]
        return small[n][kind]

    return (loss, dx0.reshape(x.shape), *[leaf(n, kind) for kind in range(4) for n in order])
```

```python
import jax
import jax.numpy as jnp
from jax import lax
from jax.experimental import pallas as pl
from jax.experimental.pallas import tpu as pltpu

D = 1024
F = 4 * D
D_IN = 6 * D
HEADS = 8
CHUNK = 128
KW = 31
HALO = 32
N_MOD = 9
NDEV = 8
N_CHIPS = 4
EPS = 1e-6
LR, B1, B2, ADAM_EPS, WD, STEP = 0.001, 0.9, 0.999, 1e-08, 0.01, 10
BC1 = 1.0 - B1 ** STEP
BC2 = 1.0 - B2 ** STEP
VMEM_LIMIT = 56 * 1024 * 1024
MESH = pl.DeviceIdType.MESH
HBM = pl.BlockSpec(memory_space=pltpu.HBM)
VMEM = pl.BlockSpec(memory_space=pltpu.VMEM)
BF = jnp.bfloat16
F32 = jnp.float32

NN = (((1,), (0,)), ((), ()))
NT = (((1,), (1,)), ((), ()))
TN = (((0,), (0,)), ((), ()))

R_CW, R_LOSS, R_TOTAL = 25, 56, 64
R_LATE = 16


def _params(sem):
    return pltpu.CompilerParams(dimension_semantics=sem, vmem_limit_bytes=VMEM_LIMIT)


def _position():
    return lax.axis_index("x"), lax.axis_index("y"), lax.axis_index("c")


def _flip(pos, k):
    x, y, c = pos
    return (x ^ (k >> 2 & 1), y ^ (k >> 1 & 1), c ^ (k & 1))


def _index(pos):
    return 4 * pos[0] + 2 * pos[1] + pos[2]


def _gather_rows(x_ref, out_ref, send_sems, recv_sems, local_sem):
    m_per = x_ref.shape[0]
    x, y, c = _position()
    me, sibling = (x, y, c), (x, y, 1 - c)
    chips = [(1 - x, y), (x, 1 - y), (1 - x, 1 - y)]

    def rows(pos):
        return out_ref.at[pl.ds(_index(pos) * m_per, m_per), :]

    def copy(k, block, to, src=None):
        return pltpu.make_async_remote_copy(
            src_ref=rows(block) if src is None else src, dst_ref=rows(block),
            send_sem=send_sems.at[k], recv_sem=recv_sems.at[k], device_id=to, device_id_type=MESH)

    mine = pltpu.make_async_copy(x_ref, rows(me), local_sem)
    mine.start()
    first = [copy(0, me, sibling, src=x_ref)]
    first += [copy(1 + j, me, (*chip, c), src=x_ref) for j, chip in enumerate(chips)]
    for cp in first:
        cp.start()
    passed = [copy(4 + j, (*chip, c), sibling) for j, chip in enumerate(chips)]
    for j, chip in enumerate(chips):
        copy(1 + j, (*chip, c), me).wait_recv()
        passed[j].start()
    copy(0, sibling, me).wait_recv()
    for j, chip in enumerate(chips):
        copy(4 + j, (*chip, 1 - c), me).wait_recv()
    for cp in first + passed:
        cp.wait_send()
    mine.wait()


def _prologue(name, c_rows, taps, ada_w, ada_b, shards):
    rider = _gather_rider(shards)
    n = len(shards)
    nc = ada_w.shape[1]

    def body(*refs):
        c_ref, taps_ref, w_ref, b_ref = refs[:4]
        shard_refs = refs[4:4 + n]
        c_all_ref, taps_all_ref, mod_all_ref = refs[4 + n:7 + n]
        gathered_refs = refs[7 + n:7 + 2 * n]
        c_buf, mod_part, sems = refs[7 + 2 * n], refs[8 + 2 * n], refs[9 + 2 * n:]
        rider.start(shard_refs, gathered_refs, sems[9:])
        _gather_rows(c_ref, c_buf, *sems[0:3])
        c_all = jnp.concatenate([c_buf[pl.ds(d * SUBLANES, 1), :] for d in range(NDEV)], axis=0)
        c_all_ref[...] = c_all
        mod_part[...] = jnp.dot(_silu(c_all), w_ref[...], preferred_element_type=F32) + b_ref[...]
        _gather_rows(taps_ref, taps_all_ref, *sems[3:6])
        _gather_rows(mod_part, mod_all_ref, *sems[6:9])
        rider.mid(shard_refs, gathered_refs, sems[9:])
        rider.relay(shard_refs, gathered_refs, sems[9:])
        rider.finish(shard_refs, gathered_refs, sems[9:])

    small_sems = [pltpu.SemaphoreType.DMA((7,)), pltpu.SemaphoreType.DMA((7,)), pltpu.SemaphoreType.DMA] * 3
    res = pl.pallas_call(
        body, name=name,
        out_shape=[jax.ShapeDtypeStruct((NDEV, D), F32), jax.ShapeDtypeStruct((NDEV * taps.shape[0], taps.shape[1]), F32),
                   jax.ShapeDtypeStruct((NDEV * NDEV, nc), F32)] + rider.out_shapes,
        in_specs=[VMEM] * 4 + [HBM] * n, out_specs=[VMEM] * 3 + [HBM] * n,
        scratch_shapes=[pltpu.VMEM((NDEV * SUBLANES, D), F32), pltpu.VMEM((NDEV, nc), F32)] + small_sems + rider.sems,
        compiler_params=_params(None),
    )(c_rows, taps, ada_w, ada_b, *shards)
    return res[0], res[1], res[2], res[3:]


class _Rider:
    def __init__(self, ins, out_shapes, sems, start, finish, mid=None, relay=None):
        self.ins, self.out_shapes, self.sems = list(ins), list(out_shapes), list(sems)
        self.start, self.finish, self.mid, self.relay = start, finish, mid, relay


def _gather_rider(shards):
    n = len(shards)

    def setup(ins, outs, sems):
        send_sems, recv_sems, local_sems = sems
        x, y, c = _position()
        places = dict(me=(x, y, c), sibling=(x, y, 1 - c), xn=(1 - x, y, c), yn=(x, 1 - y, c), diagonal=(1 - x, 1 - y, c),
                      passed_on=(x ^ c, y ^ (1 - c), c), passed_to=(x ^ (1 - c), y ^ c, c))

        def copy(a, k, block, to, own=False):
            slot = outs[a].at[_index(block)]
            return pltpu.make_async_remote_copy(
                src_ref=ins[a] if own else slot, dst_ref=slot,
                send_sem=send_sems.at[k, a], recv_sem=recv_sems.at[k, a], device_id=to, device_id_type=MESH)

        def local(a):
            return pltpu.make_async_copy(ins[a], outs[a].at[_index(places["me"])], local_sems.at[a])

        return places, copy, local

    def start(ins, outs, sems):
        p, copy, local = setup(ins, outs, sems)
        for a in range(n):
            local(a).start()
            for k, to in enumerate(("sibling", "xn", "yn")):
                copy(a, k, p["me"], p[to], own=True).start()

    def mid(ins, outs, sems):
        p, copy, _ = setup(ins, outs, sems)
        for a in range(n):
            copy(a, 1, p["xn"], p["me"]).wait_recv()
            copy(a, 2, p["yn"], p["me"]).wait_recv()
            copy(a, 3, p["passed_on"], p["passed_to"]).start()
            copy(a, 4, p["xn"], p["sibling"]).start()
            copy(a, 5, p["yn"], p["sibling"]).start()

    def relay(ins, outs, sems):
        p, copy, _ = setup(ins, outs, sems)
        for a in range(n):
            copy(a, 3, p["diagonal"], p["me"]).wait_recv()
            copy(a, 6, p["diagonal"], p["sibling"]).start()

    def finish(ins, outs, sems):
        p, copy, local = setup(ins, outs, sems)
        x, y, c = p["me"]
        for a in range(n):
            for k, block in ((0, (x, y, 1 - c)), (4, (1 - x, y, 1 - c)), (5, (x, 1 - y, 1 - c)), (6, (1 - x, 1 - y, 1 - c))):
                copy(a, k, block, p["me"]).wait_recv()
            for k, to in enumerate(("sibling", "xn", "yn")):
                copy(a, k, p["me"], p[to], own=True).wait_send()
            copy(a, 3, p["passed_on"], p["passed_to"]).wait_send()
            for k, block in ((4, "xn"), (5, "yn"), (6, "diagonal")):
                copy(a, k, p[block], p["sibling"]).wait_send()
            local(a).wait()

    return _Rider(shards, [jax.ShapeDtypeStruct((NDEV, *s.shape), s.dtype) for s in shards],
                  [pltpu.SemaphoreType.DMA((7, n)), pltpu.SemaphoreType.DMA((7, n)), pltpu.SemaphoreType.DMA((n,))],
                  start, finish, mid, relay)


def _pair_rider(parts):
    n = len(parts)

    def copies(ins, outs, sems):
        send_sems, recv_sems = sems
        x, y, c = _position()
        q = 2 * x + y
        return [pltpu.make_async_remote_copy(
            src_ref=ins[a].at[2 * (q ^ k) + (1 - c)], dst_ref=outs[a].at[k],
            send_sem=send_sems.at[k, a], recv_sem=recv_sems.at[k, a], device_id=(x, y, 1 - c), device_id_type=MESH)
            for a in range(n) for k in range(N_CHIPS)]

    def start(ins, outs, sems):
        for cp in copies(ins, outs, sems):
            cp.start()

    def finish(ins, outs, sems):
        for cp in copies(ins, outs, sems):
            cp.wait()

    return _Rider(parts, [jax.ShapeDtypeStruct((N_CHIPS, *p.shape[1:]), p.dtype) for p in parts],
                  [pltpu.SemaphoreType.DMA((N_CHIPS, n)), pltpu.SemaphoreType.DMA((N_CHIPS, n))], start, finish)


NEIGHBOURS = (1, 2)
DIAGONAL = (3,)
OTHER_CHIPS = NEIGHBOURS + DIAGONAL


def _chip_rider(sums, ks=OTHER_CHIPS):
    n = len(sums)

    def copies(ins, outs, sems):
        send_sems, recv_sems = sems
        me = _position()
        return [pltpu.make_async_remote_copy(
            src_ref=ins[a].at[k], dst_ref=outs[a].at[j],
            send_sem=send_sems.at[j, a], recv_sem=recv_sems.at[j, a], device_id=_flip(me, 2 * k), device_id_type=MESH)
            for a in range(n) for j, k in enumerate(ks)]

    def start(ins, outs, sems):
        for cp in copies(ins, outs, sems):
            cp.start()

    def finish(ins, outs, sems):
        for cp in copies(ins, outs, sems):
            cp.wait()

    return _Rider(sums, [jax.ShapeDtypeStruct((len(ks), *s.shape[1:]), s.dtype) for s in sums],
                  [pltpu.SemaphoreType.DMA((len(ks), n)), pltpu.SemaphoreType.DMA((len(ks), n))], start, finish)


def _grid_edge(grid, last):
    cond = None
    for d, n in enumerate(grid):
        here = pl.program_id(d) == (n - 1 if last else 0)
        cond = here if cond is None else jnp.logical_and(cond, here)
    return cond


def _call(name, compute, grid, ins, in_specs, out_shapes, out_specs, scratch_shapes, semantics, rider=None, aliases=None):
    riders = [rider] if isinstance(rider, _Rider) else list(rider or [])
    n_in, n_out, n_scr = len(ins), len(out_shapes), len(scratch_shapes)
    n_rin, n_rout, n_rsem = [sum(len(part(r)) for r in riders) for part in (lambda r: r.ins, lambda r: r.out_shapes, lambda r: r.sems)]
    cuts = [0, n_in, n_in + n_rin, n_in + n_rin + n_out, n_in + n_rin + n_out + n_rout, n_in + n_rin + n_out + n_rout + n_scr]

    def body(*refs):
        in_refs, rin_refs, out_refs, rout_refs, scr_refs = [refs[a:b] for a, b in zip(cuts[:-1], cuts[1:])]
        rsem_refs = refs[cuts[-1]:]
        mine, at = [], [0, 0, 0]
        for r in riders:
            mine.append((r, rin_refs[at[0]:at[0] + len(r.ins)], rout_refs[at[1]:at[1] + len(r.out_shapes)],
                         rsem_refs[at[2]:at[2] + len(r.sems)]))
            at = [at[0] + len(r.ins), at[1] + len(r.out_shapes), at[2] + len(r.sems)]
        if riders:
            @pl.when(_grid_edge(grid, last=False))
            def _():
                for r, a, b, c in mine:
                    r.start(a, b, c)

        if any(r.mid for r in riders):
            step, steps = 0, 1
            for d, size in enumerate(grid):
                step, steps = step * size + pl.program_id(d), steps * size

            @pl.when(step == steps * 5 // 8)
            def _():
                for r, a, b, c in mine:
                    if r.mid:
                        r.mid(a, b, c)

        if any(r.relay for r in riders):
            @pl.when(_grid_edge(grid, last=True))
            def _():
                for r, a, b, c in mine:
                    if r.relay:
                        r.relay(a, b, c)

        compute(in_refs, out_refs, scr_refs)
        if riders:
            @pl.when(_grid_edge(grid, last=True))
            def _():
                for r, a, b, c in mine:
                    r.finish(a, b, c)

    res = pl.pallas_call(
        body, name=name, grid=grid,
        out_shape=list(out_shapes) + [s for r in riders for s in r.out_shapes],
        in_specs=list(in_specs) + [HBM] * n_rin, out_specs=list(out_specs) + [HBM] * n_rout,
        scratch_shapes=list(scratch_shapes) + [s for r in riders for s in r.sems],
        input_output_aliases=aliases or {}, compiler_params=_params(semantics),
    )(*ins, *[a for r in riders for a in r.ins])
    return (res[:n_out], res[n_out:]) if riders else res


def _pair_add(name, parts, from_sibling, slots):
    n = len(parts)

    def body(s_ref, *refs):
        for a in range(n):
            refs[2 * n + a][...] = (refs[a][...].astype(F32) + refs[n + a][...].astype(F32)).astype(refs[2 * n + a].dtype)

    def slab(p, picked):
        _, r, c = p.shape
        return pl.BlockSpec((None, r, c), (lambda k, s: (s[k], 0, 0)) if picked else (lambda k, s: (k, 0, 0)))

    return pl.pallas_call(
        body, name=name,
        grid_spec=pltpu.PrefetchScalarGridSpec(
            num_scalar_prefetch=1, grid=(N_CHIPS,),
            in_specs=[slab(p, True) for p in parts] + [slab(p, False) for p in parts],
            out_specs=[slab(p, False) for p in parts]),
        out_shape=[jax.ShapeDtypeStruct((N_CHIPS, *p.shape[1:]), p.dtype) for p in parts],
        compiler_params=_params(("arbitrary",)),
    )(slots, *parts, *from_sibling)


def _mm(name, pairs, dims, grid, nk, out_shapes, out_specs, extras=(), extra_specs=(), epilogue=None, acc_shape=None, rider=None):
    n_pairs = len(pairs)

    def compute(ins, outs, scratch):
        def partial_sum():
            total = None
            for p in range(n_pairs):
                d = lax.dot_general(ins[2 * p][...], ins[2 * p + 1][...], dims, preferred_element_type=F32)
                total = d if total is None else total + d
            return total

        def finish(r):
            ex = [e[...] for e in ins[2 * n_pairs:]]
            res = epilogue(r, *ex) if epilogue is not None else (r,)
            for o, v in zip(outs, res):
                o[...] = v.astype(o.dtype)

        if nk == 1:
            finish(partial_sum())
        else:
            acc = scratch[0]
            k = pl.program_id(2)

            @pl.when(k == 0)
            def _():
                acc[...] = partial_sum()

            @pl.when(k > 0)
            def _():
                acc[...] += partial_sum()

            @pl.when(k == nk - 1)
            def _():
                finish(acc[...])

    operands, specs = [], []
    for a, a_spec, b, b_spec in pairs:
        operands += [a, b]
        specs += [a_spec, b_spec]
    return _call(name, compute, grid, operands + list(extras), specs + list(extra_specs), out_shapes, out_specs,
                 [pltpu.VMEM(acc_shape, F32)] if nk > 1 else [], ("parallel", "parallel", "arbitrary"), rider)


def _single(res, rider):
    return (res[0][0], res[1]) if rider else res[0]


def _silu(x):
    return x * jax.nn.sigmoid(x)


def _ffn_up(name, h, wg, wu, rider=None):
    t = h.shape[0]
    tm = min(t, 1024)
    nb = F // NDEV

    def compute(ins, outs, _):
        hv = ins[0][...]
        g = jnp.dot(hv, ins[1][...], preferred_element_type=F32)
        u = jnp.dot(hv, ins[2][...], preferred_element_type=F32)
        outs[0][...] = g.astype(BF)
        outs[1][...] = u.astype(BF)
        outs[2][...] = (_silu(g) * u).astype(BF)

    w_spec = pl.BlockSpec((None, D, nb), lambda i, j: (j, 0, 0))
    o_spec = pl.BlockSpec((tm, nb), lambda i, j: (i, j))
    return _call(name, compute, (t // tm, NDEV), [h, wg, wu], [pl.BlockSpec((tm, D), lambda i, j: (i, 0)), w_spec, w_spec],
                 [jax.ShapeDtypeStruct((t, F), BF)] * 3, [o_spec] * 3, [], ("parallel", "arbitrary"), rider)


def _mm_nn(name, a, b, tm, tn, tk, extras=(), extra_specs=(), epilogue=None, out_dtypes=(F32,), rider=None):
    m, kk = a.shape
    n = b.shape[1]
    nk = kk // tk
    return _mm(
        name, [(a, pl.BlockSpec((tm, tk), lambda i, j, k: (i, k)), b, pl.BlockSpec((tk, tn), lambda i, j, k: (k, j)))], NN,
        (m // tm, n // tn, nk), nk,
        [jax.ShapeDtypeStruct((m, n), dt) for dt in out_dtypes],
        [pl.BlockSpec((tm, tn), lambda i, j, k: (i, j))] * len(out_dtypes),
        extras, extra_specs, epilogue, (tm, tn), rider)


def _mm_nn_blocked(name, a, b3, tm, rider=None):
    m = a.shape[0]
    nb = b3.shape[2]
    return _single(_mm(
        name, [(a, pl.BlockSpec((tm, D), lambda i, j, k: (i, 0)), b3, pl.BlockSpec((None, D, nb), lambda i, j, k: (j, 0, 0)))], NN,
        (m // tm, NDEV, 1), 1,
        [jax.ShapeDtypeStruct((m, NDEV * nb), F32)], [pl.BlockSpec((tm, nb), lambda i, j, k: (i, j))], rider=rider), rider)


def _mm_nt(name, a, b, tm, tn, out_dtypes=(F32,), extras=(), extra_specs=(), epilogue=None, rider=None):
    m, kk = a.shape
    n = b.shape[0]
    return _mm(
        name, [(a, pl.BlockSpec((tm, kk), lambda i, j, k: (i, 0)), b, pl.BlockSpec((tn, kk), lambda i, j, k: (j, 0)))], NT,
        (m // tm, n // tn, 1), 1,
        [jax.ShapeDtypeStruct((m, n), dt) for dt in out_dtypes],
        [pl.BlockSpec((tm, tn), lambda i, j, k: (i, j))] * len(out_dtypes),
        extras, extra_specs, epilogue, rider=rider)


def _mm_nt_blocked(name, a_list, b3_list, tm, rider=None):
    m = a_list[0].shape[0]
    nb = b3_list[0].shape[2]
    pairs = [(a, pl.BlockSpec((tm, nb), lambda i, j, k: (i, k)), b3, pl.BlockSpec((None, D, nb), lambda i, j, k: (k, 0, 0)))
             for a, b3 in zip(a_list, b3_list)]
    return _single(_mm(name, pairs, NT, (m // tm, 1, NDEV), NDEV,
                       [jax.ShapeDtypeStruct((m, D), F32)], [pl.BlockSpec((tm, D), lambda i, j, k: (i, 0))],
                       acc_shape=(tm, D), rider=rider), rider)


def _mm_tn(name, a, b, tm, tn, rider=None):
    t, m = a.shape
    n = b.shape[1]
    return _single(_mm(
        name, [(a, pl.BlockSpec((t, tm), lambda i, j, k: (0, i)), b, pl.BlockSpec((t, tn), lambda i, j, k: (0, j)))], TN,
        (m // tm, n // tn, 1), 1,
        [jax.ShapeDtypeStruct((m, n), BF)], [pl.BlockSpec((tm, tn), lambda i, j, k: (i, j))], rider=rider), rider)


def _mm_tn_blocked(name, a, b, rider=None):
    t = a.shape[0]
    nb = b.shape[1] // NDEV
    return _single(_mm(
        name, [(a, pl.BlockSpec((t, D), lambda i, j, k: (0, 0)), b, pl.BlockSpec((t, nb), lambda i, j, k: (0, j)))], TN,
        (1, NDEV, 1), 1,
        [jax.ShapeDtypeStruct((NDEV, D, nb), BF)], [pl.BlockSpec((None, D, nb), lambda i, j, k: (j, 0, 0))], rider=rider), rider)


def _dw_gate_up(name, h, dgate, dup, rider=None):
    t = h.shape[0]
    nb = F // NDEV

    def compute(ins, outs, _):
        hv = ins[0][...]
        outs[0][...] = lax.dot_general(hv, ins[1][...], TN, preferred_element_type=F32).astype(BF)
        outs[1][...] = lax.dot_general(hv, ins[2][...], TN, preferred_element_type=F32).astype(BF)

    d_spec = pl.BlockSpec((t, nb), lambda j: (0, j))
    o_spec = pl.BlockSpec((None, D, nb), lambda j: (j, 0, 0))
    return _call(name, compute, (NDEV,), [h, dgate, dup], [pl.BlockSpec((t, D), lambda j: (0, 0)), d_spec, d_spec],
                 [jax.ShapeDtypeStruct((NDEV, D, nb), BF)] * 2, [o_spec] * 2, [], ("arbitrary",), rider)


def _swiglu_bwd(da, gate, up):
    gate = gate.astype(F32)
    s = jax.nn.sigmoid(gate)
    return da * up.astype(F32) * (s * (1.0 + gate * (1.0 - s))), da * (gate * s)


def _ffn_dact_dh(name, df, wd, gate, up, wg, wu, rider=None):
    t = df.shape[0]
    tm = min(t, 1024)
    nb = F // NDEV

    def compute(ins, outs, scr):
        acc = scr[0]
        j = pl.program_id(1)
        da = lax.dot_general(ins[0][...], ins[1][...], NT, preferred_element_type=F32)
        dgate, dup = _swiglu_bwd(da, ins[2][...], ins[3][...])
        dgate, dup = dgate.astype(BF), dup.astype(BF)
        outs[0][...] = dgate
        outs[1][...] = dup
        part = (lax.dot_general(dgate, ins[4][...], NT, preferred_element_type=F32)
                + lax.dot_general(dup, ins[5][...], NT, preferred_element_type=F32))

        @pl.when(j == 0)
        def _():
            acc[...] = part

        @pl.when(j > 0)
        def _():
            acc[...] += part

        @pl.when(j == NDEV - 1)
        def _():
            outs[2][...] = acc[...]

    blk = pl.BlockSpec((tm, nb), lambda i, j: (i, j))
    w3 = pl.BlockSpec((None, D, nb), lambda i, j: (j, 0, 0))
    row = pl.BlockSpec((tm, D), lambda i, j: (i, 0))
    return _call(name, compute, (t // tm, NDEV), [df, wd, gate, up, wg, wu],
                 [row, pl.BlockSpec((nb, D), lambda i, j: (j, 0)), blk, blk, w3, w3],
                 [jax.ShapeDtypeStruct((t, F), BF)] * 2 + [jax.ShapeDtypeStruct((t, D), F32)], [blk, blk, row],
                 [pltpu.VMEM((tm, D), F32)], ("parallel", "arbitrary"), rider)


def _rowcall(name, fn, ins, in_specs, n_row_out, out_shapes, out_specs, grid, scratch_shapes=(), rider=None, aliases=None):
    def accumulate(o, v, i):
        @pl.when(i == 0)
        def _():
            o[...] = v.astype(o.dtype)

        @pl.when(i > 0)
        def _():
            o[...] += v.astype(o.dtype)

    def compute(in_refs, out_refs, scr):
        i = pl.program_id(0)
        vals = fn(i, in_refs, scr)
        for idx, (o, v) in enumerate(zip(out_refs, vals)):
            if idx < n_row_out:
                o[...] = v.astype(o.dtype)
            else:
                accumulate(o, v, i)

    return _call(name, compute, (grid,), ins, in_specs, out_shapes, out_specs, list(scratch_shapes), ("arbitrary",), rider, aliases)


def _rows(tr, w=D, cb=0):
    return pl.BlockSpec((tr, w), lambda i: (i, cb))


def _whole(shape):
    nd = len(shape)
    return pl.BlockSpec(shape, lambda i: (0,) * nd)


def _vec(n=1):
    return jax.ShapeDtypeStruct((n, D), F32)


def _rms_mod(x, gain, sc, sh):
    y = x * lax.rsqrt(jnp.mean(x * x, axis=-1, keepdims=True) + EPS)
    return (y * gain) * (1.0 + sc) + sh


def _layer_norm(x, g, b):
    mu = jnp.mean(x, axis=-1, keepdims=True)
    var = jnp.mean(jnp.square(x - mu), axis=-1, keepdims=True)
    return (x - mu) * lax.rsqrt(var + EPS) * g + b


def _norm_mod(name, x, gain, sc, sh):
    t = x.shape[0]
    tr = min(t, 256)

    def fn(i, r, _):
        return [_rms_mod(r[0][...], r[1][...], r[2][...], r[3][...])]

    return _rowcall(name, fn, [x, gain, sc, sh], [_rows(tr)] + [_whole((1, D))] * 3, 1,
                    [jax.ShapeDtypeStruct((t, D), BF)], [_rows(tr)], t // tr)[0]


def _gate_grads(dx, f, g, scale):
    return scale * g * dx, jnp.sum(scale * dx * f.astype(F32), axis=0, keepdims=True)


def _norm_mod_bwd(name, x, gain, sc, sh, dh, dres, below=None, rider=None):
    t = x.shape[0]
    tr = min(t, 256)

    def fn(i, r, _):
        _, vjp = jax.vjp(_rms_mod, r[0][...], r[1][...], r[2][...], r[3][...])
        dx, dgain, dsc, dsh = vjp(r[4][...])
        dx = dx + r[5][...]
        if below is None:
            return [dx, dgain, dsc, dsh]
        df, dg = _gate_grads(dx, r[6][...], r[7][...], below[2])
        return [dx, df, dgain, dsc, dsh, dg]

    ins, specs = [x, gain, sc, sh, dh, dres], [_rows(tr)] + [_whole((1, D))] * 3 + [_rows(tr)] * 2
    outs, out_specs = [jax.ShapeDtypeStruct((t, D), F32)], [_rows(tr)]
    if below is not None:
        ins, specs = ins + [below[0], below[1]], specs + [_rows(tr), _whole((1, D))]
        outs, out_specs = outs + [jax.ShapeDtypeStruct((t, D), BF)], out_specs + [_rows(tr)]
    n_vec = 3 if below is None else 4
    return _rowcall(name, fn, ins, specs, len(outs), outs + [_vec()] * n_vec, out_specs + [_whole((1, D))] * n_vec, t // tr,
                    rider=rider)


def _sgu_pre(up, vp, bu, bv, ln_g, ln_b):
    return jax.nn.gelu(up + bu), _layer_norm(jax.nn.gelu(vp + bv), ln_g, ln_b)


def _causal(w_ref, h):
    rows = lax.broadcasted_iota(jnp.int32, (CHUNK, CHUNK), 0)
    cols = lax.broadcasted_iota(jnp.int32, (CHUNK, CHUNK), 1)
    return jnp.where(cols <= rows, w_ref[h], 0.0)


def _sgu(name, proj, b_in, ln_g, ln_b, w_s, bias_full, rider=None):
    t = proj.shape[0]

    def fn(i, r, _):
        u, v = _sgu_pre(r[0][...], r[1][...], r[2][...], r[3][...], r[4][...], r[5][...])
        vb = v.astype(BF)
        mixed = [jnp.dot(_causal(r[6], h).astype(BF), vb[:, h * CHUNK:(h + 1) * CHUNK], preferred_element_type=F32)
                 for h in range(HEADS)]
        return [u * (jnp.concatenate(mixed, axis=1) + r[7][...])]

    return _rowcall(
        name, fn, [proj, proj, b_in, b_in, ln_g, ln_b, w_s, bias_full],
        [_rows(CHUNK, D, 0), _rows(CHUNK, D, 1), pl.BlockSpec((1, D), lambda i: (0, 0)), pl.BlockSpec((1, D), lambda i: (0, 1)),
         _whole((1, D)), _whole((1, D)), _whole((HEADS, CHUNK, CHUNK)), _whole((CHUNK, D))],
        1, [jax.ShapeDtypeStruct((t, D), BF)], [_rows(CHUNK)], t // CHUNK, rider=rider)


def _sgu_bwd(name, proj, b_in, ln_g, ln_b, w_s, bias_full, dout, dproj, rider=None):
    t = proj.shape[0]

    def fn(i, r, _):
        (u, v), vjp = jax.vjp(_sgu_pre, r[0][...], r[1][...], r[2][...], r[3][...], r[4][...], r[5][...])
        vb = v.astype(BF)
        d = r[8][...]
        masks = [_causal(r[6], h).astype(BF) for h in range(HEADS)]
        cols = [slice(h * CHUNK, (h + 1) * CHUNK) for h in range(HEADS)]
        mixed = jnp.concatenate([jnp.dot(masks[h], vb[:, cols[h]], preferred_element_type=F32) for h in range(HEADS)], axis=1)
        du = d * (mixed + r[7][...])
        dmix = d * u
        dmb = dmix.astype(BF)
        dv = jnp.concatenate([lax.dot_general(masks[h], dmb[:, cols[h]], TN, preferred_element_type=F32) for h in range(HEADS)], axis=1)
        rows = lax.broadcasted_iota(jnp.int32, (CHUNK, CHUNK), 0)
        lanes = lax.broadcasted_iota(jnp.int32, (CHUNK, CHUNK), 1)
        dws = jnp.stack([jnp.where(lanes <= rows, lax.dot_general(dmb[:, cols[h]], vb[:, cols[h]], NT, preferred_element_type=F32), 0.0)
                         for h in range(HEADS)])
        dbs = jnp.zeros((CHUNK, CHUNK), F32)
        for h in range(HEADS):
            dbs = dbs + jnp.where(lanes == h, jnp.sum(dmix[:, cols[h]], axis=1, keepdims=True), 0.0)
        dup, dvp, dbu, dbv, dg, db = vjp((du, dv))
        return [jnp.concatenate([dup, dvp], axis=1), dbu, dbv, dg, db, dws, dbs]

    return _rowcall(
        name, fn, [proj, proj, b_in, b_in, ln_g, ln_b, w_s, bias_full, dout, dproj],
        [_rows(CHUNK, D, 0), _rows(CHUNK, D, 1), pl.BlockSpec((1, D), lambda i: (0, 0)), pl.BlockSpec((1, D), lambda i: (0, 1)),
         _whole((1, D)), _whole((1, D)), _whole((HEADS, CHUNK, CHUNK)), _whole((CHUNK, D)), _rows(CHUNK),
         pl.BlockSpec(memory_space=pl.ANY)],
        1, [jax.ShapeDtypeStruct(dproj.shape, dproj.dtype)] + [_vec()] * 4
        + [jax.ShapeDtypeStruct((HEADS, CHUNK, CHUNK), F32), jax.ShapeDtypeStruct((CHUNK, CHUNK), F32)],
        [pl.BlockSpec((CHUNK, 2 * D), lambda i: (i, 0))] + [_whole((1, D))] * 4 + [_whole((HEADS, CHUNK, CHUNK)), _whole((CHUNK, CHUNK))],
        t // CHUNK, rider=rider, aliases={9: 0})


def _halo_before(tr, cb):
    return pl.BlockSpec((HALO, D), lambda i: (jnp.maximum(i * (tr // HALO) - 1, 0), cb))


def _halo_after(tr, cb, n_tiles):
    return pl.BlockSpec((HALO, D), lambda i: (jnp.minimum((i + 1) * (tr // HALO), n_tiles * (tr // HALO) - 1), cb))


def _ln_silu(z, g, b):
    return _silu(_layer_norm(z, g, b))


SUBLANES = 8
LANES = 128
CONV_STRIP = 16
DW_STRIP = 32


def _shifted_copies(buf, copies, rows):
    for b in range(1, SUBLANES):
        copies[b - 1, pl.ds(0, rows), :] = buf[pl.ds(b, rows), :]


def _shifted(buf, copies, offset, start, rows, lanes=slice(None)):
    at = pl.ds(pl.multiple_of(start + SUBLANES * (offset // SUBLANES), SUBLANES), rows)
    return buf[at, lanes] if offset % SUBLANES == 0 else copies[offset % SUBLANES - 1, at, lanes]


def _accumulate(o, v, i):
    @pl.when(i == 0)
    def _():
        o[...] = v.astype(o.dtype)

    @pl.when(i > 0)
    def _():
        o[...] += v.astype(o.dtype)


def _conv(name, proj, b_in, conv_w, conv_b, ln_g, ln_b, rider=None):
    t = proj.shape[0]
    tr = min(t, 256)

    def compute(r, outs, scr):
        zbuf, zs = scr
        i = pl.program_id(0)
        bv, bg = r[4][...], r[5][...]
        z0 = (r[0][...] + bv) * jax.nn.sigmoid(r[1][...] + bg)
        before = (r[2][...] + bv) * jax.nn.sigmoid(r[3][...] + bg)
        zbuf[pl.ds(0, HALO), :] = jnp.where(i > 0, before, 0.0)
        zbuf[pl.ds(HALO, tr), :] = z0
        outs[0][...] = z0
        _shifted_copies(zbuf, zs, tr + HALO - SUBLANES)

        def strip(s, carry):
            r0 = s * CONV_STRIP
            acc = jnp.zeros((CONV_STRIP, D), F32) + r[7][...]
            for k in range(KW):
                acc = acc + r[6][k:k + 1, :] * _shifted(zbuf, zs, HALO - (KW - 1) + k, r0, CONV_STRIP)
            outs[1][pl.ds(pl.multiple_of(r0, SUBLANES), CONV_STRIP), :] = acc
            return carry

        lax.fori_loop(0, tr // CONV_STRIP, strip, 0)
        outs[2][...] = _ln_silu(outs[1][...], r[8][...], r[9][...]).astype(BF)

    return _call(
        name, compute, (t // tr,), [proj, proj, proj, proj, b_in, b_in, conv_w, conv_b, ln_g, ln_b],
        [_rows(tr, D, 2), _rows(tr, D, 3), _halo_before(tr, 2), _halo_before(tr, 3),
         pl.BlockSpec((1, D), lambda i: (0, 2)), pl.BlockSpec((1, D), lambda i: (0, 3)),
         _whole((HALO, D)), _whole((1, D)), _whole((1, D)), _whole((1, D))],
        [jax.ShapeDtypeStruct((t, D), F32), jax.ShapeDtypeStruct((t, D), F32), jax.ShapeDtypeStruct((t, D), BF)],
        [_rows(tr)] * 3, [pltpu.VMEM((tr + HALO, D), F32), pltpu.VMEM((SUBLANES - 1, tr + HALO, D), F32)], ("arbitrary",), rider)


def _conv_bwd(name, proj, b_in, conv_w, ln_g, ln_b, z0, z1, dz3, dproj, rider=None):
    t = proj.shape[0]
    tr = min(t, 256)
    n_tiles = t // tr

    def compute(r, outs, scr):
        zbuf, dbuf, zs, ds, dwacc = scr
        i = pl.program_id(0)
        g, b = r[5][...], r[6][...]
        zero_row = jnp.zeros((1, D), F32)
        _, vjp = jax.vjp(_ln_silu, r[9][...], g, b)
        dz1, dg, db = vjp(r[11][...])
        dcb = jnp.sum(dz1, axis=0, keepdims=True)
        _, vjp_after = jax.vjp(_ln_silu, r[10][...], g, b)
        dz1_after = vjp_after(r[12][...])[0]
        dbuf[pl.ds(0, tr), :] = dz1
        dbuf[pl.ds(tr, HALO), :] = jnp.where(i < n_tiles - 1, dz1_after, 0.0)
        zbuf[pl.ds(0, HALO), :] = jnp.where(i > 0, r[8][...], 0.0)
        zbuf[pl.ds(HALO, tr), :] = r[7][...]
        _shifted_copies(dbuf, ds, tr + HALO - SUBLANES)
        _shifted_copies(zbuf, zs, tr + HALO - SUBLANES)

        def dz0_strip(s, carry):
            r0 = s * CONV_STRIP
            at = pl.ds(pl.multiple_of(r0, CONV_STRIP), CONV_STRIP)
            acc = jnp.zeros((CONV_STRIP, D), F32)
            for k in range(KW):
                acc = acc + r[4][k:k + 1, :] * _shifted(dbuf, ds, KW - 1 - k, r0, CONV_STRIP)
            a = r[0][at, :] + r[2][...]
            sg = jax.nn.sigmoid(r[1][at, :] + r[3][...])
            dcv = acc * sg
            dcg = acc * a * sg * (1.0 - sg)
            outs[0][at, :] = jnp.concatenate([dcv, dcg], axis=1).astype(BF)
            return carry[0] + jnp.sum(dcv, axis=0, keepdims=True), carry[1] + jnp.sum(dcg, axis=0, keepdims=True)

        dbv, dbg = lax.fori_loop(0, tr // CONV_STRIP, dz0_strip, (zero_row, zero_row))

        for lb in range(D // LANES):
            lanes = slice(lb * LANES, (lb + 1) * LANES)

            def dw_strip(s, accs, lanes=lanes):
                r0 = s * DW_STRIP
                dz = dbuf[pl.ds(pl.multiple_of(r0, SUBLANES), DW_STRIP), lanes]
                out = []
                for k in range(KW):
                    prod = dz * _shifted(zbuf, zs, HALO - (KW - 1) + k, r0, DW_STRIP, lanes)
                    part = prod[0:SUBLANES]
                    for q in range(1, DW_STRIP // SUBLANES):
                        part = part + prod[q * SUBLANES:(q + 1) * SUBLANES]
                    out.append(accs[k] + part)
                return tuple(out)

            accs = lax.fori_loop(0, tr // DW_STRIP, dw_strip, tuple(jnp.zeros((SUBLANES, LANES), F32) for _ in range(KW)))
            for k in range(KW):
                dwacc[pl.ds(k * SUBLANES, SUBLANES), lanes] = accs[k]
        dw_rows = [jnp.sum(dwacc[pl.ds(k * SUBLANES, SUBLANES), :], axis=0, keepdims=True) for k in range(KW)]
        dw_rows.append(jnp.zeros((HALO - KW, D), F32))
        for o, v in zip(outs[1:], (dbv, dbg, jnp.concatenate(dw_rows, axis=0), dcb, dg, db)):
            _accumulate(o, v, i)

    wide = pl.BlockSpec((tr, 2 * D), lambda i: (i, 1))
    return _call(
        name, compute, (n_tiles,), [proj, proj, b_in, b_in, conv_w, ln_g, ln_b, z0, z0, z1, z1, dz3, dz3, dproj],
        [_rows(tr, D, 2), _rows(tr, D, 3), pl.BlockSpec((1, D), lambda i: (0, 2)), pl.BlockSpec((1, D), lambda i: (0, 3)),
         _whole((HALO, D)), _whole((1, D)), _whole((1, D)),
         _rows(tr), _halo_before(tr, 0), _rows(tr), _halo_after(tr, 0, n_tiles), _rows(tr), _halo_after(tr, 0, n_tiles),
         pl.BlockSpec(memory_space=pl.ANY)],
        [jax.ShapeDtypeStruct(dproj.shape, dproj.dtype), _vec(), _vec(), _vec(HALO), _vec(), _vec(), _vec()],
        [wide] + [_whole((1, D))] * 2 + [_whole((HALO, D))] + [_whole((1, D))] * 3,
        [pltpu.VMEM((tr + HALO, D), F32), pltpu.VMEM((tr + HALO, D), F32),
         pltpu.VMEM((SUBLANES - 1, tr + HALO, D), F32), pltpu.VMEM((SUBLANES - 1, tr + HALO, D), F32),
         pltpu.VMEM((HALO * SUBLANES, D), F32)],
        ("arbitrary",), rider, aliases={13: 0})


def _merge_fn(ga, gb, bga, bgb, ya, yb):
    return jax.nn.sigmoid(ga + bga) * ya + jax.nn.sigmoid(gb + bgb) * yb


def _mix_tail(name, ua, z3, proj, b_in, wa, wb, wo, x, g, next_norm):
    t = ua.shape[0]
    tr = min(t, 256)

    def compute(r, outs, _):
        ya = jnp.dot(r[0][...], r[6][...], preferred_element_type=F32)
        yb = jnp.dot(r[1][...], r[7][...], preferred_element_type=F32)
        merged = _merge_fn(r[2][...], r[3][...], r[4][...], r[5][...], ya, yb).astype(BF)
        y = jnp.dot(merged, r[8][...], preferred_element_type=F32)
        x_out = r[9][...] + r[10][...] * y
        for o, v in zip(outs, (ya, yb, merged, y, x_out, _rms_mod(x_out, r[11][...], r[12][...], r[13][...]))):
            o[...] = v.astype(o.dtype)

    row = _whole((1, D))
    return _call(
        name, compute, (t // tr,), [ua, z3, proj, proj, b_in, b_in, wa, wb, wo, x, g, *next_norm],
        [_rows(tr), _rows(tr), _rows(tr, D, 4), _rows(tr, D, 5), pl.BlockSpec((1, D), lambda i: (0, 4)),
         pl.BlockSpec((1, D), lambda i: (0, 5)), _whole((D, D)), _whole((D, D)), _whole((D, D)), _rows(tr), row, row, row, row],
        [jax.ShapeDtypeStruct((t, D), dt) for dt in (F32, F32, BF, BF, F32, BF)], [_rows(tr)] * 6, [], ("arbitrary",))


def _mix_tail_bwd(name, dy, proj, b_in, ya, yb, wa, wb, wo, rider=None):
    t = proj.shape[0]
    tr = min(t, 256)

    def compute(r, outs, _):
        i = pl.program_id(0)
        dm = lax.dot_general(r[0][...], r[9][...], NT, preferred_element_type=F32)
        _, vjp = jax.vjp(_merge_fn, *[x[...] for x in r[1:7]])
        dga, dgb, dbga, dbgb, dya, dyb = vjp(dm)
        dya, dyb = dya.astype(BF), dyb.astype(BF)
        outs[0][...] = jnp.concatenate([dga, dgb], axis=1).astype(BF)
        outs[1][...] = dya
        outs[2][...] = dyb
        outs[3][...] = lax.dot_general(dya, r[7][...], NT, preferred_element_type=F32)
        outs[4][...] = lax.dot_general(dyb, r[8][...], NT, preferred_element_type=F32)
        _accumulate(outs[5], dbga, i)
        _accumulate(outs[6], dbgb, i)

    return _call(
        name, compute, (t // tr,), [dy, proj, proj, b_in, b_in, ya, yb, wa, wb, wo],
        [_rows(tr), _rows(tr, D, 4), _rows(tr, D, 5), pl.BlockSpec((1, D), lambda i: (0, 4)), pl.BlockSpec((1, D), lambda i: (0, 5)),
         _rows(tr), _rows(tr), _whole((D, D)), _whole((D, D)), _whole((D, D))],
        [jax.ShapeDtypeStruct((t, D_IN), BF)] + [jax.ShapeDtypeStruct((t, D), BF)] * 2 + [jax.ShapeDtypeStruct((t, D), F32)] * 2
        + [_vec(), _vec()],
        [pl.BlockSpec((tr, 2 * D), lambda i: (i, 2))] + [_rows(tr)] * 4 + [_whole((1, D))] * 2, [], ("arbitrary",), rider)


def _loss_head(name, x, gain, target, f, g, scale):
    t = x.shape[0]
    tr = min(t, 256)

    def loss_fn(xv, gn, tgt):
        y = xv * lax.rsqrt(jnp.mean(xv * xv, axis=-1, keepdims=True) + EPS) * gn
        return 0.5 * jnp.sum(jnp.mean(jnp.square(y - tgt), axis=-1))

    def fn(i, r, _):
        loss, vjp = jax.vjp(loss_fn, r[0][...], r[1][...], r[2][...])
        dx, dgain, _ = vjp(jnp.ones((), F32))
        df, dg = _gate_grads(dx, r[3][...], r[4][...], scale)
        return [dx, df, dgain, jnp.zeros((1, D), F32) + loss, dg]

    return _rowcall(name, fn, [x, gain, target, f, g], [_rows(tr), _whole((1, D)), _rows(tr), _rows(tr), _whole((1, D))], 2,
                    [jax.ShapeDtypeStruct((t, D), F32), jax.ShapeDtypeStruct((t, D), BF), _vec(), _vec(), _vec()],
                    [_rows(tr)] * 2 + [_whole((1, D))] * 3, t // tr)


def _adamw(w, g, m, v):
    m = B1 * m + (1.0 - B1) * g
    v = B2 * v + (1.0 - B2) * jnp.square(g)
    m_hat = m / BC1
    v_hat = v / BC2
    delta = -LR * (m_hat / (jnp.sqrt(v_hat) + ADAM_EPS) + WD * w)
    return delta, m, v


ADAMW_ROWS = 64


def _adamw_group(name, items, rider=None, rows=ADAMW_ROWS):
    ins, in_specs, out_shapes, out_specs, plan = [], [], [], [], []
    first = 0
    for chip_sum, received, w, m, v in items:
        r, c = w.shape
        tr = min(r, rows)
        n = r // tr

        def tile(i, first=first, n=n):
            return jnp.clip(i - first, 0, n - 1)

        spec = pl.BlockSpec((tr, c), lambda i, tile=tile: (tile(i), 0))
        ins += [chip_sum, *received, w, m, v]
        in_specs += [pl.BlockSpec((None, tr, c), lambda i, tile=tile: (0, tile(i), 0))]
        in_specs += [pl.BlockSpec((g.shape[0], tr, c), lambda i, tile=tile: (0, tile(i), 0)) for g in received]
        in_specs += [spec] * 3
        out_shapes += [jax.ShapeDtypeStruct((r, c), F32)] * 4
        out_specs += [spec] * 4
        plan.append((first, n, [g.shape[0] for g in received]))
        first += n

    def compute(in_refs, out_refs, _):
        i = pl.program_id(0)
        at_in = at_out = 0
        for start, n, counts in plan:
            mine = in_refs[at_in:at_in + 4 + len(counts)]
            outs = out_refs[at_out:at_out + 4]
            at_in += 4 + len(counts)
            at_out += 4

            @pl.when(jnp.logical_and(i >= start, i < start + n))
            def _(mine=mine, outs=outs, counts=counts):
                g = mine[0][...].astype(F32)
                for j, count in enumerate(counts):
                    for s in range(count):
                        g = g + mine[1 + j][s].astype(F32)
                delta, m_new, v_new = _adamw(mine[-3][...], g, mine[-2][...], mine[-1][...])
                for o, val in zip(outs, (g, delta, m_new, v_new)):
                    o[...] = val

    res = _call(name, compute, (first,), ins, in_specs, out_shapes, out_specs, [], ("arbitrary",), rider)
    outs, rode = res if rider else (res, [])
    return [outs[4 * j:4 * j + 4] for j in range(len(items))], rode


def _adamw_small(name, packed_all, late_all, dws_all, vectors, w_s):
    n_vec = len(vectors)

    def body(*refs):
        p_ref, l_ref, d_ref = refs[:3]
        param_refs = refs[3:3 + 3 * n_vec + 3]
        out_refs = refs[3 + 3 * n_vec + 3:-1]
        g_ref = refs[-1]
        g = p_ref[0]
        late = l_ref[0]
        for s in range(1, NDEV):
            g = g + p_ref[s]
            late = late + l_ref[s]
        g_ref[...] = g
        g_ref[pl.ds(0, R_LATE), :] += late

        def update(gp, wmv, outs):
            delta, m_new, v_new = _adamw(wmv[0][...], gp, wmv[1][...], wmv[2][...])
            for o, val in zip(outs, (gp, delta, m_new, v_new)):
                o[...] = val

        for j, (row, rows, *_) in enumerate(vectors):
            pieces = [g_ref[pl.ds(row + r, 1), :] for r in range(rows)]
            update(pieces[0] if rows == 1 else jnp.concatenate(pieces, axis=1), param_refs[3 * j:3 * j + 3], out_refs[4 * j:4 * j + 4])
        gw = d_ref[0]
        for s in range(1, NDEV):
            gw = gw + d_ref[s]
        update(gw, param_refs[3 * n_vec:], out_refs[4 * n_vec:4 * n_vec + 4])
        out_refs[-2][...] = g_ref[pl.ds(R_CW, KW), :]
        out_refs[-1][...] = g_ref[pl.ds(R_LOSS, 1), :]

    params = [a for _, _, w, m, v in vectors for a in (w, m, v)] + list(w_s)
    out_shapes = [jax.ShapeDtypeStruct(w.shape, F32) for _, _, w, _, _ in vectors for _ in range(4)]
    out_shapes += [jax.ShapeDtypeStruct(w_s[0].shape, F32)] * 4 + [jax.ShapeDtypeStruct((KW, D), F32), _vec()]
    res = pl.pallas_call(body, name=name, out_shape=out_shapes, scratch_shapes=[pltpu.VMEM((R_TOTAL, D), F32)],
                         compiler_params=_params(None))(packed_all, late_all, dws_all, *params)
    return [res[4 * j:4 * j + 4] for j in range(n_vec + 1)], res[-2], res[-1]


def _adamw_plain(name, g, w, m, v):
    def body(g_ref, w_ref, m_ref, v_ref, d_ref, mo_ref, vo_ref):
        delta, m_new, v_new = _adamw(w_ref[...], g_ref[...], m_ref[...], v_ref[...])
        d_ref[...] = delta
        mo_ref[...] = m_new
        vo_ref[...] = v_new

    return pl.pallas_call(body, name=name, out_shape=[jax.ShapeDtypeStruct(w.shape, F32)] * 3,
                          compiler_params=_params(None))(g, w, m, v)


def _adamw_ada(name, c_all_t, dmod, dmod_late, w, m, v):
    r, c = w.shape
    tr = 256

    def fn(i, refs, _):
        ca = _silu(refs[0][...])
        dm = refs[1][...] + refs[2][...]
        g = ca[:, 0:1] * dm[0:1, :]
        for b in range(1, NDEV):
            g = g + ca[:, b:b + 1] * dm[b:b + 1, :]
        delta, m_new, v_new = _adamw(refs[3][...], g, refs[4][...], refs[5][...])
        return [g, delta, m_new, v_new]

    spec = pl.BlockSpec((tr, c), lambda i: (i, 0))
    whole = pl.BlockSpec((NDEV, c), lambda i: (0, 0))
    return _rowcall(name, fn, [c_all_t, dmod, dmod_late, w, m, v],
                    [pl.BlockSpec((tr, NDEV), lambda i: (i, 0)), whole, whole, spec, spec, spec], 4,
                    [jax.ShapeDtypeStruct((r, c), F32)] * 4, [spec] * 4, r // tr)


def _ffn_fwd(tag, x, h, g, wg, wu, wd_shard, down_rider, next_norm=None):
    t = x.shape[0]
    tm = min(t, 512 if down_rider else 1024)
    (gate, up, act), (wd,) = _ffn_up(f"{tag}_up", h, wg, wu, rider=_gather_rider([wd_shard]))
    row = pl.BlockSpec((1, D), lambda i, j, k: (0, 0))

    def epilogue(f, xv, gv, *norm):
        x_out = xv + 0.5 * gv * f
        return (x_out, f, _rms_mod(x_out, *norm)) if norm else (x_out, f)

    res = _mm_nn(f"{tag}_down", act, wd.reshape(F, D), tm, D, 1024, extras=(x, g, *(next_norm or ())),
                 extra_specs=(pl.BlockSpec((tm, D), lambda i, j, k: (i, 0)), row, *([row] * 3 if next_norm else [])),
                 epilogue=epilogue, out_dtypes=(F32, BF, BF) if next_norm else (F32, BF), rider=down_rider)
    (x_out, f, *h_next), rode = res if down_rider else (res, None)
    return x_out, (h_next[0] if next_norm else None), (x, h, gate, up, act, f), wd, rode


def _ffn_bwd(tag, dx_out, df, saved, gain, sh, sc, wg, wu, wd, slots, dact_rider=None, dwd_rider=None, dwgu_rider=None,
             below=None, fuse_dh=False):
    x, h, gate, up, act, f = saved
    t = x.shape[0]
    tm = min(t, 1024)
    if fuse_dh:
        dgate, dup, dh = _ffn_dact_dh(f"{tag}_dact_dh", df, wd.reshape(F, D), gate, up, wg, wu)
        dwd = _mm_tn(f"{tag}_dwd", act, df, 512, D).reshape(NDEV, F // NDEV, D)
        (dwg, dwu), (sib_d,) = _dw_gate_up(f"{tag}_dwgu", h, dgate, dup, rider=_pair_rider([dwd]))
        (sum_d,) = _pair_add(f"{tag}_dwd_add", [dwd], [sib_d], slots)
        normed, (sib_g, sib_u) = _norm_mod_bwd(f"{tag}_norm_bwd", x, gain, sc, sh, dh, dx_out, below=below,
                                               rider=_pair_rider([dwg, dwu]))
        sum_g, sum_u = _pair_add(f"{tag}_dwgu_add", [dwg, dwu], [sib_g, sib_u], slots)
        return normed, (sum_d, None), sum_g, sum_u, [], [], []

    blk = pl.BlockSpec((tm, F // NDEV), lambda i, j, k: (i, j))
    res = _mm_nt(f"{tag}_dact", df, wd.reshape(F, D), tm, F // NDEV, out_dtypes=(BF, BF),
                 extras=(gate, up), extra_specs=(blk, blk), epilogue=_swiglu_bwd, rider=dact_rider)
    (dgate, dup), rode_dact = res if dact_rider else (res, [])
    res = _mm_tn(f"{tag}_dwd", act, df, 512, D, rider=dwd_rider)
    dwd, rode_dwd = res if dwd_rider else (res, [])
    dwd = dwd.reshape(NDEV, F // NDEV, D)
    (dwg, dwu), (sib_d, *rode_dwgu) = _dw_gate_up(f"{tag}_dwgu", h, dgate, dup,
                                                  rider=[_pair_rider([dwd])] + ([dwgu_rider] if dwgu_rider else []))
    (sum_d,) = _pair_add(f"{tag}_dwd_add", [dwd], [sib_d], slots)
    dh, (sib_g, sib_u, got_d) = _mm_nt_blocked(f"{tag}_dh", [dgate, dup], [wg, wu], tm,
                                               rider=[_pair_rider([dwg, dwu]), _chip_rider([sum_d])])
    sum_g, sum_u = _pair_add(f"{tag}_dwgu_add", [dwg, dwu], [sib_g, sib_u], slots)
    normed = _norm_mod_bwd(f"{tag}_norm_bwd", x, gain, sc, sh, dh, dx_out, below=below)
    return normed, (sum_d, [got_d]), sum_g, sum_u, rode_dact, rode_dwd, rode_dwgu


def kernel(x, c, ada_w, ada_b, norm_ffn1, ffn1_w_gate, ffn1_w_up, ffn1_w_down, norm_mix, mix_w_in, mix_b_in, sgu_ln_g, sgu_ln_b, sgu_w_s, sgu_b_s, conv_w, conv_b, conv_ln_g, conv_ln_b, w_branch_a, w_branch_b, w_out, norm_ffn2, ffn2_w_gate, ffn2_w_up, ffn2_w_down, norm_final, loss_target, m_ada_w, m_ada_b, m_norm_ffn1, m_ffn1_w_gate, m_ffn1_w_up, m_ffn1_w_down, m_norm_mix, m_mix_w_in, m_mix_b_in, m_sgu_ln_g, m_sgu_ln_b, m_sgu_w_s, m_sgu_b_s, m_conv_w, m_conv_b, m_conv_ln_g, m_conv_ln_b, m_w_branch_a, m_w_branch_b, m_w_out, m_norm_ffn2, m_ffn2_w_gate, m_ffn2_w_up, m_ffn2_w_down, m_norm_final, v_ada_w, v_ada_b, v_norm_ffn1, v_ffn1_w_gate, v_ffn1_w_up, v_ffn1_w_down, v_norm_mix, v_mix_w_in, v_mix_b_in, v_sgu_ln_g, v_sgu_ln_b, v_sgu_w_s, v_sgu_b_s, v_conv_w, v_conv_b, v_conv_ln_g, v_conv_ln_b, v_w_branch_a, v_w_branch_b, v_w_out, v_norm_ffn2, v_ffn2_w_gate, v_ffn2_w_up, v_ffn2_w_down, v_norm_final):
    mx, my, mc = _position()
    me = 4 * mx + 2 * my + mc
    chip = 2 * mx + my
    slots = jnp.stack([2 * (chip ^ k) + mc for k in range(N_CHIPS)]).astype(jnp.int32)
    t = x.shape[1]
    tm = min(t, 1024)
    x0 = x.reshape(t, D)
    target = loss_target.reshape(t, D)
    given = dict(ffn1_w_gate=(ffn1_w_gate, m_ffn1_w_gate, v_ffn1_w_gate), ffn1_w_up=(ffn1_w_up, m_ffn1_w_up, v_ffn1_w_up),
                 ffn1_w_down=(ffn1_w_down, m_ffn1_w_down, v_ffn1_w_down), mix_w_in=(mix_w_in, m_mix_w_in, v_mix_w_in),
                 w_branch_a=(w_branch_a, m_w_branch_a, v_w_branch_a), w_branch_b=(w_branch_b, m_w_branch_b, v_w_branch_b),
                 w_out=(w_out, m_w_out, v_w_out), ffn2_w_gate=(ffn2_w_gate, m_ffn2_w_gate, v_ffn2_w_gate),
                 ffn2_w_up=(ffn2_w_up, m_ffn2_w_up, v_ffn2_w_up), ffn2_w_down=(ffn2_w_down, m_ffn2_w_down, v_ffn2_w_down))
    shard = {n: wmv[0][0].astype(BF) for n, wmv in given.items()}

    ada_cols = N_MOD * D // NDEV
    c_all, taps_all, mod_all, (wg1, wu1) = _prologue(
        "prologue", jnp.pad(c, ((0, SUBLANES - 1), (0, 0))), jnp.pad(conv_w[0], ((0, HALO - KW), (0, 0))), ada_w[0],
        lax.dynamic_slice(ada_b, (0, me * ada_cols), (1, ada_cols)), [shard["ffn1_w_gate"], shard["ffn1_w_up"]])
    conv_w_full = jnp.transpose(taps_all.reshape(NDEV, HALO, CHUNK), (1, 0, 2)).reshape(HALO, D)
    mod = lax.dynamic_index_in_dim(mod_all.reshape(NDEV, NDEV, ada_cols), me, axis=1, keepdims=False).reshape(N_MOD, 1, D)
    sh1, sc1, g1, sh2, sc2, g2, sh3, sc3, g3 = [mod[i] for i in range(N_MOD)]

    h1 = _norm_mod("ffn1_norm", x0, norm_ffn1, sc1, sh1)
    x1, h2, saved1, wd1, (w_in,) = _ffn_fwd("ffn1", x0, h1, g1, wg1, wu1, shard["ffn1_w_down"],
                                             _gather_rider([shard["mix_w_in"]]), next_norm=(norm_mix, sc2, sh2))
    proj, (wg2, wa3, wb3) = _mm_nn_blocked(
        "mix_in", h2, w_in, tm, rider=_gather_rider([shard["ffn2_w_gate"], shard["w_branch_a"], shard["w_branch_b"]]))
    bias_full = jnp.repeat(sgu_b_s[0].T, CHUNK, axis=1)
    (ua,) = _sgu("sgu", proj, mix_b_in, sgu_ln_g, sgu_ln_b, sgu_w_s[0], bias_full)
    (z0, z1, z3), (wu2, wo3) = _conv("conv", proj, mix_b_in, conv_w_full, conv_b, conv_ln_g, conv_ln_b,
                                     rider=_gather_rider([shard["ffn2_w_up"], shard["w_out"]]))
    wa, wb, wo = wa3.reshape(D, D), wb3.reshape(D, D), wo3.reshape(D, D)
    ya, yb, merged, y, x2, h3 = _mix_tail("mix_tail", ua, z3, proj, mix_b_in, wa, wb, wo, x1, g2, (norm_ffn2, sc3, sh3))
    x3, _, saved3, wd2, _ = _ffn_fwd("ffn2", x2, h3, g3, wg2, wu2, shard["ffn2_w_down"], None)

    norm_final2 = norm_final.reshape(1, D)
    dx3, df3, d_norm_final, loss_row, dg3 = _loss_head("loss_head", x3, norm_final2, target, saved3[-1], g3, 0.5)
    (dx2, dy, d_norm_ffn2, dsc3, dsh3, dg2), (sum_d2, _), sum_g2, sum_u2, _, _, _ = _ffn_bwd(
        "ffn2", dx3, df3, saved3, norm_ffn2, sh3, sc3, wg2, wu2, wd2, slots, below=(y, g2, 1.0), fuse_dh=True)
    (dproj, dya, dyb, dua, dz3, db_ga, db_gb), (got_g2_near,) = _mix_tail_bwd(
        "mix_tail_bwd", dy, proj, mix_b_in, ya, yb, wa, wb, wo, rider=_chip_rider([sum_g2], NEIGHBOURS))
    dwo = _mm_tn("mix_dwo", merged, dy, 512, D).reshape(NDEV, D // NDEV, D)
    dwa = _mm_tn("branch_dwa", ua, dya, 512, D).reshape(NDEV, D // NDEV, D)
    dwb = _mm_tn("branch_dwb", z3, dyb, 512, D).reshape(NDEV, D // NDEV, D)
    (dproj, db_u, db_v, d_sgu_g, d_sgu_b, d_ws, d_bs_t), (*sib_abo, got_g2_far) = _sgu_bwd(
        "sgu_bwd", proj, mix_b_in, sgu_ln_g, sgu_ln_b, sgu_w_s[0], bias_full, dua, dproj,
        rider=[_pair_rider([dwa, dwb, dwo]), _chip_rider([sum_g2], DIAGONAL)])
    sum_a, sum_b, sum_o = _pair_add("mix_dw_add", [dwa, dwb, dwo], sib_abo, slots)
    (dproj, db_cv, db_cg, d_cw, d_cb, d_cln_g, d_cln_b), (got_u2, got_d2) = _conv_bwd(
        "conv_bwd", proj, mix_b_in, conv_w_full, conv_ln_g, conv_ln_b, z0, z1, dz3, dproj, rider=_chip_rider([sum_u2, sum_d2]))
    dwin, (got_a, got_b, got_o) = _mm_tn_blocked("mix_dwin", h2, dproj, rider=_chip_rider([sum_a, sum_b, sum_o]))

    d_bs = jnp.transpose(d_bs_t[:, :HEADS])
    zero = jnp.zeros((1, D), F32)
    pack_rows = [zero, zero, zero, zero, zero, dg2, dsh3, dsc3, dg3,
                 zero, zero, d_norm_ffn2, d_norm_final,
                 db_u, db_v, db_cv, db_cg, db_ga, db_gb,
                 d_sgu_g, d_sgu_b, d_bs.reshape(1, D), d_cb, d_cln_g, d_cln_b,
                 d_cw[:KW], loss_row, jnp.zeros((R_TOTAL - R_LOSS - 1, D), F32)]
    packed = jnp.concatenate(pack_rows, axis=0)
    d_ws2 = d_ws.reshape(HEADS * CHUNK, CHUNK)
    dh2, (sib_in, packed_all, dws_all) = _mm_nt_blocked("mix_in_bwd", [dproj], [w_in], tm,
                                                        rider=[_pair_rider([dwin]), _gather_rider([packed, d_ws2])])
    (sum_in,) = _pair_add("mix_dwin_add", [dwin], [sib_in], slots)
    dx1, df1, d_norm_mix, dsc2, dsh2, dg1 = _norm_mod_bwd("mix_norm_bwd", x1, norm_mix, sc2, sh2, dh2, dx2,
                                                          below=(saved1[-1], g1, 0.5))
    (dx0, d_norm_ffn1, dsc1, dsh1), down1, sum_g1, sum_u1, (got_in_near,), _, (got_in_far,) = _ffn_bwd(
        "ffn1", dx1, df1, saved1, norm_ffn1, sh1, sc1, wg1, wu1, wd1, slots,
        dact_rider=_chip_rider([sum_in], NEIGHBOURS), dwgu_rider=_chip_rider([sum_in], DIAGONAL))
    packed_late = jnp.concatenate([dsh1, dsc1, dg1, dsh2, dsc2, jnp.zeros((4, D), F32), d_norm_ffn1, d_norm_mix,
                                   jnp.zeros((R_LATE - 11, D), F32)], axis=0)
    grads = dict(ffn2_w_gate=(sum_g2, [got_g2_near, got_g2_far]), ffn2_w_up=(sum_u2, [got_u2]), ffn2_w_down=(sum_d2, [got_d2]),
                 mix_w_in=(sum_in, [got_in_near, got_in_far]), w_branch_a=(sum_a, [got_a]), w_branch_b=(sum_b, [got_b]),
                 w_out=(sum_o, [got_o]), ffn1_w_down=down1)
    done, (late_all, got_g1, got_u1) = _adamw_group(
        "adamw_most", [(cs, got, *[a[0] for a in given[n]]) for n, (cs, got) in grads.items()],
        rider=[_gather_rider([packed_late]), _chip_rider([sum_g1, sum_u1])])
    last, _ = _adamw_group("adamw_ffn1_in", [(sum_g1, [got_g1], *[a[0] for a in given["ffn1_w_gate"]]),
                                            (sum_u1, [got_u1], *[a[0] for a in given["ffn1_w_up"]])], rows=256)
    big_out = {n: [o.reshape(given[n][0].shape) for o in outs]
               for n, outs in zip([*grads, "ffn1_w_gate", "ffn1_w_up"], [*done, *last])}

    flat = lambda a: a.reshape(1, -1)
    vectors = [("ada_b", 0, 9, ada_b, m_ada_b, v_ada_b), ("norm_ffn1", 9, 1, norm_ffn1, m_norm_ffn1, v_norm_ffn1),
               ("norm_mix", 10, 1, norm_mix, m_norm_mix, v_norm_mix), ("norm_ffn2", 11, 1, norm_ffn2, m_norm_ffn2, v_norm_ffn2),
               ("norm_final", 12, 1, norm_final, m_norm_final, v_norm_final), ("mix_b_in", 13, 6, mix_b_in, m_mix_b_in, v_mix_b_in),
               ("sgu_ln_g", 19, 1, sgu_ln_g, m_sgu_ln_g, v_sgu_ln_g), ("sgu_ln_b", 20, 1, sgu_ln_b, m_sgu_ln_b, v_sgu_ln_b),
               ("sgu_b_s", 21, 1, sgu_b_s, m_sgu_b_s, v_sgu_b_s), ("conv_b", 22, 1, conv_b, m_conv_b, v_conv_b),
               ("conv_ln_g", 23, 1, conv_ln_g, m_conv_ln_g, v_conv_ln_g), ("conv_ln_b", 24, 1, conv_ln_b, m_conv_ln_b, v_conv_ln_b)]
    small_out, d_cw_all, loss_sum = _adamw_small(
        "adamw_small", packed_all, late_all, dws_all, [(row, rows, flat(wv), flat(mv), flat(vv)) for _, row, rows, wv, mv, vv in vectors],
        [a.reshape(HEADS * CHUNK, CHUNK) for a in (sgu_w_s, m_sgu_w_s, v_sgu_w_s)])
    small = {n: [o.reshape(wv.shape) for o in outs] for (n, _, _, wv, _, _), outs in zip(vectors, small_out)}
    small["sgu_w_s"] = [o.reshape(sgu_w_s.shape) for o in small_out[-1]]
    g_cw = lax.dynamic_slice(d_cw_all, (0, me * CHUNK), (KW, CHUNK))
    small["conv_w"] = [o.reshape(conv_w.shape) for o in (g_cw, *_adamw_plain("adamw_conv_w", g_cw, conv_w[0], m_conv_w[0], v_conv_w[0]))]
    loss = loss_sum[0, 0]

    dmod_cols = [lax.dynamic_slice(a[:, :N_MOD, :].reshape(NDEV, N_MOD * D), (0, me * ada_cols), (NDEV, ada_cols))
                 for a in (packed_all, late_all)]
    ada_out = [o.reshape(ada_w.shape) for o in _adamw_ada("adamw_ada_w", jnp.transpose(c_all), *dmod_cols, ada_w[0], m_ada_w[0], v_ada_w[0])]

    order = ["ada_w", "ada_b", "norm_ffn1", "ffn1_w_gate", "ffn1_w_up", "ffn1_w_down", "norm_mix", "mix_w_in", "mix_b_in",
             "sgu_ln_g", "sgu_ln_b", "sgu_w_s", "sgu_b_s", "conv_w", "conv_b", "conv_ln_g", "conv_ln_b", "w_branch_a",
             "w_branch_b", "w_out", "norm_ffn2", "ffn2_w_gate", "ffn2_w_up", "ffn2_w_down", "norm_final"]

    def leaf(n, kind):
        if n == "ada_w":
            return ada_out[kind]
        if n in big_out:
            return big_out[n][kind]
        return small[n][kind]

    return (loss, dx0.reshape(x.shape), *[leaf(n, kind) for kind in range(4) for n in order])
```

```python
import jax
import jax.numpy as jnp
from jax import lax
from jax.experimental import pallas as pl
from jax.experimental.pallas import tpu as pltpu

D = 1024
F = 4 * D
D_IN = 6 * D
HEADS = 8
CHUNK = 128
KW = 31
HALO = 32
N_MOD = 9
NDEV = 8
N_CHIPS = 4
EPS = 1e-6
LR, B1, B2, ADAM_EPS, WD, STEP = 0.001, 0.9, 0.999, 1e-08, 0.01, 10
BC1 = 1.0 - B1 ** STEP
BC2 = 1.0 - B2 ** STEP
VMEM_LIMIT = 56 * 1024 * 1024
MESH = pl.DeviceIdType.MESH
HBM = pl.BlockSpec(memory_space=pltpu.HBM)
VMEM = pl.BlockSpec(memory_space=pltpu.VMEM)
BF = jnp.bfloat16
F32 = jnp.float32

NN = (((1,), (0,)), ((), ()))
NT = (((1,), (1,)), ((), ()))
TN = (((0,), (0,)), ((), ()))

R_CW, R_LOSS, R_TOTAL = 25, 56, 64
R_LATE = 16


def _params(sem):
    return pltpu.CompilerParams(dimension_semantics=sem, vmem_limit_bytes=VMEM_LIMIT)


def _position():
    return lax.axis_index("x"), lax.axis_index("y"), lax.axis_index("c")


def _flip(pos, k):
    x, y, c = pos
    return (x ^ (k >> 2 & 1), y ^ (k >> 1 & 1), c ^ (k & 1))


def _index(pos):
    return 4 * pos[0] + 2 * pos[1] + pos[2]


def _gather_rows(x_ref, out_ref, send_sems, recv_sems, local_sem):
    m_per = x_ref.shape[0]
    x, y, c = _position()
    me, sibling = (x, y, c), (x, y, 1 - c)
    chips = [(1 - x, y), (x, 1 - y), (1 - x, 1 - y)]

    def rows(pos):
        return out_ref.at[pl.ds(_index(pos) * m_per, m_per), :]

    def copy(k, block, to, src=None):
        return pltpu.make_async_remote_copy(
            src_ref=rows(block) if src is None else src, dst_ref=rows(block),
            send_sem=send_sems.at[k], recv_sem=recv_sems.at[k], device_id=to, device_id_type=MESH)

    mine = pltpu.make_async_copy(x_ref, rows(me), local_sem)
    mine.start()
    first = [copy(0, me, sibling, src=x_ref)]
    first += [copy(1 + j, me, (*chip, c), src=x_ref) for j, chip in enumerate(chips)]
    for cp in first:
        cp.start()
    passed = [copy(4 + j, (*chip, c), sibling) for j, chip in enumerate(chips)]
    for j, chip in enumerate(chips):
        copy(1 + j, (*chip, c), me).wait_recv()
        passed[j].start()
    copy(0, sibling, me).wait_recv()
    for j, chip in enumerate(chips):
        copy(4 + j, (*chip, 1 - c), me).wait_recv()
    for cp in first + passed:
        cp.wait_send()
    mine.wait()


def _prologue(name, c_rows, taps, ada_w, ada_b, shards):
    rider = _gather_rider(shards)
    n = len(shards)
    nc = ada_w.shape[1]

    def body(*refs):
        c_ref, taps_ref, w_ref, b_ref = refs[:4]
        shard_refs = refs[4:4 + n]
        c_all_ref, taps_all_ref, mod_all_ref = refs[4 + n:7 + n]
        gathered_refs = refs[7 + n:7 + 2 * n]
        c_buf, mod_part, sems = refs[7 + 2 * n], refs[8 + 2 * n], refs[9 + 2 * n:]
        rider.start(shard_refs, gathered_refs, sems[9:])
        _gather_rows(c_ref, c_buf, *sems[0:3])
        c_all = jnp.concatenate([c_buf[pl.ds(d * SUBLANES, 1), :] for d in range(NDEV)], axis=0)
        c_all_ref[...] = c_all
        mod_part[...] = jnp.dot(_silu(c_all), w_ref[...], preferred_element_type=F32) + b_ref[...]
        _gather_rows(taps_ref, taps_all_ref, *sems[3:6])
        _gather_rows(mod_part, mod_all_ref, *sems[6:9])
        rider.mid(shard_refs, gathered_refs, sems[9:])
        rider.relay(shard_refs, gathered_refs, sems[9:])
        rider.finish(shard_refs, gathered_refs, sems[9:])

    small_sems = [pltpu.SemaphoreType.DMA((7,)), pltpu.SemaphoreType.DMA((7,)), pltpu.SemaphoreType.DMA] * 3
    res = pl.pallas_call(
        body, name=name,
        out_shape=[jax.ShapeDtypeStruct((NDEV, D), F32), jax.ShapeDtypeStruct((NDEV * taps.shape[0], taps.shape[1]), F32),
                   jax.ShapeDtypeStruct((NDEV * NDEV, nc), F32)] + rider.out_shapes,
        in_specs=[VMEM] * 4 + [HBM] * n, out_specs=[VMEM] * 3 + [HBM] * n,
        scratch_shapes=[pltpu.VMEM((NDEV * SUBLANES, D), F32), pltpu.VMEM((NDEV, nc), F32)] + small_sems + rider.sems,
        compiler_params=_params(None),
    )(c_rows, taps, ada_w, ada_b, *shards)
    return res[0], res[1], res[2], res[3:]


class _Rider:
    def __init__(self, ins, out_shapes, sems, start, finish, mid=None, relay=None):
        self.ins, self.out_shapes, self.sems = list(ins), list(out_shapes), list(sems)
        self.start, self.finish, self.mid, self.relay = start, finish, mid, relay


def _gather_rider(shards):
    n = len(shards)

    def setup(ins, outs, sems):
        send_sems, recv_sems, local_sems = sems
        x, y, c = _position()
        places = dict(me=(x, y, c), sibling=(x, y, 1 - c), xn=(1 - x, y, c), yn=(x, 1 - y, c), diagonal=(1 - x, 1 - y, c),
                      passed_on=(x ^ c, y ^ (1 - c), c), passed_to=(x ^ (1 - c), y ^ c, c))

        def copy(a, k, block, to, own=False):
            slot = outs[a].at[_index(block)]
            return pltpu.make_async_remote_copy(
                src_ref=ins[a] if own else slot, dst_ref=slot,
                send_sem=send_sems.at[k, a], recv_sem=recv_sems.at[k, a], device_id=to, device_id_type=MESH)

        def local(a):
            return pltpu.make_async_copy(ins[a], outs[a].at[_index(places["me"])], local_sems.at[a])

        return places, copy, local

    def start(ins, outs, sems):
        p, copy, local = setup(ins, outs, sems)
        for a in range(n):
            local(a).start()
            for k, to in enumerate(("sibling", "xn", "yn")):
                copy(a, k, p["me"], p[to], own=True).start()

    def mid(ins, outs, sems):
        p, copy, _ = setup(ins, outs, sems)
        for a in range(n):
            copy(a, 1, p["xn"], p["me"]).wait_recv()
            copy(a, 2, p["yn"], p["me"]).wait_recv()
            copy(a, 3, p["passed_on"], p["passed_to"]).start()
            copy(a, 4, p["xn"], p["sibling"]).start()
            copy(a, 5, p["yn"], p["sibling"]).start()

    def relay(ins, outs, sems):
        p, copy, _ = setup(ins, outs, sems)
        for a in range(n):
            copy(a, 3, p["diagonal"], p["me"]).wait_recv()
            copy(a, 6, p["diagonal"], p["sibling"]).start()

    def finish(ins, outs, sems):
        p, copy, local = setup(ins, outs, sems)
        x, y, c = p["me"]
        for a in range(n):
            for k, block in ((0, (x, y, 1 - c)), (4, (1 - x, y, 1 - c)), (5, (x, 1 - y, 1 - c)), (6, (1 - x, 1 - y, 1 - c))):
                copy(a, k, block, p["me"]).wait_recv()
            for k, to in enumerate(("sibling", "xn", "yn")):
                copy(a, k, p["me"], p[to], own=True).wait_send()
            copy(a, 3, p["passed_on"], p["passed_to"]).wait_send()
            for k, block in ((4, "xn"), (5, "yn"), (6, "diagonal")):
                copy(a, k, p[block], p["sibling"]).wait_send()
            local(a).wait()

    return _Rider(shards, [jax.ShapeDtypeStruct((NDEV, *s.shape), s.dtype) for s in shards],
                  [pltpu.SemaphoreType.DMA((7, n)), pltpu.SemaphoreType.DMA((7, n)), pltpu.SemaphoreType.DMA((n,))],
                  start, finish, mid, relay)


def _pair_rider(parts):
    n = len(parts)

    def copies(ins, outs, sems):
        send_sems, recv_sems = sems
        x, y, c = _position()
        q = 2 * x + y
        return [pltpu.make_async_remote_copy(
            src_ref=ins[a].at[2 * (q ^ k) + (1 - c)], dst_ref=outs[a].at[k],
            send_sem=send_sems.at[k, a], recv_sem=recv_sems.at[k, a], device_id=(x, y, 1 - c), device_id_type=MESH)
            for a in range(n) for k in range(N_CHIPS)]

    def start(ins, outs, sems):
        for cp in copies(ins, outs, sems):
            cp.start()

    def finish(ins, outs, sems):
        for cp in copies(ins, outs, sems):
            cp.wait()

    return _Rider(parts, [jax.ShapeDtypeStruct((N_CHIPS, *p.shape[1:]), p.dtype) for p in parts],
                  [pltpu.SemaphoreType.DMA((N_CHIPS, n)), pltpu.SemaphoreType.DMA((N_CHIPS, n))], start, finish)


NEIGHBOURS = (1, 2)
DIAGONAL = (3,)
OTHER_CHIPS = NEIGHBOURS + DIAGONAL


def _chip_rider(sums, ks=OTHER_CHIPS):
    n = len(sums)

    def copies(ins, outs, sems):
        send_sems, recv_sems = sems
        me = _position()
        return [pltpu.make_async_remote_copy(
            src_ref=ins[a].at[k], dst_ref=outs[a].at[j],
            send_sem=send_sems.at[j, a], recv_sem=recv_sems.at[j, a], device_id=_flip(me, 2 * k), device_id_type=MESH)
            for a in range(n) for j, k in enumerate(ks)]

    def start(ins, outs, sems):
        for cp in copies(ins, outs, sems):
            cp.start()

    def finish(ins, outs, sems):
        for cp in copies(ins, outs, sems):
            cp.wait()

    return _Rider(sums, [jax.ShapeDtypeStruct((len(ks), *s.shape[1:]), s.dtype) for s in sums],
                  [pltpu.SemaphoreType.DMA((len(ks), n)), pltpu.SemaphoreType.DMA((len(ks), n))], start, finish)


def _grid_edge(grid, last):
    cond = None
    for d, n in enumerate(grid):
        here = pl.program_id(d) == (n - 1 if last else 0)
        cond = here if cond is None else jnp.logical_and(cond, here)
    return cond


def _call(name, compute, grid, ins, in_specs, out_shapes, out_specs, scratch_shapes, semantics, rider=None, aliases=None):
    riders = [rider] if isinstance(rider, _Rider) else list(rider or [])
    n_in, n_out, n_scr = len(ins), len(out_shapes), len(scratch_shapes)
    n_rin, n_rout, n_rsem = [sum(len(part(r)) for r in riders) for part in (lambda r: r.ins, lambda r: r.out_shapes, lambda r: r.sems)]
    cuts = [0, n_in, n_in + n_rin, n_in + n_rin + n_out, n_in + n_rin + n_out + n_rout, n_in + n_rin + n_out + n_rout + n_scr]

    def body(*refs):
        in_refs, rin_refs, out_refs, rout_refs, scr_refs = [refs[a:b] for a, b in zip(cuts[:-1], cuts[1:])]
        rsem_refs = refs[cuts[-1]:]
        mine, at = [], [0, 0, 0]
        for r in riders:
            mine.append((r, rin_refs[at[0]:at[0] + len(r.ins)], rout_refs[at[1]:at[1] + len(r.out_shapes)],
                         rsem_refs[at[2]:at[2] + len(r.sems)]))
            at = [at[0] + len(r.ins), at[1] + len(r.out_shapes), at[2] + len(r.sems)]
        if riders:
            @pl.when(_grid_edge(grid, last=False))
            def _():
                for r, a, b, c in mine:
                    r.start(a, b, c)

        if any(r.mid for r in riders):
            step, steps = 0, 1
            for d, size in enumerate(grid):
                step, steps = step * size + pl.program_id(d), steps * size

            @pl.when(step == steps * 5 // 8)
            def _():
                for r, a, b, c in mine:
                    if r.mid:
                        r.mid(a, b, c)

        if any(r.relay for r in riders):
            @pl.when(_grid_edge(grid, last=True))
            def _():
                for r, a, b, c in mine:
                    if r.relay:
                        r.relay(a, b, c)

        compute(in_refs, out_refs, scr_refs)
        if riders:
            @pl.when(_grid_edge(grid, last=True))
            def _():
                for r, a, b, c in mine:
                    r.finish(a, b, c)

    res = pl.pallas_call(
        body, name=name, grid=grid,
        out_shape=list(out_shapes) + [s for r in riders for s in r.out_shapes],
        in_specs=list(in_specs) + [HBM] * n_rin, out_specs=list(out_specs) + [HBM] * n_rout,
        scratch_shapes=list(scratch_shapes) + [s for r in riders for s in r.sems],
        input_output_aliases=aliases or {}, compiler_params=_params(semantics),
    )(*ins, *[a for r in riders for a in r.ins])
    return (res[:n_out], res[n_out:]) if riders else res


def _pair_add(name, parts, from_sibling, slots):
    n = len(parts)

    def body(s_ref, *refs):
        for a in range(n):
            refs[2 * n + a][...] = (refs[a][...].astype(F32) + refs[n + a][...].astype(F32)).astype(refs[2 * n + a].dtype)

    def slab(p, picked):
        _, r, c = p.shape
        return pl.BlockSpec((None, r, c), (lambda k, s: (s[k], 0, 0)) if picked else (lambda k, s: (k, 0, 0)))

    return pl.pallas_call(
        body, name=name,
        grid_spec=pltpu.PrefetchScalarGridSpec(
            num_scalar_prefetch=1, grid=(N_CHIPS,),
            in_specs=[slab(p, True) for p in parts] + [slab(p, False) for p in parts],
            out_specs=[slab(p, False) for p in parts]),
        out_shape=[jax.ShapeDtypeStruct((N_CHIPS, *p.shape[1:]), p.dtype) for p in parts],
        compiler_params=_params(("arbitrary",)),
    )(slots, *parts, *from_sibling)


def _mm(name, pairs, dims, grid, nk, out_shapes, out_specs, extras=(), extra_specs=(), epilogue=None, acc_shape=None, rider=None):
    n_pairs = len(pairs)

    def compute(ins, outs, scratch):
        def partial_sum():
            total = None
            for p in range(n_pairs):
                d = lax.dot_general(ins[2 * p][...], ins[2 * p + 1][...], dims, preferred_element_type=F32)
                total = d if total is None else total + d
            return total

        def finish(r):
            ex = [e[...] for e in ins[2 * n_pairs:]]
            res = epilogue(r, *ex) if epilogue is not None else (r,)
            for o, v in zip(outs, res):
                o[...] = v.astype(o.dtype)

        if nk == 1:
            finish(partial_sum())
        else:
            acc = scratch[0]
            k = pl.program_id(2)

            @pl.when(k == 0)
            def _():
                acc[...] = partial_sum()

            @pl.when(k > 0)
            def _():
                acc[...] += partial_sum()

            @pl.when(k == nk - 1)
            def _():
                finish(acc[...])

    operands, specs = [], []
    for a, a_spec, b, b_spec in pairs:
        operands += [a, b]
        specs += [a_spec, b_spec]
    return _call(name, compute, grid, operands + list(extras), specs + list(extra_specs), out_shapes, out_specs,
                 [pltpu.VMEM(acc_shape, F32)] if nk > 1 else [], ("parallel", "parallel", "arbitrary"), rider)


def _single(res, rider):
    return (res[0][0], res[1]) if rider else res[0]


def _silu(x):
    return x * jax.nn.sigmoid(x)


def _ffn_up(name, h, wg, wu, rider=None):
    t = h.shape[0]
    tm = min(t, 1024)
    nb = F // NDEV

    def compute(ins, outs, _):
        hv = ins[0][...]
        g = jnp.dot(hv, ins[1][...], preferred_element_type=F32)
        u = jnp.dot(hv, ins[2][...], preferred_element_type=F32)
        outs[0][...] = g.astype(BF)
        outs[1][...] = u.astype(BF)
        outs[2][...] = (_silu(g) * u).astype(BF)

    w_spec = pl.BlockSpec((None, D, nb), lambda i, j: (j, 0, 0))
    o_spec = pl.BlockSpec((tm, nb), lambda i, j: (i, j))
    return _call(name, compute, (t // tm, NDEV), [h, wg, wu], [pl.BlockSpec((tm, D), lambda i, j: (i, 0)), w_spec, w_spec],
                 [jax.ShapeDtypeStruct((t, F), BF)] * 3, [o_spec] * 3, [], ("parallel", "arbitrary"), rider)


def _mm_nn(name, a, b, tm, tn, tk, extras=(), extra_specs=(), epilogue=None, out_dtypes=(F32,), rider=None):
    m, kk = a.shape
    n = b.shape[1]
    nk = kk // tk
    return _mm(
        name, [(a, pl.BlockSpec((tm, tk), lambda i, j, k: (i, k)), b, pl.BlockSpec((tk, tn), lambda i, j, k: (k, j)))], NN,
        (m // tm, n // tn, nk), nk,
        [jax.ShapeDtypeStruct((m, n), dt) for dt in out_dtypes],
        [pl.BlockSpec((tm, tn), lambda i, j, k: (i, j))] * len(out_dtypes),
        extras, extra_specs, epilogue, (tm, tn), rider)


def _mm_nn_blocked(name, a, b3, tm, rider=None):
    m = a.shape[0]
    nb = b3.shape[2]
    return _single(_mm(
        name, [(a, pl.BlockSpec((tm, D), lambda i, j, k: (i, 0)), b3, pl.BlockSpec((None, D, nb), lambda i, j, k: (j, 0, 0)))], NN,
        (m // tm, NDEV, 1), 1,
        [jax.ShapeDtypeStruct((m, NDEV * nb), F32)], [pl.BlockSpec((tm, nb), lambda i, j, k: (i, j))], rider=rider), rider)


def _mm_nt(name, a, b, tm, tn, out_dtypes=(F32,), extras=(), extra_specs=(), epilogue=None, rider=None):
    m, kk = a.shape
    n = b.shape[0]
    return _mm(
        name, [(a, pl.BlockSpec((tm, kk), lambda i, j, k: (i, 0)), b, pl.BlockSpec((tn, kk), lambda i, j, k: (j, 0)))], NT,
        (m // tm, n // tn, 1), 1,
        [jax.ShapeDtypeStruct((m, n), dt) for dt in out_dtypes],
        [pl.BlockSpec((tm, tn), lambda i, j, k: (i, j))] * len(out_dtypes),
        extras, extra_specs, epilogue, rider=rider)


def _mm_nt_blocked(name, a_list, b3_list, tm, rider=None):
    m = a_list[0].shape[0]
    nb = b3_list[0].shape[2]
    pairs = [(a, pl.BlockSpec((tm, nb), lambda i, j, k: (i, k)), b3, pl.BlockSpec((None, D, nb), lambda i, j, k: (k, 0, 0)))
             for a, b3 in zip(a_list, b3_list)]
    return _single(_mm(name, pairs, NT, (m // tm, 1, NDEV), NDEV,
                       [jax.ShapeDtypeStruct((m, D), F32)], [pl.BlockSpec((tm, D), lambda i, j, k: (i, 0))],
                       acc_shape=(tm, D), rider=rider), rider)


def _mm_tn(name, a, b, tm, tn, rider=None):
    t, m = a.shape
    n = b.shape[1]
    return _single(_mm(
        name, [(a, pl.BlockSpec((t, tm), lambda i, j, k: (0, i)), b, pl.BlockSpec((t, tn), lambda i, j, k: (0, j)))], TN,
        (m // tm, n // tn, 1), 1,
        [jax.ShapeDtypeStruct((m, n), BF)], [pl.BlockSpec((tm, tn), lambda i, j, k: (i, j))], rider=rider), rider)


def _mm_tn_blocked(name, a, b, rider=None):
    t = a.shape[0]
    nb = b.shape[1] // NDEV
    return _single(_mm(
        name, [(a, pl.BlockSpec((t, D), lambda i, j, k: (0, 0)), b, pl.BlockSpec((t, nb), lambda i, j, k: (0, j)))], TN,
        (1, NDEV, 1), 1,
        [jax.ShapeDtypeStruct((NDEV, D, nb), BF)], [pl.BlockSpec((None, D, nb), lambda i, j, k: (j, 0, 0))], rider=rider), rider)


def _dw_gate_up(name, h, dgate, dup, rider=None):
    t = h.shape[0]
    nb = F // NDEV

    def compute(ins, outs, _):
        hv = ins[0][...]
        outs[0][...] = lax.dot_general(hv, ins[1][...], TN, preferred_element_type=F32).astype(BF)
        outs[1][...] = lax.dot_general(hv, ins[2][...], TN, preferred_element_type=F32).astype(BF)

    d_spec = pl.BlockSpec((t, nb), lambda j: (0, j))
    o_spec = pl.BlockSpec((None, D, nb), lambda j: (j, 0, 0))
    return _call(name, compute, (NDEV,), [h, dgate, dup], [pl.BlockSpec((t, D), lambda j: (0, 0)), d_spec, d_spec],
                 [jax.ShapeDtypeStruct((NDEV, D, nb), BF)] * 2, [o_spec] * 2, [], ("arbitrary",), rider)


def _dw_square(name, pairs):
    t = pairs[0][0].shape[0]
    tm = 512
    n = len(pairs)

    def compute(ins, outs, _):
        for p in range(n):
            outs[p][...] = lax.dot_general(ins[2 * p][...], ins[2 * p + 1][...], TN, preferred_element_type=F32).astype(BF)

    return _call(name, compute, (D // tm,), [x for pair in pairs for x in pair],
                 [pl.BlockSpec((t, tm), lambda i: (0, i)), pl.BlockSpec((t, D), lambda i: (0, 0))] * n,
                 [jax.ShapeDtypeStruct((D, D), BF)] * n, [pl.BlockSpec((tm, D), lambda i: (i, 0))] * n, [], ("arbitrary",))


def _swiglu_bwd(da, gate, up):
    gate = gate.astype(F32)
    s = jax.nn.sigmoid(gate)
    return da * up.astype(F32) * (s * (1.0 + gate * (1.0 - s))), da * (gate * s)


def _ffn_dact_dh(name, df, wd, gate, up, wg, wu, rider=None):
    t = df.shape[0]
    tm = min(t, 1024)
    nb = F // NDEV

    def compute(ins, outs, scr):
        acc = scr[0]
        j = pl.program_id(1)
        da = lax.dot_general(ins[0][...], ins[1][...], NT, preferred_element_type=F32)
        dgate, dup = _swiglu_bwd(da, ins[2][...], ins[3][...])
        dgate, dup = dgate.astype(BF), dup.astype(BF)
        outs[0][...] = dgate
        outs[1][...] = dup
        part = (lax.dot_general(dgate, ins[4][...], NT, preferred_element_type=F32)
                + lax.dot_general(dup, ins[5][...], NT, preferred_element_type=F32))

        @pl.when(j == 0)
        def _():
            acc[...] = part

        @pl.when(j > 0)
        def _():
            acc[...] += part

        @pl.when(j == NDEV - 1)
        def _():
            outs[2][...] = acc[...]

    blk = pl.BlockSpec((tm, nb), lambda i, j: (i, j))
    w3 = pl.BlockSpec((None, D, nb), lambda i, j: (j, 0, 0))
    row = pl.BlockSpec((tm, D), lambda i, j: (i, 0))
    return _call(name, compute, (t // tm, NDEV), [df, wd, gate, up, wg, wu],
                 [row, pl.BlockSpec((nb, D), lambda i, j: (j, 0)), blk, blk, w3, w3],
                 [jax.ShapeDtypeStruct((t, F), BF)] * 2 + [jax.ShapeDtypeStruct((t, D), F32)], [blk, blk, row],
                 [pltpu.VMEM((tm, D), F32)], ("parallel", "arbitrary"), rider)


def _rowcall(name, fn, ins, in_specs, n_row_out, out_shapes, out_specs, grid, scratch_shapes=(), rider=None, aliases=None):
    def accumulate(o, v, i):
        @pl.when(i == 0)
        def _():
            o[...] = v.astype(o.dtype)

        @pl.when(i > 0)
        def _():
            o[...] += v.astype(o.dtype)

    def compute(in_refs, out_refs, scr):
        i = pl.program_id(0)
        vals = fn(i, in_refs, scr)
        for idx, (o, v) in enumerate(zip(out_refs, vals)):
            if idx < n_row_out:
                o[...] = v.astype(o.dtype)
            else:
                accumulate(o, v, i)

    return _call(name, compute, (grid,), ins, in_specs, out_shapes, out_specs, list(scratch_shapes), ("arbitrary",), rider, aliases)


def _rows(tr, w=D, cb=0):
    return pl.BlockSpec((tr, w), lambda i: (i, cb))


def _whole(shape):
    nd = len(shape)
    return pl.BlockSpec(shape, lambda i: (0,) * nd)


def _vec(n=1):
    return jax.ShapeDtypeStruct((n, D), F32)


def _rms_mod(x, gain, sc, sh):
    y = x * lax.rsqrt(jnp.mean(x * x, axis=-1, keepdims=True) + EPS)
    return (y * gain) * (1.0 + sc) + sh


def _layer_norm(x, g, b):
    mu = jnp.mean(x, axis=-1, keepdims=True)
    var = jnp.mean(jnp.square(x - mu), axis=-1, keepdims=True)
    return (x - mu) * lax.rsqrt(var + EPS) * g + b


def _norm_mod(name, x, gain, sc, sh):
    t = x.shape[0]
    tr = min(t, 256)

    def fn(i, r, _):
        return [_rms_mod(r[0][...], r[1][...], r[2][...], r[3][...])]

    return _rowcall(name, fn, [x, gain, sc, sh], [_rows(tr)] + [_whole((1, D))] * 3, 1,
                    [jax.ShapeDtypeStruct((t, D), BF)], [_rows(tr)], t // tr)[0]


def _gate_grads(dx, f, g, scale):
    return scale * g * dx, jnp.sum(scale * dx * f.astype(F32), axis=0, keepdims=True)


def _norm_mod_bwd(name, x, gain, sc, sh, dh, dres, below=None, rider=None):
    t = x.shape[0]
    tr = min(t, 256)

    def fn(i, r, _):
        _, vjp = jax.vjp(_rms_mod, r[0][...], r[1][...], r[2][...], r[3][...])
        dx, dgain, dsc, dsh = vjp(r[4][...])
        dx = dx + r[5][...]
        if below is None:
            return [dx, dgain, dsc, dsh]
        df, dg = _gate_grads(dx, r[6][...], r[7][...], below[2])
        return [dx, df, dgain, dsc, dsh, dg]

    ins, specs = [x, gain, sc, sh, dh, dres], [_rows(tr)] + [_whole((1, D))] * 3 + [_rows(tr)] * 2
    outs, out_specs = [jax.ShapeDtypeStruct((t, D), F32)], [_rows(tr)]
    if below is not None:
        ins, specs = ins + [below[0], below[1]], specs + [_rows(tr), _whole((1, D))]
        outs, out_specs = outs + [jax.ShapeDtypeStruct((t, D), BF)], out_specs + [_rows(tr)]
    n_vec = 3 if below is None else 4
    return _rowcall(name, fn, ins, specs, len(outs), outs + [_vec()] * n_vec, out_specs + [_whole((1, D))] * n_vec, t // tr,
                    rider=rider)


def _sgu_pre(up, vp, bu, bv, ln_g, ln_b):
    return jax.nn.gelu(up + bu), _layer_norm(jax.nn.gelu(vp + bv), ln_g, ln_b)


def _causal(w_ref, h):
    rows = lax.broadcasted_iota(jnp.int32, (CHUNK, CHUNK), 0)
    cols = lax.broadcasted_iota(jnp.int32, (CHUNK, CHUNK), 1)
    return jnp.where(cols <= rows, w_ref[h], 0.0)


def _sgu(name, proj, b_in, ln_g, ln_b, w_s, bias_full, rider=None):
    t = proj.shape[0]

    def fn(i, r, _):
        u, v = _sgu_pre(r[0][...], r[1][...], r[2][...], r[3][...], r[4][...], r[5][...])
        vb = v.astype(BF)
        mixed = [jnp.dot(_causal(r[6], h).astype(BF), vb[:, h * CHUNK:(h + 1) * CHUNK], preferred_element_type=F32)
                 for h in range(HEADS)]
        return [u * (jnp.concatenate(mixed, axis=1) + r[7][...])]

    return _rowcall(
        name, fn, [proj, proj, b_in, b_in, ln_g, ln_b, w_s, bias_full],
        [_rows(CHUNK, D, 0), _rows(CHUNK, D, 1), pl.BlockSpec((1, D), lambda i: (0, 0)), pl.BlockSpec((1, D), lambda i: (0, 1)),
         _whole((1, D)), _whole((1, D)), _whole((HEADS, CHUNK, CHUNK)), _whole((CHUNK, D))],
        1, [jax.ShapeDtypeStruct((t, D), BF)], [_rows(CHUNK)], t // CHUNK, rider=rider)


def _sgu_bwd(name, proj, b_in, ln_g, ln_b, w_s, bias_full, dout, dproj, rider=None):
    t = proj.shape[0]

    def fn(i, r, _):
        (u, v), vjp = jax.vjp(_sgu_pre, r[0][...], r[1][...], r[2][...], r[3][...], r[4][...], r[5][...])
        vb = v.astype(BF)
        d = r[8][...]
        masks = [_causal(r[6], h).astype(BF) for h in range(HEADS)]
        cols = [slice(h * CHUNK, (h + 1) * CHUNK) for h in range(HEADS)]
        mixed = jnp.concatenate([jnp.dot(masks[h], vb[:, cols[h]], preferred_element_type=F32) for h in range(HEADS)], axis=1)
        du = d * (mixed + r[7][...])
        dmix = d * u
        dmb = dmix.astype(BF)
        dv = jnp.concatenate([lax.dot_general(masks[h], dmb[:, cols[h]], TN, preferred_element_type=F32) for h in range(HEADS)], axis=1)
        rows = lax.broadcasted_iota(jnp.int32, (CHUNK, CHUNK), 0)
        lanes = lax.broadcasted_iota(jnp.int32, (CHUNK, CHUNK), 1)
        dws = jnp.stack([jnp.where(lanes <= rows, lax.dot_general(dmb[:, cols[h]], vb[:, cols[h]], NT, preferred_element_type=F32), 0.0)
                         for h in range(HEADS)])
        dbs = jnp.zeros((CHUNK, CHUNK), F32)
        for h in range(HEADS):
            dbs = dbs + jnp.where(lanes == h, jnp.sum(dmix[:, cols[h]], axis=1, keepdims=True), 0.0)
        dup, dvp, dbu, dbv, dg, db = vjp((du, dv))
        return [jnp.concatenate([dup, dvp], axis=1), dbu, dbv, dg, db, dws, dbs]

    return _rowcall(
        name, fn, [proj, proj, b_in, b_in, ln_g, ln_b, w_s, bias_full, dout, dproj],
        [_rows(CHUNK, D, 0), _rows(CHUNK, D, 1), pl.BlockSpec((1, D), lambda i: (0, 0)), pl.BlockSpec((1, D), lambda i: (0, 1)),
         _whole((1, D)), _whole((1, D)), _whole((HEADS, CHUNK, CHUNK)), _whole((CHUNK, D)), _rows(CHUNK),
         pl.BlockSpec(memory_space=pl.ANY)],
        1, [jax.ShapeDtypeStruct(dproj.shape, dproj.dtype)] + [_vec()] * 4
        + [jax.ShapeDtypeStruct((HEADS, CHUNK, CHUNK), F32), jax.ShapeDtypeStruct((CHUNK, CHUNK), F32)],
        [pl.BlockSpec((CHUNK, 2 * D), lambda i: (i, 0))] + [_whole((1, D))] * 4 + [_whole((HEADS, CHUNK, CHUNK)), _whole((CHUNK, CHUNK))],
        t // CHUNK, rider=rider, aliases={9: 0})


def _halo_before(tr, cb):
    return pl.BlockSpec((HALO, D), lambda i: (jnp.maximum(i * (tr // HALO) - 1, 0), cb))


def _halo_after(tr, cb, n_tiles):
    return pl.BlockSpec((HALO, D), lambda i: (jnp.minimum((i + 1) * (tr // HALO), n_tiles * (tr // HALO) - 1), cb))


def _ln_silu(z, g, b):
    return _silu(_layer_norm(z, g, b))


SUBLANES = 8
LANES = 128
CONV_STRIP = 16
DW_STRIP = 32


def _shifted_copies(buf, copies, rows):
    for b in range(1, SUBLANES):
        copies[b - 1, pl.ds(0, rows), :] = buf[pl.ds(b, rows), :]


def _shifted(buf, copies, offset, start, rows, lanes=slice(None)):
    at = pl.ds(pl.multiple_of(start + SUBLANES * (offset // SUBLANES), SUBLANES), rows)
    return buf[at, lanes] if offset % SUBLANES == 0 else copies[offset % SUBLANES - 1, at, lanes]


def _accumulate(o, v, i):
    @pl.when(i == 0)
    def _():
        o[...] = v.astype(o.dtype)

    @pl.when(i > 0)
    def _():
        o[...] += v.astype(o.dtype)


def _conv(name, proj, b_in, conv_w, conv_b, ln_g, ln_b, rider=None):
    t = proj.shape[0]
    tr = min(t, 256)

    def compute(r, outs, scr):
        zbuf, zs = scr
        i = pl.program_id(0)
        bv, bg = r[4][...], r[5][...]
        z0 = (r[0][...] + bv) * jax.nn.sigmoid(r[1][...] + bg)
        before = (r[2][...] + bv) * jax.nn.sigmoid(r[3][...] + bg)
        zbuf[pl.ds(0, HALO), :] = jnp.where(i > 0, before, 0.0)
        zbuf[pl.ds(HALO, tr), :] = z0
        outs[0][...] = z0
        _shifted_copies(zbuf, zs, tr + HALO - SUBLANES)

        def strip(s, carry):
            r0 = s * CONV_STRIP
            acc = jnp.zeros((CONV_STRIP, D), F32) + r[7][...]
            for k in range(KW):
                acc = acc + r[6][k:k + 1, :] * _shifted(zbuf, zs, HALO - (KW - 1) + k, r0, CONV_STRIP)
            outs[1][pl.ds(pl.multiple_of(r0, SUBLANES), CONV_STRIP), :] = acc
            return carry

        lax.fori_loop(0, tr // CONV_STRIP, strip, 0)
        outs[2][...] = _ln_silu(outs[1][...], r[8][...], r[9][...]).astype(BF)

    return _call(
        name, compute, (t // tr,), [proj, proj, proj, proj, b_in, b_in, conv_w, conv_b, ln_g, ln_b],
        [_rows(tr, D, 2), _rows(tr, D, 3), _halo_before(tr, 2), _halo_before(tr, 3),
         pl.BlockSpec((1, D), lambda i: (0, 2)), pl.BlockSpec((1, D), lambda i: (0, 3)),
         _whole((HALO, D)), _whole((1, D)), _whole((1, D)), _whole((1, D))],
        [jax.ShapeDtypeStruct((t, D), F32), jax.ShapeDtypeStruct((t, D), F32), jax.ShapeDtypeStruct((t, D), BF)],
        [_rows(tr)] * 3, [pltpu.VMEM((tr + HALO, D), F32), pltpu.VMEM((SUBLANES - 1, tr + HALO, D), F32)], ("arbitrary",), rider)


def _conv_bwd(name, proj, b_in, conv_w, ln_g, ln_b, z0, z1, dz3, dproj, rider=None):
    t = proj.shape[0]
    tr = min(t, 256)
    n_tiles = t // tr

    def compute(r, outs, scr):
        zbuf, dbuf, zs, ds, dwacc = scr
        i = pl.program_id(0)
        g, b = r[5][...], r[6][...]
        zero_row = jnp.zeros((1, D), F32)
        _, vjp = jax.vjp(_ln_silu, r[9][...], g, b)
        dz1, dg, db = vjp(r[11][...])
        dcb = jnp.sum(dz1, axis=0, keepdims=True)
        _, vjp_after = jax.vjp(_ln_silu, r[10][...], g, b)
        dz1_after = vjp_after(r[12][...])[0]
        dbuf[pl.ds(0, tr), :] = dz1
        dbuf[pl.ds(tr, HALO), :] = jnp.where(i < n_tiles - 1, dz1_after, 0.0)
        zbuf[pl.ds(0, HALO), :] = jnp.where(i > 0, r[8][...], 0.0)
        zbuf[pl.ds(HALO, tr), :] = r[7][...]
        _shifted_copies(dbuf, ds, tr + HALO - SUBLANES)
        _shifted_copies(zbuf, zs, tr + HALO - SUBLANES)

        def dz0_strip(s, carry):
            r0 = s * CONV_STRIP
            at = pl.ds(pl.multiple_of(r0, CONV_STRIP), CONV_STRIP)
            acc = jnp.zeros((CONV_STRIP, D), F32)
            for k in range(KW):
                acc = acc + r[4][k:k + 1, :] * _shifted(dbuf, ds, KW - 1 - k, r0, CONV_STRIP)
            a = r[0][at, :] + r[2][...]
            sg = jax.nn.sigmoid(r[1][at, :] + r[3][...])
            dcv = acc * sg
            dcg = acc * a * sg * (1.0 - sg)
            outs[0][at, :] = jnp.concatenate([dcv, dcg], axis=1).astype(BF)
            return carry[0] + jnp.sum(dcv, axis=0, keepdims=True), carry[1] + jnp.sum(dcg, axis=0, keepdims=True)

        dbv, dbg = lax.fori_loop(0, tr // CONV_STRIP, dz0_strip, (zero_row, zero_row))

        for lb in range(D // LANES):
            lanes = slice(lb * LANES, (lb + 1) * LANES)

            def dw_strip(s, accs, lanes=lanes):
                r0 = s * DW_STRIP
                dz = dbuf[pl.ds(pl.multiple_of(r0, SUBLANES), DW_STRIP), lanes]
                out = []
                for k in range(KW):
                    prod = dz * _shifted(zbuf, zs, HALO - (KW - 1) + k, r0, DW_STRIP, lanes)
                    part = prod[0:SUBLANES]
                    for q in range(1, DW_STRIP // SUBLANES):
                        part = part + prod[q * SUBLANES:(q + 1) * SUBLANES]
                    out.append(accs[k] + part)
                return tuple(out)

            accs = lax.fori_loop(0, tr // DW_STRIP, dw_strip, tuple(jnp.zeros((SUBLANES, LANES), F32) for _ in range(KW)))
            for k in range(KW):
                dwacc[pl.ds(k * SUBLANES, SUBLANES), lanes] = accs[k]
        dw_rows = [jnp.sum(dwacc[pl.ds(k * SUBLANES, SUBLANES), :], axis=0, keepdims=True) for k in range(KW)]
        dw_rows.append(jnp.zeros((HALO - KW, D), F32))
        for o, v in zip(outs[1:], (dbv, dbg, jnp.concatenate(dw_rows, axis=0), dcb, dg, db)):
            _accumulate(o, v, i)

    wide = pl.BlockSpec((tr, 2 * D), lambda i: (i, 1))
    return _call(
        name, compute, (n_tiles,), [proj, proj, b_in, b_in, conv_w, ln_g, ln_b, z0, z0, z1, z1, dz3, dz3, dproj],
        [_rows(tr, D, 2), _rows(tr, D, 3), pl.BlockSpec((1, D), lambda i: (0, 2)), pl.BlockSpec((1, D), lambda i: (0, 3)),
         _whole((HALO, D)), _whole((1, D)), _whole((1, D)),
         _rows(tr), _halo_before(tr, 0), _rows(tr), _halo_after(tr, 0, n_tiles), _rows(tr), _halo_after(tr, 0, n_tiles),
         pl.BlockSpec(memory_space=pl.ANY)],
        [jax.ShapeDtypeStruct(dproj.shape, dproj.dtype), _vec(), _vec(), _vec(HALO), _vec(), _vec(), _vec()],
        [wide] + [_whole((1, D))] * 2 + [_whole((HALO, D))] + [_whole((1, D))] * 3,
        [pltpu.VMEM((tr + HALO, D), F32), pltpu.VMEM((tr + HALO, D), F32),
         pltpu.VMEM((SUBLANES - 1, tr + HALO, D), F32), pltpu.VMEM((SUBLANES - 1, tr + HALO, D), F32),
         pltpu.VMEM((HALO * SUBLANES, D), F32)],
        ("arbitrary",), rider, aliases={13: 0})


def _merge_fn(ga, gb, bga, bgb, ya, yb):
    return jax.nn.sigmoid(ga + bga) * ya + jax.nn.sigmoid(gb + bgb) * yb


def _mix_tail(name, ua, z3, proj, b_in, wa, wb, wo, x, g, next_norm):
    t = ua.shape[0]
    tr = min(t, 256)

    def compute(r, outs, _):
        ya = jnp.dot(r[0][...], r[6][...], preferred_element_type=F32)
        yb = jnp.dot(r[1][...], r[7][...], preferred_element_type=F32)
        merged = _merge_fn(r[2][...], r[3][...], r[4][...], r[5][...], ya, yb).astype(BF)
        y = jnp.dot(merged, r[8][...], preferred_element_type=F32)
        x_out = r[9][...] + r[10][...] * y
        for o, v in zip(outs, (ya, yb, merged, y, x_out, _rms_mod(x_out, r[11][...], r[12][...], r[13][...]))):
            o[...] = v.astype(o.dtype)

    row = _whole((1, D))
    return _call(
        name, compute, (t // tr,), [ua, z3, proj, proj, b_in, b_in, wa, wb, wo, x, g, *next_norm],
        [_rows(tr), _rows(tr), _rows(tr, D, 4), _rows(tr, D, 5), pl.BlockSpec((1, D), lambda i: (0, 4)),
         pl.BlockSpec((1, D), lambda i: (0, 5)), _whole((D, D)), _whole((D, D)), _whole((D, D)), _rows(tr), row, row, row, row],
        [jax.ShapeDtypeStruct((t, D), dt) for dt in (F32, F32, BF, BF, F32, BF)], [_rows(tr)] * 6, [], ("arbitrary",))


def _mix_tail_bwd(name, dy, proj, b_in, ya, yb, wa, wb, wo, rider=None):
    t = proj.shape[0]
    tr = min(t, 256)

    def compute(r, outs, _):
        i = pl.program_id(0)
        dm = lax.dot_general(r[0][...], r[9][...], NT, preferred_element_type=F32)
        _, vjp = jax.vjp(_merge_fn, *[x[...] for x in r[1:7]])
        dga, dgb, dbga, dbgb, dya, dyb = vjp(dm)
        dya, dyb = dya.astype(BF), dyb.astype(BF)
        outs[0][...] = jnp.concatenate([dga, dgb], axis=1).astype(BF)
        outs[1][...] = dya
        outs[2][...] = dyb
        outs[3][...] = lax.dot_general(dya, r[7][...], NT, preferred_element_type=F32)
        outs[4][...] = lax.dot_general(dyb, r[8][...], NT, preferred_element_type=F32)
        _accumulate(outs[5], dbga, i)
        _accumulate(outs[6], dbgb, i)

    return _call(
        name, compute, (t // tr,), [dy, proj, proj, b_in, b_in, ya, yb, wa, wb, wo],
        [_rows(tr), _rows(tr, D, 4), _rows(tr, D, 5), pl.BlockSpec((1, D), lambda i: (0, 4)), pl.BlockSpec((1, D), lambda i: (0, 5)),
         _rows(tr), _rows(tr), _whole((D, D)), _whole((D, D)), _whole((D, D))],
        [jax.ShapeDtypeStruct((t, D_IN), BF)] + [jax.ShapeDtypeStruct((t, D), BF)] * 2 + [jax.ShapeDtypeStruct((t, D), F32)] * 2
        + [_vec(), _vec()],
        [pl.BlockSpec((tr, 2 * D), lambda i: (i, 2))] + [_rows(tr)] * 4 + [_whole((1, D))] * 2, [], ("arbitrary",), rider)


def _loss_head(name, x, gain, target, f, g, scale):
    t = x.shape[0]
    tr = min(t, 256)

    def loss_fn(xv, gn, tgt):
        y = xv * lax.rsqrt(jnp.mean(xv * xv, axis=-1, keepdims=True) + EPS) * gn
        return 0.5 * jnp.sum(jnp.mean(jnp.square(y - tgt), axis=-1))

    def fn(i, r, _):
        loss, vjp = jax.vjp(loss_fn, r[0][...], r[1][...], r[2][...])
        dx, dgain, _ = vjp(jnp.ones((), F32))
        df, dg = _gate_grads(dx, r[3][...], r[4][...], scale)
        return [dx, df, dgain, jnp.zeros((1, D), F32) + loss, dg]

    return _rowcall(name, fn, [x, gain, target, f, g], [_rows(tr), _whole((1, D)), _rows(tr), _rows(tr), _whole((1, D))], 2,
                    [jax.ShapeDtypeStruct((t, D), F32), jax.ShapeDtypeStruct((t, D), BF), _vec(), _vec(), _vec()],
                    [_rows(tr)] * 2 + [_whole((1, D))] * 3, t // tr)


def _adamw(w, g, m, v):
    m = B1 * m + (1.0 - B1) * g
    v = B2 * v + (1.0 - B2) * jnp.square(g)
    m_hat = m / BC1
    v_hat = v / BC2
    delta = -LR * (m_hat / (jnp.sqrt(v_hat) + ADAM_EPS) + WD * w)
    return delta, m, v


ADAMW_ROWS = 64


def _adamw_group(name, items, rider=None, rows=ADAMW_ROWS):
    ins, in_specs, out_shapes, out_specs, plan = [], [], [], [], []
    first = 0
    for chip_sum, received, w, m, v in items:
        r, c = w.shape
        tr = min(r, rows)
        n = r // tr

        def tile(i, first=first, n=n):
            return jnp.clip(i - first, 0, n - 1)

        spec = pl.BlockSpec((tr, c), lambda i, tile=tile: (tile(i), 0))
        ins += [chip_sum, *received, w, m, v]
        in_specs += [pl.BlockSpec((None, tr, c), lambda i, tile=tile: (0, tile(i), 0))]
        in_specs += [pl.BlockSpec((g.shape[0], tr, c), lambda i, tile=tile: (0, tile(i), 0)) for g in received]
        in_specs += [spec] * 3
        out_shapes += [jax.ShapeDtypeStruct((r, c), F32)] * 4
        out_specs += [spec] * 4
        plan.append((first, n, [g.shape[0] for g in received]))
        first += n

    def compute(in_refs, out_refs, _):
        i = pl.program_id(0)
        at_in = at_out = 0
        for start, n, counts in plan:
            mine = in_refs[at_in:at_in + 4 + len(counts)]
            outs = out_refs[at_out:at_out + 4]
            at_in += 4 + len(counts)
            at_out += 4

            @pl.when(jnp.logical_and(i >= start, i < start + n))
            def _(mine=mine, outs=outs, counts=counts):
                g = mine[0][...].astype(F32)
                for j, count in enumerate(counts):
                    for s in range(count):
                        g = g + mine[1 + j][s].astype(F32)
                delta, m_new, v_new = _adamw(mine[-3][...], g, mine[-2][...], mine[-1][...])
                for o, val in zip(outs, (g, delta, m_new, v_new)):
                    o[...] = val

    res = _call(name, compute, (first,), ins, in_specs, out_shapes, out_specs, [], ("arbitrary",), rider)
    outs, rode = res if rider else (res, [])
    return [outs[4 * j:4 * j + 4] for j in range(len(items))], rode


def _adamw_small(name, packed_all, late_all, dws_all, vectors, w_s):
    n_vec = len(vectors)

    def body(*refs):
        p_ref, l_ref, d_ref = refs[:3]
        param_refs = refs[3:3 + 3 * n_vec + 3]
        out_refs = refs[3 + 3 * n_vec + 3:-1]
        g_ref = refs[-1]
        g = p_ref[0]
        late = l_ref[0]
        for s in range(1, NDEV):
            g = g + p_ref[s]
            late = late + l_ref[s]
        g_ref[...] = g
        g_ref[pl.ds(0, R_LATE), :] += late

        def update(gp, wmv, outs):
            delta, m_new, v_new = _adamw(wmv[0][...], gp, wmv[1][...], wmv[2][...])
            for o, val in zip(outs, (gp, delta, m_new, v_new)):
                o[...] = val

        for j, (row, rows, *_) in enumerate(vectors):
            pieces = [g_ref[pl.ds(row + r, 1), :] for r in range(rows)]
            update(pieces[0] if rows == 1 else jnp.concatenate(pieces, axis=1), param_refs[3 * j:3 * j + 3], out_refs[4 * j:4 * j + 4])
        gw = d_ref[0]
        for s in range(1, NDEV):
            gw = gw + d_ref[s]
        update(gw, param_refs[3 * n_vec:], out_refs[4 * n_vec:4 * n_vec + 4])
        out_refs[-2][...] = g_ref[pl.ds(R_CW, KW), :]
        out_refs[-1][...] = g_ref[pl.ds(R_LOSS, 1), :]

    params = [a for _, _, w, m, v in vectors for a in (w, m, v)] + list(w_s)
    out_shapes = [jax.ShapeDtypeStruct(w.shape, F32) for _, _, w, _, _ in vectors for _ in range(4)]
    out_shapes += [jax.ShapeDtypeStruct(w_s[0].shape, F32)] * 4 + [jax.ShapeDtypeStruct((KW, D), F32), _vec()]
    res = pl.pallas_call(body, name=name, out_shape=out_shapes, scratch_shapes=[pltpu.VMEM((R_TOTAL, D), F32)],
                         compiler_params=_params(None))(packed_all, late_all, dws_all, *params)
    return [res[4 * j:4 * j + 4] for j in range(n_vec + 1)], res[-2], res[-1]


def _adamw_plain(name, g, w, m, v):
    def body(g_ref, w_ref, m_ref, v_ref, d_ref, mo_ref, vo_ref):
        delta, m_new, v_new = _adamw(w_ref[...], g_ref[...], m_ref[...], v_ref[...])
        d_ref[...] = delta
        mo_ref[...] = m_new
        vo_ref[...] = v_new

    return pl.pallas_call(body, name=name, out_shape=[jax.ShapeDtypeStruct(w.shape, F32)] * 3,
                          compiler_params=_params(None))(g, w, m, v)


def _adamw_ada(name, c_all_t, dmod, dmod_late, w, m, v):
    r, c = w.shape
    tr = 256

    def fn(i, refs, _):
        ca = _silu(refs[0][...])
        dm = refs[1][...] + refs[2][...]
        g = ca[:, 0:1] * dm[0:1, :]
        for b in range(1, NDEV):
            g = g + ca[:, b:b + 1] * dm[b:b + 1, :]
        delta, m_new, v_new = _adamw(refs[3][...], g, refs[4][...], refs[5][...])
        return [g, delta, m_new, v_new]

    spec = pl.BlockSpec((tr, c), lambda i: (i, 0))
    whole = pl.BlockSpec((NDEV, c), lambda i: (0, 0))
    return _rowcall(name, fn, [c_all_t, dmod, dmod_late, w, m, v],
                    [pl.BlockSpec((tr, NDEV), lambda i: (i, 0)), whole, whole, spec, spec, spec], 4,
                    [jax.ShapeDtypeStruct((r, c), F32)] * 4, [spec] * 4, r // tr)


def _ffn_fwd(tag, x, h, g, wg, wu, wd_shard, down_rider, next_norm=None, more_shards=()):
    t = x.shape[0]
    tm = min(t, 512 if down_rider else 1024)
    (gate, up, act), (wd, *more) = _ffn_up(f"{tag}_up", h, wg, wu, rider=_gather_rider([wd_shard, *more_shards]))
    row = pl.BlockSpec((1, D), lambda i, j, k: (0, 0))

    def epilogue(f, xv, gv, *norm):
        x_out = xv + 0.5 * gv * f
        return (x_out, f, _rms_mod(x_out, *norm)) if norm else (x_out, f)

    res = _mm_nn(f"{tag}_down", act, wd.reshape(F, D), tm, D, 1024, extras=(x, g, *(next_norm or ())),
                 extra_specs=(pl.BlockSpec((tm, D), lambda i, j, k: (i, 0)), row, *([row] * 3 if next_norm else [])),
                 epilogue=epilogue, out_dtypes=(F32, BF, BF) if next_norm else (F32, BF), rider=down_rider)
    (x_out, f, *h_next), rode = res if down_rider else (res, None)
    return x_out, (h_next[0] if next_norm else None), (x, h, gate, up, act, f), wd, rode, more


def _ffn_bwd(tag, dx_out, df, saved, gain, sh, sc, wg, wu, wd, slots, dact_rider=None, dwd_rider=None, dwgu_rider=None,
             below=None, fuse_dh=False):
    x, h, gate, up, act, f = saved
    t = x.shape[0]
    tm = min(t, 1024)
    if fuse_dh:
        dgate, dup, dh = _ffn_dact_dh(f"{tag}_dact_dh", df, wd.reshape(F, D), gate, up, wg, wu)
        dwd = _mm_tn(f"{tag}_dwd", act, df, 512, D).reshape(NDEV, F // NDEV, D)
        (dwg, dwu), (sib_d,) = _dw_gate_up(f"{tag}_dwgu", h, dgate, dup, rider=_pair_rider([dwd]))
        (sum_d,) = _pair_add(f"{tag}_dwd_add", [dwd], [sib_d], slots)
        normed, (sib_g, sib_u) = _norm_mod_bwd(f"{tag}_norm_bwd", x, gain, sc, sh, dh, dx_out, below=below,
                                               rider=_pair_rider([dwg, dwu]))
        sum_g, sum_u = _pair_add(f"{tag}_dwgu_add", [dwg, dwu], [sib_g, sib_u], slots)
        return normed, (sum_d, None), sum_g, sum_u, [], [], []

    blk = pl.BlockSpec((tm, F // NDEV), lambda i, j, k: (i, j))
    res = _mm_nt(f"{tag}_dact", df, wd.reshape(F, D), tm, F // NDEV, out_dtypes=(BF, BF),
                 extras=(gate, up), extra_specs=(blk, blk), epilogue=_swiglu_bwd, rider=dact_rider)
    (dgate, dup), rode_dact = res if dact_rider else (res, [])
    res = _mm_tn(f"{tag}_dwd", act, df, 512, D, rider=dwd_rider)
    dwd, rode_dwd = res if dwd_rider else (res, [])
    dwd = dwd.reshape(NDEV, F // NDEV, D)
    (dwg, dwu), (sib_d, *rode_dwgu) = _dw_gate_up(f"{tag}_dwgu", h, dgate, dup,
                                                  rider=[_pair_rider([dwd])] + ([dwgu_rider] if dwgu_rider else []))
    (sum_d,) = _pair_add(f"{tag}_dwd_add", [dwd], [sib_d], slots)
    dh, (sib_g, sib_u, got_d) = _mm_nt_blocked(f"{tag}_dh", [dgate, dup], [wg, wu], tm,
                                               rider=[_pair_rider([dwg, dwu]), _chip_rider([sum_d])])
    sum_g, sum_u = _pair_add(f"{tag}_dwgu_add", [dwg, dwu], [sib_g, sib_u], slots)
    normed = _norm_mod_bwd(f"{tag}_norm_bwd", x, gain, sc, sh, dh, dx_out, below=below)
    return normed, (sum_d, [got_d]), sum_g, sum_u, rode_dact, rode_dwd, rode_dwgu


def kernel(x, c, ada_w, ada_b, norm_ffn1, ffn1_w_gate, ffn1_w_up, ffn1_w_down, norm_mix, mix_w_in, mix_b_in, sgu_ln_g, sgu_ln_b, sgu_w_s, sgu_b_s, conv_w, conv_b, conv_ln_g, conv_ln_b, w_branch_a, w_branch_b, w_out, norm_ffn2, ffn2_w_gate, ffn2_w_up, ffn2_w_down, norm_final, loss_target, m_ada_w, m_ada_b, m_norm_ffn1, m_ffn1_w_gate, m_ffn1_w_up, m_ffn1_w_down, m_norm_mix, m_mix_w_in, m_mix_b_in, m_sgu_ln_g, m_sgu_ln_b, m_sgu_w_s, m_sgu_b_s, m_conv_w, m_conv_b, m_conv_ln_g, m_conv_ln_b, m_w_branch_a, m_w_branch_b, m_w_out, m_norm_ffn2, m_ffn2_w_gate, m_ffn2_w_up, m_ffn2_w_down, m_norm_final, v_ada_w, v_ada_b, v_norm_ffn1, v_ffn1_w_gate, v_ffn1_w_up, v_ffn1_w_down, v_norm_mix, v_mix_w_in, v_mix_b_in, v_sgu_ln_g, v_sgu_ln_b, v_sgu_w_s, v_sgu_b_s, v_conv_w, v_conv_b, v_conv_ln_g, v_conv_ln_b, v_w_branch_a, v_w_branch_b, v_w_out, v_norm_ffn2, v_ffn2_w_gate, v_ffn2_w_up, v_ffn2_w_down, v_norm_final):
    mx, my, mc = _position()
    me = 4 * mx + 2 * my + mc
    chip = 2 * mx + my
    slots = jnp.stack([2 * (chip ^ k) + mc for k in range(N_CHIPS)]).astype(jnp.int32)
    t = x.shape[1]
    tm = min(t, 1024)
    x0 = x.reshape(t, D)
    target = loss_target.reshape(t, D)
    given = dict(ffn1_w_gate=(ffn1_w_gate, m_ffn1_w_gate, v_ffn1_w_gate), ffn1_w_up=(ffn1_w_up, m_ffn1_w_up, v_ffn1_w_up),
                 ffn1_w_down=(ffn1_w_down, m_ffn1_w_down, v_ffn1_w_down), mix_w_in=(mix_w_in, m_mix_w_in, v_mix_w_in),
                 w_branch_a=(w_branch_a, m_w_branch_a, v_w_branch_a), w_branch_b=(w_branch_b, m_w_branch_b, v_w_branch_b),
                 w_out=(w_out, m_w_out, v_w_out), ffn2_w_gate=(ffn2_w_gate, m_ffn2_w_gate, v_ffn2_w_gate),
                 ffn2_w_up=(ffn2_w_up, m_ffn2_w_up, v_ffn2_w_up), ffn2_w_down=(ffn2_w_down, m_ffn2_w_down, v_ffn2_w_down))
    shard = {n: wmv[0][0].astype(BF) for n, wmv in given.items()}

    ada_cols = N_MOD * D // NDEV
    c_all, taps_all, mod_all, (wg1, wu1) = _prologue(
        "prologue", jnp.pad(c, ((0, SUBLANES - 1), (0, 0))), jnp.pad(conv_w[0], ((0, HALO - KW), (0, 0))), ada_w[0],
        lax.dynamic_slice(ada_b, (0, me * ada_cols), (1, ada_cols)), [shard["ffn1_w_gate"], shard["ffn1_w_up"]])
    conv_w_full = jnp.transpose(taps_all.reshape(NDEV, HALO, CHUNK), (1, 0, 2)).reshape(HALO, D)
    mod = lax.dynamic_index_in_dim(mod_all.reshape(NDEV, NDEV, ada_cols), me, axis=1, keepdims=False).reshape(N_MOD, 1, D)
    sh1, sc1, g1, sh2, sc2, g2, sh3, sc3, g3 = [mod[i] for i in range(N_MOD)]

    h1 = _norm_mod("ffn1_norm", x0, norm_ffn1, sc1, sh1)
    x1, h2, saved1, wd1, (w_in,), (wa3, wb3) = _ffn_fwd(
        "ffn1", x0, h1, g1, wg1, wu1, shard["ffn1_w_down"], _gather_rider([shard["mix_w_in"]]),
        next_norm=(norm_mix, sc2, sh2), more_shards=(shard["w_branch_a"], shard["w_branch_b"]))
    proj, (wg2,) = _mm_nn_blocked("mix_in", h2, w_in, tm, rider=_gather_rider([shard["ffn2_w_gate"]]))
    bias_full = jnp.repeat(sgu_b_s[0].T, CHUNK, axis=1)
    (ua,) = _sgu("sgu", proj, mix_b_in, sgu_ln_g, sgu_ln_b, sgu_w_s[0], bias_full)
    (z0, z1, z3), (wu2, wo3) = _conv("conv", proj, mix_b_in, conv_w_full, conv_b, conv_ln_g, conv_ln_b,
                                     rider=_gather_rider([shard["ffn2_w_up"], shard["w_out"]]))
    wa, wb, wo = wa3.reshape(D, D), wb3.reshape(D, D), wo3.reshape(D, D)
    ya, yb, merged, y, x2, h3 = _mix_tail("mix_tail", ua, z3, proj, mix_b_in, wa, wb, wo, x1, g2, (norm_ffn2, sc3, sh3))
    x3, _, saved3, wd2, _, _ = _ffn_fwd("ffn2", x2, h3, g3, wg2, wu2, shard["ffn2_w_down"], None)

    norm_final2 = norm_final.reshape(1, D)
    dx3, df3, d_norm_final, loss_row, dg3 = _loss_head("loss_head", x3, norm_final2, target, saved3[-1], g3, 0.5)
    (dx2, dy, d_norm_ffn2, dsc3, dsh3, dg2), (sum_d2, _), sum_g2, sum_u2, _, _, _ = _ffn_bwd(
        "ffn2", dx3, df3, saved3, norm_ffn2, sh3, sc3, wg2, wu2, wd2, slots, below=(y, g2, 1.0), fuse_dh=True)
    (dproj, dya, dyb, dua, dz3, db_ga, db_gb), (got_g2_near,) = _mix_tail_bwd(
        "mix_tail_bwd", dy, proj, mix_b_in, ya, yb, wa, wb, wo, rider=_chip_rider([sum_g2], NEIGHBOURS))
    dwo, dwa, dwb = [g.reshape(NDEV, D // NDEV, D) for g in _dw_square("mix_dw", [(merged, dy), (ua, dya), (z3, dyb)])]
    (dproj, db_u, db_v, d_sgu_g, d_sgu_b, d_ws, d_bs_t), (*sib_abo, got_g2_far) = _sgu_bwd(
        "sgu_bwd", proj, mix_b_in, sgu_ln_g, sgu_ln_b, sgu_w_s[0], bias_full, dua, dproj,
        rider=[_pair_rider([dwa, dwb, dwo]), _chip_rider([sum_g2], DIAGONAL)])
    sum_a, sum_b, sum_o = _pair_add("mix_dw_add", [dwa, dwb, dwo], sib_abo, slots)
    (dproj, db_cv, db_cg, d_cw, d_cb, d_cln_g, d_cln_b), (got_u2, got_d2) = _conv_bwd(
        "conv_bwd", proj, mix_b_in, conv_w_full, conv_ln_g, conv_ln_b, z0, z1, dz3, dproj, rider=_chip_rider([sum_u2, sum_d2]))
    dwin, (got_a, got_b, got_o) = _mm_tn_blocked("mix_dwin", h2, dproj, rider=_chip_rider([sum_a, sum_b, sum_o]))

    d_bs = jnp.transpose(d_bs_t[:, :HEADS])
    zero = jnp.zeros((1, D), F32)
    pack_rows = [zero, zero, zero, zero, zero, dg2, dsh3, dsc3, dg3,
                 zero, zero, d_norm_ffn2, d_norm_final,
                 db_u, db_v, db_cv, db_cg, db_ga, db_gb,
                 d_sgu_g, d_sgu_b, d_bs.reshape(1, D), d_cb, d_cln_g, d_cln_b,
                 d_cw[:KW], loss_row, jnp.zeros((R_TOTAL - R_LOSS - 1, D), F32)]
    packed = jnp.concatenate(pack_rows, axis=0)
    d_ws2 = d_ws.reshape(HEADS * CHUNK, CHUNK)
    dh2, (sib_in, packed_all, dws_all) = _mm_nt_blocked("mix_in_bwd", [dproj], [w_in], tm,
                                                        rider=[_pair_rider([dwin]), _gather_rider([packed, d_ws2])])
    (sum_in,) = _pair_add("mix_dwin_add", [dwin], [sib_in], slots)
    dx1, df1, d_norm_mix, dsc2, dsh2, dg1 = _norm_mod_bwd("mix_norm_bwd", x1, norm_mix, sc2, sh2, dh2, dx2,
                                                          below=(saved1[-1], g1, 0.5))
    (dx0, d_norm_ffn1, dsc1, dsh1), down1, sum_g1, sum_u1, (got_in_near,), _, (got_in_far,) = _ffn_bwd(
        "ffn1", dx1, df1, saved1, norm_ffn1, sh1, sc1, wg1, wu1, wd1, slots,
        dact_rider=_chip_rider([sum_in], NEIGHBOURS), dwgu_rider=_chip_rider([sum_in], DIAGONAL))
    packed_late = jnp.concatenate([dsh1, dsc1, dg1, dsh2, dsc2, jnp.zeros((4, D), F32), d_norm_ffn1, d_norm_mix,
                                   jnp.zeros((R_LATE - 11, D), F32)], axis=0)
    grads = dict(ffn2_w_gate=(sum_g2, [got_g2_near, got_g2_far]), ffn2_w_up=(sum_u2, [got_u2]), ffn2_w_down=(sum_d2, [got_d2]),
                 mix_w_in=(sum_in, [got_in_near, got_in_far]), w_branch_a=(sum_a, [got_a]), w_branch_b=(sum_b, [got_b]),
                 w_out=(sum_o, [got_o]), ffn1_w_down=down1)
    done, (late_all, got_g1, got_u1) = _adamw_group(
        "adamw_most", [(cs, got, *[a[0] for a in given[n]]) for n, (cs, got) in grads.items()],
        rider=[_gather_rider([packed_late]), _chip_rider([sum_g1, sum_u1])])
    last, _ = _adamw_group("adamw_ffn1_in", [(sum_g1, [got_g1], *[a[0] for a in given["ffn1_w_gate"]]),
                                            (sum_u1, [got_u1], *[a[0] for a in given["ffn1_w_up"]])], rows=256)
    big_out = {n: [o.reshape(given[n][0].shape) for o in outs]
               for n, outs in zip([*grads, "ffn1_w_gate", "ffn1_w_up"], [*done, *last])}

    flat = lambda a: a.reshape(1, -1)
    vectors = [("ada_b", 0, 9, ada_b, m_ada_b, v_ada_b), ("norm_ffn1", 9, 1, norm_ffn1, m_norm_ffn1, v_norm_ffn1),
               ("norm_mix", 10, 1, norm_mix, m_norm_mix, v_norm_mix), ("norm_ffn2", 11, 1, norm_ffn2, m_norm_ffn2, v_norm_ffn2),
               ("norm_final", 12, 1, norm_final, m_norm_final, v_norm_final), ("mix_b_in", 13, 6, mix_b_in, m_mix_b_in, v_mix_b_in),
               ("sgu_ln_g", 19, 1, sgu_ln_g, m_sgu_ln_g, v_sgu_ln_g), ("sgu_ln_b", 20, 1, sgu_ln_b, m_sgu_ln_b, v_sgu_ln_b),
               ("sgu_b_s", 21, 1, sgu_b_s, m_sgu_b_s, v_sgu_b_s), ("conv_b", 22, 1, conv_b, m_conv_b, v_conv_b),
               ("conv_ln_g", 23, 1, conv_ln_g, m_conv_ln_g, v_conv_ln_g), ("conv_ln_b", 24, 1, conv_ln_b, m_conv_ln_b, v_conv_ln_b)]
    small_out, d_cw_all, loss_sum = _adamw_small(
        "adamw_small", packed_all, late_all, dws_all, [(row, rows, flat(wv), flat(mv), flat(vv)) for _, row, rows, wv, mv, vv in vectors],
        [a.reshape(HEADS * CHUNK, CHUNK) for a in (sgu_w_s, m_sgu_w_s, v_sgu_w_s)])
    small = {n: [o.reshape(wv.shape) for o in outs] for (n, _, _, wv, _, _), outs in zip(vectors, small_out)}
    small["sgu_w_s"] = [o.reshape(sgu_w_s.shape) for o in small_out[-1]]
    g_cw = lax.dynamic_slice(d_cw_all, (0, me * CHUNK), (KW, CHUNK))
    small["conv_w"] = [o.reshape(conv_w.shape) for o in (g_cw, *_adamw_plain("adamw_conv_w", g_cw, conv_w[0], m_conv_w[0], v_conv_w[0]))]
    loss = loss_sum[0, 0]

    dmod_cols = [lax.dynamic_slice(a[:, :N_MOD, :].reshape(NDEV, N_MOD * D), (0, me * ada_cols), (NDEV, ada_cols))
                 for a in (packed_all, late_all)]
    ada_out = [o.reshape(ada_w.shape) for o in _adamw_ada("adamw_ada_w", jnp.transpose(c_all), *dmod_cols, ada_w[0], m_ada_w[0], v_ada_w[0])]

    order = ["ada_w", "ada_b", "norm_ffn1", "ffn1_w_gate", "ffn1_w_up", "ffn1_w_down", "norm_mix", "mix_w_in", "mix_b_in",
             "sgu_ln_g", "sgu_ln_b", "sgu_w_s", "sgu_b_s", "conv_w", "conv_b", "conv_ln_g", "conv_ln_b", "w_branch_a",
             "w_branch_b", "w_out", "norm_ffn2", "ffn2_w_gate", "ffn2_w_up", "ffn2_w_down", "norm_final"]

    def leaf(n, kind):
        if n == "ada_w":
            return ada_out[kind]
        if n in big_out:
            return big_out[n][kind]
        return small[n][kind]

    return (loss, dx0.reshape(x.shape), *[leaf(n, kind) for kind in range(4) for n in order])
```

```python
import jax
import jax.numpy as jnp
from jax import lax
from jax.experimental import pallas as pl
from jax.experimental.pallas import tpu as pltpu

D = 1024
F = 4 * D
D_IN = 6 * D
HEADS = 8
CHUNK = 128
KW = 31
HALO = 32
N_MOD = 9
NDEV = 8
N_CHIPS = 4
EPS = 1e-6
LR, B1, B2, ADAM_EPS, WD, STEP = 0.001, 0.9, 0.999, 1e-08, 0.01, 10
BC1 = 1.0 - B1 ** STEP
BC2 = 1.0 - B2 ** STEP
VMEM_LIMIT = 56 * 1024 * 1024
MESH = pl.DeviceIdType.MESH
HBM = pl.BlockSpec(memory_space=pltpu.HBM)
VMEM = pl.BlockSpec(memory_space=pltpu.VMEM)
BF = jnp.bfloat16
F32 = jnp.float32

NN = (((1,), (0,)), ((), ()))
NT = (((1,), (1,)), ((), ()))
TN = (((0,), (0,)), ((), ()))

R_CW, R_LOSS, R_TOTAL = 25, 56, 64
R_LATE = 16


def _params(sem):
    return pltpu.CompilerParams(dimension_semantics=sem, vmem_limit_bytes=VMEM_LIMIT)


def _position():
    return lax.axis_index("x"), lax.axis_index("y"), lax.axis_index("c")


def _flip(pos, k):
    x, y, c = pos
    return (x ^ (k >> 2 & 1), y ^ (k >> 1 & 1), c ^ (k & 1))


def _index(pos):
    return 4 * pos[0] + 2 * pos[1] + pos[2]


def _gather_rows(x_ref, out_ref, send_sems, recv_sems, local_sem):
    m_per = x_ref.shape[0]
    x, y, c = _position()
    me, sibling = (x, y, c), (x, y, 1 - c)
    chips = [(1 - x, y), (x, 1 - y), (1 - x, 1 - y)]

    def rows(pos):
        return out_ref.at[pl.ds(_index(pos) * m_per, m_per), :]

    def copy(k, block, to, src=None):
        return pltpu.make_async_remote_copy(
            src_ref=rows(block) if src is None else src, dst_ref=rows(block),
            send_sem=send_sems.at[k], recv_sem=recv_sems.at[k], device_id=to, device_id_type=MESH)

    mine = pltpu.make_async_copy(x_ref, rows(me), local_sem)
    mine.start()
    first = [copy(0, me, sibling, src=x_ref)]
    first += [copy(1 + j, me, (*chip, c), src=x_ref) for j, chip in enumerate(chips)]
    for cp in first:
        cp.start()
    passed = [copy(4 + j, (*chip, c), sibling) for j, chip in enumerate(chips)]
    for j, chip in enumerate(chips):
        copy(1 + j, (*chip, c), me).wait_recv()
        passed[j].start()
    copy(0, sibling, me).wait_recv()
    for j, chip in enumerate(chips):
        copy(4 + j, (*chip, 1 - c), me).wait_recv()
    for cp in first + passed:
        cp.wait_send()
    mine.wait()


def _prologue(name, c_rows, taps, ada_w, ada_b, shards):
    rider = _gather_rider(shards)
    n = len(shards)
    nc = ada_w.shape[1]

    def body(*refs):
        c_ref, taps_ref, w_ref, b_ref = refs[:4]
        shard_refs = refs[4:4 + n]
        c_all_ref, taps_all_ref, mod_all_ref = refs[4 + n:7 + n]
        gathered_refs = refs[7 + n:7 + 2 * n]
        c_buf, mod_part, sems = refs[7 + 2 * n], refs[8 + 2 * n], refs[9 + 2 * n:]
        rider.start(shard_refs, gathered_refs, sems[9:])
        _gather_rows(c_ref, c_buf, *sems[0:3])
        c_all = jnp.concatenate([c_buf[pl.ds(d * SUBLANES, 1), :] for d in range(NDEV)], axis=0)
        c_all_ref[...] = c_all
        mod_part[...] = jnp.dot(_silu(c_all), w_ref[...], preferred_element_type=F32) + b_ref[...]
        _gather_rows(taps_ref, taps_all_ref, *sems[3:6])
        _gather_rows(mod_part, mod_all_ref, *sems[6:9])
        rider.mid(shard_refs, gathered_refs, sems[9:])
        rider.relay(shard_refs, gathered_refs, sems[9:])
        rider.finish(shard_refs, gathered_refs, sems[9:])

    small_sems = [pltpu.SemaphoreType.DMA((7,)), pltpu.SemaphoreType.DMA((7,)), pltpu.SemaphoreType.DMA] * 3
    res = pl.pallas_call(
        body, name=name,
        out_shape=[jax.ShapeDtypeStruct((NDEV, D), F32), jax.ShapeDtypeStruct((NDEV * taps.shape[0], taps.shape[1]), F32),
                   jax.ShapeDtypeStruct((NDEV * NDEV, nc), F32)] + rider.out_shapes,
        in_specs=[VMEM] * 4 + [HBM] * n, out_specs=[VMEM] * 3 + [HBM] * n,
        scratch_shapes=[pltpu.VMEM((NDEV * SUBLANES, D), F32), pltpu.VMEM((NDEV, nc), F32)] + small_sems + rider.sems,
        compiler_params=_params(None),
    )(c_rows, taps, ada_w, ada_b, *shards)
    return res[0], res[1], res[2], res[3:]


class _Rider:
    def __init__(self, ins, out_shapes, sems, start, finish, mid=None, relay=None):
        self.ins, self.out_shapes, self.sems = list(ins), list(out_shapes), list(sems)
        self.start, self.finish, self.mid, self.relay = start, finish, mid, relay


def _gather_rider(shards):
    n = len(shards)

    def setup(ins, outs, sems):
        send_sems, recv_sems, local_sems = sems
        x, y, c = _position()
        places = dict(me=(x, y, c), sibling=(x, y, 1 - c), xn=(1 - x, y, c), yn=(x, 1 - y, c), diagonal=(1 - x, 1 - y, c),
                      passed_on=(x ^ c, y ^ (1 - c), c), passed_to=(x ^ (1 - c), y ^ c, c))

        def copy(a, k, block, to, own=False):
            slot = outs[a].at[_index(block)]
            return pltpu.make_async_remote_copy(
                src_ref=ins[a] if own else slot, dst_ref=slot,
                send_sem=send_sems.at[k, a], recv_sem=recv_sems.at[k, a], device_id=to, device_id_type=MESH)

        def local(a):
            return pltpu.make_async_copy(ins[a], outs[a].at[_index(places["me"])], local_sems.at[a])

        return places, copy, local

    def start(ins, outs, sems):
        p, copy, local = setup(ins, outs, sems)
        for a in range(n):
            local(a).start()
            for k, to in enumerate(("sibling", "xn", "yn")):
                copy(a, k, p["me"], p[to], own=True).start()

    def mid(ins, outs, sems):
        p, copy, _ = setup(ins, outs, sems)
        for a in range(n):
            copy(a, 1, p["xn"], p["me"]).wait_recv()
            copy(a, 2, p["yn"], p["me"]).wait_recv()
            copy(a, 3, p["passed_on"], p["passed_to"]).start()
            copy(a, 4, p["xn"], p["sibling"]).start()
            copy(a, 5, p["yn"], p["sibling"]).start()

    def relay(ins, outs, sems):
        p, copy, _ = setup(ins, outs, sems)
        for a in range(n):
            copy(a, 3, p["diagonal"], p["me"]).wait_recv()
            copy(a, 6, p["diagonal"], p["sibling"]).start()

    def finish(ins, outs, sems):
        p, copy, local = setup(ins, outs, sems)
        x, y, c = p["me"]
        for a in range(n):
            for k, block in ((0, (x, y, 1 - c)), (4, (1 - x, y, 1 - c)), (5, (x, 1 - y, 1 - c)), (6, (1 - x, 1 - y, 1 - c))):
                copy(a, k, block, p["me"]).wait_recv()
            for k, to in enumerate(("sibling", "xn", "yn")):
                copy(a, k, p["me"], p[to], own=True).wait_send()
            copy(a, 3, p["passed_on"], p["passed_to"]).wait_send()
            for k, block in ((4, "xn"), (5, "yn"), (6, "diagonal")):
                copy(a, k, p[block], p["sibling"]).wait_send()
            local(a).wait()

    return _Rider(shards, [jax.ShapeDtypeStruct((NDEV, *s.shape), s.dtype) for s in shards],
                  [pltpu.SemaphoreType.DMA((7, n)), pltpu.SemaphoreType.DMA((7, n)), pltpu.SemaphoreType.DMA((n,))],
                  start, finish, mid, relay)


def _pair_rider(parts):
    n = len(parts)

    def copies(ins, outs, sems):
        send_sems, recv_sems = sems
        x, y, c = _position()
        q = 2 * x + y
        return [pltpu.make_async_remote_copy(
            src_ref=ins[a].at[2 * (q ^ k) + (1 - c)], dst_ref=outs[a].at[k],
            send_sem=send_sems.at[k, a], recv_sem=recv_sems.at[k, a], device_id=(x, y, 1 - c), device_id_type=MESH)
            for a in range(n) for k in range(N_CHIPS)]

    def start(ins, outs, sems):
        for cp in copies(ins, outs, sems):
            cp.start()

    def finish(ins, outs, sems):
        for cp in copies(ins, outs, sems):
            cp.wait()

    return _Rider(parts, [jax.ShapeDtypeStruct((N_CHIPS, *p.shape[1:]), p.dtype) for p in parts],
                  [pltpu.SemaphoreType.DMA((N_CHIPS, n)), pltpu.SemaphoreType.DMA((N_CHIPS, n))], start, finish)


NEIGHBOURS = (1, 2)
DIAGONAL = (3,)
OTHER_CHIPS = NEIGHBOURS + DIAGONAL


def _chip_rider(sums, ks=OTHER_CHIPS):
    n = len(sums)

    def copies(ins, outs, sems):
        send_sems, recv_sems = sems
        me = _position()
        return [pltpu.make_async_remote_copy(
            src_ref=ins[a].at[k], dst_ref=outs[a].at[j],
            send_sem=send_sems.at[j, a], recv_sem=recv_sems.at[j, a], device_id=_flip(me, 2 * k), device_id_type=MESH)
            for a in range(n) for j, k in enumerate(ks)]

    def start(ins, outs, sems):
        for cp in copies(ins, outs, sems):
            cp.start()

    def finish(ins, outs, sems):
        for cp in copies(ins, outs, sems):
            cp.wait()

    return _Rider(sums, [jax.ShapeDtypeStruct((len(ks), *s.shape[1:]), s.dtype) for s in sums],
                  [pltpu.SemaphoreType.DMA((len(ks), n)), pltpu.SemaphoreType.DMA((len(ks), n))], start, finish)


def _grid_edge(grid, last):
    cond = None
    for d, n in enumerate(grid):
        here = pl.program_id(d) == (n - 1 if last else 0)
        cond = here if cond is None else jnp.logical_and(cond, here)
    return cond


def _call(name, compute, grid, ins, in_specs, out_shapes, out_specs, scratch_shapes, semantics, rider=None, aliases=None):
    riders = [rider] if isinstance(rider, _Rider) else list(rider or [])
    n_in, n_out, n_scr = len(ins), len(out_shapes), len(scratch_shapes)
    n_rin, n_rout, n_rsem = [sum(len(part(r)) for r in riders) for part in (lambda r: r.ins, lambda r: r.out_shapes, lambda r: r.sems)]
    cuts = [0, n_in, n_in + n_rin, n_in + n_rin + n_out, n_in + n_rin + n_out + n_rout, n_in + n_rin + n_out + n_rout + n_scr]

    def body(*refs):
        in_refs, rin_refs, out_refs, rout_refs, scr_refs = [refs[a:b] for a, b in zip(cuts[:-1], cuts[1:])]
        rsem_refs = refs[cuts[-1]:]
        mine, at = [], [0, 0, 0]
        for r in riders:
            mine.append((r, rin_refs[at[0]:at[0] + len(r.ins)], rout_refs[at[1]:at[1] + len(r.out_shapes)],
                         rsem_refs[at[2]:at[2] + len(r.sems)]))
            at = [at[0] + len(r.ins), at[1] + len(r.out_shapes), at[2] + len(r.sems)]
        if riders:
            @pl.when(_grid_edge(grid, last=False))
            def _():
                for r, a, b, c in mine:
                    r.start(a, b, c)

        if any(r.mid for r in riders):
            step, steps = 0, 1
            for d, size in enumerate(grid):
                step, steps = step * size + pl.program_id(d), steps * size

            @pl.when(step == steps * 5 // 8)
            def _():
                for r, a, b, c in mine:
                    if r.mid:
                        r.mid(a, b, c)

        if any(r.relay for r in riders):
            @pl.when(_grid_edge(grid, last=True))
            def _():
                for r, a, b, c in mine:
                    if r.relay:
                        r.relay(a, b, c)

        compute(in_refs, out_refs, scr_refs)
        if riders:
            @pl.when(_grid_edge(grid, last=True))
            def _():
                for r, a, b, c in mine:
                    r.finish(a, b, c)

    res = pl.pallas_call(
        body, name=name, grid=grid,
        out_shape=list(out_shapes) + [s for r in riders for s in r.out_shapes],
        in_specs=list(in_specs) + [HBM] * n_rin, out_specs=list(out_specs) + [HBM] * n_rout,
        scratch_shapes=list(scratch_shapes) + [s for r in riders for s in r.sems],
        input_output_aliases=aliases or {}, compiler_params=_params(semantics),
    )(*ins, *[a for r in riders for a in r.ins])
    return (res[:n_out], res[n_out:]) if riders else res


def _pair_add(name, parts, from_sibling, slots):
    n = len(parts)

    def body(s_ref, *refs):
        for a in range(n):
            refs[2 * n + a][...] = (refs[a][...].astype(F32) + refs[n + a][...].astype(F32)).astype(refs[2 * n + a].dtype)

    def slab(p, picked):
        _, r, c = p.shape
        return pl.BlockSpec((None, r, c), (lambda k, s: (s[k], 0, 0)) if picked else (lambda k, s: (k, 0, 0)))

    return pl.pallas_call(
        body, name=name,
        grid_spec=pltpu.PrefetchScalarGridSpec(
            num_scalar_prefetch=1, grid=(N_CHIPS,),
            in_specs=[slab(p, True) for p in parts] + [slab(p, False) for p in parts],
            out_specs=[slab(p, False) for p in parts]),
        out_shape=[jax.ShapeDtypeStruct((N_CHIPS, *p.shape[1:]), p.dtype) for p in parts],
        compiler_params=_params(("arbitrary",)),
    )(slots, *parts, *from_sibling)


def _mm(name, pairs, dims, grid, nk, out_shapes, out_specs, extras=(), extra_specs=(), epilogue=None, acc_shape=None, rider=None):
    n_pairs = len(pairs)

    def compute(ins, outs, scratch):
        def partial_sum():
            total = None
            for p in range(n_pairs):
                d = lax.dot_general(ins[2 * p][...], ins[2 * p + 1][...], dims, preferred_element_type=F32)
                total = d if total is None else total + d
            return total

        def finish(r):
            ex = [e[...] for e in ins[2 * n_pairs:]]
            res = epilogue(r, *ex) if epilogue is not None else (r,)
            for o, v in zip(outs, res):
                o[...] = v.astype(o.dtype)

        if nk == 1:
            finish(partial_sum())
        else:
            acc = scratch[0]
            k = pl.program_id(2)

            @pl.when(k == 0)
            def _():
                acc[...] = partial_sum()

            @pl.when(k > 0)
            def _():
                acc[...] += partial_sum()

            @pl.when(k == nk - 1)
            def _():
                finish(acc[...])

    operands, specs = [], []
    for a, a_spec, b, b_spec in pairs:
        operands += [a, b]
        specs += [a_spec, b_spec]
    return _call(name, compute, grid, operands + list(extras), specs + list(extra_specs), out_shapes, out_specs,
                 [pltpu.VMEM(acc_shape, F32)] if nk > 1 else [], ("parallel", "parallel", "arbitrary"), rider)


def _single(res, rider):
    return (res[0][0], res[1]) if rider else res[0]


def _silu(x):
    return x * jax.nn.sigmoid(x)


def _ffn_up(name, h, wg, wu, rider=None):
    t = h.shape[0]
    tm = min(t, 2048)
    nb = F // NDEV

    def compute(ins, outs, _):
        hv = ins[0][...]
        g = jnp.dot(hv, ins[1][...], preferred_element_type=F32)
        u = jnp.dot(hv, ins[2][...], preferred_element_type=F32)
        outs[0][...] = g.astype(BF)
        outs[1][...] = u.astype(BF)
        outs[2][...] = (_silu(g) * u).astype(BF)

    w_spec = pl.BlockSpec((None, D, nb), lambda i, j: (j, 0, 0))
    o_spec = pl.BlockSpec((tm, nb), lambda i, j: (i, j))
    return _call(name, compute, (t // tm, NDEV), [h, wg, wu], [pl.BlockSpec((tm, D), lambda i, j: (i, 0)), w_spec, w_spec],
                 [jax.ShapeDtypeStruct((t, F), BF)] * 3, [o_spec] * 3, [], ("parallel", "arbitrary"), rider)


def _mm_nn(name, a, b, tm, tn, tk, extras=(), extra_specs=(), epilogue=None, out_dtypes=(F32,), rider=None):
    m, kk = a.shape
    n = b.shape[1]
    nk = kk // tk
    return _mm(
        name, [(a, pl.BlockSpec((tm, tk), lambda i, j, k: (i, k)), b, pl.BlockSpec((tk, tn), lambda i, j, k: (k, j)))], NN,
        (m // tm, n // tn, nk), nk,
        [jax.ShapeDtypeStruct((m, n), dt) for dt in out_dtypes],
        [pl.BlockSpec((tm, tn), lambda i, j, k: (i, j))] * len(out_dtypes),
        extras, extra_specs, epilogue, (tm, tn), rider)


def _mm_nn_blocked(name, a, b3, tm, rider=None):
    m = a.shape[0]
    nb = b3.shape[2]
    return _single(_mm(
        name, [(a, pl.BlockSpec((tm, D), lambda i, j, k: (i, 0)), b3, pl.BlockSpec((None, D, nb), lambda i, j, k: (j, 0, 0)))], NN,
        (m // tm, NDEV, 1), 1,
        [jax.ShapeDtypeStruct((m, NDEV * nb), F32)], [pl.BlockSpec((tm, nb), lambda i, j, k: (i, j))], rider=rider), rider)


def _mm_nt(name, a, b, tm, tn, out_dtypes=(F32,), extras=(), extra_specs=(), epilogue=None, rider=None):
    m, kk = a.shape
    n = b.shape[0]
    return _mm(
        name, [(a, pl.BlockSpec((tm, kk), lambda i, j, k: (i, 0)), b, pl.BlockSpec((tn, kk), lambda i, j, k: (j, 0)))], NT,
        (m // tm, n // tn, 1), 1,
        [jax.ShapeDtypeStruct((m, n), dt) for dt in out_dtypes],
        [pl.BlockSpec((tm, tn), lambda i, j, k: (i, j))] * len(out_dtypes),
        extras, extra_specs, epilogue, rider=rider)


def _mm_nt_blocked(name, a_list, b3_list, tm, rider=None):
    m = a_list[0].shape[0]
    nb = b3_list[0].shape[2]
    pairs = [(a, pl.BlockSpec((tm, nb), lambda i, j, k: (i, k)), b3, pl.BlockSpec((None, D, nb), lambda i, j, k: (k, 0, 0)))
             for a, b3 in zip(a_list, b3_list)]
    return _single(_mm(name, pairs, NT, (m // tm, 1, NDEV), NDEV,
                       [jax.ShapeDtypeStruct((m, D), F32)], [pl.BlockSpec((tm, D), lambda i, j, k: (i, 0))],
                       acc_shape=(tm, D), rider=rider), rider)


def _mm_tn(name, a, b, tm, tn, rider=None):
    t, m = a.shape
    n = b.shape[1]
    return _single(_mm(
        name, [(a, pl.BlockSpec((t, tm), lambda i, j, k: (0, i)), b, pl.BlockSpec((t, tn), lambda i, j, k: (0, j)))], TN,
        (m // tm, n // tn, 1), 1,
        [jax.ShapeDtypeStruct((m, n), BF)], [pl.BlockSpec((tm, tn), lambda i, j, k: (i, j))], rider=rider), rider)


def _mm_tn_blocked(name, a, b, rider=None):
    t = a.shape[0]
    nb = b.shape[1] // NDEV
    return _single(_mm(
        name, [(a, pl.BlockSpec((t, D), lambda i, j, k: (0, 0)), b, pl.BlockSpec((t, nb), lambda i, j, k: (0, j)))], TN,
        (1, NDEV, 1), 1,
        [jax.ShapeDtypeStruct((NDEV, D, nb), BF)], [pl.BlockSpec((None, D, nb), lambda i, j, k: (j, 0, 0))], rider=rider), rider)


def _dw_gate_up(name, h, dgate, dup, rider=None):
    t = h.shape[0]
    nb = F // NDEV

    def compute(ins, outs, _):
        hv = ins[0][...]
        outs[0][...] = lax.dot_general(hv, ins[1][...], TN, preferred_element_type=F32).astype(BF)
        outs[1][...] = lax.dot_general(hv, ins[2][...], TN, preferred_element_type=F32).astype(BF)

    d_spec = pl.BlockSpec((t, nb), lambda j: (0, j))
    o_spec = pl.BlockSpec((None, D, nb), lambda j: (j, 0, 0))
    return _call(name, compute, (NDEV,), [h, dgate, dup], [pl.BlockSpec((t, D), lambda j: (0, 0)), d_spec, d_spec],
                 [jax.ShapeDtypeStruct((NDEV, D, nb), BF)] * 2, [o_spec] * 2, [], ("arbitrary",), rider)


def _dw_square(name, pairs):
    t = pairs[0][0].shape[0]
    tm = 512
    n = len(pairs)

    def compute(ins, outs, _):
        for p in range(n):
            outs[p][...] = lax.dot_general(ins[2 * p][...], ins[2 * p + 1][...], TN, preferred_element_type=F32).astype(BF)

    return _call(name, compute, (D // tm,), [x for pair in pairs for x in pair],
                 [pl.BlockSpec((t, tm), lambda i: (0, i)), pl.BlockSpec((t, D), lambda i: (0, 0))] * n,
                 [jax.ShapeDtypeStruct((D, D), BF)] * n, [pl.BlockSpec((tm, D), lambda i: (i, 0))] * n, [], ("arbitrary",))


def _swiglu_bwd(da, gate, up):
    gate = gate.astype(F32)
    s = jax.nn.sigmoid(gate)
    return da * up.astype(F32) * (s * (1.0 + gate * (1.0 - s))), da * (gate * s)


def _ffn_dact_dh(name, df, wd, gate, up, wg, wu, rider=None):
    t = df.shape[0]
    tm = min(t, 1024)
    nb = F // NDEV

    def compute(ins, outs, scr):
        acc = scr[0]
        j = pl.program_id(1)
        da = lax.dot_general(ins[0][...], ins[1][...], NT, preferred_element_type=F32)
        dgate, dup = _swiglu_bwd(da, ins[2][...], ins[3][...])
        dgate, dup = dgate.astype(BF), dup.astype(BF)
        outs[0][...] = dgate
        outs[1][...] = dup
        part = (lax.dot_general(dgate, ins[4][...], NT, preferred_element_type=F32)
                + lax.dot_general(dup, ins[5][...], NT, preferred_element_type=F32))

        @pl.when(j == 0)
        def _():
            acc[...] = part

        @pl.when(j > 0)
        def _():
            acc[...] += part

        @pl.when(j == NDEV - 1)
        def _():
            outs[2][...] = acc[...]

    blk = pl.BlockSpec((tm, nb), lambda i, j: (i, j))
    w3 = pl.BlockSpec((None, D, nb), lambda i, j: (j, 0, 0))
    row = pl.BlockSpec((tm, D), lambda i, j: (i, 0))
    return _call(name, compute, (t // tm, NDEV), [df, wd, gate, up, wg, wu],
                 [row, pl.BlockSpec((nb, D), lambda i, j: (j, 0)), blk, blk, w3, w3],
                 [jax.ShapeDtypeStruct((t, F), BF)] * 2 + [jax.ShapeDtypeStruct((t, D), F32)], [blk, blk, row],
                 [pltpu.VMEM((tm, D), F32)], ("parallel", "arbitrary"), rider)


def _rowcall(name, fn, ins, in_specs, n_row_out, out_shapes, out_specs, grid, scratch_shapes=(), rider=None, aliases=None):
    def accumulate(o, v, i):
        @pl.when(i == 0)
        def _():
            o[...] = v.astype(o.dtype)

        @pl.when(i > 0)
        def _():
            o[...] += v.astype(o.dtype)

    def compute(in_refs, out_refs, scr):
        i = pl.program_id(0)
        vals = fn(i, in_refs, scr)
        for idx, (o, v) in enumerate(zip(out_refs, vals)):
            if idx < n_row_out:
                o[...] = v.astype(o.dtype)
            else:
                accumulate(o, v, i)

    return _call(name, compute, (grid,), ins, in_specs, out_shapes, out_specs, list(scratch_shapes), ("arbitrary",), rider, aliases)


def _rows(tr, w=D, cb=0):
    return pl.BlockSpec((tr, w), lambda i: (i, cb))


def _whole(shape):
    nd = len(shape)
    return pl.BlockSpec(shape, lambda i: (0,) * nd)


def _vec(n=1):
    return jax.ShapeDtypeStruct((n, D), F32)


def _rms_mod(x, gain, sc, sh):
    y = x * lax.rsqrt(jnp.mean(x * x, axis=-1, keepdims=True) + EPS)
    return (y * gain) * (1.0 + sc) + sh


def _layer_norm(x, g, b):
    mu = jnp.mean(x, axis=-1, keepdims=True)
    var = jnp.mean(jnp.square(x - mu), axis=-1, keepdims=True)
    return (x - mu) * lax.rsqrt(var + EPS) * g + b


def _norm_mod(name, x, gain, sc, sh):
    t = x.shape[0]
    tr = min(t, 256)

    def fn(i, r, _):
        return [_rms_mod(r[0][...], r[1][...], r[2][...], r[3][...])]

    return _rowcall(name, fn, [x, gain, sc, sh], [_rows(tr)] + [_whole((1, D))] * 3, 1,
                    [jax.ShapeDtypeStruct((t, D), BF)], [_rows(tr)], t // tr)[0]


def _gate_grads(dx, f, g, scale):
    return scale * g * dx, jnp.sum(scale * dx * f.astype(F32), axis=0, keepdims=True)


def _norm_mod_bwd(name, x, gain, sc, sh, dh, dres, below=None, rider=None):
    t = x.shape[0]
    tr = min(t, 256)

    def fn(i, r, _):
        _, vjp = jax.vjp(_rms_mod, r[0][...], r[1][...], r[2][...], r[3][...])
        dx, dgain, dsc, dsh = vjp(r[4][...])
        dx = dx + r[5][...]
        if below is None:
            return [dx, dgain, dsc, dsh]
        df, dg = _gate_grads(dx, r[6][...], r[7][...], below[2])
        return [dx, df, dgain, dsc, dsh, dg]

    ins, specs = [x, gain, sc, sh, dh, dres], [_rows(tr)] + [_whole((1, D))] * 3 + [_rows(tr)] * 2
    outs, out_specs = [jax.ShapeDtypeStruct((t, D), F32)], [_rows(tr)]
    if below is not None:
        ins, specs = ins + [below[0], below[1]], specs + [_rows(tr), _whole((1, D))]
        outs, out_specs = outs + [jax.ShapeDtypeStruct((t, D), BF)], out_specs + [_rows(tr)]
    n_vec = 3 if below is None else 4
    return _rowcall(name, fn, ins, specs, len(outs), outs + [_vec()] * n_vec, out_specs + [_whole((1, D))] * n_vec, t // tr,
                    rider=rider)


def _sgu_pre(up, vp, bu, bv, ln_g, ln_b):
    return jax.nn.gelu(up + bu), _layer_norm(jax.nn.gelu(vp + bv), ln_g, ln_b)


def _causal(w_ref, h):
    rows = lax.broadcasted_iota(jnp.int32, (CHUNK, CHUNK), 0)
    cols = lax.broadcasted_iota(jnp.int32, (CHUNK, CHUNK), 1)
    return jnp.where(cols <= rows, w_ref[h], 0.0)


def _sgu(name, proj, b_in, ln_g, ln_b, w_s, bias_full, rider=None):
    t = proj.shape[0]

    def fn(i, r, _):
        u, v = _sgu_pre(r[0][...], r[1][...], r[2][...], r[3][...], r[4][...], r[5][...])
        vb = v.astype(BF)
        mixed = [jnp.dot(_causal(r[6], h).astype(BF), vb[:, h * CHUNK:(h + 1) * CHUNK], preferred_element_type=F32)
                 for h in range(HEADS)]
        return [u * (jnp.concatenate(mixed, axis=1) + r[7][...])]

    return _rowcall(
        name, fn, [proj, proj, b_in, b_in, ln_g, ln_b, w_s, bias_full],
        [_rows(CHUNK, D, 0), _rows(CHUNK, D, 1), pl.BlockSpec((1, D), lambda i: (0, 0)), pl.BlockSpec((1, D), lambda i: (0, 1)),
         _whole((1, D)), _whole((1, D)), _whole((HEADS, CHUNK, CHUNK)), _whole((CHUNK, D))],
        1, [jax.ShapeDtypeStruct((t, D), BF)], [_rows(CHUNK)], t // CHUNK, rider=rider)


def _sgu_bwd(name, proj, b_in, ln_g, ln_b, w_s, bias_full, dout, dproj, rider=None):
    t = proj.shape[0]

    def fn(i, r, _):
        (u, v), vjp = jax.vjp(_sgu_pre, r[0][...], r[1][...], r[2][...], r[3][...], r[4][...], r[5][...])
        vb = v.astype(BF)
        d = r[8][...]
        masks = [_causal(r[6], h).astype(BF) for h in range(HEADS)]
        cols = [slice(h * CHUNK, (h + 1) * CHUNK) for h in range(HEADS)]
        mixed = jnp.concatenate([jnp.dot(masks[h], vb[:, cols[h]], preferred_element_type=F32) for h in range(HEADS)], axis=1)
        du = d * (mixed + r[7][...])
        dmix = d * u
        dmb = dmix.astype(BF)
        dv = jnp.concatenate([lax.dot_general(masks[h], dmb[:, cols[h]], TN, preferred_element_type=F32) for h in range(HEADS)], axis=1)
        rows = lax.broadcasted_iota(jnp.int32, (CHUNK, CHUNK), 0)
        lanes = lax.broadcasted_iota(jnp.int32, (CHUNK, CHUNK), 1)
        dws = jnp.stack([jnp.where(lanes <= rows, lax.dot_general(dmb[:, cols[h]], vb[:, cols[h]], NT, preferred_element_type=F32), 0.0)
                         for h in range(HEADS)])
        dbs = jnp.zeros((CHUNK, CHUNK), F32)
        for h in range(HEADS):
            dbs = dbs + jnp.where(lanes == h, jnp.sum(dmix[:, cols[h]], axis=1, keepdims=True), 0.0)
        dup, dvp, dbu, dbv, dg, db = vjp((du, dv))
        return [jnp.concatenate([dup, dvp], axis=1), dbu, dbv, dg, db, dws, dbs]

    return _rowcall(
        name, fn, [proj, proj, b_in, b_in, ln_g, ln_b, w_s, bias_full, dout, dproj],
        [_rows(CHUNK, D, 0), _rows(CHUNK, D, 1), pl.BlockSpec((1, D), lambda i: (0, 0)), pl.BlockSpec((1, D), lambda i: (0, 1)),
         _whole((1, D)), _whole((1, D)), _whole((HEADS, CHUNK, CHUNK)), _whole((CHUNK, D)), _rows(CHUNK),
         pl.BlockSpec(memory_space=pl.ANY)],
        1, [jax.ShapeDtypeStruct(dproj.shape, dproj.dtype)] + [_vec()] * 4
        + [jax.ShapeDtypeStruct((HEADS, CHUNK, CHUNK), F32), jax.ShapeDtypeStruct((CHUNK, CHUNK), F32)],
        [pl.BlockSpec((CHUNK, 2 * D), lambda i: (i, 0))] + [_whole((1, D))] * 4 + [_whole((HEADS, CHUNK, CHUNK)), _whole((CHUNK, CHUNK))],
        t // CHUNK, rider=rider, aliases={9: 0})


def _halo_before(tr, cb):
    return pl.BlockSpec((HALO, D), lambda i: (jnp.maximum(i * (tr // HALO) - 1, 0), cb))


def _halo_after(tr, cb, n_tiles):
    return pl.BlockSpec((HALO, D), lambda i: (jnp.minimum((i + 1) * (tr // HALO), n_tiles * (tr // HALO) - 1), cb))


def _ln_silu(z, g, b):
    return _silu(_layer_norm(z, g, b))


SUBLANES = 8
LANES = 128
CONV_STRIP = 16
DW_STRIP = 64


def _shifted_copies(buf, copies, rows):
    for b in range(1, SUBLANES):
        copies[b - 1, pl.ds(0, rows), :] = buf[pl.ds(b, rows), :]


def _shifted(buf, copies, offset, start, rows, lanes=slice(None)):
    at = pl.ds(pl.multiple_of(start + SUBLANES * (offset // SUBLANES), SUBLANES), rows)
    return buf[at, lanes] if offset % SUBLANES == 0 else copies[offset % SUBLANES - 1, at, lanes]


def _accumulate(o, v, i):
    @pl.when(i == 0)
    def _():
        o[...] = v.astype(o.dtype)

    @pl.when(i > 0)
    def _():
        o[...] += v.astype(o.dtype)


def _conv(name, proj, b_in, conv_w, conv_b, ln_g, ln_b, rider=None):
    t = proj.shape[0]
    tr = min(t, 256)

    def compute(r, outs, scr):
        zbuf, zs = scr
        i = pl.program_id(0)
        bv, bg = r[4][...], r[5][...]
        z0 = (r[0][...] + bv) * jax.nn.sigmoid(r[1][...] + bg)
        before = (r[2][...] + bv) * jax.nn.sigmoid(r[3][...] + bg)
        zbuf[pl.ds(0, HALO), :] = jnp.where(i > 0, before, 0.0)
        zbuf[pl.ds(HALO, tr), :] = z0
        outs[0][...] = z0
        _shifted_copies(zbuf, zs, tr + HALO - SUBLANES)

        def strip(s, carry):
            r0 = s * CONV_STRIP
            acc = jnp.zeros((CONV_STRIP, D), F32) + r[7][...]
            for k in range(KW):
                acc = acc + r[6][k:k + 1, :] * _shifted(zbuf, zs, HALO - (KW - 1) + k, r0, CONV_STRIP)
            outs[1][pl.ds(pl.multiple_of(r0, SUBLANES), CONV_STRIP), :] = acc
            return carry

        lax.fori_loop(0, tr // CONV_STRIP, strip, 0)
        outs[2][...] = _ln_silu(outs[1][...], r[8][...], r[9][...]).astype(BF)

    return _call(
        name, compute, (t // tr,), [proj, proj, proj, proj, b_in, b_in, conv_w, conv_b, ln_g, ln_b],
        [_rows(tr, D, 2), _rows(tr, D, 3), _halo_before(tr, 2), _halo_before(tr, 3),
         pl.BlockSpec((1, D), lambda i: (0, 2)), pl.BlockSpec((1, D), lambda i: (0, 3)),
         _whole((HALO, D)), _whole((1, D)), _whole((1, D)), _whole((1, D))],
        [jax.ShapeDtypeStruct((t, D), F32), jax.ShapeDtypeStruct((t, D), F32), jax.ShapeDtypeStruct((t, D), BF)],
        [_rows(tr)] * 3, [pltpu.VMEM((tr + HALO, D), F32), pltpu.VMEM((SUBLANES - 1, tr + HALO, D), F32)], ("arbitrary",), rider)


def _conv_bwd(name, proj, b_in, conv_w, ln_g, ln_b, z0, z1, dz3, dproj, rider=None):
    t = proj.shape[0]
    tr = min(t, 256)
    n_tiles = t // tr

    def compute(r, outs, scr):
        zbuf, dbuf, zs, ds, dwacc = scr
        i = pl.program_id(0)
        g, b = r[5][...], r[6][...]
        zero_row = jnp.zeros((1, D), F32)
        _, vjp = jax.vjp(_ln_silu, r[9][...], g, b)
        dz1, dg, db = vjp(r[11][...])
        dcb = jnp.sum(dz1, axis=0, keepdims=True)
        _, vjp_after = jax.vjp(_ln_silu, r[10][...], g, b)
        dz1_after = vjp_after(r[12][...])[0]
        dbuf[pl.ds(0, tr), :] = dz1
        dbuf[pl.ds(tr, HALO), :] = jnp.where(i < n_tiles - 1, dz1_after, 0.0)
        zbuf[pl.ds(0, HALO), :] = jnp.where(i > 0, r[8][...], 0.0)
        zbuf[pl.ds(HALO, tr), :] = r[7][...]
        _shifted_copies(dbuf, ds, tr + HALO - SUBLANES)
        _shifted_copies(zbuf, zs, tr + HALO - SUBLANES)

        def dz0_strip(s, carry):
            r0 = s * CONV_STRIP
            at = pl.ds(pl.multiple_of(r0, CONV_STRIP), CONV_STRIP)
            acc = jnp.zeros((CONV_STRIP, D), F32)
            for k in range(KW):
                acc = acc + r[4][k:k + 1, :] * _shifted(dbuf, ds, KW - 1 - k, r0, CONV_STRIP)
            a = r[0][at, :] + r[2][...]
            sg = jax.nn.sigmoid(r[1][at, :] + r[3][...])
            dcv = acc * sg
            dcg = acc * a * sg * (1.0 - sg)
            outs[0][at, :] = jnp.concatenate([dcv, dcg], axis=1).astype(BF)
            return carry[0] + jnp.sum(dcv, axis=0, keepdims=True), carry[1] + jnp.sum(dcg, axis=0, keepdims=True)

        dbv, dbg = lax.fori_loop(0, tr // CONV_STRIP, dz0_strip, (zero_row, zero_row))

        for lb in range(D // LANES):
            lanes = slice(lb * LANES, (lb + 1) * LANES)

            def dw_strip(s, accs, lanes=lanes):
                r0 = s * DW_STRIP
                dz = dbuf[pl.ds(pl.multiple_of(r0, SUBLANES), DW_STRIP), lanes]
                out = []
                for k in range(KW):
                    prod = dz * _shifted(zbuf, zs, HALO - (KW - 1) + k, r0, DW_STRIP, lanes)
                    part = prod[0:SUBLANES]
                    for q in range(1, DW_STRIP // SUBLANES):
                        part = part + prod[q * SUBLANES:(q + 1) * SUBLANES]
                    out.append(accs[k] + part)
                return tuple(out)

            accs = lax.fori_loop(0, tr // DW_STRIP, dw_strip, tuple(jnp.zeros((SUBLANES, LANES), F32) for _ in range(KW)))
            for k in range(KW):
                dwacc[pl.ds(k * SUBLANES, SUBLANES), lanes] = accs[k]
        dw_rows = [jnp.sum(dwacc[pl.ds(k * SUBLANES, SUBLANES), :], axis=0, keepdims=True) for k in range(KW)]
        dw_rows.append(jnp.zeros((HALO - KW, D), F32))
        for o, v in zip(outs[1:], (dbv, dbg, jnp.concatenate(dw_rows, axis=0), dcb, dg, db)):
            _accumulate(o, v, i)

    wide = pl.BlockSpec((tr, 2 * D), lambda i: (i, 1))
    return _call(
        name, compute, (n_tiles,), [proj, proj, b_in, b_in, conv_w, ln_g, ln_b, z0, z0, z1, z1, dz3, dz3, dproj],
        [_rows(tr, D, 2), _rows(tr, D, 3), pl.BlockSpec((1, D), lambda i: (0, 2)), pl.BlockSpec((1, D), lambda i: (0, 3)),
         _whole((HALO, D)), _whole((1, D)), _whole((1, D)),
         _rows(tr), _halo_before(tr, 0), _rows(tr), _halo_after(tr, 0, n_tiles), _rows(tr), _halo_after(tr, 0, n_tiles),
         pl.BlockSpec(memory_space=pl.ANY)],
        [jax.ShapeDtypeStruct(dproj.shape, dproj.dtype), _vec(), _vec(), _vec(HALO), _vec(), _vec(), _vec()],
        [wide] + [_whole((1, D))] * 2 + [_whole((HALO, D))] + [_whole((1, D))] * 3,
        [pltpu.VMEM((tr + HALO, D), F32), pltpu.VMEM((tr + HALO, D), F32),
         pltpu.VMEM((SUBLANES - 1, tr + HALO, D), F32), pltpu.VMEM((SUBLANES - 1, tr + HALO, D), F32),
         pltpu.VMEM((HALO * SUBLANES, D), F32)],
        ("arbitrary",), rider, aliases={13: 0})


def _merge_fn(ga, gb, bga, bgb, ya, yb):
    return jax.nn.sigmoid(ga + bga) * ya + jax.nn.sigmoid(gb + bgb) * yb


def _mix_tail(name, ua, z3, proj, b_in, wa, wb, wo, x, g, next_norm):
    t = ua.shape[0]
    tr = min(t, 256)

    def compute(r, outs, _):
        ya = jnp.dot(r[0][...], r[6][...], preferred_element_type=F32)
        yb = jnp.dot(r[1][...], r[7][...], preferred_element_type=F32)
        merged = _merge_fn(r[2][...], r[3][...], r[4][...], r[5][...], ya, yb).astype(BF)
        y = jnp.dot(merged, r[8][...], preferred_element_type=F32)
        x_out = r[9][...] + r[10][...] * y
        for o, v in zip(outs, (ya, yb, merged, y, x_out, _rms_mod(x_out, r[11][...], r[12][...], r[13][...]))):
            o[...] = v.astype(o.dtype)

    row = _whole((1, D))
    return _call(
        name, compute, (t // tr,), [ua, z3, proj, proj, b_in, b_in, wa, wb, wo, x, g, *next_norm],
        [_rows(tr), _rows(tr), _rows(tr, D, 4), _rows(tr, D, 5), pl.BlockSpec((1, D), lambda i: (0, 4)),
         pl.BlockSpec((1, D), lambda i: (0, 5)), _whole((D, D)), _whole((D, D)), _whole((D, D)), _rows(tr), row, row, row, row],
        [jax.ShapeDtypeStruct((t, D), dt) for dt in (F32, F32, BF, BF, F32, BF)], [_rows(tr)] * 6, [], ("arbitrary",))


def _mix_tail_bwd(name, dy, proj, b_in, ya, yb, wa, wb, wo, rider=None):
    t = proj.shape[0]
    tr = min(t, 256)

    def compute(r, outs, _):
        i = pl.program_id(0)
        dm = lax.dot_general(r[0][...], r[9][...], NT, preferred_element_type=F32)
        _, vjp = jax.vjp(_merge_fn, *[x[...] for x in r[1:7]])
        dga, dgb, dbga, dbgb, dya, dyb = vjp(dm)
        dya, dyb = dya.astype(BF), dyb.astype(BF)
        outs[0][...] = jnp.concatenate([dga, dgb], axis=1).astype(BF)
        outs[1][...] = dya
        outs[2][...] = dyb
        outs[3][...] = lax.dot_general(dya, r[7][...], NT, preferred_element_type=F32)
        outs[4][...] = lax.dot_general(dyb, r[8][...], NT, preferred_element_type=F32)
        _accumulate(outs[5], dbga, i)
        _accumulate(outs[6], dbgb, i)

    return _call(
        name, compute, (t // tr,), [dy, proj, proj, b_in, b_in, ya, yb, wa, wb, wo],
        [_rows(tr), _rows(tr, D, 4), _rows(tr, D, 5), pl.BlockSpec((1, D), lambda i: (0, 4)), pl.BlockSpec((1, D), lambda i: (0, 5)),
         _rows(tr), _rows(tr), _whole((D, D)), _whole((D, D)), _whole((D, D))],
        [jax.ShapeDtypeStruct((t, D_IN), BF)] + [jax.ShapeDtypeStruct((t, D), BF)] * 2 + [jax.ShapeDtypeStruct((t, D), F32)] * 2
        + [_vec(), _vec()],
        [pl.BlockSpec((tr, 2 * D), lambda i: (i, 2))] + [_rows(tr)] * 4 + [_whole((1, D))] * 2, [], ("arbitrary",), rider)


def _loss_head(name, x, gain, target, f, g, scale):
    t = x.shape[0]
    tr = min(t, 256)

    def loss_fn(xv, gn, tgt):
        y = xv * lax.rsqrt(jnp.mean(xv * xv, axis=-1, keepdims=True) + EPS) * gn
        return 0.5 * jnp.sum(jnp.mean(jnp.square(y - tgt), axis=-1))

    def fn(i, r, _):
        loss, vjp = jax.vjp(loss_fn, r[0][...], r[1][...], r[2][...])
        dx, dgain, _ = vjp(jnp.ones((), F32))
        df, dg = _gate_grads(dx, r[3][...], r[4][...], scale)
        return [dx, df, dgain, jnp.zeros((1, D), F32) + loss, dg]

    return _rowcall(name, fn, [x, gain, target, f, g], [_rows(tr), _whole((1, D)), _rows(tr), _rows(tr), _whole((1, D))], 2,
                    [jax.ShapeDtypeStruct((t, D), F32), jax.ShapeDtypeStruct((t, D), BF), _vec(), _vec(), _vec()],
                    [_rows(tr)] * 2 + [_whole((1, D))] * 3, t // tr)


def _adamw(w, g, m, v):
    m = B1 * m + (1.0 - B1) * g
    v = B2 * v + (1.0 - B2) * jnp.square(g)
    m_hat = m / BC1
    v_hat = v / BC2
    delta = -LR * (m_hat / (jnp.sqrt(v_hat) + ADAM_EPS) + WD * w)
    return delta, m, v


ADAMW_ROWS = 64


def _adamw_group(name, items, rider=None, rows=ADAMW_ROWS):
    ins, in_specs, out_shapes, out_specs, plan = [], [], [], [], []
    first = 0
    for chip_sum, received, w, m, v in items:
        r, c = w.shape
        tr = min(r, rows)
        n = r // tr

        def tile(i, first=first, n=n):
            return jnp.clip(i - first, 0, n - 1)

        spec = pl.BlockSpec((tr, c), lambda i, tile=tile: (tile(i), 0))
        ins += [chip_sum, *received, w, m, v]
        in_specs += [pl.BlockSpec((None, tr, c), lambda i, tile=tile: (0, tile(i), 0))]
        in_specs += [pl.BlockSpec((g.shape[0], tr, c), lambda i, tile=tile: (0, tile(i), 0)) for g in received]
        in_specs += [spec] * 3
        out_shapes += [jax.ShapeDtypeStruct((r, c), F32)] * 4
        out_specs += [spec] * 4
        plan.append((first, n, [g.shape[0] for g in received]))
        first += n

    def compute(in_refs, out_refs, _):
        i = pl.program_id(0)
        at_in = at_out = 0
        for start, n, counts in plan:
            mine = in_refs[at_in:at_in + 4 + len(counts)]
            outs = out_refs[at_out:at_out + 4]
            at_in += 4 + len(counts)
            at_out += 4

            @pl.when(jnp.logical_and(i >= start, i < start + n))
            def _(mine=mine, outs=outs, counts=counts):
                g = mine[0][...].astype(F32)
                for j, count in enumerate(counts):
                    for s in range(count):
                        g = g + mine[1 + j][s].astype(F32)
                delta, m_new, v_new = _adamw(mine[-3][...], g, mine[-2][...], mine[-1][...])
                for o, val in zip(outs, (g, delta, m_new, v_new)):
                    o[...] = val

    res = _call(name, compute, (first,), ins, in_specs, out_shapes, out_specs, [], ("arbitrary",), rider)
    outs, rode = res if rider else (res, [])
    return [outs[4 * j:4 * j + 4] for j in range(len(items))], rode


def _adamw_small(name, packed_all, late_all, dws_all, vectors, w_s):
    n_vec = len(vectors)

    def body(*refs):
        p_ref, l_ref, d_ref = refs[:3]
        param_refs = refs[3:3 + 3 * n_vec + 3]
        out_refs = refs[3 + 3 * n_vec + 3:-1]
        g_ref = refs[-1]
        g = p_ref[0]
        late = l_ref[0]
        for s in range(1, NDEV):
            g = g + p_ref[s]
            late = late + l_ref[s]
        g_ref[...] = g
        g_ref[pl.ds(0, R_LATE), :] += late

        def update(gp, wmv, outs):
            delta, m_new, v_new = _adamw(wmv[0][...], gp, wmv[1][...], wmv[2][...])
            for o, val in zip(outs, (gp, delta, m_new, v_new)):
                o[...] = val

        for j, (row, rows, *_) in enumerate(vectors):
            pieces = [g_ref[pl.ds(row + r, 1), :] for r in range(rows)]
            update(pieces[0] if rows == 1 else jnp.concatenate(pieces, axis=1), param_refs[3 * j:3 * j + 3], out_refs[4 * j:4 * j + 4])
        gw = d_ref[0]
        for s in range(1, NDEV):
            gw = gw + d_ref[s]
        update(gw, param_refs[3 * n_vec:], out_refs[4 * n_vec:4 * n_vec + 4])
        out_refs[-2][...] = g_ref[pl.ds(R_CW, KW), :]
        out_refs[-1][...] = g_ref[pl.ds(R_LOSS, 1), :]

    params = [a for _, _, w, m, v in vectors for a in (w, m, v)] + list(w_s)
    out_shapes = [jax.ShapeDtypeStruct(w.shape, F32) for _, _, w, _, _ in vectors for _ in range(4)]
    out_shapes += [jax.ShapeDtypeStruct(w_s[0].shape, F32)] * 4 + [jax.ShapeDtypeStruct((KW, D), F32), _vec()]
    res = pl.pallas_call(body, name=name, out_shape=out_shapes, scratch_shapes=[pltpu.VMEM((R_TOTAL, D), F32)],
                         compiler_params=_params(None))(packed_all, late_all, dws_all, *params)
    return [res[4 * j:4 * j + 4] for j in range(n_vec + 1)], res[-2], res[-1]


def _adamw_plain(name, g, w, m, v):
    def body(g_ref, w_ref, m_ref, v_ref, d_ref, mo_ref, vo_ref):
        delta, m_new, v_new = _adamw(w_ref[...], g_ref[...], m_ref[...], v_ref[...])
        d_ref[...] = delta
        mo_ref[...] = m_new
        vo_ref[...] = v_new

    return pl.pallas_call(body, name=name, out_shape=[jax.ShapeDtypeStruct(w.shape, F32)] * 3,
                          compiler_params=_params(None))(g, w, m, v)


def _adamw_ada(name, c_all_t, dmod, dmod_late, w, m, v):
    r, c = w.shape
    tr = 256

    def fn(i, refs, _):
        ca = _silu(refs[0][...])
        dm = refs[1][...] + refs[2][...]
        g = ca[:, 0:1] * dm[0:1, :]
        for b in range(1, NDEV):
            g = g + ca[:, b:b + 1] * dm[b:b + 1, :]
        delta, m_new, v_new = _adamw(refs[3][...], g, refs[4][...], refs[5][...])
        return [g, delta, m_new, v_new]

    spec = pl.BlockSpec((tr, c), lambda i: (i, 0))
    whole = pl.BlockSpec((NDEV, c), lambda i: (0, 0))
    return _rowcall(name, fn, [c_all_t, dmod, dmod_late, w, m, v],
                    [pl.BlockSpec((tr, NDEV), lambda i: (i, 0)), whole, whole, spec, spec, spec], 4,
                    [jax.ShapeDtypeStruct((r, c), F32)] * 4, [spec] * 4, r // tr)


def _ffn_fwd(tag, x, h, g, wg, wu, wd_shard, down_rider, next_norm=None, more_shards=()):
    t = x.shape[0]
    tm = min(t, 512 if down_rider else 1024)
    (gate, up, act), (wd, *more) = _ffn_up(f"{tag}_up", h, wg, wu, rider=_gather_rider([wd_shard, *more_shards]))
    row = pl.BlockSpec((1, D), lambda i, j, k: (0, 0))

    def epilogue(f, xv, gv, *norm):
        x_out = xv + 0.5 * gv * f
        return (x_out, f, _rms_mod(x_out, *norm)) if norm else (x_out, f)

    res = _mm_nn(f"{tag}_down", act, wd.reshape(F, D), tm, D, 1024, extras=(x, g, *(next_norm or ())),
                 extra_specs=(pl.BlockSpec((tm, D), lambda i, j, k: (i, 0)), row, *([row] * 3 if next_norm else [])),
                 epilogue=epilogue, out_dtypes=(F32, BF, BF) if next_norm else (F32, BF), rider=down_rider)
    (x_out, f, *h_next), rode = res if down_rider else (res, None)
    return x_out, (h_next[0] if next_norm else None), (x, h, gate, up, act, f), wd, rode, more


def _ffn_bwd(tag, dx_out, df, saved, gain, sh, sc, wg, wu, wd, slots, dact_rider=None, dwd_rider=None, dwgu_rider=None,
             below=None, fuse_dh=False):
    x, h, gate, up, act, f = saved
    t = x.shape[0]
    tm = min(t, 1024)
    if fuse_dh:
        dgate, dup, dh = _ffn_dact_dh(f"{tag}_dact_dh", df, wd.reshape(F, D), gate, up, wg, wu)
        dwd = _mm_tn(f"{tag}_dwd", act, df, 512, D).reshape(NDEV, F // NDEV, D)
        (dwg, dwu), (sib_d,) = _dw_gate_up(f"{tag}_dwgu", h, dgate, dup, rider=_pair_rider([dwd]))
        (sum_d,) = _pair_add(f"{tag}_dwd_add", [dwd], [sib_d], slots)
        normed, (sib_g, sib_u) = _norm_mod_bwd(f"{tag}_norm_bwd", x, gain, sc, sh, dh, dx_out, below=below,
                                               rider=_pair_rider([dwg, dwu]))
        sum_g, sum_u = _pair_add(f"{tag}_dwgu_add", [dwg, dwu], [sib_g, sib_u], slots)
        return normed, (sum_d, None), sum_g, sum_u, [], [], []

    blk = pl.BlockSpec((t, F // NDEV), lambda i, j, k: (i, j))
    res = _mm_nt(f"{tag}_dact", df, wd.reshape(F, D), t, F // NDEV, out_dtypes=(BF, BF),
                 extras=(gate, up), extra_specs=(blk, blk), epilogue=_swiglu_bwd, rider=dact_rider)
    (dgate, dup), rode_dact = res if dact_rider else (res, [])
    res = _mm_tn(f"{tag}_dwd", act, df, 512, D, rider=dwd_rider)
    dwd, rode_dwd = res if dwd_rider else (res, [])
    dwd = dwd.reshape(NDEV, F // NDEV, D)
    (dwg, dwu), (sib_d, *rode_dwgu) = _dw_gate_up(f"{tag}_dwgu", h, dgate, dup,
                                                  rider=[_pair_rider([dwd])] + ([dwgu_rider] if dwgu_rider else []))
    (sum_d,) = _pair_add(f"{tag}_dwd_add", [dwd], [sib_d], slots)
    dh, (sib_g, sib_u, got_d) = _mm_nt_blocked(f"{tag}_dh", [dgate, dup], [wg, wu], tm,
                                               rider=[_pair_rider([dwg, dwu]), _chip_rider([sum_d])])
    sum_g, sum_u = _pair_add(f"{tag}_dwgu_add", [dwg, dwu], [sib_g, sib_u], slots)
    normed = _norm_mod_bwd(f"{tag}_norm_bwd", x, gain, sc, sh, dh, dx_out, below=below)
    return normed, (sum_d, [got_d]), sum_g, sum_u, rode_dact, rode_dwd, rode_dwgu


def kernel(x, c, ada_w, ada_b, norm_ffn1, ffn1_w_gate, ffn1_w_up, ffn1_w_down, norm_mix, mix_w_in, mix_b_in, sgu_ln_g, sgu_ln_b, sgu_w_s, sgu_b_s, conv_w, conv_b, conv_ln_g, conv_ln_b, w_branch_a, w_branch_b, w_out, norm_ffn2, ffn2_w_gate, ffn2_w_up, ffn2_w_down, norm_final, loss_target, m_ada_w, m_ada_b, m_norm_ffn1, m_ffn1_w_gate, m_ffn1_w_up, m_ffn1_w_down, m_norm_mix, m_mix_w_in, m_mix_b_in, m_sgu_ln_g, m_sgu_ln_b, m_sgu_w_s, m_sgu_b_s, m_conv_w, m_conv_b, m_conv_ln_g, m_conv_ln_b, m_w_branch_a, m_w_branch_b, m_w_out, m_norm_ffn2, m_ffn2_w_gate, m_ffn2_w_up, m_ffn2_w_down, m_norm_final, v_ada_w, v_ada_b, v_norm_ffn1, v_ffn1_w_gate, v_ffn1_w_up, v_ffn1_w_down, v_norm_mix, v_mix_w_in, v_mix_b_in, v_sgu_ln_g, v_sgu_ln_b, v_sgu_w_s, v_sgu_b_s, v_conv_w, v_conv_b, v_conv_ln_g, v_conv_ln_b, v_w_branch_a, v_w_branch_b, v_w_out, v_norm_ffn2, v_ffn2_w_gate, v_ffn2_w_up, v_ffn2_w_down, v_norm_final):
    mx, my, mc = _position()
    me = 4 * mx + 2 * my + mc
    chip = 2 * mx + my
    slots = jnp.stack([2 * (chip ^ k) + mc for k in range(N_CHIPS)]).astype(jnp.int32)
    t = x.shape[1]
    tm = min(t, 1024)
    x0 = x.reshape(t, D)
    target = loss_target.reshape(t, D)
    given = dict(ffn1_w_gate=(ffn1_w_gate, m_ffn1_w_gate, v_ffn1_w_gate), ffn1_w_up=(ffn1_w_up, m_ffn1_w_up, v_ffn1_w_up),
                 ffn1_w_down=(ffn1_w_down, m_ffn1_w_down, v_ffn1_w_down), mix_w_in=(mix_w_in, m_mix_w_in, v_mix_w_in),
                 w_branch_a=(w_branch_a, m_w_branch_a, v_w_branch_a), w_branch_b=(w_branch_b, m_w_branch_b, v_w_branch_b),
                 w_out=(w_out, m_w_out, v_w_out), ffn2_w_gate=(ffn2_w_gate, m_ffn2_w_gate, v_ffn2_w_gate),
                 ffn2_w_up=(ffn2_w_up, m_ffn2_w_up, v_ffn2_w_up), ffn2_w_down=(ffn2_w_down, m_ffn2_w_down, v_ffn2_w_down))
    shard = {n: wmv[0][0].astype(BF) for n, wmv in given.items()}

    ada_cols = N_MOD * D // NDEV
    c_all, taps_all, mod_all, (wg1, wu1) = _prologue(
        "prologue", jnp.pad(c, ((0, SUBLANES - 1), (0, 0))), jnp.pad(conv_w[0], ((0, HALO - KW), (0, 0))), ada_w[0],
        lax.dynamic_slice(ada_b, (0, me * ada_cols), (1, ada_cols)), [shard["ffn1_w_gate"], shard["ffn1_w_up"]])
    conv_w_full = jnp.transpose(taps_all.reshape(NDEV, HALO, CHUNK), (1, 0, 2)).reshape(HALO, D)
    mod = lax.dynamic_index_in_dim(mod_all.reshape(NDEV, NDEV, ada_cols), me, axis=1, keepdims=False).reshape(N_MOD, 1, D)
    sh1, sc1, g1, sh2, sc2, g2, sh3, sc3, g3 = [mod[i] for i in range(N_MOD)]

    h1 = _norm_mod("ffn1_norm", x0, norm_ffn1, sc1, sh1)
    x1, h2, saved1, wd1, (w_in,), (wa3, wb3) = _ffn_fwd(
        "ffn1", x0, h1, g1, wg1, wu1, shard["ffn1_w_down"], _gather_rider([shard["mix_w_in"]]),
        next_norm=(norm_mix, sc2, sh2), more_shards=(shard["w_branch_a"], shard["w_branch_b"]))
    proj, (wg2,) = _mm_nn_blocked("mix_in", h2, w_in, tm, rider=_gather_rider([shard["ffn2_w_gate"]]))
    bias_full = jnp.repeat(sgu_b_s[0].T, CHUNK, axis=1)
    (ua,) = _sgu("sgu", proj, mix_b_in, sgu_ln_g, sgu_ln_b, sgu_w_s[0], bias_full)
    (z0, z1, z3), (wu2, wo3) = _conv("conv", proj, mix_b_in, conv_w_full, conv_b, conv_ln_g, conv_ln_b,
                                     rider=_gather_rider([shard["ffn2_w_up"], shard["w_out"]]))
    wa, wb, wo = wa3.reshape(D, D), wb3.reshape(D, D), wo3.reshape(D, D)
    ya, yb, merged, y, x2, h3 = _mix_tail("mix_tail", ua, z3, proj, mix_b_in, wa, wb, wo, x1, g2, (norm_ffn2, sc3, sh3))
    x3, _, saved3, wd2, _, _ = _ffn_fwd("ffn2", x2, h3, g3, wg2, wu2, shard["ffn2_w_down"], None)

    norm_final2 = norm_final.reshape(1, D)
    dx3, df3, d_norm_final, loss_row, dg3 = _loss_head("loss_head", x3, norm_final2, target, saved3[-1], g3, 0.5)
    (dx2, dy, d_norm_ffn2, dsc3, dsh3, dg2), (sum_d2, _), sum_g2, sum_u2, _, _, _ = _ffn_bwd(
        "ffn2", dx3, df3, saved3, norm_ffn2, sh3, sc3, wg2, wu2, wd2, slots, below=(y, g2, 1.0), fuse_dh=True)
    (dproj, dya, dyb, dua, dz3, db_ga, db_gb), (got_g2_near,) = _mix_tail_bwd(
        "mix_tail_bwd", dy, proj, mix_b_in, ya, yb, wa, wb, wo, rider=_chip_rider([sum_g2], NEIGHBOURS))
    dwo, dwa, dwb = [g.reshape(NDEV, D // NDEV, D) for g in _dw_square("mix_dw", [(merged, dy), (ua, dya), (z3, dyb)])]
    (dproj, db_u, db_v, d_sgu_g, d_sgu_b, d_ws, d_bs_t), (*sib_abo, got_g2_far) = _sgu_bwd(
        "sgu_bwd", proj, mix_b_in, sgu_ln_g, sgu_ln_b, sgu_w_s[0], bias_full, dua, dproj,
        rider=[_pair_rider([dwa, dwb, dwo]), _chip_rider([sum_g2], DIAGONAL)])
    sum_a, sum_b, sum_o = _pair_add("mix_dw_add", [dwa, dwb, dwo], sib_abo, slots)
    (dproj, db_cv, db_cg, d_cw, d_cb, d_cln_g, d_cln_b), (got_u2, got_d2) = _conv_bwd(
        "conv_bwd", proj, mix_b_in, conv_w_full, conv_ln_g, conv_ln_b, z0, z1, dz3, dproj, rider=_chip_rider([sum_u2, sum_d2]))
    dwin, (got_a, got_b, got_o) = _mm_tn_blocked("mix_dwin", h2, dproj, rider=_chip_rider([sum_a, sum_b, sum_o]))

    d_bs = jnp.transpose(d_bs_t[:, :HEADS])
    zero = jnp.zeros((1, D), F32)
    pack_rows = [zero, zero, zero, zero, zero, dg2, dsh3, dsc3, dg3,
                 zero, zero, d_norm_ffn2, d_norm_final,
                 db_u, db_v, db_cv, db_cg, db_ga, db_gb,
                 d_sgu_g, d_sgu_b, d_bs.reshape(1, D), d_cb, d_cln_g, d_cln_b,
                 d_cw[:KW], loss_row, jnp.zeros((R_TOTAL - R_LOSS - 1, D), F32)]
    packed = jnp.concatenate(pack_rows, axis=0)
    d_ws2 = d_ws.reshape(HEADS * CHUNK, CHUNK)
    dh2, (sib_in, packed_all, dws_all) = _mm_nt_blocked("mix_in_bwd", [dproj], [w_in], tm,
                                                        rider=[_pair_rider([dwin]), _gather_rider([packed, d_ws2])])
    (sum_in,) = _pair_add("mix_dwin_add", [dwin], [sib_in], slots)
    dx1, df1, d_norm_mix, dsc2, dsh2, dg1 = _norm_mod_bwd("mix_norm_bwd", x1, norm_mix, sc2, sh2, dh2, dx2,
                                                          below=(saved1[-1], g1, 0.5))
    (dx0, d_norm_ffn1, dsc1, dsh1), down1, sum_g1, sum_u1, (got_in_near,), _, (got_in_far,) = _ffn_bwd(
        "ffn1", dx1, df1, saved1, norm_ffn1, sh1, sc1, wg1, wu1, wd1, slots,
        dact_rider=_chip_rider([sum_in], NEIGHBOURS), dwgu_rider=_chip_rider([sum_in], DIAGONAL))
    packed_late = jnp.concatenate([dsh1, dsc1, dg1, dsh2, dsc2, jnp.zeros((4, D), F32), d_norm_ffn1, d_norm_mix,
                                   jnp.zeros((R_LATE - 11, D), F32)], axis=0)
    grads = dict(ffn2_w_gate=(sum_g2, [got_g2_near, got_g2_far]), ffn2_w_up=(sum_u2, [got_u2]), ffn2_w_down=(sum_d2, [got_d2]),
                 mix_w_in=(sum_in, [got_in_near, got_in_far]), w_branch_a=(sum_a, [got_a]), w_branch_b=(sum_b, [got_b]),
                 w_out=(sum_o, [got_o]), ffn1_w_down=down1)
    done, (late_all, got_g1, got_u1) = _adamw_group(
        "adamw_most", [(cs, got, *[a[0] for a in given[n]]) for n, (cs, got) in grads.items()],
        rider=[_gather_rider([packed_late]), _chip_rider([sum_g1, sum_u1])])
    last, _ = _adamw_group("adamw_ffn1_in", [(sum_g1, [got_g1], *[a[0] for a in given["ffn1_w_gate"]]),
                                            (sum_u1, [got_u1], *[a[0] for a in given["ffn1_w_up"]])], rows=256)
    big_out = {n: [o.reshape(given[n][0].shape) for o in outs]
               for n, outs in zip([*grads, "ffn1_w_gate", "ffn1_w_up"], [*done, *last])}

    flat = lambda a: a.reshape(1, -1)
    vectors = [("ada_b", 0, 9, ada_b, m_ada_b, v_ada_b), ("norm_ffn1", 9, 1, norm_ffn1, m_norm_ffn1, v_norm_ffn1),
               ("norm_mix", 10, 1, norm_mix, m_norm_mix, v_norm_mix), ("norm_ffn2", 11, 1, norm_ffn2, m_norm_ffn2, v_norm_ffn2),
               ("norm_final", 12, 1, norm_final, m_norm_final, v_norm_final), ("mix_b_in", 13, 6, mix_b_in, m_mix_b_in, v_mix_b_in),
               ("sgu_ln_g", 19, 1, sgu_ln_g, m_sgu_ln_g, v_sgu_ln_g), ("sgu_ln_b", 20, 1, sgu_ln_b, m_sgu_ln_b, v_sgu_ln_b),
               ("sgu_b_s", 21, 1, sgu_b_s, m_sgu_b_s, v_sgu_b_s), ("conv_b", 22, 1, conv_b, m_conv_b, v_conv_b),
               ("conv_ln_g", 23, 1, conv_ln_g, m_conv_ln_g, v_conv_ln_g), ("conv_ln_b", 24, 1, conv_ln_b, m_conv_ln_b, v_conv_ln_b)]
    small_out, d_cw_all, loss_sum = _adamw_small(
        "adamw_small", packed_all, late_all, dws_all, [(row, rows, flat(wv), flat(mv), flat(vv)) for _, row, rows, wv, mv, vv in vectors],
        [a.reshape(HEADS * CHUNK, CHUNK) for a in (sgu_w_s, m_sgu_w_s, v_sgu_w_s)])
    small = {n: [o.reshape(wv.shape) for o in outs] for (n, _, _, wv, _, _), outs in zip(vectors, small_out)}
    small["sgu_w_s"] = [o.reshape(sgu_w_s.shape) for o in small_out[-1]]
    g_cw = lax.dynamic_slice(d_cw_all, (0, me * CHUNK), (KW, CHUNK))
    small["conv_w"] = [o.reshape(conv_w.shape) for o in (g_cw, *_adamw_plain("adamw_conv_w", g_cw, conv_w[0], m_conv_w[0], v_conv_w[0]))]
    loss = loss_sum[0, 0]

    dmod_cols = [lax.dynamic_slice(a[:, :N_MOD, :].reshape(NDEV, N_MOD * D), (0, me * ada_cols), (NDEV, ada_cols))
                 for a in (packed_all, late_all)]
    ada_out = [o.reshape(ada_w.shape) for o in _adamw_ada("adamw_ada_w", jnp.transpose(c_all), *dmod_cols, ada_w[0], m_ada_w[0], v_ada_w[0])]

    order = ["ada_w", "ada_b", "norm_ffn1", "ffn1_w_gate", "ffn1_w_up", "ffn1_w_down", "norm_mix", "mix_w_in", "mix_b_in",
             "sgu_ln_g", "sgu_ln_b", "sgu_w_s", "sgu_b_s", "conv_w", "conv_b", "conv_ln_g", "conv_ln_b", "w_branch_a",
             "w_branch_b", "w_out", "norm_ffn2", "ffn2_w_gate", "ffn2_w_up", "ffn2_w_down", "norm_final"]

    def leaf(n, kind):
        if n == "ada_w":
            return ada_out[kind]
        if n in big_out:
            return big_out[n][kind]
        return small[n][kind]

    return (loss, dx0.reshape(x.shape), *[leaf(n, kind) for kind in range(4) for n in order])
```

```python
import jax
import jax.numpy as jnp
from jax import lax
from jax.experimental import pallas as pl
from jax.experimental.pallas import tpu as pltpu

D = 1024
F = 4 * D
D_IN = 6 * D
HEADS = 8
CHUNK = 128
KW = 31
HALO = 32
N_MOD = 9
NDEV = 8
N_CHIPS = 4
EPS = 1e-6
LR, B1, B2, ADAM_EPS, WD, STEP = 0.001, 0.9, 0.999, 1e-08, 0.01, 10
BC1 = 1.0 - B1 ** STEP
BC2 = 1.0 - B2 ** STEP
VMEM_LIMIT = 56 * 1024 * 1024
MESH = pl.DeviceIdType.MESH
HBM = pl.BlockSpec(memory_space=pltpu.HBM)
VMEM = pl.BlockSpec(memory_space=pltpu.VMEM)
BF = jnp.bfloat16
F32 = jnp.float32

NN = (((1,), (0,)), ((), ()))
NT = (((1,), (1,)), ((), ()))
TN = (((0,), (0,)), ((), ()))

R_CW, R_LOSS, R_TOTAL = 25, 56, 64
R_LATE = 16


def _params(sem):
    return pltpu.CompilerParams(dimension_semantics=sem, vmem_limit_bytes=VMEM_LIMIT)


def _position():
    return lax.axis_index("x"), lax.axis_index("y"), lax.axis_index("c")


def _flip(pos, k):
    x, y, c = pos
    return (x ^ (k >> 2 & 1), y ^ (k >> 1 & 1), c ^ (k & 1))


def _index(pos):
    return 4 * pos[0] + 2 * pos[1] + pos[2]


def _gather_rows(x_ref, out_ref, send_sems, recv_sems, local_sem):
    m_per = x_ref.shape[0]
    x, y, c = _position()
    me, sibling = (x, y, c), (x, y, 1 - c)
    chips = [(1 - x, y), (x, 1 - y), (1 - x, 1 - y)]

    def rows(pos):
        return out_ref.at[pl.ds(_index(pos) * m_per, m_per), :]

    def copy(k, block, to, src=None):
        return pltpu.make_async_remote_copy(
            src_ref=rows(block) if src is None else src, dst_ref=rows(block),
            send_sem=send_sems.at[k], recv_sem=recv_sems.at[k], device_id=to, device_id_type=MESH)

    mine = pltpu.make_async_copy(x_ref, rows(me), local_sem)
    mine.start()
    first = [copy(0, me, sibling, src=x_ref)]
    first += [copy(1 + j, me, (*chip, c), src=x_ref) for j, chip in enumerate(chips)]
    for cp in first:
        cp.start()
    passed = [copy(4 + j, (*chip, c), sibling) for j, chip in enumerate(chips)]
    for j, chip in enumerate(chips):
        copy(1 + j, (*chip, c), me).wait_recv()
        passed[j].start()
    copy(0, sibling, me).wait_recv()
    for j, chip in enumerate(chips):
        copy(4 + j, (*chip, 1 - c), me).wait_recv()
    for cp in first + passed:
        cp.wait_send()
    mine.wait()


def _prologue(name, c_rows, taps, ada_w, ada_b, shards):
    rider = _gather_rider(shards)
    n = len(shards)
    nc = ada_w.shape[1]

    def body(*refs):
        c_ref, taps_ref, w_ref, b_ref = refs[:4]
        shard_refs = refs[4:4 + n]
        c_all_ref, taps_all_ref, mod_all_ref = refs[4 + n:7 + n]
        gathered_refs = refs[7 + n:7 + 2 * n]
        c_buf, mod_part, sems = refs[7 + 2 * n], refs[8 + 2 * n], refs[9 + 2 * n:]
        rider.start(shard_refs, gathered_refs, sems[9:])
        _gather_rows(c_ref, c_buf, *sems[0:3])
        c_all = jnp.concatenate([c_buf[pl.ds(d * SUBLANES, 1), :] for d in range(NDEV)], axis=0)
        c_all_ref[...] = c_all
        mod_part[...] = jnp.dot(_silu(c_all), w_ref[...], preferred_element_type=F32) + b_ref[...]
        _gather_rows(taps_ref, taps_all_ref, *sems[3:6])
        _gather_rows(mod_part, mod_all_ref, *sems[6:9])
        rider.mid(shard_refs, gathered_refs, sems[9:])
        rider.relay(shard_refs, gathered_refs, sems[9:])
        rider.finish(shard_refs, gathered_refs, sems[9:])

    small_sems = [pltpu.SemaphoreType.DMA((7,)), pltpu.SemaphoreType.DMA((7,)), pltpu.SemaphoreType.DMA] * 3
    res = pl.pallas_call(
        body, name=name,
        out_shape=[jax.ShapeDtypeStruct((NDEV, D), F32), jax.ShapeDtypeStruct((NDEV * taps.shape[0], taps.shape[1]), F32),
                   jax.ShapeDtypeStruct((NDEV * NDEV, nc), F32)] + rider.out_shapes,
        in_specs=[VMEM] * 4 + [HBM] * n, out_specs=[VMEM] * 3 + [HBM] * n,
        scratch_shapes=[pltpu.VMEM((NDEV * SUBLANES, D), F32), pltpu.VMEM((NDEV, nc), F32)] + small_sems + rider.sems,
        compiler_params=_params(None),
    )(c_rows, taps, ada_w, ada_b, *shards)
    return res[0], res[1], res[2], res[3:]


class _Rider:
    def __init__(self, ins, out_shapes, sems, start, finish, mid=None, relay=None):
        self.ins, self.out_shapes, self.sems = list(ins), list(out_shapes), list(sems)
        self.start, self.finish, self.mid, self.relay = start, finish, mid, relay


def _gather_rider(shards):
    n = len(shards)

    def setup(ins, outs, sems):
        send_sems, recv_sems, local_sems = sems
        x, y, c = _position()
        places = dict(me=(x, y, c), sibling=(x, y, 1 - c), xn=(1 - x, y, c), yn=(x, 1 - y, c), diagonal=(1 - x, 1 - y, c),
                      passed_on=(x ^ c, y ^ (1 - c), c), passed_to=(x ^ (1 - c), y ^ c, c))

        def copy(a, k, block, to, own=False):
            slot = outs[a].at[_index(block)]
            return pltpu.make_async_remote_copy(
                src_ref=ins[a] if own else slot, dst_ref=slot,
                send_sem=send_sems.at[k, a], recv_sem=recv_sems.at[k, a], device_id=to, device_id_type=MESH)

        def local(a):
            return pltpu.make_async_copy(ins[a], outs[a].at[_index(places["me"])], local_sems.at[a])

        return places, copy, local

    def start(ins, outs, sems):
        p, copy, local = setup(ins, outs, sems)
        for a in range(n):
            local(a).start()
            for k, to in enumerate(("sibling", "xn", "yn")):
                copy(a, k, p["me"], p[to], own=True).start()

    def mid(ins, outs, sems):
        p, copy, _ = setup(ins, outs, sems)
        for a in range(n):
            copy(a, 1, p["xn"], p["me"]).wait_recv()
            copy(a, 2, p["yn"], p["me"]).wait_recv()
            copy(a, 3, p["passed_on"], p["passed_to"]).start()
            copy(a, 4, p["xn"], p["sibling"]).start()
            copy(a, 5, p["yn"], p["sibling"]).start()

    def relay(ins, outs, sems):
        p, copy, _ = setup(ins, outs, sems)
        for a in range(n):
            copy(a, 3, p["diagonal"], p["me"]).wait_recv()
            copy(a, 6, p["diagonal"], p["sibling"]).start()

    def finish(ins, outs, sems):
        p, copy, local = setup(ins, outs, sems)
        x, y, c = p["me"]
        for a in range(n):
            for k, block in ((0, (x, y, 1 - c)), (4, (1 - x, y, 1 - c)), (5, (x, 1 - y, 1 - c)), (6, (1 - x, 1 - y, 1 - c))):
                copy(a, k, block, p["me"]).wait_recv()
            for k, to in enumerate(("sibling", "xn", "yn")):
                copy(a, k, p["me"], p[to], own=True).wait_send()
            copy(a, 3, p["passed_on"], p["passed_to"]).wait_send()
            for k, block in ((4, "xn"), (5, "yn"), (6, "diagonal")):
                copy(a, k, p[block], p["sibling"]).wait_send()
            local(a).wait()

    return _Rider(shards, [jax.ShapeDtypeStruct((NDEV, *s.shape), s.dtype) for s in shards],
                  [pltpu.SemaphoreType.DMA((7, n)), pltpu.SemaphoreType.DMA((7, n)), pltpu.SemaphoreType.DMA((n,))],
                  start, finish, mid, relay)


def _pair_rider(parts):
    n = len(parts)

    def copies(ins, outs, sems):
        send_sems, recv_sems = sems
        x, y, c = _position()
        q = 2 * x + y
        return [pltpu.make_async_remote_copy(
            src_ref=ins[a].at[2 * (q ^ k) + (1 - c)], dst_ref=outs[a].at[k],
            send_sem=send_sems.at[k, a], recv_sem=recv_sems.at[k, a], device_id=(x, y, 1 - c), device_id_type=MESH)
            for a in range(n) for k in range(N_CHIPS)]

    def start(ins, outs, sems):
        for cp in copies(ins, outs, sems):
            cp.start()

    def finish(ins, outs, sems):
        for cp in copies(ins, outs, sems):
            cp.wait()

    return _Rider(parts, [jax.ShapeDtypeStruct((N_CHIPS, *p.shape[1:]), p.dtype) for p in parts],
                  [pltpu.SemaphoreType.DMA((N_CHIPS, n)), pltpu.SemaphoreType.DMA((N_CHIPS, n))], start, finish)


NEIGHBOURS = (1, 2)
DIAGONAL = (3,)
OTHER_CHIPS = NEIGHBOURS + DIAGONAL


def _chip_rider(sums, ks=OTHER_CHIPS):
    n = len(sums)

    def copies(ins, outs, sems):
        send_sems, recv_sems = sems
        me = _position()
        return [pltpu.make_async_remote_copy(
            src_ref=ins[a].at[k], dst_ref=outs[a].at[j],
            send_sem=send_sems.at[j, a], recv_sem=recv_sems.at[j, a], device_id=_flip(me, 2 * k), device_id_type=MESH)
            for a in range(n) for j, k in enumerate(ks)]

    def start(ins, outs, sems):
        for cp in copies(ins, outs, sems):
            cp.start()

    def finish(ins, outs, sems):
        for cp in copies(ins, outs, sems):
            cp.wait()

    return _Rider(sums, [jax.ShapeDtypeStruct((len(ks), *s.shape[1:]), s.dtype) for s in sums],
                  [pltpu.SemaphoreType.DMA((len(ks), n)), pltpu.SemaphoreType.DMA((len(ks), n))], start, finish)


def _grid_edge(grid, last):
    cond = None
    for d, n in enumerate(grid):
        here = pl.program_id(d) == (n - 1 if last else 0)
        cond = here if cond is None else jnp.logical_and(cond, here)
    return cond


def _call(name, compute, grid, ins, in_specs, out_shapes, out_specs, scratch_shapes, semantics, rider=None, aliases=None):
    riders = [rider] if isinstance(rider, _Rider) else list(rider or [])
    n_in, n_out, n_scr = len(ins), len(out_shapes), len(scratch_shapes)
    n_rin, n_rout, n_rsem = [sum(len(part(r)) for r in riders) for part in (lambda r: r.ins, lambda r: r.out_shapes, lambda r: r.sems)]
    cuts = [0, n_in, n_in + n_rin, n_in + n_rin + n_out, n_in + n_rin + n_out + n_rout, n_in + n_rin + n_out + n_rout + n_scr]

    def body(*refs):
        in_refs, rin_refs, out_refs, rout_refs, scr_refs = [refs[a:b] for a, b in zip(cuts[:-1], cuts[1:])]
        rsem_refs = refs[cuts[-1]:]
        mine, at = [], [0, 0, 0]
        for r in riders:
            mine.append((r, rin_refs[at[0]:at[0] + len(r.ins)], rout_refs[at[1]:at[1] + len(r.out_shapes)],
                         rsem_refs[at[2]:at[2] + len(r.sems)]))
            at = [at[0] + len(r.ins), at[1] + len(r.out_shapes), at[2] + len(r.sems)]
        if riders:
            @pl.when(_grid_edge(grid, last=False))
            def _():
                for r, a, b, c in mine:
                    r.start(a, b, c)

        if any(r.mid for r in riders):
            step, steps = 0, 1
            for d, size in enumerate(grid):
                step, steps = step * size + pl.program_id(d), steps * size

            @pl.when(step == steps * 5 // 8)
            def _():
                for r, a, b, c in mine:
                    if r.mid:
                        r.mid(a, b, c)

        if any(r.relay for r in riders):
            @pl.when(_grid_edge(grid, last=True))
            def _():
                for r, a, b, c in mine:
                    if r.relay:
                        r.relay(a, b, c)

        compute(in_refs, out_refs, scr_refs)
        if riders:
            @pl.when(_grid_edge(grid, last=True))
            def _():
                for r, a, b, c in mine:
                    r.finish(a, b, c)

    res = pl.pallas_call(
        body, name=name, grid=grid,
        out_shape=list(out_shapes) + [s for r in riders for s in r.out_shapes],
        in_specs=list(in_specs) + [HBM] * n_rin, out_specs=list(out_specs) + [HBM] * n_rout,
        scratch_shapes=list(scratch_shapes) + [s for r in riders for s in r.sems],
        input_output_aliases=aliases or {}, compiler_params=_params(semantics),
    )(*ins, *[a for r in riders for a in r.ins])
    return (res[:n_out], res[n_out:]) if riders else res


def _pair_add(name, parts, from_sibling, slots):
    n = len(parts)

    def body(s_ref, *refs):
        for a in range(n):
            refs[2 * n + a][...] = (refs[a][...].astype(F32) + refs[n + a][...].astype(F32)).astype(refs[2 * n + a].dtype)

    def slab(p, picked):
        _, r, c = p.shape
        return pl.BlockSpec((None, r, c), (lambda k, s: (s[k], 0, 0)) if picked else (lambda k, s: (k, 0, 0)))

    return pl.pallas_call(
        body, name=name,
        grid_spec=pltpu.PrefetchScalarGridSpec(
            num_scalar_prefetch=1, grid=(N_CHIPS,),
            in_specs=[slab(p, True) for p in parts] + [slab(p, False) for p in parts],
            out_specs=[slab(p, False) for p in parts]),
        out_shape=[jax.ShapeDtypeStruct((N_CHIPS, *p.shape[1:]), p.dtype) for p in parts],
        compiler_params=_params(("arbitrary",)),
    )(slots, *parts, *from_sibling)


def _mm(name, pairs, dims, grid, nk, out_shapes, out_specs, extras=(), extra_specs=(), epilogue=None, acc_shape=None, rider=None):
    n_pairs = len(pairs)

    def compute(ins, outs, scratch):
        def partial_sum():
            total = None
            for p in range(n_pairs):
                d = lax.dot_general(ins[2 * p][...], ins[2 * p + 1][...], dims, preferred_element_type=F32)
                total = d if total is None else total + d
            return total

        def finish(r):
            ex = [e[...] for e in ins[2 * n_pairs:]]
            res = epilogue(r, *ex) if epilogue is not None else (r,)
            for o, v in zip(outs, res):
                o[...] = v.astype(o.dtype)

        if nk == 1:
            finish(partial_sum())
        else:
            acc = scratch[0]
            k = pl.program_id(2)

            @pl.when(k == 0)
            def _():
                acc[...] = partial_sum()

            @pl.when(k > 0)
            def _():
                acc[...] += partial_sum()

            @pl.when(k == nk - 1)
            def _():
                finish(acc[...])

    operands, specs = [], []
    for a, a_spec, b, b_spec in pairs:
        operands += [a, b]
        specs += [a_spec, b_spec]
    return _call(name, compute, grid, operands + list(extras), specs + list(extra_specs), out_shapes, out_specs,
                 [pltpu.VMEM(acc_shape, F32)] if nk > 1 else [], ("parallel", "parallel", "arbitrary"), rider)


def _single(res, rider):
    return (res[0][0], res[1]) if rider else res[0]


def _silu(x):
    return x * jax.nn.sigmoid(x)


def _ffn_up(name, h, wg, wu, rider=None, norm=None):
    t = h.shape[0]
    tm = min(t, 1024)
    nb = F // NDEV

    def compute(ins, outs, scr):
        if norm:
            @pl.when(pl.program_id(1) == 0)
            def _():
                scr[0][...] = _rms_mod(ins[0][...], ins[3][...], ins[4][...], ins[5][...]).astype(BF)
                outs[3][...] = scr[0][...]

            hv = scr[0][...]
        else:
            hv = ins[0][...]
        g = jnp.dot(hv, ins[1][...], preferred_element_type=F32)
        u = jnp.dot(hv, ins[2][...], preferred_element_type=F32)
        outs[0][...] = g.astype(BF)
        outs[1][...] = u.astype(BF)
        outs[2][...] = (_silu(g) * u).astype(BF)

    w_spec = pl.BlockSpec((None, D, nb), lambda i, j: (j, 0, 0))
    o_spec = pl.BlockSpec((tm, nb), lambda i, j: (i, j))
    rows = pl.BlockSpec((tm, D), lambda i, j: (i, 0))
    vec = pl.BlockSpec((1, D), lambda i, j: (0, 0))
    return _call(name, compute, (t // tm, NDEV), [h, wg, wu, *(norm or ())], [rows, w_spec, w_spec] + [vec] * (3 if norm else 0),
                 [jax.ShapeDtypeStruct((t, F), BF)] * 3 + ([jax.ShapeDtypeStruct((t, D), BF)] if norm else []),
                 [o_spec] * 3 + ([rows] if norm else []), [pltpu.VMEM((tm, D), BF)] if norm else [],
                 ("parallel", "arbitrary"), rider)


def _mm_nn(name, a, b, tm, tn, tk, extras=(), extra_specs=(), epilogue=None, out_dtypes=(F32,), rider=None):
    m, kk = a.shape
    n = b.shape[1]
    nk = kk // tk
    return _mm(
        name, [(a, pl.BlockSpec((tm, tk), lambda i, j, k: (i, k)), b, pl.BlockSpec((tk, tn), lambda i, j, k: (k, j)))], NN,
        (m // tm, n // tn, nk), nk,
        [jax.ShapeDtypeStruct((m, n), dt) for dt in out_dtypes],
        [pl.BlockSpec((tm, tn), lambda i, j, k: (i, j))] * len(out_dtypes),
        extras, extra_specs, epilogue, (tm, tn), rider)


def _mm_nn_blocked(name, a, b3, tm, rider=None):
    m = a.shape[0]
    nb = b3.shape[2]
    return _single(_mm(
        name, [(a, pl.BlockSpec((tm, D), lambda i, j, k: (i, 0)), b3, pl.BlockSpec((None, D, nb), lambda i, j, k: (j, 0, 0)))], NN,
        (m // tm, NDEV, 1), 1,
        [jax.ShapeDtypeStruct((m, NDEV * nb), F32)], [pl.BlockSpec((tm, nb), lambda i, j, k: (i, j))], rider=rider), rider)


def _mm_nt(name, a, b, tm, tn, out_dtypes=(F32,), extras=(), extra_specs=(), epilogue=None, rider=None):
    m, kk = a.shape
    n = b.shape[0]
    return _mm(
        name, [(a, pl.BlockSpec((tm, kk), lambda i, j, k: (i, 0)), b, pl.BlockSpec((tn, kk), lambda i, j, k: (j, 0)))], NT,
        (m // tm, n // tn, 1), 1,
        [jax.ShapeDtypeStruct((m, n), dt) for dt in out_dtypes],
        [pl.BlockSpec((tm, tn), lambda i, j, k: (i, j))] * len(out_dtypes),
        extras, extra_specs, epilogue, rider=rider)


def _mm_nt_blocked(name, a_list, b3_list, tm, rider=None):
    m = a_list[0].shape[0]
    nb = b3_list[0].shape[2]
    pairs = [(a, pl.BlockSpec((tm, nb), lambda i, j, k: (i, k)), b3, pl.BlockSpec((None, D, nb), lambda i, j, k: (k, 0, 0)))
             for a, b3 in zip(a_list, b3_list)]
    return _single(_mm(name, pairs, NT, (m // tm, 1, NDEV), NDEV,
                       [jax.ShapeDtypeStruct((m, D), F32)], [pl.BlockSpec((tm, D), lambda i, j, k: (i, 0))],
                       acc_shape=(tm, D), rider=rider), rider)


def _mm_tn(name, a, b, tm, tn, rider=None):
    t, m = a.shape
    n = b.shape[1]
    return _single(_mm(
        name, [(a, pl.BlockSpec((t, tm), lambda i, j, k: (0, i)), b, pl.BlockSpec((t, tn), lambda i, j, k: (0, j)))], TN,
        (m // tm, n // tn, 1), 1,
        [jax.ShapeDtypeStruct((m, n), BF)], [pl.BlockSpec((tm, tn), lambda i, j, k: (i, j))], rider=rider), rider)


def _mm_tn_blocked(name, a, b, rider=None):
    t = a.shape[0]
    nb = b.shape[1] // NDEV
    return _single(_mm(
        name, [(a, pl.BlockSpec((t, D), lambda i, j, k: (0, 0)), b, pl.BlockSpec((t, nb), lambda i, j, k: (0, j)))], TN,
        (1, NDEV, 1), 1,
        [jax.ShapeDtypeStruct((NDEV, D, nb), BF)], [pl.BlockSpec((None, D, nb), lambda i, j, k: (j, 0, 0))], rider=rider), rider)


def _dw_gate_up(name, h, dgate, dup, rider=None):
    t = h.shape[0]
    nb = F // NDEV

    def compute(ins, outs, _):
        hv = ins[0][...]
        outs[0][...] = lax.dot_general(hv, ins[1][...], TN, preferred_element_type=F32).astype(BF)
        outs[1][...] = lax.dot_general(hv, ins[2][...], TN, preferred_element_type=F32).astype(BF)

    d_spec = pl.BlockSpec((t, nb), lambda j: (0, j))
    o_spec = pl.BlockSpec((None, D, nb), lambda j: (j, 0, 0))
    return _call(name, compute, (NDEV,), [h, dgate, dup], [pl.BlockSpec((t, D), lambda j: (0, 0)), d_spec, d_spec],
                 [jax.ShapeDtypeStruct((NDEV, D, nb), BF)] * 2, [o_spec] * 2, [], ("arbitrary",), rider)


def _dw_square(name, pairs):
    t = pairs[0][0].shape[0]
    tm = 512
    n = len(pairs)

    def compute(ins, outs, _):
        for p in range(n):
            outs[p][...] = lax.dot_general(ins[2 * p][...], ins[2 * p + 1][...], TN, preferred_element_type=F32).astype(BF)

    return _call(name, compute, (D // tm,), [x for pair in pairs for x in pair],
                 [pl.BlockSpec((t, tm), lambda i: (0, i)), pl.BlockSpec((t, D), lambda i: (0, 0))] * n,
                 [jax.ShapeDtypeStruct((D, D), BF)] * n, [pl.BlockSpec((tm, D), lambda i: (i, 0))] * n, [], ("arbitrary",))


def _swiglu_bwd(da, gate, up):
    gate = gate.astype(F32)
    s = jax.nn.sigmoid(gate)
    return da * up.astype(F32) * (s * (1.0 + gate * (1.0 - s))), da * (gate * s)


def _ffn_dact_dh(name, df, wd, gate, up, wg, wu, rider=None):
    t = df.shape[0]
    tm = min(t, 1024)
    nb = F // NDEV

    def compute(ins, outs, scr):
        acc = scr[0]
        j = pl.program_id(1)
        da = lax.dot_general(ins[0][...], ins[1][...], NT, preferred_element_type=F32)
        dgate, dup = _swiglu_bwd(da, ins[2][...], ins[3][...])
        dgate, dup = dgate.astype(BF), dup.astype(BF)
        outs[0][...] = dgate
        outs[1][...] = dup
        part = (lax.dot_general(dgate, ins[4][...], NT, preferred_element_type=F32)
                + lax.dot_general(dup, ins[5][...], NT, preferred_element_type=F32))

        @pl.when(j == 0)
        def _():
            acc[...] = part

        @pl.when(j > 0)
        def _():
            acc[...] += part

        @pl.when(j == NDEV - 1)
        def _():
            outs[2][...] = acc[...]

    blk = pl.BlockSpec((tm, nb), lambda i, j: (i, j))
    w3 = pl.BlockSpec((None, D, nb), lambda i, j: (j, 0, 0))
    row = pl.BlockSpec((tm, D), lambda i, j: (i, 0))
    return _call(name, compute, (t // tm, NDEV), [df, wd, gate, up, wg, wu],
                 [row, pl.BlockSpec((nb, D), lambda i, j: (j, 0)), blk, blk, w3, w3],
                 [jax.ShapeDtypeStruct((t, F), BF)] * 2 + [jax.ShapeDtypeStruct((t, D), F32)], [blk, blk, row],
                 [pltpu.VMEM((tm, D), F32)], ("parallel", "arbitrary"), rider)


def _rowcall(name, fn, ins, in_specs, n_row_out, out_shapes, out_specs, grid, scratch_shapes=(), rider=None, aliases=None):
    def accumulate(o, v, i):
        @pl.when(i == 0)
        def _():
            o[...] = v.astype(o.dtype)

        @pl.when(i > 0)
        def _():
            o[...] += v.astype(o.dtype)

    def compute(in_refs, out_refs, scr):
        i = pl.program_id(0)
        vals = fn(i, in_refs, scr)
        for idx, (o, v) in enumerate(zip(out_refs, vals)):
            if idx < n_row_out:
                o[...] = v.astype(o.dtype)
            else:
                accumulate(o, v, i)

    return _call(name, compute, (grid,), ins, in_specs, out_shapes, out_specs, list(scratch_shapes), ("arbitrary",), rider, aliases)


def _rows(tr, w=D, cb=0):
    return pl.BlockSpec((tr, w), lambda i: (i, cb))


def _whole(shape):
    nd = len(shape)
    return pl.BlockSpec(shape, lambda i: (0,) * nd)


def _vec(n=1):
    return jax.ShapeDtypeStruct((n, D), F32)


def _rms_mod(x, gain, sc, sh):
    y = x * lax.rsqrt(jnp.mean(x * x, axis=-1, keepdims=True) + EPS)
    return (y * gain) * (1.0 + sc) + sh


def _layer_norm(x, g, b):
    mu = jnp.mean(x, axis=-1, keepdims=True)
    var = jnp.mean(jnp.square(x - mu), axis=-1, keepdims=True)
    return (x - mu) * lax.rsqrt(var + EPS) * g + b


def _gate_grads(dx, f, g, scale):
    return scale * g * dx, jnp.sum(scale * dx * f.astype(F32), axis=0, keepdims=True)


def _norm_mod_bwd(name, x, gain, sc, sh, dh, dres, below=None, rider=None):
    t = x.shape[0]
    tr = min(t, 256)

    def fn(i, r, _):
        _, vjp = jax.vjp(_rms_mod, r[0][...], r[1][...], r[2][...], r[3][...])
        dx, dgain, dsc, dsh = vjp(r[4][...])
        dx = dx + r[5][...]
        if below is None:
            return [dx, dgain, dsc, dsh]
        df, dg = _gate_grads(dx, r[6][...], r[7][...], below[2])
        return [dx, df, dgain, dsc, dsh, dg]

    ins, specs = [x, gain, sc, sh, dh, dres], [_rows(tr)] + [_whole((1, D))] * 3 + [_rows(tr)] * 2
    outs, out_specs = [jax.ShapeDtypeStruct((t, D), F32)], [_rows(tr)]
    if below is not None:
        ins, specs = ins + [below[0], below[1]], specs + [_rows(tr), _whole((1, D))]
        outs, out_specs = outs + [jax.ShapeDtypeStruct((t, D), BF)], out_specs + [_rows(tr)]
    n_vec = 3 if below is None else 4
    return _rowcall(name, fn, ins, specs, len(outs), outs + [_vec()] * n_vec, out_specs + [_whole((1, D))] * n_vec, t // tr,
                    rider=rider)


def _sgu_pre(up, vp, bu, bv, ln_g, ln_b):
    return jax.nn.gelu(up + bu), _layer_norm(jax.nn.gelu(vp + bv), ln_g, ln_b)


def _causal(w_ref, h):
    rows = lax.broadcasted_iota(jnp.int32, (CHUNK, CHUNK), 0)
    cols = lax.broadcasted_iota(jnp.int32, (CHUNK, CHUNK), 1)
    return jnp.where(cols <= rows, w_ref[h], 0.0)


def _sgu(name, proj, b_in, ln_g, ln_b, w_s, bias_full, rider=None):
    t = proj.shape[0]

    def fn(i, r, _):
        u, v = _sgu_pre(r[0][...], r[1][...], r[2][...], r[3][...], r[4][...], r[5][...])
        vb = v.astype(BF)
        mixed = [jnp.dot(_causal(r[6], h).astype(BF), vb[:, h * CHUNK:(h + 1) * CHUNK], preferred_element_type=F32)
                 for h in range(HEADS)]
        return [u * (jnp.concatenate(mixed, axis=1) + r[7][...])]

    return _rowcall(
        name, fn, [proj, proj, b_in, b_in, ln_g, ln_b, w_s, bias_full],
        [_rows(CHUNK, D, 0), _rows(CHUNK, D, 1), pl.BlockSpec((1, D), lambda i: (0, 0)), pl.BlockSpec((1, D), lambda i: (0, 1)),
         _whole((1, D)), _whole((1, D)), _whole((HEADS, CHUNK, CHUNK)), _whole((CHUNK, D))],
        1, [jax.ShapeDtypeStruct((t, D), BF)], [_rows(CHUNK)], t // CHUNK, rider=rider)


def _sgu_bwd(name, proj, b_in, ln_g, ln_b, w_s, bias_full, dout, dproj, rider=None):
    t = proj.shape[0]

    def fn(i, r, _):
        (u, v), vjp = jax.vjp(_sgu_pre, r[0][...], r[1][...], r[2][...], r[3][...], r[4][...], r[5][...])
        vb = v.astype(BF)
        d = r[8][...]
        masks = [_causal(r[6], h).astype(BF) for h in range(HEADS)]
        cols = [slice(h * CHUNK, (h + 1) * CHUNK) for h in range(HEADS)]
        mixed = jnp.concatenate([jnp.dot(masks[h], vb[:, cols[h]], preferred_element_type=F32) for h in range(HEADS)], axis=1)
        du = d * (mixed + r[7][...])
        dmix = d * u
        dmb = dmix.astype(BF)
        dv = jnp.concatenate([lax.dot_general(masks[h], dmb[:, cols[h]], TN, preferred_element_type=F32) for h in range(HEADS)], axis=1)
        rows = lax.broadcasted_iota(jnp.int32, (CHUNK, CHUNK), 0)
        lanes = lax.broadcasted_iota(jnp.int32, (CHUNK, CHUNK), 1)
        dws = jnp.stack([jnp.where(lanes <= rows, lax.dot_general(dmb[:, cols[h]], vb[:, cols[h]], NT, preferred_element_type=F32), 0.0)
                         for h in range(HEADS)])
        dbs = jnp.zeros((CHUNK, CHUNK), F32)
        for h in range(HEADS):
            dbs = dbs + jnp.where(lanes == h, jnp.sum(dmix[:, cols[h]], axis=1, keepdims=True), 0.0)
        dup, dvp, dbu, dbv, dg, db = vjp((du, dv))
        return [jnp.concatenate([dup, dvp], axis=1), dbu, dbv, dg, db, dws, dbs]

    return _rowcall(
        name, fn, [proj, proj, b_in, b_in, ln_g, ln_b, w_s, bias_full, dout, dproj],
        [_rows(CHUNK, D, 0), _rows(CHUNK, D, 1), pl.BlockSpec((1, D), lambda i: (0, 0)), pl.BlockSpec((1, D), lambda i: (0, 1)),
         _whole((1, D)), _whole((1, D)), _whole((HEADS, CHUNK, CHUNK)), _whole((CHUNK, D)), _rows(CHUNK),
         pl.BlockSpec(memory_space=pl.ANY)],
        1, [jax.ShapeDtypeStruct(dproj.shape, dproj.dtype)] + [_vec()] * 4
        + [jax.ShapeDtypeStruct((HEADS, CHUNK, CHUNK), F32), jax.ShapeDtypeStruct((CHUNK, CHUNK), F32)],
        [pl.BlockSpec((CHUNK, 2 * D), lambda i: (i, 0))] + [_whole((1, D))] * 4 + [_whole((HEADS, CHUNK, CHUNK)), _whole((CHUNK, CHUNK))],
        t // CHUNK, rider=rider, aliases={9: 0})


def _halo_before(tr, cb):
    return pl.BlockSpec((HALO, D), lambda i: (jnp.maximum(i * (tr // HALO) - 1, 0), cb))


def _halo_after(tr, cb, n_tiles):
    return pl.BlockSpec((HALO, D), lambda i: (jnp.minimum((i + 1) * (tr // HALO), n_tiles * (tr // HALO) - 1), cb))


def _ln_silu(z, g, b):
    return _silu(_layer_norm(z, g, b))


SUBLANES = 8
LANES = 128
CONV_STRIP = 16
DW_STRIP = 32


def _shifted_copies(buf, copies, rows):
    for b in range(1, SUBLANES):
        copies[b - 1, pl.ds(0, rows), :] = buf[pl.ds(b, rows), :]


def _shifted(buf, copies, offset, start, rows, lanes=slice(None)):
    at = pl.ds(pl.multiple_of(start + SUBLANES * (offset // SUBLANES), SUBLANES), rows)
    return buf[at, lanes] if offset % SUBLANES == 0 else copies[offset % SUBLANES - 1, at, lanes]


def _accumulate(o, v, i):
    @pl.when(i == 0)
    def _():
        o[...] = v.astype(o.dtype)

    @pl.when(i > 0)
    def _():
        o[...] += v.astype(o.dtype)


def _conv(name, proj, b_in, conv_w, conv_b, ln_g, ln_b, rider=None):
    t = proj.shape[0]
    tr = min(t, 256)

    def compute(r, outs, scr):
        zbuf, zs = scr
        i = pl.program_id(0)
        bv, bg = r[4][...], r[5][...]
        z0 = (r[0][...] + bv) * jax.nn.sigmoid(r[1][...] + bg)
        before = (r[2][...] + bv) * jax.nn.sigmoid(r[3][...] + bg)
        zbuf[pl.ds(0, HALO), :] = jnp.where(i > 0, before, 0.0)
        zbuf[pl.ds(HALO, tr), :] = z0
        outs[0][...] = z0
        _shifted_copies(zbuf, zs, tr + HALO - SUBLANES)

        def strip(s, carry):
            r0 = s * CONV_STRIP
            acc = jnp.zeros((CONV_STRIP, D), F32) + r[7][...]
            for k in range(KW):
                acc = acc + r[6][k:k + 1, :] * _shifted(zbuf, zs, HALO - (KW - 1) + k, r0, CONV_STRIP)
            outs[1][pl.ds(pl.multiple_of(r0, SUBLANES), CONV_STRIP), :] = acc
            return carry

        lax.fori_loop(0, tr // CONV_STRIP, strip, 0)
        outs[2][...] = _ln_silu(outs[1][...], r[8][...], r[9][...]).astype(BF)

    return _call(
        name, compute, (t // tr,), [proj, proj, proj, proj, b_in, b_in, conv_w, conv_b, ln_g, ln_b],
        [_rows(tr, D, 2), _rows(tr, D, 3), _halo_before(tr, 2), _halo_before(tr, 3),
         pl.BlockSpec((1, D), lambda i: (0, 2)), pl.BlockSpec((1, D), lambda i: (0, 3)),
         _whole((HALO, D)), _whole((1, D)), _whole((1, D)), _whole((1, D))],
        [jax.ShapeDtypeStruct((t, D), F32), jax.ShapeDtypeStruct((t, D), F32), jax.ShapeDtypeStruct((t, D), BF)],
        [_rows(tr)] * 3, [pltpu.VMEM((tr + HALO, D), F32), pltpu.VMEM((SUBLANES - 1, tr + HALO, D), F32)], ("arbitrary",), rider)


def _conv_bwd(name, proj, b_in, conv_w, ln_g, ln_b, z0, z1, dz3, dproj, rider=None):
    t = proj.shape[0]
    tr = min(t, 256)
    n_tiles = t // tr

    def compute(r, outs, scr):
        zbuf, dbuf, zs, ds, dwacc = scr
        i = pl.program_id(0)
        g, b = r[5][...], r[6][...]
        zero_row = jnp.zeros((1, D), F32)
        _, vjp = jax.vjp(_ln_silu, r[9][...], g, b)
        dz1, dg, db = vjp(r[11][...])
        dcb = jnp.sum(dz1, axis=0, keepdims=True)
        _, vjp_after = jax.vjp(_ln_silu, r[10][...], g, b)
        dz1_after = vjp_after(r[12][...])[0]
        dbuf[pl.ds(0, tr), :] = dz1
        dbuf[pl.ds(tr, HALO), :] = jnp.where(i < n_tiles - 1, dz1_after, 0.0)
        zbuf[pl.ds(0, HALO), :] = jnp.where(i > 0, r[8][...], 0.0)
        zbuf[pl.ds(HALO, tr), :] = r[7][...]
        _shifted_copies(dbuf, ds, tr + HALO - SUBLANES)
        _shifted_copies(zbuf, zs, tr + HALO - SUBLANES)

        def dz0_strip(s, carry):
            r0 = s * CONV_STRIP
            at = pl.ds(pl.multiple_of(r0, CONV_STRIP), CONV_STRIP)
            acc = jnp.zeros((CONV_STRIP, D), F32)
            for k in range(KW):
                acc = acc + r[4][k:k + 1, :] * _shifted(dbuf, ds, KW - 1 - k, r0, CONV_STRIP)
            a = r[0][at, :] + r[2][...]
            sg = jax.nn.sigmoid(r[1][at, :] + r[3][...])
            dcv = acc * sg
            dcg = acc * a * sg * (1.0 - sg)
            outs[0][at, :] = jnp.concatenate([dcv, dcg], axis=1).astype(BF)
            return carry[0] + jnp.sum(dcv, axis=0, keepdims=True), carry[1] + jnp.sum(dcg, axis=0, keepdims=True)

        dbv, dbg = lax.fori_loop(0, tr // CONV_STRIP, dz0_strip, (zero_row, zero_row))

        for lb in range(D // LANES):
            lanes = slice(lb * LANES, (lb + 1) * LANES)

            def dw_strip(s, accs, lanes=lanes):
                r0 = s * DW_STRIP
                dz = dbuf[pl.ds(pl.multiple_of(r0, SUBLANES), DW_STRIP), lanes]
                out = []
                for k in range(KW):
                    prod = dz * _shifted(zbuf, zs, HALO - (KW - 1) + k, r0, DW_STRIP, lanes)
                    part = prod[0:SUBLANES]
                    for q in range(1, DW_STRIP // SUBLANES):
                        part = part + prod[q * SUBLANES:(q + 1) * SUBLANES]
                    out.append(accs[k] + part)
                return tuple(out)

            accs = lax.fori_loop(0, tr // DW_STRIP, dw_strip, tuple(jnp.zeros((SUBLANES, LANES), F32) for _ in range(KW)))
            for k in range(KW):
                dwacc[pl.ds(k * SUBLANES, SUBLANES), lanes] = accs[k]
        dw_rows = [jnp.sum(dwacc[pl.ds(k * SUBLANES, SUBLANES), :], axis=0, keepdims=True) for k in range(KW)]
        dw_rows.append(jnp.zeros((HALO - KW, D), F32))
        for o, v in zip(outs[1:], (dbv, dbg, jnp.concatenate(dw_rows, axis=0), dcb, dg, db)):
            _accumulate(o, v, i)

    wide = pl.BlockSpec((tr, 2 * D), lambda i: (i, 1))
    return _call(
        name, compute, (n_tiles,), [proj, proj, b_in, b_in, conv_w, ln_g, ln_b, z0, z0, z1, z1, dz3, dz3, dproj],
        [_rows(tr, D, 2), _rows(tr, D, 3), pl.BlockSpec((1, D), lambda i: (0, 2)), pl.BlockSpec((1, D), lambda i: (0, 3)),
         _whole((HALO, D)), _whole((1, D)), _whole((1, D)),
         _rows(tr), _halo_before(tr, 0), _rows(tr), _halo_after(tr, 0, n_tiles), _rows(tr), _halo_after(tr, 0, n_tiles),
         pl.BlockSpec(memory_space=pl.ANY)],
        [jax.ShapeDtypeStruct(dproj.shape, dproj.dtype), _vec(), _vec(), _vec(HALO), _vec(), _vec(), _vec()],
        [wide] + [_whole((1, D))] * 2 + [_whole((HALO, D))] + [_whole((1, D))] * 3,
        [pltpu.VMEM((tr + HALO, D), F32), pltpu.VMEM((tr + HALO, D), F32),
         pltpu.VMEM((SUBLANES - 1, tr + HALO, D), F32), pltpu.VMEM((SUBLANES - 1, tr + HALO, D), F32),
         pltpu.VMEM((HALO * SUBLANES, D), F32)],
        ("arbitrary",), rider, aliases={13: 0})


def _merge_fn(ga, gb, bga, bgb, ya, yb):
    return jax.nn.sigmoid(ga + bga) * ya + jax.nn.sigmoid(gb + bgb) * yb


def _mix_tail(name, ua, z3, proj, b_in, wa, wb, wo, x, g, next_norm):
    t = ua.shape[0]
    tr = min(t, 256)

    def compute(r, outs, _):
        ya = jnp.dot(r[0][...], r[6][...], preferred_element_type=F32)
        yb = jnp.dot(r[1][...], r[7][...], preferred_element_type=F32)
        merged = _merge_fn(r[2][...], r[3][...], r[4][...], r[5][...], ya, yb).astype(BF)
        y = jnp.dot(merged, r[8][...], preferred_element_type=F32)
        x_out = r[9][...] + r[10][...] * y
        for o, v in zip(outs, (ya, yb, merged, y, x_out, _rms_mod(x_out, r[11][...], r[12][...], r[13][...]))):
            o[...] = v.astype(o.dtype)

    row = _whole((1, D))
    return _call(
        name, compute, (t // tr,), [ua, z3, proj, proj, b_in, b_in, wa, wb, wo, x, g, *next_norm],
        [_rows(tr), _rows(tr), _rows(tr, D, 4), _rows(tr, D, 5), pl.BlockSpec((1, D), lambda i: (0, 4)),
         pl.BlockSpec((1, D), lambda i: (0, 5)), _whole((D, D)), _whole((D, D)), _whole((D, D)), _rows(tr), row, row, row, row],
        [jax.ShapeDtypeStruct((t, D), dt) for dt in (F32, F32, BF, BF, F32, BF)], [_rows(tr)] * 6, [], ("arbitrary",))


def _mix_tail_bwd(name, dy, proj, b_in, ya, yb, wa, wb, wo, rider=None):
    t = proj.shape[0]
    tr = min(t, 256)

    def compute(r, outs, _):
        i = pl.program_id(0)
        dm = lax.dot_general(r[0][...], r[9][...], NT, preferred_element_type=F32)
        _, vjp = jax.vjp(_merge_fn, *[x[...] for x in r[1:7]])
        dga, dgb, dbga, dbgb, dya, dyb = vjp(dm)
        dya, dyb = dya.astype(BF), dyb.astype(BF)
        outs[0][...] = jnp.concatenate([dga, dgb], axis=1).astype(BF)
        outs[1][...] = dya
        outs[2][...] = dyb
        outs[3][...] = lax.dot_general(dya, r[7][...], NT, preferred_element_type=F32)
        outs[4][...] = lax.dot_general(dyb, r[8][...], NT, preferred_element_type=F32)
        _accumulate(outs[5], dbga, i)
        _accumulate(outs[6], dbgb, i)

    return _call(
        name, compute, (t // tr,), [dy, proj, proj, b_in, b_in, ya, yb, wa, wb, wo],
        [_rows(tr), _rows(tr, D, 4), _rows(tr, D, 5), pl.BlockSpec((1, D), lambda i: (0, 4)), pl.BlockSpec((1, D), lambda i: (0, 5)),
         _rows(tr), _rows(tr), _whole((D, D)), _whole((D, D)), _whole((D, D))],
        [jax.ShapeDtypeStruct((t, D_IN), BF)] + [jax.ShapeDtypeStruct((t, D), BF)] * 2 + [jax.ShapeDtypeStruct((t, D), F32)] * 2
        + [_vec(), _vec()],
        [pl.BlockSpec((tr, 2 * D), lambda i: (i, 2))] + [_rows(tr)] * 4 + [_whole((1, D))] * 2, [], ("arbitrary",), rider)


def _loss_head(name, x, gain, target, f, g, scale):
    t = x.shape[0]
    tr = min(t, 256)

    def loss_fn(xv, gn, tgt):
        y = xv * lax.rsqrt(jnp.mean(xv * xv, axis=-1, keepdims=True) + EPS) * gn
        return 0.5 * jnp.sum(jnp.mean(jnp.square(y - tgt), axis=-1))

    def fn(i, r, _):
        loss, vjp = jax.vjp(loss_fn, r[0][...], r[1][...], r[2][...])
        dx, dgain, _ = vjp(jnp.ones((), F32))
        df, dg = _gate_grads(dx, r[3][...], r[4][...], scale)
        return [dx, df, dgain, jnp.zeros((1, D), F32) + loss, dg]

    return _rowcall(name, fn, [x, gain, target, f, g], [_rows(tr), _whole((1, D)), _rows(tr), _rows(tr), _whole((1, D))], 2,
                    [jax.ShapeDtypeStruct((t, D), F32), jax.ShapeDtypeStruct((t, D), BF), _vec(), _vec(), _vec()],
                    [_rows(tr)] * 2 + [_whole((1, D))] * 3, t // tr)


def _adamw(w, g, m, v):
    m = B1 * m + (1.0 - B1) * g
    v = B2 * v + (1.0 - B2) * jnp.square(g)
    m_hat = m / BC1
    v_hat = v / BC2
    delta = -LR * (m_hat / (jnp.sqrt(v_hat) + ADAM_EPS) + WD * w)
    return delta, m, v


ADAMW_ROWS = 64


def _adamw_group(name, items, rider=None, rows=ADAMW_ROWS):
    ins, in_specs, out_shapes, out_specs, plan = [], [], [], [], []
    first = 0
    for chip_sum, received, w, m, v in items:
        r, c = w.shape
        tr = min(r, rows)
        n = r // tr

        def tile(i, first=first, n=n):
            return jnp.clip(i - first, 0, n - 1)

        spec = pl.BlockSpec((tr, c), lambda i, tile=tile: (tile(i), 0))
        ins += [chip_sum, *received, w, m, v]
        in_specs += [pl.BlockSpec((None, tr, c), lambda i, tile=tile: (0, tile(i), 0))]
        in_specs += [pl.BlockSpec((g.shape[0], tr, c), lambda i, tile=tile: (0, tile(i), 0)) for g in received]
        in_specs += [spec] * 3
        out_shapes += [jax.ShapeDtypeStruct((r, c), F32)] * 4
        out_specs += [spec] * 4
        plan.append((first, n, [g.shape[0] for g in received]))
        first += n

    def compute(in_refs, out_refs, _):
        i = pl.program_id(0)
        at_in = at_out = 0
        for start, n, counts in plan:
            mine = in_refs[at_in:at_in + 4 + len(counts)]
            outs = out_refs[at_out:at_out + 4]
            at_in += 4 + len(counts)
            at_out += 4

            @pl.when(jnp.logical_and(i >= start, i < start + n))
            def _(mine=mine, outs=outs, counts=counts):
                g = mine[0][...].astype(F32)
                for j, count in enumerate(counts):
                    for s in range(count):
                        g = g + mine[1 + j][s].astype(F32)
                delta, m_new, v_new = _adamw(mine[-3][...], g, mine[-2][...], mine[-1][...])
                for o, val in zip(outs, (g, delta, m_new, v_new)):
                    o[...] = val

    res = _call(name, compute, (first,), ins, in_specs, out_shapes, out_specs, [], ("arbitrary",), rider)
    outs, rode = res if rider else (res, [])
    return [outs[4 * j:4 * j + 4] for j in range(len(items))], rode


def _adamw_small(name, packed_all, late_all, dws_all, vectors, w_s):
    n_vec = len(vectors)

    def body(*refs):
        p_ref, l_ref, d_ref = refs[:3]
        param_refs = refs[3:3 + 3 * n_vec + 3]
        out_refs = refs[3 + 3 * n_vec + 3:-1]
        g_ref = refs[-1]
        g = p_ref[0]
        late = l_ref[0]
        for s in range(1, NDEV):
            g = g + p_ref[s]
            late = late + l_ref[s]
        g_ref[...] = g
        g_ref[pl.ds(0, R_LATE), :] += late

        def update(gp, wmv, outs):
            delta, m_new, v_new = _adamw(wmv[0][...], gp, wmv[1][...], wmv[2][...])
            for o, val in zip(outs, (gp, delta, m_new, v_new)):
                o[...] = val

        for j, (row, rows, *_) in enumerate(vectors):
            pieces = [g_ref[pl.ds(row + r, 1), :] for r in range(rows)]
            update(pieces[0] if rows == 1 else jnp.concatenate(pieces, axis=1), param_refs[3 * j:3 * j + 3], out_refs[4 * j:4 * j + 4])
        gw = d_ref[0]
        for s in range(1, NDEV):
            gw = gw + d_ref[s]
        update(gw, param_refs[3 * n_vec:], out_refs[4 * n_vec:4 * n_vec + 4])
        out_refs[-2][...] = g_ref[pl.ds(R_CW, KW), :]
        out_refs[-1][...] = g_ref[pl.ds(R_LOSS, 1), :]

    params = [a for _, _, w, m, v in vectors for a in (w, m, v)] + list(w_s)
    out_shapes = [jax.ShapeDtypeStruct(w.shape, F32) for _, _, w, _, _ in vectors for _ in range(4)]
    out_shapes += [jax.ShapeDtypeStruct(w_s[0].shape, F32)] * 4 + [jax.ShapeDtypeStruct((KW, D), F32), _vec()]
    res = pl.pallas_call(body, name=name, out_shape=out_shapes, scratch_shapes=[pltpu.VMEM((R_TOTAL, D), F32)],
                         compiler_params=_params(None))(packed_all, late_all, dws_all, *params)
    return [res[4 * j:4 * j + 4] for j in range(n_vec + 1)], res[-2], res[-1]


def _adamw_plain(name, g, w, m, v):
    def body(g_ref, w_ref, m_ref, v_ref, d_ref, mo_ref, vo_ref):
        delta, m_new, v_new = _adamw(w_ref[...], g_ref[...], m_ref[...], v_ref[...])
        d_ref[...] = delta
        mo_ref[...] = m_new
        vo_ref[...] = v_new

    return pl.pallas_call(body, name=name, out_shape=[jax.ShapeDtypeStruct(w.shape, F32)] * 3,
                          compiler_params=_params(None))(g, w, m, v)


def _adamw_ada(name, c_all_t, dmod, dmod_late, w, m, v):
    r, c = w.shape
    tr = 256

    def fn(i, refs, _):
        ca = _silu(refs[0][...])
        dm = refs[1][...] + refs[2][...]
        g = ca[:, 0:1] * dm[0:1, :]
        for b in range(1, NDEV):
            g = g + ca[:, b:b + 1] * dm[b:b + 1, :]
        delta, m_new, v_new = _adamw(refs[3][...], g, refs[4][...], refs[5][...])
        return [g, delta, m_new, v_new]

    spec = pl.BlockSpec((tr, c), lambda i: (i, 0))
    whole = pl.BlockSpec((NDEV, c), lambda i: (0, 0))
    return _rowcall(name, fn, [c_all_t, dmod, dmod_late, w, m, v],
                    [pl.BlockSpec((tr, NDEV), lambda i: (i, 0)), whole, whole, spec, spec, spec], 4,
                    [jax.ShapeDtypeStruct((r, c), F32)] * 4, [spec] * 4, r // tr)


def _ffn_fwd(tag, x, h, g, wg, wu, wd_shard, down_rider, next_norm=None, more_shards=(), norm=None):
    t = x.shape[0]
    tm = min(t, 512 if down_rider else 1024)
    (gate, up, act, *normed), (wd, *more) = _ffn_up(f"{tag}_up", x if norm else h, wg, wu,
                                                    rider=_gather_rider([wd_shard, *more_shards]), norm=norm)
    h = normed[0] if norm else h
    row = pl.BlockSpec((1, D), lambda i, j, k: (0, 0))

    def epilogue(f, xv, gv, *norm):
        x_out = xv + 0.5 * gv * f
        return (x_out, f, _rms_mod(x_out, *norm)) if norm else (x_out, f)

    res = _mm_nn(f"{tag}_down", act, wd.reshape(F, D), tm, D, 1024, extras=(x, g, *(next_norm or ())),
                 extra_specs=(pl.BlockSpec((tm, D), lambda i, j, k: (i, 0)), row, *([row] * 3 if next_norm else [])),
                 epilogue=epilogue, out_dtypes=(F32, BF, BF) if next_norm else (F32, BF), rider=down_rider)
    (x_out, f, *h_next), rode = res if down_rider else (res, None)
    return x_out, (h_next[0] if next_norm else None), (x, h, gate, up, act, f), wd, rode, more


def _ffn_bwd(tag, dx_out, df, saved, gain, sh, sc, wg, wu, wd, slots, dact_rider=None, dwd_rider=None, dwgu_rider=None,
             below=None, fuse_dh=False):
    x, h, gate, up, act, f = saved
    t = x.shape[0]
    tm = min(t, 1024)
    if fuse_dh:
        dgate, dup, dh = _ffn_dact_dh(f"{tag}_dact_dh", df, wd.reshape(F, D), gate, up, wg, wu)
        dwd = _mm_tn(f"{tag}_dwd", act, df, 512, D).reshape(NDEV, F // NDEV, D)
        (dwg, dwu), (sib_d,) = _dw_gate_up(f"{tag}_dwgu", h, dgate, dup, rider=_pair_rider([dwd]))
        (sum_d,) = _pair_add(f"{tag}_dwd_add", [dwd], [sib_d], slots)
        normed, (sib_g, sib_u) = _norm_mod_bwd(f"{tag}_norm_bwd", x, gain, sc, sh, dh, dx_out, below=below,
                                               rider=_pair_rider([dwg, dwu]))
        sum_g, sum_u = _pair_add(f"{tag}_dwgu_add", [dwg, dwu], [sib_g, sib_u], slots)
        return normed, (sum_d, None), sum_g, sum_u, [], [], []

    blk = pl.BlockSpec((t, F // NDEV), lambda i, j, k: (i, j))
    res = _mm_nt(f"{tag}_dact", df, wd.reshape(F, D), t, F // NDEV, out_dtypes=(BF, BF),
                 extras=(gate, up), extra_specs=(blk, blk), epilogue=_swiglu_bwd, rider=dact_rider)
    (dgate, dup), rode_dact = res if dact_rider else (res, [])
    res = _mm_tn(f"{tag}_dwd", act, df, 512, D, rider=dwd_rider)
    dwd, rode_dwd = res if dwd_rider else (res, [])
    dwd = dwd.reshape(NDEV, F // NDEV, D)
    (dwg, dwu), (sib_d, *rode_dwgu) = _dw_gate_up(f"{tag}_dwgu", h, dgate, dup,
                                                  rider=[_pair_rider([dwd])] + ([dwgu_rider] if dwgu_rider else []))
    (sum_d,) = _pair_add(f"{tag}_dwd_add", [dwd], [sib_d], slots)
    dh, (sib_g, sib_u, got_d) = _mm_nt_blocked(f"{tag}_dh", [dgate, dup], [wg, wu], tm,
                                               rider=[_pair_rider([dwg, dwu]), _chip_rider([sum_d])])
    sum_g, sum_u = _pair_add(f"{tag}_dwgu_add", [dwg, dwu], [sib_g, sib_u], slots)
    normed = _norm_mod_bwd(f"{tag}_norm_bwd", x, gain, sc, sh, dh, dx_out, below=below)
    return normed, (sum_d, [got_d]), sum_g, sum_u, rode_dact, rode_dwd, rode_dwgu


def kernel(x, c, ada_w, ada_b, norm_ffn1, ffn1_w_gate, ffn1_w_up, ffn1_w_down, norm_mix, mix_w_in, mix_b_in, sgu_ln_g, sgu_ln_b, sgu_w_s, sgu_b_s, conv_w, conv_b, conv_ln_g, conv_ln_b, w_branch_a, w_branch_b, w_out, norm_ffn2, ffn2_w_gate, ffn2_w_up, ffn2_w_down, norm_final, loss_target, m_ada_w, m_ada_b, m_norm_ffn1, m_ffn1_w_gate, m_ffn1_w_up, m_ffn1_w_down, m_norm_mix, m_mix_w_in, m_mix_b_in, m_sgu_ln_g, m_sgu_ln_b, m_sgu_w_s, m_sgu_b_s, m_conv_w, m_conv_b, m_conv_ln_g, m_conv_ln_b, m_w_branch_a, m_w_branch_b, m_w_out, m_norm_ffn2, m_ffn2_w_gate, m_ffn2_w_up, m_ffn2_w_down, m_norm_final, v_ada_w, v_ada_b, v_norm_ffn1, v_ffn1_w_gate, v_ffn1_w_up, v_ffn1_w_down, v_norm_mix, v_mix_w_in, v_mix_b_in, v_sgu_ln_g, v_sgu_ln_b, v_sgu_w_s, v_sgu_b_s, v_conv_w, v_conv_b, v_conv_ln_g, v_conv_ln_b, v_w_branch_a, v_w_branch_b, v_w_out, v_norm_ffn2, v_ffn2_w_gate, v_ffn2_w_up, v_ffn2_w_down, v_norm_final):
    mx, my, mc = _position()
    me = 4 * mx + 2 * my + mc
    chip = 2 * mx + my
    slots = jnp.stack([2 * (chip ^ k) + mc for k in range(N_CHIPS)]).astype(jnp.int32)
    t = x.shape[1]
    tm = min(t, 1024)
    x0 = x.reshape(t, D)
    target = loss_target.reshape(t, D)
    given = dict(ffn1_w_gate=(ffn1_w_gate, m_ffn1_w_gate, v_ffn1_w_gate), ffn1_w_up=(ffn1_w_up, m_ffn1_w_up, v_ffn1_w_up),
                 ffn1_w_down=(ffn1_w_down, m_ffn1_w_down, v_ffn1_w_down), mix_w_in=(mix_w_in, m_mix_w_in, v_mix_w_in),
                 w_branch_a=(w_branch_a, m_w_branch_a, v_w_branch_a), w_branch_b=(w_branch_b, m_w_branch_b, v_w_branch_b),
                 w_out=(w_out, m_w_out, v_w_out), ffn2_w_gate=(ffn2_w_gate, m_ffn2_w_gate, v_ffn2_w_gate),
                 ffn2_w_up=(ffn2_w_up, m_ffn2_w_up, v_ffn2_w_up), ffn2_w_down=(ffn2_w_down, m_ffn2_w_down, v_ffn2_w_down))
    shard = {n: wmv[0][0].astype(BF) for n, wmv in given.items()}

    ada_cols = N_MOD * D // NDEV
    c_all, taps_all, mod_all, (wg1, wu1) = _prologue(
        "prologue", jnp.pad(c, ((0, SUBLANES - 1), (0, 0))), jnp.pad(conv_w[0], ((0, HALO - KW), (0, 0))), ada_w[0],
        lax.dynamic_slice(ada_b, (0, me * ada_cols), (1, ada_cols)), [shard["ffn1_w_gate"], shard["ffn1_w_up"]])
    conv_w_full = jnp.transpose(taps_all.reshape(NDEV, HALO, CHUNK), (1, 0, 2)).reshape(HALO, D)
    mod = lax.dynamic_index_in_dim(mod_all.reshape(NDEV, NDEV, ada_cols), me, axis=1, keepdims=False).reshape(N_MOD, 1, D)
    sh1, sc1, g1, sh2, sc2, g2, sh3, sc3, g3 = [mod[i] for i in range(N_MOD)]

    x1, h2, saved1, wd1, (w_in,), (wa3, wb3) = _ffn_fwd(
        "ffn1", x0, None, g1, wg1, wu1, shard["ffn1_w_down"], _gather_rider([shard["mix_w_in"]]),
        next_norm=(norm_mix, sc2, sh2), more_shards=(shard["w_branch_a"], shard["w_branch_b"]), norm=(norm_ffn1, sc1, sh1))
    proj, (wg2, wo3) = _mm_nn_blocked("mix_in", h2, w_in, tm, rider=_gather_rider([shard["ffn2_w_gate"], shard["w_out"]]))
    bias_full = jnp.repeat(sgu_b_s[0].T, CHUNK, axis=1)
    (ua,) = _sgu("sgu", proj, mix_b_in, sgu_ln_g, sgu_ln_b, sgu_w_s[0], bias_full)
    (z0, z1, z3), (wu2,) = _conv("conv", proj, mix_b_in, conv_w_full, conv_b, conv_ln_g, conv_ln_b,
                                 rider=_gather_rider([shard["ffn2_w_up"]]))
    wa, wb, wo = wa3.reshape(D, D), wb3.reshape(D, D), wo3.reshape(D, D)
    ya, yb, merged, y, x2, h3 = _mix_tail("mix_tail", ua, z3, proj, mix_b_in, wa, wb, wo, x1, g2, (norm_ffn2, sc3, sh3))
    x3, _, saved3, wd2, _, _ = _ffn_fwd("ffn2", x2, h3, g3, wg2, wu2, shard["ffn2_w_down"], None)

    norm_final2 = norm_final.reshape(1, D)
    dx3, df3, d_norm_final, loss_row, dg3 = _loss_head("loss_head", x3, norm_final2, target, saved3[-1], g3, 0.5)
    (dx2, dy, d_norm_ffn2, dsc3, dsh3, dg2), (sum_d2, _), sum_g2, sum_u2, _, _, _ = _ffn_bwd(
        "ffn2", dx3, df3, saved3, norm_ffn2, sh3, sc3, wg2, wu2, wd2, slots, below=(y, g2, 1.0), fuse_dh=True)
    (dproj, dya, dyb, dua, dz3, db_ga, db_gb), (got_g2_near,) = _mix_tail_bwd(
        "mix_tail_bwd", dy, proj, mix_b_in, ya, yb, wa, wb, wo, rider=_chip_rider([sum_g2], NEIGHBOURS))
    dwo, dwa, dwb = [g.reshape(NDEV, D // NDEV, D) for g in _dw_square("mix_dw", [(merged, dy), (ua, dya), (z3, dyb)])]
    (dproj, db_u, db_v, d_sgu_g, d_sgu_b, d_ws, d_bs_t), (*sib_abo, got_g2_far) = _sgu_bwd(
        "sgu_bwd", proj, mix_b_in, sgu_ln_g, sgu_ln_b, sgu_w_s[0], bias_full, dua, dproj,
        rider=[_pair_rider([dwa, dwb, dwo]), _chip_rider([sum_g2], DIAGONAL)])
    sum_a, sum_b, sum_o = _pair_add("mix_dw_add", [dwa, dwb, dwo], sib_abo, slots)
    (dproj, db_cv, db_cg, d_cw, d_cb, d_cln_g, d_cln_b), (got_u2, got_d2) = _conv_bwd(
        "conv_bwd", proj, mix_b_in, conv_w_full, conv_ln_g, conv_ln_b, z0, z1, dz3, dproj, rider=_chip_rider([sum_u2, sum_d2]))
    dwin, (got_a, got_b, got_o) = _mm_tn_blocked("mix_dwin", h2, dproj, rider=_chip_rider([sum_a, sum_b, sum_o]))

    d_bs = jnp.transpose(d_bs_t[:, :HEADS])
    zero = jnp.zeros((1, D), F32)
    pack_rows = [zero, zero, zero, zero, zero, dg2, dsh3, dsc3, dg3,
                 zero, zero, d_norm_ffn2, d_norm_final,
                 db_u, db_v, db_cv, db_cg, db_ga, db_gb,
                 d_sgu_g, d_sgu_b, d_bs.reshape(1, D), d_cb, d_cln_g, d_cln_b,
                 d_cw[:KW], loss_row, jnp.zeros((R_TOTAL - R_LOSS - 1, D), F32)]
    packed = jnp.concatenate(pack_rows, axis=0)
    d_ws2 = d_ws.reshape(HEADS * CHUNK, CHUNK)
    dh2, (sib_in, packed_all, dws_all) = _mm_nt_blocked("mix_in_bwd", [dproj], [w_in], tm,
                                                        rider=[_pair_rider([dwin]), _gather_rider([packed, d_ws2])])
    (sum_in,) = _pair_add("mix_dwin_add", [dwin], [sib_in], slots)
    dx1, df1, d_norm_mix, dsc2, dsh2, dg1 = _norm_mod_bwd("mix_norm_bwd", x1, norm_mix, sc2, sh2, dh2, dx2,
                                                          below=(saved1[-1], g1, 0.5))
    (dx0, d_norm_ffn1, dsc1, dsh1), down1, sum_g1, sum_u1, (got_in_near,), _, (got_in_far,) = _ffn_bwd(
        "ffn1", dx1, df1, saved1, norm_ffn1, sh1, sc1, wg1, wu1, wd1, slots,
        dact_rider=_chip_rider([sum_in], NEIGHBOURS), dwgu_rider=_chip_rider([sum_in], DIAGONAL))
    packed_late = jnp.concatenate([dsh1, dsc1, dg1, dsh2, dsc2, jnp.zeros((4, D), F32), d_norm_ffn1, d_norm_mix,
                                   jnp.zeros((R_LATE - 11, D), F32)], axis=0)
    grads = dict(ffn2_w_gate=(sum_g2, [got_g2_near, got_g2_far]), ffn2_w_up=(sum_u2, [got_u2]), ffn2_w_down=(sum_d2, [got_d2]),
                 mix_w_in=(sum_in, [got_in_near, got_in_far]), w_branch_a=(sum_a, [got_a]), w_branch_b=(sum_b, [got_b]),
                 w_out=(sum_o, [got_o]), ffn1_w_down=down1)
    done, (late_all, got_g1, got_u1) = _adamw_group(
        "adamw_most", [(cs, got, *[a[0] for a in given[n]]) for n, (cs, got) in grads.items()],
        rider=[_gather_rider([packed_late]), _chip_rider([sum_g1, sum_u1])])
    last, _ = _adamw_group("adamw_ffn1_in", [(sum_g1, [got_g1], *[a[0] for a in given["ffn1_w_gate"]]),
                                            (sum_u1, [got_u1], *[a[0] for a in given["ffn1_w_up"]])], rows=256)
    big_out = {n: [o.reshape(given[n][0].shape) for o in outs]
               for n, outs in zip([*grads, "ffn1_w_gate", "ffn1_w_up"], [*done, *last])}

    flat = lambda a: a.reshape(1, -1)
    vectors = [("ada_b", 0, 9, ada_b, m_ada_b, v_ada_b), ("norm_ffn1", 9, 1, norm_ffn1, m_norm_ffn1, v_norm_ffn1),
               ("norm_mix", 10, 1, norm_mix, m_norm_mix, v_norm_mix), ("norm_ffn2", 11, 1, norm_ffn2, m_norm_ffn2, v_norm_ffn2),
               ("norm_final", 12, 1, norm_final, m_norm_final, v_norm_final), ("mix_b_in", 13, 6, mix_b_in, m_mix_b_in, v_mix_b_in),
               ("sgu_ln_g", 19, 1, sgu_ln_g, m_sgu_ln_g, v_sgu_ln_g), ("sgu_ln_b", 20, 1, sgu_ln_b, m_sgu_ln_b, v_sgu_ln_b),
               ("sgu_b_s", 21, 1, sgu_b_s, m_sgu_b_s, v_sgu_b_s), ("conv_b", 22, 1, conv_b, m_conv_b, v_conv_b),
               ("conv_ln_g", 23, 1, conv_ln_g, m_conv_ln_g, v_conv_ln_g), ("conv_ln_b", 24, 1, conv_ln_b, m_conv_ln_b, v_conv_ln_b)]
    small_out, d_cw_all, loss_sum = _adamw_small(
        "adamw_small", packed_all, late_all, dws_all, [(row, rows, flat(wv), flat(mv), flat(vv)) for _, row, rows, wv, mv, vv in vectors],
        [a.reshape(HEADS * CHUNK, CHUNK) for a in (sgu_w_s, m_sgu_w_s, v_sgu_w_s)])
    small = {n: [o.reshape(wv.shape) for o in outs] for (n, _, _, wv, _, _), outs in zip(vectors, small_out)}
    small["sgu_w_s"] = [o.reshape(sgu_w_s.shape) for o in small_out[-1]]
    g_cw = lax.dynamic_slice(d_cw_all, (0, me * CHUNK), (KW, CHUNK))
    small["conv_w"] = [o.reshape(conv_w.shape) for o in (g_cw, *_adamw_plain("adamw_conv_w", g_cw, conv_w[0], m_conv_w[0], v_conv_w[0]))]
    loss = loss_sum[0, 0]

    dmod_cols = [lax.dynamic_slice(a[:, :N_MOD, :].reshape(NDEV, N_MOD * D), (0, me * ada_cols), (NDEV, ada_cols))
                 for a in (packed_all, late_all)]
    ada_out = [o.reshape(ada_w.shape) for o in _adamw_ada("adamw_ada_w", jnp.transpose(c_all), *dmod_cols, ada_w[0], m_ada_w[0], v_ada_w[0])]

    order = ["ada_w", "ada_b", "norm_ffn1", "ffn1_w_gate", "ffn1_w_up", "ffn1_w_down", "norm_mix", "mix_w_in", "mix_b_in",
             "sgu_ln_g", "sgu_ln_b", "sgu_w_s", "sgu_b_s", "conv_w", "conv_b", "conv_ln_g", "conv_ln_b", "w_branch_a",
             "w_branch_b", "w_out", "norm_ffn2", "ffn2_w_gate", "ffn2_w_up", "ffn2_w_down", "norm_final"]

    def leaf(n, kind):
        if n == "ada_w":
            return ada_out[kind]
        if n in big_out:
            return big_out[n][kind]
        return small[n][kind]

    return (loss, dx0.reshape(x.shape), *[leaf(n, kind) for kind in range(4) for n in order])
```

```python
import jax
import jax.numpy as jnp
from jax import lax
from jax.experimental import pallas as pl
from jax.experimental.pallas import tpu as pltpu

D = 1024
F = 4 * D
D_IN = 6 * D
HEADS = 8
CHUNK = 128
KW = 31
HALO = 32
N_MOD = 9
NDEV = 8
N_CHIPS = 4
EPS = 1e-6
LR, B1, B2, ADAM_EPS, WD, STEP = 0.001, 0.9, 0.999, 1e-08, 0.01, 10
BC1 = 1.0 - B1 ** STEP
BC2 = 1.0 - B2 ** STEP
VMEM_LIMIT = 56 * 1024 * 1024
MESH = pl.DeviceIdType.MESH
HBM = pl.BlockSpec(memory_space=pltpu.HBM)
VMEM = pl.BlockSpec(memory_space=pltpu.VMEM)
BF = jnp.bfloat16
F32 = jnp.float32

NN = (((1,), (0,)), ((), ()))
NT = (((1,), (1,)), ((), ()))
TN = (((0,), (0,)), ((), ()))

R_CW, R_LOSS, R_TOTAL = 25, 56, 64
R_LATE = 16


def _params(sem):
    return pltpu.CompilerParams(dimension_semantics=sem, vmem_limit_bytes=VMEM_LIMIT)


def _position():
    return lax.axis_index("x"), lax.axis_index("y"), lax.axis_index("c")


def _flip(pos, k):
    x, y, c = pos
    return (x ^ (k >> 2 & 1), y ^ (k >> 1 & 1), c ^ (k & 1))


def _index(pos):
    return 4 * pos[0] + 2 * pos[1] + pos[2]


def _gather_rows(x_ref, out_ref, send_sems, recv_sems, local_sem):
    m_per = x_ref.shape[0]
    x, y, c = _position()
    me, sibling = (x, y, c), (x, y, 1 - c)
    chips = [(1 - x, y), (x, 1 - y), (1 - x, 1 - y)]

    def rows(pos):
        return out_ref.at[pl.ds(_index(pos) * m_per, m_per), :]

    def copy(k, block, to, src=None):
        return pltpu.make_async_remote_copy(
            src_ref=rows(block) if src is None else src, dst_ref=rows(block),
            send_sem=send_sems.at[k], recv_sem=recv_sems.at[k], device_id=to, device_id_type=MESH)

    mine = pltpu.make_async_copy(x_ref, rows(me), local_sem)
    mine.start()
    first = [copy(0, me, sibling, src=x_ref)]
    first += [copy(1 + j, me, (*chip, c), src=x_ref) for j, chip in enumerate(chips)]
    for cp in first:
        cp.start()
    passed = [copy(4 + j, (*chip, c), sibling) for j, chip in enumerate(chips)]
    for j, chip in enumerate(chips):
        copy(1 + j, (*chip, c), me).wait_recv()
        passed[j].start()
    copy(0, sibling, me).wait_recv()
    for j, chip in enumerate(chips):
        copy(4 + j, (*chip, 1 - c), me).wait_recv()
    for cp in first + passed:
        cp.wait_send()
    mine.wait()


def _prologue(name, c_rows, taps, ada_w, ada_b, shards):
    rider = _gather_rider(shards)
    n = len(shards)
    nc = ada_w.shape[1]

    def body(*refs):
        c_ref, taps_ref, w_ref, b_ref = refs[:4]
        shard_refs = refs[4:4 + n]
        c_all_ref, taps_all_ref, mod_all_ref = refs[4 + n:7 + n]
        gathered_refs = refs[7 + n:7 + 2 * n]
        c_buf, mod_part, sems = refs[7 + 2 * n], refs[8 + 2 * n], refs[9 + 2 * n:]
        _gather_rows(c_ref, c_buf, *sems[0:3])
        rider.start(shard_refs, gathered_refs, sems[9:])
        c_all = jnp.concatenate([c_buf[pl.ds(d * SUBLANES, 1), :] for d in range(NDEV)], axis=0)
        c_all_ref[...] = c_all
        mod_part[...] = jnp.dot(_silu(c_all), w_ref[...], preferred_element_type=F32) + b_ref[...]
        _gather_rows(taps_ref, taps_all_ref, *sems[3:6])
        _gather_rows(mod_part, mod_all_ref, *sems[6:9])
        rider.mid(shard_refs, gathered_refs, sems[9:])
        rider.relay(shard_refs, gathered_refs, sems[9:])
        rider.finish(shard_refs, gathered_refs, sems[9:])

    small_sems = [pltpu.SemaphoreType.DMA((7,)), pltpu.SemaphoreType.DMA((7,)), pltpu.SemaphoreType.DMA] * 3
    res = pl.pallas_call(
        body, name=name,
        out_shape=[jax.ShapeDtypeStruct((NDEV, D), F32), jax.ShapeDtypeStruct((NDEV * taps.shape[0], taps.shape[1]), F32),
                   jax.ShapeDtypeStruct((NDEV * NDEV, nc), F32)] + rider.out_shapes,
        in_specs=[VMEM] * 4 + [HBM] * n, out_specs=[VMEM] * 3 + [HBM] * n,
        scratch_shapes=[pltpu.VMEM((NDEV * SUBLANES, D), F32), pltpu.VMEM((NDEV, nc), F32)] + small_sems + rider.sems,
        compiler_params=_params(None),
    )(c_rows, taps, ada_w, ada_b, *shards)
    return res[0], res[1], res[2], res[3:]


class _Rider:
    def __init__(self, ins, out_shapes, sems, start, finish, mid=None, relay=None):
        self.ins, self.out_shapes, self.sems = list(ins), list(out_shapes), list(sems)
        self.start, self.finish, self.mid, self.relay = start, finish, mid, relay


def _gather_rider(shards):
    n = len(shards)

    def setup(ins, outs, sems):
        send_sems, recv_sems, local_sems = sems
        x, y, c = _position()
        places = dict(me=(x, y, c), sibling=(x, y, 1 - c), xn=(1 - x, y, c), yn=(x, 1 - y, c), diagonal=(1 - x, 1 - y, c),
                      passed_on=(x ^ c, y ^ (1 - c), c), passed_to=(x ^ (1 - c), y ^ c, c))

        def copy(a, k, block, to, own=False):
            slot = outs[a].at[_index(block)]
            return pltpu.make_async_remote_copy(
                src_ref=ins[a] if own else slot, dst_ref=slot,
                send_sem=send_sems.at[k, a], recv_sem=recv_sems.at[k, a], device_id=to, device_id_type=MESH)

        def local(a):
            return pltpu.make_async_copy(ins[a], outs[a].at[_index(places["me"])], local_sems.at[a])

        return places, copy, local

    def start(ins, outs, sems):
        p, copy, local = setup(ins, outs, sems)
        for a in range(n):
            local(a).start()
            for k, to in enumerate(("sibling", "xn", "yn")):
                copy(a, k, p["me"], p[to], own=True).start()

    def mid(ins, outs, sems):
        p, copy, _ = setup(ins, outs, sems)
        for a in range(n):
            copy(a, 1, p["xn"], p["me"]).wait_recv()
            copy(a, 2, p["yn"], p["me"]).wait_recv()
            copy(a, 3, p["passed_on"], p["passed_to"]).start()
            copy(a, 4, p["xn"], p["sibling"]).start()
            copy(a, 5, p["yn"], p["sibling"]).start()

    def relay(ins, outs, sems):
        p, copy, _ = setup(ins, outs, sems)
        for a in range(n):
            copy(a, 3, p["diagonal"], p["me"]).wait_recv()
            copy(a, 6, p["diagonal"], p["sibling"]).start()

    def finish(ins, outs, sems):
        p, copy, local = setup(ins, outs, sems)
        x, y, c = p["me"]
        for a in range(n):
            for k, block in ((0, (x, y, 1 - c)), (4, (1 - x, y, 1 - c)), (5, (x, 1 - y, 1 - c)), (6, (1 - x, 1 - y, 1 - c))):
                copy(a, k, block, p["me"]).wait_recv()
            for k, to in enumerate(("sibling", "xn", "yn")):
                copy(a, k, p["me"], p[to], own=True).wait_send()
            copy(a, 3, p["passed_on"], p["passed_to"]).wait_send()
            for k, block in ((4, "xn"), (5, "yn"), (6, "diagonal")):
                copy(a, k, p[block], p["sibling"]).wait_send()
            local(a).wait()

    return _Rider(shards, [jax.ShapeDtypeStruct((NDEV, *s.shape), s.dtype) for s in shards],
                  [pltpu.SemaphoreType.DMA((7, n)), pltpu.SemaphoreType.DMA((7, n)), pltpu.SemaphoreType.DMA((n,))],
                  start, finish, mid, relay)


def _pair_rider(parts):
    n = len(parts)

    def copies(ins, outs, sems):
        send_sems, recv_sems = sems
        x, y, c = _position()
        q = 2 * x + y
        return [pltpu.make_async_remote_copy(
            src_ref=ins[a].at[2 * (q ^ k) + (1 - c)], dst_ref=outs[a].at[k],
            send_sem=send_sems.at[k, a], recv_sem=recv_sems.at[k, a], device_id=(x, y, 1 - c), device_id_type=MESH)
            for a in range(n) for k in range(N_CHIPS)]

    def start(ins, outs, sems):
        for cp in copies(ins, outs, sems):
            cp.start()

    def finish(ins, outs, sems):
        for cp in copies(ins, outs, sems):
            cp.wait()

    return _Rider(parts, [jax.ShapeDtypeStruct((N_CHIPS, *p.shape[1:]), p.dtype) for p in parts],
                  [pltpu.SemaphoreType.DMA((N_CHIPS, n)), pltpu.SemaphoreType.DMA((N_CHIPS, n))], start, finish)


NEIGHBOURS = (1, 2)
DIAGONAL = (3,)
OTHER_CHIPS = NEIGHBOURS + DIAGONAL


def _chip_rider(sums, ks=OTHER_CHIPS):
    n = len(sums)

    def copies(ins, outs, sems):
        send_sems, recv_sems = sems
        me = _position()
        return [pltpu.make_async_remote_copy(
            src_ref=ins[a].at[k], dst_ref=outs[a].at[j],
            send_sem=send_sems.at[j, a], recv_sem=recv_sems.at[j, a], device_id=_flip(me, 2 * k), device_id_type=MESH)
            for a in range(n) for j, k in enumerate(ks)]

    def start(ins, outs, sems):
        for cp in copies(ins, outs, sems):
            cp.start()

    def finish(ins, outs, sems):
        for cp in copies(ins, outs, sems):
            cp.wait()

    return _Rider(sums, [jax.ShapeDtypeStruct((len(ks), *s.shape[1:]), s.dtype) for s in sums],
                  [pltpu.SemaphoreType.DMA((len(ks), n)), pltpu.SemaphoreType.DMA((len(ks), n))], start, finish)


def _grid_edge(grid, last):
    cond = None
    for d, n in enumerate(grid):
        here = pl.program_id(d) == (n - 1 if last else 0)
        cond = here if cond is None else jnp.logical_and(cond, here)
    return cond


def _call(name, compute, grid, ins, in_specs, out_shapes, out_specs, scratch_shapes, semantics, rider=None, aliases=None):
    riders = [rider] if isinstance(rider, _Rider) else list(rider or [])
    n_in, n_out, n_scr = len(ins), len(out_shapes), len(scratch_shapes)
    n_rin, n_rout, n_rsem = [sum(len(part(r)) for r in riders) for part in (lambda r: r.ins, lambda r: r.out_shapes, lambda r: r.sems)]
    cuts = [0, n_in, n_in + n_rin, n_in + n_rin + n_out, n_in + n_rin + n_out + n_rout, n_in + n_rin + n_out + n_rout + n_scr]

    def body(*refs):
        in_refs, rin_refs, out_refs, rout_refs, scr_refs = [refs[a:b] for a, b in zip(cuts[:-1], cuts[1:])]
        rsem_refs = refs[cuts[-1]:]
        mine, at = [], [0, 0, 0]
        for r in riders:
            mine.append((r, rin_refs[at[0]:at[0] + len(r.ins)], rout_refs[at[1]:at[1] + len(r.out_shapes)],
                         rsem_refs[at[2]:at[2] + len(r.sems)]))
            at = [at[0] + len(r.ins), at[1] + len(r.out_shapes), at[2] + len(r.sems)]
        if riders:
            @pl.when(_grid_edge(grid, last=False))
            def _():
                for r, a, b, c in mine:
                    r.start(a, b, c)

        if any(r.mid for r in riders):
            step, steps = 0, 1
            for d, size in enumerate(grid):
                step, steps = step * size + pl.program_id(d), steps * size

            @pl.when(step == steps * 5 // 8)
            def _():
                for r, a, b, c in mine:
                    if r.mid:
                        r.mid(a, b, c)

        if any(r.relay for r in riders):
            @pl.when(_grid_edge(grid, last=True))
            def _():
                for r, a, b, c in mine:
                    if r.relay:
                        r.relay(a, b, c)

        compute(in_refs, out_refs, scr_refs)
        if riders:
            @pl.when(_grid_edge(grid, last=True))
            def _():
                for r, a, b, c in mine:
                    r.finish(a, b, c)

    res = pl.pallas_call(
        body, name=name, grid=grid,
        out_shape=list(out_shapes) + [s for r in riders for s in r.out_shapes],
        in_specs=list(in_specs) + [HBM] * n_rin, out_specs=list(out_specs) + [HBM] * n_rout,
        scratch_shapes=list(scratch_shapes) + [s for r in riders for s in r.sems],
        input_output_aliases=aliases or {}, compiler_params=_params(semantics),
    )(*ins, *[a for r in riders for a in r.ins])
    return (res[:n_out], res[n_out:]) if riders else res


def _pair_add(name, parts, from_sibling, slots):
    n = len(parts)

    def body(s_ref, *refs):
        for a in range(n):
            refs[2 * n + a][...] = (refs[a][...].astype(F32) + refs[n + a][...].astype(F32)).astype(refs[2 * n + a].dtype)

    def slab(p, picked):
        _, r, c = p.shape
        return pl.BlockSpec((None, r, c), (lambda k, s: (s[k], 0, 0)) if picked else (lambda k, s: (k, 0, 0)))

    return pl.pallas_call(
        body, name=name,
        grid_spec=pltpu.PrefetchScalarGridSpec(
            num_scalar_prefetch=1, grid=(N_CHIPS,),
            in_specs=[slab(p, True) for p in parts] + [slab(p, False) for p in parts],
            out_specs=[slab(p, False) for p in parts]),
        out_shape=[jax.ShapeDtypeStruct((N_CHIPS, *p.shape[1:]), p.dtype) for p in parts],
        compiler_params=_params(("arbitrary",)),
    )(slots, *parts, *from_sibling)


def _mm(name, pairs, dims, grid, nk, out_shapes, out_specs, extras=(), extra_specs=(), epilogue=None, acc_shape=None, rider=None):
    n_pairs = len(pairs)

    def compute(ins, outs, scratch):
        def partial_sum():
            total = None
            for p in range(n_pairs):
                d = lax.dot_general(ins[2 * p][...], ins[2 * p + 1][...], dims, preferred_element_type=F32)
                total = d if total is None else total + d
            return total

        def finish(r):
            ex = [e[...] for e in ins[2 * n_pairs:]]
            res = epilogue(r, *ex) if epilogue is not None else (r,)
            for o, v in zip(outs, res):
                o[...] = v.astype(o.dtype)

        if nk == 1:
            finish(partial_sum())
        else:
            acc = scratch[0]
            k = pl.program_id(2)

            @pl.when(k == 0)
            def _():
                acc[...] = partial_sum()

            @pl.when(k > 0)
            def _():
                acc[...] += partial_sum()

            @pl.when(k == nk - 1)
            def _():
                finish(acc[...])

    operands, specs = [], []
    for a, a_spec, b, b_spec in pairs:
        operands += [a, b]
        specs += [a_spec, b_spec]
    return _call(name, compute, grid, operands + list(extras), specs + list(extra_specs), out_shapes, out_specs,
                 [pltpu.VMEM(acc_shape, F32)] if nk > 1 else [], ("parallel", "parallel", "arbitrary"), rider)


def _single(res, rider):
    return (res[0][0], res[1]) if rider else res[0]


def _silu(x):
    return x * jax.nn.sigmoid(x)


def _ffn_up(name, h, wg, wu, rider=None, norm=None):
    t = h.shape[0]
    tm = min(t, 1024)
    nb = F // NDEV

    def compute(ins, outs, scr):
        if norm:
            @pl.when(pl.program_id(1) == 0)
            def _():
                scr[0][...] = _rms_mod(ins[0][...], ins[3][...], ins[4][...], ins[5][...]).astype(BF)
                outs[3][...] = scr[0][...]

            hv = scr[0][...]
        else:
            hv = ins[0][...]
        g = jnp.dot(hv, ins[1][...], preferred_element_type=F32)
        u = jnp.dot(hv, ins[2][...], preferred_element_type=F32)
        outs[0][...] = g.astype(BF)
        outs[1][...] = u.astype(BF)
        outs[2][...] = (_silu(g) * u).astype(BF)

    w_spec = pl.BlockSpec((None, D, nb), lambda i, j: (j, 0, 0))
    o_spec = pl.BlockSpec((tm, nb), lambda i, j: (i, j))
    rows = pl.BlockSpec((tm, D), lambda i, j: (i, 0))
    vec = pl.BlockSpec((1, D), lambda i, j: (0, 0))
    return _call(name, compute, (t // tm, NDEV), [h, wg, wu, *(norm or ())], [rows, w_spec, w_spec] + [vec] * (3 if norm else 0),
                 [jax.ShapeDtypeStruct((t, F), BF)] * 3 + ([jax.ShapeDtypeStruct((t, D), BF)] if norm else []),
                 [o_spec] * 3 + ([rows] if norm else []), [pltpu.VMEM((tm, D), BF)] if norm else [],
                 ("parallel", "arbitrary"), rider)


def _mm_nn(name, a, b, tm, tn, tk, extras=(), extra_specs=(), epilogue=None, out_dtypes=(F32,), rider=None):
    m, kk = a.shape
    n = b.shape[1]
    nk = kk // tk
    return _mm(
        name, [(a, pl.BlockSpec((tm, tk), lambda i, j, k: (i, k)), b, pl.BlockSpec((tk, tn), lambda i, j, k: (k, j)))], NN,
        (m // tm, n // tn, nk), nk,
        [jax.ShapeDtypeStruct((m, n), dt) for dt in out_dtypes],
        [pl.BlockSpec((tm, tn), lambda i, j, k: (i, j))] * len(out_dtypes),
        extras, extra_specs, epilogue, (tm, tn), rider)


def _mm_nn_blocked(name, a, b3, tm, rider=None):
    m = a.shape[0]
    nb = b3.shape[2]
    return _single(_mm(
        name, [(a, pl.BlockSpec((tm, D), lambda i, j, k: (i, 0)), b3, pl.BlockSpec((None, D, nb), lambda i, j, k: (j, 0, 0)))], NN,
        (m // tm, NDEV, 1), 1,
        [jax.ShapeDtypeStruct((m, NDEV * nb), F32)], [pl.BlockSpec((tm, nb), lambda i, j, k: (i, j))], rider=rider), rider)


def _mm_nt(name, a, b, tm, tn, out_dtypes=(F32,), extras=(), extra_specs=(), epilogue=None, rider=None):
    m, kk = a.shape
    n = b.shape[0]
    return _mm(
        name, [(a, pl.BlockSpec((tm, kk), lambda i, j, k: (i, 0)), b, pl.BlockSpec((tn, kk), lambda i, j, k: (j, 0)))], NT,
        (m // tm, n // tn, 1), 1,
        [jax.ShapeDtypeStruct((m, n), dt) for dt in out_dtypes],
        [pl.BlockSpec((tm, tn), lambda i, j, k: (i, j))] * len(out_dtypes),
        extras, extra_specs, epilogue, rider=rider)


def _mm_nt_blocked(name, a_list, b3_list, tm, rider=None):
    m = a_list[0].shape[0]
    nb = b3_list[0].shape[2]
    pairs = [(a, pl.BlockSpec((tm, nb), lambda i, j, k: (i, k)), b3, pl.BlockSpec((None, D, nb), lambda i, j, k: (k, 0, 0)))
             for a, b3 in zip(a_list, b3_list)]
    return _single(_mm(name, pairs, NT, (m // tm, 1, NDEV), NDEV,
                       [jax.ShapeDtypeStruct((m, D), F32)], [pl.BlockSpec((tm, D), lambda i, j, k: (i, 0))],
                       acc_shape=(tm, D), rider=rider), rider)


def _mm_tn(name, a, b, tm, tn, rider=None):
    t, m = a.shape
    n = b.shape[1]
    return _single(_mm(
        name, [(a, pl.BlockSpec((t, tm), lambda i, j, k: (0, i)), b, pl.BlockSpec((t, tn), lambda i, j, k: (0, j)))], TN,
        (m // tm, n // tn, 1), 1,
        [jax.ShapeDtypeStruct((m, n), BF)], [pl.BlockSpec((tm, tn), lambda i, j, k: (i, j))], rider=rider), rider)


def _mm_tn_blocked(name, a, b, rider=None):
    t = a.shape[0]
    nb = b.shape[1] // NDEV
    return _single(_mm(
        name, [(a, pl.BlockSpec((t, D), lambda i, j, k: (0, 0)), b, pl.BlockSpec((t, nb), lambda i, j, k: (0, j)))], TN,
        (1, NDEV, 1), 1,
        [jax.ShapeDtypeStruct((NDEV, D, nb), BF)], [pl.BlockSpec((None, D, nb), lambda i, j, k: (j, 0, 0))], rider=rider), rider)


def _dw_gate_up(name, h, dgate, dup, rider=None):
    t = h.shape[0]
    nb = F // NDEV

    def compute(ins, outs, _):
        hv = ins[0][...]
        outs[0][...] = lax.dot_general(hv, ins[1][...], TN, preferred_element_type=F32).astype(BF)
        outs[1][...] = lax.dot_general(hv, ins[2][...], TN, preferred_element_type=F32).astype(BF)

    d_spec = pl.BlockSpec((t, nb), lambda j: (0, j))
    o_spec = pl.BlockSpec((None, D, nb), lambda j: (j, 0, 0))
    return _call(name, compute, (NDEV,), [h, dgate, dup], [pl.BlockSpec((t, D), lambda j: (0, 0)), d_spec, d_spec],
                 [jax.ShapeDtypeStruct((NDEV, D, nb), BF)] * 2, [o_spec] * 2, [], ("arbitrary",), rider)


def _dw_square(name, pairs):
    t = pairs[0][0].shape[0]
    tm = 512
    n = len(pairs)

    def compute(ins, outs, _):
        for p in range(n):
            outs[p][...] = lax.dot_general(ins[2 * p][...], ins[2 * p + 1][...], TN, preferred_element_type=F32).astype(BF)

    return _call(name, compute, (D // tm,), [x for pair in pairs for x in pair],
                 [pl.BlockSpec((t, tm), lambda i: (0, i)), pl.BlockSpec((t, D), lambda i: (0, 0))] * n,
                 [jax.ShapeDtypeStruct((D, D), BF)] * n, [pl.BlockSpec((tm, D), lambda i: (i, 0))] * n, [], ("arbitrary",))


def _swiglu_bwd(da, gate, up):
    gate = gate.astype(F32)
    s = jax.nn.sigmoid(gate)
    return da * up.astype(F32) * (s * (1.0 + gate * (1.0 - s))), da * (gate * s)


def _ffn_dact_dh(name, df, wd, gate, up, wg, wu, rider=None):
    t = df.shape[0]
    tm = min(t, 1024)
    nb = F // NDEV

    def compute(ins, outs, scr):
        acc = scr[0]
        j = pl.program_id(1)
        da = lax.dot_general(ins[0][...], ins[1][...], NT, preferred_element_type=F32)
        dgate, dup = _swiglu_bwd(da, ins[2][...], ins[3][...])
        dgate, dup = dgate.astype(BF), dup.astype(BF)
        outs[0][...] = dgate
        outs[1][...] = dup
        part = (lax.dot_general(dgate, ins[4][...], NT, preferred_element_type=F32)
                + lax.dot_general(dup, ins[5][...], NT, preferred_element_type=F32))

        @pl.when(j == 0)
        def _():
            acc[...] = part

        @pl.when(j > 0)
        def _():
            acc[...] += part

        @pl.when(j == NDEV - 1)
        def _():
            outs[2][...] = acc[...]

    blk = pl.BlockSpec((tm, nb), lambda i, j: (i, j))
    w3 = pl.BlockSpec((None, D, nb), lambda i, j: (j, 0, 0))
    row = pl.BlockSpec((tm, D), lambda i, j: (i, 0))
    return _call(name, compute, (t // tm, NDEV), [df, wd, gate, up, wg, wu],
                 [row, pl.BlockSpec((nb, D), lambda i, j: (j, 0)), blk, blk, w3, w3],
                 [jax.ShapeDtypeStruct((t, F), BF)] * 2 + [jax.ShapeDtypeStruct((t, D), F32)], [blk, blk, row],
                 [pltpu.VMEM((tm, D), F32)], ("parallel", "arbitrary"), rider)


def _rowcall(name, fn, ins, in_specs, n_row_out, out_shapes, out_specs, grid, scratch_shapes=(), rider=None, aliases=None):
    def accumulate(o, v, i):
        @pl.when(i == 0)
        def _():
            o[...] = v.astype(o.dtype)

        @pl.when(i > 0)
        def _():
            o[...] += v.astype(o.dtype)

    def compute(in_refs, out_refs, scr):
        i = pl.program_id(0)
        vals = fn(i, in_refs, scr)
        for idx, (o, v) in enumerate(zip(out_refs, vals)):
            if idx < n_row_out:
                o[...] = v.astype(o.dtype)
            else:
                accumulate(o, v, i)

    return _call(name, compute, (grid,), ins, in_specs, out_shapes, out_specs, list(scratch_shapes), ("arbitrary",), rider, aliases)


def _rows(tr, w=D, cb=0):
    return pl.BlockSpec((tr, w), lambda i: (i, cb))


def _whole(shape):
    nd = len(shape)
    return pl.BlockSpec(shape, lambda i: (0,) * nd)


def _vec(n=1):
    return jax.ShapeDtypeStruct((n, D), F32)


def _rms_mod(x, gain, sc, sh):
    y = x * lax.rsqrt(jnp.mean(x * x, axis=-1, keepdims=True) + EPS)
    return (y * gain) * (1.0 + sc) + sh


def _layer_norm(x, g, b):
    mu = jnp.mean(x, axis=-1, keepdims=True)
    var = jnp.mean(jnp.square(x - mu), axis=-1, keepdims=True)
    return (x - mu) * lax.rsqrt(var + EPS) * g + b


def _gate_grads(dx, f, g, scale):
    return scale * g * dx, jnp.sum(scale * dx * f.astype(F32), axis=0, keepdims=True)


def _norm_mod_bwd(name, x, gain, sc, sh, dh, dres, below=None, rider=None):
    t = x.shape[0]
    tr = min(t, 256)

    def fn(i, r, _):
        _, vjp = jax.vjp(_rms_mod, r[0][...], r[1][...], r[2][...], r[3][...])
        dx, dgain, dsc, dsh = vjp(r[4][...])
        dx = dx + r[5][...]
        if below is None:
            return [dx, dgain, dsc, dsh]
        df, dg = _gate_grads(dx, r[6][...], r[7][...], below[2])
        return [dx, df, dgain, dsc, dsh, dg]

    ins, specs = [x, gain, sc, sh, dh, dres], [_rows(tr)] + [_whole((1, D))] * 3 + [_rows(tr)] * 2
    outs, out_specs = [jax.ShapeDtypeStruct((t, D), F32)], [_rows(tr)]
    if below is not None:
        ins, specs = ins + [below[0], below[1]], specs + [_rows(tr), _whole((1, D))]
        outs, out_specs = outs + [jax.ShapeDtypeStruct((t, D), BF)], out_specs + [_rows(tr)]
    n_vec = 3 if below is None else 4
    return _rowcall(name, fn, ins, specs, len(outs), outs + [_vec()] * n_vec, out_specs + [_whole((1, D))] * n_vec, t // tr,
                    rider=rider)


def _sgu_pre(up, vp, bu, bv, ln_g, ln_b):
    return jax.nn.gelu(up + bu), _layer_norm(jax.nn.gelu(vp + bv), ln_g, ln_b)


def _causal(w_ref, h):
    rows = lax.broadcasted_iota(jnp.int32, (CHUNK, CHUNK), 0)
    cols = lax.broadcasted_iota(jnp.int32, (CHUNK, CHUNK), 1)
    return jnp.where(cols <= rows, w_ref[h], 0.0)


def _sgu(name, proj, b_in, ln_g, ln_b, w_s, bias_full, rider=None):
    t = proj.shape[0]

    def fn(i, r, _):
        u, v = _sgu_pre(r[0][...], r[1][...], r[2][...], r[3][...], r[4][...], r[5][...])
        vb = v.astype(BF)
        mixed = [jnp.dot(_causal(r[6], h).astype(BF), vb[:, h * CHUNK:(h + 1) * CHUNK], preferred_element_type=F32)
                 for h in range(HEADS)]
        return [u * (jnp.concatenate(mixed, axis=1) + r[7][...])]

    return _rowcall(
        name, fn, [proj, proj, b_in, b_in, ln_g, ln_b, w_s, bias_full],
        [_rows(CHUNK, D, 0), _rows(CHUNK, D, 1), pl.BlockSpec((1, D), lambda i: (0, 0)), pl.BlockSpec((1, D), lambda i: (0, 1)),
         _whole((1, D)), _whole((1, D)), _whole((HEADS, CHUNK, CHUNK)), _whole((CHUNK, D))],
        1, [jax.ShapeDtypeStruct((t, D), BF)], [_rows(CHUNK)], t // CHUNK, rider=rider)


def _sgu_bwd(name, proj, b_in, ln_g, ln_b, w_s, bias_full, dout, dproj, rider=None):
    t = proj.shape[0]

    def fn(i, r, _):
        (u, v), vjp = jax.vjp(_sgu_pre, r[0][...], r[1][...], r[2][...], r[3][...], r[4][...], r[5][...])
        vb = v.astype(BF)
        d = r[8][...]
        masks = [_causal(r[6], h).astype(BF) for h in range(HEADS)]
        cols = [slice(h * CHUNK, (h + 1) * CHUNK) for h in range(HEADS)]
        mixed = jnp.concatenate([jnp.dot(masks[h], vb[:, cols[h]], preferred_element_type=F32) for h in range(HEADS)], axis=1)
        du = d * (mixed + r[7][...])
        dmix = d * u
        dmb = dmix.astype(BF)
        dv = jnp.concatenate([lax.dot_general(masks[h], dmb[:, cols[h]], TN, preferred_element_type=F32) for h in range(HEADS)], axis=1)
        rows = lax.broadcasted_iota(jnp.int32, (CHUNK, CHUNK), 0)
        lanes = lax.broadcasted_iota(jnp.int32, (CHUNK, CHUNK), 1)
        dws = jnp.stack([jnp.where(lanes <= rows, lax.dot_general(dmb[:, cols[h]], vb[:, cols[h]], NT, preferred_element_type=F32), 0.0)
                         for h in range(HEADS)])
        dbs = jnp.zeros((CHUNK, CHUNK), F32)
        for h in range(HEADS):
            dbs = dbs + jnp.where(lanes == h, jnp.sum(dmix[:, cols[h]], axis=1, keepdims=True), 0.0)
        dup, dvp, dbu, dbv, dg, db = vjp((du, dv))
        return [jnp.concatenate([dup, dvp], axis=1), dbu, dbv, dg, db, dws, dbs]

    return _rowcall(
        name, fn, [proj, proj, b_in, b_in, ln_g, ln_b, w_s, bias_full, dout, dproj],
        [_rows(CHUNK, D, 0), _rows(CHUNK, D, 1), pl.BlockSpec((1, D), lambda i: (0, 0)), pl.BlockSpec((1, D), lambda i: (0, 1)),
         _whole((1, D)), _whole((1, D)), _whole((HEADS, CHUNK, CHUNK)), _whole((CHUNK, D)), _rows(CHUNK),
         pl.BlockSpec(memory_space=pl.ANY)],
        1, [jax.ShapeDtypeStruct(dproj.shape, dproj.dtype)] + [_vec()] * 4
        + [jax.ShapeDtypeStruct((HEADS, CHUNK, CHUNK), F32), jax.ShapeDtypeStruct((CHUNK, CHUNK), F32)],
        [pl.BlockSpec((CHUNK, 2 * D), lambda i: (i, 0))] + [_whole((1, D))] * 4 + [_whole((HEADS, CHUNK, CHUNK)), _whole((CHUNK, CHUNK))],
        t // CHUNK, rider=rider, aliases={9: 0})


def _halo_before(tr, cb):
    return pl.BlockSpec((HALO, D), lambda i: (jnp.maximum(i * (tr // HALO) - 1, 0), cb))


def _halo_after(tr, cb, n_tiles):
    return pl.BlockSpec((HALO, D), lambda i: (jnp.minimum((i + 1) * (tr // HALO), n_tiles * (tr // HALO) - 1), cb))


def _ln_silu(z, g, b):
    return _silu(_layer_norm(z, g, b))


SUBLANES = 8
LANES = 128
CONV_STRIP = 16
DW_STRIP = 32


def _shifted_copies(buf, copies, rows):
    for b in range(1, SUBLANES):
        copies[b - 1, pl.ds(0, rows), :] = buf[pl.ds(b, rows), :]


def _shifted(buf, copies, offset, start, rows, lanes=slice(None)):
    at = pl.ds(pl.multiple_of(start + SUBLANES * (offset // SUBLANES), SUBLANES), rows)
    return buf[at, lanes] if offset % SUBLANES == 0 else copies[offset % SUBLANES - 1, at, lanes]


def _accumulate(o, v, i):
    @pl.when(i == 0)
    def _():
        o[...] = v.astype(o.dtype)

    @pl.when(i > 0)
    def _():
        o[...] += v.astype(o.dtype)


def _conv(name, proj, b_in, conv_w, conv_b, ln_g, ln_b, rider=None):
    t = proj.shape[0]
    tr = min(t, 256)

    def compute(r, outs, scr):
        zbuf, zs = scr
        i = pl.program_id(0)
        bv, bg = r[4][...], r[5][...]
        z0 = (r[0][...] + bv) * jax.nn.sigmoid(r[1][...] + bg)
        before = (r[2][...] + bv) * jax.nn.sigmoid(r[3][...] + bg)
        zbuf[pl.ds(0, HALO), :] = jnp.where(i > 0, before, 0.0)
        zbuf[pl.ds(HALO, tr), :] = z0
        outs[0][...] = z0
        _shifted_copies(zbuf, zs, tr + HALO - SUBLANES)

        def strip(s, carry):
            r0 = s * CONV_STRIP
            acc = jnp.zeros((CONV_STRIP, D), F32) + r[7][...]
            for k in range(KW):
                acc = acc + r[6][k:k + 1, :] * _shifted(zbuf, zs, HALO - (KW - 1) + k, r0, CONV_STRIP)
            outs[1][pl.ds(pl.multiple_of(r0, SUBLANES), CONV_STRIP), :] = acc
            return carry

        lax.fori_loop(0, tr // CONV_STRIP, strip, 0)
        outs[2][...] = _ln_silu(outs[1][...], r[8][...], r[9][...]).astype(BF)

    return _call(
        name, compute, (t // tr,), [proj, proj, proj, proj, b_in, b_in, conv_w, conv_b, ln_g, ln_b],
        [_rows(tr, D, 2), _rows(tr, D, 3), _halo_before(tr, 2), _halo_before(tr, 3),
         pl.BlockSpec((1, D), lambda i: (0, 2)), pl.BlockSpec((1, D), lambda i: (0, 3)),
         _whole((HALO, D)), _whole((1, D)), _whole((1, D)), _whole((1, D))],
        [jax.ShapeDtypeStruct((t, D), F32), jax.ShapeDtypeStruct((t, D), F32), jax.ShapeDtypeStruct((t, D), BF)],
        [_rows(tr)] * 3, [pltpu.VMEM((tr + HALO, D), F32), pltpu.VMEM((SUBLANES - 1, tr + HALO, D), F32)], ("arbitrary",), rider)


def _conv_bwd(name, proj, b_in, conv_w, ln_g, ln_b, z0, z1, dz3, dproj, rider=None):
    t = proj.shape[0]
    tr = min(t, 256)
    n_tiles = t // tr

    def compute(r, outs, scr):
        zbuf, dbuf, zs, ds, dwacc = scr
        i = pl.program_id(0)
        g, b = r[5][...], r[6][...]
        zero_row = jnp.zeros((1, D), F32)
        _, vjp = jax.vjp(_ln_silu, r[9][...], g, b)
        dz1, dg, db = vjp(r[11][...])
        dcb = jnp.sum(dz1, axis=0, keepdims=True)
        _, vjp_after = jax.vjp(_ln_silu, r[10][...], g, b)
        dz1_after = vjp_after(r[12][...])[0]
        dbuf[pl.ds(0, tr), :] = dz1
        dbuf[pl.ds(tr, HALO), :] = jnp.where(i < n_tiles - 1, dz1_after, 0.0)
        zbuf[pl.ds(0, HALO), :] = jnp.where(i > 0, r[8][...], 0.0)
        zbuf[pl.ds(HALO, tr), :] = r[7][...]
        _shifted_copies(dbuf, ds, tr + HALO - SUBLANES)
        _shifted_copies(zbuf, zs, tr + HALO - SUBLANES)

        def dz0_strip(s, carry):
            r0 = s * CONV_STRIP
            at = pl.ds(pl.multiple_of(r0, CONV_STRIP), CONV_STRIP)
            acc = jnp.zeros((CONV_STRIP, D), F32)
            for k in range(KW):
                acc = acc + r[4][k:k + 1, :] * _shifted(dbuf, ds, KW - 1 - k, r0, CONV_STRIP)
            a = r[0][at, :] + r[2][...]
            sg = jax.nn.sigmoid(r[1][at, :] + r[3][...])
            dcv = acc * sg
            dcg = acc * a * sg * (1.0 - sg)
            outs[0][at, :] = jnp.concatenate([dcv, dcg], axis=1).astype(BF)
            return carry[0] + jnp.sum(dcv, axis=0, keepdims=True), carry[1] + jnp.sum(dcg, axis=0, keepdims=True)

        dbv, dbg = lax.fori_loop(0, tr // CONV_STRIP, dz0_strip, (zero_row, zero_row))

        for lb in range(D // LANES):
            lanes = slice(lb * LANES, (lb + 1) * LANES)

            def dw_strip(s, accs, lanes=lanes):
                r0 = s * DW_STRIP
                dz = dbuf[pl.ds(pl.multiple_of(r0, SUBLANES), DW_STRIP), lanes]
                out = []
                for k in range(KW):
                    prod = dz * _shifted(zbuf, zs, HALO - (KW - 1) + k, r0, DW_STRIP, lanes)
                    part = prod[0:SUBLANES]
                    for q in range(1, DW_STRIP // SUBLANES):
                        part = part + prod[q * SUBLANES:(q + 1) * SUBLANES]
                    out.append(accs[k] + part)
                return tuple(out)

            accs = lax.fori_loop(0, tr // DW_STRIP, dw_strip, tuple(jnp.zeros((SUBLANES, LANES), F32) for _ in range(KW)))
            for k in range(KW):
                dwacc[pl.ds(k * SUBLANES, SUBLANES), lanes] = accs[k]
        dw_rows = [jnp.sum(dwacc[pl.ds(k * SUBLANES, SUBLANES), :], axis=0, keepdims=True) for k in range(KW)]
        dw_rows.append(jnp.zeros((HALO - KW, D), F32))
        for o, v in zip(outs[1:], (dbv, dbg, jnp.concatenate(dw_rows, axis=0), dcb, dg, db)):
            _accumulate(o, v, i)

    wide = pl.BlockSpec((tr, 2 * D), lambda i: (i, 1))
    return _call(
        name, compute, (n_tiles,), [proj, proj, b_in, b_in, conv_w, ln_g, ln_b, z0, z0, z1, z1, dz3, dz3, dproj],
        [_rows(tr, D, 2), _rows(tr, D, 3), pl.BlockSpec((1, D), lambda i: (0, 2)), pl.BlockSpec((1, D), lambda i: (0, 3)),
         _whole((HALO, D)), _whole((1, D)), _whole((1, D)),
         _rows(tr), _halo_before(tr, 0), _rows(tr), _halo_after(tr, 0, n_tiles), _rows(tr), _halo_after(tr, 0, n_tiles),
         pl.BlockSpec(memory_space=pl.ANY)],
        [jax.ShapeDtypeStruct(dproj.shape, dproj.dtype), _vec(), _vec(), _vec(HALO), _vec(), _vec(), _vec()],
        [wide] + [_whole((1, D))] * 2 + [_whole((HALO, D))] + [_whole((1, D))] * 3,
        [pltpu.VMEM((tr + HALO, D), F32), pltpu.VMEM((tr + HALO, D), F32),
         pltpu.VMEM((SUBLANES - 1, tr + HALO, D), F32), pltpu.VMEM((SUBLANES - 1, tr + HALO, D), F32),
         pltpu.VMEM((HALO * SUBLANES, D), F32)],
        ("arbitrary",), rider, aliases={13: 0})


def _merge_fn(ga, gb, bga, bgb, ya, yb):
    return jax.nn.sigmoid(ga + bga) * ya + jax.nn.sigmoid(gb + bgb) * yb


def _mix_tail(name, ua, z3, proj, b_in, wa, wb, wo, x, g, next_norm):
    t = ua.shape[0]
    tr = min(t, 256)

    def compute(r, outs, _):
        ya = jnp.dot(r[0][...], r[6][...], preferred_element_type=F32)
        yb = jnp.dot(r[1][...], r[7][...], preferred_element_type=F32)
        merged = _merge_fn(r[2][...], r[3][...], r[4][...], r[5][...], ya, yb).astype(BF)
        y = jnp.dot(merged, r[8][...], preferred_element_type=F32)
        x_out = r[9][...] + r[10][...] * y
        for o, v in zip(outs, (ya, yb, merged, y, x_out, _rms_mod(x_out, r[11][...], r[12][...], r[13][...]))):
            o[...] = v.astype(o.dtype)

    row = _whole((1, D))
    return _call(
        name, compute, (t // tr,), [ua, z3, proj, proj, b_in, b_in, wa, wb, wo, x, g, *next_norm],
        [_rows(tr), _rows(tr), _rows(tr, D, 4), _rows(tr, D, 5), pl.BlockSpec((1, D), lambda i: (0, 4)),
         pl.BlockSpec((1, D), lambda i: (0, 5)), _whole((D, D)), _whole((D, D)), _whole((D, D)), _rows(tr), row, row, row, row],
        [jax.ShapeDtypeStruct((t, D), dt) for dt in (F32, F32, BF, BF, F32, BF)], [_rows(tr)] * 6, [], ("arbitrary",))


def _mix_tail_bwd(name, dy, proj, b_in, ya, yb, wa, wb, wo, rider=None):
    t = proj.shape[0]
    tr = min(t, 256)

    def compute(r, outs, _):
        i = pl.program_id(0)
        dm = lax.dot_general(r[0][...], r[9][...], NT, preferred_element_type=F32)
        _, vjp = jax.vjp(_merge_fn, *[x[...] for x in r[1:7]])
        dga, dgb, dbga, dbgb, dya, dyb = vjp(dm)
        dya, dyb = dya.astype(BF), dyb.astype(BF)
        outs[0][...] = jnp.concatenate([dga, dgb], axis=1).astype(BF)
        outs[1][...] = dya
        outs[2][...] = dyb
        outs[3][...] = lax.dot_general(dya, r[7][...], NT, preferred_element_type=F32)
        outs[4][...] = lax.dot_general(dyb, r[8][...], NT, preferred_element_type=F32)
        _accumulate(outs[5], dbga, i)
        _accumulate(outs[6], dbgb, i)

    return _call(
        name, compute, (t // tr,), [dy, proj, proj, b_in, b_in, ya, yb, wa, wb, wo],
        [_rows(tr), _rows(tr, D, 4), _rows(tr, D, 5), pl.BlockSpec((1, D), lambda i: (0, 4)), pl.BlockSpec((1, D), lambda i: (0, 5)),
         _rows(tr), _rows(tr), _whole((D, D)), _whole((D, D)), _whole((D, D))],
        [jax.ShapeDtypeStruct((t, D_IN), BF)] + [jax.ShapeDtypeStruct((t, D), BF)] * 2 + [jax.ShapeDtypeStruct((t, D), F32)] * 2
        + [_vec(), _vec()],
        [pl.BlockSpec((tr, 2 * D), lambda i: (i, 2))] + [_rows(tr)] * 4 + [_whole((1, D))] * 2, [], ("arbitrary",), rider)


def _loss_head(name, x, gain, target, f, g, scale):
    t = x.shape[0]
    tr = min(t, 256)

    def loss_fn(xv, gn, tgt):
        y = xv * lax.rsqrt(jnp.mean(xv * xv, axis=-1, keepdims=True) + EPS) * gn
        return 0.5 * jnp.sum(jnp.mean(jnp.square(y - tgt), axis=-1))

    def fn(i, r, _):
        loss, vjp = jax.vjp(loss_fn, r[0][...], r[1][...], r[2][...])
        dx, dgain, _ = vjp(jnp.ones((), F32))
        df, dg = _gate_grads(dx, r[3][...], r[4][...], scale)
        return [dx, df, dgain, jnp.zeros((1, D), F32) + loss, dg]

    return _rowcall(name, fn, [x, gain, target, f, g], [_rows(tr), _whole((1, D)), _rows(tr), _rows(tr), _whole((1, D))], 2,
                    [jax.ShapeDtypeStruct((t, D), F32), jax.ShapeDtypeStruct((t, D), BF), _vec(), _vec(), _vec()],
                    [_rows(tr)] * 2 + [_whole((1, D))] * 3, t // tr)


def _adamw(w, g, m, v):
    m = B1 * m + (1.0 - B1) * g
    v = B2 * v + (1.0 - B2) * jnp.square(g)
    m_hat = m / BC1
    v_hat = v / BC2
    delta = -LR * (m_hat / (jnp.sqrt(v_hat) + ADAM_EPS) + WD * w)
    return delta, m, v


ADAMW_ROWS = 64


def _adamw_group(name, items, rider=None, rows=ADAMW_ROWS):
    ins, in_specs, out_shapes, out_specs, plan = [], [], [], [], []
    first = 0
    for chip_sum, received, w, m, v in items:
        r, c = w.shape
        tr = min(r, rows)
        n = r // tr

        def tile(i, first=first, n=n):
            return jnp.clip(i - first, 0, n - 1)

        spec = pl.BlockSpec((tr, c), lambda i, tile=tile: (tile(i), 0))
        ins += [chip_sum, *received, w, m, v]
        in_specs += [pl.BlockSpec((None, tr, c), lambda i, tile=tile: (0, tile(i), 0))]
        in_specs += [pl.BlockSpec((g.shape[0], tr, c), lambda i, tile=tile: (0, tile(i), 0)) for g in received]
        in_specs += [spec] * 3
        out_shapes += [jax.ShapeDtypeStruct((r, c), F32)] * 4
        out_specs += [spec] * 4
        plan.append((first, n, [g.shape[0] for g in received]))
        first += n

    def compute(in_refs, out_refs, _):
        i = pl.program_id(0)
        at_in = at_out = 0
        for start, n, counts in plan:
            mine = in_refs[at_in:at_in + 4 + len(counts)]
            outs = out_refs[at_out:at_out + 4]
            at_in += 4 + len(counts)
            at_out += 4

            @pl.when(jnp.logical_and(i >= start, i < start + n))
            def _(mine=mine, outs=outs, counts=counts):
                g = mine[0][...].astype(F32)
                for j, count in enumerate(counts):
                    for s in range(count):
                        g = g + mine[1 + j][s].astype(F32)
                delta, m_new, v_new = _adamw(mine[-3][...], g, mine[-2][...], mine[-1][...])
                for o, val in zip(outs, (g, delta, m_new, v_new)):
                    o[...] = val

    res = _call(name, compute, (first,), ins, in_specs, out_shapes, out_specs, [], ("arbitrary",), rider)
    outs, rode = res if rider else (res, [])
    return [outs[4 * j:4 * j + 4] for j in range(len(items))], rode


def _adamw_small(name, packed_all, late_all, dws_all, vectors, w_s):
    n_vec = len(vectors)

    def body(*refs):
        p_ref, l_ref, d_ref = refs[:3]
        param_refs = refs[3:3 + 3 * n_vec + 3]
        out_refs = refs[3 + 3 * n_vec + 3:-1]
        g_ref = refs[-1]
        g = p_ref[0]
        late = l_ref[0]
        for s in range(1, NDEV):
            g = g + p_ref[s]
            late = late + l_ref[s]
        g_ref[...] = g
        g_ref[pl.ds(0, R_LATE), :] += late

        def update(gp, wmv, outs):
            delta, m_new, v_new = _adamw(wmv[0][...], gp, wmv[1][...], wmv[2][...])
            for o, val in zip(outs, (gp, delta, m_new, v_new)):
                o[...] = val

        for j, (row, rows, *_) in enumerate(vectors):
            pieces = [g_ref[pl.ds(row + r, 1), :] for r in range(rows)]
            update(pieces[0] if rows == 1 else jnp.concatenate(pieces, axis=1), param_refs[3 * j:3 * j + 3], out_refs[4 * j:4 * j + 4])
        gw = d_ref[0]
        for s in range(1, NDEV):
            gw = gw + d_ref[s]
        update(gw, param_refs[3 * n_vec:], out_refs[4 * n_vec:4 * n_vec + 4])
        out_refs[-2][...] = g_ref[pl.ds(R_CW, KW), :]
        out_refs[-1][...] = g_ref[pl.ds(R_LOSS, 1), :]

    params = [a for _, _, w, m, v in vectors for a in (w, m, v)] + list(w_s)
    out_shapes = [jax.ShapeDtypeStruct(w.shape, F32) for _, _, w, _, _ in vectors for _ in range(4)]
    out_shapes += [jax.ShapeDtypeStruct(w_s[0].shape, F32)] * 4 + [jax.ShapeDtypeStruct((KW, D), F32), _vec()]
    res = pl.pallas_call(body, name=name, out_shape=out_shapes, scratch_shapes=[pltpu.VMEM((R_TOTAL, D), F32)],
                         compiler_params=_params(None))(packed_all, late_all, dws_all, *params)
    return [res[4 * j:4 * j + 4] for j in range(n_vec + 1)], res[-2], res[-1]


def _adamw_plain(name, g, w, m, v):
    def body(g_ref, w_ref, m_ref, v_ref, d_ref, mo_ref, vo_ref):
        delta, m_new, v_new = _adamw(w_ref[...], g_ref[...], m_ref[...], v_ref[...])
        d_ref[...] = delta
        mo_ref[...] = m_new
        vo_ref[...] = v_new

    return pl.pallas_call(body, name=name, out_shape=[jax.ShapeDtypeStruct(w.shape, F32)] * 3,
                          compiler_params=_params(None))(g, w, m, v)


def _adamw_ada(name, c_all_t, dmod, dmod_late, w, m, v):
    r, c = w.shape
    tr = 256

    def fn(i, refs, _):
        ca = _silu(refs[0][...])
        dm = refs[1][...] + refs[2][...]
        g = ca[:, 0:1] * dm[0:1, :]
        for b in range(1, NDEV):
            g = g + ca[:, b:b + 1] * dm[b:b + 1, :]
        delta, m_new, v_new = _adamw(refs[3][...], g, refs[4][...], refs[5][...])
        return [g, delta, m_new, v_new]

    spec = pl.BlockSpec((tr, c), lambda i: (i, 0))
    whole = pl.BlockSpec((NDEV, c), lambda i: (0, 0))
    return _rowcall(name, fn, [c_all_t, dmod, dmod_late, w, m, v],
                    [pl.BlockSpec((tr, NDEV), lambda i: (i, 0)), whole, whole, spec, spec, spec], 4,
                    [jax.ShapeDtypeStruct((r, c), F32)] * 4, [spec] * 4, r // tr)


def _ffn_fwd(tag, x, h, g, wg, wu, wd_shard, down_rider, next_norm=None, more_shards=(), norm=None):
    t = x.shape[0]
    tm = min(t, 512 if down_rider else 1024)
    (gate, up, act, *normed), (wd, *more) = _ffn_up(f"{tag}_up", x if norm else h, wg, wu,
                                                    rider=_gather_rider([wd_shard, *more_shards]), norm=norm)
    h = normed[0] if norm else h
    row = pl.BlockSpec((1, D), lambda i, j, k: (0, 0))

    def epilogue(f, xv, gv, *norm):
        x_out = xv + 0.5 * gv * f
        return (x_out, f, _rms_mod(x_out, *norm)) if norm else (x_out, f)

    res = _mm_nn(f"{tag}_down", act, wd.reshape(F, D), tm, D, 1024, extras=(x, g, *(next_norm or ())),
                 extra_specs=(pl.BlockSpec((tm, D), lambda i, j, k: (i, 0)), row, *([row] * 3 if next_norm else [])),
                 epilogue=epilogue, out_dtypes=(F32, BF, BF) if next_norm else (F32, BF), rider=down_rider)
    (x_out, f, *h_next), rode = res if down_rider else (res, None)
    return x_out, (h_next[0] if next_norm else None), (x, h, gate, up, act, f), wd, rode, more


def _ffn_bwd(tag, dx_out, df, saved, gain, sh, sc, wg, wu, wd, slots, dact_rider=None, dwd_rider=None, dwgu_rider=None,
             below=None, fuse_dh=False):
    x, h, gate, up, act, f = saved
    t = x.shape[0]
    tm = min(t, 1024)
    if fuse_dh:
        dgate, dup, dh = _ffn_dact_dh(f"{tag}_dact_dh", df, wd.reshape(F, D), gate, up, wg, wu)
        dwd = _mm_tn(f"{tag}_dwd", act, df, 512, D).reshape(NDEV, F // NDEV, D)
        (dwg, dwu), (sib_d,) = _dw_gate_up(f"{tag}_dwgu", h, dgate, dup, rider=_pair_rider([dwd]))
        (sum_d,) = _pair_add(f"{tag}_dwd_add", [dwd], [sib_d], slots)
        normed, (sib_g, sib_u) = _norm_mod_bwd(f"{tag}_norm_bwd", x, gain, sc, sh, dh, dx_out, below=below,
                                               rider=_pair_rider([dwg, dwu]))
        sum_g, sum_u = _pair_add(f"{tag}_dwgu_add", [dwg, dwu], [sib_g, sib_u], slots)
        return normed, (sum_d, None), sum_g, sum_u, [], [], []

    blk = pl.BlockSpec((t, F // NDEV), lambda i, j, k: (i, j))
    res = _mm_nt(f"{tag}_dact", df, wd.reshape(F, D), t, F // NDEV, out_dtypes=(BF, BF),
                 extras=(gate, up), extra_specs=(blk, blk), epilogue=_swiglu_bwd, rider=dact_rider)
    (dgate, dup), rode_dact = res if dact_rider else (res, [])
    res = _mm_tn(f"{tag}_dwd", act, df, 512, D, rider=dwd_rider)
    dwd, rode_dwd = res if dwd_rider else (res, [])
    dwd = dwd.reshape(NDEV, F // NDEV, D)
    (dwg, dwu), (sib_d, *rode_dwgu) = _dw_gate_up(f"{tag}_dwgu", h, dgate, dup,
                                                  rider=[_pair_rider([dwd])] + ([dwgu_rider] if dwgu_rider else []))
    (sum_d,) = _pair_add(f"{tag}_dwd_add", [dwd], [sib_d], slots)
    dh, (sib_g, sib_u, got_d) = _mm_nt_blocked(f"{tag}_dh", [dgate, dup], [wg, wu], tm,
                                               rider=[_pair_rider([dwg, dwu]), _chip_rider([sum_d])])
    sum_g, sum_u = _pair_add(f"{tag}_dwgu_add", [dwg, dwu], [sib_g, sib_u], slots)
    normed = _norm_mod_bwd(f"{tag}_norm_bwd", x, gain, sc, sh, dh, dx_out, below=below)
    return normed, (sum_d, [got_d]), sum_g, sum_u, rode_dact, rode_dwd, rode_dwgu


def kernel(x, c, ada_w, ada_b, norm_ffn1, ffn1_w_gate, ffn1_w_up, ffn1_w_down, norm_mix, mix_w_in, mix_b_in, sgu_ln_g, sgu_ln_b, sgu_w_s, sgu_b_s, conv_w, conv_b, conv_ln_g, conv_ln_b, w_branch_a, w_branch_b, w_out, norm_ffn2, ffn2_w_gate, ffn2_w_up, ffn2_w_down, norm_final, loss_target, m_ada_w, m_ada_b, m_norm_ffn1, m_ffn1_w_gate, m_ffn1_w_up, m_ffn1_w_down, m_norm_mix, m_mix_w_in, m_mix_b_in, m_sgu_ln_g, m_sgu_ln_b, m_sgu_w_s, m_sgu_b_s, m_conv_w, m_conv_b, m_conv_ln_g, m_conv_ln_b, m_w_branch_a, m_w_branch_b, m_w_out, m_norm_ffn2, m_ffn2_w_gate, m_ffn2_w_up, m_ffn2_w_down, m_norm_final, v_ada_w, v_ada_b, v_norm_ffn1, v_ffn1_w_gate, v_ffn1_w_up, v_ffn1_w_down, v_norm_mix, v_mix_w_in, v_mix_b_in, v_sgu_ln_g, v_sgu_ln_b, v_sgu_w_s, v_sgu_b_s, v_conv_w, v_conv_b, v_conv_ln_g, v_conv_ln_b, v_w_branch_a, v_w_branch_b, v_w_out, v_norm_ffn2, v_ffn2_w_gate, v_ffn2_w_up, v_ffn2_w_down, v_norm_final):
    mx, my, mc = _position()
    me = 4 * mx + 2 * my + mc
    chip = 2 * mx + my
    slots = jnp.stack([2 * (chip ^ k) + mc for k in range(N_CHIPS)]).astype(jnp.int32)
    t = x.shape[1]
    tm = min(t, 1024)
    x0 = x.reshape(t, D)
    target = loss_target.reshape(t, D)
    given = dict(ffn1_w_gate=(ffn1_w_gate, m_ffn1_w_gate, v_ffn1_w_gate), ffn1_w_up=(ffn1_w_up, m_ffn1_w_up, v_ffn1_w_up),
                 ffn1_w_down=(ffn1_w_down, m_ffn1_w_down, v_ffn1_w_down), mix_w_in=(mix_w_in, m_mix_w_in, v_mix_w_in),
                 w_branch_a=(w_branch_a, m_w_branch_a, v_w_branch_a), w_branch_b=(w_branch_b, m_w_branch_b, v_w_branch_b),
                 w_out=(w_out, m_w_out, v_w_out), ffn2_w_gate=(ffn2_w_gate, m_ffn2_w_gate, v_ffn2_w_gate),
                 ffn2_w_up=(ffn2_w_up, m_ffn2_w_up, v_ffn2_w_up), ffn2_w_down=(ffn2_w_down, m_ffn2_w_down, v_ffn2_w_down))
    shard = {n: wmv[0][0].astype(BF) for n, wmv in given.items()}

    ada_cols = N_MOD * D // NDEV
    c_all, taps_all, mod_all, (wg1, wu1) = _prologue(
        "prologue", jnp.pad(c, ((0, SUBLANES - 1), (0, 0))), jnp.pad(conv_w[0], ((0, HALO - KW), (0, 0))), ada_w[0],
        lax.dynamic_slice(ada_b, (0, me * ada_cols), (1, ada_cols)), [shard["ffn1_w_gate"], shard["ffn1_w_up"]])
    conv_w_full = jnp.transpose(taps_all.reshape(NDEV, HALO, CHUNK), (1, 0, 2)).reshape(HALO, D)
    mod = lax.dynamic_index_in_dim(mod_all.reshape(NDEV, NDEV, ada_cols), me, axis=1, keepdims=False).reshape(N_MOD, 1, D)
    sh1, sc1, g1, sh2, sc2, g2, sh3, sc3, g3 = [mod[i] for i in range(N_MOD)]

    x1, h2, saved1, wd1, (w_in,), (wa3, wb3) = _ffn_fwd(
        "ffn1", x0, None, g1, wg1, wu1, shard["ffn1_w_down"], _gather_rider([shard["mix_w_in"]]),
        next_norm=(norm_mix, sc2, sh2), more_shards=(shard["w_branch_a"], shard["w_branch_b"]), norm=(norm_ffn1, sc1, sh1))
    proj, (wg2, wo3) = _mm_nn_blocked("mix_in", h2, w_in, tm, rider=_gather_rider([shard["ffn2_w_gate"], shard["w_out"]]))
    bias_full = jnp.repeat(sgu_b_s[0].T, CHUNK, axis=1)
    (ua,) = _sgu("sgu", proj, mix_b_in, sgu_ln_g, sgu_ln_b, sgu_w_s[0], bias_full)
    (z0, z1, z3), (wu2,) = _conv("conv", proj, mix_b_in, conv_w_full, conv_b, conv_ln_g, conv_ln_b,
                                 rider=_gather_rider([shard["ffn2_w_up"]]))
    wa, wb, wo = wa3.reshape(D, D), wb3.reshape(D, D), wo3.reshape(D, D)
    ya, yb, merged, y, x2, h3 = _mix_tail("mix_tail", ua, z3, proj, mix_b_in, wa, wb, wo, x1, g2, (norm_ffn2, sc3, sh3))
    x3, _, saved3, wd2, _, _ = _ffn_fwd("ffn2", x2, h3, g3, wg2, wu2, shard["ffn2_w_down"], None)

    norm_final2 = norm_final.reshape(1, D)
    dx3, df3, d_norm_final, loss_row, dg3 = _loss_head("loss_head", x3, norm_final2, target, saved3[-1], g3, 0.5)
    (dx2, dy, d_norm_ffn2, dsc3, dsh3, dg2), (sum_d2, _), sum_g2, sum_u2, _, _, _ = _ffn_bwd(
        "ffn2", dx3, df3, saved3, norm_ffn2, sh3, sc3, wg2, wu2, wd2, slots, below=(y, g2, 1.0), fuse_dh=True)
    (dproj, dya, dyb, dua, dz3, db_ga, db_gb), (got_g2_near,) = _mix_tail_bwd(
        "mix_tail_bwd", dy, proj, mix_b_in, ya, yb, wa, wb, wo, rider=_chip_rider([sum_g2], NEIGHBOURS))
    dwo, dwa, dwb = [g.reshape(NDEV, D // NDEV, D) for g in _dw_square("mix_dw", [(merged, dy), (ua, dya), (z3, dyb)])]
    (dproj, db_u, db_v, d_sgu_g, d_sgu_b, d_ws, d_bs_t), (*sib_abo, got_g2_far) = _sgu_bwd(
        "sgu_bwd", proj, mix_b_in, sgu_ln_g, sgu_ln_b, sgu_w_s[0], bias_full, dua, dproj,
        rider=[_pair_rider([dwa, dwb, dwo]), _chip_rider([sum_g2], DIAGONAL)])
    sum_a, sum_b, sum_o = _pair_add("mix_dw_add", [dwa, dwb, dwo], sib_abo, slots)
    (dproj, db_cv, db_cg, d_cw, d_cb, d_cln_g, d_cln_b), (got_u2, got_d2) = _conv_bwd(
        "conv_bwd", proj, mix_b_in, conv_w_full, conv_ln_g, conv_ln_b, z0, z1, dz3, dproj, rider=_chip_rider([sum_u2, sum_d2]))
    dwin, (got_a, got_b, got_o) = _mm_tn_blocked("mix_dwin", h2, dproj, rider=_chip_rider([sum_a, sum_b, sum_o]))

    d_bs = jnp.transpose(d_bs_t[:, :HEADS])
    zero = jnp.zeros((1, D), F32)
    pack_rows = [zero, zero, zero, zero, zero, dg2, dsh3, dsc3, dg3,
                 zero, zero, d_norm_ffn2, d_norm_final,
                 db_u, db_v, db_cv, db_cg, db_ga, db_gb,
                 d_sgu_g, d_sgu_b, d_bs.reshape(1, D), d_cb, d_cln_g, d_cln_b,
                 d_cw[:KW], loss_row, jnp.zeros((R_TOTAL - R_LOSS - 1, D), F32)]
    packed = jnp.concatenate(pack_rows, axis=0)
    d_ws2 = d_ws.reshape(HEADS * CHUNK, CHUNK)
    dh2, (sib_in, packed_all, dws_all) = _mm_nt_blocked("mix_in_bwd", [dproj], [w_in], tm,
                                                        rider=[_pair_rider([dwin]), _gather_rider([packed, d_ws2])])
    (sum_in,) = _pair_add("mix_dwin_add", [dwin], [sib_in], slots)
    dx1, df1, d_norm_mix, dsc2, dsh2, dg1 = _norm_mod_bwd("mix_norm_bwd", x1, norm_mix, sc2, sh2, dh2, dx2,
                                                          below=(saved1[-1], g1, 0.5))
    (dx0, d_norm_ffn1, dsc1, dsh1), down1, sum_g1, sum_u1, (got_in_near,), _, (got_in_far,) = _ffn_bwd(
        "ffn1", dx1, df1, saved1, norm_ffn1, sh1, sc1, wg1, wu1, wd1, slots,
        dact_rider=_chip_rider([sum_in], NEIGHBOURS), dwgu_rider=_chip_rider([sum_in], DIAGONAL))
    packed_late = jnp.concatenate([dsh1, dsc1, dg1, dsh2, dsc2, jnp.zeros((4, D), F32), d_norm_ffn1, d_norm_mix,
                                   jnp.zeros((R_LATE - 11, D), F32)], axis=0)
    grads = dict(ffn2_w_gate=(sum_g2, [got_g2_near, got_g2_far]), ffn2_w_up=(sum_u2, [got_u2]), ffn2_w_down=(sum_d2, [got_d2]),
                 mix_w_in=(sum_in, [got_in_near, got_in_far]), w_branch_a=(sum_a, [got_a]), w_branch_b=(sum_b, [got_b]),
                 w_out=(sum_o, [got_o]), ffn1_w_down=down1)
    done, (late_all, got_g1, got_u1) = _adamw_group(
        "adamw_most", [(cs, got, *[a[0] for a in given[n]]) for n, (cs, got) in grads.items()],
        rider=[_gather_rider([packed_late]), _chip_rider([sum_g1, sum_u1])])
    last, _ = _adamw_group("adamw_ffn1_in", [(sum_g1, [got_g1], *[a[0] for a in given["ffn1_w_gate"]]),
                                            (sum_u1, [got_u1], *[a[0] for a in given["ffn1_w_up"]])], rows=256)
    big_out = {n: [o.reshape(given[n][0].shape) for o in outs]
               for n, outs in zip([*grads, "ffn1_w_gate", "ffn1_w_up"], [*done, *last])}

    flat = lambda a: a.reshape(1, -1)
    vectors = [("ada_b", 0, 9, ada_b, m_ada_b, v_ada_b), ("norm_ffn1", 9, 1, norm_ffn1, m_norm_ffn1, v_norm_ffn1),
               ("norm_mix", 10, 1, norm_mix, m_norm_mix, v_norm_mix), ("norm_ffn2", 11, 1, norm_ffn2, m_norm_ffn2, v_norm_ffn2),
               ("norm_final", 12, 1, norm_final, m_norm_final, v_norm_final), ("mix_b_in", 13, 6, mix_b_in, m_mix_b_in, v_mix_b_in),
               ("sgu_ln_g", 19, 1, sgu_ln_g, m_sgu_ln_g, v_sgu_ln_g), ("sgu_ln_b", 20, 1, sgu_ln_b, m_sgu_ln_b, v_sgu_ln_b),
               ("sgu_b_s", 21, 1, sgu_b_s, m_sgu_b_s, v_sgu_b_s), ("conv_b", 22, 1, conv_b, m_conv_b, v_conv_b),
               ("conv_ln_g", 23, 1, conv_ln_g, m_conv_ln_g, v_conv_ln_g), ("conv_ln_b", 24, 1, conv_ln_b, m_conv_ln_b, v_conv_ln_b)]
    small_out, d_cw_all, loss_sum = _adamw_small(
        "adamw_small", packed_all, late_all, dws_all, [(row, rows, flat(wv), flat(mv), flat(vv)) for _, row, rows, wv, mv, vv in vectors],
        [a.reshape(HEADS * CHUNK, CHUNK) for a in (sgu_w_s, m_sgu_w_s, v_sgu_w_s)])
    small = {n: [o.reshape(wv.shape) for o in outs] for (n, _, _, wv, _, _), outs in zip(vectors, small_out)}
    small["sgu_w_s"] = [o.reshape(sgu_w_s.shape) for o in small_out[-1]]
    g_cw = lax.dynamic_slice(d_cw_all, (0, me * CHUNK), (KW, CHUNK))
    small["conv_w"] = [o.reshape(conv_w.shape) for o in (g_cw, *_adamw_plain("adamw_conv_w", g_cw, conv_w[0], m_conv_w[0], v_conv_w[0]))]
    loss = loss_sum[0, 0]

    dmod_cols = [lax.dynamic_slice(a[:, :N_MOD, :].reshape(NDEV, N_MOD * D), (0, me * ada_cols), (NDEV, ada_cols))
                 for a in (packed_all, late_all)]
    ada_out = [o.reshape(ada_w.shape) for o in _adamw_ada("adamw_ada_w", jnp.transpose(c_all), *dmod_cols, ada_w[0], m_ada_w[0], v_ada_w[0])]

    order = ["ada_w", "ada_b", "norm_ffn1", "ffn1_w_gate", "ffn1_w_up", "ffn1_w_down", "norm_mix", "mix_w_in", "mix_b_in",
             "sgu_ln_g", "sgu_ln_b", "sgu_w_s", "sgu_b_s", "conv_w", "conv_b", "conv_ln_g", "conv_ln_b", "w_branch_a",
             "w_branch_b", "w_out", "norm_ffn2", "ffn2_w_gate", "ffn2_w_up", "ffn2_w_down", "norm_final"]

    def leaf(n, kind):
        if n == "ada_w":
            return ada_out[kind]
        if n in big_out:
            return big_out[n][kind]
        return small[n][kind]

    return (loss, dx0.reshape(x.shape), *[leaf(n, kind) for kind in range(4) for n in order])
```

```python
import jax
import jax.numpy as jnp
from jax import lax
from jax.experimental import pallas as pl
from jax.experimental.pallas import tpu as pltpu

D = 1024
F = 4 * D
D_IN = 6 * D
HEADS = 8
CHUNK = 128
KW = 31
HALO = 32
N_MOD = 9
NDEV = 8
N_CHIPS = 4
EPS = 1e-6
LR, B1, B2, ADAM_EPS, WD, STEP = 0.001, 0.9, 0.999, 1e-08, 0.01, 10
BC1 = 1.0 - B1 ** STEP
BC2 = 1.0 - B2 ** STEP
VMEM_LIMIT = 56 * 1024 * 1024
MESH = pl.DeviceIdType.MESH
HBM = pl.BlockSpec(memory_space=pltpu.HBM)
VMEM = pl.BlockSpec(memory_space=pltpu.VMEM)
BF = jnp.bfloat16
F32 = jnp.float32

NN = (((1,), (0,)), ((), ()))
NT = (((1,), (1,)), ((), ()))
TN = (((0,), (0,)), ((), ()))

R_CW, R_LOSS, R_TOTAL = 25, 56, 64
R_LATE = 16


def _params(sem):
    return pltpu.CompilerParams(dimension_semantics=sem, vmem_limit_bytes=VMEM_LIMIT)


def _position():
    return lax.axis_index("x"), lax.axis_index("y"), lax.axis_index("c")


def _flip(pos, k):
    x, y, c = pos
    return (x ^ (k >> 2 & 1), y ^ (k >> 1 & 1), c ^ (k & 1))


def _index(pos):
    return 4 * pos[0] + 2 * pos[1] + pos[2]


def _gather_rows(x_ref, out_ref, send_sems, recv_sems, local_sem):
    m_per = x_ref.shape[0]
    x, y, c = _position()
    me, sibling = (x, y, c), (x, y, 1 - c)
    chips = [(1 - x, y), (x, 1 - y), (1 - x, 1 - y)]

    def rows(pos):
        return out_ref.at[pl.ds(_index(pos) * m_per, m_per), :]

    def copy(k, block, to, src=None):
        return pltpu.make_async_remote_copy(
            src_ref=rows(block) if src is None else src, dst_ref=rows(block),
            send_sem=send_sems.at[k], recv_sem=recv_sems.at[k], device_id=to, device_id_type=MESH)

    mine = pltpu.make_async_copy(x_ref, rows(me), local_sem)
    mine.start()
    first = [copy(0, me, sibling, src=x_ref)]
    first += [copy(1 + j, me, (*chip, c), src=x_ref) for j, chip in enumerate(chips)]
    for cp in first:
        cp.start()
    passed = [copy(4 + j, (*chip, c), sibling) for j, chip in enumerate(chips)]
    for j, chip in enumerate(chips):
        copy(1 + j, (*chip, c), me).wait_recv()
        passed[j].start()
    copy(0, sibling, me).wait_recv()
    for j, chip in enumerate(chips):
        copy(4 + j, (*chip, 1 - c), me).wait_recv()
    for cp in first + passed:
        cp.wait_send()
    mine.wait()


def _prologue(name, c_rows, taps, ada_w, ada_b, shards):
    rider = _gather_rider(shards)
    n = len(shards)
    nc = ada_w.shape[1]

    def body(*refs):
        c_ref, taps_ref, w_ref, b_ref = refs[:4]
        shard_refs = refs[4:4 + n]
        c_all_ref, taps_all_ref, mod_all_ref = refs[4 + n:7 + n]
        gathered_refs = refs[7 + n:7 + 2 * n]
        c_buf, mod_part, sems = refs[7 + 2 * n], refs[8 + 2 * n], refs[9 + 2 * n:]
        rider.start(shard_refs, gathered_refs, sems[9:])
        _gather_rows(c_ref, c_buf, *sems[0:3])
        c_all = jnp.concatenate([c_buf[pl.ds(d * SUBLANES, 1), :] for d in range(NDEV)], axis=0)
        c_all_ref[...] = c_all
        mod_part[...] = jnp.dot(_silu(c_all), w_ref[...], preferred_element_type=F32) + b_ref[...]
        _gather_rows(taps_ref, taps_all_ref, *sems[3:6])
        _gather_rows(mod_part, mod_all_ref, *sems[6:9])
        rider.mid(shard_refs, gathered_refs, sems[9:])
        rider.relay(shard_refs, gathered_refs, sems[9:])
        rider.finish(shard_refs, gathered_refs, sems[9:])

    small_sems = [pltpu.SemaphoreType.DMA((7,)), pltpu.SemaphoreType.DMA((7,)), pltpu.SemaphoreType.DMA] * 3
    res = pl.pallas_call(
        body, name=name,
        out_shape=[jax.ShapeDtypeStruct((NDEV, D), F32), jax.ShapeDtypeStruct((NDEV * taps.shape[0], taps.shape[1]), F32),
                   jax.ShapeDtypeStruct((NDEV * NDEV, nc), F32)] + rider.out_shapes,
        in_specs=[VMEM] * 4 + [HBM] * n, out_specs=[VMEM] * 3 + [HBM] * n,
        scratch_shapes=[pltpu.VMEM((NDEV * SUBLANES, D), F32), pltpu.VMEM((NDEV, nc), F32)] + small_sems + rider.sems,
        compiler_params=_params(None),
    )(c_rows, taps, ada_w, ada_b, *shards)
    return res[0], res[1], res[2], res[3:]


class _Rider:
    def __init__(self, ins, out_shapes, sems, start, finish, mid=None, relay=None):
        self.ins, self.out_shapes, self.sems = list(ins), list(out_shapes), list(sems)
        self.start, self.finish, self.mid, self.relay = start, finish, mid, relay


def _gather_rider(shards):
    n = len(shards)

    def setup(ins, outs, sems):
        send_sems, recv_sems, local_sems = sems
        x, y, c = _position()
        places = dict(me=(x, y, c), sibling=(x, y, 1 - c), xn=(1 - x, y, c), yn=(x, 1 - y, c), diagonal=(1 - x, 1 - y, c),
                      passed_on=(x ^ c, y ^ (1 - c), c), passed_to=(x ^ (1 - c), y ^ c, c))

        def copy(a, k, block, to, own=False):
            slot = outs[a].at[_index(block)]
            return pltpu.make_async_remote_copy(
                src_ref=ins[a] if own else slot, dst_ref=slot,
                send_sem=send_sems.at[k, a], recv_sem=recv_sems.at[k, a], device_id=to, device_id_type=MESH)

        def local(a):
            return pltpu.make_async_copy(ins[a], outs[a].at[_index(places["me"])], local_sems.at[a])

        return places, copy, local

    def start(ins, outs, sems):
        p, copy, local = setup(ins, outs, sems)
        for a in range(n):
            local(a).start()
            for k, to in enumerate(("sibling", "xn", "yn")):
                copy(a, k, p["me"], p[to], own=True).start()

    def mid(ins, outs, sems):
        p, copy, _ = setup(ins, outs, sems)
        for a in range(n):
            copy(a, 1, p["xn"], p["me"]).wait_recv()
            copy(a, 2, p["yn"], p["me"]).wait_recv()
            copy(a, 3, p["passed_on"], p["passed_to"]).start()
            copy(a, 4, p["xn"], p["sibling"]).start()
            copy(a, 5, p["yn"], p["sibling"]).start()

    def relay(ins, outs, sems):
        p, copy, _ = setup(ins, outs, sems)
        for a in range(n):
            copy(a, 3, p["diagonal"], p["me"]).wait_recv()
            copy(a, 6, p["diagonal"], p["sibling"]).start()

    def finish(ins, outs, sems):
        p, copy, local = setup(ins, outs, sems)
        x, y, c = p["me"]
        for a in range(n):
            for k, block in ((0, (x, y, 1 - c)), (4, (1 - x, y, 1 - c)), (5, (x, 1 - y, 1 - c)), (6, (1 - x, 1 - y, 1 - c))):
                copy(a, k, block, p["me"]).wait_recv()
            for k, to in enumerate(("sibling", "xn", "yn")):
                copy(a, k, p["me"], p[to], own=True).wait_send()
            copy(a, 3, p["passed_on"], p["passed_to"]).wait_send()
            for k, block in ((4, "xn"), (5, "yn"), (6, "diagonal")):
                copy(a, k, p[block], p["sibling"]).wait_send()
            local(a).wait()

    return _Rider(shards, [jax.ShapeDtypeStruct((NDEV, *s.shape), s.dtype) for s in shards],
                  [pltpu.SemaphoreType.DMA((7, n)), pltpu.SemaphoreType.DMA((7, n)), pltpu.SemaphoreType.DMA((n,))],
                  start, finish, mid, relay)


def _pair_rider(parts):
    n = len(parts)

    def copies(ins, outs, sems):
        send_sems, recv_sems = sems
        x, y, c = _position()
        q = 2 * x + y
        return [pltpu.make_async_remote_copy(
            src_ref=ins[a].at[2 * (q ^ k) + (1 - c)], dst_ref=outs[a].at[k],
            send_sem=send_sems.at[k, a], recv_sem=recv_sems.at[k, a], device_id=(x, y, 1 - c), device_id_type=MESH)
            for a in range(n) for k in range(N_CHIPS)]

    def start(ins, outs, sems):
        for cp in copies(ins, outs, sems):
            cp.start()

    def finish(ins, outs, sems):
        for cp in copies(ins, outs, sems):
            cp.wait()

    return _Rider(parts, [jax.ShapeDtypeStruct((N_CHIPS, *p.shape[1:]), p.dtype) for p in parts],
                  [pltpu.SemaphoreType.DMA((N_CHIPS, n)), pltpu.SemaphoreType.DMA((N_CHIPS, n))], start, finish)


NEIGHBOURS = (1, 2)
DIAGONAL = (3,)
OTHER_CHIPS = NEIGHBOURS + DIAGONAL


def _chip_rider(sums, ks=OTHER_CHIPS):
    n = len(sums)

    def copies(ins, outs, sems):
        send_sems, recv_sems = sems
        me = _position()
        return [pltpu.make_async_remote_copy(
            src_ref=ins[a].at[k], dst_ref=outs[a].at[j],
            send_sem=send_sems.at[j, a], recv_sem=recv_sems.at[j, a], device_id=_flip(me, 2 * k), device_id_type=MESH)
            for a in range(n) for j, k in enumerate(ks)]

    def start(ins, outs, sems):
        for cp in copies(ins, outs, sems):
            cp.start()

    def finish(ins, outs, sems):
        for cp in copies(ins, outs, sems):
            cp.wait()

    return _Rider(sums, [jax.ShapeDtypeStruct((len(ks), *s.shape[1:]), s.dtype) for s in sums],
                  [pltpu.SemaphoreType.DMA((len(ks), n)), pltpu.SemaphoreType.DMA((len(ks), n))], start, finish)


def _grid_edge(grid, last):
    cond = None
    for d, n in enumerate(grid):
        here = pl.program_id(d) == (n - 1 if last else 0)
        cond = here if cond is None else jnp.logical_and(cond, here)
    return cond


def _call(name, compute, grid, ins, in_specs, out_shapes, out_specs, scratch_shapes, semantics, rider=None, aliases=None):
    riders = [rider] if isinstance(rider, _Rider) else list(rider or [])
    n_in, n_out, n_scr = len(ins), len(out_shapes), len(scratch_shapes)
    n_rin, n_rout, n_rsem = [sum(len(part(r)) for r in riders) for part in (lambda r: r.ins, lambda r: r.out_shapes, lambda r: r.sems)]
    cuts = [0, n_in, n_in + n_rin, n_in + n_rin + n_out, n_in + n_rin + n_out + n_rout, n_in + n_rin + n_out + n_rout + n_scr]

    def body(*refs):
        in_refs, rin_refs, out_refs, rout_refs, scr_refs = [refs[a:b] for a, b in zip(cuts[:-1], cuts[1:])]
        rsem_refs = refs[cuts[-1]:]
        mine, at = [], [0, 0, 0]
        for r in riders:
            mine.append((r, rin_refs[at[0]:at[0] + len(r.ins)], rout_refs[at[1]:at[1] + len(r.out_shapes)],
                         rsem_refs[at[2]:at[2] + len(r.sems)]))
            at = [at[0] + len(r.ins), at[1] + len(r.out_shapes), at[2] + len(r.sems)]
        if riders:
            @pl.when(_grid_edge(grid, last=False))
            def _():
                for r, a, b, c in mine:
                    r.start(a, b, c)

        if any(r.mid for r in riders):
            step, steps = 0, 1
            for d, size in enumerate(grid):
                step, steps = step * size + pl.program_id(d), steps * size

            @pl.when(step == steps * 5 // 8)
            def _():
                for r, a, b, c in mine:
                    if r.mid:
                        r.mid(a, b, c)

        if any(r.relay for r in riders):
            @pl.when(_grid_edge(grid, last=True))
            def _():
                for r, a, b, c in mine:
                    if r.relay:
                        r.relay(a, b, c)

        compute(in_refs, out_refs, scr_refs)
        if riders:
            @pl.when(_grid_edge(grid, last=True))
            def _():
                for r, a, b, c in mine:
                    r.finish(a, b, c)

    res = pl.pallas_call(
        body, name=name, grid=grid,
        out_shape=list(out_shapes) + [s for r in riders for s in r.out_shapes],
        in_specs=list(in_specs) + [HBM] * n_rin, out_specs=list(out_specs) + [HBM] * n_rout,
        scratch_shapes=list(scratch_shapes) + [s for r in riders for s in r.sems],
        input_output_aliases=aliases or {}, compiler_params=_params(semantics),
    )(*ins, *[a for r in riders for a in r.ins])
    return (res[:n_out], res[n_out:]) if riders else res


def _pair_add(name, parts, from_sibling, slots):
    n = len(parts)

    def body(s_ref, *refs):
        for a in range(n):
            refs[2 * n + a][...] = (refs[a][...].astype(F32) + refs[n + a][...].astype(F32)).astype(refs[2 * n + a].dtype)

    def slab(p, picked):
        _, r, c = p.shape
        return pl.BlockSpec((None, r, c), (lambda k, s: (s[k], 0, 0)) if picked else (lambda k, s: (k, 0, 0)))

    return pl.pallas_call(
        body, name=name,
        grid_spec=pltpu.PrefetchScalarGridSpec(
            num_scalar_prefetch=1, grid=(N_CHIPS,),
            in_specs=[slab(p, True) for p in parts] + [slab(p, False) for p in parts],
            out_specs=[slab(p, False) for p in parts]),
        out_shape=[jax.ShapeDtypeStruct((N_CHIPS, *p.shape[1:]), p.dtype) for p in parts],
        compiler_params=_params(("arbitrary",)),
    )(slots, *parts, *from_sibling)


def _mm(name, pairs, dims, grid, nk, out_shapes, out_specs, extras=(), extra_specs=(), epilogue=None, acc_shape=None, rider=None):
    n_pairs = len(pairs)

    def compute(ins, outs, scratch):
        def partial_sum():
            total = None
            for p in range(n_pairs):
                d = lax.dot_general(ins[2 * p][...], ins[2 * p + 1][...], dims, preferred_element_type=F32)
                total = d if total is None else total + d
            return total

        def finish(r):
            ex = [e[...] for e in ins[2 * n_pairs:]]
            res = epilogue(r, *ex) if epilogue is not None else (r,)
            for o, v in zip(outs, res):
                o[...] = v.astype(o.dtype)

        if nk == 1:
            finish(partial_sum())
        else:
            acc = scratch[0]
            k = pl.program_id(2)

            @pl.when(k == 0)
            def _():
                acc[...] = partial_sum()

            @pl.when(k > 0)
            def _():
                acc[...] += partial_sum()

            @pl.when(k == nk - 1)
            def _():
                finish(acc[...])

    operands, specs = [], []
    for a, a_spec, b, b_spec in pairs:
        operands += [a, b]
        specs += [a_spec, b_spec]
    return _call(name, compute, grid, operands + list(extras), specs + list(extra_specs), out_shapes, out_specs,
                 [pltpu.VMEM(acc_shape, F32)] if nk > 1 else [], ("parallel", "parallel", "arbitrary"), rider)


def _single(res, rider):
    return (res[0][0], res[1]) if rider else res[0]


def _silu(x):
    return x * jax.nn.sigmoid(x)


def _ffn_up(name, h, wg, wu, rider=None, norm=None):
    t = h.shape[0]
    tm = min(t, 1024)
    nb = F // NDEV

    def compute(ins, outs, scr):
        if norm:
            @pl.when(pl.program_id(1) == 0)
            def _():
                scr[0][...] = _rms_mod(ins[0][...], ins[3][...], ins[4][...], ins[5][...]).astype(BF)
                outs[3][...] = scr[0][...]

            hv = scr[0][...]
        else:
            hv = ins[0][...]
        g = jnp.dot(hv, ins[1][...], preferred_element_type=F32)
        u = jnp.dot(hv, ins[2][...], preferred_element_type=F32)
        outs[0][...] = g.astype(BF)
        outs[1][...] = u.astype(BF)
        outs[2][...] = (_silu(g) * u).astype(BF)

    w_spec = pl.BlockSpec((None, D, nb), lambda i, j: (j, 0, 0))
    o_spec = pl.BlockSpec((tm, nb), lambda i, j: (i, j))
    rows = pl.BlockSpec((tm, D), lambda i, j: (i, 0))
    vec = pl.BlockSpec((1, D), lambda i, j: (0, 0))
    return _call(name, compute, (t // tm, NDEV), [h, wg, wu, *(norm or ())], [rows, w_spec, w_spec] + [vec] * (3 if norm else 0),
                 [jax.ShapeDtypeStruct((t, F), BF)] * 3 + ([jax.ShapeDtypeStruct((t, D), BF)] if norm else []),
                 [o_spec] * 3 + ([rows] if norm else []), [pltpu.VMEM((tm, D), BF)] if norm else [],
                 ("parallel", "arbitrary"), rider)


def _mm_nn(name, a, b, tm, tn, tk, extras=(), extra_specs=(), epilogue=None, out_dtypes=(F32,), rider=None):
    m, kk = a.shape
    n = b.shape[1]
    nk = kk // tk
    return _mm(
        name, [(a, pl.BlockSpec((tm, tk), lambda i, j, k: (i, k)), b, pl.BlockSpec((tk, tn), lambda i, j, k: (k, j)))], NN,
        (m // tm, n // tn, nk), nk,
        [jax.ShapeDtypeStruct((m, n), dt) for dt in out_dtypes],
        [pl.BlockSpec((tm, tn), lambda i, j, k: (i, j))] * len(out_dtypes),
        extras, extra_specs, epilogue, (tm, tn), rider)


def _mm_nn_blocked(name, a, b3, tm, rider=None):
    m = a.shape[0]
    nb = b3.shape[2]
    return _single(_mm(
        name, [(a, pl.BlockSpec((tm, D), lambda i, j, k: (i, 0)), b3, pl.BlockSpec((None, D, nb), lambda i, j, k: (j, 0, 0)))], NN,
        (m // tm, NDEV, 1), 1,
        [jax.ShapeDtypeStruct((m, NDEV * nb), F32)], [pl.BlockSpec((tm, nb), lambda i, j, k: (i, j))], rider=rider), rider)


def _mm_nt(name, a, b, tm, tn, out_dtypes=(F32,), extras=(), extra_specs=(), epilogue=None, rider=None):
    m, kk = a.shape
    n = b.shape[0]
    return _mm(
        name, [(a, pl.BlockSpec((tm, kk), lambda i, j, k: (i, 0)), b, pl.BlockSpec((tn, kk), lambda i, j, k: (j, 0)))], NT,
        (m // tm, n // tn, 1), 1,
        [jax.ShapeDtypeStruct((m, n), dt) for dt in out_dtypes],
        [pl.BlockSpec((tm, tn), lambda i, j, k: (i, j))] * len(out_dtypes),
        extras, extra_specs, epilogue, rider=rider)


def _mm_nt_blocked(name, a_list, b3_list, tm, rider=None):
    m = a_list[0].shape[0]
    nb = b3_list[0].shape[2]
    pairs = [(a, pl.BlockSpec((tm, nb), lambda i, j, k: (i, k)), b3, pl.BlockSpec((None, D, nb), lambda i, j, k: (k, 0, 0)))
             for a, b3 in zip(a_list, b3_list)]
    return _single(_mm(name, pairs, NT, (m // tm, 1, NDEV), NDEV,
                       [jax.ShapeDtypeStruct((m, D), F32)], [pl.BlockSpec((tm, D), lambda i, j, k: (i, 0))],
                       acc_shape=(tm, D), rider=rider), rider)


def _mm_tn(name, a, b, tm, tn, rider=None):
    t, m = a.shape
    n = b.shape[1]
    return _single(_mm(
        name, [(a, pl.BlockSpec((t, tm), lambda i, j, k: (0, i)), b, pl.BlockSpec((t, tn), lambda i, j, k: (0, j)))], TN,
        (m // tm, n // tn, 1), 1,
        [jax.ShapeDtypeStruct((m, n), BF)], [pl.BlockSpec((tm, tn), lambda i, j, k: (i, j))], rider=rider), rider)


def _mm_tn_blocked(name, a, b, rider=None):
    t = a.shape[0]
    nb = b.shape[1] // NDEV
    return _single(_mm(
        name, [(a, pl.BlockSpec((t, D), lambda i, j, k: (0, 0)), b, pl.BlockSpec((t, nb), lambda i, j, k: (0, j)))], TN,
        (1, NDEV, 1), 1,
        [jax.ShapeDtypeStruct((NDEV, D, nb), BF)], [pl.BlockSpec((None, D, nb), lambda i, j, k: (j, 0, 0))], rider=rider), rider)


def _dw_gate_up(name, h, dgate, dup, rider=None):
    t = h.shape[0]
    nb = F // NDEV

    def compute(ins, outs, _):
        hv = ins[0][...]
        outs[0][...] = lax.dot_general(hv, ins[1][...], TN, preferred_element_type=F32).astype(BF)
        outs[1][...] = lax.dot_general(hv, ins[2][...], TN, preferred_element_type=F32).astype(BF)

    d_spec = pl.BlockSpec((t, nb), lambda j: (0, j))
    o_spec = pl.BlockSpec((None, D, nb), lambda j: (j, 0, 0))
    return _call(name, compute, (NDEV,), [h, dgate, dup], [pl.BlockSpec((t, D), lambda j: (0, 0)), d_spec, d_spec],
                 [jax.ShapeDtypeStruct((NDEV, D, nb), BF)] * 2, [o_spec] * 2, [], ("arbitrary",), rider)


def _dw_square(name, pairs):
    t = pairs[0][0].shape[0]
    tm = 512
    n = len(pairs)

    def compute(ins, outs, _):
        for p in range(n):
            outs[p][...] = lax.dot_general(ins[2 * p][...], ins[2 * p + 1][...], TN, preferred_element_type=F32).astype(BF)

    return _call(name, compute, (D // tm,), [x for pair in pairs for x in pair],
                 [pl.BlockSpec((t, tm), lambda i: (0, i)), pl.BlockSpec((t, D), lambda i: (0, 0))] * n,
                 [jax.ShapeDtypeStruct((D, D), BF)] * n, [pl.BlockSpec((tm, D), lambda i: (i, 0))] * n, [], ("arbitrary",))


def _swiglu_bwd(da, gate, up):
    gate = gate.astype(F32)
    s = jax.nn.sigmoid(gate)
    return da * up.astype(F32) * (s * (1.0 + gate * (1.0 - s))), da * (gate * s)


def _ffn_dact_dh(name, df, wd, gate, up, wg, wu, rider=None):
    t = df.shape[0]
    tm = min(t, 1024)
    nb = F // NDEV

    def compute(ins, outs, scr):
        acc = scr[0]
        j = pl.program_id(1)
        da = lax.dot_general(ins[0][...], ins[1][...], NT, preferred_element_type=F32)
        dgate, dup = _swiglu_bwd(da, ins[2][...], ins[3][...])
        dgate, dup = dgate.astype(BF), dup.astype(BF)
        outs[0][...] = dgate
        outs[1][...] = dup
        part = (lax.dot_general(dgate, ins[4][...], NT, preferred_element_type=F32)
                + lax.dot_general(dup, ins[5][...], NT, preferred_element_type=F32))

        @pl.when(j == 0)
        def _():
            acc[...] = part

        @pl.when(j > 0)
        def _():
            acc[...] += part

        @pl.when(j == NDEV - 1)
        def _():
            outs[2][...] = acc[...]

    blk = pl.BlockSpec((tm, nb), lambda i, j: (i, j))
    w3 = pl.BlockSpec((None, D, nb), lambda i, j: (j, 0, 0))
    row = pl.BlockSpec((tm, D), lambda i, j: (i, 0))
    return _call(name, compute, (t // tm, NDEV), [df, wd, gate, up, wg, wu],
                 [row, pl.BlockSpec((nb, D), lambda i, j: (j, 0)), blk, blk, w3, w3],
                 [jax.ShapeDtypeStruct((t, F), BF)] * 2 + [jax.ShapeDtypeStruct((t, D), F32)], [blk, blk, row],
                 [pltpu.VMEM((tm, D), F32)], ("parallel", "arbitrary"), rider)


def _rowcall(name, fn, ins, in_specs, n_row_out, out_shapes, out_specs, grid, scratch_shapes=(), rider=None, aliases=None):
    def accumulate(o, v, i):
        @pl.when(i == 0)
        def _():
            o[...] = v.astype(o.dtype)

        @pl.when(i > 0)
        def _():
            o[...] += v.astype(o.dtype)

    def compute(in_refs, out_refs, scr):
        i = pl.program_id(0)
        vals = fn(i, in_refs, scr)
        for idx, (o, v) in enumerate(zip(out_refs, vals)):
            if idx < n_row_out:
                o[...] = v.astype(o.dtype)
            else:
                accumulate(o, v, i)

    return _call(name, compute, (grid,), ins, in_specs, out_shapes, out_specs, list(scratch_shapes), ("arbitrary",), rider, aliases)


def _rows(tr, w=D, cb=0):
    return pl.BlockSpec((tr, w), lambda i: (i, cb))


def _whole(shape):
    nd = len(shape)
    return pl.BlockSpec(shape, lambda i: (0,) * nd)


def _vec(n=1):
    return jax.ShapeDtypeStruct((n, D), F32)


def _rms_mod(x, gain, sc, sh):
    y = x * lax.rsqrt(jnp.mean(x * x, axis=-1, keepdims=True) + EPS)
    return (y * gain) * (1.0 + sc) + sh


def _layer_norm(x, g, b):
    mu = jnp.mean(x, axis=-1, keepdims=True)
    var = jnp.mean(jnp.square(x - mu), axis=-1, keepdims=True)
    return (x - mu) * lax.rsqrt(var + EPS) * g + b


def _gate_grads(dx, f, g, scale):
    return scale * g * dx, jnp.sum(scale * dx * f.astype(F32), axis=0, keepdims=True)


def _norm_mod_bwd(name, x, gain, sc, sh, dh, dres, below=None, rider=None):
    t = x.shape[0]
    tr = min(t, 512)

    def fn(i, r, _):
        _, vjp = jax.vjp(_rms_mod, r[0][...], r[1][...], r[2][...], r[3][...])
        dx, dgain, dsc, dsh = vjp(r[4][...])
        dx = dx + r[5][...]
        if below is None:
            return [dx, dgain, dsc, dsh]
        df, dg = _gate_grads(dx, r[6][...], r[7][...], below[2])
        return [dx, df, dgain, dsc, dsh, dg]

    ins, specs = [x, gain, sc, sh, dh, dres], [_rows(tr)] + [_whole((1, D))] * 3 + [_rows(tr)] * 2
    outs, out_specs = [jax.ShapeDtypeStruct((t, D), F32)], [_rows(tr)]
    if below is not None:
        ins, specs = ins + [below[0], below[1]], specs + [_rows(tr), _whole((1, D))]
        outs, out_specs = outs + [jax.ShapeDtypeStruct((t, D), BF)], out_specs + [_rows(tr)]
    n_vec = 3 if below is None else 4
    return _rowcall(name, fn, ins, specs, len(outs), outs + [_vec()] * n_vec, out_specs + [_whole((1, D))] * n_vec, t // tr,
                    rider=rider)


def _sgu_pre(up, vp, bu, bv, ln_g, ln_b):
    return jax.nn.gelu(up + bu), _layer_norm(jax.nn.gelu(vp + bv), ln_g, ln_b)


def _causal(w_ref, h):
    rows = lax.broadcasted_iota(jnp.int32, (CHUNK, CHUNK), 0)
    cols = lax.broadcasted_iota(jnp.int32, (CHUNK, CHUNK), 1)
    return jnp.where(cols <= rows, w_ref[h], 0.0)


def _sgu(name, proj, b_in, ln_g, ln_b, w_s, bias_full, rider=None):
    t = proj.shape[0]

    def fn(i, r, _):
        u, v = _sgu_pre(r[0][...], r[1][...], r[2][...], r[3][...], r[4][...], r[5][...])
        vb = v.astype(BF)
        mixed = [jnp.dot(_causal(r[6], h).astype(BF), vb[:, h * CHUNK:(h + 1) * CHUNK], preferred_element_type=F32)
                 for h in range(HEADS)]
        return [u * (jnp.concatenate(mixed, axis=1) + r[7][...])]

    return _rowcall(
        name, fn, [proj, proj, b_in, b_in, ln_g, ln_b, w_s, bias_full],
        [_rows(CHUNK, D, 0), _rows(CHUNK, D, 1), pl.BlockSpec((1, D), lambda i: (0, 0)), pl.BlockSpec((1, D), lambda i: (0, 1)),
         _whole((1, D)), _whole((1, D)), _whole((HEADS, CHUNK, CHUNK)), _whole((CHUNK, D))],
        1, [jax.ShapeDtypeStruct((t, D), BF)], [_rows(CHUNK)], t // CHUNK, rider=rider)


def _sgu_bwd(name, proj, b_in, ln_g, ln_b, w_s, bias_full, dout, dproj, rider=None):
    t = proj.shape[0]

    def fn(i, r, _):
        (u, v), vjp = jax.vjp(_sgu_pre, r[0][...], r[1][...], r[2][...], r[3][...], r[4][...], r[5][...])
        vb = v.astype(BF)
        d = r[8][...]
        masks = [_causal(r[6], h).astype(BF) for h in range(HEADS)]
        cols = [slice(h * CHUNK, (h + 1) * CHUNK) for h in range(HEADS)]
        mixed = jnp.concatenate([jnp.dot(masks[h], vb[:, cols[h]], preferred_element_type=F32) for h in range(HEADS)], axis=1)
        du = d * (mixed + r[7][...])
        dmix = d * u
        dmb = dmix.astype(BF)
        dv = jnp.concatenate([lax.dot_general(masks[h], dmb[:, cols[h]], TN, preferred_element_type=F32) for h in range(HEADS)], axis=1)
        rows = lax.broadcasted_iota(jnp.int32, (CHUNK, CHUNK), 0)
        lanes = lax.broadcasted_iota(jnp.int32, (CHUNK, CHUNK), 1)
        dws = jnp.stack([jnp.where(lanes <= rows, lax.dot_general(dmb[:, cols[h]], vb[:, cols[h]], NT, preferred_element_type=F32), 0.0)
                         for h in range(HEADS)])
        dbs = jnp.zeros((CHUNK, CHUNK), F32)
        for h in range(HEADS):
            dbs = dbs + jnp.where(lanes == h, jnp.sum(dmix[:, cols[h]], axis=1, keepdims=True), 0.0)
        dup, dvp, dbu, dbv, dg, db = vjp((du, dv))
        return [jnp.concatenate([dup, dvp], axis=1), dbu, dbv, dg, db, dws, dbs]

    return _rowcall(
        name, fn, [proj, proj, b_in, b_in, ln_g, ln_b, w_s, bias_full, dout, dproj],
        [_rows(CHUNK, D, 0), _rows(CHUNK, D, 1), pl.BlockSpec((1, D), lambda i: (0, 0)), pl.BlockSpec((1, D), lambda i: (0, 1)),
         _whole((1, D)), _whole((1, D)), _whole((HEADS, CHUNK, CHUNK)), _whole((CHUNK, D)), _rows(CHUNK),
         pl.BlockSpec(memory_space=pl.ANY)],
        1, [jax.ShapeDtypeStruct(dproj.shape, dproj.dtype)] + [_vec()] * 4
        + [jax.ShapeDtypeStruct((HEADS, CHUNK, CHUNK), F32), jax.ShapeDtypeStruct((CHUNK, CHUNK), F32)],
        [pl.BlockSpec((CHUNK, 2 * D), lambda i: (i, 0))] + [_whole((1, D))] * 4 + [_whole((HEADS, CHUNK, CHUNK)), _whole((CHUNK, CHUNK))],
        t // CHUNK, rider=rider, aliases={9: 0})


def _halo_before(tr, cb):
    return pl.BlockSpec((HALO, D), lambda i: (jnp.maximum(i * (tr // HALO) - 1, 0), cb))


def _halo_after(tr, cb, n_tiles):
    return pl.BlockSpec((HALO, D), lambda i: (jnp.minimum((i + 1) * (tr // HALO), n_tiles * (tr // HALO) - 1), cb))


def _ln_silu(z, g, b):
    return _silu(_layer_norm(z, g, b))


SUBLANES = 8
LANES = 128
CONV_STRIP = 16
DW_STRIP = 32


def _shifted_copies(buf, copies, rows):
    for b in range(1, SUBLANES):
        copies[b - 1, pl.ds(0, rows), :] = buf[pl.ds(b, rows), :]


def _shifted(buf, copies, offset, start, rows, lanes=slice(None)):
    at = pl.ds(pl.multiple_of(start + SUBLANES * (offset // SUBLANES), SUBLANES), rows)
    return buf[at, lanes] if offset % SUBLANES == 0 else copies[offset % SUBLANES - 1, at, lanes]


def _accumulate(o, v, i):
    @pl.when(i == 0)
    def _():
        o[...] = v.astype(o.dtype)

    @pl.when(i > 0)
    def _():
        o[...] += v.astype(o.dtype)


def _conv(name, proj, b_in, conv_w, conv_b, ln_g, ln_b, rider=None):
    t = proj.shape[0]
    tr = min(t, 256)

    def compute(r, outs, scr):
        zbuf, zs = scr
        i = pl.program_id(0)
        bv, bg = r[4][...], r[5][...]
        z0 = (r[0][...] + bv) * jax.nn.sigmoid(r[1][...] + bg)
        before = (r[2][...] + bv) * jax.nn.sigmoid(r[3][...] + bg)
        zbuf[pl.ds(0, HALO), :] = jnp.where(i > 0, before, 0.0)
        zbuf[pl.ds(HALO, tr), :] = z0
        outs[0][...] = z0
        _shifted_copies(zbuf, zs, tr + HALO - SUBLANES)

        def strip(s, carry):
            r0 = s * CONV_STRIP
            acc = jnp.zeros((CONV_STRIP, D), F32) + r[7][...]
            for k in range(KW):
                acc = acc + r[6][k:k + 1, :] * _shifted(zbuf, zs, HALO - (KW - 1) + k, r0, CONV_STRIP)
            outs[1][pl.ds(pl.multiple_of(r0, SUBLANES), CONV_STRIP), :] = acc
            return carry

        lax.fori_loop(0, tr // CONV_STRIP, strip, 0)
        outs[2][...] = _ln_silu(outs[1][...], r[8][...], r[9][...]).astype(BF)

    return _call(
        name, compute, (t // tr,), [proj, proj, proj, proj, b_in, b_in, conv_w, conv_b, ln_g, ln_b],
        [_rows(tr, D, 2), _rows(tr, D, 3), _halo_before(tr, 2), _halo_before(tr, 3),
         pl.BlockSpec((1, D), lambda i: (0, 2)), pl.BlockSpec((1, D), lambda i: (0, 3)),
         _whole((HALO, D)), _whole((1, D)), _whole((1, D)), _whole((1, D))],
        [jax.ShapeDtypeStruct((t, D), F32), jax.ShapeDtypeStruct((t, D), F32), jax.ShapeDtypeStruct((t, D), BF)],
        [_rows(tr)] * 3, [pltpu.VMEM((tr + HALO, D), F32), pltpu.VMEM((SUBLANES - 1, tr + HALO, D), F32)], ("arbitrary",), rider)


def _conv_bwd(name, proj, b_in, conv_w, ln_g, ln_b, z0, z1, dz3, dproj, rider=None):
    t = proj.shape[0]
    tr = min(t, 256)
    n_tiles = t // tr

    def compute(r, outs, scr):
        zbuf, dbuf, zs, ds, dwacc = scr
        i = pl.program_id(0)
        g, b = r[5][...], r[6][...]
        zero_row = jnp.zeros((1, D), F32)
        _, vjp = jax.vjp(_ln_silu, r[9][...], g, b)
        dz1, dg, db = vjp(r[11][...])
        dcb = jnp.sum(dz1, axis=0, keepdims=True)
        _, vjp_after = jax.vjp(_ln_silu, r[10][...], g, b)
        dz1_after = vjp_after(r[12][...])[0]
        dbuf[pl.ds(0, tr), :] = dz1
        dbuf[pl.ds(tr, HALO), :] = jnp.where(i < n_tiles - 1, dz1_after, 0.0)
        zbuf[pl.ds(0, HALO), :] = jnp.where(i > 0, r[8][...], 0.0)
        zbuf[pl.ds(HALO, tr), :] = r[7][...]
        _shifted_copies(dbuf, ds, tr + HALO - SUBLANES)
        _shifted_copies(zbuf, zs, tr + HALO - SUBLANES)

        def dz0_strip(s, carry):
            r0 = s * CONV_STRIP
            at = pl.ds(pl.multiple_of(r0, CONV_STRIP), CONV_STRIP)
            acc = jnp.zeros((CONV_STRIP, D), F32)
            for k in range(KW):
                acc = acc + r[4][k:k + 1, :] * _shifted(dbuf, ds, KW - 1 - k, r0, CONV_STRIP)
            a = r[0][at, :] + r[2][...]
            sg = jax.nn.sigmoid(r[1][at, :] + r[3][...])
            dcv = acc * sg
            dcg = acc * a * sg * (1.0 - sg)
            outs[0][at, :] = jnp.concatenate([dcv, dcg], axis=1).astype(BF)
            return carry[0] + jnp.sum(dcv, axis=0, keepdims=True), carry[1] + jnp.sum(dcg, axis=0, keepdims=True)

        dbv, dbg = lax.fori_loop(0, tr // CONV_STRIP, dz0_strip, (zero_row, zero_row))

        for lb in range(D // LANES):
            lanes = slice(lb * LANES, (lb + 1) * LANES)

            def dw_strip(s, accs, lanes=lanes):
                r0 = s * DW_STRIP
                dz = dbuf[pl.ds(pl.multiple_of(r0, SUBLANES), DW_STRIP), lanes]
                out = []
                for k in range(KW):
                    prod = dz * _shifted(zbuf, zs, HALO - (KW - 1) + k, r0, DW_STRIP, lanes)
                    part = prod[0:SUBLANES]
                    for q in range(1, DW_STRIP // SUBLANES):
                        part = part + prod[q * SUBLANES:(q + 1) * SUBLANES]
                    out.append(accs[k] + part)
                return tuple(out)

            accs = lax.fori_loop(0, tr // DW_STRIP, dw_strip, tuple(jnp.zeros((SUBLANES, LANES), F32) for _ in range(KW)))
            for k in range(KW):
                dwacc[pl.ds(k * SUBLANES, SUBLANES), lanes] = accs[k]
        dw_rows = [jnp.sum(dwacc[pl.ds(k * SUBLANES, SUBLANES), :], axis=0, keepdims=True) for k in range(KW)]
        dw_rows.append(jnp.zeros((HALO - KW, D), F32))
        for o, v in zip(outs[1:], (dbv, dbg, jnp.concatenate(dw_rows, axis=0), dcb, dg, db)):
            _accumulate(o, v, i)

    wide = pl.BlockSpec((tr, 2 * D), lambda i: (i, 1))
    return _call(
        name, compute, (n_tiles,), [proj, proj, b_in, b_in, conv_w, ln_g, ln_b, z0, z0, z1, z1, dz3, dz3, dproj],
        [_rows(tr, D, 2), _rows(tr, D, 3), pl.BlockSpec((1, D), lambda i: (0, 2)), pl.BlockSpec((1, D), lambda i: (0, 3)),
         _whole((HALO, D)), _whole((1, D)), _whole((1, D)),
         _rows(tr), _halo_before(tr, 0), _rows(tr), _halo_after(tr, 0, n_tiles), _rows(tr), _halo_after(tr, 0, n_tiles),
         pl.BlockSpec(memory_space=pl.ANY)],
        [jax.ShapeDtypeStruct(dproj.shape, dproj.dtype), _vec(), _vec(), _vec(HALO), _vec(), _vec(), _vec()],
        [wide] + [_whole((1, D))] * 2 + [_whole((HALO, D))] + [_whole((1, D))] * 3,
        [pltpu.VMEM((tr + HALO, D), F32), pltpu.VMEM((tr + HALO, D), F32),
         pltpu.VMEM((SUBLANES - 1, tr + HALO, D), F32), pltpu.VMEM((SUBLANES - 1, tr + HALO, D), F32),
         pltpu.VMEM((HALO * SUBLANES, D), F32)],
        ("arbitrary",), rider, aliases={13: 0})


def _merge_fn(ga, gb, bga, bgb, ya, yb):
    return jax.nn.sigmoid(ga + bga) * ya + jax.nn.sigmoid(gb + bgb) * yb


def _mix_tail(name, ua, z3, proj, b_in, wa, wb, wo, x, g, next_norm):
    t = ua.shape[0]
    tr = min(t, 256)

    def compute(r, outs, _):
        ya = jnp.dot(r[0][...], r[6][...], preferred_element_type=F32)
        yb = jnp.dot(r[1][...], r[7][...], preferred_element_type=F32)
        merged = _merge_fn(r[2][...], r[3][...], r[4][...], r[5][...], ya, yb).astype(BF)
        y = jnp.dot(merged, r[8][...], preferred_element_type=F32)
        x_out = r[9][...] + r[10][...] * y
        for o, v in zip(outs, (ya, yb, merged, y, x_out, _rms_mod(x_out, r[11][...], r[12][...], r[13][...]))):
            o[...] = v.astype(o.dtype)

    row = _whole((1, D))
    return _call(
        name, compute, (t // tr,), [ua, z3, proj, proj, b_in, b_in, wa, wb, wo, x, g, *next_norm],
        [_rows(tr), _rows(tr), _rows(tr, D, 4), _rows(tr, D, 5), pl.BlockSpec((1, D), lambda i: (0, 4)),
         pl.BlockSpec((1, D), lambda i: (0, 5)), _whole((D, D)), _whole((D, D)), _whole((D, D)), _rows(tr), row, row, row, row],
        [jax.ShapeDtypeStruct((t, D), dt) for dt in (F32, F32, BF, BF, F32, BF)], [_rows(tr)] * 6, [], ("arbitrary",))


def _mix_tail_bwd(name, dy, proj, b_in, ya, yb, wa, wb, wo, rider=None):
    t = proj.shape[0]
    tr = min(t, 256)

    def compute(r, outs, _):
        i = pl.program_id(0)
        dm = lax.dot_general(r[0][...], r[9][...], NT, preferred_element_type=F32)
        _, vjp = jax.vjp(_merge_fn, *[x[...] for x in r[1:7]])
        dga, dgb, dbga, dbgb, dya, dyb = vjp(dm)
        dya, dyb = dya.astype(BF), dyb.astype(BF)
        outs[0][...] = jnp.concatenate([dga, dgb], axis=1).astype(BF)
        outs[1][...] = dya
        outs[2][...] = dyb
        outs[3][...] = lax.dot_general(dya, r[7][...], NT, preferred_element_type=F32)
        outs[4][...] = lax.dot_general(dyb, r[8][...], NT, preferred_element_type=F32)
        _accumulate(outs[5], dbga, i)
        _accumulate(outs[6], dbgb, i)

    return _call(
        name, compute, (t // tr,), [dy, proj, proj, b_in, b_in, ya, yb, wa, wb, wo],
        [_rows(tr), _rows(tr, D, 4), _rows(tr, D, 5), pl.BlockSpec((1, D), lambda i: (0, 4)), pl.BlockSpec((1, D), lambda i: (0, 5)),
         _rows(tr), _rows(tr), _whole((D, D)), _whole((D, D)), _whole((D, D))],
        [jax.ShapeDtypeStruct((t, D_IN), BF)] + [jax.ShapeDtypeStruct((t, D), BF)] * 2 + [jax.ShapeDtypeStruct((t, D), F32)] * 2
        + [_vec(), _vec()],
        [pl.BlockSpec((tr, 2 * D), lambda i: (i, 2))] + [_rows(tr)] * 4 + [_whole((1, D))] * 2, [], ("arbitrary",), rider)


def _loss_head(name, x, gain, target, f, g, scale):
    t = x.shape[0]
    tr = min(t, 512)

    def loss_fn(xv, gn, tgt):
        y = xv * lax.rsqrt(jnp.mean(xv * xv, axis=-1, keepdims=True) + EPS) * gn
        return 0.5 * jnp.sum(jnp.mean(jnp.square(y - tgt), axis=-1))

    def fn(i, r, _):
        loss, vjp = jax.vjp(loss_fn, r[0][...], r[1][...], r[2][...])
        dx, dgain, _ = vjp(jnp.ones((), F32))
        df, dg = _gate_grads(dx, r[3][...], r[4][...], scale)
        return [dx, df, dgain, jnp.zeros((1, D), F32) + loss, dg]

    return _rowcall(name, fn, [x, gain, target, f, g], [_rows(tr), _whole((1, D)), _rows(tr), _rows(tr), _whole((1, D))], 2,
                    [jax.ShapeDtypeStruct((t, D), F32), jax.ShapeDtypeStruct((t, D), BF), _vec(), _vec(), _vec()],
                    [_rows(tr)] * 2 + [_whole((1, D))] * 3, t // tr)


def _adamw(w, g, m, v):
    m = B1 * m + (1.0 - B1) * g
    v = B2 * v + (1.0 - B2) * jnp.square(g)
    m_hat = m / BC1
    v_hat = v / BC2
    delta = -LR * (m_hat / (jnp.sqrt(v_hat) + ADAM_EPS) + WD * w)
    return delta, m, v


ADAMW_ROWS = 64


def _adamw_group(name, items, rider=None, rows=ADAMW_ROWS):
    ins, in_specs, out_shapes, out_specs, plan = [], [], [], [], []
    first = 0
    for chip_sum, received, w, m, v in items:
        r, c = w.shape
        tr = min(r, rows)
        n = r // tr

        def tile(i, first=first, n=n):
            return jnp.clip(i - first, 0, n - 1)

        spec = pl.BlockSpec((tr, c), lambda i, tile=tile: (tile(i), 0))
        ins += [chip_sum, *received, w, m, v]
        in_specs += [pl.BlockSpec((None, tr, c), lambda i, tile=tile: (0, tile(i), 0))]
        in_specs += [pl.BlockSpec((g.shape[0], tr, c), lambda i, tile=tile: (0, tile(i), 0)) for g in received]
        in_specs += [spec] * 3
        out_shapes += [jax.ShapeDtypeStruct((r, c), F32)] * 4
        out_specs += [spec] * 4
        plan.append((first, n, [g.shape[0] for g in received]))
        first += n

    def compute(in_refs, out_refs, _):
        i = pl.program_id(0)
        at_in = at_out = 0
        for start, n, counts in plan:
            mine = in_refs[at_in:at_in + 4 + len(counts)]
            outs = out_refs[at_out:at_out + 4]
            at_in += 4 + len(counts)
            at_out += 4

            @pl.when(jnp.logical_and(i >= start, i < start + n))
            def _(mine=mine, outs=outs, counts=counts):
                g = mine[0][...].astype(F32)
                for j, count in enumerate(counts):
                    for s in range(count):
                        g = g + mine[1 + j][s].astype(F32)
                delta, m_new, v_new = _adamw(mine[-3][...], g, mine[-2][...], mine[-1][...])
                for o, val in zip(outs, (g, delta, m_new, v_new)):
                    o[...] = val

    res = _call(name, compute, (first,), ins, in_specs, out_shapes, out_specs, [], ("arbitrary",), rider)
    outs, rode = res if rider else (res, [])
    return [outs[4 * j:4 * j + 4] for j in range(len(items))], rode


def _adamw_small(name, packed_all, late_all, dws_all, vectors, w_s):
    n_vec = len(vectors)

    def body(*refs):
        p_ref, l_ref, d_ref = refs[:3]
        param_refs = refs[3:3 + 3 * n_vec + 3]
        out_refs = refs[3 + 3 * n_vec + 3:-1]
        g_ref = refs[-1]
        g = p_ref[0]
        late = l_ref[0]
        for s in range(1, NDEV):
            g = g + p_ref[s]
            late = late + l_ref[s]
        g_ref[...] = g
        g_ref[pl.ds(0, R_LATE), :] += late

        def update(gp, wmv, outs):
            delta, m_new, v_new = _adamw(wmv[0][...], gp, wmv[1][...], wmv[2][...])
            for o, val in zip(outs, (gp, delta, m_new, v_new)):
                o[...] = val

        for j, (row, rows, *_) in enumerate(vectors):
            pieces = [g_ref[pl.ds(row + r, 1), :] for r in range(rows)]
            update(pieces[0] if rows == 1 else jnp.concatenate(pieces, axis=1), param_refs[3 * j:3 * j + 3], out_refs[4 * j:4 * j + 4])
        gw = d_ref[0]
        for s in range(1, NDEV):
            gw = gw + d_ref[s]
        update(gw, param_refs[3 * n_vec:], out_refs[4 * n_vec:4 * n_vec + 4])
        out_refs[-2][...] = g_ref[pl.ds(R_CW, KW), :]
        out_refs[-1][...] = g_ref[pl.ds(R_LOSS, 1), :]

    params = [a for _, _, w, m, v in vectors for a in (w, m, v)] + list(w_s)
    out_shapes = [jax.ShapeDtypeStruct(w.shape, F32) for _, _, w, _, _ in vectors for _ in range(4)]
    out_shapes += [jax.ShapeDtypeStruct(w_s[0].shape, F32)] * 4 + [jax.ShapeDtypeStruct((KW, D), F32), _vec()]
    res = pl.pallas_call(body, name=name, out_shape=out_shapes, scratch_shapes=[pltpu.VMEM((R_TOTAL, D), F32)],
                         compiler_params=_params(None))(packed_all, late_all, dws_all, *params)
    return [res[4 * j:4 * j + 4] for j in range(n_vec + 1)], res[-2], res[-1]


def _adamw_plain(name, g, w, m, v):
    def body(g_ref, w_ref, m_ref, v_ref, d_ref, mo_ref, vo_ref):
        delta, m_new, v_new = _adamw(w_ref[...], g_ref[...], m_ref[...], v_ref[...])
        d_ref[...] = delta
        mo_ref[...] = m_new
        vo_ref[...] = v_new

    return pl.pallas_call(body, name=name, out_shape=[jax.ShapeDtypeStruct(w.shape, F32)] * 3,
                          compiler_params=_params(None))(g, w, m, v)


def _adamw_ada(name, c_all_t, dmod, dmod_late, w, m, v):
    r, c = w.shape
    tr = 256

    def fn(i, refs, _):
        ca = _silu(refs[0][...])
        dm = refs[1][...] + refs[2][...]
        g = ca[:, 0:1] * dm[0:1, :]
        for b in range(1, NDEV):
            g = g + ca[:, b:b + 1] * dm[b:b + 1, :]
        delta, m_new, v_new = _adamw(refs[3][...], g, refs[4][...], refs[5][...])
        return [g, delta, m_new, v_new]

    spec = pl.BlockSpec((tr, c), lambda i: (i, 0))
    whole = pl.BlockSpec((NDEV, c), lambda i: (0, 0))
    return _rowcall(name, fn, [c_all_t, dmod, dmod_late, w, m, v],
                    [pl.BlockSpec((tr, NDEV), lambda i: (i, 0)), whole, whole, spec, spec, spec], 4,
                    [jax.ShapeDtypeStruct((r, c), F32)] * 4, [spec] * 4, r // tr)


def _ffn_fwd(tag, x, h, g, wg, wu, wd_shard, down_rider, next_norm=None, more_shards=(), norm=None):
    t = x.shape[0]
    tm = min(t, 512 if down_rider else 1024)
    (gate, up, act, *normed), (wd, *more) = _ffn_up(f"{tag}_up", x if norm else h, wg, wu,
                                                    rider=_gather_rider([wd_shard, *more_shards]), norm=norm)
    h = normed[0] if norm else h
    row = pl.BlockSpec((1, D), lambda i, j, k: (0, 0))

    def epilogue(f, xv, gv, *norm):
        x_out = xv + 0.5 * gv * f
        return (x_out, f, _rms_mod(x_out, *norm)) if norm else (x_out, f)

    res = _mm_nn(f"{tag}_down", act, wd.reshape(F, D), tm, D, 1024, extras=(x, g, *(next_norm or ())),
                 extra_specs=(pl.BlockSpec((tm, D), lambda i, j, k: (i, 0)), row, *([row] * 3 if next_norm else [])),
                 epilogue=epilogue, out_dtypes=(F32, BF, BF) if next_norm else (F32, BF), rider=down_rider)
    (x_out, f, *h_next), rode = res if down_rider else (res, None)
    return x_out, (h_next[0] if next_norm else None), (x, h, gate, up, act, f), wd, rode, more


def _ffn_bwd(tag, dx_out, df, saved, gain, sh, sc, wg, wu, wd, slots, dact_rider=None, dwd_rider=None, dwgu_rider=None,
             below=None, fuse_dh=False):
    x, h, gate, up, act, f = saved
    t = x.shape[0]
    tm = min(t, 1024)
    if fuse_dh:
        dgate, dup, dh = _ffn_dact_dh(f"{tag}_dact_dh", df, wd.reshape(F, D), gate, up, wg, wu)
        dwd = _mm_tn(f"{tag}_dwd", act, df, 512, D).reshape(NDEV, F // NDEV, D)
        (dwg, dwu), (sib_d,) = _dw_gate_up(f"{tag}_dwgu", h, dgate, dup, rider=_pair_rider([dwd]))
        (sum_d,) = _pair_add(f"{tag}_dwd_add", [dwd], [sib_d], slots)
        normed, (sib_g, sib_u) = _norm_mod_bwd(f"{tag}_norm_bwd", x, gain, sc, sh, dh, dx_out, below=below,
                                               rider=_pair_rider([dwg, dwu]))
        sum_g, sum_u = _pair_add(f"{tag}_dwgu_add", [dwg, dwu], [sib_g, sib_u], slots)
        return normed, (sum_d, None), sum_g, sum_u, [], [], []

    blk = pl.BlockSpec((t, F // NDEV), lambda i, j, k: (i, j))
    res = _mm_nt(f"{tag}_dact", df, wd.reshape(F, D), t, F // NDEV, out_dtypes=(BF, BF),
                 extras=(gate, up), extra_specs=(blk, blk), epilogue=_swiglu_bwd, rider=dact_rider)
    (dgate, dup), rode_dact = res if dact_rider else (res, [])
    res = _mm_tn(f"{tag}_dwd", act, df, 512, D, rider=dwd_rider)
    dwd, rode_dwd = res if dwd_rider else (res, [])
    dwd = dwd.reshape(NDEV, F // NDEV, D)
    (dwg, dwu), (sib_d, *rode_dwgu) = _dw_gate_up(f"{tag}_dwgu", h, dgate, dup,
                                                  rider=[_pair_rider([dwd])] + ([dwgu_rider] if dwgu_rider else []))
    (sum_d,) = _pair_add(f"{tag}_dwd_add", [dwd], [sib_d], slots)
    dh, (sib_g, sib_u, got_d) = _mm_nt_blocked(f"{tag}_dh", [dgate, dup], [wg, wu], tm,
                                               rider=[_pair_rider([dwg, dwu]), _chip_rider([sum_d])])
    sum_g, sum_u = _pair_add(f"{tag}_dwgu_add", [dwg, dwu], [sib_g, sib_u], slots)
    normed = _norm_mod_bwd(f"{tag}_norm_bwd", x, gain, sc, sh, dh, dx_out, below=below)
    return normed, (sum_d, [got_d]), sum_g, sum_u, rode_dact, rode_dwd, rode_dwgu


def kernel(x, c, ada_w, ada_b, norm_ffn1, ffn1_w_gate, ffn1_w_up, ffn1_w_down, norm_mix, mix_w_in, mix_b_in, sgu_ln_g, sgu_ln_b, sgu_w_s, sgu_b_s, conv_w, conv_b, conv_ln_g, conv_ln_b, w_branch_a, w_branch_b, w_out, norm_ffn2, ffn2_w_gate, ffn2_w_up, ffn2_w_down, norm_final, loss_target, m_ada_w, m_ada_b, m_norm_ffn1, m_ffn1_w_gate, m_ffn1_w_up, m_ffn1_w_down, m_norm_mix, m_mix_w_in, m_mix_b_in, m_sgu_ln_g, m_sgu_ln_b, m_sgu_w_s, m_sgu_b_s, m_conv_w, m_conv_b, m_conv_ln_g, m_conv_ln_b, m_w_branch_a, m_w_branch_b, m_w_out, m_norm_ffn2, m_ffn2_w_gate, m_ffn2_w_up, m_ffn2_w_down, m_norm_final, v_ada_w, v_ada_b, v_norm_ffn1, v_ffn1_w_gate, v_ffn1_w_up, v_ffn1_w_down, v_norm_mix, v_mix_w_in, v_mix_b_in, v_sgu_ln_g, v_sgu_ln_b, v_sgu_w_s, v_sgu_b_s, v_conv_w, v_conv_b, v_conv_ln_g, v_conv_ln_b, v_w_branch_a, v_w_branch_b, v_w_out, v_norm_ffn2, v_ffn2_w_gate, v_ffn2_w_up, v_ffn2_w_down, v_norm_final):
    mx, my, mc = _position()
    me = 4 * mx + 2 * my + mc
    chip = 2 * mx + my
    slots = jnp.stack([2 * (chip ^ k) + mc for k in range(N_CHIPS)]).astype(jnp.int32)
    t = x.shape[1]
    tm = min(t, 1024)
    x0 = x.reshape(t, D)
    target = loss_target.reshape(t, D)
    given = dict(ffn1_w_gate=(ffn1_w_gate, m_ffn1_w_gate, v_ffn1_w_gate), ffn1_w_up=(ffn1_w_up, m_ffn1_w_up, v_ffn1_w_up),
                 ffn1_w_down=(ffn1_w_down, m_ffn1_w_down, v_ffn1_w_down), mix_w_in=(mix_w_in, m_mix_w_in, v_mix_w_in),
                 w_branch_a=(w_branch_a, m_w_branch_a, v_w_branch_a), w_branch_b=(w_branch_b, m_w_branch_b, v_w_branch_b),
                 w_out=(w_out, m_w_out, v_w_out), ffn2_w_gate=(ffn2_w_gate, m_ffn2_w_gate, v_ffn2_w_gate),
                 ffn2_w_up=(ffn2_w_up, m_ffn2_w_up, v_ffn2_w_up), ffn2_w_down=(ffn2_w_down, m_ffn2_w_down, v_ffn2_w_down))
    shard = {n: wmv[0][0].astype(BF) for n, wmv in given.items()}

    ada_cols = N_MOD * D // NDEV
    c_all, taps_all, mod_all, (wg1, wu1) = _prologue(
        "prologue", jnp.pad(c, ((0, SUBLANES - 1), (0, 0))), jnp.pad(conv_w[0], ((0, HALO - KW), (0, 0))), ada_w[0],
        lax.dynamic_slice(ada_b, (0, me * ada_cols), (1, ada_cols)), [shard["ffn1_w_gate"], shard["ffn1_w_up"]])
    conv_w_full = jnp.transpose(taps_all.reshape(NDEV, HALO, CHUNK), (1, 0, 2)).reshape(HALO, D)
    mod = lax.dynamic_index_in_dim(mod_all.reshape(NDEV, NDEV, ada_cols), me, axis=1, keepdims=False).reshape(N_MOD, 1, D)
    sh1, sc1, g1, sh2, sc2, g2, sh3, sc3, g3 = [mod[i] for i in range(N_MOD)]

    x1, h2, saved1, wd1, (w_in,), (wa3, wb3) = _ffn_fwd(
        "ffn1", x0, None, g1, wg1, wu1, shard["ffn1_w_down"], _gather_rider([shard["mix_w_in"]]),
        next_norm=(norm_mix, sc2, sh2), more_shards=(shard["w_branch_a"], shard["w_branch_b"]), norm=(norm_ffn1, sc1, sh1))
    proj, (wg2, wo3) = _mm_nn_blocked("mix_in", h2, w_in, tm, rider=_gather_rider([shard["ffn2_w_gate"], shard["w_out"]]))
    bias_full = jnp.repeat(sgu_b_s[0].T, CHUNK, axis=1)
    (ua,) = _sgu("sgu", proj, mix_b_in, sgu_ln_g, sgu_ln_b, sgu_w_s[0], bias_full)
    (z0, z1, z3), (wu2,) = _conv("conv", proj, mix_b_in, conv_w_full, conv_b, conv_ln_g, conv_ln_b,
                                 rider=_gather_rider([shard["ffn2_w_up"]]))
    wa, wb, wo = wa3.reshape(D, D), wb3.reshape(D, D), wo3.reshape(D, D)
    ya, yb, merged, y, x2, h3 = _mix_tail("mix_tail", ua, z3, proj, mix_b_in, wa, wb, wo, x1, g2, (norm_ffn2, sc3, sh3))
    x3, _, saved3, wd2, _, _ = _ffn_fwd("ffn2", x2, h3, g3, wg2, wu2, shard["ffn2_w_down"], None)

    norm_final2 = norm_final.reshape(1, D)
    dx3, df3, d_norm_final, loss_row, dg3 = _loss_head("loss_head", x3, norm_final2, target, saved3[-1], g3, 0.5)
    (dx2, dy, d_norm_ffn2, dsc3, dsh3, dg2), (sum_d2, _), sum_g2, sum_u2, _, _, _ = _ffn_bwd(
        "ffn2", dx3, df3, saved3, norm_ffn2, sh3, sc3, wg2, wu2, wd2, slots, below=(y, g2, 1.0), fuse_dh=True)
    (dproj, dya, dyb, dua, dz3, db_ga, db_gb), (got_g2_near,) = _mix_tail_bwd(
        "mix_tail_bwd", dy, proj, mix_b_in, ya, yb, wa, wb, wo, rider=_chip_rider([sum_g2], NEIGHBOURS))
    dwo, dwa, dwb = [g.reshape(NDEV, D // NDEV, D) for g in _dw_square("mix_dw", [(merged, dy), (ua, dya), (z3, dyb)])]
    (dproj, db_u, db_v, d_sgu_g, d_sgu_b, d_ws, d_bs_t), (*sib_abo, got_g2_far) = _sgu_bwd(
        "sgu_bwd", proj, mix_b_in, sgu_ln_g, sgu_ln_b, sgu_w_s[0], bias_full, dua, dproj,
        rider=[_pair_rider([dwa, dwb, dwo]), _chip_rider([sum_g2], DIAGONAL)])
    sum_a, sum_b, sum_o = _pair_add("mix_dw_add", [dwa, dwb, dwo], sib_abo, slots)
    (dproj, db_cv, db_cg, d_cw, d_cb, d_cln_g, d_cln_b), (got_u2, got_d2) = _conv_bwd(
        "conv_bwd", proj, mix_b_in, conv_w_full, conv_ln_g, conv_ln_b, z0, z1, dz3, dproj, rider=_chip_rider([sum_u2, sum_d2]))
    dwin, (got_a, got_b, got_o) = _mm_tn_blocked("mix_dwin", h2, dproj, rider=_chip_rider([sum_a, sum_b, sum_o]))

    d_bs = jnp.transpose(d_bs_t[:, :HEADS])
    zero = jnp.zeros((1, D), F32)
    pack_rows = [zero, zero, zero, zero, zero, dg2, dsh3, dsc3, dg3,
                 zero, zero, d_norm_ffn2, d_norm_final,
                 db_u, db_v, db_cv, db_cg, db_ga, db_gb,
                 d_sgu_g, d_sgu_b, d_bs.reshape(1, D), d_cb, d_cln_g, d_cln_b,
                 d_cw[:KW], loss_row, jnp.zeros((R_TOTAL - R_LOSS - 1, D), F32)]
    packed = jnp.concatenate(pack_rows, axis=0)
    d_ws2 = d_ws.reshape(HEADS * CHUNK, CHUNK)
    dh2, (sib_in, packed_all, dws_all) = _mm_nt_blocked("mix_in_bwd", [dproj], [w_in], tm,
                                                        rider=[_pair_rider([dwin]), _gather_rider([packed, d_ws2])])
    (sum_in,) = _pair_add("mix_dwin_add", [dwin], [sib_in], slots)
    dx1, df1, d_norm_mix, dsc2, dsh2, dg1 = _norm_mod_bwd("mix_norm_bwd", x1, norm_mix, sc2, sh2, dh2, dx2,
                                                          below=(saved1[-1], g1, 0.5))
    (dx0, d_norm_ffn1, dsc1, dsh1), down1, sum_g1, sum_u1, (got_in_near,), _, (got_in_far,) = _ffn_bwd(
        "ffn1", dx1, df1, saved1, norm_ffn1, sh1, sc1, wg1, wu1, wd1, slots,
        dact_rider=_chip_rider([sum_in], NEIGHBOURS), dwgu_rider=_chip_rider([sum_in], DIAGONAL))
    packed_late = jnp.concatenate([dsh1, dsc1, dg1, dsh2, dsc2, jnp.zeros((4, D), F32), d_norm_ffn1, d_norm_mix,
                                   jnp.zeros((R_LATE - 11, D), F32)], axis=0)
    grads = dict(ffn2_w_gate=(sum_g2, [got_g2_near, got_g2_far]), ffn2_w_up=(sum_u2, [got_u2]), ffn2_w_down=(sum_d2, [got_d2]),
                 mix_w_in=(sum_in, [got_in_near, got_in_far]), w_branch_a=(sum_a, [got_a]), w_branch_b=(sum_b, [got_b]),
                 w_out=(sum_o, [got_o]), ffn1_w_down=down1)
    done, (late_all, got_g1, got_u1) = _adamw_group(
        "adamw_most", [(cs, got, *[a[0] for a in given[n]]) for n, (cs, got) in grads.items()],
        rider=[_gather_rider([packed_late]), _chip_rider([sum_g1, sum_u1])])
    last, _ = _adamw_group("adamw_ffn1_in", [(sum_g1, [got_g1], *[a[0] for a in given["ffn1_w_gate"]]),
                                            (sum_u1, [got_u1], *[a[0] for a in given["ffn1_w_up"]])], rows=256)
    big_out = {n: [o.reshape(given[n][0].shape) for o in outs]
               for n, outs in zip([*grads, "ffn1_w_gate", "ffn1_w_up"], [*done, *last])}

    flat = lambda a: a.reshape(1, -1)
    vectors = [("ada_b", 0, 9, ada_b, m_ada_b, v_ada_b), ("norm_ffn1", 9, 1, norm_ffn1, m_norm_ffn1, v_norm_ffn1),
               ("norm_mix", 10, 1, norm_mix, m_norm_mix, v_norm_mix), ("norm_ffn2", 11, 1, norm_ffn2, m_norm_ffn2, v_norm_ffn2),
               ("norm_final", 12, 1, norm_final, m_norm_final, v_norm_final), ("mix_b_in", 13, 6, mix_b_in, m_mix_b_in, v_mix_b_in),
               ("sgu_ln_g", 19, 1, sgu_ln_g, m_sgu_ln_g, v_sgu_ln_g), ("sgu_ln_b", 20, 1, sgu_ln_b, m_sgu_ln_b, v_sgu_ln_b),
               ("sgu_b_s", 21, 1, sgu_b_s, m_sgu_b_s, v_sgu_b_s), ("conv_b", 22, 1, conv_b, m_conv_b, v_conv_b),
               ("conv_ln_g", 23, 1, conv_ln_g, m_conv_ln_g, v_conv_ln_g), ("conv_ln_b", 24, 1, conv_ln_b, m_conv_ln_b, v_conv_ln_b)]
    small_out, d_cw_all, loss_sum = _adamw_small(
        "adamw_small", packed_all, late_all, dws_all, [(row, rows, flat(wv), flat(mv), flat(vv)) for _, row, rows, wv, mv, vv in vectors],
        [a.reshape(HEADS * CHUNK, CHUNK) for a in (sgu_w_s, m_sgu_w_s, v_sgu_w_s)])
    small = {n: [o.reshape(wv.shape) for o in outs] for (n, _, _, wv, _, _), outs in zip(vectors, small_out)}
    small["sgu_w_s"] = [o.reshape(sgu_w_s.shape) for o in small_out[-1]]
    g_cw = lax.dynamic_slice(d_cw_all, (0, me * CHUNK), (KW, CHUNK))
    small["conv_w"] = [o.reshape(conv_w.shape) for o in (g_cw, *_adamw_plain("adamw_conv_w", g_cw, conv_w[0], m_conv_w[0], v_conv_w[0]))]
    loss = loss_sum[0, 0]

    dmod_cols = [lax.dynamic_slice(a[:, :N_MOD, :].reshape(NDEV, N_MOD * D), (0, me * ada_cols), (NDEV, ada_cols))
                 for a in (packed_all, late_all)]
    ada_out = [o.reshape(ada_w.shape) for o in _adamw_ada("adamw_ada_w", jnp.transpose(c_all), *dmod_cols, ada_w[0], m_ada_w[0], v_ada_w[0])]

    order = ["ada_w", "ada_b", "norm_ffn1", "ffn1_w_gate", "ffn1_w_up", "ffn1_w_down", "norm_mix", "mix_w_in", "mix_b_in",
             "sgu_ln_g", "sgu_ln_b", "sgu_w_s", "sgu_b_s", "conv_w", "conv_b", "conv_ln_g", "conv_ln_b", "w_branch_a",
             "w_branch_b", "w_out", "norm_ffn2", "ffn2_w_gate", "ffn2_w_up", "ffn2_w_down", "norm_final"]

    def leaf(n, kind):
        if n == "ada_w":
            return ada_out[kind]
        if n in big_out:
            return big_out[n][kind]
        return small[n][kind]

    return (loss, dx0.reshape(x.shape), *[leaf(n, kind) for kind in range(4) for n in order])
```

```python
import jax
import jax.numpy as jnp
from jax import lax
from jax.experimental import pallas as pl
from jax.experimental.pallas import tpu as pltpu

D = 1024
F = 4 * D
D_IN = 6 * D
HEADS = 8
CHUNK = 128
KW = 31
HALO = 32
N_MOD = 9
NDEV = 8
N_CHIPS = 4
EPS = 1e-6
LR, B1, B2, ADAM_EPS, WD, STEP = 0.001, 0.9, 0.999, 1e-08, 0.01, 10
BC1 = 1.0 - B1 ** STEP
BC2 = 1.0 - B2 ** STEP
VMEM_LIMIT = 56 * 1024 * 1024
MESH = pl.DeviceIdType.MESH
HBM = pl.BlockSpec(memory_space=pltpu.HBM)
VMEM = pl.BlockSpec(memory_space=pltpu.VMEM)
BF = jnp.bfloat16
F32 = jnp.float32

NN = (((1,), (0,)), ((), ()))
NT = (((1,), (1,)), ((), ()))
TN = (((0,), (0,)), ((), ()))

R_CW, R_LOSS, R_TOTAL = 25, 56, 64
R_LATE = 16


def _params(sem):
    return pltpu.CompilerParams(dimension_semantics=sem, vmem_limit_bytes=VMEM_LIMIT)


def _position():
    return lax.axis_index("x"), lax.axis_index("y"), lax.axis_index("c")


def _flip(pos, k):
    x, y, c = pos
    return (x ^ (k >> 2 & 1), y ^ (k >> 1 & 1), c ^ (k & 1))


def _index(pos):
    return 4 * pos[0] + 2 * pos[1] + pos[2]


def _gather_rows(x_ref, out_ref, send_sems, recv_sems, local_sem):
    m_per = x_ref.shape[0]
    x, y, c = _position()
    me, sibling = (x, y, c), (x, y, 1 - c)
    chips = [(1 - x, y), (x, 1 - y), (1 - x, 1 - y)]

    def rows(pos):
        return out_ref.at[pl.ds(_index(pos) * m_per, m_per), :]

    def copy(k, block, to, src=None):
        return pltpu.make_async_remote_copy(
            src_ref=rows(block) if src is None else src, dst_ref=rows(block),
            send_sem=send_sems.at[k], recv_sem=recv_sems.at[k], device_id=to, device_id_type=MESH)

    mine = pltpu.make_async_copy(x_ref, rows(me), local_sem)
    mine.start()
    first = [copy(0, me, sibling, src=x_ref)]
    first += [copy(1 + j, me, (*chip, c), src=x_ref) for j, chip in enumerate(chips)]
    for cp in first:
        cp.start()
    passed = [copy(4 + j, (*chip, c), sibling) for j, chip in enumerate(chips)]
    for j, chip in enumerate(chips):
        copy(1 + j, (*chip, c), me).wait_recv()
        passed[j].start()
    copy(0, sibling, me).wait_recv()
    for j, chip in enumerate(chips):
        copy(4 + j, (*chip, 1 - c), me).wait_recv()
    for cp in first + passed:
        cp.wait_send()
    mine.wait()


def _prologue(name, c_rows, taps, ada_w, ada_b, shards):
    rider = _gather_rider(shards)
    n = len(shards)
    nc = ada_w.shape[1]

    def body(*refs):
        c_ref, taps_ref, w_ref, b_ref = refs[:4]
        shard_refs = refs[4:4 + n]
        c_all_ref, taps_all_ref, mod_all_ref = refs[4 + n:7 + n]
        gathered_refs = refs[7 + n:7 + 2 * n]
        c_buf, mod_part, sems = refs[7 + 2 * n], refs[8 + 2 * n], refs[9 + 2 * n:]
        rider.start(shard_refs, gathered_refs, sems[9:])
        _gather_rows(c_ref, c_buf, *sems[0:3])
        rider.mid(shard_refs, gathered_refs, sems[9:])
        c_all = jnp.concatenate([c_buf[pl.ds(d * SUBLANES, 1), :] for d in range(NDEV)], axis=0)
        c_all_ref[...] = c_all
        mod_part[...] = jnp.dot(_silu(c_all), w_ref[...], preferred_element_type=F32) + b_ref[...]
        _gather_rows(taps_ref, taps_all_ref, *sems[3:6])
        _gather_rows(mod_part, mod_all_ref, *sems[6:9])
        rider.relay(shard_refs, gathered_refs, sems[9:])
        rider.finish(shard_refs, gathered_refs, sems[9:])

    small_sems = [pltpu.SemaphoreType.DMA((7,)), pltpu.SemaphoreType.DMA((7,)), pltpu.SemaphoreType.DMA] * 3
    res = pl.pallas_call(
        body, name=name,
        out_shape=[jax.ShapeDtypeStruct((NDEV, D), F32), jax.ShapeDtypeStruct((NDEV * taps.shape[0], taps.shape[1]), F32),
                   jax.ShapeDtypeStruct((NDEV * NDEV, nc), F32)] + rider.out_shapes,
        in_specs=[VMEM] * 4 + [HBM] * n, out_specs=[VMEM] * 3 + [HBM] * n,
        scratch_shapes=[pltpu.VMEM((NDEV * SUBLANES, D), F32), pltpu.VMEM((NDEV, nc), F32)] + small_sems + rider.sems,
        compiler_params=_params(None),
    )(c_rows, taps, ada_w, ada_b, *shards)
    return res[0], res[1], res[2], res[3:]


class _Rider:
    def __init__(self, ins, out_shapes, sems, start, finish, mid=None, relay=None):
        self.ins, self.out_shapes, self.sems = list(ins), list(out_shapes), list(sems)
        self.start, self.finish, self.mid, self.relay = start, finish, mid, relay


def _gather_rider(shards):
    n = len(shards)

    def setup(ins, outs, sems):
        send_sems, recv_sems, local_sems = sems
        x, y, c = _position()
        places = dict(me=(x, y, c), sibling=(x, y, 1 - c), xn=(1 - x, y, c), yn=(x, 1 - y, c), diagonal=(1 - x, 1 - y, c),
                      passed_on=(x ^ c, y ^ (1 - c), c), passed_to=(x ^ (1 - c), y ^ c, c))

        def copy(a, k, block, to, own=False):
            slot = outs[a].at[_index(block)]
            return pltpu.make_async_remote_copy(
                src_ref=ins[a] if own else slot, dst_ref=slot,
                send_sem=send_sems.at[k, a], recv_sem=recv_sems.at[k, a], device_id=to, device_id_type=MESH)

        def local(a):
            return pltpu.make_async_copy(ins[a], outs[a].at[_index(places["me"])], local_sems.at[a])

        return places, copy, local

    def start(ins, outs, sems):
        p, copy, local = setup(ins, outs, sems)
        for a in range(n):
            local(a).start()
            for k, to in enumerate(("sibling", "xn", "yn")):
                copy(a, k, p["me"], p[to], own=True).start()

    def mid(ins, outs, sems):
        p, copy, _ = setup(ins, outs, sems)
        for a in range(n):
            copy(a, 1, p["xn"], p["me"]).wait_recv()
            copy(a, 2, p["yn"], p["me"]).wait_recv()
            copy(a, 3, p["passed_on"], p["passed_to"]).start()
            copy(a, 4, p["xn"], p["sibling"]).start()
            copy(a, 5, p["yn"], p["sibling"]).start()

    def relay(ins, outs, sems):
        p, copy, _ = setup(ins, outs, sems)
        for a in range(n):
            copy(a, 3, p["diagonal"], p["me"]).wait_recv()
            copy(a, 6, p["diagonal"], p["sibling"]).start()

    def finish(ins, outs, sems):
        p, copy, local = setup(ins, outs, sems)
        x, y, c = p["me"]
        for a in range(n):
            for k, block in ((0, (x, y, 1 - c)), (4, (1 - x, y, 1 - c)), (5, (x, 1 - y, 1 - c)), (6, (1 - x, 1 - y, 1 - c))):
                copy(a, k, block, p["me"]).wait_recv()
            for k, to in enumerate(("sibling", "xn", "yn")):
                copy(a, k, p["me"], p[to], own=True).wait_send()
            copy(a, 3, p["passed_on"], p["passed_to"]).wait_send()
            for k, block in ((4, "xn"), (5, "yn"), (6, "diagonal")):
                copy(a, k, p[block], p["sibling"]).wait_send()
            local(a).wait()

    return _Rider(shards, [jax.ShapeDtypeStruct((NDEV, *s.shape), s.dtype) for s in shards],
                  [pltpu.SemaphoreType.DMA((7, n)), pltpu.SemaphoreType.DMA((7, n)), pltpu.SemaphoreType.DMA((n,))],
                  start, finish, mid, relay)


def _pair_rider(parts):
    n = len(parts)

    def copies(ins, outs, sems):
        send_sems, recv_sems = sems
        x, y, c = _position()
        q = 2 * x + y
        return [pltpu.make_async_remote_copy(
            src_ref=ins[a].at[2 * (q ^ k) + (1 - c)], dst_ref=outs[a].at[k],
            send_sem=send_sems.at[k, a], recv_sem=recv_sems.at[k, a], device_id=(x, y, 1 - c), device_id_type=MESH)
            for a in range(n) for k in range(N_CHIPS)]

    def start(ins, outs, sems):
        for cp in copies(ins, outs, sems):
            cp.start()

    def finish(ins, outs, sems):
        for cp in copies(ins, outs, sems):
            cp.wait()

    return _Rider(parts, [jax.ShapeDtypeStruct((N_CHIPS, *p.shape[1:]), p.dtype) for p in parts],
                  [pltpu.SemaphoreType.DMA((N_CHIPS, n)), pltpu.SemaphoreType.DMA((N_CHIPS, n))], start, finish)


NEIGHBOURS = (1, 2)
DIAGONAL = (3,)
OTHER_CHIPS = NEIGHBOURS + DIAGONAL


def _chip_rider(sums, ks=OTHER_CHIPS):
    n = len(sums)

    def copies(ins, outs, sems):
        send_sems, recv_sems = sems
        me = _position()
        return [pltpu.make_async_remote_copy(
            src_ref=ins[a].at[k], dst_ref=outs[a].at[j],
            send_sem=send_sems.at[j, a], recv_sem=recv_sems.at[j, a], device_id=_flip(me, 2 * k), device_id_type=MESH)
            for a in range(n) for j, k in enumerate(ks)]

    def start(ins, outs, sems):
        for cp in copies(ins, outs, sems):
            cp.start()

    def finish(ins, outs, sems):
        for cp in copies(ins, outs, sems):
            cp.wait()

    return _Rider(sums, [jax.ShapeDtypeStruct((len(ks), *s.shape[1:]), s.dtype) for s in sums],
                  [pltpu.SemaphoreType.DMA((len(ks), n)), pltpu.SemaphoreType.DMA((len(ks), n))], start, finish)


def _grid_edge(grid, last):
    cond = None
    for d, n in enumerate(grid):
        here = pl.program_id(d) == (n - 1 if last else 0)
        cond = here if cond is None else jnp.logical_and(cond, here)
    return cond


def _call(name, compute, grid, ins, in_specs, out_shapes, out_specs, scratch_shapes, semantics, rider=None, aliases=None):
    riders = [rider] if isinstance(rider, _Rider) else list(rider or [])
    n_in, n_out, n_scr = len(ins), len(out_shapes), len(scratch_shapes)
    n_rin, n_rout, n_rsem = [sum(len(part(r)) for r in riders) for part in (lambda r: r.ins, lambda r: r.out_shapes, lambda r: r.sems)]
    cuts = [0, n_in, n_in + n_rin, n_in + n_rin + n_out, n_in + n_rin + n_out + n_rout, n_in + n_rin + n_out + n_rout + n_scr]

    def body(*refs):
        in_refs, rin_refs, out_refs, rout_refs, scr_refs = [refs[a:b] for a, b in zip(cuts[:-1], cuts[1:])]
        rsem_refs = refs[cuts[-1]:]
        mine, at = [], [0, 0, 0]
        for r in riders:
            mine.append((r, rin_refs[at[0]:at[0] + len(r.ins)], rout_refs[at[1]:at[1] + len(r.out_shapes)],
                         rsem_refs[at[2]:at[2] + len(r.sems)]))
            at = [at[0] + len(r.ins), at[1] + len(r.out_shapes), at[2] + len(r.sems)]
        if riders:
            @pl.when(_grid_edge(grid, last=False))
            def _():
                for r, a, b, c in mine:
                    r.start(a, b, c)

        if any(r.mid for r in riders):
            step, steps = 0, 1
            for d, size in enumerate(grid):
                step, steps = step * size + pl.program_id(d), steps * size

            @pl.when(step == steps * 5 // 8)
            def _():
                for r, a, b, c in mine:
                    if r.mid:
                        r.mid(a, b, c)

        if any(r.relay for r in riders):
            @pl.when(_grid_edge(grid, last=True))
            def _():
                for r, a, b, c in mine:
                    if r.relay:
                        r.relay(a, b, c)

        compute(in_refs, out_refs, scr_refs)
        if riders:
            @pl.when(_grid_edge(grid, last=True))
            def _():
                for r, a, b, c in mine:
                    r.finish(a, b, c)

    res = pl.pallas_call(
        body, name=name, grid=grid,
        out_shape=list(out_shapes) + [s for r in riders for s in r.out_shapes],
        in_specs=list(in_specs) + [HBM] * n_rin, out_specs=list(out_specs) + [HBM] * n_rout,
        scratch_shapes=list(scratch_shapes) + [s for r in riders for s in r.sems],
        input_output_aliases=aliases or {}, compiler_params=_params(semantics),
    )(*ins, *[a for r in riders for a in r.ins])
    return (res[:n_out], res[n_out:]) if riders else res


def _pair_add(name, parts, from_sibling, slots):
    n = len(parts)

    def body(s_ref, *refs):
        for a in range(n):
            refs[2 * n + a][...] = (refs[a][...].astype(F32) + refs[n + a][...].astype(F32)).astype(refs[2 * n + a].dtype)

    def slab(p, picked):
        _, r, c = p.shape
        return pl.BlockSpec((None, r, c), (lambda k, s: (s[k], 0, 0)) if picked else (lambda k, s: (k, 0, 0)))

    return pl.pallas_call(
        body, name=name,
        grid_spec=pltpu.PrefetchScalarGridSpec(
            num_scalar_prefetch=1, grid=(N_CHIPS,),
            in_specs=[slab(p, True) for p in parts] + [slab(p, False) for p in parts],
            out_specs=[slab(p, False) for p in parts]),
        out_shape=[jax.ShapeDtypeStruct((N_CHIPS, *p.shape[1:]), p.dtype) for p in parts],
        compiler_params=_params(("arbitrary",)),
    )(slots, *parts, *from_sibling)


def _mm(name, pairs, dims, grid, nk, out_shapes, out_specs, extras=(), extra_specs=(), epilogue=None, acc_shape=None, rider=None):
    n_pairs = len(pairs)

    def compute(ins, outs, scratch):
        def partial_sum():
            total = None
            for p in range(n_pairs):
                d = lax.dot_general(ins[2 * p][...], ins[2 * p + 1][...], dims, preferred_element_type=F32)
                total = d if total is None else total + d
            return total

        def finish(r):
            ex = [e[...] for e in ins[2 * n_pairs:]]
            res = epilogue(r, *ex) if epilogue is not None else (r,)
            for o, v in zip(outs, res):
                o[...] = v.astype(o.dtype)

        if nk == 1:
            finish(partial_sum())
        else:
            acc = scratch[0]
            k = pl.program_id(2)

            @pl.when(k == 0)
            def _():
                acc[...] = partial_sum()

            @pl.when(k > 0)
            def _():
                acc[...] += partial_sum()

            @pl.when(k == nk - 1)
            def _():
                finish(acc[...])

    operands, specs = [], []
    for a, a_spec, b, b_spec in pairs:
        operands += [a, b]
        specs += [a_spec, b_spec]
    return _call(name, compute, grid, operands + list(extras), specs + list(extra_specs), out_shapes, out_specs,
                 [pltpu.VMEM(acc_shape, F32)] if nk > 1 else [], ("parallel", "parallel", "arbitrary"), rider)


def _single(res, rider):
    return (res[0][0], res[1]) if rider else res[0]


def _silu(x):
    return x * jax.nn.sigmoid(x)


def _ffn_up(name, h, wg, wu, rider=None, norm=None):
    t = h.shape[0]
    tm = min(t, 1024)
    nb = F // NDEV

    def compute(ins, outs, scr):
        if norm:
            @pl.when(pl.program_id(1) == 0)
            def _():
                scr[0][...] = _rms_mod(ins[0][...], ins[3][...], ins[4][...], ins[5][...]).astype(BF)
                outs[3][...] = scr[0][...]

            hv = scr[0][...]
        else:
            hv = ins[0][...]
        g = jnp.dot(hv, ins[1][...], preferred_element_type=F32)
        u = jnp.dot(hv, ins[2][...], preferred_element_type=F32)
        outs[0][...] = g.astype(BF)
        outs[1][...] = u.astype(BF)
        outs[2][...] = (_silu(g) * u).astype(BF)

    w_spec = pl.BlockSpec((None, D, nb), lambda i, j: (j, 0, 0))
    o_spec = pl.BlockSpec((tm, nb), lambda i, j: (i, j))
    rows = pl.BlockSpec((tm, D), lambda i, j: (i, 0))
    vec = pl.BlockSpec((1, D), lambda i, j: (0, 0))
    return _call(name, compute, (t // tm, NDEV), [h, wg, wu, *(norm or ())], [rows, w_spec, w_spec] + [vec] * (3 if norm else 0),
                 [jax.ShapeDtypeStruct((t, F), BF)] * 3 + ([jax.ShapeDtypeStruct((t, D), BF)] if norm else []),
                 [o_spec] * 3 + ([rows] if norm else []), [pltpu.VMEM((tm, D), BF)] if norm else [],
                 ("parallel", "arbitrary"), rider)


def _mm_nn(name, a, b, tm, tn, tk, extras=(), extra_specs=(), epilogue=None, out_dtypes=(F32,), rider=None):
    m, kk = a.shape
    n = b.shape[1]
    nk = kk // tk
    return _mm(
        name, [(a, pl.BlockSpec((tm, tk), lambda i, j, k: (i, k)), b, pl.BlockSpec((tk, tn), lambda i, j, k: (k, j)))], NN,
        (m // tm, n // tn, nk), nk,
        [jax.ShapeDtypeStruct((m, n), dt) for dt in out_dtypes],
        [pl.BlockSpec((tm, tn), lambda i, j, k: (i, j))] * len(out_dtypes),
        extras, extra_specs, epilogue, (tm, tn), rider)


def _mm_nn_blocked(name, a, b3, tm, rider=None):
    m = a.shape[0]
    nb = b3.shape[2]
    return _single(_mm(
        name, [(a, pl.BlockSpec((tm, D), lambda i, j, k: (i, 0)), b3, pl.BlockSpec((None, D, nb), lambda i, j, k: (j, 0, 0)))], NN,
        (m // tm, NDEV, 1), 1,
        [jax.ShapeDtypeStruct((m, NDEV * nb), F32)], [pl.BlockSpec((tm, nb), lambda i, j, k: (i, j))], rider=rider), rider)


def _mm_nt(name, a, b, tm, tn, out_dtypes=(F32,), extras=(), extra_specs=(), epilogue=None, rider=None):
    m, kk = a.shape
    n = b.shape[0]
    return _mm(
        name, [(a, pl.BlockSpec((tm, kk), lambda i, j, k: (i, 0)), b, pl.BlockSpec((tn, kk), lambda i, j, k: (j, 0)))], NT,
        (m // tm, n // tn, 1), 1,
        [jax.ShapeDtypeStruct((m, n), dt) for dt in out_dtypes],
        [pl.BlockSpec((tm, tn), lambda i, j, k: (i, j))] * len(out_dtypes),
        extras, extra_specs, epilogue, rider=rider)


def _mm_nt_blocked(name, a_list, b3_list, tm, rider=None):
    m = a_list[0].shape[0]
    nb = b3_list[0].shape[2]
    pairs = [(a, pl.BlockSpec((tm, nb), lambda i, j, k: (i, k)), b3, pl.BlockSpec((None, D, nb), lambda i, j, k: (k, 0, 0)))
             for a, b3 in zip(a_list, b3_list)]
    return _single(_mm(name, pairs, NT, (m // tm, 1, NDEV), NDEV,
                       [jax.ShapeDtypeStruct((m, D), F32)], [pl.BlockSpec((tm, D), lambda i, j, k: (i, 0))],
                       acc_shape=(tm, D), rider=rider), rider)


def _mm_tn(name, a, b, tm, tn, rider=None):
    t, m = a.shape
    n = b.shape[1]
    return _single(_mm(
        name, [(a, pl.BlockSpec((t, tm), lambda i, j, k: (0, i)), b, pl.BlockSpec((t, tn), lambda i, j, k: (0, j)))], TN,
        (m // tm, n // tn, 1), 1,
        [jax.ShapeDtypeStruct((m, n), BF)], [pl.BlockSpec((tm, tn), lambda i, j, k: (i, j))], rider=rider), rider)


def _mm_tn_blocked(name, a, b, rider=None):
    t = a.shape[0]
    nb = b.shape[1] // NDEV
    return _single(_mm(
        name, [(a, pl.BlockSpec((t, D), lambda i, j, k: (0, 0)), b, pl.BlockSpec((t, nb), lambda i, j, k: (0, j)))], TN,
        (1, NDEV, 1), 1,
        [jax.ShapeDtypeStruct((NDEV, D, nb), BF)], [pl.BlockSpec((None, D, nb), lambda i, j, k: (j, 0, 0))], rider=rider), rider)


def _dw_gate_up(name, h, dgate, dup, rider=None):
    t = h.shape[0]
    nb = F // NDEV

    def compute(ins, outs, _):
        hv = ins[0][...]
        outs[0][...] = lax.dot_general(hv, ins[1][...], TN, preferred_element_type=F32).astype(BF)
        outs[1][...] = lax.dot_general(hv, ins[2][...], TN, preferred_element_type=F32).astype(BF)

    d_spec = pl.BlockSpec((t, nb), lambda j: (0, j))
    o_spec = pl.BlockSpec((None, D, nb), lambda j: (j, 0, 0))
    return _call(name, compute, (NDEV,), [h, dgate, dup], [pl.BlockSpec((t, D), lambda j: (0, 0)), d_spec, d_spec],
                 [jax.ShapeDtypeStruct((NDEV, D, nb), BF)] * 2, [o_spec] * 2, [], ("arbitrary",), rider)


def _dw_square(name, pairs):
    t = pairs[0][0].shape[0]
    tm = 512
    n = len(pairs)

    def compute(ins, outs, _):
        for p in range(n):
            outs[p][...] = lax.dot_general(ins[2 * p][...], ins[2 * p + 1][...], TN, preferred_element_type=F32).astype(BF)

    return _call(name, compute, (D // tm,), [x for pair in pairs for x in pair],
                 [pl.BlockSpec((t, tm), lambda i: (0, i)), pl.BlockSpec((t, D), lambda i: (0, 0))] * n,
                 [jax.ShapeDtypeStruct((D, D), BF)] * n, [pl.BlockSpec((tm, D), lambda i: (i, 0))] * n, [], ("arbitrary",))


def _swiglu_bwd(da, gate, up):
    gate = gate.astype(F32)
    s = jax.nn.sigmoid(gate)
    return da * up.astype(F32) * (s * (1.0 + gate * (1.0 - s))), da * (gate * s)


def _ffn_dact_dh(name, df, wd, gate, up, wg, wu, rider=None):
    t = df.shape[0]
    tm = min(t, 1024)
    nb = F // NDEV

    def compute(ins, outs, scr):
        acc = scr[0]
        j = pl.program_id(1)
        da = lax.dot_general(ins[0][...], ins[1][...], NT, preferred_element_type=F32)
        dgate, dup = _swiglu_bwd(da, ins[2][...], ins[3][...])
        dgate, dup = dgate.astype(BF), dup.astype(BF)
        outs[0][...] = dgate
        outs[1][...] = dup
        part = (lax.dot_general(dgate, ins[4][...], NT, preferred_element_type=F32)
                + lax.dot_general(dup, ins[5][...], NT, preferred_element_type=F32))

        @pl.when(j == 0)
        def _():
            acc[...] = part

        @pl.when(j > 0)
        def _():
            acc[...] += part

        @pl.when(j == NDEV - 1)
        def _():
            outs[2][...] = acc[...]

    blk = pl.BlockSpec((tm, nb), lambda i, j: (i, j))
    w3 = pl.BlockSpec((None, D, nb), lambda i, j: (j, 0, 0))
    row = pl.BlockSpec((tm, D), lambda i, j: (i, 0))
    return _call(name, compute, (t // tm, NDEV), [df, wd, gate, up, wg, wu],
                 [row, pl.BlockSpec((nb, D), lambda i, j: (j, 0)), blk, blk, w3, w3],
                 [jax.ShapeDtypeStruct((t, F), BF)] * 2 + [jax.ShapeDtypeStruct((t, D), F32)], [blk, blk, row],
                 [pltpu.VMEM((tm, D), F32)], ("parallel", "arbitrary"), rider)


def _rowcall(name, fn, ins, in_specs, n_row_out, out_shapes, out_specs, grid, scratch_shapes=(), rider=None, aliases=None):
    def accumulate(o, v, i):
        @pl.when(i == 0)
        def _():
            o[...] = v.astype(o.dtype)

        @pl.when(i > 0)
        def _():
            o[...] += v.astype(o.dtype)

    def compute(in_refs, out_refs, scr):
        i = pl.program_id(0)
        vals = fn(i, in_refs, scr)
        for idx, (o, v) in enumerate(zip(out_refs, vals)):
            if idx < n_row_out:
                o[...] = v.astype(o.dtype)
            else:
                accumulate(o, v, i)

    return _call(name, compute, (grid,), ins, in_specs, out_shapes, out_specs, list(scratch_shapes), ("arbitrary",), rider, aliases)


def _rows(tr, w=D, cb=0):
    return pl.BlockSpec((tr, w), lambda i: (i, cb))


def _whole(shape):
    nd = len(shape)
    return pl.BlockSpec(shape, lambda i: (0,) * nd)


def _vec(n=1):
    return jax.ShapeDtypeStruct((n, D), F32)


def _rms_mod(x, gain, sc, sh):
    y = x * lax.rsqrt(jnp.mean(x * x, axis=-1, keepdims=True) + EPS)
    return (y * gain) * (1.0 + sc) + sh


def _layer_norm(x, g, b):
    mu = jnp.mean(x, axis=-1, keepdims=True)
    var = jnp.mean(jnp.square(x - mu), axis=-1, keepdims=True)
    return (x - mu) * lax.rsqrt(var + EPS) * g + b


def _gate_grads(dx, f, g, scale):
    return scale * g * dx, jnp.sum(scale * dx * f.astype(F32), axis=0, keepdims=True)


def _norm_mod_bwd(name, x, gain, sc, sh, dh, dres, below=None, rider=None):
    t = x.shape[0]
    tr = min(t, 512)

    def fn(i, r, _):
        _, vjp = jax.vjp(_rms_mod, r[0][...], r[1][...], r[2][...], r[3][...])
        dx, dgain, dsc, dsh = vjp(r[4][...])
        dx = dx + r[5][...]
        if below is None:
            return [dx, dgain, dsc, dsh]
        df, dg = _gate_grads(dx, r[6][...], r[7][...], below[2])
        return [dx, df, dgain, dsc, dsh, dg]

    ins, specs = [x, gain, sc, sh, dh, dres], [_rows(tr)] + [_whole((1, D))] * 3 + [_rows(tr)] * 2
    outs, out_specs = [jax.ShapeDtypeStruct((t, D), F32)], [_rows(tr)]
    if below is not None:
        ins, specs = ins + [below[0], below[1]], specs + [_rows(tr), _whole((1, D))]
        outs, out_specs = outs + [jax.ShapeDtypeStruct((t, D), BF)], out_specs + [_rows(tr)]
    n_vec = 3 if below is None else 4
    return _rowcall(name, fn, ins, specs, len(outs), outs + [_vec()] * n_vec, out_specs + [_whole((1, D))] * n_vec, t // tr,
                    rider=rider)


def _sgu_pre(up, vp, bu, bv, ln_g, ln_b):
    return jax.nn.gelu(up + bu), _layer_norm(jax.nn.gelu(vp + bv), ln_g, ln_b)


def _causal(w_ref, h):
    rows = lax.broadcasted_iota(jnp.int32, (CHUNK, CHUNK), 0)
    cols = lax.broadcasted_iota(jnp.int32, (CHUNK, CHUNK), 1)
    return jnp.where(cols <= rows, w_ref[h], 0.0)


def _sgu(name, proj, b_in, ln_g, ln_b, w_s, bias_full, rider=None):
    t = proj.shape[0]

    def fn(i, r, _):
        u, v = _sgu_pre(r[0][...], r[1][...], r[2][...], r[3][...], r[4][...], r[5][...])
        vb = v.astype(BF)
        mixed = [jnp.dot(_causal(r[6], h).astype(BF), vb[:, h * CHUNK:(h + 1) * CHUNK], preferred_element_type=F32)
                 for h in range(HEADS)]
        return [u * (jnp.concatenate(mixed, axis=1) + r[7][...])]

    return _rowcall(
        name, fn, [proj, proj, b_in, b_in, ln_g, ln_b, w_s, bias_full],
        [_rows(CHUNK, D, 0), _rows(CHUNK, D, 1), pl.BlockSpec((1, D), lambda i: (0, 0)), pl.BlockSpec((1, D), lambda i: (0, 1)),
         _whole((1, D)), _whole((1, D)), _whole((HEADS, CHUNK, CHUNK)), _whole((CHUNK, D))],
        1, [jax.ShapeDtypeStruct((t, D), BF)], [_rows(CHUNK)], t // CHUNK, rider=rider)


def _sgu_bwd(name, proj, b_in, ln_g, ln_b, w_s, bias_full, dout, dproj, rider=None):
    t = proj.shape[0]

    def fn(i, r, _):
        (u, v), vjp = jax.vjp(_sgu_pre, r[0][...], r[1][...], r[2][...], r[3][...], r[4][...], r[5][...])
        vb = v.astype(BF)
        d = r[8][...]
        masks = [_causal(r[6], h).astype(BF) for h in range(HEADS)]
        cols = [slice(h * CHUNK, (h + 1) * CHUNK) for h in range(HEADS)]
        mixed = jnp.concatenate([jnp.dot(masks[h], vb[:, cols[h]], preferred_element_type=F32) for h in range(HEADS)], axis=1)
        du = d * (mixed + r[7][...])
        dmix = d * u
        dmb = dmix.astype(BF)
        dv = jnp.concatenate([lax.dot_general(masks[h], dmb[:, cols[h]], TN, preferred_element_type=F32) for h in range(HEADS)], axis=1)
        rows = lax.broadcasted_iota(jnp.int32, (CHUNK, CHUNK), 0)
        lanes = lax.broadcasted_iota(jnp.int32, (CHUNK, CHUNK), 1)
        dws = jnp.stack([jnp.where(lanes <= rows, lax.dot_general(dmb[:, cols[h]], vb[:, cols[h]], NT, preferred_element_type=F32), 0.0)
                         for h in range(HEADS)])
        dbs = jnp.zeros((CHUNK, CHUNK), F32)
        for h in range(HEADS):
            dbs = dbs + jnp.where(lanes == h, jnp.sum(dmix[:, cols[h]], axis=1, keepdims=True), 0.0)
        dup, dvp, dbu, dbv, dg, db = vjp((du, dv))
        return [jnp.concatenate([dup, dvp], axis=1), dbu, dbv, dg, db, dws, dbs]

    return _rowcall(
        name, fn, [proj, proj, b_in, b_in, ln_g, ln_b, w_s, bias_full, dout, dproj],
        [_rows(CHUNK, D, 0), _rows(CHUNK, D, 1), pl.BlockSpec((1, D), lambda i: (0, 0)), pl.BlockSpec((1, D), lambda i: (0, 1)),
         _whole((1, D)), _whole((1, D)), _whole((HEADS, CHUNK, CHUNK)), _whole((CHUNK, D)), _rows(CHUNK),
         pl.BlockSpec(memory_space=pl.ANY)],
        1, [jax.ShapeDtypeStruct(dproj.shape, dproj.dtype)] + [_vec()] * 4
        + [jax.ShapeDtypeStruct((HEADS, CHUNK, CHUNK), F32), jax.ShapeDtypeStruct((CHUNK, CHUNK), F32)],
        [pl.BlockSpec((CHUNK, 2 * D), lambda i: (i, 0))] + [_whole((1, D))] * 4 + [_whole((HEADS, CHUNK, CHUNK)), _whole((CHUNK, CHUNK))],
        t // CHUNK, rider=rider, aliases={9: 0})


def _halo_before(tr, cb):
    return pl.BlockSpec((HALO, D), lambda i: (jnp.maximum(i * (tr // HALO) - 1, 0), cb))


def _halo_after(tr, cb, n_tiles):
    return pl.BlockSpec((HALO, D), lambda i: (jnp.minimum((i + 1) * (tr // HALO), n_tiles * (tr // HALO) - 1), cb))


def _ln_silu(z, g, b):
    return _silu(_layer_norm(z, g, b))


SUBLANES = 8
LANES = 128
CONV_STRIP = 16
DW_STRIP = 32


def _shifted_copies(buf, copies, rows):
    for b in range(1, SUBLANES):
        copies[b - 1, pl.ds(0, rows), :] = buf[pl.ds(b, rows), :]


def _shifted(buf, copies, offset, start, rows, lanes=slice(None)):
    at = pl.ds(pl.multiple_of(start + SUBLANES * (offset // SUBLANES), SUBLANES), rows)
    return buf[at, lanes] if offset % SUBLANES == 0 else copies[offset % SUBLANES - 1, at, lanes]


def _accumulate(o, v, i):
    @pl.when(i == 0)
    def _():
        o[...] = v.astype(o.dtype)

    @pl.when(i > 0)
    def _():
        o[...] += v.astype(o.dtype)


def _conv(name, proj, b_in, conv_w, conv_b, ln_g, ln_b, rider=None):
    t = proj.shape[0]
    tr = min(t, 256)

    def compute(r, outs, scr):
        zbuf, zs = scr
        i = pl.program_id(0)
        bv, bg = r[4][...], r[5][...]
        z0 = (r[0][...] + bv) * jax.nn.sigmoid(r[1][...] + bg)
        before = (r[2][...] + bv) * jax.nn.sigmoid(r[3][...] + bg)
        zbuf[pl.ds(0, HALO), :] = jnp.where(i > 0, before, 0.0)
        zbuf[pl.ds(HALO, tr), :] = z0
        outs[0][...] = z0
        _shifted_copies(zbuf, zs, tr + HALO - SUBLANES)

        def strip(s, carry):
            r0 = s * CONV_STRIP
            acc = jnp.zeros((CONV_STRIP, D), F32) + r[7][...]
            for k in range(KW):
                acc = acc + r[6][k:k + 1, :] * _shifted(zbuf, zs, HALO - (KW - 1) + k, r0, CONV_STRIP)
            outs[1][pl.ds(pl.multiple_of(r0, SUBLANES), CONV_STRIP), :] = acc
            return carry

        lax.fori_loop(0, tr // CONV_STRIP, strip, 0)
        outs[2][...] = _ln_silu(outs[1][...], r[8][...], r[9][...]).astype(BF)

    return _call(
        name, compute, (t // tr,), [proj, proj, proj, proj, b_in, b_in, conv_w, conv_b, ln_g, ln_b],
        [_rows(tr, D, 2), _rows(tr, D, 3), _halo_before(tr, 2), _halo_before(tr, 3),
         pl.BlockSpec((1, D), lambda i: (0, 2)), pl.BlockSpec((1, D), lambda i: (0, 3)),
         _whole((HALO, D)), _whole((1, D)), _whole((1, D)), _whole((1, D))],
        [jax.ShapeDtypeStruct((t, D), F32), jax.ShapeDtypeStruct((t, D), F32), jax.ShapeDtypeStruct((t, D), BF)],
        [_rows(tr)] * 3, [pltpu.VMEM((tr + HALO, D), F32), pltpu.VMEM((SUBLANES - 1, tr + HALO, D), F32)], ("arbitrary",), rider)


def _conv_bwd(name, proj, b_in, conv_w, ln_g, ln_b, z0, z1, dz3, dproj, rider=None):
    t = proj.shape[0]
    tr = min(t, 256)
    n_tiles = t // tr

    def compute(r, outs, scr):
        zbuf, dbuf, zs, ds, dwacc = scr
        i = pl.program_id(0)
        g, b = r[5][...], r[6][...]
        zero_row = jnp.zeros((1, D), F32)
        _, vjp = jax.vjp(_ln_silu, r[9][...], g, b)
        dz1, dg, db = vjp(r[11][...])
        dcb = jnp.sum(dz1, axis=0, keepdims=True)
        _, vjp_after = jax.vjp(_ln_silu, r[10][...], g, b)
        dz1_after = vjp_after(r[12][...])[0]
        dbuf[pl.ds(0, tr), :] = dz1
        dbuf[pl.ds(tr, HALO), :] = jnp.where(i < n_tiles - 1, dz1_after, 0.0)
        zbuf[pl.ds(0, HALO), :] = jnp.where(i > 0, r[8][...], 0.0)
        zbuf[pl.ds(HALO, tr), :] = r[7][...]
        _shifted_copies(dbuf, ds, tr + HALO - SUBLANES)
        _shifted_copies(zbuf, zs, tr + HALO - SUBLANES)

        def dz0_strip(s, carry):
            r0 = s * CONV_STRIP
            at = pl.ds(pl.multiple_of(r0, CONV_STRIP), CONV_STRIP)
            acc = jnp.zeros((CONV_STRIP, D), F32)
            for k in range(KW):
                acc = acc + r[4][k:k + 1, :] * _shifted(dbuf, ds, KW - 1 - k, r0, CONV_STRIP)
            a = r[0][at, :] + r[2][...]
            sg = jax.nn.sigmoid(r[1][at, :] + r[3][...])
            dcv = acc * sg
            dcg = acc * a * sg * (1.0 - sg)
            outs[0][at, :] = jnp.concatenate([dcv, dcg], axis=1).astype(BF)
            return carry[0] + jnp.sum(dcv, axis=0, keepdims=True), carry[1] + jnp.sum(dcg, axis=0, keepdims=True)

        dbv, dbg = lax.fori_loop(0, tr // CONV_STRIP, dz0_strip, (zero_row, zero_row))

        for lb in range(D // LANES):
            lanes = slice(lb * LANES, (lb + 1) * LANES)

            def dw_strip(s, accs, lanes=lanes):
                r0 = s * DW_STRIP
                dz = dbuf[pl.ds(pl.multiple_of(r0, SUBLANES), DW_STRIP), lanes]
                out = []
                for k in range(KW):
                    prod = dz * _shifted(zbuf, zs, HALO - (KW - 1) + k, r0, DW_STRIP, lanes)
                    part = prod[0:SUBLANES]
                    for q in range(1, DW_STRIP // SUBLANES):
                        part = part + prod[q * SUBLANES:(q + 1) * SUBLANES]
                    out.append(accs[k] + part)
                return tuple(out)

            accs = lax.fori_loop(0, tr // DW_STRIP, dw_strip, tuple(jnp.zeros((SUBLANES, LANES), F32) for _ in range(KW)))
            for k in range(KW):
                dwacc[pl.ds(k * SUBLANES, SUBLANES), lanes] = accs[k]
        dw_rows = [jnp.sum(dwacc[pl.ds(k * SUBLANES, SUBLANES), :], axis=0, keepdims=True) for k in range(KW)]
        dw_rows.append(jnp.zeros((HALO - KW, D), F32))
        for o, v in zip(outs[1:], (dbv, dbg, jnp.concatenate(dw_rows, axis=0), dcb, dg, db)):
            _accumulate(o, v, i)

    wide = pl.BlockSpec((tr, 2 * D), lambda i: (i, 1))
    return _call(
        name, compute, (n_tiles,), [proj, proj, b_in, b_in, conv_w, ln_g, ln_b, z0, z0, z1, z1, dz3, dz3, dproj],
        [_rows(tr, D, 2), _rows(tr, D, 3), pl.BlockSpec((1, D), lambda i: (0, 2)), pl.BlockSpec((1, D), lambda i: (0, 3)),
         _whole((HALO, D)), _whole((1, D)), _whole((1, D)),
         _rows(tr), _halo_before(tr, 0), _rows(tr), _halo_after(tr, 0, n_tiles), _rows(tr), _halo_after(tr, 0, n_tiles),
         pl.BlockSpec(memory_space=pl.ANY)],
        [jax.ShapeDtypeStruct(dproj.shape, dproj.dtype), _vec(), _vec(), _vec(HALO), _vec(), _vec(), _vec()],
        [wide] + [_whole((1, D))] * 2 + [_whole((HALO, D))] + [_whole((1, D))] * 3,
        [pltpu.VMEM((tr + HALO, D), F32), pltpu.VMEM((tr + HALO, D), F32),
         pltpu.VMEM((SUBLANES - 1, tr + HALO, D), F32), pltpu.VMEM((SUBLANES - 1, tr + HALO, D), F32),
         pltpu.VMEM((HALO * SUBLANES, D), F32)],
        ("arbitrary",), rider, aliases={13: 0})


def _merge_fn(ga, gb, bga, bgb, ya, yb):
    return jax.nn.sigmoid(ga + bga) * ya + jax.nn.sigmoid(gb + bgb) * yb


def _mix_tail(name, ua, z3, proj, b_in, wa, wb, wo, x, g, next_norm):
    t = ua.shape[0]
    tr = min(t, 256)

    def compute(r, outs, _):
        ya = jnp.dot(r[0][...], r[6][...], preferred_element_type=F32)
        yb = jnp.dot(r[1][...], r[7][...], preferred_element_type=F32)
        merged = _merge_fn(r[2][...], r[3][...], r[4][...], r[5][...], ya, yb).astype(BF)
        y = jnp.dot(merged, r[8][...], preferred_element_type=F32)
        x_out = r[9][...] + r[10][...] * y
        for o, v in zip(outs, (ya, yb, merged, y, x_out, _rms_mod(x_out, r[11][...], r[12][...], r[13][...]))):
            o[...] = v.astype(o.dtype)

    row = _whole((1, D))
    return _call(
        name, compute, (t // tr,), [ua, z3, proj, proj, b_in, b_in, wa, wb, wo, x, g, *next_norm],
        [_rows(tr), _rows(tr), _rows(tr, D, 4), _rows(tr, D, 5), pl.BlockSpec((1, D), lambda i: (0, 4)),
         pl.BlockSpec((1, D), lambda i: (0, 5)), _whole((D, D)), _whole((D, D)), _whole((D, D)), _rows(tr), row, row, row, row],
        [jax.ShapeDtypeStruct((t, D), dt) for dt in (F32, F32, BF, BF, F32, BF)], [_rows(tr)] * 6, [], ("arbitrary",))


def _mix_tail_bwd(name, dy, proj, b_in, ya, yb, wa, wb, wo, rider=None):
    t = proj.shape[0]
    tr = min(t, 256)

    def compute(r, outs, _):
        i = pl.program_id(0)
        dm = lax.dot_general(r[0][...], r[9][...], NT, preferred_element_type=F32)
        _, vjp = jax.vjp(_merge_fn, *[x[...] for x in r[1:7]])
        dga, dgb, dbga, dbgb, dya, dyb = vjp(dm)
        dya, dyb = dya.astype(BF), dyb.astype(BF)
        outs[0][...] = jnp.concatenate([dga, dgb], axis=1).astype(BF)
        outs[1][...] = dya
        outs[2][...] = dyb
        outs[3][...] = lax.dot_general(dya, r[7][...], NT, preferred_element_type=F32)
        outs[4][...] = lax.dot_general(dyb, r[8][...], NT, preferred_element_type=F32)
        _accumulate(outs[5], dbga, i)
        _accumulate(outs[6], dbgb, i)

    return _call(
        name, compute, (t // tr,), [dy, proj, proj, b_in, b_in, ya, yb, wa, wb, wo],
        [_rows(tr), _rows(tr, D, 4), _rows(tr, D, 5), pl.BlockSpec((1, D), lambda i: (0, 4)), pl.BlockSpec((1, D), lambda i: (0, 5)),
         _rows(tr), _rows(tr), _whole((D, D)), _whole((D, D)), _whole((D, D))],
        [jax.ShapeDtypeStruct((t, D_IN), BF)] + [jax.ShapeDtypeStruct((t, D), BF)] * 2 + [jax.ShapeDtypeStruct((t, D), F32)] * 2
        + [_vec(), _vec()],
        [pl.BlockSpec((tr, 2 * D), lambda i: (i, 2))] + [_rows(tr)] * 4 + [_whole((1, D))] * 2, [], ("arbitrary",), rider)


def _loss_head(name, x, gain, target, f, g, scale):
    t = x.shape[0]
    tr = min(t, 512)

    def loss_fn(xv, gn, tgt):
        y = xv * lax.rsqrt(jnp.mean(xv * xv, axis=-1, keepdims=True) + EPS) * gn
        return 0.5 * jnp.sum(jnp.mean(jnp.square(y - tgt), axis=-1))

    def fn(i, r, _):
        loss, vjp = jax.vjp(loss_fn, r[0][...], r[1][...], r[2][...])
        dx, dgain, _ = vjp(jnp.ones((), F32))
        df, dg = _gate_grads(dx, r[3][...], r[4][...], scale)
        return [dx, df, dgain, jnp.zeros((1, D), F32) + loss, dg]

    return _rowcall(name, fn, [x, gain, target, f, g], [_rows(tr), _whole((1, D)), _rows(tr), _rows(tr), _whole((1, D))], 2,
                    [jax.ShapeDtypeStruct((t, D), F32), jax.ShapeDtypeStruct((t, D), BF), _vec(), _vec(), _vec()],
                    [_rows(tr)] * 2 + [_whole((1, D))] * 3, t // tr)


def _adamw(w, g, m, v):
    m = B1 * m + (1.0 - B1) * g
    v = B2 * v + (1.0 - B2) * jnp.square(g)
    m_hat = m / BC1
    v_hat = v / BC2
    delta = -LR * (m_hat / (jnp.sqrt(v_hat) + ADAM_EPS) + WD * w)
    return delta, m, v


ADAMW_ROWS = 64


def _adamw_group(name, items, rider=None, rows=ADAMW_ROWS):
    ins, in_specs, out_shapes, out_specs, plan = [], [], [], [], []
    first = 0
    for chip_sum, received, w, m, v in items:
        r, c = w.shape
        tr = min(r, rows)
        n = r // tr

        def tile(i, first=first, n=n):
            return jnp.clip(i - first, 0, n - 1)

        spec = pl.BlockSpec((tr, c), lambda i, tile=tile: (tile(i), 0))
        ins += [chip_sum, *received, w, m, v]
        in_specs += [pl.BlockSpec((None, tr, c), lambda i, tile=tile: (0, tile(i), 0))]
        in_specs += [pl.BlockSpec((g.shape[0], tr, c), lambda i, tile=tile: (0, tile(i), 0)) for g in received]
        in_specs += [spec] * 3
        out_shapes += [jax.ShapeDtypeStruct((r, c), F32)] * 4
        out_specs += [spec] * 4
        plan.append((first, n, [g.shape[0] for g in received]))
        first += n

    def compute(in_refs, out_refs, _):
        i = pl.program_id(0)
        at_in = at_out = 0
        for start, n, counts in plan:
            mine = in_refs[at_in:at_in + 4 + len(counts)]
            outs = out_refs[at_out:at_out + 4]
            at_in += 4 + len(counts)
            at_out += 4

            @pl.when(jnp.logical_and(i >= start, i < start + n))
            def _(mine=mine, outs=outs, counts=counts):
                g = mine[0][...].astype(F32)
                for j, count in enumerate(counts):
                    for s in range(count):
                        g = g + mine[1 + j][s].astype(F32)
                delta, m_new, v_new = _adamw(mine[-3][...], g, mine[-2][...], mine[-1][...])
                for o, val in zip(outs, (g, delta, m_new, v_new)):
                    o[...] = val

    res = _call(name, compute, (first,), ins, in_specs, out_shapes, out_specs, [], ("arbitrary",), rider)
    outs, rode = res if rider else (res, [])
    return [outs[4 * j:4 * j + 4] for j in range(len(items))], rode


def _adamw_small(name, packed_all, late_all, dws_all, vectors, w_s):
    n_vec = len(vectors)

    def body(*refs):
        p_ref, l_ref, d_ref = refs[:3]
        param_refs = refs[3:3 + 3 * n_vec + 3]
        out_refs = refs[3 + 3 * n_vec + 3:-1]
        g_ref = refs[-1]
        g = p_ref[0]
        late = l_ref[0]
        for s in range(1, NDEV):
            g = g + p_ref[s]
            late = late + l_ref[s]
        g_ref[...] = g
        g_ref[pl.ds(0, R_LATE), :] += late

        def update(gp, wmv, outs):
            delta, m_new, v_new = _adamw(wmv[0][...], gp, wmv[1][...], wmv[2][...])
            for o, val in zip(outs, (gp, delta, m_new, v_new)):
                o[...] = val

        for j, (row, rows, *_) in enumerate(vectors):
            pieces = [g_ref[pl.ds(row + r, 1), :] for r in range(rows)]
            update(pieces[0] if rows == 1 else jnp.concatenate(pieces, axis=1), param_refs[3 * j:3 * j + 3], out_refs[4 * j:4 * j + 4])
        gw = d_ref[0]
        for s in range(1, NDEV):
            gw = gw + d_ref[s]
        update(gw, param_refs[3 * n_vec:], out_refs[4 * n_vec:4 * n_vec + 4])
        out_refs[-2][...] = g_ref[pl.ds(R_CW, KW), :]
        out_refs[-1][...] = g_ref[pl.ds(R_LOSS, 1), :]

    params = [a for _, _, w, m, v in vectors for a in (w, m, v)] + list(w_s)
    out_shapes = [jax.ShapeDtypeStruct(w.shape, F32) for _, _, w, _, _ in vectors for _ in range(4)]
    out_shapes += [jax.ShapeDtypeStruct(w_s[0].shape, F32)] * 4 + [jax.ShapeDtypeStruct((KW, D), F32), _vec()]
    res = pl.pallas_call(body, name=name, out_shape=out_shapes, scratch_shapes=[pltpu.VMEM((R_TOTAL, D), F32)],
                         compiler_params=_params(None))(packed_all, late_all, dws_all, *params)
    return [res[4 * j:4 * j + 4] for j in range(n_vec + 1)], res[-2], res[-1]


def _adamw_plain(name, g, w, m, v):
    def body(g_ref, w_ref, m_ref, v_ref, d_ref, mo_ref, vo_ref):
        delta, m_new, v_new = _adamw(w_ref[...], g_ref[...], m_ref[...], v_ref[...])
        d_ref[...] = delta
        mo_ref[...] = m_new
        vo_ref[...] = v_new

    return pl.pallas_call(body, name=name, out_shape=[jax.ShapeDtypeStruct(w.shape, F32)] * 3,
                          compiler_params=_params(None))(g, w, m, v)


def _adamw_ada(name, c_all_t, dmod, dmod_late, w, m, v):
    r, c = w.shape
    tr = 256

    def fn(i, refs, _):
        ca = _silu(refs[0][...])
        dm = refs[1][...] + refs[2][...]
        g = ca[:, 0:1] * dm[0:1, :]
        for b in range(1, NDEV):
            g = g + ca[:, b:b + 1] * dm[b:b + 1, :]
        delta, m_new, v_new = _adamw(refs[3][...], g, refs[4][...], refs[5][...])
        return [g, delta, m_new, v_new]

    spec = pl.BlockSpec((tr, c), lambda i: (i, 0))
    whole = pl.BlockSpec((NDEV, c), lambda i: (0, 0))
    return _rowcall(name, fn, [c_all_t, dmod, dmod_late, w, m, v],
                    [pl.BlockSpec((tr, NDEV), lambda i: (i, 0)), whole, whole, spec, spec, spec], 4,
                    [jax.ShapeDtypeStruct((r, c), F32)] * 4, [spec] * 4, r // tr)


def _ffn_fwd(tag, x, h, g, wg, wu, wd_shard, down_rider, next_norm=None, more_shards=(), norm=None):
    t = x.shape[0]
    tm = min(t, 512 if down_rider else 1024)
    (gate, up, act, *normed), (wd, *more) = _ffn_up(f"{tag}_up", x if norm else h, wg, wu,
                                                    rider=_gather_rider([wd_shard, *more_shards]), norm=norm)
    h = normed[0] if norm else h
    row = pl.BlockSpec((1, D), lambda i, j, k: (0, 0))

    def epilogue(f, xv, gv, *norm):
        x_out = xv + 0.5 * gv * f
        return (x_out, f, _rms_mod(x_out, *norm)) if norm else (x_out, f)

    res = _mm_nn(f"{tag}_down", act, wd.reshape(F, D), tm, D, 1024, extras=(x, g, *(next_norm or ())),
                 extra_specs=(pl.BlockSpec((tm, D), lambda i, j, k: (i, 0)), row, *([row] * 3 if next_norm else [])),
                 epilogue=epilogue, out_dtypes=(F32, BF, BF) if next_norm else (F32, BF), rider=down_rider)
    (x_out, f, *h_next), rode = res if down_rider else (res, None)
    return x_out, (h_next[0] if next_norm else None), (x, h, gate, up, act, f), wd, rode, more


def _ffn_bwd(tag, dx_out, df, saved, gain, sh, sc, wg, wu, wd, slots, dact_rider=None, dwd_rider=None, dwgu_rider=None,
             below=None, fuse_dh=False):
    x, h, gate, up, act, f = saved
    t = x.shape[0]
    tm = min(t, 1024)
    if fuse_dh:
        dgate, dup, dh = _ffn_dact_dh(f"{tag}_dact_dh", df, wd.reshape(F, D), gate, up, wg, wu)
        dwd = _mm_tn(f"{tag}_dwd", act, df, 512, D).reshape(NDEV, F // NDEV, D)
        (dwg, dwu), (sib_d,) = _dw_gate_up(f"{tag}_dwgu", h, dgate, dup, rider=_pair_rider([dwd]))
        (sum_d,) = _pair_add(f"{tag}_dwd_add", [dwd], [sib_d], slots)
        normed, (sib_g, sib_u) = _norm_mod_bwd(f"{tag}_norm_bwd", x, gain, sc, sh, dh, dx_out, below=below,
                                               rider=_pair_rider([dwg, dwu]))
        sum_g, sum_u = _pair_add(f"{tag}_dwgu_add", [dwg, dwu], [sib_g, sib_u], slots)
        return normed, (sum_d, None), sum_g, sum_u, [], [], []

    blk = pl.BlockSpec((t, F // NDEV), lambda i, j, k: (i, j))
    res = _mm_nt(f"{tag}_dact", df, wd.reshape(F, D), t, F // NDEV, out_dtypes=(BF, BF),
                 extras=(gate, up), extra_specs=(blk, blk), epilogue=_swiglu_bwd, rider=dact_rider)
    (dgate, dup), rode_dact = res if dact_rider else (res, [])
    res = _mm_tn(f"{tag}_dwd", act, df, 512, D, rider=dwd_rider)
    dwd, rode_dwd = res if dwd_rider else (res, [])
    dwd = dwd.reshape(NDEV, F // NDEV, D)
    (dwg, dwu), (sib_d, *rode_dwgu) = _dw_gate_up(f"{tag}_dwgu", h, dgate, dup,
                                                  rider=[_pair_rider([dwd])] + ([dwgu_rider] if dwgu_rider else []))
    (sum_d,) = _pair_add(f"{tag}_dwd_add", [dwd], [sib_d], slots)
    dh, (sib_g, sib_u, got_d) = _mm_nt_blocked(f"{tag}_dh", [dgate, dup], [wg, wu], tm,
                                               rider=[_pair_rider([dwg, dwu]), _chip_rider([sum_d])])
    sum_g, sum_u = _pair_add(f"{tag}_dwgu_add", [dwg, dwu], [sib_g, sib_u], slots)
    normed = _norm_mod_bwd(f"{tag}_norm_bwd", x, gain, sc, sh, dh, dx_out, below=below)
    return normed, (sum_d, [got_d]), sum_g, sum_u, rode_dact, rode_dwd, rode_dwgu


def kernel(x, c, ada_w, ada_b, norm_ffn1, ffn1_w_gate, ffn1_w_up, ffn1_w_down, norm_mix, mix_w_in, mix_b_in, sgu_ln_g, sgu_ln_b, sgu_w_s, sgu_b_s, conv_w, conv_b, conv_ln_g, conv_ln_b, w_branch_a, w_branch_b, w_out, norm_ffn2, ffn2_w_gate, ffn2_w_up, ffn2_w_down, norm_final, loss_target, m_ada_w, m_ada_b, m_norm_ffn1, m_ffn1_w_gate, m_ffn1_w_up, m_ffn1_w_down, m_norm_mix, m_mix_w_in, m_mix_b_in, m_sgu_ln_g, m_sgu_ln_b, m_sgu_w_s, m_sgu_b_s, m_conv_w, m_conv_b, m_conv_ln_g, m_conv_ln_b, m_w_branch_a, m_w_branch_b, m_w_out, m_norm_ffn2, m_ffn2_w_gate, m_ffn2_w_up, m_ffn2_w_down, m_norm_final, v_ada_w, v_ada_b, v_norm_ffn1, v_ffn1_w_gate, v_ffn1_w_up, v_ffn1_w_down, v_norm_mix, v_mix_w_in, v_mix_b_in, v_sgu_ln_g, v_sgu_ln_b, v_sgu_w_s, v_sgu_b_s, v_conv_w, v_conv_b, v_conv_ln_g, v_conv_ln_b, v_w_branch_a, v_w_branch_b, v_w_out, v_norm_ffn2, v_ffn2_w_gate, v_ffn2_w_up, v_ffn2_w_down, v_norm_final):
    mx, my, mc = _position()
    me = 4 * mx + 2 * my + mc
    chip = 2 * mx + my
    slots = jnp.stack([2 * (chip ^ k) + mc for k in range(N_CHIPS)]).astype(jnp.int32)
    t = x.shape[1]
    tm = min(t, 1024)
    x0 = x.reshape(t, D)
    target = loss_target.reshape(t, D)
    given = dict(ffn1_w_gate=(ffn1_w_gate, m_ffn1_w_gate, v_ffn1_w_gate), ffn1_w_up=(ffn1_w_up, m_ffn1_w_up, v_ffn1_w_up),
                 ffn1_w_down=(ffn1_w_down, m_ffn1_w_down, v_ffn1_w_down), mix_w_in=(mix_w_in, m_mix_w_in, v_mix_w_in),
                 w_branch_a=(w_branch_a, m_w_branch_a, v_w_branch_a), w_branch_b=(w_branch_b, m_w_branch_b, v_w_branch_b),
                 w_out=(w_out, m_w_out, v_w_out), ffn2_w_gate=(ffn2_w_gate, m_ffn2_w_gate, v_ffn2_w_gate),
                 ffn2_w_up=(ffn2_w_up, m_ffn2_w_up, v_ffn2_w_up), ffn2_w_down=(ffn2_w_down, m_ffn2_w_down, v_ffn2_w_down))
    shard = {n: wmv[0][0].astype(BF) for n, wmv in given.items()}

    ada_cols = N_MOD * D // NDEV
    c_all, taps_all, mod_all, (wg1, wu1) = _prologue(
        "prologue", jnp.pad(c, ((0, SUBLANES - 1), (0, 0))), jnp.pad(conv_w[0], ((0, HALO - KW), (0, 0))), ada_w[0],
        lax.dynamic_slice(ada_b, (0, me * ada_cols), (1, ada_cols)), [shard["ffn1_w_gate"], shard["ffn1_w_up"]])
    conv_w_full = jnp.transpose(taps_all.reshape(NDEV, HALO, CHUNK), (1, 0, 2)).reshape(HALO, D)
    mod = lax.dynamic_index_in_dim(mod_all.reshape(NDEV, NDEV, ada_cols), me, axis=1, keepdims=False).reshape(N_MOD, 1, D)
    sh1, sc1, g1, sh2, sc2, g2, sh3, sc3, g3 = [mod[i] for i in range(N_MOD)]

    x1, h2, saved1, wd1, (w_in,), (wa3, wb3) = _ffn_fwd(
        "ffn1", x0, None, g1, wg1, wu1, shard["ffn1_w_down"], _gather_rider([shard["mix_w_in"]]),
        next_norm=(norm_mix, sc2, sh2), more_shards=(shard["w_branch_a"], shard["w_branch_b"]), norm=(norm_ffn1, sc1, sh1))
    proj, (wg2, wo3) = _mm_nn_blocked("mix_in", h2, w_in, tm, rider=_gather_rider([shard["ffn2_w_gate"], shard["w_out"]]))
    bias_full = jnp.repeat(sgu_b_s[0].T, CHUNK, axis=1)
    (ua,) = _sgu("sgu", proj, mix_b_in, sgu_ln_g, sgu_ln_b, sgu_w_s[0], bias_full)
    (z0, z1, z3), (wu2,) = _conv("conv", proj, mix_b_in, conv_w_full, conv_b, conv_ln_g, conv_ln_b,
                                 rider=_gather_rider([shard["ffn2_w_up"]]))
    wa, wb, wo = wa3.reshape(D, D), wb3.reshape(D, D), wo3.reshape(D, D)
    ya, yb, merged, y, x2, h3 = _mix_tail("mix_tail", ua, z3, proj, mix_b_in, wa, wb, wo, x1, g2, (norm_ffn2, sc3, sh3))
    x3, _, saved3, wd2, _, _ = _ffn_fwd("ffn2", x2, h3, g3, wg2, wu2, shard["ffn2_w_down"], None)

    norm_final2 = norm_final.reshape(1, D)
    dx3, df3, d_norm_final, loss_row, dg3 = _loss_head("loss_head", x3, norm_final2, target, saved3[-1], g3, 0.5)
    (dx2, dy, d_norm_ffn2, dsc3, dsh3, dg2), (sum_d2, _), sum_g2, sum_u2, _, _, _ = _ffn_bwd(
        "ffn2", dx3, df3, saved3, norm_ffn2, sh3, sc3, wg2, wu2, wd2, slots, below=(y, g2, 1.0), fuse_dh=True)
    (dproj, dya, dyb, dua, dz3, db_ga, db_gb), (got_g2_near,) = _mix_tail_bwd(
        "mix_tail_bwd", dy, proj, mix_b_in, ya, yb, wa, wb, wo, rider=_chip_rider([sum_g2], NEIGHBOURS))
    dwo, dwa, dwb = [g.reshape(NDEV, D // NDEV, D) for g in _dw_square("mix_dw", [(merged, dy), (ua, dya), (z3, dyb)])]
    (dproj, db_u, db_v, d_sgu_g, d_sgu_b, d_ws, d_bs_t), (*sib_abo, got_g2_far) = _sgu_bwd(
        "sgu_bwd", proj, mix_b_in, sgu_ln_g, sgu_ln_b, sgu_w_s[0], bias_full, dua, dproj,
        rider=[_pair_rider([dwa, dwb, dwo]), _chip_rider([sum_g2], DIAGONAL)])
    sum_a, sum_b, sum_o = _pair_add("mix_dw_add", [dwa, dwb, dwo], sib_abo, slots)
    (dproj, db_cv, db_cg, d_cw, d_cb, d_cln_g, d_cln_b), (got_u2, got_d2) = _conv_bwd(
        "conv_bwd", proj, mix_b_in, conv_w_full, conv_ln_g, conv_ln_b, z0, z1, dz3, dproj, rider=_chip_rider([sum_u2, sum_d2]))
    dwin, (got_a, got_b, got_o) = _mm_tn_blocked("mix_dwin", h2, dproj, rider=_chip_rider([sum_a, sum_b, sum_o]))

    d_bs = jnp.transpose(d_bs_t[:, :HEADS])
    zero = jnp.zeros((1, D), F32)
    pack_rows = [zero, zero, zero, zero, zero, dg2, dsh3, dsc3, dg3,
                 zero, zero, d_norm_ffn2, d_norm_final,
                 db_u, db_v, db_cv, db_cg, db_ga, db_gb,
                 d_sgu_g, d_sgu_b, d_bs.reshape(1, D), d_cb, d_cln_g, d_cln_b,
                 d_cw[:KW], loss_row, jnp.zeros((R_TOTAL - R_LOSS - 1, D), F32)]
    packed = jnp.concatenate(pack_rows, axis=0)
    d_ws2 = d_ws.reshape(HEADS * CHUNK, CHUNK)
    dh2, (sib_in, packed_all, dws_all) = _mm_nt_blocked("mix_in_bwd", [dproj], [w_in], tm,
                                                        rider=[_pair_rider([dwin]), _gather_rider([packed, d_ws2])])
    (sum_in,) = _pair_add("mix_dwin_add", [dwin], [sib_in], slots)
    dx1, df1, d_norm_mix, dsc2, dsh2, dg1 = _norm_mod_bwd("mix_norm_bwd", x1, norm_mix, sc2, sh2, dh2, dx2,
                                                          below=(saved1[-1], g1, 0.5))
    (dx0, d_norm_ffn1, dsc1, dsh1), down1, sum_g1, sum_u1, (got_in_near,), _, (got_in_far,) = _ffn_bwd(
        "ffn1", dx1, df1, saved1, norm_ffn1, sh1, sc1, wg1, wu1, wd1, slots,
        dact_rider=_chip_rider([sum_in], NEIGHBOURS), dwgu_rider=_chip_rider([sum_in], DIAGONAL))
    packed_late = jnp.concatenate([dsh1, dsc1, dg1, dsh2, dsc2, jnp.zeros((4, D), F32), d_norm_ffn1, d_norm_mix,
                                   jnp.zeros((R_LATE - 11, D), F32)], axis=0)
    grads = dict(ffn2_w_gate=(sum_g2, [got_g2_near, got_g2_far]), ffn2_w_up=(sum_u2, [got_u2]), ffn2_w_down=(sum_d2, [got_d2]),
                 mix_w_in=(sum_in, [got_in_near, got_in_far]), w_branch_a=(sum_a, [got_a]), w_branch_b=(sum_b, [got_b]),
                 w_out=(sum_o, [got_o]), ffn1_w_down=down1)
    done, (late_all, got_g1, got_u1) = _adamw_group(
        "adamw_most", [(cs, got, *[a[0] for a in given[n]]) for n, (cs, got) in grads.items()],
        rider=[_gather_rider([packed_late]), _chip_rider([sum_g1, sum_u1])])
    last, _ = _adamw_group("adamw_ffn1_in", [(sum_g1, [got_g1], *[a[0] for a in given["ffn1_w_gate"]]),
                                            (sum_u1, [got_u1], *[a[0] for a in given["ffn1_w_up"]])], rows=256)
    big_out = {n: [o.reshape(given[n][0].shape) for o in outs]
               for n, outs in zip([*grads, "ffn1_w_gate", "ffn1_w_up"], [*done, *last])}

    flat = lambda a: a.reshape(1, -1)
    vectors = [("ada_b", 0, 9, ada_b, m_ada_b, v_ada_b), ("norm_ffn1", 9, 1, norm_ffn1, m_norm_ffn1, v_norm_ffn1),
               ("norm_mix", 10, 1, norm_mix, m_norm_mix, v_norm_mix), ("norm_ffn2", 11, 1, norm_ffn2, m_norm_ffn2, v_norm_ffn2),
               ("norm_final", 12, 1, norm_final, m_norm_final, v_norm_final), ("mix_b_in", 13, 6, mix_b_in, m_mix_b_in, v_mix_b_in),
               ("sgu_ln_g", 19, 1, sgu_ln_g, m_sgu_ln_g, v_sgu_ln_g), ("sgu_ln_b", 20, 1, sgu_ln_b, m_sgu_ln_b, v_sgu_ln_b),
               ("sgu_b_s", 21, 1, sgu_b_s, m_sgu_b_s, v_sgu_b_s), ("conv_b", 22, 1, conv_b, m_conv_b, v_conv_b),
               ("conv_ln_g", 23, 1, conv_ln_g, m_conv_ln_g, v_conv_ln_g), ("conv_ln_b", 24, 1, conv_ln_b, m_conv_ln_b, v_conv_ln_b)]
    small_out, d_cw_all, loss_sum = _adamw_small(
        "adamw_small", packed_all, late_all, dws_all, [(row, rows, flat(wv), flat(mv), flat(vv)) for _, row, rows, wv, mv, vv in vectors],
        [a.reshape(HEADS * CHUNK, CHUNK) for a in (sgu_w_s, m_sgu_w_s, v_sgu_w_s)])
    small = {n: [o.reshape(wv.shape) for o in outs] for (n, _, _, wv, _, _), outs in zip(vectors, small_out)}
    small["sgu_w_s"] = [o.reshape(sgu_w_s.shape) for o in small_out[-1]]
    g_cw = lax.dynamic_slice(d_cw_all, (0, me * CHUNK), (KW, CHUNK))
    small["conv_w"] = [o.reshape(conv_w.shape) for o in (g_cw, *_adamw_plain("adamw_conv_w", g_cw, conv_w[0], m_conv_w[0], v_conv_w[0]))]
    loss = loss_sum[0, 0]

    dmod_cols = [lax.dynamic_slice(a[:, :N_MOD, :].reshape(NDEV, N_MOD * D), (0, me * ada_cols), (NDEV, ada_cols))
                 for a in (packed_all, late_all)]
    ada_out = [o.reshape(ada_w.shape) for o in _adamw_ada("adamw_ada_w", jnp.transpose(c_all), *dmod_cols, ada_w[0], m_ada_w[0], v_ada_w[0])]

    order = ["ada_w", "ada_b", "norm_ffn1", "ffn1_w_gate", "ffn1_w_up", "ffn1_w_down", "norm_mix", "mix_w_in", "mix_b_in",
             "sgu_ln_g", "sgu_ln_b", "sgu_w_s", "sgu_b_s", "conv_w", "conv_b", "conv_ln_g", "conv_ln_b", "w_branch_a",
             "w_branch_b", "w_out", "norm_ffn2", "ffn2_w_gate", "ffn2_w_up", "ffn2_w_down", "norm_final"]

    def leaf(n, kind):
        if n == "ada_w":
            return ada_out[kind]
        if n in big_out:
            return big_out[n][kind]
        return small[n][kind]

    return (loss, dx0.reshape(x.shape), *[leaf(n, kind) for kind in range(4) for n in order])
```

```python
import jax
import jax.numpy as jnp
from jax import lax
from jax.experimental import pallas as pl
from jax.experimental.pallas import tpu as pltpu
from jax.experimental.pallas import tpu_sc as plsc

D = 1024
F = 4 * D
D_IN = 6 * D
HEADS = 8
CHUNK = 128
KW = 31
HALO = 32
N_MOD = 9
NDEV = 8
N_CHIPS = 4
EPS = 1e-6
LR, B1, B2, ADAM_EPS, WD, STEP = 0.001, 0.9, 0.999, 1e-08, 0.01, 10
BC1 = 1.0 - B1 ** STEP
BC2 = 1.0 - B2 ** STEP
VMEM_LIMIT = 56 * 1024 * 1024
MESH = pl.DeviceIdType.MESH
HBM = pl.BlockSpec(memory_space=pltpu.HBM)
VMEM = pl.BlockSpec(memory_space=pltpu.VMEM)
BF = jnp.bfloat16
F32 = jnp.float32

NN = (((1,), (0,)), ((), ()))
NT = (((1,), (1,)), ((), ()))
TN = (((0,), (0,)), ((), ()))

R_CW, R_LOSS, R_TOTAL = 25, 56, 64
R_LATE = 16


def _params(sem):
    return pltpu.CompilerParams(dimension_semantics=sem, vmem_limit_bytes=VMEM_LIMIT)


def _position():
    return lax.axis_index("x"), lax.axis_index("y"), lax.axis_index("c")


def _flip(pos, k):
    x, y, c = pos
    return (x ^ (k >> 2 & 1), y ^ (k >> 1 & 1), c ^ (k & 1))


def _index(pos):
    return 4 * pos[0] + 2 * pos[1] + pos[2]


def _gather_rows(x_ref, out_ref, send_sems, recv_sems, local_sem):
    m_per = x_ref.shape[0]
    x, y, c = _position()
    me, sibling = (x, y, c), (x, y, 1 - c)
    chips = [(1 - x, y), (x, 1 - y), (1 - x, 1 - y)]

    def rows(pos):
        return out_ref.at[pl.ds(_index(pos) * m_per, m_per), :]

    def copy(k, block, to, src=None):
        return pltpu.make_async_remote_copy(
            src_ref=rows(block) if src is None else src, dst_ref=rows(block),
            send_sem=send_sems.at[k], recv_sem=recv_sems.at[k], device_id=to, device_id_type=MESH)

    mine = pltpu.make_async_copy(x_ref, rows(me), local_sem)
    mine.start()
    first = [copy(0, me, sibling, src=x_ref)]
    first += [copy(1 + j, me, (*chip, c), src=x_ref) for j, chip in enumerate(chips)]
    for cp in first:
        cp.start()
    passed = [copy(4 + j, (*chip, c), sibling) for j, chip in enumerate(chips)]
    for j, chip in enumerate(chips):
        copy(1 + j, (*chip, c), me).wait_recv()
        passed[j].start()
    copy(0, sibling, me).wait_recv()
    for j, chip in enumerate(chips):
        copy(4 + j, (*chip, 1 - c), me).wait_recv()
    for cp in first + passed:
        cp.wait_send()
    mine.wait()


def _prologue(name, c_rows, taps, ada_w, ada_b, shards):
    rider = _gather_rider(shards)
    n = len(shards)
    nc = ada_w.shape[1]

    def body(*refs):
        c_ref, taps_ref, w_ref, b_ref = refs[:4]
        shard_refs = refs[4:4 + n]
        c_all_ref, taps_all_ref, mod_all_ref = refs[4 + n:7 + n]
        gathered_refs = refs[7 + n:7 + 2 * n]
        c_buf, mod_part, sems = refs[7 + 2 * n], refs[8 + 2 * n], refs[9 + 2 * n:]
        rider.start(shard_refs, gathered_refs, sems[9:])
        _gather_rows(c_ref, c_buf, *sems[0:3])
        rider.mid(shard_refs, gathered_refs, sems[9:])
        c_all = jnp.concatenate([c_buf[pl.ds(d * SUBLANES, 1), :] for d in range(NDEV)], axis=0)
        c_all_ref[...] = c_all
        mod_part[...] = jnp.dot(_silu(c_all), w_ref[...], preferred_element_type=F32) + b_ref[...]
        _gather_rows(taps_ref, taps_all_ref, *sems[3:6])
        _gather_rows(mod_part, mod_all_ref, *sems[6:9])
        rider.relay(shard_refs, gathered_refs, sems[9:])
        rider.finish(shard_refs, gathered_refs, sems[9:])

    small_sems = [pltpu.SemaphoreType.DMA((7,)), pltpu.SemaphoreType.DMA((7,)), pltpu.SemaphoreType.DMA] * 3
    res = pl.pallas_call(
        body, name=name,
        out_shape=[jax.ShapeDtypeStruct((NDEV, D), F32), jax.ShapeDtypeStruct((NDEV * taps.shape[0], taps.shape[1]), F32),
                   jax.ShapeDtypeStruct((NDEV * NDEV, nc), F32)] + rider.out_shapes,
        in_specs=[VMEM] * 4 + [HBM] * n, out_specs=[VMEM] * 3 + [HBM] * n,
        scratch_shapes=[pltpu.VMEM((NDEV * SUBLANES, D), F32), pltpu.VMEM((NDEV, nc), F32)] + small_sems + rider.sems,
        compiler_params=_params(None),
    )(c_rows, taps, ada_w, ada_b, *shards)
    return res[0], res[1], res[2], res[3:]


class _Rider:
    def __init__(self, ins, out_shapes, sems, start, finish, mid=None, relay=None):
        self.ins, self.out_shapes, self.sems = list(ins), list(out_shapes), list(sems)
        self.start, self.finish, self.mid, self.relay = start, finish, mid, relay


def _gather_rider(shards):
    n = len(shards)

    def setup(ins, outs, sems):
        send_sems, recv_sems, local_sems = sems
        x, y, c = _position()
        places = dict(me=(x, y, c), sibling=(x, y, 1 - c), xn=(1 - x, y, c), yn=(x, 1 - y, c), diagonal=(1 - x, 1 - y, c),
                      passed_on=(x ^ c, y ^ (1 - c), c), passed_to=(x ^ (1 - c), y ^ c, c))

        def copy(a, k, block, to, own=False):
            slot = outs[a].at[_index(block)]
            return pltpu.make_async_remote_copy(
                src_ref=ins[a] if own else slot, dst_ref=slot,
                send_sem=send_sems.at[k, a], recv_sem=recv_sems.at[k, a], device_id=to, device_id_type=MESH)

        def local(a):
            return pltpu.make_async_copy(ins[a], outs[a].at[_index(places["me"])], local_sems.at[a])

        return places, copy, local

    def start(ins, outs, sems):
        p, copy, local = setup(ins, outs, sems)
        for a in range(n):
            local(a).start()
            for k, to in enumerate(("sibling", "xn", "yn")):
                copy(a, k, p["me"], p[to], own=True).start()

    def mid(ins, outs, sems):
        p, copy, _ = setup(ins, outs, sems)
        for a in range(n):
            copy(a, 1, p["xn"], p["me"]).wait_recv()
            copy(a, 2, p["yn"], p["me"]).wait_recv()
            copy(a, 3, p["passed_on"], p["passed_to"]).start()
            copy(a, 4, p["xn"], p["sibling"]).start()
            copy(a, 5, p["yn"], p["sibling"]).start()

    def relay(ins, outs, sems):
        p, copy, _ = setup(ins, outs, sems)
        for a in range(n):
            copy(a, 3, p["diagonal"], p["me"]).wait_recv()
            copy(a, 6, p["diagonal"], p["sibling"]).start()

    def finish(ins, outs, sems):
        p, copy, local = setup(ins, outs, sems)
        x, y, c = p["me"]
        for a in range(n):
            for k, block in ((0, (x, y, 1 - c)), (4, (1 - x, y, 1 - c)), (5, (x, 1 - y, 1 - c)), (6, (1 - x, 1 - y, 1 - c))):
                copy(a, k, block, p["me"]).wait_recv()
            for k, to in enumerate(("sibling", "xn", "yn")):
                copy(a, k, p["me"], p[to], own=True).wait_send()
            copy(a, 3, p["passed_on"], p["passed_to"]).wait_send()
            for k, block in ((4, "xn"), (5, "yn"), (6, "diagonal")):
                copy(a, k, p[block], p["sibling"]).wait_send()
            local(a).wait()

    return _Rider(shards, [jax.ShapeDtypeStruct((NDEV, *s.shape), s.dtype) for s in shards],
                  [pltpu.SemaphoreType.DMA((7, n)), pltpu.SemaphoreType.DMA((7, n)), pltpu.SemaphoreType.DMA((n,))],
                  start, finish, mid, relay)


def _pair_rider(parts):
    n = len(parts)

    def copies(ins, outs, sems):
        send_sems, recv_sems = sems
        x, y, c = _position()
        q = 2 * x + y
        return [pltpu.make_async_remote_copy(
            src_ref=ins[a].at[2 * (q ^ k) + (1 - c)], dst_ref=outs[a].at[k],
            send_sem=send_sems.at[k, a], recv_sem=recv_sems.at[k, a], device_id=(x, y, 1 - c), device_id_type=MESH)
            for a in range(n) for k in range(N_CHIPS)]

    def start(ins, outs, sems):
        for cp in copies(ins, outs, sems):
            cp.start()

    def finish(ins, outs, sems):
        for cp in copies(ins, outs, sems):
            cp.wait()

    return _Rider(parts, [jax.ShapeDtypeStruct((N_CHIPS, *p.shape[1:]), p.dtype) for p in parts],
                  [pltpu.SemaphoreType.DMA((N_CHIPS, n)), pltpu.SemaphoreType.DMA((N_CHIPS, n))], start, finish)


NEIGHBOURS = (1, 2)
DIAGONAL = (3,)
OTHER_CHIPS = NEIGHBOURS + DIAGONAL


def _chip_rider(sums, ks=OTHER_CHIPS):
    n = len(sums)

    def copies(ins, outs, sems):
        send_sems, recv_sems = sems
        me = _position()
        return [pltpu.make_async_remote_copy(
            src_ref=ins[a].at[k], dst_ref=outs[a].at[j],
            send_sem=send_sems.at[j, a], recv_sem=recv_sems.at[j, a], device_id=_flip(me, 2 * k), device_id_type=MESH)
            for a in range(n) for j, k in enumerate(ks)]

    def start(ins, outs, sems):
        for cp in copies(ins, outs, sems):
            cp.start()

    def finish(ins, outs, sems):
        for cp in copies(ins, outs, sems):
            cp.wait()

    return _Rider(sums, [jax.ShapeDtypeStruct((len(ks), *s.shape[1:]), s.dtype) for s in sums],
                  [pltpu.SemaphoreType.DMA((len(ks), n)), pltpu.SemaphoreType.DMA((len(ks), n))], start, finish)


def _grid_edge(grid, last):
    cond = None
    for d, n in enumerate(grid):
        here = pl.program_id(d) == (n - 1 if last else 0)
        cond = here if cond is None else jnp.logical_and(cond, here)
    return cond


def _call(name, compute, grid, ins, in_specs, out_shapes, out_specs, scratch_shapes, semantics, rider=None, aliases=None):
    riders = [rider] if isinstance(rider, _Rider) else list(rider or [])
    n_in, n_out, n_scr = len(ins), len(out_shapes), len(scratch_shapes)
    n_rin, n_rout, n_rsem = [sum(len(part(r)) for r in riders) for part in (lambda r: r.ins, lambda r: r.out_shapes, lambda r: r.sems)]
    cuts = [0, n_in, n_in + n_rin, n_in + n_rin + n_out, n_in + n_rin + n_out + n_rout, n_in + n_rin + n_out + n_rout + n_scr]

    def body(*refs):
        in_refs, rin_refs, out_refs, rout_refs, scr_refs = [refs[a:b] for a, b in zip(cuts[:-1], cuts[1:])]
        rsem_refs = refs[cuts[-1]:]
        mine, at = [], [0, 0, 0]
        for r in riders:
            mine.append((r, rin_refs[at[0]:at[0] + len(r.ins)], rout_refs[at[1]:at[1] + len(r.out_shapes)],
                         rsem_refs[at[2]:at[2] + len(r.sems)]))
            at = [at[0] + len(r.ins), at[1] + len(r.out_shapes), at[2] + len(r.sems)]
        if riders:
            @pl.when(_grid_edge(grid, last=False))
            def _():
                for r, a, b, c in mine:
                    r.start(a, b, c)

        if any(r.mid for r in riders):
            step, steps = 0, 1
            for d, size in enumerate(grid):
                step, steps = step * size + pl.program_id(d), steps * size

            @pl.when(step == steps * 5 // 8)
            def _():
                for r, a, b, c in mine:
                    if r.mid:
                        r.mid(a, b, c)

        if any(r.relay for r in riders):
            @pl.when(_grid_edge(grid, last=True))
            def _():
                for r, a, b, c in mine:
                    if r.relay:
                        r.relay(a, b, c)

        compute(in_refs, out_refs, scr_refs)
        if riders:
            @pl.when(_grid_edge(grid, last=True))
            def _():
                for r, a, b, c in mine:
                    r.finish(a, b, c)

    res = pl.pallas_call(
        body, name=name, grid=grid,
        out_shape=list(out_shapes) + [s for r in riders for s in r.out_shapes],
        in_specs=list(in_specs) + [HBM] * n_rin, out_specs=list(out_specs) + [HBM] * n_rout,
        scratch_shapes=list(scratch_shapes) + [s for r in riders for s in r.sems],
        input_output_aliases=aliases or {}, compiler_params=_params(semantics),
    )(*ins, *[a for r in riders for a in r.ins])
    return (res[:n_out], res[n_out:]) if riders else res


def _pair_add(name, parts, from_sibling, slots):
    n = len(parts)

    def body(s_ref, *refs):
        for a in range(n):
            refs[2 * n + a][...] = (refs[a][...].astype(F32) + refs[n + a][...].astype(F32)).astype(refs[2 * n + a].dtype)

    def slab(p, picked):
        _, r, c = p.shape
        return pl.BlockSpec((None, r, c), (lambda k, s: (s[k], 0, 0)) if picked else (lambda k, s: (k, 0, 0)))

    return pl.pallas_call(
        body, name=name,
        grid_spec=pltpu.PrefetchScalarGridSpec(
            num_scalar_prefetch=1, grid=(N_CHIPS,),
            in_specs=[slab(p, True) for p in parts] + [slab(p, False) for p in parts],
            out_specs=[slab(p, False) for p in parts]),
        out_shape=[jax.ShapeDtypeStruct((N_CHIPS, *p.shape[1:]), p.dtype) for p in parts],
        compiler_params=_params(("arbitrary",)),
    )(slots, *parts, *from_sibling)


def _mm(name, pairs, dims, grid, nk, out_shapes, out_specs, extras=(), extra_specs=(), epilogue=None, acc_shape=None, rider=None):
    n_pairs = len(pairs)

    def compute(ins, outs, scratch):
        def partial_sum():
            total = None
            for p in range(n_pairs):
                d = lax.dot_general(ins[2 * p][...], ins[2 * p + 1][...], dims, preferred_element_type=F32)
                total = d if total is None else total + d
            return total

        def finish(r):
            ex = [e[...] for e in ins[2 * n_pairs:]]
            res = epilogue(r, *ex) if epilogue is not None else (r,)
            for o, v in zip(outs, res):
                o[...] = v.astype(o.dtype)

        if nk == 1:
            finish(partial_sum())
        else:
            acc = scratch[0]
            k = pl.program_id(2)

            @pl.when(k == 0)
            def _():
                acc[...] = partial_sum()

            @pl.when(k > 0)
            def _():
                acc[...] += partial_sum()

            @pl.when(k == nk - 1)
            def _():
                finish(acc[...])

    operands, specs = [], []
    for a, a_spec, b, b_spec in pairs:
        operands += [a, b]
        specs += [a_spec, b_spec]
    return _call(name, compute, grid, operands + list(extras), specs + list(extra_specs), out_shapes, out_specs,
                 [pltpu.VMEM(acc_shape, F32)] if nk > 1 else [], ("parallel", "parallel", "arbitrary"), rider)


def _single(res, rider):
    return (res[0][0], res[1]) if rider else res[0]


def _silu(x):
    return x * jax.nn.sigmoid(x)


def _ffn_up(name, h, wg, wu, rider=None, norm=None):
    t = h.shape[0]
    tm = min(t, 1024)
    nb = F // NDEV

    def compute(ins, outs, scr):
        if norm:
            @pl.when(pl.program_id(1) == 0)
            def _():
                scr[0][...] = _rms_mod(ins[0][...], ins[3][...], ins[4][...], ins[5][...]).astype(BF)
                outs[3][...] = scr[0][...]

            hv = scr[0][...]
        else:
            hv = ins[0][...]
        g = jnp.dot(hv, ins[1][...], preferred_element_type=F32)
        u = jnp.dot(hv, ins[2][...], preferred_element_type=F32)
        outs[0][...] = g.astype(BF)
        outs[1][...] = u.astype(BF)
        outs[2][...] = (_silu(g) * u).astype(BF)

    w_spec = pl.BlockSpec((None, D, nb), lambda i, j: (j, 0, 0))
    o_spec = pl.BlockSpec((tm, nb), lambda i, j: (i, j))
    rows = pl.BlockSpec((tm, D), lambda i, j: (i, 0))
    vec = pl.BlockSpec((1, D), lambda i, j: (0, 0))
    return _call(name, compute, (t // tm, NDEV), [h, wg, wu, *(norm or ())], [rows, w_spec, w_spec] + [vec] * (3 if norm else 0),
                 [jax.ShapeDtypeStruct((t, F), BF)] * 3 + ([jax.ShapeDtypeStruct((t, D), BF)] if norm else []),
                 [o_spec] * 3 + ([rows] if norm else []), [pltpu.VMEM((tm, D), BF)] if norm else [],
                 ("parallel", "arbitrary"), rider)


def _mm_nn(name, a, b, tm, tn, tk, extras=(), extra_specs=(), epilogue=None, out_dtypes=(F32,), rider=None):
    m, kk = a.shape
    n = b.shape[1]
    nk = kk // tk
    return _mm(
        name, [(a, pl.BlockSpec((tm, tk), lambda i, j, k: (i, k)), b, pl.BlockSpec((tk, tn), lambda i, j, k: (k, j)))], NN,
        (m // tm, n // tn, nk), nk,
        [jax.ShapeDtypeStruct((m, n), dt) for dt in out_dtypes],
        [pl.BlockSpec((tm, tn), lambda i, j, k: (i, j))] * len(out_dtypes),
        extras, extra_specs, epilogue, (tm, tn), rider)


def _mm_nn_blocked(name, a, b3, tm, rider=None):
    m = a.shape[0]
    nb = b3.shape[2]
    return _single(_mm(
        name, [(a, pl.BlockSpec((tm, D), lambda i, j, k: (i, 0)), b3, pl.BlockSpec((None, D, nb), lambda i, j, k: (j, 0, 0)))], NN,
        (m // tm, NDEV, 1), 1,
        [jax.ShapeDtypeStruct((m, NDEV * nb), F32)], [pl.BlockSpec((tm, nb), lambda i, j, k: (i, j))], rider=rider), rider)


def _mm_nt(name, a, b, tm, tn, out_dtypes=(F32,), extras=(), extra_specs=(), epilogue=None, rider=None):
    m, kk = a.shape
    n = b.shape[0]
    return _mm(
        name, [(a, pl.BlockSpec((tm, kk), lambda i, j, k: (i, 0)), b, pl.BlockSpec((tn, kk), lambda i, j, k: (j, 0)))], NT,
        (m // tm, n // tn, 1), 1,
        [jax.ShapeDtypeStruct((m, n), dt) for dt in out_dtypes],
        [pl.BlockSpec((tm, tn), lambda i, j, k: (i, j))] * len(out_dtypes),
        extras, extra_specs, epilogue, rider=rider)


def _mm_nt_blocked(name, a_list, b3_list, tm, rider=None):
    m = a_list[0].shape[0]
    nb = b3_list[0].shape[2]
    pairs = [(a, pl.BlockSpec((tm, nb), lambda i, j, k: (i, k)), b3, pl.BlockSpec((None, D, nb), lambda i, j, k: (k, 0, 0)))
             for a, b3 in zip(a_list, b3_list)]
    return _single(_mm(name, pairs, NT, (m // tm, 1, NDEV), NDEV,
                       [jax.ShapeDtypeStruct((m, D), F32)], [pl.BlockSpec((tm, D), lambda i, j, k: (i, 0))],
                       acc_shape=(tm, D), rider=rider), rider)


def _mm_tn(name, a, b, tm, tn, rider=None):
    t, m = a.shape
    n = b.shape[1]
    return _single(_mm(
        name, [(a, pl.BlockSpec((t, tm), lambda i, j, k: (0, i)), b, pl.BlockSpec((t, tn), lambda i, j, k: (0, j)))], TN,
        (m // tm, n // tn, 1), 1,
        [jax.ShapeDtypeStruct((m, n), BF)], [pl.BlockSpec((tm, tn), lambda i, j, k: (i, j))], rider=rider), rider)


def _mm_tn_blocked(name, a, b, rider=None):
    t = a.shape[0]
    nb = b.shape[1] // NDEV
    return _single(_mm(
        name, [(a, pl.BlockSpec((t, D), lambda i, j, k: (0, 0)), b, pl.BlockSpec((t, nb), lambda i, j, k: (0, j)))], TN,
        (1, NDEV, 1), 1,
        [jax.ShapeDtypeStruct((NDEV, D, nb), BF)], [pl.BlockSpec((None, D, nb), lambda i, j, k: (j, 0, 0))], rider=rider), rider)


def _dw_gate_up(name, h, dgate, dup, rider=None):
    t = h.shape[0]
    nb = F // NDEV

    def compute(ins, outs, _):
        hv = ins[0][...]
        outs[0][...] = lax.dot_general(hv, ins[1][...], TN, preferred_element_type=F32).astype(BF)
        outs[1][...] = lax.dot_general(hv, ins[2][...], TN, preferred_element_type=F32).astype(BF)

    d_spec = pl.BlockSpec((t, nb), lambda j: (0, j))
    o_spec = pl.BlockSpec((None, D, nb), lambda j: (j, 0, 0))
    return _call(name, compute, (NDEV,), [h, dgate, dup], [pl.BlockSpec((t, D), lambda j: (0, 0)), d_spec, d_spec],
                 [jax.ShapeDtypeStruct((NDEV, D, nb), BF)] * 2, [o_spec] * 2, [], ("arbitrary",), rider)


def _dw_square(name, pairs):
    t = pairs[0][0].shape[0]
    tm = 512
    n = len(pairs)

    def compute(ins, outs, _):
        for p in range(n):
            outs[p][...] = lax.dot_general(ins[2 * p][...], ins[2 * p + 1][...], TN, preferred_element_type=F32).astype(BF)

    return _call(name, compute, (D // tm,), [x for pair in pairs for x in pair],
                 [pl.BlockSpec((t, tm), lambda i: (0, i)), pl.BlockSpec((t, D), lambda i: (0, 0))] * n,
                 [jax.ShapeDtypeStruct((D, D), BF)] * n, [pl.BlockSpec((tm, D), lambda i: (i, 0))] * n, [], ("arbitrary",))


def _swiglu_bwd(da, gate, up):
    gate = gate.astype(F32)
    s = jax.nn.sigmoid(gate)
    return da * up.astype(F32) * (s * (1.0 + gate * (1.0 - s))), da * (gate * s)


def _ffn_dact_dh(name, df, wd, gate, up, wg, wu, rider=None):
    t = df.shape[0]
    tm = min(t, 1024)
    nb = F // NDEV

    def compute(ins, outs, scr):
        acc = scr[0]
        j = pl.program_id(1)
        da = lax.dot_general(ins[0][...], ins[1][...], NT, preferred_element_type=F32)
        dgate, dup = _swiglu_bwd(da, ins[2][...], ins[3][...])
        dgate, dup = dgate.astype(BF), dup.astype(BF)
        outs[0][...] = dgate
        outs[1][...] = dup
        part = (lax.dot_general(dgate, ins[4][...], NT, preferred_element_type=F32)
                + lax.dot_general(dup, ins[5][...], NT, preferred_element_type=F32))

        @pl.when(j == 0)
        def _():
            acc[...] = part

        @pl.when(j > 0)
        def _():
            acc[...] += part

        @pl.when(j == NDEV - 1)
        def _():
            outs[2][...] = acc[...]

    blk = pl.BlockSpec((tm, nb), lambda i, j: (i, j))
    w3 = pl.BlockSpec((None, D, nb), lambda i, j: (j, 0, 0))
    row = pl.BlockSpec((tm, D), lambda i, j: (i, 0))
    return _call(name, compute, (t // tm, NDEV), [df, wd, gate, up, wg, wu],
                 [row, pl.BlockSpec((nb, D), lambda i, j: (j, 0)), blk, blk, w3, w3],
                 [jax.ShapeDtypeStruct((t, F), BF)] * 2 + [jax.ShapeDtypeStruct((t, D), F32)], [blk, blk, row],
                 [pltpu.VMEM((tm, D), F32)], ("parallel", "arbitrary"), rider)


def _rowcall(name, fn, ins, in_specs, n_row_out, out_shapes, out_specs, grid, scratch_shapes=(), rider=None, aliases=None):
    def accumulate(o, v, i):
        @pl.when(i == 0)
        def _():
            o[...] = v.astype(o.dtype)

        @pl.when(i > 0)
        def _():
            o[...] += v.astype(o.dtype)

    def compute(in_refs, out_refs, scr):
        i = pl.program_id(0)
        vals = fn(i, in_refs, scr)
        for idx, (o, v) in enumerate(zip(out_refs, vals)):
            if idx < n_row_out:
                o[...] = v.astype(o.dtype)
            else:
                accumulate(o, v, i)

    return _call(name, compute, (grid,), ins, in_specs, out_shapes, out_specs, list(scratch_shapes), ("arbitrary",), rider, aliases)


def _rows(tr, w=D, cb=0):
    return pl.BlockSpec((tr, w), lambda i: (i, cb))


def _whole(shape):
    nd = len(shape)
    return pl.BlockSpec(shape, lambda i: (0,) * nd)


def _vec(n=1):
    return jax.ShapeDtypeStruct((n, D), F32)


def _rms_mod(x, gain, sc, sh):
    y = x * lax.rsqrt(jnp.mean(x * x, axis=-1, keepdims=True) + EPS)
    return (y * gain) * (1.0 + sc) + sh


def _layer_norm(x, g, b):
    mu = jnp.mean(x, axis=-1, keepdims=True)
    var = jnp.mean(jnp.square(x - mu), axis=-1, keepdims=True)
    return (x - mu) * lax.rsqrt(var + EPS) * g + b


def _gate_grads(dx, f, g, scale):
    return scale * g * dx, jnp.sum(scale * dx * f.astype(F32), axis=0, keepdims=True)


def _norm_mod_bwd(name, x, gain, sc, sh, dh, dres, below=None, rider=None):
    t = x.shape[0]
    tr = min(t, 512)

    def fn(i, r, _):
        _, vjp = jax.vjp(_rms_mod, r[0][...], r[1][...], r[2][...], r[3][...])
        dx, dgain, dsc, dsh = vjp(r[4][...])
        dx = dx + r[5][...]
        if below is None:
            return [dx, dgain, dsc, dsh]
        df, dg = _gate_grads(dx, r[6][...], r[7][...], below[2])
        return [dx, df, dgain, dsc, dsh, dg]

    ins, specs = [x, gain, sc, sh, dh, dres], [_rows(tr)] + [_whole((1, D))] * 3 + [_rows(tr)] * 2
    outs, out_specs = [jax.ShapeDtypeStruct((t, D), F32)], [_rows(tr)]
    if below is not None:
        ins, specs = ins + [below[0], below[1]], specs + [_rows(tr), _whole((1, D))]
        outs, out_specs = outs + [jax.ShapeDtypeStruct((t, D), BF)], out_specs + [_rows(tr)]
    n_vec = 3 if below is None else 4
    return _rowcall(name, fn, ins, specs, len(outs), outs + [_vec()] * n_vec, out_specs + [_whole((1, D))] * n_vec, t // tr,
                    rider=rider)


def _sgu_pre(up, vp, bu, bv, ln_g, ln_b):
    return jax.nn.gelu(up + bu), _layer_norm(jax.nn.gelu(vp + bv), ln_g, ln_b)


def _causal(w_ref, h):
    rows = lax.broadcasted_iota(jnp.int32, (CHUNK, CHUNK), 0)
    cols = lax.broadcasted_iota(jnp.int32, (CHUNK, CHUNK), 1)
    return jnp.where(cols <= rows, w_ref[h], 0.0)


def _sgu(name, proj, b_in, ln_g, ln_b, w_s, bias_full, rider=None):
    t = proj.shape[0]

    def fn(i, r, _):
        u, v = _sgu_pre(r[0][...], r[1][...], r[2][...], r[3][...], r[4][...], r[5][...])
        vb = v.astype(BF)
        mixed = [jnp.dot(_causal(r[6], h).astype(BF), vb[:, h * CHUNK:(h + 1) * CHUNK], preferred_element_type=F32)
                 for h in range(HEADS)]
        return [u * (jnp.concatenate(mixed, axis=1) + r[7][...])]

    return _rowcall(
        name, fn, [proj, proj, b_in, b_in, ln_g, ln_b, w_s, bias_full],
        [_rows(CHUNK, D, 0), _rows(CHUNK, D, 1), pl.BlockSpec((1, D), lambda i: (0, 0)), pl.BlockSpec((1, D), lambda i: (0, 1)),
         _whole((1, D)), _whole((1, D)), _whole((HEADS, CHUNK, CHUNK)), _whole((CHUNK, D))],
        1, [jax.ShapeDtypeStruct((t, D), BF)], [_rows(CHUNK)], t // CHUNK, rider=rider)


def _sgu_bwd(name, proj, b_in, ln_g, ln_b, w_s, bias_full, dout, dproj, rider=None):
    t = proj.shape[0]

    def fn(i, r, _):
        (u, v), vjp = jax.vjp(_sgu_pre, r[0][...], r[1][...], r[2][...], r[3][...], r[4][...], r[5][...])
        vb = v.astype(BF)
        d = r[8][...]
        masks = [_causal(r[6], h).astype(BF) for h in range(HEADS)]
        cols = [slice(h * CHUNK, (h + 1) * CHUNK) for h in range(HEADS)]
        mixed = jnp.concatenate([jnp.dot(masks[h], vb[:, cols[h]], preferred_element_type=F32) for h in range(HEADS)], axis=1)
        du = d * (mixed + r[7][...])
        dmix = d * u
        dmb = dmix.astype(BF)
        dv = jnp.concatenate([lax.dot_general(masks[h], dmb[:, cols[h]], TN, preferred_element_type=F32) for h in range(HEADS)], axis=1)
        rows = lax.broadcasted_iota(jnp.int32, (CHUNK, CHUNK), 0)
        lanes = lax.broadcasted_iota(jnp.int32, (CHUNK, CHUNK), 1)
        dws = jnp.stack([jnp.where(lanes <= rows, lax.dot_general(dmb[:, cols[h]], vb[:, cols[h]], NT, preferred_element_type=F32), 0.0)
                         for h in range(HEADS)])
        dbs = jnp.zeros((CHUNK, CHUNK), F32)
        for h in range(HEADS):
            dbs = dbs + jnp.where(lanes == h, jnp.sum(dmix[:, cols[h]], axis=1, keepdims=True), 0.0)
        dup, dvp, dbu, dbv, dg, db = vjp((du, dv))
        return [jnp.concatenate([dup, dvp], axis=1), dbu, dbv, dg, db, dws, dbs]

    return _rowcall(
        name, fn, [proj, proj, b_in, b_in, ln_g, ln_b, w_s, bias_full, dout, dproj],
        [_rows(CHUNK, D, 0), _rows(CHUNK, D, 1), pl.BlockSpec((1, D), lambda i: (0, 0)), pl.BlockSpec((1, D), lambda i: (0, 1)),
         _whole((1, D)), _whole((1, D)), _whole((HEADS, CHUNK, CHUNK)), _whole((CHUNK, D)), _rows(CHUNK),
         pl.BlockSpec(memory_space=pl.ANY)],
        1, [jax.ShapeDtypeStruct(dproj.shape, dproj.dtype)] + [_vec()] * 4
        + [jax.ShapeDtypeStruct((HEADS, CHUNK, CHUNK), F32), jax.ShapeDtypeStruct((CHUNK, CHUNK), F32)],
        [pl.BlockSpec((CHUNK, 2 * D), lambda i: (i, 0))] + [_whole((1, D))] * 4 + [_whole((HEADS, CHUNK, CHUNK)), _whole((CHUNK, CHUNK))],
        t // CHUNK, rider=rider, aliases={9: 0})


def _halo_before(tr, cb):
    return pl.BlockSpec((HALO, D), lambda i: (jnp.maximum(i * (tr // HALO) - 1, 0), cb))


def _halo_after(tr, cb, n_tiles):
    return pl.BlockSpec((HALO, D), lambda i: (jnp.minimum((i + 1) * (tr // HALO), n_tiles * (tr // HALO) - 1), cb))


def _ln_silu(z, g, b):
    return _silu(_layer_norm(z, g, b))


SUBLANES = 8
LANES = 128
CONV_STRIP = 16
DW_STRIP = 32


def _shifted_copies(buf, copies, rows):
    for b in range(1, SUBLANES):
        copies[b - 1, pl.ds(0, rows), :] = buf[pl.ds(b, rows), :]


def _shifted(buf, copies, offset, start, rows, lanes=slice(None)):
    at = pl.ds(pl.multiple_of(start + SUBLANES * (offset // SUBLANES), SUBLANES), rows)
    return buf[at, lanes] if offset % SUBLANES == 0 else copies[offset % SUBLANES - 1, at, lanes]


def _accumulate(o, v, i):
    @pl.when(i == 0)
    def _():
        o[...] = v.astype(o.dtype)

    @pl.when(i > 0)
    def _():
        o[...] += v.astype(o.dtype)


def _conv(name, proj, b_in, conv_w, conv_b, ln_g, ln_b, rider=None):
    t = proj.shape[0]
    tr = min(t, 256)

    def compute(r, outs, scr):
        zbuf, zs = scr
        i = pl.program_id(0)
        bv, bg = r[4][...], r[5][...]
        z0 = (r[0][...] + bv) * jax.nn.sigmoid(r[1][...] + bg)
        before = (r[2][...] + bv) * jax.nn.sigmoid(r[3][...] + bg)
        zbuf[pl.ds(0, HALO), :] = jnp.where(i > 0, before, 0.0)
        zbuf[pl.ds(HALO, tr), :] = z0
        outs[0][...] = z0
        _shifted_copies(zbuf, zs, tr + HALO - SUBLANES)

        def strip(s, carry):
            r0 = s * CONV_STRIP
            acc = jnp.zeros((CONV_STRIP, D), F32) + r[7][...]
            for k in range(KW):
                acc = acc + r[6][k:k + 1, :] * _shifted(zbuf, zs, HALO - (KW - 1) + k, r0, CONV_STRIP)
            outs[1][pl.ds(pl.multiple_of(r0, SUBLANES), CONV_STRIP), :] = acc
            return carry

        lax.fori_loop(0, tr // CONV_STRIP, strip, 0)
        outs[2][...] = _ln_silu(outs[1][...], r[8][...], r[9][...]).astype(BF)

    return _call(
        name, compute, (t // tr,), [proj, proj, proj, proj, b_in, b_in, conv_w, conv_b, ln_g, ln_b],
        [_rows(tr, D, 2), _rows(tr, D, 3), _halo_before(tr, 2), _halo_before(tr, 3),
         pl.BlockSpec((1, D), lambda i: (0, 2)), pl.BlockSpec((1, D), lambda i: (0, 3)),
         _whole((HALO, D)), _whole((1, D)), _whole((1, D)), _whole((1, D))],
        [jax.ShapeDtypeStruct((t, D), F32), jax.ShapeDtypeStruct((t, D), F32), jax.ShapeDtypeStruct((t, D), BF)],
        [_rows(tr)] * 3, [pltpu.VMEM((tr + HALO, D), F32), pltpu.VMEM((SUBLANES - 1, tr + HALO, D), F32)], ("arbitrary",), rider)


def _conv_bwd(name, proj, b_in, conv_w, ln_g, ln_b, z0, z1, dz3, dproj, rider=None):
    t = proj.shape[0]
    tr = min(t, 256)
    n_tiles = t // tr

    def compute(r, outs, scr):
        zbuf, dbuf, zs, ds, dwacc = scr
        i = pl.program_id(0)
        g, b = r[5][...], r[6][...]
        zero_row = jnp.zeros((1, D), F32)
        _, vjp = jax.vjp(_ln_silu, r[9][...], g, b)
        dz1, dg, db = vjp(r[11][...])
        dcb = jnp.sum(dz1, axis=0, keepdims=True)
        _, vjp_after = jax.vjp(_ln_silu, r[10][...], g, b)
        dz1_after = vjp_after(r[12][...])[0]
        dbuf[pl.ds(0, tr), :] = dz1
        dbuf[pl.ds(tr, HALO), :] = jnp.where(i < n_tiles - 1, dz1_after, 0.0)
        zbuf[pl.ds(0, HALO), :] = jnp.where(i > 0, r[8][...], 0.0)
        zbuf[pl.ds(HALO, tr), :] = r[7][...]
        _shifted_copies(dbuf, ds, tr + HALO - SUBLANES)
        _shifted_copies(zbuf, zs, tr + HALO - SUBLANES)

        def dz0_strip(s, carry):
            r0 = s * CONV_STRIP
            at = pl.ds(pl.multiple_of(r0, CONV_STRIP), CONV_STRIP)
            acc = jnp.zeros((CONV_STRIP, D), F32)
            for k in range(KW):
                acc = acc + r[4][k:k + 1, :] * _shifted(dbuf, ds, KW - 1 - k, r0, CONV_STRIP)
            a = r[0][at, :] + r[2][...]
            sg = jax.nn.sigmoid(r[1][at, :] + r[3][...])
            dcv = acc * sg
            dcg = acc * a * sg * (1.0 - sg)
            outs[0][at, :] = jnp.concatenate([dcv, dcg], axis=1).astype(BF)
            return carry[0] + jnp.sum(dcv, axis=0, keepdims=True), carry[1] + jnp.sum(dcg, axis=0, keepdims=True)

        dbv, dbg = lax.fori_loop(0, tr // CONV_STRIP, dz0_strip, (zero_row, zero_row))

        for lb in range(D // LANES):
            lanes = slice(lb * LANES, (lb + 1) * LANES)

            def dw_strip(s, accs, lanes=lanes):
                r0 = s * DW_STRIP
                dz = dbuf[pl.ds(pl.multiple_of(r0, SUBLANES), DW_STRIP), lanes]
                out = []
                for k in range(KW):
                    prod = dz * _shifted(zbuf, zs, HALO - (KW - 1) + k, r0, DW_STRIP, lanes)
                    part = prod[0:SUBLANES]
                    for q in range(1, DW_STRIP // SUBLANES):
                        part = part + prod[q * SUBLANES:(q + 1) * SUBLANES]
                    out.append(accs[k] + part)
                return tuple(out)

            accs = lax.fori_loop(0, tr // DW_STRIP, dw_strip, tuple(jnp.zeros((SUBLANES, LANES), F32) for _ in range(KW)))
            for k in range(KW):
                dwacc[pl.ds(k * SUBLANES, SUBLANES), lanes] = accs[k]
        dw_rows = [jnp.sum(dwacc[pl.ds(k * SUBLANES, SUBLANES), :], axis=0, keepdims=True) for k in range(KW)]
        dw_rows.append(jnp.zeros((HALO - KW, D), F32))
        for o, v in zip(outs[1:], (dbv, dbg, jnp.concatenate(dw_rows, axis=0), dcb, dg, db)):
            _accumulate(o, v, i)

    wide = pl.BlockSpec((tr, 2 * D), lambda i: (i, 1))
    return _call(
        name, compute, (n_tiles,), [proj, proj, b_in, b_in, conv_w, ln_g, ln_b, z0, z0, z1, z1, dz3, dz3, dproj],
        [_rows(tr, D, 2), _rows(tr, D, 3), pl.BlockSpec((1, D), lambda i: (0, 2)), pl.BlockSpec((1, D), lambda i: (0, 3)),
         _whole((HALO, D)), _whole((1, D)), _whole((1, D)),
         _rows(tr), _halo_before(tr, 0), _rows(tr), _halo_after(tr, 0, n_tiles), _rows(tr), _halo_after(tr, 0, n_tiles),
         pl.BlockSpec(memory_space=pl.ANY)],
        [jax.ShapeDtypeStruct(dproj.shape, dproj.dtype), _vec(), _vec(), _vec(HALO), _vec(), _vec(), _vec()],
        [wide] + [_whole((1, D))] * 2 + [_whole((HALO, D))] + [_whole((1, D))] * 3,
        [pltpu.VMEM((tr + HALO, D), F32), pltpu.VMEM((tr + HALO, D), F32),
         pltpu.VMEM((SUBLANES - 1, tr + HALO, D), F32), pltpu.VMEM((SUBLANES - 1, tr + HALO, D), F32),
         pltpu.VMEM((HALO * SUBLANES, D), F32)],
        ("arbitrary",), rider, aliases={13: 0})


def _merge_fn(ga, gb, bga, bgb, ya, yb):
    return jax.nn.sigmoid(ga + bga) * ya + jax.nn.sigmoid(gb + bgb) * yb


def _mix_tail(name, ua, z3, proj, b_in, wa, wb, wo, x, g, next_norm):
    t = ua.shape[0]
    tr = min(t, 256)

    def compute(r, outs, _):
        ya = jnp.dot(r[0][...], r[6][...], preferred_element_type=F32)
        yb = jnp.dot(r[1][...], r[7][...], preferred_element_type=F32)
        merged = _merge_fn(r[2][...], r[3][...], r[4][...], r[5][...], ya, yb).astype(BF)
        y = jnp.dot(merged, r[8][...], preferred_element_type=F32)
        x_out = r[9][...] + r[10][...] * y
        for o, v in zip(outs, (ya, yb, merged, y, x_out, _rms_mod(x_out, r[11][...], r[12][...], r[13][...]))):
            o[...] = v.astype(o.dtype)

    row = _whole((1, D))
    return _call(
        name, compute, (t // tr,), [ua, z3, proj, proj, b_in, b_in, wa, wb, wo, x, g, *next_norm],
        [_rows(tr), _rows(tr), _rows(tr, D, 4), _rows(tr, D, 5), pl.BlockSpec((1, D), lambda i: (0, 4)),
         pl.BlockSpec((1, D), lambda i: (0, 5)), _whole((D, D)), _whole((D, D)), _whole((D, D)), _rows(tr), row, row, row, row],
        [jax.ShapeDtypeStruct((t, D), dt) for dt in (F32, F32, BF, BF, F32, BF)], [_rows(tr)] * 6, [], ("arbitrary",))


def _mix_tail_bwd(name, dy, proj, b_in, ya, yb, wa, wb, wo, rider=None):
    t = proj.shape[0]
    tr = min(t, 256)

    def compute(r, outs, _):
        i = pl.program_id(0)
        dm = lax.dot_general(r[0][...], r[9][...], NT, preferred_element_type=F32)
        _, vjp = jax.vjp(_merge_fn, *[x[...] for x in r[1:7]])
        dga, dgb, dbga, dbgb, dya, dyb = vjp(dm)
        dya, dyb = dya.astype(BF), dyb.astype(BF)
        outs[0][...] = jnp.concatenate([dga, dgb], axis=1).astype(BF)
        outs[1][...] = dya
        outs[2][...] = dyb
        outs[3][...] = lax.dot_general(dya, r[7][...], NT, preferred_element_type=F32)
        outs[4][...] = lax.dot_general(dyb, r[8][...], NT, preferred_element_type=F32)
        _accumulate(outs[5], dbga, i)
        _accumulate(outs[6], dbgb, i)

    return _call(
        name, compute, (t // tr,), [dy, proj, proj, b_in, b_in, ya, yb, wa, wb, wo],
        [_rows(tr), _rows(tr, D, 4), _rows(tr, D, 5), pl.BlockSpec((1, D), lambda i: (0, 4)), pl.BlockSpec((1, D), lambda i: (0, 5)),
         _rows(tr), _rows(tr), _whole((D, D)), _whole((D, D)), _whole((D, D))],
        [jax.ShapeDtypeStruct((t, D_IN), BF)] + [jax.ShapeDtypeStruct((t, D), BF)] * 2 + [jax.ShapeDtypeStruct((t, D), F32)] * 2
        + [_vec(), _vec()],
        [pl.BlockSpec((tr, 2 * D), lambda i: (i, 2))] + [_rows(tr)] * 4 + [_whole((1, D))] * 2, [], ("arbitrary",), rider)


def _loss_head(name, x, gain, target, f, g, scale):
    t = x.shape[0]
    tr = min(t, 512)

    def loss_fn(xv, gn, tgt):
        y = xv * lax.rsqrt(jnp.mean(xv * xv, axis=-1, keepdims=True) + EPS) * gn
        return 0.5 * jnp.sum(jnp.mean(jnp.square(y - tgt), axis=-1))

    def fn(i, r, _):
        loss, vjp = jax.vjp(loss_fn, r[0][...], r[1][...], r[2][...])
        dx, dgain, _ = vjp(jnp.ones((), F32))
        df, dg = _gate_grads(dx, r[3][...], r[4][...], scale)
        return [dx, df, dgain, jnp.zeros((1, D), F32) + loss, dg]

    return _rowcall(name, fn, [x, gain, target, f, g], [_rows(tr), _whole((1, D)), _rows(tr), _rows(tr), _whole((1, D))], 2,
                    [jax.ShapeDtypeStruct((t, D), F32), jax.ShapeDtypeStruct((t, D), BF), _vec(), _vec(), _vec()],
                    [_rows(tr)] * 2 + [_whole((1, D))] * 3, t // tr)


def _adamw(w, g, m, v):
    m = B1 * m + (1.0 - B1) * g
    v = B2 * v + (1.0 - B2) * (g * g)
    m_hat = m / BC1
    v_hat = v / BC2
    delta = -LR * (m_hat / (jnp.sqrt(v_hat) + ADAM_EPS) + WD * w)
    return delta, m, v


ADAMW_ROWS = 64


SC_TILES = 32
SC_LANES = 16
SC_ROWS = 8


def _adamw_sc(name, g, w, m, v):
    r, c = w.shape
    per_tile = r // SC_TILES
    rows = min(per_tile, SC_ROWS)

    def body(g_hbm, w_hbm, m_hbm, v_hbm, d_hbm, mo_hbm, vo_hbm, gb, wb, mb, vb, db):
        tile = lax.axis_index("subcore") * 2 + lax.axis_index("core")

        @pl.loop(0, per_tile, step=rows)
        def _(r0):
            mine = pl.ds(tile * per_tile + r0, rows)
            for src, buf in ((g_hbm, gb), (w_hbm, wb), (m_hbm, mb), (v_hbm, vb)):
                pltpu.sync_copy(src.at[mine, :], buf)

            @pl.loop(0, rows)
            def _(rr):
                @pl.loop(0, c, step=SC_LANES)
                def _(i):
                    at = pl.ds(i, SC_LANES)
                    delta, m_new, v_new = _adamw(wb[rr, at], gb[rr, at], mb[rr, at], vb[rr, at])
                    db[rr, at] = delta
                    mb[rr, at] = m_new
                    vb[rr, at] = v_new

            for buf, dst in ((db, d_hbm), (mb, mo_hbm), (vb, vo_hbm)):
                pltpu.sync_copy(buf, dst.at[mine, :])

    return pl.kernel(body, name=name, out_type=[jax.ShapeDtypeStruct(w.shape, F32)] * 3,
                     mesh=plsc.VectorSubcoreMesh(core_axis_name="core", subcore_axis_name="subcore"),
                     scratch_types=[pltpu.VMEM((rows, c), F32)] * 5)(g, w, m, v)


def _grad_sums(name, items, rows=256):
    ins, in_specs, out_shapes, out_specs, plan = [], [], [], [], []
    first = 0
    for chip_sum, received in items:
        _, r, c = chip_sum.shape
        tr = min(r, rows)
        n = r // tr

        def tile(i, first=first, n=n):
            return jnp.clip(i - first, 0, n - 1)

        ins += [chip_sum, *received]
        in_specs += [pl.BlockSpec((None, tr, c), lambda i, tile=tile: (0, tile(i), 0))]
        in_specs += [pl.BlockSpec((g.shape[0], tr, c), lambda i, tile=tile: (0, tile(i), 0)) for g in received]
        out_shapes.append(jax.ShapeDtypeStruct((r, c), F32))
        out_specs.append(pl.BlockSpec((tr, c), lambda i, tile=tile: (tile(i), 0)))
        plan.append((first, n, [g.shape[0] for g in received]))
        first += n

    def compute(in_refs, out_refs, _):
        i = pl.program_id(0)
        at_in = 0
        for j, (start, n, counts) in enumerate(plan):
            mine = in_refs[at_in:at_in + 1 + len(counts)]
            at_in += 1 + len(counts)

            @pl.when(jnp.logical_and(i >= start, i < start + n))
            def _(mine=mine, out=out_refs[j], counts=counts):
                g = mine[0][...].astype(F32)
                for k, count in enumerate(counts):
                    for s in range(count):
                        g = g + mine[1 + k][s].astype(F32)
                out[...] = g

    return _call(name, compute, (first,), ins, in_specs, out_shapes, out_specs, [], ("arbitrary",))


def _adamw_group(name, items, rider=None, rows=ADAMW_ROWS):
    ins, in_specs, out_shapes, out_specs, plan = [], [], [], [], []
    first = 0
    for chip_sum, received, w, m, v in items:
        r, c = w.shape
        tr = min(r, rows)
        n = r // tr

        def tile(i, first=first, n=n):
            return jnp.clip(i - first, 0, n - 1)

        spec = pl.BlockSpec((tr, c), lambda i, tile=tile: (tile(i), 0))
        ins += [chip_sum, *received, w, m, v]
        in_specs += [pl.BlockSpec((None, tr, c), lambda i, tile=tile: (0, tile(i), 0))]
        in_specs += [pl.BlockSpec((g.shape[0], tr, c), lambda i, tile=tile: (0, tile(i), 0)) for g in received]
        in_specs += [spec] * 3
        out_shapes += [jax.ShapeDtypeStruct((r, c), F32)] * 4
        out_specs += [spec] * 4
        plan.append((first, n, [g.shape[0] for g in received]))
        first += n

    def compute(in_refs, out_refs, _):
        i = pl.program_id(0)
        at_in = at_out = 0
        for start, n, counts in plan:
            mine = in_refs[at_in:at_in + 4 + len(counts)]
            outs = out_refs[at_out:at_out + 4]
            at_in += 4 + len(counts)
            at_out += 4

            @pl.when(jnp.logical_and(i >= start, i < start + n))
            def _(mine=mine, outs=outs, counts=counts):
                g = mine[0][...].astype(F32)
                for j, count in enumerate(counts):
                    for s in range(count):
                        g = g + mine[1 + j][s].astype(F32)
                delta, m_new, v_new = _adamw(mine[-3][...], g, mine[-2][...], mine[-1][...])
                for o, val in zip(outs, (g, delta, m_new, v_new)):
                    o[...] = val

    res = _call(name, compute, (first,), ins, in_specs, out_shapes, out_specs, [], ("arbitrary",), rider)
    outs, rode = res if rider else (res, [])
    return [outs[4 * j:4 * j + 4] for j in range(len(items))], rode


def _adamw_small(name, packed_all, late_all, dws_all, vectors, w_s):
    n_vec = len(vectors)

    def body(*refs):
        p_ref, l_ref, d_ref = refs[:3]
        param_refs = refs[3:3 + 3 * n_vec + 3]
        out_refs = refs[3 + 3 * n_vec + 3:-1]
        g_ref = refs[-1]
        g = p_ref[0]
        late = l_ref[0]
        for s in range(1, NDEV):
            g = g + p_ref[s]
            late = late + l_ref[s]
        g_ref[...] = g
        g_ref[pl.ds(0, R_LATE), :] += late

        def update(gp, wmv, outs):
            delta, m_new, v_new = _adamw(wmv[0][...], gp, wmv[1][...], wmv[2][...])
            for o, val in zip(outs, (gp, delta, m_new, v_new)):
                o[...] = val

        for j, (row, rows, *_) in enumerate(vectors):
            pieces = [g_ref[pl.ds(row + r, 1), :] for r in range(rows)]
            update(pieces[0] if rows == 1 else jnp.concatenate(pieces, axis=1), param_refs[3 * j:3 * j + 3], out_refs[4 * j:4 * j + 4])
        gw = d_ref[0]
        for s in range(1, NDEV):
            gw = gw + d_ref[s]
        update(gw, param_refs[3 * n_vec:], out_refs[4 * n_vec:4 * n_vec + 4])
        out_refs[-2][...] = g_ref[pl.ds(R_CW, KW), :]
        out_refs[-1][...] = g_ref[pl.ds(R_LOSS, 1), :]

    params = [a for _, _, w, m, v in vectors for a in (w, m, v)] + list(w_s)
    out_shapes = [jax.ShapeDtypeStruct(w.shape, F32) for _, _, w, _, _ in vectors for _ in range(4)]
    out_shapes += [jax.ShapeDtypeStruct(w_s[0].shape, F32)] * 4 + [jax.ShapeDtypeStruct((KW, D), F32), _vec()]
    res = pl.pallas_call(body, name=name, out_shape=out_shapes, scratch_shapes=[pltpu.VMEM((R_TOTAL, D), F32)],
                         compiler_params=_params(None))(packed_all, late_all, dws_all, *params)
    return [res[4 * j:4 * j + 4] for j in range(n_vec + 1)], res[-2], res[-1]


def _adamw_plain(name, g, w, m, v):
    def body(g_ref, w_ref, m_ref, v_ref, d_ref, mo_ref, vo_ref):
        delta, m_new, v_new = _adamw(w_ref[...], g_ref[...], m_ref[...], v_ref[...])
        d_ref[...] = delta
        mo_ref[...] = m_new
        vo_ref[...] = v_new

    return pl.pallas_call(body, name=name, out_shape=[jax.ShapeDtypeStruct(w.shape, F32)] * 3,
                          compiler_params=_params(None))(g, w, m, v)


def _adamw_ada(name, c_all_t, dmod, dmod_late, w, m, v):
    r, c = w.shape
    tr = 256

    def fn(i, refs, _):
        ca = _silu(refs[0][...])
        dm = refs[1][...] + refs[2][...]
        g = ca[:, 0:1] * dm[0:1, :]
        for b in range(1, NDEV):
            g = g + ca[:, b:b + 1] * dm[b:b + 1, :]
        delta, m_new, v_new = _adamw(refs[3][...], g, refs[4][...], refs[5][...])
        return [g, delta, m_new, v_new]

    spec = pl.BlockSpec((tr, c), lambda i: (i, 0))
    whole = pl.BlockSpec((NDEV, c), lambda i: (0, 0))
    return _rowcall(name, fn, [c_all_t, dmod, dmod_late, w, m, v],
                    [pl.BlockSpec((tr, NDEV), lambda i: (i, 0)), whole, whole, spec, spec, spec], 4,
                    [jax.ShapeDtypeStruct((r, c), F32)] * 4, [spec] * 4, r // tr)


def _ffn_fwd(tag, x, h, g, wg, wu, wd_shard, down_rider, next_norm=None, more_shards=(), norm=None):
    t = x.shape[0]
    tm = min(t, 512 if down_rider else 1024)
    (gate, up, act, *normed), (wd, *more) = _ffn_up(f"{tag}_up", x if norm else h, wg, wu,
                                                    rider=_gather_rider([wd_shard, *more_shards]), norm=norm)
    h = normed[0] if norm else h
    row = pl.BlockSpec((1, D), lambda i, j, k: (0, 0))

    def epilogue(f, xv, gv, *norm):
        x_out = xv + 0.5 * gv * f
        return (x_out, f, _rms_mod(x_out, *norm)) if norm else (x_out, f)

    res = _mm_nn(f"{tag}_down", act, wd.reshape(F, D), tm, D, 1024, extras=(x, g, *(next_norm or ())),
                 extra_specs=(pl.BlockSpec((tm, D), lambda i, j, k: (i, 0)), row, *([row] * 3 if next_norm else [])),
                 epilogue=epilogue, out_dtypes=(F32, BF, BF) if next_norm else (F32, BF), rider=down_rider)
    (x_out, f, *h_next), rode = res if down_rider else (res, None)
    return x_out, (h_next[0] if next_norm else None), (x, h, gate, up, act, f), wd, rode, more


def _ffn_bwd(tag, dx_out, df, saved, gain, sh, sc, wg, wu, wd, slots, dact_rider=None, dwd_rider=None, dwgu_rider=None,
             below=None, fuse_dh=False):
    x, h, gate, up, act, f = saved
    t = x.shape[0]
    tm = min(t, 1024)
    if fuse_dh:
        dgate, dup, dh = _ffn_dact_dh(f"{tag}_dact_dh", df, wd.reshape(F, D), gate, up, wg, wu)
        dwd = _mm_tn(f"{tag}_dwd", act, df, 512, D).reshape(NDEV, F // NDEV, D)
        (dwg, dwu), (sib_d,) = _dw_gate_up(f"{tag}_dwgu", h, dgate, dup, rider=_pair_rider([dwd]))
        (sum_d,) = _pair_add(f"{tag}_dwd_add", [dwd], [sib_d], slots)
        normed, (sib_g, sib_u) = _norm_mod_bwd(f"{tag}_norm_bwd", x, gain, sc, sh, dh, dx_out, below=below,
                                               rider=_pair_rider([dwg, dwu]))
        sum_g, sum_u = _pair_add(f"{tag}_dwgu_add", [dwg, dwu], [sib_g, sib_u], slots)
        return normed, (sum_d, None), sum_g, sum_u, [], [], []

    blk = pl.BlockSpec((t, F // NDEV), lambda i, j, k: (i, j))
    res = _mm_nt(f"{tag}_dact", df, wd.reshape(F, D), t, F // NDEV, out_dtypes=(BF, BF),
                 extras=(gate, up), extra_specs=(blk, blk), epilogue=_swiglu_bwd, rider=dact_rider)
    (dgate, dup), rode_dact = res if dact_rider else (res, [])
    res = _mm_tn(f"{tag}_dwd", act, df, 512, D, rider=dwd_rider)
    dwd, rode_dwd = res if dwd_rider else (res, [])
    dwd = dwd.reshape(NDEV, F // NDEV, D)
    (dwg, dwu), (sib_d, *rode_dwgu) = _dw_gate_up(f"{tag}_dwgu", h, dgate, dup,
                                                  rider=[_pair_rider([dwd])] + ([dwgu_rider] if dwgu_rider else []))
    (sum_d,) = _pair_add(f"{tag}_dwd_add", [dwd], [sib_d], slots)
    dh, (sib_g, sib_u, got_d) = _mm_nt_blocked(f"{tag}_dh", [dgate, dup], [wg, wu], tm,
                                               rider=[_pair_rider([dwg, dwu]), _chip_rider([sum_d])])
    sum_g, sum_u = _pair_add(f"{tag}_dwgu_add", [dwg, dwu], [sib_g, sib_u], slots)
    normed = _norm_mod_bwd(f"{tag}_norm_bwd", x, gain, sc, sh, dh, dx_out, below=below)
    return normed, (sum_d, [got_d]), sum_g, sum_u, rode_dact, rode_dwd, rode_dwgu


def kernel(x, c, ada_w, ada_b, norm_ffn1, ffn1_w_gate, ffn1_w_up, ffn1_w_down, norm_mix, mix_w_in, mix_b_in, sgu_ln_g, sgu_ln_b, sgu_w_s, sgu_b_s, conv_w, conv_b, conv_ln_g, conv_ln_b, w_branch_a, w_branch_b, w_out, norm_ffn2, ffn2_w_gate, ffn2_w_up, ffn2_w_down, norm_final, loss_target, m_ada_w, m_ada_b, m_norm_ffn1, m_ffn1_w_gate, m_ffn1_w_up, m_ffn1_w_down, m_norm_mix, m_mix_w_in, m_mix_b_in, m_sgu_ln_g, m_sgu_ln_b, m_sgu_w_s, m_sgu_b_s, m_conv_w, m_conv_b, m_conv_ln_g, m_conv_ln_b, m_w_branch_a, m_w_branch_b, m_w_out, m_norm_ffn2, m_ffn2_w_gate, m_ffn2_w_up, m_ffn2_w_down, m_norm_final, v_ada_w, v_ada_b, v_norm_ffn1, v_ffn1_w_gate, v_ffn1_w_up, v_ffn1_w_down, v_norm_mix, v_mix_w_in, v_mix_b_in, v_sgu_ln_g, v_sgu_ln_b, v_sgu_w_s, v_sgu_b_s, v_conv_w, v_conv_b, v_conv_ln_g, v_conv_ln_b, v_w_branch_a, v_w_branch_b, v_w_out, v_norm_ffn2, v_ffn2_w_gate, v_ffn2_w_up, v_ffn2_w_down, v_norm_final):
    mx, my, mc = _position()
    me = 4 * mx + 2 * my + mc
    chip = 2 * mx + my
    slots = jnp.stack([2 * (chip ^ k) + mc for k in range(N_CHIPS)]).astype(jnp.int32)
    t = x.shape[1]
    tm = min(t, 1024)
    x0 = x.reshape(t, D)
    target = loss_target.reshape(t, D)
    given = dict(ffn1_w_gate=(ffn1_w_gate, m_ffn1_w_gate, v_ffn1_w_gate), ffn1_w_up=(ffn1_w_up, m_ffn1_w_up, v_ffn1_w_up),
                 ffn1_w_down=(ffn1_w_down, m_ffn1_w_down, v_ffn1_w_down), mix_w_in=(mix_w_in, m_mix_w_in, v_mix_w_in),
                 w_branch_a=(w_branch_a, m_w_branch_a, v_w_branch_a), w_branch_b=(w_branch_b, m_w_branch_b, v_w_branch_b),
                 w_out=(w_out, m_w_out, v_w_out), ffn2_w_gate=(ffn2_w_gate, m_ffn2_w_gate, v_ffn2_w_gate),
                 ffn2_w_up=(ffn2_w_up, m_ffn2_w_up, v_ffn2_w_up), ffn2_w_down=(ffn2_w_down, m_ffn2_w_down, v_ffn2_w_down))
    shard = {n: wmv[0][0].astype(BF) for n, wmv in given.items()}

    ada_cols = N_MOD * D // NDEV
    c_all, taps_all, mod_all, (wg1, wu1) = _prologue(
        "prologue", jnp.pad(c, ((0, SUBLANES - 1), (0, 0))), jnp.pad(conv_w[0], ((0, HALO - KW), (0, 0))), ada_w[0],
        lax.dynamic_slice(ada_b, (0, me * ada_cols), (1, ada_cols)), [shard["ffn1_w_gate"], shard["ffn1_w_up"]])
    conv_w_full = jnp.transpose(taps_all.reshape(NDEV, HALO, CHUNK), (1, 0, 2)).reshape(HALO, D)
    mod = lax.dynamic_index_in_dim(mod_all.reshape(NDEV, NDEV, ada_cols), me, axis=1, keepdims=False).reshape(N_MOD, 1, D)
    sh1, sc1, g1, sh2, sc2, g2, sh3, sc3, g3 = [mod[i] for i in range(N_MOD)]

    x1, h2, saved1, wd1, (w_in,), (wa3, wb3) = _ffn_fwd(
        "ffn1", x0, None, g1, wg1, wu1, shard["ffn1_w_down"], _gather_rider([shard["mix_w_in"]]),
        next_norm=(norm_mix, sc2, sh2), more_shards=(shard["w_branch_a"], shard["w_branch_b"]), norm=(norm_ffn1, sc1, sh1))
    proj, (wg2, wo3) = _mm_nn_blocked("mix_in", h2, w_in, tm, rider=_gather_rider([shard["ffn2_w_gate"], shard["w_out"]]))
    bias_full = jnp.repeat(sgu_b_s[0].T, CHUNK, axis=1)
    (ua,) = _sgu("sgu", proj, mix_b_in, sgu_ln_g, sgu_ln_b, sgu_w_s[0], bias_full)
    (z0, z1, z3), (wu2,) = _conv("conv", proj, mix_b_in, conv_w_full, conv_b, conv_ln_g, conv_ln_b,
                                 rider=_gather_rider([shard["ffn2_w_up"]]))
    wa, wb, wo = wa3.reshape(D, D), wb3.reshape(D, D), wo3.reshape(D, D)
    ya, yb, merged, y, x2, h3 = _mix_tail("mix_tail", ua, z3, proj, mix_b_in, wa, wb, wo, x1, g2, (norm_ffn2, sc3, sh3))
    x3, _, saved3, wd2, _, _ = _ffn_fwd("ffn2", x2, h3, g3, wg2, wu2, shard["ffn2_w_down"], None)

    norm_final2 = norm_final.reshape(1, D)
    dx3, df3, d_norm_final, loss_row, dg3 = _loss_head("loss_head", x3, norm_final2, target, saved3[-1], g3, 0.5)
    (dx2, dy, d_norm_ffn2, dsc3, dsh3, dg2), (sum_d2, _), sum_g2, sum_u2, _, _, _ = _ffn_bwd(
        "ffn2", dx3, df3, saved3, norm_ffn2, sh3, sc3, wg2, wu2, wd2, slots, below=(y, g2, 1.0), fuse_dh=True)
    (dproj, dya, dyb, dua, dz3, db_ga, db_gb), (got_g2_near,) = _mix_tail_bwd(
        "mix_tail_bwd", dy, proj, mix_b_in, ya, yb, wa, wb, wo, rider=_chip_rider([sum_g2], NEIGHBOURS))
    dwo, dwa, dwb = [g.reshape(NDEV, D // NDEV, D) for g in _dw_square("mix_dw", [(merged, dy), (ua, dya), (z3, dyb)])]
    (dproj, db_u, db_v, d_sgu_g, d_sgu_b, d_ws, d_bs_t), (*sib_abo, got_g2_far) = _sgu_bwd(
        "sgu_bwd", proj, mix_b_in, sgu_ln_g, sgu_ln_b, sgu_w_s[0], bias_full, dua, dproj,
        rider=[_pair_rider([dwa, dwb, dwo]), _chip_rider([sum_g2], DIAGONAL)])
    sum_a, sum_b, sum_o = _pair_add("mix_dw_add", [dwa, dwb, dwo], sib_abo, slots)
    (dproj, db_cv, db_cg, d_cw, d_cb, d_cln_g, d_cln_b), (got_u2, got_d2) = _conv_bwd(
        "conv_bwd", proj, mix_b_in, conv_w_full, conv_ln_g, conv_ln_b, z0, z1, dz3, dproj, rider=_chip_rider([sum_u2, sum_d2]))
    dwin, (got_a, got_b, got_o) = _mm_tn_blocked("mix_dwin", h2, dproj, rider=_chip_rider([sum_a, sum_b, sum_o]))

    early = dict(ffn2_w_gate=(sum_g2, [got_g2_near, got_g2_far]), ffn2_w_up=(sum_u2, [got_u2]), ffn2_w_down=(sum_d2, [got_d2]),
                 w_branch_a=(sum_a, [got_a]), w_branch_b=(sum_b, [got_b]), w_out=(sum_o, [got_o]))
    early_out = {}
    for n, g in zip(early, _grad_sums("grad_sums", list(early.values()))):
        wv, mv, vv = given[n]
        early_out[n] = [o.reshape(wv.shape) for o in (g, *_adamw_sc(f"adamw_sc_{n}", g, wv[0], mv[0], vv[0]))]

    d_bs = jnp.transpose(d_bs_t[:, :HEADS])
    zero = jnp.zeros((1, D), F32)
    pack_rows = [zero, zero, zero, zero, zero, dg2, dsh3, dsc3, dg3,
                 zero, zero, d_norm_ffn2, d_norm_final,
                 db_u, db_v, db_cv, db_cg, db_ga, db_gb,
                 d_sgu_g, d_sgu_b, d_bs.reshape(1, D), d_cb, d_cln_g, d_cln_b,
                 d_cw[:KW], loss_row, jnp.zeros((R_TOTAL - R_LOSS - 1, D), F32)]
    packed = jnp.concatenate(pack_rows, axis=0)
    d_ws2 = d_ws.reshape(HEADS * CHUNK, CHUNK)
    dh2, (sib_in, packed_all, dws_all) = _mm_nt_blocked("mix_in_bwd", [dproj], [w_in], tm,
                                                        rider=[_pair_rider([dwin]), _gather_rider([packed, d_ws2])])
    (sum_in,) = _pair_add("mix_dwin_add", [dwin], [sib_in], slots)
    dx1, df1, d_norm_mix, dsc2, dsh2, dg1 = _norm_mod_bwd("mix_norm_bwd", x1, norm_mix, sc2, sh2, dh2, dx2,
                                                          below=(saved1[-1], g1, 0.5))
    (dx0, d_norm_ffn1, dsc1, dsh1), down1, sum_g1, sum_u1, (got_in_near,), _, (got_in_far,) = _ffn_bwd(
        "ffn1", dx1, df1, saved1, norm_ffn1, sh1, sc1, wg1, wu1, wd1, slots,
        dact_rider=_chip_rider([sum_in], NEIGHBOURS), dwgu_rider=_chip_rider([sum_in], DIAGONAL))
    packed_late = jnp.concatenate([dsh1, dsc1, dg1, dsh2, dsc2, jnp.zeros((4, D), F32), d_norm_ffn1, d_norm_mix,
                                   jnp.zeros((R_LATE - 11, D), F32)], axis=0)
    grads = dict(mix_w_in=(sum_in, [got_in_near, got_in_far]), ffn1_w_down=down1)
    done, (late_all, got_g1, got_u1) = _adamw_group(
        "adamw_most", [(cs, got, *[a[0] for a in given[n]]) for n, (cs, got) in grads.items()],
        rider=[_gather_rider([packed_late]), _chip_rider([sum_g1, sum_u1])])
    last, _ = _adamw_group("adamw_ffn1_in", [(sum_g1, [got_g1], *[a[0] for a in given["ffn1_w_gate"]]),
                                            (sum_u1, [got_u1], *[a[0] for a in given["ffn1_w_up"]])], rows=256)
    big_out = {n: [o.reshape(given[n][0].shape) for o in outs]
               for n, outs in zip([*grads, "ffn1_w_gate", "ffn1_w_up"], [*done, *last])}
    big_out.update(early_out)

    flat = lambda a: a.reshape(1, -1)
    vectors = [("ada_b", 0, 9, ada_b, m_ada_b, v_ada_b), ("norm_ffn1", 9, 1, norm_ffn1, m_norm_ffn1, v_norm_ffn1),
               ("norm_mix", 10, 1, norm_mix, m_norm_mix, v_norm_mix), ("norm_ffn2", 11, 1, norm_ffn2, m_norm_ffn2, v_norm_ffn2),
               ("norm_final", 12, 1, norm_final, m_norm_final, v_norm_final), ("mix_b_in", 13, 6, mix_b_in, m_mix_b_in, v_mix_b_in),
               ("sgu_ln_g", 19, 1, sgu_ln_g, m_sgu_ln_g, v_sgu_ln_g), ("sgu_ln_b", 20, 1, sgu_ln_b, m_sgu_ln_b, v_sgu_ln_b),
               ("sgu_b_s", 21, 1, sgu_b_s, m_sgu_b_s, v_sgu_b_s), ("conv_b", 22, 1, conv_b, m_conv_b, v_conv_b),
               ("conv_ln_g", 23, 1, conv_ln_g, m_conv_ln_g, v_conv_ln_g), ("conv_ln_b", 24, 1, conv_ln_b, m_conv_ln_b, v_conv_ln_b)]
    small_out, d_cw_all, loss_sum = _adamw_small(
        "adamw_small", packed_all, late_all, dws_all, [(row, rows, flat(wv), flat(mv), flat(vv)) for _, row, rows, wv, mv, vv in vectors],
        [a.reshape(HEADS * CHUNK, CHUNK) for a in (sgu_w_s, m_sgu_w_s, v_sgu_w_s)])
    small = {n: [o.reshape(wv.shape) for o in outs] for (n, _, _, wv, _, _), outs in zip(vectors, small_out)}
    small["sgu_w_s"] = [o.reshape(sgu_w_s.shape) for o in small_out[-1]]
    g_cw = lax.dynamic_slice(d_cw_all, (0, me * CHUNK), (KW, CHUNK))
    small["conv_w"] = [o.reshape(conv_w.shape) for o in (g_cw, *_adamw_plain("adamw_conv_w", g_cw, conv_w[0], m_conv_w[0], v_conv_w[0]))]
    loss = loss_sum[0, 0]

    dmod_cols = [lax.dynamic_slice(a[:, :N_MOD, :].reshape(NDEV, N_MOD * D), (0, me * ada_cols), (NDEV, ada_cols))
                 for a in (packed_all, late_all)]
    ada_out = [o.reshape(ada_w.shape) for o in _adamw_ada("adamw_ada_w", jnp.transpose(c_all), *dmod_cols, ada_w[0], m_ada_w[0], v_ada_w[0])]

    order = ["ada_w", "ada_b", "norm_ffn1", "ffn1_w_gate", "ffn1_w_up", "ffn1_w_down", "norm_mix", "mix_w_in", "mix_b_in",
             "sgu_ln_g", "sgu_ln_b", "sgu_w_s", "sgu_b_s", "conv_w", "conv_b", "conv_ln_g", "conv_ln_b", "w_branch_a",
             "w_branch_b", "w_out", "norm_ffn2", "ffn2_w_gate", "ffn2_w_up", "ffn2_w_down", "norm_final"]

    def leaf(n, kind):
        if n == "ada_w":
            return ada_out[kind]
        if n in big_out:
            return big_out[n][kind]
        return small[n][kind]

    return (loss, dx0.reshape(x.shape), *[leaf(n, kind) for kind in range(4) for n in order])
```

```python
import jax
import jax.numpy as jnp
from jax import lax
from jax.experimental import pallas as pl
from jax.experimental.pallas import tpu as pltpu

D = 1024
F = 4 * D
D_IN = 6 * D
HEADS = 8
CHUNK = 128
KW = 31
HALO = 32
N_MOD = 9
NDEV = 8
N_CHIPS = 4
EPS = 1e-6
LR, B1, B2, ADAM_EPS, WD, STEP = 0.001, 0.9, 0.999, 1e-08, 0.01, 10
BC1 = 1.0 - B1 ** STEP
BC2 = 1.0 - B2 ** STEP
VMEM_LIMIT = 56 * 1024 * 1024
MESH = pl.DeviceIdType.MESH
HBM = pl.BlockSpec(memory_space=pltpu.HBM)
VMEM = pl.BlockSpec(memory_space=pltpu.VMEM)
BF = jnp.bfloat16
F32 = jnp.float32

NN = (((1,), (0,)), ((), ()))
NT = (((1,), (1,)), ((), ()))
TN = (((0,), (0,)), ((), ()))

R_CW, R_LOSS, R_TOTAL = 25, 56, 64
R_LATE = 16


def _params(sem):
    return pltpu.CompilerParams(dimension_semantics=sem, vmem_limit_bytes=VMEM_LIMIT)


def _position():
    return lax.axis_index("x"), lax.axis_index("y"), lax.axis_index("c")


def _flip(pos, k):
    x, y, c = pos
    return (x ^ (k >> 2 & 1), y ^ (k >> 1 & 1), c ^ (k & 1))


def _index(pos):
    return 4 * pos[0] + 2 * pos[1] + pos[2]


def _gather_rows(x_ref, out_ref, send_sems, recv_sems, local_sem):
    m_per = x_ref.shape[0]
    x, y, c = _position()
    me, sibling = (x, y, c), (x, y, 1 - c)
    chips = [(1 - x, y), (x, 1 - y), (1 - x, 1 - y)]

    def rows(pos):
        return out_ref.at[pl.ds(_index(pos) * m_per, m_per), :]

    def copy(k, block, to, src=None):
        return pltpu.make_async_remote_copy(
            src_ref=rows(block) if src is None else src, dst_ref=rows(block),
            send_sem=send_sems.at[k], recv_sem=recv_sems.at[k], device_id=to, device_id_type=MESH)

    mine = pltpu.make_async_copy(x_ref, rows(me), local_sem)
    mine.start()
    first = [copy(0, me, sibling, src=x_ref)]
    first += [copy(1 + j, me, (*chip, c), src=x_ref) for j, chip in enumerate(chips)]
    for cp in first:
        cp.start()
    passed = [copy(4 + j, (*chip, c), sibling) for j, chip in enumerate(chips)]
    for j, chip in enumerate(chips):
        copy(1 + j, (*chip, c), me).wait_recv()
        passed[j].start()
    copy(0, sibling, me).wait_recv()
    for j, chip in enumerate(chips):
        copy(4 + j, (*chip, 1 - c), me).wait_recv()
    for cp in first + passed:
        cp.wait_send()
    mine.wait()


def _prologue(name, c_rows, taps, ada_w, ada_b, shards):
    rider = _gather_rider(shards)
    n = len(shards)
    nc = ada_w.shape[1]

    def body(*refs):
        c_ref, taps_ref, w_ref, b_ref = refs[:4]
        shard_refs = refs[4:4 + n]
        c_all_ref, taps_all_ref, mod_all_ref = refs[4 + n:7 + n]
        gathered_refs = refs[7 + n:7 + 2 * n]
        c_buf, mod_part, sems = refs[7 + 2 * n], refs[8 + 2 * n], refs[9 + 2 * n:]
        rider.start(shard_refs, gathered_refs, sems[9:])
        _gather_rows(c_ref, c_buf, *sems[0:3])
        rider.mid(shard_refs, gathered_refs, sems[9:])
        c_all = jnp.concatenate([c_buf[pl.ds(d * SUBLANES, 1), :] for d in range(NDEV)], axis=0)
        c_all_ref[...] = c_all
        mod_part[...] = jnp.dot(_silu(c_all), w_ref[...], preferred_element_type=F32) + b_ref[...]
        _gather_rows(taps_ref, taps_all_ref, *sems[3:6])
        _gather_rows(mod_part, mod_all_ref, *sems[6:9])
        rider.relay(shard_refs, gathered_refs, sems[9:])
        rider.finish(shard_refs, gathered_refs, sems[9:])

    small_sems = [pltpu.SemaphoreType.DMA((7,)), pltpu.SemaphoreType.DMA((7,)), pltpu.SemaphoreType.DMA] * 3
    res = pl.pallas_call(
        body, name=name,
        out_shape=[jax.ShapeDtypeStruct((NDEV, D), F32), jax.ShapeDtypeStruct((NDEV * taps.shape[0], taps.shape[1]), F32),
                   jax.ShapeDtypeStruct((NDEV * NDEV, nc), F32)] + rider.out_shapes,
        in_specs=[VMEM] * 4 + [HBM] * n, out_specs=[VMEM] * 3 + [HBM] * n,
        scratch_shapes=[pltpu.VMEM((NDEV * SUBLANES, D), F32), pltpu.VMEM((NDEV, nc), F32)] + small_sems + rider.sems,
        compiler_params=_params(None),
    )(c_rows, taps, ada_w, ada_b, *shards)
    return res[0], res[1], res[2], res[3:]


class _Rider:
    def __init__(self, ins, out_shapes, sems, start, finish, mid=None, relay=None):
        self.ins, self.out_shapes, self.sems = list(ins), list(out_shapes), list(sems)
        self.start, self.finish, self.mid, self.relay = start, finish, mid, relay


def _gather_rider(shards):
    n = len(shards)

    def setup(ins, outs, sems):
        send_sems, recv_sems, local_sems = sems
        x, y, c = _position()
        places = dict(me=(x, y, c), sibling=(x, y, 1 - c), xn=(1 - x, y, c), yn=(x, 1 - y, c), diagonal=(1 - x, 1 - y, c),
                      passed_on=(x ^ c, y ^ (1 - c), c), passed_to=(x ^ (1 - c), y ^ c, c))

        def copy(a, k, block, to, own=False):
            slot = outs[a].at[_index(block)]
            return pltpu.make_async_remote_copy(
                src_ref=ins[a] if own else slot, dst_ref=slot,
                send_sem=send_sems.at[k, a], recv_sem=recv_sems.at[k, a], device_id=to, device_id_type=MESH)

        def local(a):
            return pltpu.make_async_copy(ins[a], outs[a].at[_index(places["me"])], local_sems.at[a])

        return places, copy, local

    def start(ins, outs, sems):
        p, copy, local = setup(ins, outs, sems)
        for a in range(n):
            local(a).start()
            for k, to in enumerate(("sibling", "xn", "yn")):
                copy(a, k, p["me"], p[to], own=True).start()

    def mid(ins, outs, sems):
        p, copy, _ = setup(ins, outs, sems)
        for a in range(n):
            copy(a, 1, p["xn"], p["me"]).wait_recv()
            copy(a, 2, p["yn"], p["me"]).wait_recv()
            copy(a, 3, p["passed_on"], p["passed_to"]).start()
            copy(a, 4, p["xn"], p["sibling"]).start()
            copy(a, 5, p["yn"], p["sibling"]).start()

    def relay(ins, outs, sems):
        p, copy, _ = setup(ins, outs, sems)
        for a in range(n):
            copy(a, 3, p["diagonal"], p["me"]).wait_recv()
            copy(a, 6, p["diagonal"], p["sibling"]).start()

    def finish(ins, outs, sems):
        p, copy, local = setup(ins, outs, sems)
        x, y, c = p["me"]
        for a in range(n):
            for k, block in ((0, (x, y, 1 - c)), (4, (1 - x, y, 1 - c)), (5, (x, 1 - y, 1 - c)), (6, (1 - x, 1 - y, 1 - c))):
                copy(a, k, block, p["me"]).wait_recv()
            for k, to in enumerate(("sibling", "xn", "yn")):
                copy(a, k, p["me"], p[to], own=True).wait_send()
            copy(a, 3, p["passed_on"], p["passed_to"]).wait_send()
            for k, block in ((4, "xn"), (5, "yn"), (6, "diagonal")):
                copy(a, k, p[block], p["sibling"]).wait_send()
            local(a).wait()

    return _Rider(shards, [jax.ShapeDtypeStruct((NDEV, *s.shape), s.dtype) for s in shards],
                  [pltpu.SemaphoreType.DMA((7, n)), pltpu.SemaphoreType.DMA((7, n)), pltpu.SemaphoreType.DMA((n,))],
                  start, finish, mid, relay)


def _pair_rider(parts):
    n = len(parts)

    def copies(ins, outs, sems):
        send_sems, recv_sems = sems
        x, y, c = _position()
        q = 2 * x + y
        return [pltpu.make_async_remote_copy(
            src_ref=ins[a].at[2 * (q ^ k) + (1 - c)], dst_ref=outs[a].at[k],
            send_sem=send_sems.at[k, a], recv_sem=recv_sems.at[k, a], device_id=(x, y, 1 - c), device_id_type=MESH)
            for a in range(n) for k in range(N_CHIPS)]

    def start(ins, outs, sems):
        for cp in copies(ins, outs, sems):
            cp.start()

    def finish(ins, outs, sems):
        for cp in copies(ins, outs, sems):
            cp.wait()

    return _Rider(parts, [jax.ShapeDtypeStruct((N_CHIPS, *p.shape[1:]), p.dtype) for p in parts],
                  [pltpu.SemaphoreType.DMA((N_CHIPS, n)), pltpu.SemaphoreType.DMA((N_CHIPS, n))], start, finish)


NEIGHBOURS = (1, 2)
DIAGONAL = (3,)
OTHER_CHIPS = NEIGHBOURS + DIAGONAL


def _chip_rider(sums, ks=OTHER_CHIPS):
    n = len(sums)

    def copies(ins, outs, sems):
        send_sems, recv_sems = sems
        me = _position()
        return [pltpu.make_async_remote_copy(
            src_ref=ins[a].at[k], dst_ref=outs[a].at[j],
            send_sem=send_sems.at[j, a], recv_sem=recv_sems.at[j, a], device_id=_flip(me, 2 * k), device_id_type=MESH)
            for a in range(n) for j, k in enumerate(ks)]

    def start(ins, outs, sems):
        for cp in copies(ins, outs, sems):
            cp.start()

    def finish(ins, outs, sems):
        for cp in copies(ins, outs, sems):
            cp.wait()

    return _Rider(sums, [jax.ShapeDtypeStruct((len(ks), *s.shape[1:]), s.dtype) for s in sums],
                  [pltpu.SemaphoreType.DMA((len(ks), n)), pltpu.SemaphoreType.DMA((len(ks), n))], start, finish)


def _grid_edge(grid, last):
    cond = None
    for d, n in enumerate(grid):
        here = pl.program_id(d) == (n - 1 if last else 0)
        cond = here if cond is None else jnp.logical_and(cond, here)
    return cond


def _call(name, compute, grid, ins, in_specs, out_shapes, out_specs, scratch_shapes, semantics, rider=None, aliases=None):
    riders = [rider] if isinstance(rider, _Rider) else list(rider or [])
    n_in, n_out, n_scr = len(ins), len(out_shapes), len(scratch_shapes)
    n_rin, n_rout, n_rsem = [sum(len(part(r)) for r in riders) for part in (lambda r: r.ins, lambda r: r.out_shapes, lambda r: r.sems)]
    cuts = [0, n_in, n_in + n_rin, n_in + n_rin + n_out, n_in + n_rin + n_out + n_rout, n_in + n_rin + n_out + n_rout + n_scr]

    def body(*refs):
        in_refs, rin_refs, out_refs, rout_refs, scr_refs = [refs[a:b] for a, b in zip(cuts[:-1], cuts[1:])]
        rsem_refs = refs[cuts[-1]:]
        mine, at = [], [0, 0, 0]
        for r in riders:
            mine.append((r, rin_refs[at[0]:at[0] + len(r.ins)], rout_refs[at[1]:at[1] + len(r.out_shapes)],
                         rsem_refs[at[2]:at[2] + len(r.sems)]))
            at = [at[0] + len(r.ins), at[1] + len(r.out_shapes), at[2] + len(r.sems)]
        if riders:
            @pl.when(_grid_edge(grid, last=False))
            def _():
                for r, a, b, c in mine:
                    r.start(a, b, c)

        if any(r.mid for r in riders):
            step, steps = 0, 1
            for d, size in enumerate(grid):
                step, steps = step * size + pl.program_id(d), steps * size

            @pl.when(step == steps * 5 // 8)
            def _():
                for r, a, b, c in mine:
                    if r.mid:
                        r.mid(a, b, c)

        if any(r.relay for r in riders):
            @pl.when(_grid_edge(grid, last=True))
            def _():
                for r, a, b, c in mine:
                    if r.relay:
                        r.relay(a, b, c)

        compute(in_refs, out_refs, scr_refs)
        if riders:
            @pl.when(_grid_edge(grid, last=True))
            def _():
                for r, a, b, c in mine:
                    r.finish(a, b, c)

    res = pl.pallas_call(
        body, name=name, grid=grid,
        out_shape=list(out_shapes) + [s for r in riders for s in r.out_shapes],
        in_specs=list(in_specs) + [HBM] * n_rin, out_specs=list(out_specs) + [HBM] * n_rout,
        scratch_shapes=list(scratch_shapes) + [s for r in riders for s in r.sems],
        input_output_aliases=aliases or {}, compiler_params=_params(semantics),
    )(*ins, *[a for r in riders for a in r.ins])
    return (res[:n_out], res[n_out:]) if riders else res


def _pair_add(name, parts, from_sibling, slots):
    n = len(parts)

    def body(s_ref, *refs):
        for a in range(n):
            refs[2 * n + a][...] = (refs[a][...].astype(F32) + refs[n + a][...].astype(F32)).astype(refs[2 * n + a].dtype)

    def slab(p, picked):
        _, r, c = p.shape
        return pl.BlockSpec((None, r, c), (lambda k, s: (s[k], 0, 0)) if picked else (lambda k, s: (k, 0, 0)))

    return pl.pallas_call(
        body, name=name,
        grid_spec=pltpu.PrefetchScalarGridSpec(
            num_scalar_prefetch=1, grid=(N_CHIPS,),
            in_specs=[slab(p, True) for p in parts] + [slab(p, False) for p in parts],
            out_specs=[slab(p, False) for p in parts]),
        out_shape=[jax.ShapeDtypeStruct((N_CHIPS, *p.shape[1:]), p.dtype) for p in parts],
        compiler_params=_params(("arbitrary",)),
    )(slots, *parts, *from_sibling)


def _mm(name, pairs, dims, grid, nk, out_shapes, out_specs, extras=(), extra_specs=(), epilogue=None, acc_shape=None, rider=None):
    n_pairs = len(pairs)

    def compute(ins, outs, scratch):
        def partial_sum():
            total = None
            for p in range(n_pairs):
                d = lax.dot_general(ins[2 * p][...], ins[2 * p + 1][...], dims, preferred_element_type=F32)
                total = d if total is None else total + d
            return total

        def finish(r):
            ex = [e[...] for e in ins[2 * n_pairs:]]
            res = epilogue(r, *ex) if epilogue is not None else (r,)
            for o, v in zip(outs, res):
                o[...] = v.astype(o.dtype)

        if nk == 1:
            finish(partial_sum())
        else:
            acc = scratch[0]
            k = pl.program_id(2)

            @pl.when(k == 0)
            def _():
                acc[...] = partial_sum()

            @pl.when(k > 0)
            def _():
                acc[...] += partial_sum()

            @pl.when(k == nk - 1)
            def _():
                finish(acc[...])

    operands, specs = [], []
    for a, a_spec, b, b_spec in pairs:
        operands += [a, b]
        specs += [a_spec, b_spec]
    return _call(name, compute, grid, operands + list(extras), specs + list(extra_specs), out_shapes, out_specs,
                 [pltpu.VMEM(acc_shape, F32)] if nk > 1 else [], ("parallel", "parallel", "arbitrary"), rider)


def _single(res, rider):
    return (res[0][0], res[1]) if rider else res[0]


def _silu(x):
    return x * jax.nn.sigmoid(x)


def _ffn_up(name, h, wg, wu, rider=None, norm=None):
    t = h.shape[0]
    tm = min(t, 1024)
    nb = F // NDEV

    def compute(ins, outs, scr):
        if norm:
            @pl.when(pl.program_id(1) == 0)
            def _():
                scr[0][...] = _rms_mod(ins[0][...], ins[3][...], ins[4][...], ins[5][...]).astype(BF)
                outs[3][...] = scr[0][...]

            hv = scr[0][...]
        else:
            hv = ins[0][...]
        g = jnp.dot(hv, ins[1][...], preferred_element_type=F32)
        u = jnp.dot(hv, ins[2][...], preferred_element_type=F32)
        outs[0][...] = g.astype(BF)
        outs[1][...] = u.astype(BF)
        outs[2][...] = (_silu(g) * u).astype(BF)

    w_spec = pl.BlockSpec((None, D, nb), lambda i, j: (j, 0, 0))
    o_spec = pl.BlockSpec((tm, nb), lambda i, j: (i, j))
    rows = pl.BlockSpec((tm, D), lambda i, j: (i, 0))
    vec = pl.BlockSpec((1, D), lambda i, j: (0, 0))
    return _call(name, compute, (t // tm, NDEV), [h, wg, wu, *(norm or ())], [rows, w_spec, w_spec] + [vec] * (3 if norm else 0),
                 [jax.ShapeDtypeStruct((t, F), BF)] * 3 + ([jax.ShapeDtypeStruct((t, D), BF)] if norm else []),
                 [o_spec] * 3 + ([rows] if norm else []), [pltpu.VMEM((tm, D), BF)] if norm else [],
                 ("parallel", "arbitrary"), rider)


def _mm_nn(name, a, b, tm, tn, tk, extras=(), extra_specs=(), epilogue=None, out_dtypes=(F32,), rider=None):
    m, kk = a.shape
    n = b.shape[1]
    nk = kk // tk
    return _mm(
        name, [(a, pl.BlockSpec((tm, tk), lambda i, j, k: (i, k)), b, pl.BlockSpec((tk, tn), lambda i, j, k: (k, j)))], NN,
        (m // tm, n // tn, nk), nk,
        [jax.ShapeDtypeStruct((m, n), dt) for dt in out_dtypes],
        [pl.BlockSpec((tm, tn), lambda i, j, k: (i, j))] * len(out_dtypes),
        extras, extra_specs, epilogue, (tm, tn), rider)


def _mm_nn_blocked(name, a, b3, tm, rider=None):
    m = a.shape[0]
    nb = b3.shape[2]
    return _single(_mm(
        name, [(a, pl.BlockSpec((tm, D), lambda i, j, k: (i, 0)), b3, pl.BlockSpec((None, D, nb), lambda i, j, k: (j, 0, 0)))], NN,
        (m // tm, NDEV, 1), 1,
        [jax.ShapeDtypeStruct((m, NDEV * nb), BF)], [pl.BlockSpec((tm, nb), lambda i, j, k: (i, j))], rider=rider), rider)


def _mm_nt(name, a, b, tm, tn, out_dtypes=(F32,), extras=(), extra_specs=(), epilogue=None, rider=None):
    m, kk = a.shape
    n = b.shape[0]
    return _mm(
        name, [(a, pl.BlockSpec((tm, kk), lambda i, j, k: (i, 0)), b, pl.BlockSpec((tn, kk), lambda i, j, k: (j, 0)))], NT,
        (m // tm, n // tn, 1), 1,
        [jax.ShapeDtypeStruct((m, n), dt) for dt in out_dtypes],
        [pl.BlockSpec((tm, tn), lambda i, j, k: (i, j))] * len(out_dtypes),
        extras, extra_specs, epilogue, rider=rider)


def _mm_nt_blocked(name, a_list, b3_list, tm, rider=None):
    m = a_list[0].shape[0]
    nb = b3_list[0].shape[2]
    pairs = [(a, pl.BlockSpec((tm, nb), lambda i, j, k: (i, k)), b3, pl.BlockSpec((None, D, nb), lambda i, j, k: (k, 0, 0)))
             for a, b3 in zip(a_list, b3_list)]
    return _single(_mm(name, pairs, NT, (m // tm, 1, NDEV), NDEV,
                       [jax.ShapeDtypeStruct((m, D), F32)], [pl.BlockSpec((tm, D), lambda i, j, k: (i, 0))],
                       acc_shape=(tm, D), rider=rider), rider)


def _mm_tn(name, a, b, tm, tn, rider=None):
    t, m = a.shape
    n = b.shape[1]
    return _single(_mm(
        name, [(a, pl.BlockSpec((t, tm), lambda i, j, k: (0, i)), b, pl.BlockSpec((t, tn), lambda i, j, k: (0, j)))], TN,
        (m // tm, n // tn, 1), 1,
        [jax.ShapeDtypeStruct((m, n), BF)], [pl.BlockSpec((tm, tn), lambda i, j, k: (i, j))], rider=rider), rider)


def _mm_tn_blocked(name, a, b, rider=None):
    t = a.shape[0]
    nb = b.shape[1] // NDEV
    return _single(_mm(
        name, [(a, pl.BlockSpec((t, D), lambda i, j, k: (0, 0)), b, pl.BlockSpec((t, nb), lambda i, j, k: (0, j)))], TN,
        (1, NDEV, 1), 1,
        [jax.ShapeDtypeStruct((NDEV, D, nb), BF)], [pl.BlockSpec((None, D, nb), lambda i, j, k: (j, 0, 0))], rider=rider), rider)


def _dw_gate_up(name, h, dgate, dup, rider=None):
    t = h.shape[0]
    nb = F // NDEV

    def compute(ins, outs, _):
        hv = ins[0][...]
        outs[0][...] = lax.dot_general(hv, ins[1][...], TN, preferred_element_type=F32).astype(BF)
        outs[1][...] = lax.dot_general(hv, ins[2][...], TN, preferred_element_type=F32).astype(BF)

    d_spec = pl.BlockSpec((t, nb), lambda j: (0, j))
    o_spec = pl.BlockSpec((None, D, nb), lambda j: (j, 0, 0))
    return _call(name, compute, (NDEV,), [h, dgate, dup], [pl.BlockSpec((t, D), lambda j: (0, 0)), d_spec, d_spec],
                 [jax.ShapeDtypeStruct((NDEV, D, nb), BF)] * 2, [o_spec] * 2, [], ("arbitrary",), rider)


def _dw_square(name, pairs):
    t = pairs[0][0].shape[0]
    tm = 512
    n = len(pairs)

    def compute(ins, outs, _):
        for p in range(n):
            outs[p][...] = lax.dot_general(ins[2 * p][...], ins[2 * p + 1][...], TN, preferred_element_type=F32).astype(BF)

    return _call(name, compute, (D // tm,), [x for pair in pairs for x in pair],
                 [pl.BlockSpec((t, tm), lambda i: (0, i)), pl.BlockSpec((t, D), lambda i: (0, 0))] * n,
                 [jax.ShapeDtypeStruct((D, D), BF)] * n, [pl.BlockSpec((tm, D), lambda i: (i, 0))] * n, [], ("arbitrary",))


def _swiglu_bwd(da, gate, up):
    gate = gate.astype(F32)
    s = jax.nn.sigmoid(gate)
    return da * up.astype(F32) * (s * (1.0 + gate * (1.0 - s))), da * (gate * s)


def _ffn_dact_dh(name, df, wd, gate, up, wg, wu, rider=None):
    t = df.shape[0]
    tm = min(t, 1024)
    nb = F // NDEV

    def compute(ins, outs, scr):
        acc = scr[0]
        j = pl.program_id(1)
        da = lax.dot_general(ins[0][...], ins[1][...], NT, preferred_element_type=F32)
        dgate, dup = _swiglu_bwd(da, ins[2][...], ins[3][...])
        dgate, dup = dgate.astype(BF), dup.astype(BF)
        outs[0][...] = dgate
        outs[1][...] = dup
        part = (lax.dot_general(dgate, ins[4][...], NT, preferred_element_type=F32)
                + lax.dot_general(dup, ins[5][...], NT, preferred_element_type=F32))

        @pl.when(j == 0)
        def _():
            acc[...] = part

        @pl.when(j > 0)
        def _():
            acc[...] += part

        @pl.when(j == NDEV - 1)
        def _():
            outs[2][...] = acc[...]

    blk = pl.BlockSpec((tm, nb), lambda i, j: (i, j))
    w3 = pl.BlockSpec((None, D, nb), lambda i, j: (j, 0, 0))
    row = pl.BlockSpec((tm, D), lambda i, j: (i, 0))
    return _call(name, compute, (t // tm, NDEV), [df, wd, gate, up, wg, wu],
                 [row, pl.BlockSpec((nb, D), lambda i, j: (j, 0)), blk, blk, w3, w3],
                 [jax.ShapeDtypeStruct((t, F), BF)] * 2 + [jax.ShapeDtypeStruct((t, D), F32)], [blk, blk, row],
                 [pltpu.VMEM((tm, D), F32)], ("parallel", "arbitrary"), rider)


def _rowcall(name, fn, ins, in_specs, n_row_out, out_shapes, out_specs, grid, scratch_shapes=(), rider=None, aliases=None):
    def accumulate(o, v, i):
        @pl.when(i == 0)
        def _():
            o[...] = v.astype(o.dtype)

        @pl.when(i > 0)
        def _():
            o[...] += v.astype(o.dtype)

    def compute(in_refs, out_refs, scr):
        i = pl.program_id(0)
        vals = fn(i, in_refs, scr)
        for idx, (o, v) in enumerate(zip(out_refs, vals)):
            if idx < n_row_out:
                o[...] = v.astype(o.dtype)
            else:
                accumulate(o, v, i)

    return _call(name, compute, (grid,), ins, in_specs, out_shapes, out_specs, list(scratch_shapes), ("arbitrary",), rider, aliases)


def _rows(tr, w=D, cb=0):
    return pl.BlockSpec((tr, w), lambda i: (i, cb))


def _whole(shape):
    nd = len(shape)
    return pl.BlockSpec(shape, lambda i: (0,) * nd)


def _vec(n=1):
    return jax.ShapeDtypeStruct((n, D), F32)


def _rms_mod(x, gain, sc, sh):
    y = x * lax.rsqrt(jnp.mean(x * x, axis=-1, keepdims=True) + EPS)
    return (y * gain) * (1.0 + sc) + sh


def _layer_norm(x, g, b):
    mu = jnp.mean(x, axis=-1, keepdims=True)
    var = jnp.mean(jnp.square(x - mu), axis=-1, keepdims=True)
    return (x - mu) * lax.rsqrt(var + EPS) * g + b


def _gate_grads(dx, f, g, scale):
    return scale * g * dx, jnp.sum(scale * dx * f.astype(F32), axis=0, keepdims=True)


def _norm_mod_bwd(name, x, gain, sc, sh, dh, dres, below=None, rider=None):
    t = x.shape[0]
    tr = min(t, 512)

    def fn(i, r, _):
        _, vjp = jax.vjp(_rms_mod, r[0][...], r[1][...], r[2][...], r[3][...])
        dx, dgain, dsc, dsh = vjp(r[4][...])
        dx = dx + r[5][...]
        if below is None:
            return [dx, dgain, dsc, dsh]
        df, dg = _gate_grads(dx, r[6][...], r[7][...], below[2])
        return [dx, df, dgain, dsc, dsh, dg]

    ins, specs = [x, gain, sc, sh, dh, dres], [_rows(tr)] + [_whole((1, D))] * 3 + [_rows(tr)] * 2
    outs, out_specs = [jax.ShapeDtypeStruct((t, D), F32)], [_rows(tr)]
    if below is not None:
        ins, specs = ins + [below[0], below[1]], specs + [_rows(tr), _whole((1, D))]
        outs, out_specs = outs + [jax.ShapeDtypeStruct((t, D), BF)], out_specs + [_rows(tr)]
    n_vec = 3 if below is None else 4
    return _rowcall(name, fn, ins, specs, len(outs), outs + [_vec()] * n_vec, out_specs + [_whole((1, D))] * n_vec, t // tr,
                    rider=rider)


def _sgu_pre(up, vp, bu, bv, ln_g, ln_b):
    return jax.nn.gelu(up + bu), _layer_norm(jax.nn.gelu(vp + bv), ln_g, ln_b)


def _causal(w_ref, h):
    rows = lax.broadcasted_iota(jnp.int32, (CHUNK, CHUNK), 0)
    cols = lax.broadcasted_iota(jnp.int32, (CHUNK, CHUNK), 1)
    return jnp.where(cols <= rows, w_ref[h], 0.0)


def _sgu(name, proj, b_in, ln_g, ln_b, w_s, bias_full, rider=None):
    t = proj.shape[0]

    def fn(i, r, _):
        u, v = _sgu_pre(r[0][...], r[1][...], r[2][...], r[3][...], r[4][...], r[5][...])
        vb = v.astype(BF)
        mixed = [jnp.dot(_causal(r[6], h).astype(BF), vb[:, h * CHUNK:(h + 1) * CHUNK], preferred_element_type=F32)
                 for h in range(HEADS)]
        return [u * (jnp.concatenate(mixed, axis=1) + r[7][...])]

    return _rowcall(
        name, fn, [proj, proj, b_in, b_in, ln_g, ln_b, w_s, bias_full],
        [_rows(CHUNK, D, 0), _rows(CHUNK, D, 1), pl.BlockSpec((1, D), lambda i: (0, 0)), pl.BlockSpec((1, D), lambda i: (0, 1)),
         _whole((1, D)), _whole((1, D)), _whole((HEADS, CHUNK, CHUNK)), _whole((CHUNK, D))],
        1, [jax.ShapeDtypeStruct((t, D), BF)], [_rows(CHUNK)], t // CHUNK, rider=rider)


def _sgu_bwd(name, proj, b_in, ln_g, ln_b, w_s, bias_full, dout, dproj, rider=None):
    t = proj.shape[0]

    def fn(i, r, _):
        (u, v), vjp = jax.vjp(_sgu_pre, r[0][...], r[1][...], r[2][...], r[3][...], r[4][...], r[5][...])
        vb = v.astype(BF)
        d = r[8][...]
        masks = [_causal(r[6], h).astype(BF) for h in range(HEADS)]
        cols = [slice(h * CHUNK, (h + 1) * CHUNK) for h in range(HEADS)]
        mixed = jnp.concatenate([jnp.dot(masks[h], vb[:, cols[h]], preferred_element_type=F32) for h in range(HEADS)], axis=1)
        du = d * (mixed + r[7][...])
        dmix = d * u
        dmb = dmix.astype(BF)
        dv = jnp.concatenate([lax.dot_general(masks[h], dmb[:, cols[h]], TN, preferred_element_type=F32) for h in range(HEADS)], axis=1)
        rows = lax.broadcasted_iota(jnp.int32, (CHUNK, CHUNK), 0)
        lanes = lax.broadcasted_iota(jnp.int32, (CHUNK, CHUNK), 1)
        dws = jnp.stack([jnp.where(lanes <= rows, lax.dot_general(dmb[:, cols[h]], vb[:, cols[h]], NT, preferred_element_type=F32), 0.0)
                         for h in range(HEADS)])
        dbs = jnp.zeros((CHUNK, CHUNK), F32)
        for h in range(HEADS):
            dbs = dbs + jnp.where(lanes == h, jnp.sum(dmix[:, cols[h]], axis=1, keepdims=True), 0.0)
        dup, dvp, dbu, dbv, dg, db = vjp((du, dv))
        return [jnp.concatenate([dup, dvp], axis=1), dbu, dbv, dg, db, dws, dbs]

    return _rowcall(
        name, fn, [proj, proj, b_in, b_in, ln_g, ln_b, w_s, bias_full, dout, dproj],
        [_rows(CHUNK, D, 0), _rows(CHUNK, D, 1), pl.BlockSpec((1, D), lambda i: (0, 0)), pl.BlockSpec((1, D), lambda i: (0, 1)),
         _whole((1, D)), _whole((1, D)), _whole((HEADS, CHUNK, CHUNK)), _whole((CHUNK, D)), _rows(CHUNK),
         pl.BlockSpec(memory_space=pl.ANY)],
        1, [jax.ShapeDtypeStruct(dproj.shape, dproj.dtype)] + [_vec()] * 4
        + [jax.ShapeDtypeStruct((HEADS, CHUNK, CHUNK), F32), jax.ShapeDtypeStruct((CHUNK, CHUNK), F32)],
        [pl.BlockSpec((CHUNK, 2 * D), lambda i: (i, 0))] + [_whole((1, D))] * 4 + [_whole((HEADS, CHUNK, CHUNK)), _whole((CHUNK, CHUNK))],
        t // CHUNK, rider=rider, aliases={9: 0})


def _halo_before(tr, cb):
    return pl.BlockSpec((HALO, D), lambda i: (jnp.maximum(i * (tr // HALO) - 1, 0), cb))


def _halo_after(tr, cb, n_tiles):
    return pl.BlockSpec((HALO, D), lambda i: (jnp.minimum((i + 1) * (tr // HALO), n_tiles * (tr // HALO) - 1), cb))


def _ln_silu(z, g, b):
    return _silu(_layer_norm(z, g, b))


SUBLANES = 8
LANES = 128
CONV_STRIP = 16
DW_STRIP = 32


def _shifted_copies(buf, copies, rows):
    for b in range(1, SUBLANES):
        copies[b - 1, pl.ds(0, rows), :] = buf[pl.ds(b, rows), :]


def _shifted(buf, copies, offset, start, rows, lanes=slice(None)):
    at = pl.ds(pl.multiple_of(start + SUBLANES * (offset // SUBLANES), SUBLANES), rows)
    return buf[at, lanes] if offset % SUBLANES == 0 else copies[offset % SUBLANES - 1, at, lanes]


def _accumulate(o, v, i):
    @pl.when(i == 0)
    def _():
        o[...] = v.astype(o.dtype)

    @pl.when(i > 0)
    def _():
        o[...] += v.astype(o.dtype)


def _conv(name, proj, b_in, conv_w, conv_b, ln_g, ln_b, rider=None):
    t = proj.shape[0]
    tr = min(t, 256)

    def compute(r, outs, scr):
        zbuf, zs = scr
        i = pl.program_id(0)
        bv, bg = r[4][...], r[5][...]
        z0 = (r[0][...] + bv) * jax.nn.sigmoid(r[1][...] + bg)
        before = (r[2][...] + bv) * jax.nn.sigmoid(r[3][...] + bg)
        zbuf[pl.ds(0, HALO), :] = jnp.where(i > 0, before, 0.0)
        zbuf[pl.ds(HALO, tr), :] = z0
        outs[0][...] = z0
        _shifted_copies(zbuf, zs, tr + HALO - SUBLANES)

        def strip(s, carry):
            r0 = s * CONV_STRIP
            acc = jnp.zeros((CONV_STRIP, D), F32) + r[7][...]
            for k in range(KW):
                acc = acc + r[6][k:k + 1, :] * _shifted(zbuf, zs, HALO - (KW - 1) + k, r0, CONV_STRIP)
            outs[1][pl.ds(pl.multiple_of(r0, SUBLANES), CONV_STRIP), :] = acc
            return carry

        lax.fori_loop(0, tr // CONV_STRIP, strip, 0)
        outs[2][...] = _ln_silu(outs[1][...], r[8][...], r[9][...]).astype(BF)

    return _call(
        name, compute, (t // tr,), [proj, proj, proj, proj, b_in, b_in, conv_w, conv_b, ln_g, ln_b],
        [_rows(tr, D, 2), _rows(tr, D, 3), _halo_before(tr, 2), _halo_before(tr, 3),
         pl.BlockSpec((1, D), lambda i: (0, 2)), pl.BlockSpec((1, D), lambda i: (0, 3)),
         _whole((HALO, D)), _whole((1, D)), _whole((1, D)), _whole((1, D))],
        [jax.ShapeDtypeStruct((t, D), F32), jax.ShapeDtypeStruct((t, D), F32), jax.ShapeDtypeStruct((t, D), BF)],
        [_rows(tr)] * 3, [pltpu.VMEM((tr + HALO, D), F32), pltpu.VMEM((SUBLANES - 1, tr + HALO, D), F32)], ("arbitrary",), rider)


def _conv_bwd(name, proj, b_in, conv_w, ln_g, ln_b, z0, z1, dz3, dproj, rider=None):
    t = proj.shape[0]
    tr = min(t, 256)
    n_tiles = t // tr

    def compute(r, outs, scr):
        zbuf, dbuf, zs, ds, dwacc = scr
        i = pl.program_id(0)
        g, b = r[5][...], r[6][...]
        zero_row = jnp.zeros((1, D), F32)
        _, vjp = jax.vjp(_ln_silu, r[9][...], g, b)
        dz1, dg, db = vjp(r[11][...])
        dcb = jnp.sum(dz1, axis=0, keepdims=True)
        _, vjp_after = jax.vjp(_ln_silu, r[10][...], g, b)
        dz1_after = vjp_after(r[12][...])[0]
        dbuf[pl.ds(0, tr), :] = dz1
        dbuf[pl.ds(tr, HALO), :] = jnp.where(i < n_tiles - 1, dz1_after, 0.0)
        zbuf[pl.ds(0, HALO), :] = jnp.where(i > 0, r[8][...], 0.0)
        zbuf[pl.ds(HALO, tr), :] = r[7][...]
        _shifted_copies(dbuf, ds, tr + HALO - SUBLANES)
        _shifted_copies(zbuf, zs, tr + HALO - SUBLANES)

        def dz0_strip(s, carry):
            r0 = s * CONV_STRIP
            at = pl.ds(pl.multiple_of(r0, CONV_STRIP), CONV_STRIP)
            acc = jnp.zeros((CONV_STRIP, D), F32)
            for k in range(KW):
                acc = acc + r[4][k:k + 1, :] * _shifted(dbuf, ds, KW - 1 - k, r0, CONV_STRIP)
            a = r[0][at, :] + r[2][...]
            sg = jax.nn.sigmoid(r[1][at, :] + r[3][...])
            dcv = acc * sg
            dcg = acc * a * sg * (1.0 - sg)
            outs[0][at, :] = jnp.concatenate([dcv, dcg], axis=1).astype(BF)
            return carry[0] + jnp.sum(dcv, axis=0, keepdims=True), carry[1] + jnp.sum(dcg, axis=0, keepdims=True)

        dbv, dbg = lax.fori_loop(0, tr // CONV_STRIP, dz0_strip, (zero_row, zero_row))

        for lb in range(D // LANES):
            lanes = slice(lb * LANES, (lb + 1) * LANES)

            def dw_strip(s, accs, lanes=lanes):
                r0 = s * DW_STRIP
                dz = dbuf[pl.ds(pl.multiple_of(r0, SUBLANES), DW_STRIP), lanes]
                out = []
                for k in range(KW):
                    prod = dz * _shifted(zbuf, zs, HALO - (KW - 1) + k, r0, DW_STRIP, lanes)
                    part = prod[0:SUBLANES]
                    for q in range(1, DW_STRIP // SUBLANES):
                        part = part + prod[q * SUBLANES:(q + 1) * SUBLANES]
                    out.append(accs[k] + part)
                return tuple(out)

            accs = lax.fori_loop(0, tr // DW_STRIP, dw_strip, tuple(jnp.zeros((SUBLANES, LANES), F32) for _ in range(KW)))
            for k in range(KW):
                dwacc[pl.ds(k * SUBLANES, SUBLANES), lanes] = accs[k]
        dw_rows = [jnp.sum(dwacc[pl.ds(k * SUBLANES, SUBLANES), :], axis=0, keepdims=True) for k in range(KW)]
        dw_rows.append(jnp.zeros((HALO - KW, D), F32))
        for o, v in zip(outs[1:], (dbv, dbg, jnp.concatenate(dw_rows, axis=0), dcb, dg, db)):
            _accumulate(o, v, i)

    wide = pl.BlockSpec((tr, 2 * D), lambda i: (i, 1))
    return _call(
        name, compute, (n_tiles,), [proj, proj, b_in, b_in, conv_w, ln_g, ln_b, z0, z0, z1, z1, dz3, dz3, dproj],
        [_rows(tr, D, 2), _rows(tr, D, 3), pl.BlockSpec((1, D), lambda i: (0, 2)), pl.BlockSpec((1, D), lambda i: (0, 3)),
         _whole((HALO, D)), _whole((1, D)), _whole((1, D)),
         _rows(tr), _halo_before(tr, 0), _rows(tr), _halo_after(tr, 0, n_tiles), _rows(tr), _halo_after(tr, 0, n_tiles),
         pl.BlockSpec(memory_space=pl.ANY)],
        [jax.ShapeDtypeStruct(dproj.shape, dproj.dtype), _vec(), _vec(), _vec(HALO), _vec(), _vec(), _vec()],
        [wide] + [_whole((1, D))] * 2 + [_whole((HALO, D))] + [_whole((1, D))] * 3,
        [pltpu.VMEM((tr + HALO, D), F32), pltpu.VMEM((tr + HALO, D), F32),
         pltpu.VMEM((SUBLANES - 1, tr + HALO, D), F32), pltpu.VMEM((SUBLANES - 1, tr + HALO, D), F32),
         pltpu.VMEM((HALO * SUBLANES, D), F32)],
        ("arbitrary",), rider, aliases={13: 0})


def _merge_fn(ga, gb, bga, bgb, ya, yb):
    return jax.nn.sigmoid(ga + bga) * ya + jax.nn.sigmoid(gb + bgb) * yb


def _mix_tail(name, ua, z3, proj, b_in, wa, wb, wo, x, g, next_norm):
    t = ua.shape[0]
    tr = min(t, 256)

    def compute(r, outs, _):
        ya = jnp.dot(r[0][...], r[6][...], preferred_element_type=F32)
        yb = jnp.dot(r[1][...], r[7][...], preferred_element_type=F32)
        merged = _merge_fn(r[2][...], r[3][...], r[4][...], r[5][...], ya, yb).astype(BF)
        y = jnp.dot(merged, r[8][...], preferred_element_type=F32)
        x_out = r[9][...] + r[10][...] * y
        for o, v in zip(outs, (ya, yb, merged, y, x_out, _rms_mod(x_out, r[11][...], r[12][...], r[13][...]))):
            o[...] = v.astype(o.dtype)

    row = _whole((1, D))
    return _call(
        name, compute, (t // tr,), [ua, z3, proj, proj, b_in, b_in, wa, wb, wo, x, g, *next_norm],
        [_rows(tr), _rows(tr), _rows(tr, D, 4), _rows(tr, D, 5), pl.BlockSpec((1, D), lambda i: (0, 4)),
         pl.BlockSpec((1, D), lambda i: (0, 5)), _whole((D, D)), _whole((D, D)), _whole((D, D)), _rows(tr), row, row, row, row],
        [jax.ShapeDtypeStruct((t, D), dt) for dt in (F32, F32, BF, BF, F32, BF)], [_rows(tr)] * 6, [], ("arbitrary",))


def _mix_tail_bwd(name, dy, proj, b_in, ya, yb, wa, wb, wo, rider=None):
    t = proj.shape[0]
    tr = min(t, 256)

    def compute(r, outs, _):
        i = pl.program_id(0)
        dm = lax.dot_general(r[0][...], r[9][...], NT, preferred_element_type=F32)
        _, vjp = jax.vjp(_merge_fn, *[x[...] for x in r[1:7]])
        dga, dgb, dbga, dbgb, dya, dyb = vjp(dm)
        dya, dyb = dya.astype(BF), dyb.astype(BF)
        outs[0][...] = jnp.concatenate([dga, dgb], axis=1).astype(BF)
        outs[1][...] = dya
        outs[2][...] = dyb
        outs[3][...] = lax.dot_general(dya, r[7][...], NT, preferred_element_type=F32)
        outs[4][...] = lax.dot_general(dyb, r[8][...], NT, preferred_element_type=F32)
        _accumulate(outs[5], dbga, i)
        _accumulate(outs[6], dbgb, i)

    return _call(
        name, compute, (t // tr,), [dy, proj, proj, b_in, b_in, ya, yb, wa, wb, wo],
        [_rows(tr), _rows(tr, D, 4), _rows(tr, D, 5), pl.BlockSpec((1, D), lambda i: (0, 4)), pl.BlockSpec((1, D), lambda i: (0, 5)),
         _rows(tr), _rows(tr), _whole((D, D)), _whole((D, D)), _whole((D, D))],
        [jax.ShapeDtypeStruct((t, D_IN), BF)] + [jax.ShapeDtypeStruct((t, D), BF)] * 2 + [jax.ShapeDtypeStruct((t, D), F32)] * 2
        + [_vec(), _vec()],
        [pl.BlockSpec((tr, 2 * D), lambda i: (i, 2))] + [_rows(tr)] * 4 + [_whole((1, D))] * 2, [], ("arbitrary",), rider)


def _loss_head(name, x, gain, target, f, g, scale):
    t = x.shape[0]
    tr = min(t, 512)

    def loss_fn(xv, gn, tgt):
        y = xv * lax.rsqrt(jnp.mean(xv * xv, axis=-1, keepdims=True) + EPS) * gn
        return 0.5 * jnp.sum(jnp.mean(jnp.square(y - tgt), axis=-1))

    def fn(i, r, _):
        loss, vjp = jax.vjp(loss_fn, r[0][...], r[1][...], r[2][...])
        dx, dgain, _ = vjp(jnp.ones((), F32))
        df, dg = _gate_grads(dx, r[3][...], r[4][...], scale)
        return [dx, df, dgain, jnp.zeros((1, D), F32) + loss, dg]

    return _rowcall(name, fn, [x, gain, target, f, g], [_rows(tr), _whole((1, D)), _rows(tr), _rows(tr), _whole((1, D))], 2,
                    [jax.ShapeDtypeStruct((t, D), F32), jax.ShapeDtypeStruct((t, D), BF), _vec(), _vec(), _vec()],
                    [_rows(tr)] * 2 + [_whole((1, D))] * 3, t // tr)


def _adamw(w, g, m, v):
    m = B1 * m + (1.0 - B1) * g
    v = B2 * v + (1.0 - B2) * jnp.square(g)
    m_hat = m / BC1
    v_hat = v / BC2
    delta = -LR * (m_hat / (jnp.sqrt(v_hat) + ADAM_EPS) + WD * w)
    return delta, m, v


ADAMW_ROWS = 64


def _adamw_group(name, items, rider=None, rows=ADAMW_ROWS):
    ins, in_specs, out_shapes, out_specs, plan = [], [], [], [], []
    first = 0
    for chip_sum, received, w, m, v in items:
        r, c = w.shape
        tr = min(r, rows)
        n = r // tr

        def tile(i, first=first, n=n):
            return jnp.clip(i - first, 0, n - 1)

        spec = pl.BlockSpec((tr, c), lambda i, tile=tile: (tile(i), 0))
        ins += [chip_sum, *received, w, m, v]
        in_specs += [pl.BlockSpec((None, tr, c), lambda i, tile=tile: (0, tile(i), 0))]
        in_specs += [pl.BlockSpec((g.shape[0], tr, c), lambda i, tile=tile: (0, tile(i), 0)) for g in received]
        in_specs += [spec] * 3
        out_shapes += [jax.ShapeDtypeStruct((r, c), F32)] * 4
        out_specs += [spec] * 4
        plan.append((first, n, [g.shape[0] for g in received]))
        first += n

    def compute(in_refs, out_refs, _):
        i = pl.program_id(0)
        at_in = at_out = 0
        for start, n, counts in plan:
            mine = in_refs[at_in:at_in + 4 + len(counts)]
            outs = out_refs[at_out:at_out + 4]
            at_in += 4 + len(counts)
            at_out += 4

            @pl.when(jnp.logical_and(i >= start, i < start + n))
            def _(mine=mine, outs=outs, counts=counts):
                g = mine[0][...].astype(F32)
                for j, count in enumerate(counts):
                    for s in range(count):
                        g = g + mine[1 + j][s].astype(F32)
                delta, m_new, v_new = _adamw(mine[-3][...], g, mine[-2][...], mine[-1][...])
                for o, val in zip(outs, (g, delta, m_new, v_new)):
                    o[...] = val

    res = _call(name, compute, (first,), ins, in_specs, out_shapes, out_specs, [], ("arbitrary",), rider)
    outs, rode = res if rider else (res, [])
    return [outs[4 * j:4 * j + 4] for j in range(len(items))], rode


def _adamw_small(name, packed_all, late_all, dws_all, vectors, w_s):
    n_vec = len(vectors)

    def body(*refs):
        p_ref, l_ref, d_ref = refs[:3]
        param_refs = refs[3:3 + 3 * n_vec + 3]
        out_refs = refs[3 + 3 * n_vec + 3:-1]
        g_ref = refs[-1]
        g = p_ref[0]
        late = l_ref[0]
        for s in range(1, NDEV):
            g = g + p_ref[s]
            late = late + l_ref[s]
        g_ref[...] = g
        g_ref[pl.ds(0, R_LATE), :] += late

        def update(gp, wmv, outs):
            delta, m_new, v_new = _adamw(wmv[0][...], gp, wmv[1][...], wmv[2][...])
            for o, val in zip(outs, (gp, delta, m_new, v_new)):
                o[...] = val

        for j, (row, rows, *_) in enumerate(vectors):
            pieces = [g_ref[pl.ds(row + r, 1), :] for r in range(rows)]
            update(pieces[0] if rows == 1 else jnp.concatenate(pieces, axis=1), param_refs[3 * j:3 * j + 3], out_refs[4 * j:4 * j + 4])
        gw = d_ref[0]
        for s in range(1, NDEV):
            gw = gw + d_ref[s]
        update(gw, param_refs[3 * n_vec:], out_refs[4 * n_vec:4 * n_vec + 4])
        out_refs[-2][...] = g_ref[pl.ds(R_CW, KW), :]
        out_refs[-1][...] = g_ref[pl.ds(R_LOSS, 1), :]

    params = [a for _, _, w, m, v in vectors for a in (w, m, v)] + list(w_s)
    out_shapes = [jax.ShapeDtypeStruct(w.shape, F32) for _, _, w, _, _ in vectors for _ in range(4)]
    out_shapes += [jax.ShapeDtypeStruct(w_s[0].shape, F32)] * 4 + [jax.ShapeDtypeStruct((KW, D), F32), _vec()]
    res = pl.pallas_call(body, name=name, out_shape=out_shapes, scratch_shapes=[pltpu.VMEM((R_TOTAL, D), F32)],
                         compiler_params=_params(None))(packed_all, late_all, dws_all, *params)
    return [res[4 * j:4 * j + 4] for j in range(n_vec + 1)], res[-2], res[-1]


def _adamw_plain(name, g, w, m, v):
    def body(g_ref, w_ref, m_ref, v_ref, d_ref, mo_ref, vo_ref):
        delta, m_new, v_new = _adamw(w_ref[...], g_ref[...], m_ref[...], v_ref[...])
        d_ref[...] = delta
        mo_ref[...] = m_new
        vo_ref[...] = v_new

    return pl.pallas_call(body, name=name, out_shape=[jax.ShapeDtypeStruct(w.shape, F32)] * 3,
                          compiler_params=_params(None))(g, w, m, v)


def _adamw_ada(name, c_all_t, dmod, dmod_late, w, m, v):
    r, c = w.shape
    tr = 256

    def fn(i, refs, _):
        ca = _silu(refs[0][...])
        dm = refs[1][...] + refs[2][...]
        g = ca[:, 0:1] * dm[0:1, :]
        for b in range(1, NDEV):
            g = g + ca[:, b:b + 1] * dm[b:b + 1, :]
        delta, m_new, v_new = _adamw(refs[3][...], g, refs[4][...], refs[5][...])
        return [g, delta, m_new, v_new]

    spec = pl.BlockSpec((tr, c), lambda i: (i, 0))
    whole = pl.BlockSpec((NDEV, c), lambda i: (0, 0))
    return _rowcall(name, fn, [c_all_t, dmod, dmod_late, w, m, v],
                    [pl.BlockSpec((tr, NDEV), lambda i: (i, 0)), whole, whole, spec, spec, spec], 4,
                    [jax.ShapeDtypeStruct((r, c), F32)] * 4, [spec] * 4, r // tr)


def _ffn_fwd(tag, x, h, g, wg, wu, wd_shard, down_rider, next_norm=None, more_shards=(), norm=None):
    t = x.shape[0]
    tm = min(t, 512 if down_rider else 1024)
    (gate, up, act, *normed), (wd, *more) = _ffn_up(f"{tag}_up", x if norm else h, wg, wu,
                                                    rider=_gather_rider([wd_shard, *more_shards]), norm=norm)
    h = normed[0] if norm else h
    row = pl.BlockSpec((1, D), lambda i, j, k: (0, 0))

    def epilogue(f, xv, gv, *norm):
        x_out = xv + 0.5 * gv * f
        return (x_out, f, _rms_mod(x_out, *norm)) if norm else (x_out, f)

    res = _mm_nn(f"{tag}_down", act, wd.reshape(F, D), tm, D, 1024, extras=(x, g, *(next_norm or ())),
                 extra_specs=(pl.BlockSpec((tm, D), lambda i, j, k: (i, 0)), row, *([row] * 3 if next_norm else [])),
                 epilogue=epilogue, out_dtypes=(F32, BF, BF) if next_norm else (F32, BF), rider=down_rider)
    (x_out, f, *h_next), rode = res if down_rider else (res, None)
    return x_out, (h_next[0] if next_norm else None), (x, h, gate, up, act, f), wd, rode, more


def _ffn_bwd(tag, dx_out, df, saved, gain, sh, sc, wg, wu, wd, slots, dact_rider=None, dwd_rider=None, dwgu_rider=None,
             below=None, fuse_dh=False):
    x, h, gate, up, act, f = saved
    t = x.shape[0]
    tm = min(t, 1024)
    if fuse_dh:
        dgate, dup, dh = _ffn_dact_dh(f"{tag}_dact_dh", df, wd.reshape(F, D), gate, up, wg, wu)
        dwd = _mm_tn(f"{tag}_dwd", act, df, 512, D).reshape(NDEV, F // NDEV, D)
        (dwg, dwu), (sib_d,) = _dw_gate_up(f"{tag}_dwgu", h, dgate, dup, rider=_pair_rider([dwd]))
        (sum_d,) = _pair_add(f"{tag}_dwd_add", [dwd], [sib_d], slots)
        normed, (sib_g, sib_u) = _norm_mod_bwd(f"{tag}_norm_bwd", x, gain, sc, sh, dh, dx_out, below=below,
                                               rider=_pair_rider([dwg, dwu]))
        sum_g, sum_u = _pair_add(f"{tag}_dwgu_add", [dwg, dwu], [sib_g, sib_u], slots)
        return normed, (sum_d, None), sum_g, sum_u, [], [], []

    blk = pl.BlockSpec((t, F // NDEV), lambda i, j, k: (i, j))
    res = _mm_nt(f"{tag}_dact", df, wd.reshape(F, D), t, F // NDEV, out_dtypes=(BF, BF),
                 extras=(gate, up), extra_specs=(blk, blk), epilogue=_swiglu_bwd, rider=dact_rider)
    (dgate, dup), rode_dact = res if dact_rider else (res, [])
    res = _mm_tn(f"{tag}_dwd", act, df, 512, D, rider=dwd_rider)
    dwd, rode_dwd = res if dwd_rider else (res, [])
    dwd = dwd.reshape(NDEV, F // NDEV, D)
    (dwg, dwu), (sib_d, *rode_dwgu) = _dw_gate_up(f"{tag}_dwgu", h, dgate, dup,
                                                  rider=[_pair_rider([dwd])] + ([dwgu_rider] if dwgu_rider else []))
    (sum_d,) = _pair_add(f"{tag}_dwd_add", [dwd], [sib_d], slots)
    dh, (sib_g, sib_u, got_d) = _mm_nt_blocked(f"{tag}_dh", [dgate, dup], [wg, wu], tm,
                                               rider=[_pair_rider([dwg, dwu]), _chip_rider([sum_d])])
    sum_g, sum_u = _pair_add(f"{tag}_dwgu_add", [dwg, dwu], [sib_g, sib_u], slots)
    normed = _norm_mod_bwd(f"{tag}_norm_bwd", x, gain, sc, sh, dh, dx_out, below=below)
    return normed, (sum_d, [got_d]), sum_g, sum_u, rode_dact, rode_dwd, rode_dwgu


def kernel(x, c, ada_w, ada_b, norm_ffn1, ffn1_w_gate, ffn1_w_up, ffn1_w_down, norm_mix, mix_w_in, mix_b_in, sgu_ln_g, sgu_ln_b, sgu_w_s, sgu_b_s, conv_w, conv_b, conv_ln_g, conv_ln_b, w_branch_a, w_branch_b, w_out, norm_ffn2, ffn2_w_gate, ffn2_w_up, ffn2_w_down, norm_final, loss_target, m_ada_w, m_ada_b, m_norm_ffn1, m_ffn1_w_gate, m_ffn1_w_up, m_ffn1_w_down, m_norm_mix, m_mix_w_in, m_mix_b_in, m_sgu_ln_g, m_sgu_ln_b, m_sgu_w_s, m_sgu_b_s, m_conv_w, m_conv_b, m_conv_ln_g, m_conv_ln_b, m_w_branch_a, m_w_branch_b, m_w_out, m_norm_ffn2, m_ffn2_w_gate, m_ffn2_w_up, m_ffn2_w_down, m_norm_final, v_ada_w, v_ada_b, v_norm_ffn1, v_ffn1_w_gate, v_ffn1_w_up, v_ffn1_w_down, v_norm_mix, v_mix_w_in, v_mix_b_in, v_sgu_ln_g, v_sgu_ln_b, v_sgu_w_s, v_sgu_b_s, v_conv_w, v_conv_b, v_conv_ln_g, v_conv_ln_b, v_w_branch_a, v_w_branch_b, v_w_out, v_norm_ffn2, v_ffn2_w_gate, v_ffn2_w_up, v_ffn2_w_down, v_norm_final):
    mx, my, mc = _position()
    me = 4 * mx + 2 * my + mc
    chip = 2 * mx + my
    slots = jnp.stack([2 * (chip ^ k) + mc for k in range(N_CHIPS)]).astype(jnp.int32)
    t = x.shape[1]
    tm = min(t, 1024)
    x0 = x.reshape(t, D)
    target = loss_target.reshape(t, D)
    given = dict(ffn1_w_gate=(ffn1_w_gate, m_ffn1_w_gate, v_ffn1_w_gate), ffn1_w_up=(ffn1_w_up, m_ffn1_w_up, v_ffn1_w_up),
                 ffn1_w_down=(ffn1_w_down, m_ffn1_w_down, v_ffn1_w_down), mix_w_in=(mix_w_in, m_mix_w_in, v_mix_w_in),
                 w_branch_a=(w_branch_a, m_w_branch_a, v_w_branch_a), w_branch_b=(w_branch_b, m_w_branch_b, v_w_branch_b),
                 w_out=(w_out, m_w_out, v_w_out), ffn2_w_gate=(ffn2_w_gate, m_ffn2_w_gate, v_ffn2_w_gate),
                 ffn2_w_up=(ffn2_w_up, m_ffn2_w_up, v_ffn2_w_up), ffn2_w_down=(ffn2_w_down, m_ffn2_w_down, v_ffn2_w_down))
    shard = {n: wmv[0][0].astype(BF) for n, wmv in given.items()}

    ada_cols = N_MOD * D // NDEV
    c_all, taps_all, mod_all, (wg1, wu1) = _prologue(
        "prologue", jnp.pad(c, ((0, SUBLANES - 1), (0, 0))), jnp.pad(conv_w[0], ((0, HALO - KW), (0, 0))), ada_w[0],
        lax.dynamic_slice(ada_b, (0, me * ada_cols), (1, ada_cols)), [shard["ffn1_w_gate"], shard["ffn1_w_up"]])
    conv_w_full = jnp.transpose(taps_all.reshape(NDEV, HALO, CHUNK), (1, 0, 2)).reshape(HALO, D)
    mod = lax.dynamic_index_in_dim(mod_all.reshape(NDEV, NDEV, ada_cols), me, axis=1, keepdims=False).reshape(N_MOD, 1, D)
    sh1, sc1, g1, sh2, sc2, g2, sh3, sc3, g3 = [mod[i] for i in range(N_MOD)]

    x1, h2, saved1, wd1, (w_in,), (wa3, wb3) = _ffn_fwd(
        "ffn1", x0, None, g1, wg1, wu1, shard["ffn1_w_down"], _gather_rider([shard["mix_w_in"]]),
        next_norm=(norm_mix, sc2, sh2), more_shards=(shard["w_branch_a"], shard["w_branch_b"]), norm=(norm_ffn1, sc1, sh1))
    proj, (wg2, wo3) = _mm_nn_blocked("mix_in", h2, w_in, tm, rider=_gather_rider([shard["ffn2_w_gate"], shard["w_out"]]))
    bias_full = jnp.repeat(sgu_b_s[0].T, CHUNK, axis=1)
    (ua,) = _sgu("sgu", proj, mix_b_in, sgu_ln_g, sgu_ln_b, sgu_w_s[0], bias_full)
    (z0, z1, z3), (wu2,) = _conv("conv", proj, mix_b_in, conv_w_full, conv_b, conv_ln_g, conv_ln_b,
                                 rider=_gather_rider([shard["ffn2_w_up"]]))
    wa, wb, wo = wa3.reshape(D, D), wb3.reshape(D, D), wo3.reshape(D, D)
    ya, yb, merged, y, x2, h3 = _mix_tail("mix_tail", ua, z3, proj, mix_b_in, wa, wb, wo, x1, g2, (norm_ffn2, sc3, sh3))
    x3, _, saved3, wd2, _, _ = _ffn_fwd("ffn2", x2, h3, g3, wg2, wu2, shard["ffn2_w_down"], None)

    norm_final2 = norm_final.reshape(1, D)
    dx3, df3, d_norm_final, loss_row, dg3 = _loss_head("loss_head", x3, norm_final2, target, saved3[-1], g3, 0.5)
    (dx2, dy, d_norm_ffn2, dsc3, dsh3, dg2), (sum_d2, _), sum_g2, sum_u2, _, _, _ = _ffn_bwd(
        "ffn2", dx3, df3, saved3, norm_ffn2, sh3, sc3, wg2, wu2, wd2, slots, below=(y, g2, 1.0), fuse_dh=True)
    (dproj, dya, dyb, dua, dz3, db_ga, db_gb), (got_g2_near,) = _mix_tail_bwd(
        "mix_tail_bwd", dy, proj, mix_b_in, ya, yb, wa, wb, wo, rider=_chip_rider([sum_g2], NEIGHBOURS))
    dwo, dwa, dwb = [g.reshape(NDEV, D // NDEV, D) for g in _dw_square("mix_dw", [(merged, dy), (ua, dya), (z3, dyb)])]
    (dproj, db_u, db_v, d_sgu_g, d_sgu_b, d_ws, d_bs_t), (*sib_abo, got_g2_far) = _sgu_bwd(
        "sgu_bwd", proj, mix_b_in, sgu_ln_g, sgu_ln_b, sgu_w_s[0], bias_full, dua, dproj,
        rider=[_pair_rider([dwa, dwb, dwo]), _chip_rider([sum_g2], DIAGONAL)])
    sum_a, sum_b, sum_o = _pair_add("mix_dw_add", [dwa, dwb, dwo], sib_abo, slots)
    (dproj, db_cv, db_cg, d_cw, d_cb, d_cln_g, d_cln_b), (got_u2, got_d2) = _conv_bwd(
        "conv_bwd", proj, mix_b_in, conv_w_full, conv_ln_g, conv_ln_b, z0, z1, dz3, dproj, rider=_chip_rider([sum_u2, sum_d2]))
    dwin, (got_a, got_b, got_o) = _mm_tn_blocked("mix_dwin", h2, dproj, rider=_chip_rider([sum_a, sum_b, sum_o]))

    d_bs = jnp.transpose(d_bs_t[:, :HEADS])
    zero = jnp.zeros((1, D), F32)
    pack_rows = [zero, zero, zero, zero, zero, dg2, dsh3, dsc3, dg3,
                 zero, zero, d_norm_ffn2, d_norm_final,
                 db_u, db_v, db_cv, db_cg, db_ga, db_gb,
                 d_sgu_g, d_sgu_b, d_bs.reshape(1, D), d_cb, d_cln_g, d_cln_b,
                 d_cw[:KW], loss_row, jnp.zeros((R_TOTAL - R_LOSS - 1, D), F32)]
    packed = jnp.concatenate(pack_rows, axis=0)
    d_ws2 = d_ws.reshape(HEADS * CHUNK, CHUNK)
    dh2, (sib_in, packed_all, dws_all) = _mm_nt_blocked("mix_in_bwd", [dproj], [w_in], tm,
                                                        rider=[_pair_rider([dwin]), _gather_rider([packed, d_ws2])])
    (sum_in,) = _pair_add("mix_dwin_add", [dwin], [sib_in], slots)
    dx1, df1, d_norm_mix, dsc2, dsh2, dg1 = _norm_mod_bwd("mix_norm_bwd", x1, norm_mix, sc2, sh2, dh2, dx2,
                                                          below=(saved1[-1], g1, 0.5))
    (dx0, d_norm_ffn1, dsc1, dsh1), down1, sum_g1, sum_u1, (got_in_near,), _, (got_in_far,) = _ffn_bwd(
        "ffn1", dx1, df1, saved1, norm_ffn1, sh1, sc1, wg1, wu1, wd1, slots,
        dact_rider=_chip_rider([sum_in], NEIGHBOURS), dwgu_rider=_chip_rider([sum_in], DIAGONAL))
    packed_late = jnp.concatenate([dsh1, dsc1, dg1, dsh2, dsc2, jnp.zeros((4, D), F32), d_norm_ffn1, d_norm_mix,
                                   jnp.zeros((R_LATE - 11, D), F32)], axis=0)
    grads = dict(ffn2_w_gate=(sum_g2, [got_g2_near, got_g2_far]), ffn2_w_up=(sum_u2, [got_u2]), ffn2_w_down=(sum_d2, [got_d2]),
                 mix_w_in=(sum_in, [got_in_near, got_in_far]), w_branch_a=(sum_a, [got_a]), w_branch_b=(sum_b, [got_b]),
                 w_out=(sum_o, [got_o]), ffn1_w_down=down1)
    done, (late_all, got_g1, got_u1) = _adamw_group(
        "adamw_most", [(cs, got, *[a[0] for a in given[n]]) for n, (cs, got) in grads.items()],
        rider=[_gather_rider([packed_late]), _chip_rider([sum_g1, sum_u1])])
    last, _ = _adamw_group("adamw_ffn1_in", [(sum_g1, [got_g1], *[a[0] for a in given["ffn1_w_gate"]]),
                                            (sum_u1, [got_u1], *[a[0] for a in given["ffn1_w_up"]])], rows=256)
    big_out = {n: [o.reshape(given[n][0].shape) for o in outs]
               for n, outs in zip([*grads, "ffn1_w_gate", "ffn1_w_up"], [*done, *last])}

    flat = lambda a: a.reshape(1, -1)
    vectors = [("ada_b", 0, 9, ada_b, m_ada_b, v_ada_b), ("norm_ffn1", 9, 1, norm_ffn1, m_norm_ffn1, v_norm_ffn1),
               ("norm_mix", 10, 1, norm_mix, m_norm_mix, v_norm_mix), ("norm_ffn2", 11, 1, norm_ffn2, m_norm_ffn2, v_norm_ffn2),
               ("norm_final", 12, 1, norm_final, m_norm_final, v_norm_final), ("mix_b_in", 13, 6, mix_b_in, m_mix_b_in, v_mix_b_in),
               ("sgu_ln_g", 19, 1, sgu_ln_g, m_sgu_ln_g, v_sgu_ln_g), ("sgu_ln_b", 20, 1, sgu_ln_b, m_sgu_ln_b, v_sgu_ln_b),
               ("sgu_b_s", 21, 1, sgu_b_s, m_sgu_b_s, v_sgu_b_s), ("conv_b", 22, 1, conv_b, m_conv_b, v_conv_b),
               ("conv_ln_g", 23, 1, conv_ln_g, m_conv_ln_g, v_conv_ln_g), ("conv_ln_b", 24, 1, conv_ln_b, m_conv_ln_b, v_conv_ln_b)]
    small_out, d_cw_all, loss_sum = _adamw_small(
        "adamw_small", packed_all, late_all, dws_all, [(row, rows, flat(wv), flat(mv), flat(vv)) for _, row, rows, wv, mv, vv in vectors],
        [a.reshape(HEADS * CHUNK, CHUNK) for a in (sgu_w_s, m_sgu_w_s, v_sgu_w_s)])
    small = {n: [o.reshape(wv.shape) for o in outs] for (n, _, _, wv, _, _), outs in zip(vectors, small_out)}
    small["sgu_w_s"] = [o.reshape(sgu_w_s.shape) for o in small_out[-1]]
    g_cw = lax.dynamic_slice(d_cw_all, (0, me * CHUNK), (KW, CHUNK))
    small["conv_w"] = [o.reshape(conv_w.shape) for o in (g_cw, *_adamw_plain("adamw_conv_w", g_cw, conv_w[0], m_conv_w[0], v_conv_w[0]))]
    loss = loss_sum[0, 0]

    dmod_cols = [lax.dynamic_slice(a[:, :N_MOD, :].reshape(NDEV, N_MOD * D), (0, me * ada_cols), (NDEV, ada_cols))
                 for a in (packed_all, late_all)]
    ada_out = [o.reshape(ada_w.shape) for o in _adamw_ada("adamw_ada_w", jnp.transpose(c_all), *dmod_cols, ada_w[0], m_ada_w[0], v_ada_w[0])]

    order = ["ada_w", "ada_b", "norm_ffn1", "ffn1_w_gate", "ffn1_w_up", "ffn1_w_down", "norm_mix", "mix_w_in", "mix_b_in",
             "sgu_ln_g", "sgu_ln_b", "sgu_w_s", "sgu_b_s", "conv_w", "conv_b", "conv_ln_g", "conv_ln_b", "w_branch_a",
             "w_branch_b", "w_out", "norm_ffn2", "ffn2_w_gate", "ffn2_w_up", "ffn2_w_down", "norm_final"]

    def leaf(n, kind):
        if n == "ada_w":
            return ada_out[kind]
        if n in big_out:
            return big_out[n][kind]
        return small[n][kind]

    return (loss, dx0.reshape(x.shape), *[leaf(n, kind) for kind in range(4) for n in order])
```

```python
import jax
import jax.numpy as jnp
from jax import lax
from jax.experimental import pallas as pl
from jax.experimental.pallas import tpu as pltpu

D = 1024
F = 4 * D
D_IN = 6 * D
HEADS = 8
CHUNK = 128
KW = 31
HALO = 32
N_MOD = 9
NDEV = 8
N_CHIPS = 4
EPS = 1e-6
LR, B1, B2, ADAM_EPS, WD, STEP = 0.001, 0.9, 0.999, 1e-08, 0.01, 10
BC1 = 1.0 - B1 ** STEP
BC2 = 1.0 - B2 ** STEP
VMEM_LIMIT = 56 * 1024 * 1024
MESH = pl.DeviceIdType.MESH
HBM = pl.BlockSpec(memory_space=pltpu.HBM)
VMEM = pl.BlockSpec(memory_space=pltpu.VMEM)
BF = jnp.bfloat16
F32 = jnp.float32

NN = (((1,), (0,)), ((), ()))
NT = (((1,), (1,)), ((), ()))
TN = (((0,), (0,)), ((), ()))

R_CW, R_LOSS, R_TOTAL = 25, 56, 64
R_LATE = 16


def _params(sem):
    return pltpu.CompilerParams(dimension_semantics=sem, vmem_limit_bytes=VMEM_LIMIT)


def _position():
    return lax.axis_index("x"), lax.axis_index("y"), lax.axis_index("c")


def _flip(pos, k):
    x, y, c = pos
    return (x ^ (k >> 2 & 1), y ^ (k >> 1 & 1), c ^ (k & 1))


def _index(pos):
    return 4 * pos[0] + 2 * pos[1] + pos[2]


def _gather_rows(x_ref, out_ref, send_sems, recv_sems, local_sem):
    m_per = x_ref.shape[0]
    x, y, c = _position()
    me, sibling = (x, y, c), (x, y, 1 - c)
    chips = [(1 - x, y), (x, 1 - y), (1 - x, 1 - y)]

    def rows(pos):
        return out_ref.at[pl.ds(_index(pos) * m_per, m_per), :]

    def copy(k, block, to, src=None):
        return pltpu.make_async_remote_copy(
            src_ref=rows(block) if src is None else src, dst_ref=rows(block),
            send_sem=send_sems.at[k], recv_sem=recv_sems.at[k], device_id=to, device_id_type=MESH)

    mine = pltpu.make_async_copy(x_ref, rows(me), local_sem)
    mine.start()
    first = [copy(0, me, sibling, src=x_ref)]
    first += [copy(1 + j, me, (*chip, c), src=x_ref) for j, chip in enumerate(chips)]
    for cp in first:
        cp.start()
    passed = [copy(4 + j, (*chip, c), sibling) for j, chip in enumerate(chips)]
    for j, chip in enumerate(chips):
        copy(1 + j, (*chip, c), me).wait_recv()
        passed[j].start()
    copy(0, sibling, me).wait_recv()
    for j, chip in enumerate(chips):
        copy(4 + j, (*chip, 1 - c), me).wait_recv()
    for cp in first + passed:
        cp.wait_send()
    mine.wait()


def _prologue(name, c_rows, taps, ada_w, ada_b, shards):
    rider = _gather_rider(shards)
    n = len(shards)
    nc = ada_w.shape[1]

    def body(*refs):
        c_ref, taps_ref, w_ref, b_ref = refs[:4]
        shard_refs = refs[4:4 + n]
        c_all_ref, taps_all_ref, mod_all_ref = refs[4 + n:7 + n]
        gathered_refs = refs[7 + n:7 + 2 * n]
        c_buf, mod_part, sems = refs[7 + 2 * n], refs[8 + 2 * n], refs[9 + 2 * n:]
        rider.start(shard_refs, gathered_refs, sems[9:])
        _gather_rows(c_ref, c_buf, *sems[0:3])
        rider.mid(shard_refs, gathered_refs, sems[9:])
        c_all = jnp.concatenate([c_buf[pl.ds(d * SUBLANES, 1), :] for d in range(NDEV)], axis=0)
        c_all_ref[...] = c_all
        mod_part[...] = jnp.dot(_silu(c_all), w_ref[...], preferred_element_type=F32) + b_ref[...]
        _gather_rows(taps_ref, taps_all_ref, *sems[3:6])
        _gather_rows(mod_part, mod_all_ref, *sems[6:9])
        rider.relay(shard_refs, gathered_refs, sems[9:])
        rider.finish(shard_refs, gathered_refs, sems[9:])

    small_sems = [pltpu.SemaphoreType.DMA((7,)), pltpu.SemaphoreType.DMA((7,)), pltpu.SemaphoreType.DMA] * 3
    res = pl.pallas_call(
        body, name=name,
        out_shape=[jax.ShapeDtypeStruct((NDEV, D), F32), jax.ShapeDtypeStruct((NDEV * taps.shape[0], taps.shape[1]), F32),
                   jax.ShapeDtypeStruct((NDEV * NDEV, nc), F32)] + rider.out_shapes,
        in_specs=[VMEM] * 4 + [HBM] * n, out_specs=[VMEM] * 3 + [HBM] * n,
        scratch_shapes=[pltpu.VMEM((NDEV * SUBLANES, D), F32), pltpu.VMEM((NDEV, nc), F32)] + small_sems + rider.sems,
        compiler_params=_params(None),
    )(c_rows, taps, ada_w, ada_b, *shards)
    return res[0], res[1], res[2], res[3:]


class _Rider:
    def __init__(self, ins, out_shapes, sems, start, finish, mid=None, relay=None):
        self.ins, self.out_shapes, self.sems = list(ins), list(out_shapes), list(sems)
        self.start, self.finish, self.mid, self.relay = start, finish, mid, relay


def _gather_rider(shards):
    n = len(shards)

    def setup(ins, outs, sems):
        send_sems, recv_sems, local_sems = sems
        x, y, c = _position()
        places = dict(me=(x, y, c), sibling=(x, y, 1 - c), xn=(1 - x, y, c), yn=(x, 1 - y, c), diagonal=(1 - x, 1 - y, c),
                      passed_on=(x ^ c, y ^ (1 - c), c), passed_to=(x ^ (1 - c), y ^ c, c))

        def copy(a, k, block, to, own=False):
            slot = outs[a].at[_index(block)]
            return pltpu.make_async_remote_copy(
                src_ref=ins[a] if own else slot, dst_ref=slot,
                send_sem=send_sems.at[k, a], recv_sem=recv_sems.at[k, a], device_id=to, device_id_type=MESH)

        def local(a):
            return pltpu.make_async_copy(ins[a], outs[a].at[_index(places["me"])], local_sems.at[a])

        return places, copy, local

    def start(ins, outs, sems):
        p, copy, local = setup(ins, outs, sems)
        for a in range(n):
            local(a).start()
            for k, to in enumerate(("sibling", "xn", "yn")):
                copy(a, k, p["me"], p[to], own=True).start()

    def mid(ins, outs, sems):
        p, copy, _ = setup(ins, outs, sems)
        for a in range(n):
            copy(a, 1, p["xn"], p["me"]).wait_recv()
            copy(a, 2, p["yn"], p["me"]).wait_recv()
            copy(a, 3, p["passed_on"], p["passed_to"]).start()
            copy(a, 4, p["xn"], p["sibling"]).start()
            copy(a, 5, p["yn"], p["sibling"]).start()

    def relay(ins, outs, sems):
        p, copy, _ = setup(ins, outs, sems)
        for a in range(n):
            copy(a, 3, p["diagonal"], p["me"]).wait_recv()
            copy(a, 6, p["diagonal"], p["sibling"]).start()

    def finish(ins, outs, sems):
        p, copy, local = setup(ins, outs, sems)
        x, y, c = p["me"]
        for a in range(n):
            for k, block in ((0, (x, y, 1 - c)), (4, (1 - x, y, 1 - c)), (5, (x, 1 - y, 1 - c)), (6, (1 - x, 1 - y, 1 - c))):
                copy(a, k, block, p["me"]).wait_recv()
            for k, to in enumerate(("sibling", "xn", "yn")):
                copy(a, k, p["me"], p[to], own=True).wait_send()
            copy(a, 3, p["passed_on"], p["passed_to"]).wait_send()
            for k, block in ((4, "xn"), (5, "yn"), (6, "diagonal")):
                copy(a, k, p[block], p["sibling"]).wait_send()
            local(a).wait()

    return _Rider(shards, [jax.ShapeDtypeStruct((NDEV, *s.shape), s.dtype) for s in shards],
                  [pltpu.SemaphoreType.DMA((7, n)), pltpu.SemaphoreType.DMA((7, n)), pltpu.SemaphoreType.DMA((n,))],
                  start, finish, mid, relay)


def _pair_rider(parts):
    n = len(parts)

    def copies(ins, outs, sems):
        send_sems, recv_sems = sems
        x, y, c = _position()
        q = 2 * x + y
        return [pltpu.make_async_remote_copy(
            src_ref=ins[a].at[2 * (q ^ k) + (1 - c)], dst_ref=outs[a].at[k],
            send_sem=send_sems.at[k, a], recv_sem=recv_sems.at[k, a], device_id=(x, y, 1 - c), device_id_type=MESH)
            for a in range(n) for k in range(N_CHIPS)]

    def start(ins, outs, sems):
        for cp in copies(ins, outs, sems):
            cp.start()

    def finish(ins, outs, sems):
        for cp in copies(ins, outs, sems):
            cp.wait()

    return _Rider(parts, [jax.ShapeDtypeStruct((N_CHIPS, *p.shape[1:]), p.dtype) for p in parts],
                  [pltpu.SemaphoreType.DMA((N_CHIPS, n)), pltpu.SemaphoreType.DMA((N_CHIPS, n))], start, finish)


NEIGHBOURS = (1, 2)
DIAGONAL = (3,)
OTHER_CHIPS = NEIGHBOURS + DIAGONAL


def _chip_rider(sums, ks=OTHER_CHIPS):
    n = len(sums)

    def copies(ins, outs, sems):
        send_sems, recv_sems = sems
        me = _position()
        return [pltpu.make_async_remote_copy(
            src_ref=ins[a].at[k], dst_ref=outs[a].at[j],
            send_sem=send_sems.at[j, a], recv_sem=recv_sems.at[j, a], device_id=_flip(me, 2 * k), device_id_type=MESH)
            for a in range(n) for j, k in enumerate(ks)]

    def start(ins, outs, sems):
        for cp in copies(ins, outs, sems):
            cp.start()

    def finish(ins, outs, sems):
        for cp in copies(ins, outs, sems):
            cp.wait()

    return _Rider(sums, [jax.ShapeDtypeStruct((len(ks), *s.shape[1:]), s.dtype) for s in sums],
                  [pltpu.SemaphoreType.DMA((len(ks), n)), pltpu.SemaphoreType.DMA((len(ks), n))], start, finish)


def _grid_edge(grid, last):
    cond = None
    for d, n in enumerate(grid):
        here = pl.program_id(d) == (n - 1 if last else 0)
        cond = here if cond is None else jnp.logical_and(cond, here)
    return cond


def _call(name, compute, grid, ins, in_specs, out_shapes, out_specs, scratch_shapes, semantics, rider=None, aliases=None):
    riders = [rider] if isinstance(rider, _Rider) else list(rider or [])
    n_in, n_out, n_scr = len(ins), len(out_shapes), len(scratch_shapes)
    n_rin, n_rout, n_rsem = [sum(len(part(r)) for r in riders) for part in (lambda r: r.ins, lambda r: r.out_shapes, lambda r: r.sems)]
    cuts = [0, n_in, n_in + n_rin, n_in + n_rin + n_out, n_in + n_rin + n_out + n_rout, n_in + n_rin + n_out + n_rout + n_scr]

    def body(*refs):
        in_refs, rin_refs, out_refs, rout_refs, scr_refs = [refs[a:b] for a, b in zip(cuts[:-1], cuts[1:])]
        rsem_refs = refs[cuts[-1]:]
        mine, at = [], [0, 0, 0]
        for r in riders:
            mine.append((r, rin_refs[at[0]:at[0] + len(r.ins)], rout_refs[at[1]:at[1] + len(r.out_shapes)],
                         rsem_refs[at[2]:at[2] + len(r.sems)]))
            at = [at[0] + len(r.ins), at[1] + len(r.out_shapes), at[2] + len(r.sems)]
        if riders:
            @pl.when(_grid_edge(grid, last=False))
            def _():
                for r, a, b, c in mine:
                    r.start(a, b, c)

        if any(r.mid for r in riders):
            step, steps = 0, 1
            for d, size in enumerate(grid):
                step, steps = step * size + pl.program_id(d), steps * size

            @pl.when(step == steps * 5 // 8)
            def _():
                for r, a, b, c in mine:
                    if r.mid:
                        r.mid(a, b, c)

        if any(r.relay for r in riders):
            @pl.when(_grid_edge(grid, last=True))
            def _():
                for r, a, b, c in mine:
                    if r.relay:
                        r.relay(a, b, c)

        compute(in_refs, out_refs, scr_refs)
        if riders:
            @pl.when(_grid_edge(grid, last=True))
            def _():
                for r, a, b, c in mine:
                    r.finish(a, b, c)

    res = pl.pallas_call(
        body, name=name, grid=grid,
        out_shape=list(out_shapes) + [s for r in riders for s in r.out_shapes],
        in_specs=list(in_specs) + [HBM] * n_rin, out_specs=list(out_specs) + [HBM] * n_rout,
        scratch_shapes=list(scratch_shapes) + [s for r in riders for s in r.sems],
        input_output_aliases=aliases or {}, compiler_params=_params(semantics),
    )(*ins, *[a for r in riders for a in r.ins])
    return (res[:n_out], res[n_out:]) if riders else res


def _pair_add(name, parts, from_sibling, slots):
    n = len(parts)

    def body(s_ref, *refs):
        for a in range(n):
            refs[2 * n + a][...] = (refs[a][...].astype(F32) + refs[n + a][...].astype(F32)).astype(refs[2 * n + a].dtype)

    def slab(p, picked):
        _, r, c = p.shape
        return pl.BlockSpec((None, r, c), (lambda k, s: (s[k], 0, 0)) if picked else (lambda k, s: (k, 0, 0)))

    return pl.pallas_call(
        body, name=name,
        grid_spec=pltpu.PrefetchScalarGridSpec(
            num_scalar_prefetch=1, grid=(N_CHIPS,),
            in_specs=[slab(p, True) for p in parts] + [slab(p, False) for p in parts],
            out_specs=[slab(p, False) for p in parts]),
        out_shape=[jax.ShapeDtypeStruct((N_CHIPS, *p.shape[1:]), p.dtype) for p in parts],
        compiler_params=_params(("arbitrary",)),
    )(slots, *parts, *from_sibling)


def _mm(name, pairs, dims, grid, nk, out_shapes, out_specs, extras=(), extra_specs=(), epilogue=None, acc_shape=None, rider=None):
    n_pairs = len(pairs)

    def compute(ins, outs, scratch):
        def partial_sum():
            total = None
            for p in range(n_pairs):
                d = lax.dot_general(ins[2 * p][...], ins[2 * p + 1][...], dims, preferred_element_type=F32)
                total = d if total is None else total + d
            return total

        def finish(r):
            ex = [e[...] for e in ins[2 * n_pairs:]]
            res = epilogue(r, *ex) if epilogue is not None else (r,)
            for o, v in zip(outs, res):
                o[...] = v.astype(o.dtype)

        if nk == 1:
            finish(partial_sum())
        else:
            acc = scratch[0]
            k = pl.program_id(2)

            @pl.when(k == 0)
            def _():
                acc[...] = partial_sum()

            @pl.when(k > 0)
            def _():
                acc[...] += partial_sum()

            @pl.when(k == nk - 1)
            def _():
                finish(acc[...])

    operands, specs = [], []
    for a, a_spec, b, b_spec in pairs:
        operands += [a, b]
        specs += [a_spec, b_spec]
    return _call(name, compute, grid, operands + list(extras), specs + list(extra_specs), out_shapes, out_specs,
                 [pltpu.VMEM(acc_shape, F32)] if nk > 1 else [], ("parallel", "parallel", "arbitrary"), rider)


def _single(res, rider):
    return (res[0][0], res[1]) if rider else res[0]


def _silu(x):
    return x * jax.nn.sigmoid(x)


def _ffn_up(name, h, wg, wu, rider=None, norm=None):
    t = h.shape[0]
    tm = min(t, 1024)
    nb = F // NDEV

    def compute(ins, outs, scr):
        if norm:
            @pl.when(pl.program_id(1) == 0)
            def _():
                scr[0][...] = _rms_mod(ins[0][...], ins[3][...], ins[4][...], ins[5][...]).astype(BF)
                outs[3][...] = scr[0][...]

            hv = scr[0][...]
        else:
            hv = ins[0][...]
        g = jnp.dot(hv, ins[1][...], preferred_element_type=F32)
        u = jnp.dot(hv, ins[2][...], preferred_element_type=F32)
        outs[0][...] = g.astype(BF)
        outs[1][...] = u.astype(BF)
        outs[2][...] = (_silu(g) * u).astype(BF)

    w_spec = pl.BlockSpec((None, D, nb), lambda i, j: (j, 0, 0))
    o_spec = pl.BlockSpec((tm, nb), lambda i, j: (i, j))
    rows = pl.BlockSpec((tm, D), lambda i, j: (i, 0))
    vec = pl.BlockSpec((1, D), lambda i, j: (0, 0))
    return _call(name, compute, (t // tm, NDEV), [h, wg, wu, *(norm or ())], [rows, w_spec, w_spec] + [vec] * (3 if norm else 0),
                 [jax.ShapeDtypeStruct((t, F), BF)] * 3 + ([jax.ShapeDtypeStruct((t, D), BF)] if norm else []),
                 [o_spec] * 3 + ([rows] if norm else []), [pltpu.VMEM((tm, D), BF)] if norm else [],
                 ("parallel", "arbitrary"), rider)


def _mm_nn(name, a, b, tm, tn, tk, extras=(), extra_specs=(), epilogue=None, out_dtypes=(F32,), rider=None):
    m, kk = a.shape
    n = b.shape[1]
    nk = kk // tk
    return _mm(
        name, [(a, pl.BlockSpec((tm, tk), lambda i, j, k: (i, k)), b, pl.BlockSpec((tk, tn), lambda i, j, k: (k, j)))], NN,
        (m // tm, n // tn, nk), nk,
        [jax.ShapeDtypeStruct((m, n), dt) for dt in out_dtypes],
        [pl.BlockSpec((tm, tn), lambda i, j, k: (i, j))] * len(out_dtypes),
        extras, extra_specs, epilogue, (tm, tn), rider)


def _mm_nn_blocked(name, a, b3, tm, rider=None):
    m = a.shape[0]
    nb = b3.shape[2]
    return _single(_mm(
        name, [(a, pl.BlockSpec((tm, D), lambda i, j, k: (i, 0)), b3, pl.BlockSpec((None, D, nb), lambda i, j, k: (j, 0, 0)))], NN,
        (m // tm, NDEV, 1), 1,
        [jax.ShapeDtypeStruct((m, NDEV * nb), BF)], [pl.BlockSpec((tm, nb), lambda i, j, k: (i, j))], rider=rider), rider)


def _mm_nt(name, a, b, tm, tn, out_dtypes=(F32,), extras=(), extra_specs=(), epilogue=None, rider=None):
    m, kk = a.shape
    n = b.shape[0]
    return _mm(
        name, [(a, pl.BlockSpec((tm, kk), lambda i, j, k: (i, 0)), b, pl.BlockSpec((tn, kk), lambda i, j, k: (j, 0)))], NT,
        (m // tm, n // tn, 1), 1,
        [jax.ShapeDtypeStruct((m, n), dt) for dt in out_dtypes],
        [pl.BlockSpec((tm, tn), lambda i, j, k: (i, j))] * len(out_dtypes),
        extras, extra_specs, epilogue, rider=rider)


def _mm_nt_blocked(name, a_list, b3_list, tm, rider=None):
    m = a_list[0].shape[0]
    nb = b3_list[0].shape[2]
    pairs = [(a, pl.BlockSpec((tm, nb), lambda i, j, k: (i, k)), b3, pl.BlockSpec((None, D, nb), lambda i, j, k: (k, 0, 0)))
             for a, b3 in zip(a_list, b3_list)]
    return _single(_mm(name, pairs, NT, (m // tm, 1, NDEV), NDEV,
                       [jax.ShapeDtypeStruct((m, D), F32)], [pl.BlockSpec((tm, D), lambda i, j, k: (i, 0))],
                       acc_shape=(tm, D), rider=rider), rider)


def _mm_tn(name, a, b, tm, tn, rider=None):
    t, m = a.shape
    n = b.shape[1]
    return _single(_mm(
        name, [(a, pl.BlockSpec((t, tm), lambda i, j, k: (0, i)), b, pl.BlockSpec((t, tn), lambda i, j, k: (0, j)))], TN,
        (m // tm, n // tn, 1), 1,
        [jax.ShapeDtypeStruct((m, n), BF)], [pl.BlockSpec((tm, tn), lambda i, j, k: (i, j))], rider=rider), rider)


def _mm_tn_blocked(name, a, b, rider=None):
    t = a.shape[0]
    nb = b.shape[1] // NDEV
    return _single(_mm(
        name, [(a, pl.BlockSpec((t, D), lambda i, j, k: (0, 0)), b, pl.BlockSpec((t, nb), lambda i, j, k: (0, j)))], TN,
        (1, NDEV, 1), 1,
        [jax.ShapeDtypeStruct((NDEV, D, nb), BF)], [pl.BlockSpec((None, D, nb), lambda i, j, k: (j, 0, 0))], rider=rider), rider)


def _dw_gate_up(name, h, dgate, dup, rider=None):
    t = h.shape[0]
    nb = F // NDEV

    def compute(ins, outs, _):
        hv = ins[0][...]
        outs[0][...] = lax.dot_general(hv, ins[1][...], TN, preferred_element_type=F32).astype(BF)
        outs[1][...] = lax.dot_general(hv, ins[2][...], TN, preferred_element_type=F32).astype(BF)

    d_spec = pl.BlockSpec((t, nb), lambda j: (0, j))
    o_spec = pl.BlockSpec((None, D, nb), lambda j: (j, 0, 0))
    return _call(name, compute, (NDEV,), [h, dgate, dup], [pl.BlockSpec((t, D), lambda j: (0, 0)), d_spec, d_spec],
                 [jax.ShapeDtypeStruct((NDEV, D, nb), BF)] * 2, [o_spec] * 2, [], ("arbitrary",), rider)


def _dw_square(name, pairs):
    t = pairs[0][0].shape[0]
    tm = 512
    n = len(pairs)

    def compute(ins, outs, _):
        for p in range(n):
            outs[p][...] = lax.dot_general(ins[2 * p][...], ins[2 * p + 1][...], TN, preferred_element_type=F32).astype(BF)

    return _call(name, compute, (D // tm,), [x for pair in pairs for x in pair],
                 [pl.BlockSpec((t, tm), lambda i: (0, i)), pl.BlockSpec((t, D), lambda i: (0, 0))] * n,
                 [jax.ShapeDtypeStruct((D, D), BF)] * n, [pl.BlockSpec((tm, D), lambda i: (i, 0))] * n, [], ("arbitrary",))


def _swiglu_bwd(da, gate, up):
    gate = gate.astype(F32)
    s = jax.nn.sigmoid(gate)
    return da * up.astype(F32) * (s * (1.0 + gate * (1.0 - s))), da * (gate * s)


def _ffn_dact_dh(name, df, wd, gate, up, wg, wu, rider=None):
    t = df.shape[0]
    tm = min(t, 1024)
    nb = F // NDEV

    def compute(ins, outs, scr):
        acc = scr[0]
        j = pl.program_id(1)
        da = lax.dot_general(ins[0][...], ins[1][...], NT, preferred_element_type=F32)
        dgate, dup = _swiglu_bwd(da, ins[2][...], ins[3][...])
        dgate, dup = dgate.astype(BF), dup.astype(BF)
        outs[0][...] = dgate
        outs[1][...] = dup
        part = (lax.dot_general(dgate, ins[4][...], NT, preferred_element_type=F32)
                + lax.dot_general(dup, ins[5][...], NT, preferred_element_type=F32))

        @pl.when(j == 0)
        def _():
            acc[...] = part

        @pl.when(j > 0)
        def _():
            acc[...] += part

        @pl.when(j == NDEV - 1)
        def _():
            outs[2][...] = acc[...]

    blk = pl.BlockSpec((tm, nb), lambda i, j: (i, j))
    w3 = pl.BlockSpec((None, D, nb), lambda i, j: (j, 0, 0))
    row = pl.BlockSpec((tm, D), lambda i, j: (i, 0))
    return _call(name, compute, (t // tm, NDEV), [df, wd, gate, up, wg, wu],
                 [row, pl.BlockSpec((nb, D), lambda i, j: (j, 0)), blk, blk, w3, w3],
                 [jax.ShapeDtypeStruct((t, F), BF)] * 2 + [jax.ShapeDtypeStruct((t, D), F32)], [blk, blk, row],
                 [pltpu.VMEM((tm, D), F32)], ("parallel", "arbitrary"), rider)


def _rowcall(name, fn, ins, in_specs, n_row_out, out_shapes, out_specs, grid, scratch_shapes=(), rider=None, aliases=None):
    def accumulate(o, v, i):
        @pl.when(i == 0)
        def _():
            o[...] = v.astype(o.dtype)

        @pl.when(i > 0)
        def _():
            o[...] += v.astype(o.dtype)

    def compute(in_refs, out_refs, scr):
        i = pl.program_id(0)
        vals = fn(i, in_refs, scr)
        for idx, (o, v) in enumerate(zip(out_refs, vals)):
            if idx < n_row_out:
                o[...] = v.astype(o.dtype)
            else:
                accumulate(o, v, i)

    return _call(name, compute, (grid,), ins, in_specs, out_shapes, out_specs, list(scratch_shapes), ("arbitrary",), rider, aliases)


def _rows(tr, w=D, cb=0):
    return pl.BlockSpec((tr, w), lambda i: (i, cb))


def _whole(shape):
    nd = len(shape)
    return pl.BlockSpec(shape, lambda i: (0,) * nd)


def _vec(n=1):
    return jax.ShapeDtypeStruct((n, D), F32)


def _rms_mod(x, gain, sc, sh):
    y = x * lax.rsqrt(jnp.mean(x * x, axis=-1, keepdims=True) + EPS)
    return (y * gain) * (1.0 + sc) + sh


def _layer_norm(x, g, b):
    mu = jnp.mean(x, axis=-1, keepdims=True)
    var = jnp.mean(jnp.square(x - mu), axis=-1, keepdims=True)
    return (x - mu) * lax.rsqrt(var + EPS) * g + b


def _gate_grads(dx, f, g, scale):
    return scale * g * dx, jnp.sum(scale * dx * f.astype(F32), axis=0, keepdims=True)


def _norm_mod_bwd(name, x, gain, sc, sh, dh, dres, below=None, rider=None):
    t = x.shape[0]
    tr = min(t, 512)

    def fn(i, r, _):
        _, vjp = jax.vjp(_rms_mod, r[0][...], r[1][...], r[2][...], r[3][...])
        dx, dgain, dsc, dsh = vjp(r[4][...])
        dx = dx + r[5][...]
        if below is None:
            return [dx, dgain, dsc, dsh]
        df, dg = _gate_grads(dx, r[6][...], r[7][...], below[2])
        return [dx, df, dgain, dsc, dsh, dg]

    ins, specs = [x, gain, sc, sh, dh, dres], [_rows(tr)] + [_whole((1, D))] * 3 + [_rows(tr)] * 2
    outs, out_specs = [jax.ShapeDtypeStruct((t, D), F32)], [_rows(tr)]
    if below is not None:
        ins, specs = ins + [below[0], below[1]], specs + [_rows(tr), _whole((1, D))]
        outs, out_specs = outs + [jax.ShapeDtypeStruct((t, D), BF)], out_specs + [_rows(tr)]
    n_vec = 3 if below is None else 4
    return _rowcall(name, fn, ins, specs, len(outs), outs + [_vec()] * n_vec, out_specs + [_whole((1, D))] * n_vec, t // tr,
                    rider=rider)


def _sgu_pre(up, vp, bu, bv, ln_g, ln_b):
    return jax.nn.gelu(up + bu), _layer_norm(jax.nn.gelu(vp + bv), ln_g, ln_b)


def _causal(w_ref, h):
    rows = lax.broadcasted_iota(jnp.int32, (CHUNK, CHUNK), 0)
    cols = lax.broadcasted_iota(jnp.int32, (CHUNK, CHUNK), 1)
    return jnp.where(cols <= rows, w_ref[h], 0.0)


def _sgu(name, proj, b_in, ln_g, ln_b, w_s, bias_full, rider=None):
    t = proj.shape[0]

    def fn(i, r, _):
        u, v = _sgu_pre(r[0][...], r[1][...], r[2][...], r[3][...], r[4][...], r[5][...])
        vb = v.astype(BF)
        mixed = [jnp.dot(_causal(r[6], h).astype(BF), vb[:, h * CHUNK:(h + 1) * CHUNK], preferred_element_type=F32)
                 for h in range(HEADS)]
        return [u * (jnp.concatenate(mixed, axis=1) + r[7][...])]

    return _rowcall(
        name, fn, [proj, proj, b_in, b_in, ln_g, ln_b, w_s, bias_full],
        [_rows(CHUNK, D, 0), _rows(CHUNK, D, 1), pl.BlockSpec((1, D), lambda i: (0, 0)), pl.BlockSpec((1, D), lambda i: (0, 1)),
         _whole((1, D)), _whole((1, D)), _whole((HEADS, CHUNK, CHUNK)), _whole((CHUNK, D))],
        1, [jax.ShapeDtypeStruct((t, D), BF)], [_rows(CHUNK)], t // CHUNK, rider=rider)


def _sgu_bwd(name, proj, b_in, ln_g, ln_b, w_s, bias_full, dout, dproj, rider=None):
    t = proj.shape[0]

    def fn(i, r, _):
        (u, v), vjp = jax.vjp(_sgu_pre, r[0][...], r[1][...], r[2][...], r[3][...], r[4][...], r[5][...])
        vb = v.astype(BF)
        d = r[8][...]
        masks = [_causal(r[6], h).astype(BF) for h in range(HEADS)]
        cols = [slice(h * CHUNK, (h + 1) * CHUNK) for h in range(HEADS)]
        mixed = jnp.concatenate([jnp.dot(masks[h], vb[:, cols[h]], preferred_element_type=F32) for h in range(HEADS)], axis=1)
        du = d * (mixed + r[7][...])
        dmix = d * u
        dmb = dmix.astype(BF)
        dv = jnp.concatenate([lax.dot_general(masks[h], dmb[:, cols[h]], TN, preferred_element_type=F32) for h in range(HEADS)], axis=1)
        rows = lax.broadcasted_iota(jnp.int32, (CHUNK, CHUNK), 0)
        lanes = lax.broadcasted_iota(jnp.int32, (CHUNK, CHUNK), 1)
        dws = jnp.stack([jnp.where(lanes <= rows, lax.dot_general(dmb[:, cols[h]], vb[:, cols[h]], NT, preferred_element_type=F32), 0.0)
                         for h in range(HEADS)])
        dbs = jnp.zeros((CHUNK, CHUNK), F32)
        for h in range(HEADS):
            dbs = dbs + jnp.where(lanes == h, jnp.sum(dmix[:, cols[h]], axis=1, keepdims=True), 0.0)
        dup, dvp, dbu, dbv, dg, db = vjp((du, dv))
        return [jnp.concatenate([dup, dvp], axis=1), dbu, dbv, dg, db, dws, dbs]

    return _rowcall(
        name, fn, [proj, proj, b_in, b_in, ln_g, ln_b, w_s, bias_full, dout, dproj],
        [_rows(CHUNK, D, 0), _rows(CHUNK, D, 1), pl.BlockSpec((1, D), lambda i: (0, 0)), pl.BlockSpec((1, D), lambda i: (0, 1)),
         _whole((1, D)), _whole((1, D)), _whole((HEADS, CHUNK, CHUNK)), _whole((CHUNK, D)), _rows(CHUNK),
         pl.BlockSpec(memory_space=pl.ANY)],
        1, [jax.ShapeDtypeStruct(dproj.shape, dproj.dtype)] + [_vec()] * 4
        + [jax.ShapeDtypeStruct((HEADS, CHUNK, CHUNK), F32), jax.ShapeDtypeStruct((CHUNK, CHUNK), F32)],
        [pl.BlockSpec((CHUNK, 2 * D), lambda i: (i, 0))] + [_whole((1, D))] * 4 + [_whole((HEADS, CHUNK, CHUNK)), _whole((CHUNK, CHUNK))],
        t // CHUNK, rider=rider, aliases={9: 0})


def _halo_before(tr, cb):
    return pl.BlockSpec((HALO, D), lambda i: (jnp.maximum(i * (tr // HALO) - 1, 0), cb))


def _halo_after(tr, cb, n_tiles):
    return pl.BlockSpec((HALO, D), lambda i: (jnp.minimum((i + 1) * (tr // HALO), n_tiles * (tr // HALO) - 1), cb))


def _ln_silu(z, g, b):
    return _silu(_layer_norm(z, g, b))


SUBLANES = 8
LANES = 128
CONV_STRIP = 16
DW_STRIP = 32


def _shifted_copies(buf, copies, rows):
    for b in range(1, SUBLANES):
        copies[b - 1, pl.ds(0, rows), :] = buf[pl.ds(b, rows), :]


def _shifted(buf, copies, offset, start, rows, lanes=slice(None)):
    at = pl.ds(pl.multiple_of(start + SUBLANES * (offset // SUBLANES), SUBLANES), rows)
    return buf[at, lanes] if offset % SUBLANES == 0 else copies[offset % SUBLANES - 1, at, lanes]


def _accumulate(o, v, i):
    @pl.when(i == 0)
    def _():
        o[...] = v.astype(o.dtype)

    @pl.when(i > 0)
    def _():
        o[...] += v.astype(o.dtype)


def _conv(name, proj, b_in, conv_w, conv_b, ln_g, ln_b, rider=None):
    t = proj.shape[0]
    tr = min(t, 256)

    def compute(r, outs, scr):
        zbuf, zs = scr
        i = pl.program_id(0)
        bv, bg = r[4][...], r[5][...]
        z0 = (r[0][...] + bv) * jax.nn.sigmoid(r[1][...] + bg)
        before = (r[2][...] + bv) * jax.nn.sigmoid(r[3][...] + bg)
        zbuf[pl.ds(0, HALO), :] = jnp.where(i > 0, before, 0.0)
        zbuf[pl.ds(HALO, tr), :] = z0
        outs[0][...] = z0
        _shifted_copies(zbuf, zs, tr + HALO - SUBLANES)

        def strip(s, carry):
            r0 = s * CONV_STRIP
            acc = jnp.zeros((CONV_STRIP, D), F32) + r[7][...]
            for k in range(KW):
                acc = acc + r[6][k:k + 1, :] * _shifted(zbuf, zs, HALO - (KW - 1) + k, r0, CONV_STRIP)
            outs[1][pl.ds(pl.multiple_of(r0, SUBLANES), CONV_STRIP), :] = acc
            return carry

        lax.fori_loop(0, tr // CONV_STRIP, strip, 0)
        outs[2][...] = _ln_silu(outs[1][...], r[8][...], r[9][...]).astype(BF)

    return _call(
        name, compute, (t // tr,), [proj, proj, proj, proj, b_in, b_in, conv_w, conv_b, ln_g, ln_b],
        [_rows(tr, D, 2), _rows(tr, D, 3), _halo_before(tr, 2), _halo_before(tr, 3),
         pl.BlockSpec((1, D), lambda i: (0, 2)), pl.BlockSpec((1, D), lambda i: (0, 3)),
         _whole((HALO, D)), _whole((1, D)), _whole((1, D)), _whole((1, D))],
        [jax.ShapeDtypeStruct((t, D), F32), jax.ShapeDtypeStruct((t, D), F32), jax.ShapeDtypeStruct((t, D), BF)],
        [_rows(tr)] * 3, [pltpu.VMEM((tr + HALO, D), F32), pltpu.VMEM((SUBLANES - 1, tr + HALO, D), F32)], ("arbitrary",), rider)


def _conv_bwd(name, proj, b_in, conv_w, ln_g, ln_b, z0, z1, dz3, dproj, rider=None):
    t = proj.shape[0]
    tr = min(t, 256)
    n_tiles = t // tr

    def compute(r, outs, scr):
        zbuf, dbuf, zs, ds, dwacc = scr
        i = pl.program_id(0)
        g, b = r[5][...], r[6][...]
        zero_row = jnp.zeros((1, D), F32)
        _, vjp = jax.vjp(_ln_silu, r[9][...], g, b)
        dz1, dg, db = vjp(r[11][...])
        dcb = jnp.sum(dz1, axis=0, keepdims=True)
        _, vjp_after = jax.vjp(_ln_silu, r[10][...], g, b)
        dz1_after = vjp_after(r[12][...])[0]
        dbuf[pl.ds(0, tr), :] = dz1
        dbuf[pl.ds(tr, HALO), :] = jnp.where(i < n_tiles - 1, dz1_after, 0.0)
        zbuf[pl.ds(0, HALO), :] = jnp.where(i > 0, r[8][...], 0.0)
        zbuf[pl.ds(HALO, tr), :] = r[7][...]
        _shifted_copies(dbuf, ds, tr + HALO - SUBLANES)
        _shifted_copies(zbuf, zs, tr + HALO - SUBLANES)

        def dz0_strip(s, carry):
            r0 = s * CONV_STRIP
            at = pl.ds(pl.multiple_of(r0, CONV_STRIP), CONV_STRIP)
            acc = jnp.zeros((CONV_STRIP, D), F32)
            for k in range(KW):
                acc = acc + r[4][k:k + 1, :] * _shifted(dbuf, ds, KW - 1 - k, r0, CONV_STRIP)
            a = r[0][at, :] + r[2][...]
            sg = jax.nn.sigmoid(r[1][at, :] + r[3][...])
            dcv = acc * sg
            dcg = acc * a * sg * (1.0 - sg)
            outs[0][at, :] = jnp.concatenate([dcv, dcg], axis=1).astype(BF)
            return carry[0] + jnp.sum(dcv, axis=0, keepdims=True), carry[1] + jnp.sum(dcg, axis=0, keepdims=True)

        dbv, dbg = lax.fori_loop(0, tr // CONV_STRIP, dz0_strip, (zero_row, zero_row))

        for lb in range(D // LANES):
            lanes = slice(lb * LANES, (lb + 1) * LANES)

            def dw_strip(s, accs, lanes=lanes):
                r0 = s * DW_STRIP
                dz = dbuf[pl.ds(pl.multiple_of(r0, SUBLANES), DW_STRIP), lanes]
                out = []
                for k in range(KW):
                    prod = dz * _shifted(zbuf, zs, HALO - (KW - 1) + k, r0, DW_STRIP, lanes)
                    part = prod[0:SUBLANES]
                    for q in range(1, DW_STRIP // SUBLANES):
                        part = part + prod[q * SUBLANES:(q + 1) * SUBLANES]
                    out.append(accs[k] + part)
                return tuple(out)

            accs = lax.fori_loop(0, tr // DW_STRIP, dw_strip, tuple(jnp.zeros((SUBLANES, LANES), F32) for _ in range(KW)))
            for k in range(KW):
                dwacc[pl.ds(k * SUBLANES, SUBLANES), lanes] = accs[k]
        dw_rows = [jnp.sum(dwacc[pl.ds(k * SUBLANES, SUBLANES), :], axis=0, keepdims=True) for k in range(KW)]
        dw_rows.append(jnp.zeros((HALO - KW, D), F32))
        for o, v in zip(outs[1:], (dbv, dbg, jnp.concatenate(dw_rows, axis=0), dcb, dg, db)):
            _accumulate(o, v, i)

    wide = pl.BlockSpec((tr, 2 * D), lambda i: (i, 1))
    return _call(
        name, compute, (n_tiles,), [proj, proj, b_in, b_in, conv_w, ln_g, ln_b, z0, z0, z1, z1, dz3, dz3, dproj],
        [_rows(tr, D, 2), _rows(tr, D, 3), pl.BlockSpec((1, D), lambda i: (0, 2)), pl.BlockSpec((1, D), lambda i: (0, 3)),
         _whole((HALO, D)), _whole((1, D)), _whole((1, D)),
         _rows(tr), _halo_before(tr, 0), _rows(tr), _halo_after(tr, 0, n_tiles), _rows(tr), _halo_after(tr, 0, n_tiles),
         pl.BlockSpec(memory_space=pl.ANY)],
        [jax.ShapeDtypeStruct(dproj.shape, dproj.dtype), _vec(), _vec(), _vec(HALO), _vec(), _vec(), _vec()],
        [wide] + [_whole((1, D))] * 2 + [_whole((HALO, D))] + [_whole((1, D))] * 3,
        [pltpu.VMEM((tr + HALO, D), F32), pltpu.VMEM((tr + HALO, D), F32),
         pltpu.VMEM((SUBLANES - 1, tr + HALO, D), F32), pltpu.VMEM((SUBLANES - 1, tr + HALO, D), F32),
         pltpu.VMEM((HALO * SUBLANES, D), F32)],
        ("arbitrary",), rider, aliases={13: 0})


def _merge_fn(ga, gb, bga, bgb, ya, yb):
    return jax.nn.sigmoid(ga + bga) * ya + jax.nn.sigmoid(gb + bgb) * yb


def _mix_tail(name, ua, z3, proj, b_in, wa, wb, wo, x, g, next_norm):
    t = ua.shape[0]
    tr = min(t, 256)

    def compute(r, outs, _):
        ya = jnp.dot(r[0][...], r[6][...], preferred_element_type=F32)
        yb = jnp.dot(r[1][...], r[7][...], preferred_element_type=F32)
        merged = _merge_fn(r[2][...], r[3][...], r[4][...], r[5][...], ya, yb).astype(BF)
        y = jnp.dot(merged, r[8][...], preferred_element_type=F32)
        x_out = r[9][...] + r[10][...] * y
        for o, v in zip(outs, (ya, yb, merged, y, x_out, _rms_mod(x_out, r[11][...], r[12][...], r[13][...]))):
            o[...] = v.astype(o.dtype)

    row = _whole((1, D))
    return _call(
        name, compute, (t // tr,), [ua, z3, proj, proj, b_in, b_in, wa, wb, wo, x, g, *next_norm],
        [_rows(tr), _rows(tr), _rows(tr, D, 4), _rows(tr, D, 5), pl.BlockSpec((1, D), lambda i: (0, 4)),
         pl.BlockSpec((1, D), lambda i: (0, 5)), _whole((D, D)), _whole((D, D)), _whole((D, D)), _rows(tr), row, row, row, row],
        [jax.ShapeDtypeStruct((t, D), dt) for dt in (BF, BF, BF, BF, F32, BF)], [_rows(tr)] * 6, [], ("arbitrary",))


def _mix_tail_bwd(name, dy, proj, b_in, ya, yb, wa, wb, wo, rider=None):
    t = proj.shape[0]
    tr = min(t, 256)

    def compute(r, outs, _):
        i = pl.program_id(0)
        dm = lax.dot_general(r[0][...], r[9][...], NT, preferred_element_type=F32)
        _, vjp = jax.vjp(_merge_fn, *[x[...] for x in r[1:7]])
        dga, dgb, dbga, dbgb, dya, dyb = vjp(dm)
        dya, dyb = dya.astype(BF), dyb.astype(BF)
        outs[0][...] = jnp.concatenate([dga, dgb], axis=1).astype(BF)
        outs[1][...] = dya
        outs[2][...] = dyb
        outs[3][...] = lax.dot_general(dya, r[7][...], NT, preferred_element_type=F32)
        outs[4][...] = lax.dot_general(dyb, r[8][...], NT, preferred_element_type=F32)
        _accumulate(outs[5], dbga, i)
        _accumulate(outs[6], dbgb, i)

    return _call(
        name, compute, (t // tr,), [dy, proj, proj, b_in, b_in, ya, yb, wa, wb, wo],
        [_rows(tr), _rows(tr, D, 4), _rows(tr, D, 5), pl.BlockSpec((1, D), lambda i: (0, 4)), pl.BlockSpec((1, D), lambda i: (0, 5)),
         _rows(tr), _rows(tr), _whole((D, D)), _whole((D, D)), _whole((D, D))],
        [jax.ShapeDtypeStruct((t, D_IN), BF)] + [jax.ShapeDtypeStruct((t, D), BF)] * 2 + [jax.ShapeDtypeStruct((t, D), F32)] * 2
        + [_vec(), _vec()],
        [pl.BlockSpec((tr, 2 * D), lambda i: (i, 2))] + [_rows(tr)] * 4 + [_whole((1, D))] * 2, [], ("arbitrary",), rider)


def _loss_head(name, x, gain, target, f, g, scale):
    t = x.shape[0]
    tr = min(t, 512)

    def loss_fn(xv, gn, tgt):
        y = xv * lax.rsqrt(jnp.mean(xv * xv, axis=-1, keepdims=True) + EPS) * gn
        return 0.5 * jnp.sum(jnp.mean(jnp.square(y - tgt), axis=-1))

    def fn(i, r, _):
        loss, vjp = jax.vjp(loss_fn, r[0][...], r[1][...], r[2][...])
        dx, dgain, _ = vjp(jnp.ones((), F32))
        df, dg = _gate_grads(dx, r[3][...], r[4][...], scale)
        return [dx, df, dgain, jnp.zeros((1, D), F32) + loss, dg]

    return _rowcall(name, fn, [x, gain, target, f, g], [_rows(tr), _whole((1, D)), _rows(tr), _rows(tr), _whole((1, D))], 2,
                    [jax.ShapeDtypeStruct((t, D), F32), jax.ShapeDtypeStruct((t, D), BF), _vec(), _vec(), _vec()],
                    [_rows(tr)] * 2 + [_whole((1, D))] * 3, t // tr)


def _adamw(w, g, m, v):
    m = B1 * m + (1.0 - B1) * g
    v = B2 * v + (1.0 - B2) * jnp.square(g)
    m_hat = m / BC1
    v_hat = v / BC2
    delta = -LR * (m_hat / (jnp.sqrt(v_hat) + ADAM_EPS) + WD * w)
    return delta, m, v


ADAMW_ROWS = 64


def _adamw_group(name, items, rider=None, rows=ADAMW_ROWS):
    ins, in_specs, out_shapes, out_specs, plan = [], [], [], [], []
    first = 0
    for chip_sum, received, w, m, v in items:
        r, c = w.shape
        tr = min(r, rows)
        n = r // tr

        def tile(i, first=first, n=n):
            return jnp.clip(i - first, 0, n - 1)

        spec = pl.BlockSpec((tr, c), lambda i, tile=tile: (tile(i), 0))
        ins += [chip_sum, *received, w, m, v]
        in_specs += [pl.BlockSpec((None, tr, c), lambda i, tile=tile: (0, tile(i), 0))]
        in_specs += [pl.BlockSpec((g.shape[0], tr, c), lambda i, tile=tile: (0, tile(i), 0)) for g in received]
        in_specs += [spec] * 3
        out_shapes += [jax.ShapeDtypeStruct((r, c), F32)] * 4
        out_specs += [spec] * 4
        plan.append((first, n, [g.shape[0] for g in received]))
        first += n

    def compute(in_refs, out_refs, _):
        i = pl.program_id(0)
        at_in = at_out = 0
        for start, n, counts in plan:
            mine = in_refs[at_in:at_in + 4 + len(counts)]
            outs = out_refs[at_out:at_out + 4]
            at_in += 4 + len(counts)
            at_out += 4

            @pl.when(jnp.logical_and(i >= start, i < start + n))
            def _(mine=mine, outs=outs, counts=counts):
                g = mine[0][...].astype(F32)
                for j, count in enumerate(counts):
                    for s in range(count):
                        g = g + mine[1 + j][s].astype(F32)
                delta, m_new, v_new = _adamw(mine[-3][...], g, mine[-2][...], mine[-1][...])
                for o, val in zip(outs, (g, delta, m_new, v_new)):
                    o[...] = val

    res = _call(name, compute, (first,), ins, in_specs, out_shapes, out_specs, [], ("arbitrary",), rider)
    outs, rode = res if rider else (res, [])
    return [outs[4 * j:4 * j + 4] for j in range(len(items))], rode


def _adamw_small(name, packed_all, late_all, dws_all, vectors, w_s):
    n_vec = len(vectors)

    def body(*refs):
        p_ref, l_ref, d_ref = refs[:3]
        param_refs = refs[3:3 + 3 * n_vec + 3]
        out_refs = refs[3 + 3 * n_vec + 3:-1]
        g_ref = refs[-1]
        g = p_ref[0]
        late = l_ref[0]
        for s in range(1, NDEV):
            g = g + p_ref[s]
            late = late + l_ref[s]
        g_ref[...] = g
        g_ref[pl.ds(0, R_LATE), :] += late

        def update(gp, wmv, outs):
            delta, m_new, v_new = _adamw(wmv[0][...], gp, wmv[1][...], wmv[2][...])
            for o, val in zip(outs, (gp, delta, m_new, v_new)):
                o[...] = val

        for j, (row, rows, *_) in enumerate(vectors):
            pieces = [g_ref[pl.ds(row + r, 1), :] for r in range(rows)]
            update(pieces[0] if rows == 1 else jnp.concatenate(pieces, axis=1), param_refs[3 * j:3 * j + 3], out_refs[4 * j:4 * j + 4])
        gw = d_ref[0]
        for s in range(1, NDEV):
            gw = gw + d_ref[s]
        update(gw, param_refs[3 * n_vec:], out_refs[4 * n_vec:4 * n_vec + 4])
        out_refs[-2][...] = g_ref[pl.ds(R_CW, KW), :]
        out_refs[-1][...] = g_ref[pl.ds(R_LOSS, 1), :]

    params = [a for _, _, w, m, v in vectors for a in (w, m, v)] + list(w_s)
    out_shapes = [jax.ShapeDtypeStruct(w.shape, F32) for _, _, w, _, _ in vectors for _ in range(4)]
    out_shapes += [jax.ShapeDtypeStruct(w_s[0].shape, F32)] * 4 + [jax.ShapeDtypeStruct((KW, D), F32), _vec()]
    res = pl.pallas_call(body, name=name, out_shape=out_shapes, scratch_shapes=[pltpu.VMEM((R_TOTAL, D), F32)],
                         compiler_params=_params(None))(packed_all, late_all, dws_all, *params)
    return [res[4 * j:4 * j + 4] for j in range(n_vec + 1)], res[-2], res[-1]


def _adamw_plain(name, g, w, m, v):
    def body(g_ref, w_ref, m_ref, v_ref, d_ref, mo_ref, vo_ref):
        delta, m_new, v_new = _adamw(w_ref[...], g_ref[...], m_ref[...], v_ref[...])
        d_ref[...] = delta
        mo_ref[...] = m_new
        vo_ref[...] = v_new

    return pl.pallas_call(body, name=name, out_shape=[jax.ShapeDtypeStruct(w.shape, F32)] * 3,
                          compiler_params=_params(None))(g, w, m, v)


def _adamw_ada(name, c_all_t, dmod, dmod_late, w, m, v):
    r, c = w.shape
    tr = 256

    def fn(i, refs, _):
        ca = _silu(refs[0][...])
        dm = refs[1][...] + refs[2][...]
        g = ca[:, 0:1] * dm[0:1, :]
        for b in range(1, NDEV):
            g = g + ca[:, b:b + 1] * dm[b:b + 1, :]
        delta, m_new, v_new = _adamw(refs[3][...], g, refs[4][...], refs[5][...])
        return [g, delta, m_new, v_new]

    spec = pl.BlockSpec((tr, c), lambda i: (i, 0))
    whole = pl.BlockSpec((NDEV, c), lambda i: (0, 0))
    return _rowcall(name, fn, [c_all_t, dmod, dmod_late, w, m, v],
                    [pl.BlockSpec((tr, NDEV), lambda i: (i, 0)), whole, whole, spec, spec, spec], 4,
                    [jax.ShapeDtypeStruct((r, c), F32)] * 4, [spec] * 4, r // tr)


def _ffn_fwd(tag, x, h, g, wg, wu, wd_shard, down_rider, next_norm=None, more_shards=(), norm=None):
    t = x.shape[0]
    tm = min(t, 512 if down_rider else 1024)
    (gate, up, act, *normed), (wd, *more) = _ffn_up(f"{tag}_up", x if norm else h, wg, wu,
                                                    rider=_gather_rider([wd_shard, *more_shards]), norm=norm)
    h = normed[0] if norm else h
    row = pl.BlockSpec((1, D), lambda i, j, k: (0, 0))

    def epilogue(f, xv, gv, *norm):
        x_out = xv + 0.5 * gv * f
        return (x_out, f, _rms_mod(x_out, *norm)) if norm else (x_out, f)

    res = _mm_nn(f"{tag}_down", act, wd.reshape(F, D), tm, D, 1024, extras=(x, g, *(next_norm or ())),
                 extra_specs=(pl.BlockSpec((tm, D), lambda i, j, k: (i, 0)), row, *([row] * 3 if next_norm else [])),
                 epilogue=epilogue, out_dtypes=(F32, BF, BF) if next_norm else (F32, BF), rider=down_rider)
    (x_out, f, *h_next), rode = res if down_rider else (res, None)
    return x_out, (h_next[0] if next_norm else None), (x, h, gate, up, act, f), wd, rode, more


def _ffn_bwd(tag, dx_out, df, saved, gain, sh, sc, wg, wu, wd, slots, dact_rider=None, dwd_rider=None, dwgu_rider=None,
             below=None, fuse_dh=False):
    x, h, gate, up, act, f = saved
    t = x.shape[0]
    tm = min(t, 1024)
    if fuse_dh:
        dgate, dup, dh = _ffn_dact_dh(f"{tag}_dact_dh", df, wd.reshape(F, D), gate, up, wg, wu)
        dwd = _mm_tn(f"{tag}_dwd", act, df, 512, D).reshape(NDEV, F // NDEV, D)
        (dwg, dwu), (sib_d,) = _dw_gate_up(f"{tag}_dwgu", h, dgate, dup, rider=_pair_rider([dwd]))
        (sum_d,) = _pair_add(f"{tag}_dwd_add", [dwd], [sib_d], slots)
        normed, (sib_g, sib_u) = _norm_mod_bwd(f"{tag}_norm_bwd", x, gain, sc, sh, dh, dx_out, below=below,
                                               rider=_pair_rider([dwg, dwu]))
        sum_g, sum_u = _pair_add(f"{tag}_dwgu_add", [dwg, dwu], [sib_g, sib_u], slots)
        return normed, (sum_d, None), sum_g, sum_u, [], [], []

    blk = pl.BlockSpec((t, F // NDEV), lambda i, j, k: (i, j))
    res = _mm_nt(f"{tag}_dact", df, wd.reshape(F, D), t, F // NDEV, out_dtypes=(BF, BF),
                 extras=(gate, up), extra_specs=(blk, blk), epilogue=_swiglu_bwd, rider=dact_rider)
    (dgate, dup), rode_dact = res if dact_rider else (res, [])
    res = _mm_tn(f"{tag}_dwd", act, df, 512, D, rider=dwd_rider)
    dwd, rode_dwd = res if dwd_rider else (res, [])
    dwd = dwd.reshape(NDEV, F // NDEV, D)
    (dwg, dwu), (sib_d, *rode_dwgu) = _dw_gate_up(f"{tag}_dwgu", h, dgate, dup,
                                                  rider=[_pair_rider([dwd])] + ([dwgu_rider] if dwgu_rider else []))
    (sum_d,) = _pair_add(f"{tag}_dwd_add", [dwd], [sib_d], slots)
    dh, (sib_g, sib_u, got_d) = _mm_nt_blocked(f"{tag}_dh", [dgate, dup], [wg, wu], tm,
                                               rider=[_pair_rider([dwg, dwu]), _chip_rider([sum_d])])
    sum_g, sum_u = _pair_add(f"{tag}_dwgu_add", [dwg, dwu], [sib_g, sib_u], slots)
    normed = _norm_mod_bwd(f"{tag}_norm_bwd", x, gain, sc, sh, dh, dx_out, below=below)
    return normed, (sum_d, [got_d]), sum_g, sum_u, rode_dact, rode_dwd, rode_dwgu


def kernel(x, c, ada_w, ada_b, norm_ffn1, ffn1_w_gate, ffn1_w_up, ffn1_w_down, norm_mix, mix_w_in, mix_b_in, sgu_ln_g, sgu_ln_b, sgu_w_s, sgu_b_s, conv_w, conv_b, conv_ln_g, conv_ln_b, w_branch_a, w_branch_b, w_out, norm_ffn2, ffn2_w_gate, ffn2_w_up, ffn2_w_down, norm_final, loss_target, m_ada_w, m_ada_b, m_norm_ffn1, m_ffn1_w_gate, m_ffn1_w_up, m_ffn1_w_down, m_norm_mix, m_mix_w_in, m_mix_b_in, m_sgu_ln_g, m_sgu_ln_b, m_sgu_w_s, m_sgu_b_s, m_conv_w, m_conv_b, m_conv_ln_g, m_conv_ln_b, m_w_branch_a, m_w_branch_b, m_w_out, m_norm_ffn2, m_ffn2_w_gate, m_ffn2_w_up, m_ffn2_w_down, m_norm_final, v_ada_w, v_ada_b, v_norm_ffn1, v_ffn1_w_gate, v_ffn1_w_up, v_ffn1_w_down, v_norm_mix, v_mix_w_in, v_mix_b_in, v_sgu_ln_g, v_sgu_ln_b, v_sgu_w_s, v_sgu_b_s, v_conv_w, v_conv_b, v_conv_ln_g, v_conv_ln_b, v_w_branch_a, v_w_branch_b, v_w_out, v_norm_ffn2, v_ffn2_w_gate, v_ffn2_w_up, v_ffn2_w_down, v_norm_final):
    mx, my, mc = _position()
    me = 4 * mx + 2 * my + mc
    chip = 2 * mx + my
    slots = jnp.stack([2 * (chip ^ k) + mc for k in range(N_CHIPS)]).astype(jnp.int32)
    t = x.shape[1]
    tm = min(t, 1024)
    x0 = x.reshape(t, D)
    target = loss_target.reshape(t, D)
    given = dict(ffn1_w_gate=(ffn1_w_gate, m_ffn1_w_gate, v_ffn1_w_gate), ffn1_w_up=(ffn1_w_up, m_ffn1_w_up, v_ffn1_w_up),
                 ffn1_w_down=(ffn1_w_down, m_ffn1_w_down, v_ffn1_w_down), mix_w_in=(mix_w_in, m_mix_w_in, v_mix_w_in),
                 w_branch_a=(w_branch_a, m_w_branch_a, v_w_branch_a), w_branch_b=(w_branch_b, m_w_branch_b, v_w_branch_b),
                 w_out=(w_out, m_w_out, v_w_out), ffn2_w_gate=(ffn2_w_gate, m_ffn2_w_gate, v_ffn2_w_gate),
                 ffn2_w_up=(ffn2_w_up, m_ffn2_w_up, v_ffn2_w_up), ffn2_w_down=(ffn2_w_down, m_ffn2_w_down, v_ffn2_w_down))
    shard = {n: wmv[0][0].astype(BF) for n, wmv in given.items()}

    ada_cols = N_MOD * D // NDEV
    c_all, taps_all, mod_all, (wg1, wu1) = _prologue(
        "prologue", jnp.pad(c, ((0, SUBLANES - 1), (0, 0))), jnp.pad(conv_w[0], ((0, HALO - KW), (0, 0))), ada_w[0],
        lax.dynamic_slice(ada_b, (0, me * ada_cols), (1, ada_cols)), [shard["ffn1_w_gate"], shard["ffn1_w_up"]])
    conv_w_full = jnp.transpose(taps_all.reshape(NDEV, HALO, CHUNK), (1, 0, 2)).reshape(HALO, D)
    mod = lax.dynamic_index_in_dim(mod_all.reshape(NDEV, NDEV, ada_cols), me, axis=1, keepdims=False).reshape(N_MOD, 1, D)
    sh1, sc1, g1, sh2, sc2, g2, sh3, sc3, g3 = [mod[i] for i in range(N_MOD)]

    x1, h2, saved1, wd1, (w_in,), (wa3, wb3) = _ffn_fwd(
        "ffn1", x0, None, g1, wg1, wu1, shard["ffn1_w_down"], _gather_rider([shard["mix_w_in"]]),
        next_norm=(norm_mix, sc2, sh2), more_shards=(shard["w_branch_a"], shard["w_branch_b"]), norm=(norm_ffn1, sc1, sh1))
    proj, (wg2, wo3) = _mm_nn_blocked("mix_in", h2, w_in, tm, rider=_gather_rider([shard["ffn2_w_gate"], shard["w_out"]]))
    bias_full = jnp.repeat(sgu_b_s[0].T, CHUNK, axis=1)
    (ua,) = _sgu("sgu", proj, mix_b_in, sgu_ln_g, sgu_ln_b, sgu_w_s[0], bias_full)
    (z0, z1, z3), (wu2,) = _conv("conv", proj, mix_b_in, conv_w_full, conv_b, conv_ln_g, conv_ln_b,
                                 rider=_gather_rider([shard["ffn2_w_up"]]))
    wa, wb, wo = wa3.reshape(D, D), wb3.reshape(D, D), wo3.reshape(D, D)
    ya, yb, merged, y, x2, h3 = _mix_tail("mix_tail", ua, z3, proj, mix_b_in, wa, wb, wo, x1, g2, (norm_ffn2, sc3, sh3))
    x3, _, saved3, wd2, _, _ = _ffn_fwd("ffn2", x2, h3, g3, wg2, wu2, shard["ffn2_w_down"], None)

    norm_final2 = norm_final.reshape(1, D)
    dx3, df3, d_norm_final, loss_row, dg3 = _loss_head("loss_head", x3, norm_final2, target, saved3[-1], g3, 0.5)
    (dx2, dy, d_norm_ffn2, dsc3, dsh3, dg2), (sum_d2, _), sum_g2, sum_u2, _, _, _ = _ffn_bwd(
        "ffn2", dx3, df3, saved3, norm_ffn2, sh3, sc3, wg2, wu2, wd2, slots, below=(y, g2, 1.0), fuse_dh=True)
    (dproj, dya, dyb, dua, dz3, db_ga, db_gb), (got_g2_near,) = _mix_tail_bwd(
        "mix_tail_bwd", dy, proj, mix_b_in, ya, yb, wa, wb, wo, rider=_chip_rider([sum_g2], NEIGHBOURS))
    dwo, dwa, dwb = [g.reshape(NDEV, D // NDEV, D) for g in _dw_square("mix_dw", [(merged, dy), (ua, dya), (z3, dyb)])]
    (dproj, db_u, db_v, d_sgu_g, d_sgu_b, d_ws, d_bs_t), (*sib_abo, got_g2_far) = _sgu_bwd(
        "sgu_bwd", proj, mix_b_in, sgu_ln_g, sgu_ln_b, sgu_w_s[0], bias_full, dua, dproj,
        rider=[_pair_rider([dwa, dwb, dwo]), _chip_rider([sum_g2], DIAGONAL)])
    sum_a, sum_b, sum_o = _pair_add("mix_dw_add", [dwa, dwb, dwo], sib_abo, slots)
    (dproj, db_cv, db_cg, d_cw, d_cb, d_cln_g, d_cln_b), (got_u2, got_d2) = _conv_bwd(
        "conv_bwd", proj, mix_b_in, conv_w_full, conv_ln_g, conv_ln_b, z0, z1, dz3, dproj, rider=_chip_rider([sum_u2, sum_d2]))
    dwin, (got_a, got_b, got_o) = _mm_tn_blocked("mix_dwin", h2, dproj, rider=_chip_rider([sum_a, sum_b, sum_o]))

    d_bs = jnp.transpose(d_bs_t[:, :HEADS])
    zero = jnp.zeros((1, D), F32)
    pack_rows = [zero, zero, zero, zero, zero, dg2, dsh3, dsc3, dg3,
                 zero, zero, d_norm_ffn2, d_norm_final,
                 db_u, db_v, db_cv, db_cg, db_ga, db_gb,
                 d_sgu_g, d_sgu_b, d_bs.reshape(1, D), d_cb, d_cln_g, d_cln_b,
                 d_cw[:KW], loss_row, jnp.zeros((R_TOTAL - R_LOSS - 1, D), F32)]
    packed = jnp.concatenate(pack_rows, axis=0)
    d_ws2 = d_ws.reshape(HEADS * CHUNK, CHUNK)
    dh2, (sib_in, packed_all, dws_all) = _mm_nt_blocked("mix_in_bwd", [dproj], [w_in], tm,
                                                        rider=[_pair_rider([dwin]), _gather_rider([packed, d_ws2])])
    (sum_in,) = _pair_add("mix_dwin_add", [dwin], [sib_in], slots)
    dx1, df1, d_norm_mix, dsc2, dsh2, dg1 = _norm_mod_bwd("mix_norm_bwd", x1, norm_mix, sc2, sh2, dh2, dx2,
                                                          below=(saved1[-1], g1, 0.5))
    (dx0, d_norm_ffn1, dsc1, dsh1), down1, sum_g1, sum_u1, (got_in_near,), _, (got_in_far,) = _ffn_bwd(
        "ffn1", dx1, df1, saved1, norm_ffn1, sh1, sc1, wg1, wu1, wd1, slots,
        dact_rider=_chip_rider([sum_in], NEIGHBOURS), dwgu_rider=_chip_rider([sum_in], DIAGONAL))
    packed_late = jnp.concatenate([dsh1, dsc1, dg1, dsh2, dsc2, jnp.zeros((4, D), F32), d_norm_ffn1, d_norm_mix,
                                   jnp.zeros((R_LATE - 11, D), F32)], axis=0)
    grads = dict(ffn2_w_gate=(sum_g2, [got_g2_near, got_g2_far]), ffn2_w_up=(sum_u2, [got_u2]), ffn2_w_down=(sum_d2, [got_d2]),
                 mix_w_in=(sum_in, [got_in_near, got_in_far]), w_branch_a=(sum_a, [got_a]), w_branch_b=(sum_b, [got_b]),
                 w_out=(sum_o, [got_o]), ffn1_w_down=down1)
    done, (late_all, got_g1, got_u1) = _adamw_group(
        "adamw_most", [(cs, got, *[a[0] for a in given[n]]) for n, (cs, got) in grads.items()],
        rider=[_gather_rider([packed_late]), _chip_rider([sum_g1, sum_u1])])
    last, _ = _adamw_group("adamw_ffn1_in", [(sum_g1, [got_g1], *[a[0] for a in given["ffn1_w_gate"]]),
                                            (sum_u1, [got_u1], *[a[0] for a in given["ffn1_w_up"]])], rows=256)
    big_out = {n: [o.reshape(given[n][0].shape) for o in outs]
               for n, outs in zip([*grads, "ffn1_w_gate", "ffn1_w_up"], [*done, *last])}

    flat = lambda a: a.reshape(1, -1)
    vectors = [("ada_b", 0, 9, ada_b, m_ada_b, v_ada_b), ("norm_ffn1", 9, 1, norm_ffn1, m_norm_ffn1, v_norm_ffn1),
               ("norm_mix", 10, 1, norm_mix, m_norm_mix, v_norm_mix), ("norm_ffn2", 11, 1, norm_ffn2, m_norm_ffn2, v_norm_ffn2),
               ("norm_final", 12, 1, norm_final, m_norm_final, v_norm_final), ("mix_b_in", 13, 6, mix_b_in, m_mix_b_in, v_mix_b_in),
               ("sgu_ln_g", 19, 1, sgu_ln_g, m_sgu_ln_g, v_sgu_ln_g), ("sgu_ln_b", 20, 1, sgu_ln_b, m_sgu_ln_b, v_sgu_ln_b),
               ("sgu_b_s", 21, 1, sgu_b_s, m_sgu_b_s, v_sgu_b_s), ("conv_b", 22, 1, conv_b, m_conv_b, v_conv_b),
               ("conv_ln_g", 23, 1, conv_ln_g, m_conv_ln_g, v_conv_ln_g), ("conv_ln_b", 24, 1, conv_ln_b, m_conv_ln_b, v_conv_ln_b)]
    small_out, d_cw_all, loss_sum = _adamw_small(
        "adamw_small", packed_all, late_all, dws_all, [(row, rows, flat(wv), flat(mv), flat(vv)) for _, row, rows, wv, mv, vv in vectors],
        [a.reshape(HEADS * CHUNK, CHUNK) for a in (sgu_w_s, m_sgu_w_s, v_sgu_w_s)])
    small = {n: [o.reshape(wv.shape) for o in outs] for (n, _, _, wv, _, _), outs in zip(vectors, small_out)}
    small["sgu_w_s"] = [o.reshape(sgu_w_s.shape) for o in small_out[-1]]
    g_cw = lax.dynamic_slice(d_cw_all, (0, me * CHUNK), (KW, CHUNK))
    small["conv_w"] = [o.reshape(conv_w.shape) for o in (g_cw, *_adamw_plain("adamw_conv_w", g_cw, conv_w[0], m_conv_w[0], v_conv_w[0]))]
    loss = loss_sum[0, 0]

    dmod_cols = [lax.dynamic_slice(a[:, :N_MOD, :].reshape(NDEV, N_MOD * D), (0, me * ada_cols), (NDEV, ada_cols))
                 for a in (packed_all, late_all)]
    ada_out = [o.reshape(ada_w.shape) for o in _adamw_ada("adamw_ada_w", jnp.transpose(c_all), *dmod_cols, ada_w[0], m_ada_w[0], v_ada_w[0])]

    order = ["ada_w", "ada_b", "norm_ffn1", "ffn1_w_gate", "ffn1_w_up", "ffn1_w_down", "norm_mix", "mix_w_in", "mix_b_in",
             "sgu_ln_g", "sgu_ln_b", "sgu_w_s", "sgu_b_s", "conv_w", "conv_b", "conv_ln_g", "conv_ln_b", "w_branch_a",
             "w_branch_b", "w_out", "norm_ffn2", "ffn2_w_gate", "ffn2_w_up", "ffn2_w_down", "norm_final"]

    def leaf(n, kind):
        if n == "ada_w":
            return ada_out[kind]
        if n in big_out:
            return big_out[n][kind]
        return small[n][kind]

    return (loss, dx0.reshape(x.shape), *[leaf(n, kind) for kind in range(4) for n in order])
```
